```python
import math
import jax, jax.numpy as jnp
from jax import lax
import numpy as np

D_MODEL = 1024
BATCH = 2
SEQ = 8192
DEPTH = 2

GRID_W = 64
CTX_LEN = 256
HEAD_DIM = 64
SC_W = D_MODEL // 2
SC_K = 3
DN_W = D_MODEL // 2
DN_HEAD_DIM = 128
DN_HEADS = DN_W // DN_HEAD_DIM
DN_CONV_K = 3
DN_CHUNK = 64
SWA_HEADS = (D_MODEL // 2) // HEAD_DIM
SWA_KV_HEADS = SWA_HEADS // 4
SWA_WINDOW = 128
SWA_BLOCK = 128
NA_HEADS = (D_MODEL // 2) // HEAD_DIM
NA_KH = 8
NA_KW = 16
ROPE_THETA = 10000.0
N_EXPERTS = 16
N_GROUPS = 4
TOP_K = 2
D_EXPERT = D_MODEL
N_MOD = 6
EPS = 1e-6
EV_SIZES = (SC_W, SC_W, SC_W, DN_W, DN_W, DN_W, DN_W, DN_HEADS, DN_HEADS, DN_HEADS, DN_HEADS)
OD_SIZES = (SWA_HEADS * HEAD_DIM, SWA_KV_HEADS * HEAD_DIM, SWA_KV_HEADS * HEAD_DIM,
            NA_HEADS * HEAD_DIM, NA_HEADS * HEAD_DIM, NA_HEADS * HEAD_DIM)

kernel_name = "hybrid_conv_deltanet_swa_natten_moe_dit"


def rms_norm(x, g):
    xf = x.astype(jnp.float32)
    y = xf * lax.rsqrt(jnp.mean(xf * xf, axis=-1, keepdims=True) + EPS)
    return (y * g.astype(jnp.float32)).astype(x.dtype)


def l2_normalize(x):
    xf = x.astype(jnp.float32)
    return (xf * lax.rsqrt(jnp.sum(xf * xf, axis=-1, keepdims=True) + EPS)).astype(x.dtype)


def modulate(h, shift, scale):
    return h * (1 + scale) + shift


def split_cols(p, sizes):
    return jnp.split(p, np.cumsum(sizes)[:-1].tolist(), axis=-1)


def rev(*arrs):
    return [jnp.flip(a, axis=1) for a in arrs]


def dwconv_centered(x, w):
    k, ch = w.shape
    return lax.conv_general_dilated(x, w[:, None, :].astype(x.dtype), window_strides=(1,),
                                    padding=[(k // 2, k // 2)], dimension_numbers=("NWC", "WIO", "NWC"),
                                    feature_group_count=ch)


def axial_rope(x):
    s, d = x.shape[1], x.shape[-1]
    half = d // 2
    nf = half // 2
    t = jnp.arange(s)
    inv = ROPE_THETA ** (-jnp.arange(nf, dtype=jnp.float32) / nf)
    xf = x.astype(jnp.float32)

    def rot(xa, pos):
        ang = pos.astype(jnp.float32)[:, None] * inv
        cos = jnp.cos(ang)[None, :, None, :]
        sin = jnp.sin(ang)[None, :, None, :]
        x1, x2 = xa[..., :nf], xa[..., nf:]
        return jnp.concatenate([x1 * cos - x2 * sin, x1 * sin + x2 * cos], axis=-1)

    out = jnp.concatenate([rot(xf[..., :half], t // GRID_W), rot(xf[..., half:], t % GRID_W)], axis=-1)
    return out.astype(x.dtype)


def gated_delta_rule(q, k, v, beta, g, state):
    bsz, t_len, h, dk = q.shape
    dv = v.shape[-1]
    c = DN_CHUNK
    n = t_len // c
    f32 = jnp.float32

    def chunks(a):
        a = a.astype(f32).reshape(bsz, n, c, h, *a.shape[3:])
        return jnp.moveaxis(a, (1, 3), (0, 2))

    qc, kc, vc, bc, gc = [chunks(a) for a in (q, k, v, beta, g)]
    gcum = jnp.cumsum(gc, axis=-1)
    incl = np.tril(np.ones((c, c), dtype=bool))
    strict = np.tril(np.ones((c, c), dtype=bool), -1)
    decay = jnp.exp(jnp.where(incl, gcum[..., :, None] - gcum[..., None, :], -jnp.inf))
    kkt = jnp.einsum("nbhid,nbhjd->nbhij", kc, kc)
    a_mat = jnp.where(strict, bc[..., :, None] * kkt * decay, 0.0)
    eye = jnp.eye(c, dtype=f32)
    t_inv = lax.linalg.triangular_solve(eye + a_mat, jnp.broadcast_to(eye, a_mat.shape),
                                        left_side=True, lower=True, unit_diagonal=True)
    u = t_inv @ (bc[..., None] * vc)
    w = t_inv @ (bc[..., None] * jnp.exp(gcum)[..., None] * kc)
    qdec = qc * jnp.exp(gcum)[..., None]
    attn = jnp.einsum("nbhid,nbhjd->nbhij", qc, kc) * decay
    g_last = gcum[..., -1]
    kdec = kc * jnp.exp(g_last[..., None] - gcum)[..., None]

    def step(s, xs):
        u_i, w_i, qd_i, at_i, kd_i, gl_i = xs
        v_new = u_i - w_i @ s
        o = qd_i @ s + at_i @ v_new
        s = s * jnp.exp(gl_i)[..., None, None] + jnp.swapaxes(kd_i, -1, -2) @ v_new
        return s, o

    s_fin, o = lax.scan(step, state, (u, w, qdec, attn, kdec, g_last))
    o = jnp.moveaxis(o, (0, 2), (1, 3)).reshape(bsz, t_len, h, dv)
    return o.astype(q.dtype), s_fin


def gated_rms_norm(o, z, g):
    y = rms_norm(o, g) * jax.nn.silu(z.reshape(o.shape))
    return y.reshape(*o.shape[:2], -1)


def dense_attention(q, k, v, sink=None):
    bsz, t_len, hq, d = q.shape
    hkv = k.shape[2]
    qg = q.reshape(bsz, t_len, hkv, hq // hkv, d) * d ** -0.5
    s = jnp.einsum("btkgd,bskd->bkgts", qg, k).astype(jnp.float32)
    if sink is not None:
        sk = jnp.broadcast_to(sink.astype(jnp.float32).reshape(1, hkv, hq // hkv, 1, 1), s.shape[:-1] + (1,))
        s = jnp.concatenate([s, sk], axis=-1)
    p = jax.nn.softmax(s, axis=-1)[..., :t_len].astype(v.dtype)
    return jnp.einsum("bkgts,bskd->btkgd", p, v).reshape(bsz, t_len, hq, d)


def window_attention(q, k, v, kc, vc, sink):
    bsz, s_len, hq, d = q.shape
    hkv = k.shape[2]
    grp = hq // hkv
    wb = SWA_BLOCK
    nb = s_len // wb
    qb = q.reshape(bsz, nb, wb, hkv, grp, d) * d ** -0.5
    pad = ((0, 0), (wb, wb), (0, 0), (0, 0))
    kp = jnp.pad(k, pad).reshape(bsz, nb + 2, wb, hkv, d)
    vp = jnp.pad(v, pad).reshape(bsz, nb + 2, wb, hkv, d)
    kw = jnp.concatenate([kp[:, :-2], kp[:, 1:-1], kp[:, 2:]], axis=2)
    vw = jnp.concatenate([vp[:, :-2], vp[:, 1:-1], vp[:, 2:]], axis=2)
    rel = np.arange(3 * wb)[None, :] - wb - np.arange(wb)[:, None]
    band = np.abs(rel) <= SWA_WINDOW
    key_pos = np.arange(nb)[:, None] * wb - wb + np.arange(3 * wb)[None, :]
    mask = band[None] & ((key_pos >= 0) & (key_pos < s_len))[:, None, :]
    s_loc = jnp.einsum("bnqkgd,bnjkd->bnkgqj", qb, kw).astype(jnp.float32)
    s_loc = jnp.where(mask[None, :, None, None], s_loc, -jnp.inf)
    s_ctx = jnp.einsum("bnqkgd,bckd->bnkgqc", qb, kc).astype(jnp.float32)
    s_sink = jnp.broadcast_to(sink.astype(jnp.float32).reshape(1, 1, hkv, grp, 1, 1), s_loc.shape[:-1] + (1,))
    p = jax.nn.softmax(jnp.concatenate([s_loc, s_ctx, s_sink], axis=-1), axis=-1)
    n_loc = 3 * wb
    n_ctx = kc.shape[1]
    p_loc = p[..., :n_loc].astype(v.dtype)
    p_ctx = p[..., n_loc:n_loc + n_ctx].astype(v.dtype)
    o = jnp.einsum("bnkgqj,bnjkd->bnqkgd", p_loc, vw) + jnp.einsum("bnkgqc,bckd->bnqkgd", p_ctx, vc)
    return o.reshape(bsz, s_len, hq, d)


def neighbourhood_attention(q, k, v, kc, vc, rpb):
    bsz, s_len, h, d = q.shape
    rows = s_len // GRID_W
    kh = min(NA_KH, rows)
    n_cb = GRID_W // NA_KW
    band = 2 * NA_KW
    qcol = np.arange(GRID_W).reshape(n_cb, NA_KW)
    qc_start = np.clip(qcol - NA_KW // 2, 0, GRID_W - NA_KW)
    cb_start = np.clip(np.arange(n_cb) * NA_KW - NA_KW // 2, 0, GRID_W - band)
    kcol = cb_start[:, None] + np.arange(band)[None, :]
    col_ok = (kcol[:, None, :] >= qc_start[:, :, None]) & (kcol[:, None, :] < qc_start[:, :, None] + NA_KW)
    dc_idx = np.clip(kcol[:, None, :] - qcol[:, :, None] + NA_KW - 1, 0, 2 * NA_KW - 2)
    rpb_c = rpb[:, :, dc_idx].astype(jnp.float32)
    qg = q.reshape(bsz, rows, n_cb, NA_KW, h, d) * d ** -0.5
    kg = k.reshape(bsz, rows, GRID_W, h, d)
    vg = v.reshape(bsz, rows, GRID_W, h, d)

    def one_row(args):
        r, q_r = args
        rs = jnp.clip(r - kh // 2, 0, rows - kh)
        k_nb = lax.dynamic_slice_in_dim(kg, rs, kh, axis=1)[:, :, kcol]
        v_nb = lax.dynamic_slice_in_dim(vg, rs, kh, axis=1)[:, :, kcol]
        bias = rpb_c[:, rs + jnp.arange(kh) - r + NA_KH - 1]
        s_loc = jnp.einsum("bmqhd,bimjhd->bhmqij", q_r, k_nb).astype(jnp.float32)
        s_loc = s_loc + jnp.transpose(bias, (0, 2, 3, 1, 4))[None]
        s_loc = jnp.where(col_ok[:, :, None, :], s_loc, -jnp.inf).reshape(bsz, h, n_cb, NA_KW, kh * band)
        s_ctx = jnp.einsum("bmqhd,bchd->bhmqc", q_r, kc).astype(jnp.float32)
        p = jax.nn.softmax(jnp.concatenate([s_loc, s_ctx], axis=-1), axis=-1)
        p_loc = p[..., :kh * band].reshape(bsz, h, n_cb, NA_KW, kh, band).astype(v.dtype)
        p_ctx = p[..., kh * band:].astype(v.dtype)
        return (jnp.einsum("bhmqij,bimjhd->bmqhd", p_loc, v_nb)
                + jnp.einsum("bhmqc,bchd->bmqhd", p_ctx, vc))

    out = lax.map(one_row, (jnp.arange(rows), jnp.moveaxis(qg, 1, 0)))
    return jnp.moveaxis(out, 0, 1).reshape(bsz, s_len, h, d)


def even_mixer(hc, hl, w_in, w_out, sc_w, dn_w, a_log, dt_bias, onorm_g, need_ctx):
    f32 = jnp.float32

    def project(h):
        sb, sg, sx, q, k, v, z, bf, bb, af, ab = split_cols(h @ w_in, EV_SIZES)
        qkv = jax.nn.silu(dwconv_centered(jnp.concatenate([q, k, v], axis=-1), dn_w))
        q, k, v = [a.reshape(*a.shape[:2], DN_HEADS, DN_HEAD_DIM) for a in jnp.split(qkv, 3, axis=-1)]
        q = l2_normalize(q) * DN_HEAD_DIM ** -0.5
        k = l2_normalize(k)
        dirs = []
        for di, (b_raw, a_raw) in enumerate(((bf, af), (bb, ab))):
            beta = jax.nn.sigmoid(b_raw.astype(f32))
            g = -jnp.exp(a_log[di].astype(f32)) * jax.nn.softplus(a_raw.astype(f32) + dt_bias[di].astype(f32))
            dirs.append((beta, g))
        return (sb, sg, sx), (q, k, v), z, dirs

    def finish(sc_parts, o, z):
        sb, sg, sx = sc_parts
        y_a = sb * dwconv_centered(sg * sx, sc_w)
        y_b = gated_rms_norm(o, z, onorm_g)
        return jnp.concatenate([y_a, y_b], axis=-1) @ w_out

    sc_c, qkv_c, z_c, (fw_c, bw_c) = project(hc)
    sc_l, qkv_l, z_l, (fw_l, bw_l) = project(hl)
    s0 = jnp.zeros((hl.shape[0], DN_HEADS, DN_HEAD_DIM, DN_HEAD_DIM), f32)
    o_cf, s_cf = gated_delta_rule(*qkv_c, *fw_c, s0)
    o_cb, s_cb = gated_delta_rule(*rev(*qkv_c), *rev(*bw_c), s0)
    o_lf, _ = gated_delta_rule(*qkv_l, *fw_l, s_cf)
    o_lb, _ = gated_delta_rule(*rev(*qkv_l), *rev(*bw_l), s_cb)
    yl = finish(sc_l, o_lf + jnp.flip(o_lb, axis=1), z_l)
    yc = finish(sc_c, o_cf + jnp.flip(o_cb, axis=1), z_c) if need_ctx else None
    return yc, yl


def odd_mixer(hc, hl, w_in, w_out, sink, rpb, need_ctx):
    def heads(a, n):
        return a.reshape(*a.shape[:2], n, HEAD_DIM)

    def project(h):
        cq, ck, cv, dq, dk, dv = split_cols(h @ w_in, OD_SIZES)
        return (heads(cq, SWA_HEADS), heads(ck, SWA_KV_HEADS), heads(cv, SWA_KV_HEADS),
                heads(dq, NA_HEADS), heads(dk, NA_HEADS), heads(dv, NA_HEADS))

    cqc, ckc, cvc, dqc, dkc, dvc = project(hc)
    cql, ckl, cvl, dql, dkl, dvl = project(hl)
    bsz, s_len = hl.shape[:2]
    o_c = window_attention(axial_rope(cql), axial_rope(ckl), cvl, ckc, cvc, sink)
    o_d = neighbourhood_attention(dql, dkl, dvl, dkc, dvc, rpb)
    yl = jnp.concatenate([o_c.reshape(bsz, s_len, -1), o_d.reshape(bsz, s_len, -1)], axis=-1) @ w_out
    yc = None
    if need_ctx:
        t_c = hc.shape[1]
        oc_c = dense_attention(cqc, ckc, cvc, sink)
        oc_d = dense_attention(dqc, dkc, dvc)
        yc = jnp.concatenate([oc_c.reshape(bsz, t_c, -1), oc_d.reshape(bsz, t_c, -1)], axis=-1) @ w_out
    return yc, yl


def moe(h, router_w, router_b, w_gate, w_up, w_down):
    t_len = h.shape[0]
    epg = N_EXPERTS // N_GROUPS
    scores = jax.nn.sigmoid((h @ router_w).astype(jnp.float32))
    gsel = (scores + router_b.astype(jnp.float32)).reshape(t_len, N_GROUPS, epg)
    group_score = lax.top_k(gsel, 2)[0].sum(-1)
    gidx = jnp.argmax(group_score, axis=-1)
    in_group = jnp.take_along_axis(gsel, gidx[:, None, None], axis=1)[:, 0]
    _, loc = lax.top_k(in_group, TOP_K)
    eidx = gidx[:, None] * epg + loc
    wsel = jnp.take_along_axis(scores, eidx, axis=1)
    wsel = wsel / jnp.sum(wsel, axis=-1, keepdims=True)
    gates = jnp.sum(jax.nn.one_hot(eidx, N_EXPERTS, dtype=jnp.float32) * wsel[..., None], axis=1).astype(h.dtype)
    y = jnp.zeros_like(h)
    for e in range(N_EXPERTS):
        a = jax.nn.silu(h @ w_gate[e]) * (h @ w_up[e])
        y = y + gates[:, e:e + 1] * (a @ w_down[e])
    return y


def setup_inputs(seed: int = 0) -> dict:
    key = jax.random.key(seed)
    ks = iter(jax.random.split(key, 32))
    f32 = jnp.float32
    n_even, n_odd = (DEPTH + 1) // 2, DEPTH // 2
    d = D_MODEL

    def normal(shape, scale):
        return jax.random.normal(next(ks), shape, f32) * scale

    def gain(shape):
        return 1.0 + normal(shape, 0.05)

    dt = jnp.exp(jax.random.uniform(next(ks), (n_even, 2, DN_HEADS), f32, math.log(1e-3), math.log(1e-1)))
    a_log = jnp.log(jax.random.uniform(next(ks), (n_even, 2, DN_HEADS), f32, 1.0, 16.0))
    mix_w = SC_W + DN_W
    att_w = (SWA_HEADS + NA_HEADS) * HEAD_DIM
    return {
        "x": normal((BATCH, SEQ, d), 1.0),
        "c": normal((BATCH, d), 1.0),
        "ctx": normal((BATCH, CTX_LEN, d), 1.0),
        "c_ctx": normal((d,), 1.0),
        "ada_w": normal((DEPTH, d, N_MOD * d), 0.5 * d ** -0.5),
        "ada_b": normal((DEPTH, N_MOD * d), 0.02),
        "norm1_g": gain((DEPTH, d)),
        "norm2_g": gain((DEPTH, d)),
        "ev_w_in": normal((n_even, d, sum(EV_SIZES)), d ** -0.5),
        "ev_w_out": normal((n_even, mix_w, d), mix_w ** -0.5),
        "sc_conv_w": normal((n_even, SC_K, SC_W), SC_K ** -0.5),
        "dn_conv_w": normal((n_even, DN_CONV_K, 3 * DN_W), DN_CONV_K ** -0.5),
        "dn_a_log": a_log,
        "dn_dt_bias": dt + jnp.log(-jnp.expm1(-dt)),
        "dn_onorm_g": gain((n_even, DN_HEAD_DIM)),
        "od_w_in": normal((n_odd, d, sum(OD_SIZES)), d ** -0.5),
        "od_w_out": normal((n_odd, att_w, d), att_w ** -0.5),
        "swa_sink": normal((n_odd, SWA_HEADS), 0.5),
        "na_rpb": normal((n_odd, NA_HEADS, 2 * NA_KH - 1, 2 * NA_KW - 1), 0.1),
        "router_w": normal((d, N_EXPERTS), d ** -0.5),
        "router_b": normal((N_EXPERTS,), 0.01),
        "moe_w_gate": normal((DEPTH, N_EXPERTS, d, D_EXPERT), d ** -0.5),
        "moe_w_up": normal((DEPTH, N_EXPERTS, d, D_EXPERT), d ** -0.5),
        "moe_w_down": normal((DEPTH, N_EXPERTS, D_EXPERT, d), D_EXPERT ** -0.5),
        "final_g": gain((d,)),
    }


def reference(x, c, ctx, c_ctx, ada_w, ada_b, norm1_g, norm2_g,
              ev_w_in, ev_w_out, sc_conv_w, dn_conv_w, dn_a_log, dn_dt_bias, dn_onorm_g,
              od_w_in, od_w_out, swa_sink, na_rpb,
              router_w, router_b, moe_w_gate, moe_w_up, moe_w_down, final_g):
    xl, xc = x, ctx
    d = x.shape[-1]
    for l in range(DEPTH):
        need_ctx = l < DEPTH - 1
        ml = jnp.split((jax.nn.silu(c) @ ada_w[l] + ada_b[l])[:, None, :], N_MOD, axis=-1)
        mc = jnp.split(jax.nn.silu(c_ctx) @ ada_w[l] + ada_b[l], N_MOD, axis=-1)
        hl = modulate(rms_norm(xl, norm1_g[l]), ml[0], ml[1])
        hc = modulate(rms_norm(xc, norm1_g[l]), mc[0], mc[1])
        i = l // 2
        if l % 2 == 0:
            yc, yl = even_mixer(hc, hl, ev_w_in[i], ev_w_out[i], sc_conv_w[i], dn_conv_w[i],
                                dn_a_log[i], dn_dt_bias[i], dn_onorm_g[i], need_ctx)
        else:
            yc, yl = odd_mixer(hc, hl, od_w_in[i], od_w_out[i], swa_sink[i], na_rpb[i], need_ctx)
        xl = xl + ml[2] * yl
        hl2 = modulate(rms_norm(xl, norm2_g[l]), ml[3], ml[4])
        if need_ctx:
            xc = xc + mc[2] * yc
            hc2 = modulate(rms_norm(xc, norm2_g[l]), mc[3], mc[4])
            n_c = hc2.shape[0] * hc2.shape[1]
            f = moe(jnp.concatenate([hc2.reshape(-1, d), hl2.reshape(-1, d)], axis=0),
                    router_w, router_b, moe_w_gate[l], moe_w_up[l], moe_w_down[l])
            xc = xc + mc[5] * f[:n_c].reshape(xc.shape)
            xl = xl + ml[5] * f[n_c:].reshape(xl.shape)
        else:
            f = moe(hl2.reshape(-1, d), router_w, router_b, moe_w_gate[l], moe_w_up[l], moe_w_down[l])
            xl = xl + ml[5] * f.reshape(xl.shape)
    return rms_norm(xl, final_g)
```

```python
import functools
import math

import numpy as np
import jax
import jax.numpy as jnp
from jax import lax
from jax.experimental import pallas as pl
from jax.experimental.pallas import tpu as pltpu

F32 = jnp.float32
BF16 = jnp.bfloat16
I32 = jnp.int32
HI = lax.Precision.HIGHEST

EPS = 1e-6
N_MOD = 6
GRID_W = 64
HEAD_DIM = 64
DN_HEADS = 4
DN_HD = 128
DN_CHUNK = 64
SWA_HEADS = 8
SWA_KV = 2
SWA_BLOCK = 128
SWA_WINDOW = 128
NA_HEADS = 8
NA_KH = 8
NA_KW = 16
ROPE_THETA = 10000.0
N_EXPERTS = 16
N_GROUPS = 4
NEG = -1e30
VMEM_LIMIT = 56 * 1024 * 1024


def _cp(sem, vmem=VMEM_LIMIT):
    return pltpu.CompilerParams(dimension_semantics=sem, vmem_limit_bytes=vmem)


def _dot(a, b, precision=None):
    return jnp.dot(a, b, preferred_element_type=F32, precision=precision)


def _dot_nt(a, b, precision=None):
    return lax.dot_general(a, b, (((1,), (1,)), ((), ())), preferred_element_type=F32, precision=precision)


def _dot_tn(a, b, precision=None):
    return lax.dot_general(a, b, (((0,), (0,)), ((), ())), preferred_element_type=F32, precision=precision)


def _silu(x):
    return x * jax.nn.sigmoid(x)


def _rms(x, g):
    return x * lax.rsqrt(jnp.mean(x * x, axis=-1, keepdims=True) + EPS) * g


def _ada_kernel(cc_ref, w_ref, b_ref, o_ref):
    a = _silu(cc_ref[...])
    o_ref[0] = _dot(a, w_ref[0], HI) + b_ref[0]


def _ada(cc, ada_w, ada_b):
    depth, d, n = ada_w.shape
    tn = 1536
    return pl.pallas_call(
        _ada_kernel,
        grid=(depth, n // tn),
        in_specs=[pl.BlockSpec((8, d), lambda l, j: (0, 0)),
                  pl.BlockSpec((1, d, tn), lambda l, j: (l, 0, j)),
                  pl.BlockSpec((1, 1, tn), lambda l, j: (l, 0, j))],
        out_specs=pl.BlockSpec((1, 8, tn), lambda l, j: (l, 0, j)),
        out_shape=jax.ShapeDtypeStruct((depth, 8, n), F32),
        compiler_params=_cp(("parallel", "parallel")),
        name="ada",
    )(cc, ada_w, ada_b.reshape(depth, 1, n))


def _mod_index(i, tm, bs, s, nb):
    row0 = i * tm
    return jnp.where(row0 < bs, row0 // s, nb)


def _inproj0_kernel(x_ref, mod_ref, g_ref, w_ref, sc_ref, qkv_ref, z_ref, bg_ref):
    m = mod_ref[0]
    h = (_rms(x_ref[...], g_ref[...]) * (1.0 + m[1:2]) + m[0:1]).astype(BF16)
    sc_ref[...] = _dot(h, w_ref[:, 0:1536])
    qkv_ref[...] = _dot(h, w_ref[:, 1536:3072])
    z_ref[...] = _dot(h, w_ref[:, 3072:3584])
    bg_ref[...] = _dot(h, w_ref[:, 3584:3712])


def _inproj0(x, mods, g, w, tm, bs, s, nb):
    r, d = x.shape
    mi = functools.partial(_mod_index, tm=tm, bs=bs, s=s, nb=nb)
    return pl.pallas_call(
        _inproj0_kernel,
        grid=(r // tm,),
        in_specs=[pl.BlockSpec((tm, d), lambda i: (i, 0)),
                  pl.BlockSpec((1, N_MOD, d), lambda i: (mi(i), 0, 0)),
                  pl.BlockSpec((1, d), lambda i: (0, 0)),
                  pl.BlockSpec(w.shape, lambda i: (0, 0))],
        out_specs=[pl.BlockSpec((tm, 1536), lambda i: (i, 0)),
                   pl.BlockSpec((tm, 1536), lambda i: (i, 0)),
                   pl.BlockSpec((tm, 512), lambda i: (i, 0)),
                   pl.BlockSpec((tm, 128), lambda i: (i, 0))],
        out_shape=[jax.ShapeDtypeStruct((r, 1536), F32), jax.ShapeDtypeStruct((r, 1536), F32),
                   jax.ShapeDtypeStruct((r, 512), F32), jax.ShapeDtypeStruct((r, 128), F32)],
        compiler_params=_cp(("parallel",)),
        name="inproj0",
    )(x, mods, g, w)


def _seq_edges(i, ts, bs, s, cl):
    row0 = i * ts
    in_lat = row0 < bs
    r_in = jnp.where(in_lat, row0 % s, (row0 - bs) % cl)
    seqlen = jnp.where(in_lat, s, cl)
    return r_in == 0, r_in + ts == seqlen


def _shifted(x, prev_row, next_row):
    n = x.shape[0]
    rows = lax.broadcasted_iota(I32, x.shape, 0)
    xp = jnp.where(rows == 0, prev_row, pltpu.roll(x, 1, 0))
    xn = jnp.where(rows == n - 1, next_row, pltpu.roll(x, n - 1, 0))
    return xp, xn


def _halo_specs(ts, width, r):
    nb8 = r // 8
    k = ts // 8
    return [pl.BlockSpec((8, width), lambda i: (jnp.maximum(i * k - 1, 0), 0)),
            pl.BlockSpec((8, width), lambda i: (jnp.minimum((i + 1) * k, nb8 - 1), 0))]


def _dnprep_kernel(x_ref, prev_ref, next_ref, bg_ref, cw_ref, alog_ref, dt_ref,
                   q_ref, k_ref, v_ref, bga_ref, *, ts, bs, s, cl):
    first, last = _seq_edges(pl.program_id(0), ts, bs, s, cl)
    for c in range(12):
        sl = slice(128 * c, 128 * c + 128)
        x = x_ref[:, sl]
        pr = jnp.where(first, 0.0, prev_ref[7:8, sl])
        nx = jnp.where(last, 0.0, next_ref[0:1, sl])
        xp, xn = _shifted(x, pr, nx)
        w = cw_ref[:, sl]
        y = _silu(xp * w[0:1] + x * w[1:2] + xn * w[2:3])
        hs = slice(128 * (c % 4), 128 * (c % 4) + 128)
        if c < 8:
            y = y * lax.rsqrt(jnp.sum(y * y, axis=-1, keepdims=True) + EPS)
        if c < 4:
            q_ref[:, hs] = y * DN_HD ** -0.5
        elif c < 8:
            k_ref[:, hs] = y
        else:
            v_ref[:, hs] = y
    b = bg_ref[...]
    cols = lax.broadcasted_iota(I32, b.shape, 1)
    beta = jax.nn.sigmoid(b)
    t = b + dt_ref[...]
    softplus = jnp.maximum(t, 0.0) + jnp.log1p(jnp.exp(-jnp.abs(t)))
    g = -jnp.exp(alog_ref[...]) * softplus
    bga_ref[...] = jnp.where(cols < 8, beta, jnp.where(cols < 16, g, 0.0))


def _dnprep(qkv, bg, cw, alog_row, dt_row, ts, bs, s, cl):
    r = qkv.shape[0]
    kern = functools.partial(_dnprep_kernel, ts=ts, bs=bs, s=s, cl=cl)
    return pl.pallas_call(
        kern,
        grid=(r // ts,),
        in_specs=[pl.BlockSpec((ts, 1536), lambda i: (i, 0))] + _halo_specs(ts, 1536, r) + [
            pl.BlockSpec((ts, 128), lambda i: (i, 0)),
            pl.BlockSpec((3, 1536), lambda i: (0, 0)),
            pl.BlockSpec((1, 128), lambda i: (0, 0)),
            pl.BlockSpec((1, 128), lambda i: (0, 0))],
        out_specs=[pl.BlockSpec((ts, 512), lambda i: (i, 0))] * 3 + [pl.BlockSpec((ts, 128), lambda i: (i, 0))],
        out_shape=[jax.ShapeDtypeStruct((r, 512), F32)] * 3 + [jax.ShapeDtypeStruct((r, 128), F32)],
        compiler_params=_cp(("parallel",)),
        name="dnprep",
    )(qkv, qkv, qkv, bg, cw, alog_row, dt_row)


def _dnchunk_kernel(q_ref, k_ref, v_ref, bg_ref,
                    uf_ref, ub_ref, wf_ref, wb_ref, qf_ref, qb_ref, kf_ref, kb_ref, af_ref, ab_ref, gc_ref):
    c = DN_CHUNK
    bg = bg_ref[...]
    ii = lax.broadcasted_iota(I32, (c, c), 0)
    jj = lax.broadcasted_iota(I32, (c, c), 1)
    lower = ii >= jj
    upper = ii <= jj
    eye = (ii == jj).astype(F32)
    cols = lax.broadcasted_iota(I32, bg.shape, 1)
    gcf = _dot(lower.astype(F32), bg, HI)
    gcb = _dot(upper.astype(F32), bg, HI)
    gc = jnp.where(cols >= 12, gcb, gcf)
    gc_ref[...] = gc
    gct = gc.T
    outs = ((uf_ref, wf_ref, qf_ref, kf_ref, af_ref), (ub_ref, wb_ref, qb_ref, kb_ref, ab_ref))
    for d in range(2):
        u_ref, w_ref, qd_ref, kd_ref, at_ref = outs[d]
        incl = lower if d == 0 else upper
        strict = (ii > jj) if d == 0 else (ii < jj)
        last = c - 1 if d == 0 else 0
        ats = []
        for h in range(DN_HEADS):
            hs = slice(DN_HD * h, DN_HD * h + DN_HD)
            cb, cg = 4 * d + h, 8 + 4 * d + h
            beta = bg[:, cb:cb + 1]
            gcol = gc[:, cg:cg + 1]
            grow = gct[cg:cg + 1, :]
            glast = gc[last:last + 1, cg:cg + 1]
            kh, qh, vh = k_ref[:, hs], q_ref[:, hs], v_ref[:, hs]
            decay = jnp.exp(jnp.where(incl, gcol - grow, NEG))
            kkt = _dot_nt(kh, kh, HI)
            nmat = jnp.where(strict, -(beta * kkt * decay), 0.0)
            tinv = eye + nmat
            npow = nmat
            for _ in range(5):
                npow = _dot(npow, npow, HI)
                tinv = tinv + _dot(tinv, npow, HI)
            eg = jnp.exp(gcol)
            u_ref[:, hs] = _dot(tinv, beta * vh, HI)
            w_ref[:, hs] = _dot(tinv, (beta * eg) * kh, HI).astype(BF16)
            qd_ref[:, hs] = (qh * eg).astype(BF16)
            kd_ref[:, hs] = (kh * jnp.exp(glast - gcol)).astype(BF16)
            ats.append(_dot_nt(qh, kh, HI) * decay)
        at_ref[...] = jnp.concatenate(ats, axis=1).astype(BF16)


def _dnchunk(q, k, v, bga):
    r = q.shape[0]
    c = DN_CHUNK
    row = lambda w: pl.BlockSpec((c, w), lambda i: (i, 0))
    shp = lambda w, dt: jax.ShapeDtypeStruct((r, w), dt)
    return pl.pallas_call(
        _dnchunk_kernel,
        grid=(r // c,),
        in_specs=[row(512), row(512), row(512), row(128)],
        out_specs=[row(512)] * 8 + [row(256), row(256), row(128)],
        out_shape=[shp(512, F32)] * 2 + [shp(512, BF16)] * 6 + [shp(256, BF16)] * 2 + [shp(128, F32)],
        compiler_params=_cp(("parallel",)),
        name="dnchunk",
    )(q, k, v, bga)


def _dnscan_kernel(uf_ref, wf_ref, qf_ref, kf_ref, af_ref, gf_ref,
                   ub_ref, wb_ref, qb_ref, kb_ref, ab_ref, gb_ref,
                   of_ref, ob_ref, s_ref):
    @pl.when(pl.program_id(1) == 0)
    def _():
        s_ref[...] = jnp.zeros_like(s_ref)

    c = DN_CHUNK
    dirs = ((uf_ref, wf_ref, qf_ref, kf_ref, af_ref, gf_ref, of_ref, c - 1),
            (ub_ref, wb_ref, qb_ref, kb_ref, ab_ref, gb_ref, ob_ref, 0))
    for d, (u_ref, w_ref, qd_ref, kd_ref, at_ref, g_ref, o_ref, last) in enumerate(dirs):
        for h in range(DN_HEADS):
            hs = slice(DN_HD * h, DN_HD * h + DN_HD)
            cg = 8 + 4 * d + h
            decay = jnp.exp(g_ref[last:last + 1, cg:cg + 1])
            st = s_ref[4 * d + h]
            stb = st.astype(BF16)
            vnew = u_ref[:, hs] - _dot(w_ref[:, hs], stb)
            vnb = vnew.astype(BF16)
            o_ref[:, hs] = _dot(qd_ref[:, hs], stb) + _dot(at_ref[:, c * h:c * h + c], vnb)
            s_ref[4 * d + h] = st * decay + _dot_tn(kd_ref[:, hs], vnb)


def _dnscan(uf, ub, wf, wb, qf, qb, kf, kb, af, ab, gc, nb, s, cl, bs):
    r = uf.shape[0]
    c = DN_CHUNK
    ncc, ncl = cl // c, s // c
    ns = ncc + ncl

    def fwd(b, t):
        return jnp.where(t < ncc, bs // c + b * ncc + t, b * ncl + t - ncc)

    def bwd(b, t):
        return jnp.where(t < ncc, bs // c + b * ncc + (ncc - 1 - t), b * ncl + (ncl - 1 - (t - ncc)))

    def specs(idx):
        blk = lambda w: pl.BlockSpec((c, w), lambda b, t: (idx(b, t), 0))
        return [blk(512), blk(512), blk(512), blk(512), blk(256), blk(128)]

    return pl.pallas_call(
        _dnscan_kernel,
        grid=(nb, ns),
        in_specs=specs(fwd) + specs(bwd),
        out_specs=[pl.BlockSpec((c, 512), lambda b, t: (fwd(b, t), 0)),
                   pl.BlockSpec((c, 512), lambda b, t: (bwd(b, t), 0))],
        out_shape=[jax.ShapeDtypeStruct((r, 512), F32)] * 2,
        scratch_shapes=[pltpu.VMEM((2 * DN_HEADS, DN_HD, DN_HD), F32)],
        compiler_params=_cp(("arbitrary", "arbitrary")),
        name="dnscan",
    )(uf, wf, qf, kf, af, gc, ub, wb, qb, kb, ab, gc)


def _route(logits, bias_row):
    epg = N_EXPERTS // N_GROUPS
    lane = lax.broadcasted_iota(I32, logits.shape, 1).astype(F32)
    scores = jax.nn.sigmoid(logits)
    gsel = scores + bias_row
    big = float(1 << 20)

    def first_argmax(vals, mask):
        mx = jnp.max(jnp.where(mask, vals, -jnp.inf), axis=-1, keepdims=True)
        idx = jnp.min(jnp.where(mask & (vals == mx), lane, big), axis=-1, keepdims=True)
        return mx, idx

    best = None
    gidx = None
    for g in range(N_GROUPS):
        mask = (lane >= epg * g) & (lane < epg * (g + 1))
        t1, i1 = first_argmax(gsel, mask)
        t2, _ = first_argmax(gsel, mask & (lane != i1))
        gs = t1 + t2
        if g == 0:
            best, gidx = gs, jnp.zeros_like(i1)
        else:
            better = gs > best
            best = jnp.where(better, gs, best)
            gidx = jnp.where(better, float(g), gidx)
    mask = (lane >= epg * gidx) & (lane < epg * (gidx + 1.0))
    _, e1 = first_argmax(gsel, mask)
    _, e2 = first_argmax(gsel, mask & (lane != e1))
    w1 = jnp.sum(jnp.where(lane == e1, scores, 0.0), axis=-1, keepdims=True)
    w2 = jnp.sum(jnp.where(lane == e2, scores, 0.0), axis=-1, keepdims=True)
    tot = w1 + w2
    eidx = jnp.where(lane == 0, e1, jnp.where(lane == 1, e2, 0.0)).astype(I32)
    gates = jnp.where(lane == 0, w1 / tot, jnp.where(lane == 1, w2 / tot, 0.0))
    return eidx, gates


def _post_mixer(x, y, m, g2, rw, rb):
    xn = x + m[2:3] * y
    h2 = _rms(xn, g2) * (1.0 + m[4:5]) + m[3:4]
    eidx, gates = _route(_dot(h2, rw, HI), rb)
    return xn, h2, eidx, gates


def _mix0_kernel(x_ref, sc_ref, prev_ref, next_ref, of_ref, ob_ref, z_ref, mod_ref, cw_ref, on_ref, wo_ref,
                 g2_ref, rw_ref, rb_ref, xn_ref, h2_ref, ei_ref, ga_ref, *, ts, bs, s, cl):
    first, last = _seq_edges(pl.program_id(0), ts, bs, s, cl)
    ya = []
    for c in range(4):
        sl = slice(128 * c, 128 * c + 128)
        sg = slice(512 + 128 * c, 512 + 128 * c + 128)
        sx = slice(1024 + 128 * c, 1024 + 128 * c + 128)
        u = sc_ref[:, sg] * sc_ref[:, sx]
        pr = jnp.where(first, 0.0, prev_ref[7:8, sg] * prev_ref[7:8, sx])
        nx = jnp.where(last, 0.0, next_ref[0:1, sg] * next_ref[0:1, sx])
        up, un = _shifted(u, pr, nx)
        w = cw_ref[:, sl]
        ya.append((sc_ref[:, sl] * (up * w[0:1] + u * w[1:2] + un * w[2:3])).astype(BF16))
    yb = []
    for h in range(DN_HEADS):
        hs = slice(DN_HD * h, DN_HD * h + DN_HD)
        o = of_ref[:, hs] + ob_ref[:, hs]
        yb.append((_rms(o, on_ref[...]) * _silu(z_ref[:, hs])).astype(BF16))
    ycat = jnp.concatenate(ya + yb, axis=1)
    y = _dot(ycat, wo_ref[...])
    xn, h2, eidx, gates = _post_mixer(x_ref[...], y, mod_ref[0], g2_ref[...], rw_ref[...], rb_ref[...])
    xn_ref[...] = xn
    h2_ref[...] = h2.astype(BF16)
    ei_ref[...] = eidx
    ga_ref[...] = gates


def _mix0(x, sc, of, ob, z, mods, cw, on, wo, g2, rw, rb, ts, bs, s, cl, nb):
    r, d = x.shape
    kern = functools.partial(_mix0_kernel, ts=ts, bs=bs, s=s, cl=cl)
    mi = functools.partial(_mod_index, tm=ts, bs=bs, s=s, nb=nb)
    row = lambda w: pl.BlockSpec((ts, w), lambda i: (i, 0))
    full = lambda a: pl.BlockSpec(a.shape, lambda i: (0,) * a.ndim)
    return pl.pallas_call(
        kern,
        grid=(r // ts,),
        in_specs=[row(d), row(1536)] + _halo_specs(ts, 1536, r) + [
            row(512), row(512), row(512),
            pl.BlockSpec((1, N_MOD, d), lambda i: (mi(i), 0, 0)),
            full(cw), full(on), full(wo), full(g2), full(rw), full(rb)],
        out_specs=[row(d), row(d), row(128), row(128)],
        out_shape=[jax.ShapeDtypeStruct((r, d), F32), jax.ShapeDtypeStruct((r, d), BF16),
                   jax.ShapeDtypeStruct((r, 128), I32), jax.ShapeDtypeStruct((r, 128), F32)],
        compiler_params=_cp(("parallel",)),
        name="mix0",
    )(x, sc, sc, sc, of, ob, z, mods, cw, on, wo, g2, rw, rb)


def _gmm_kernel(te_ref, tf_ref, tv_ref, x_ref, wg_ref, wu_ref, wd_ref, y_ref, wgb, wub, wdb):
    t = pl.program_id(0)

    @pl.when(tf_ref[t] == 1)
    def _():
        wgb[...] = wg_ref[0].astype(BF16)
        wub[...] = wu_ref[0].astype(BF16)
        wdb[...] = wd_ref[0].astype(BF16)

    @pl.when(tv_ref[t] == 1)
    def _():
        x = x_ref[...]
        a = (_silu(_dot(x, wgb[...])) * _dot(x, wub[...])).astype(BF16)
        y_ref[...] = _dot(a, wdb[...]).astype(BF16)

    @pl.when(tv_ref[t] == 0)
    def _():
        y_ref[...] = jnp.zeros_like(y_ref)


def _gmm(xs, w_gate, w_up, w_down, tile_expert, tile_first, tile_valid, tmm):
    p, d = xs.shape
    de = w_gate.shape[-1]
    nt = p // tmm
    grid_spec = pltpu.PrefetchScalarGridSpec(
        num_scalar_prefetch=3,
        grid=(nt,),
        in_specs=[pl.BlockSpec((tmm, d), lambda t, te, tf, tv: (t, 0)),
                  pl.BlockSpec((1, d, de), lambda t, te, tf, tv: (te[t], 0, 0)),
                  pl.BlockSpec((1, d, de), lambda t, te, tf, tv: (te[t], 0, 0)),
                  pl.BlockSpec((1, de, d), lambda t, te, tf, tv: (te[t], 0, 0))],
        out_specs=pl.BlockSpec((tmm, d), lambda t, te, tf, tv: (t, 0)),
        scratch_shapes=[pltpu.VMEM((d, de), BF16), pltpu.VMEM((d, de), BF16), pltpu.VMEM((de, d), BF16)],
    )
    return pl.pallas_call(
        _gmm_kernel,
        grid_spec=grid_spec,
        out_shape=jax.ShapeDtypeStruct((p, d), BF16),
        compiler_params=_cp(("arbitrary",)),
        name="gmm",
    )(tile_expert, tile_first, tile_valid, xs, w_gate, w_up, w_down)


def _moe(h2, eidx2, w_gate, w_up, w_down, tmm):
    t_tok = h2.shape[0]
    n = 2 * t_tok
    e_flat = eidx2.reshape(-1)
    onehot = (e_flat[:, None] == jnp.arange(N_EXPERTS, dtype=I32)[None, :]).astype(I32)
    csum = jnp.cumsum(onehot, axis=0)
    rank = jnp.take_along_axis(csum, e_flat[:, None], axis=1)[:, 0] - 1
    counts = csum[-1]
    ptiles = (counts + tmm - 1) // tmm
    tile_end = jnp.cumsum(ptiles)
    dest = (tile_end - ptiles)[e_flat] * tmm + rank
    nt = n // tmm + N_EXPERTS
    src = jnp.zeros((nt * tmm,), I32).at[dest].set(jnp.arange(n, dtype=I32) // 2)
    tid = jnp.arange(nt, dtype=I32)
    tile_valid = (tid < tile_end[-1]).astype(I32)
    te = jnp.minimum(jnp.searchsorted(tile_end, tid, side="right").astype(I32), N_EXPERTS - 1)
    last_used = jnp.max(jnp.where(tile_valid == 1, te, 0))
    te = jnp.where(tile_valid == 1, te, last_used)
    tile_first = jnp.concatenate([jnp.ones((1,), I32), (te[1:] != te[:-1]).astype(I32)])
    xs = jnp.take(h2, src, axis=0)
    ys = _gmm(xs, w_gate, w_up, w_down, te, tile_first, tile_valid, tmm)
    d2 = dest.reshape(t_tok, 2)
    return jnp.take(ys, d2[:, 0], axis=0), jnp.take(ys, d2[:, 1], axis=0)


def _moe_combine(x, y1, y2, gates, m5):
    g = gates
    f = g[:, 0:1] * y1.astype(F32) + g[:, 1:2] * y2.astype(F32)
    return x + m5 * f


def _rope(x, cos, sin):
    n = x.shape[1]
    lane = lax.broadcasted_iota(I32, x.shape, 1)
    sw = jnp.where(lane % 32 < 16, pltpu.roll(x, n - 16, 1), pltpu.roll(x, 16, 1))
    reps = n // 128
    if reps > 1:
        cos = jnp.concatenate([cos] * reps, axis=1)
        sin = jnp.concatenate([sin] * reps, axis=1)
    return x * cos + sw * sin


def _inproj1_kernel(x_ref, y1_ref, y2_ref, ga_ref, m0_ref, m1_ref, g_ref, w_ref, cos_ref, sin_ref,
                    x1_ref, cq_ref, ckv_ref, dq_ref, dk_ref, dv_ref, *, tm, bs):
    x1 = _moe_combine(x_ref[...], y1_ref[...], y2_ref[...], ga_ref[...], m0_ref[0][5:6])
    x1_ref[...] = x1
    m = m1_ref[0]
    h = (_rms(x1, g_ref[...]) * (1.0 + m[1:2]) + m[0:1]).astype(BF16)
    in_lat = pl.program_id(0) * tm < bs
    cos, sin = cos_ref[...], sin_ref[...]
    scale = HEAD_DIM ** -0.5
    cq = _dot(h, w_ref[:, 0:512])
    cq_ref[...] = (jnp.where(in_lat, _rope(cq, cos, sin), cq) * scale).astype(BF16)
    ck = _dot(h, w_ref[:, 512:640])
    ckv_ref[:, 0:128] = jnp.where(in_lat, _rope(ck, cos, sin), ck).astype(BF16)
    ckv_ref[:, 128:256] = _dot(h, w_ref[:, 640:768]).astype(BF16)
    dq_ref[...] = (_dot(h, w_ref[:, 768:1280]) * scale).astype(BF16)
    dk_ref[...] = _dot(h, w_ref[:, 1280:1792]).astype(BF16)
    dv_ref[...] = _dot(h, w_ref[:, 1792:2304]).astype(BF16)


def _inproj1(x, y1, y2, gates, mods0, mods1, g, w, cos, sin, tm, bs, s, nb):
    r, d = x.shape
    kern = functools.partial(_inproj1_kernel, tm=tm, bs=bs)
    mi = functools.partial(_mod_index, tm=tm, bs=bs, s=s, nb=nb)
    row = lambda wd: pl.BlockSpec((tm, wd), lambda i: (i, 0))
    modspec = pl.BlockSpec((1, N_MOD, d), lambda i: (mi(i), 0, 0))
    tab = pl.BlockSpec((tm, 128), lambda i: (jnp.where(i * tm < bs, (i * tm % s) // tm, 0), 0))
    shp = lambda wd, dt: jax.ShapeDtypeStruct((r, wd), dt)
    return pl.pallas_call(
        kern,
        grid=(r // tm,),
        in_specs=[row(d), row(d), row(d), row(128), modspec, modspec,
                  pl.BlockSpec((1, d), lambda i: (0, 0)), pl.BlockSpec(w.shape, lambda i: (0, 0)), tab, tab],
        out_specs=[row(d), row(512), row(256), row(512), row(512), row(512)],
        out_shape=[shp(d, F32), shp(512, BF16), shp(256, BF16), shp(512, BF16), shp(512, BF16), shp(512, BF16)],
        compiler_params=_cp(("parallel",)),
        name="inproj1",
    )(x, y1, y2, gates, mods0, mods1, g, w, cos, sin)


def _swa_kernel(q_ref, kp_ref, kc_ref, kn_ref, kx_ref, sink_ref, o_ref, *, nblk):
    i = pl.program_id(1)
    wb = SWA_BLOCK
    kv = jnp.concatenate([kp_ref[...], kc_ref[...], kn_ref[...]], axis=0)
    kk, vv = kv[:, 0:128], kv[:, 128:256]
    kx, vx = kx_ref[:, 0:128], kx_ref[:, 128:256]
    a_i = lax.broadcasted_iota(I32, (wb, 3 * wb), 0)
    c_i = lax.broadcasted_iota(I32, (wb, 3 * wb), 1)
    ok = (c_i >= a_i) & (c_i <= a_i + 2 * SWA_WINDOW)
    lo = jnp.where(i > 0, 0, wb)
    hi = jnp.where(i < nblk - 1, 3 * wb, 2 * wb)
    ok = ok & (c_i >= lo) & (c_i < hi)
    half = lax.broadcasted_iota(I32, (1, 128), 1) // HEAD_DIM
    zero = jnp.zeros((), BF16)
    vm = [jnp.where(half == a, vv, zero) for a in range(2)]
    vxm = [jnp.where(half == a, vx, zero) for a in range(2)]
    sink = sink_ref[...]
    for g in range(4):
        q2 = q_ref[:, 128 * g:128 * g + 128]
        acc = None
        for a in range(2):
            qa = jnp.where(half == a, q2, zero)
            s_loc = jnp.where(ok, _dot_nt(qa, kk), NEG)
            s_ctx = _dot_nt(qa, kx)
            sk = sink[0:1, 2 * g + a:2 * g + a + 1]
            m = jnp.maximum(jnp.maximum(jnp.max(s_loc, axis=-1, keepdims=True),
                                        jnp.max(s_ctx, axis=-1, keepdims=True)), sk)
            p_loc = jnp.exp(s_loc - m)
            p_ctx = jnp.exp(s_ctx - m)
            den = (jnp.sum(p_loc, axis=-1, keepdims=True) + jnp.sum(p_ctx, axis=-1, keepdims=True)
                   + jnp.exp(sk - m))
            o = (_dot(p_loc.astype(BF16), vm[a]) + _dot(p_ctx.astype(BF16), vxm[a])) / den
            acc = o if acc is None else acc + o
        o_ref[:, 128 * g:128 * g + 128] = acc.astype(BF16)


def _swa(cq, ckv, sink_row, nb, s, cl, bs):
    wb = SWA_BLOCK
    nblk = s // wb
    kern = functools.partial(_swa_kernel, nblk=nblk)
    return pl.pallas_call(
        kern,
        grid=(nb, nblk),
        in_specs=[pl.BlockSpec((wb, 512), lambda b, i: (b * nblk + i, 0)),
                  pl.BlockSpec((wb, 256), lambda b, i: (b * nblk + jnp.maximum(i - 1, 0), 0)),
                  pl.BlockSpec((wb, 256), lambda b, i: (b * nblk + i, 0)),
                  pl.BlockSpec((wb, 256), lambda b, i: (b * nblk + jnp.minimum(i + 1, nblk - 1), 0)),
                  pl.BlockSpec((cl, 256), lambda b, i: (bs // cl + b, 0)),
                  pl.BlockSpec((1, 128), lambda b, i: (0, 0))],
        out_specs=pl.BlockSpec((wb, 512), lambda b, i: (b * nblk + i, 0)),
        out_shape=jax.ShapeDtypeStruct((bs, 512), BF16),
        compiler_params=_cp(("parallel", "parallel")),
        name="swa",
    )(cq, ckv, ckv, ckv, ckv, sink_row)


def _na_kernel(q_ref, k_ref, v_ref, kx_ref, vx_ref, bias_ref, o_ref, *, rows):
    half = lax.broadcasted_iota(I32, (1, 128), 1) // HEAD_DIM
    zero = jnp.zeros((), BF16)
    kx, vx = kx_ref[...], vx_ref[...]
    vxm = [jnp.where(half == a, vx, zero) for a in range(2)]
    span = NA_KH * GRID_W

    def body(r, carry):
        rs = jnp.clip(r - NA_KH // 2, 0, rows - NA_KH)
        off = rs - r + NA_KH - 1
        q0 = pl.multiple_of(r * GRID_W, GRID_W)
        k0 = pl.multiple_of(rs * GRID_W, GRID_W)
        q2 = q_ref[pl.ds(q0, GRID_W), :]
        kk = k_ref[pl.ds(k0, span), :]
        vv = v_ref[pl.ds(k0, span), :]
        acc = None
        for a in range(2):
            qa = jnp.where(half == a, q2, zero)
            s_loc = _dot_nt(qa, kk) + bias_ref[0, off, a]
            s_ctx = _dot_nt(qa, kx)
            m = jnp.maximum(jnp.max(s_loc, axis=-1, keepdims=True), jnp.max(s_ctx, axis=-1, keepdims=True))
            p_loc = jnp.exp(s_loc - m)
            p_ctx = jnp.exp(s_ctx - m)
            den = jnp.sum(p_loc, axis=-1, keepdims=True) + jnp.sum(p_ctx, axis=-1, keepdims=True)
            va = jnp.where(half == a, vv, zero)
            o = (_dot(p_loc.astype(BF16), va) + _dot(p_ctx.astype(BF16), vxm[a])) / den
            acc = o if acc is None else acc + o
        o_ref[pl.ds(q0, GRID_W), :] = acc.astype(BF16)
        return carry

    lax.fori_loop(0, rows, body, 0)


def _na(dq, dk, dv, bias, nb, s, cl, bs):
    rows = s // GRID_W
    kern = functools.partial(_na_kernel, rows=rows)
    seq = pl.BlockSpec((s, 128), lambda b, p: (b, p))
    ctx = pl.BlockSpec((cl, 128), lambda b, p: (bs // cl + b, p))
    return pl.pallas_call(
        kern,
        grid=(nb, NA_HEADS // 2),
        in_specs=[seq, seq, seq, ctx, ctx,
                  pl.BlockSpec((1, NA_KH, 2, GRID_W, NA_KH * GRID_W), lambda b, p: (p, 0, 0, 0, 0))],
        out_specs=seq,
        out_shape=jax.ShapeDtypeStruct((bs, 512), BF16),
        compiler_params=_cp(("parallel", "parallel")),
        name="na",
    )(dq, dk, dv, dk, dv, bias)


def _na_bias_table(rpb):
    c = np.arange(GRID_W)
    qs = np.clip(c - NA_KW // 2, 0, GRID_W - NA_KW)
    kc = np.arange(GRID_W)
    ok = (kc[None, :] >= qs[:, None]) & (kc[None, :] < qs[:, None] + NA_KW)
    dc = np.clip(kc[None, :] - c[:, None] + NA_KW - 1, 0, 2 * NA_KW - 2)
    dr = np.arange(NA_KH)[:, None] + np.arange(NA_KH)[None, :]
    t = rpb.astype(F32)[:, dr][:, :, :, dc]
    t = jnp.where(ok[None, None, None], t, NEG)
    t = jnp.transpose(t, (0, 1, 3, 2, 4)).reshape(NA_HEADS, NA_KH, GRID_W, NA_KH * GRID_W)
    t = t.reshape(NA_HEADS // 2, 2, NA_KH, GRID_W, NA_KH * GRID_W)
    return jnp.transpose(t, (0, 2, 1, 3, 4))


def _mix1_kernel(x_ref, oc_ref, od_ref, mod_ref, wo_ref, g2_ref, rw_ref, rb_ref,
                 xn_ref, h2_ref, ei_ref, ga_ref):
    y = _dot(oc_ref[...], wo_ref[0:512, :]) + _dot(od_ref[...], wo_ref[512:1024, :])
    xn, h2, eidx, gates = _post_mixer(x_ref[...], y, mod_ref[0], g2_ref[...], rw_ref[...], rb_ref[...])
    xn_ref[...] = xn
    h2_ref[...] = h2.astype(BF16)
    ei_ref[...] = eidx
    ga_ref[...] = gates


def _mix1(x, oc, od, mods, wo, g2, rw, rb, tm, bs, s):
    d = x.shape[1]
    row = lambda w: pl.BlockSpec((tm, w), lambda i: (i, 0))
    full = lambda a: pl.BlockSpec(a.shape, lambda i: (0,) * a.ndim)
    return pl.pallas_call(
        _mix1_kernel,
        grid=(bs // tm,),
        in_specs=[row(d), row(512), row(512), pl.BlockSpec((1, N_MOD, d), lambda i: (i * tm // s, 0, 0)),
                  full(wo), full(g2), full(rw), full(rb)],
        out_specs=[row(d), row(d), row(128), row(128)],
        out_shape=[jax.ShapeDtypeStruct((bs, d), F32), jax.ShapeDtypeStruct((bs, d), BF16),
                   jax.ShapeDtypeStruct((bs, 128), I32), jax.ShapeDtypeStruct((bs, 128), F32)],
        compiler_params=_cp(("parallel",)),
        name="mix1",
    )(x, oc, od, mods, wo, g2, rw, rb)


def _final_kernel(x_ref, y1_ref, y2_ref, ga_ref, mod_ref, g_ref, o_ref):
    x = _moe_combine(x_ref[...], y1_ref[...], y2_ref[...], ga_ref[...], mod_ref[0][5:6])
    o_ref[...] = _rms(x, g_ref[...])


def _final(x, y1, y2, gates, mods, g, tm, s):
    r, d = x.shape
    row = lambda w: pl.BlockSpec((tm, w), lambda i: (i, 0))
    return pl.pallas_call(
        _final_kernel,
        grid=(r // tm,),
        in_specs=[row(d), row(d), row(d), row(128), pl.BlockSpec((1, N_MOD, d), lambda i: (i * tm // s, 0, 0)),
                  pl.BlockSpec((1, d), lambda i: (0, 0))],
        out_specs=row(d),
        out_shape=jax.ShapeDtypeStruct((r, d), F32),
        compiler_params=_cp(("parallel",)),
        name="final",
    )(x, y1, y2, gates, mods, g)


def _rope_tables(s):
    half = HEAD_DIM // 2
    nf = half // 2
    t = np.arange(s)
    inv = ROPE_THETA ** (-jnp.arange(nf, dtype=F32) / nf)
    ar = jnp.asarray(t // GRID_W, F32)[:, None] * inv
    ac = jnp.asarray(t % GRID_W, F32)[:, None] * inv
    cos = jnp.concatenate([jnp.cos(ar), jnp.cos(ar), jnp.cos(ac), jnp.cos(ac)], axis=1)
    sin = jnp.concatenate([-jnp.sin(ar), jnp.sin(ar), -jnp.sin(ac), jnp.sin(ac)], axis=1)
    return jnp.concatenate([cos, cos], axis=1), jnp.concatenate([sin, sin], axis=1)


def kernel(x, c, ctx, c_ctx, ada_w, ada_b, norm1_g, norm2_g, ev_w_in, ev_w_out, sc_conv_w, dn_conv_w, dn_a_log, dn_dt_bias, dn_onorm_g, od_w_in, od_w_out, swa_sink, na_rpb, router_w, router_b, moe_w_gate, moe_w_up, moe_w_down, final_g):
    nb, s, d = x.shape
    cl = ctx.shape[1]
    bs = nb * s
    tm = 512
    ts = 256
    tmm = 512
    assert d == 1024 and s % tm == 0 and (nb * cl) % tm == 0 and cl % ts == 0 and s % ts == 0
    assert s // GRID_W >= NA_KH and bs % cl == 0 and nb + 1 <= 8

    xf = jnp.concatenate([x.reshape(bs, d), ctx.reshape(nb * cl, d)], axis=0)
    cc = jnp.zeros((8, d), F32).at[:nb].set(c).at[nb].set(c_ctx)
    mods = _ada(cc, ada_w, ada_b).reshape(ada_w.shape[0], 8, N_MOD, d)
    rw = jnp.pad(router_w, ((0, 0), (0, 128 - N_EXPERTS)))
    rb = jnp.pad(router_b, (0, 128 - N_EXPERTS)).reshape(1, 128)
    row = lambda v: v.reshape(1, -1)

    w_in0 = jnp.pad(ev_w_in[0], ((0, 0), (0, 3712 - ev_w_in.shape[-1]))).astype(BF16)
    sc, qkv, z, bg = _inproj0(xf, mods[0], row(norm1_g[0]), w_in0, tm, bs, s, nb)
    pad16 = lambda v: jnp.pad(v.reshape(-1), (8, 128 - 16)).reshape(1, 128)
    q, k, v, bga = _dnprep(qkv, bg, dn_conv_w[0], pad16(dn_a_log[0]), pad16(dn_dt_bias[0]), ts, bs, s, cl)
    uf, ub, wf, wb, qf, qb, kf, kb, af, ab, gc = _dnchunk(q, k, v, bga)
    of, ob = _dnscan(uf, ub, wf, wb, qf, qb, kf, kb, af, ab, gc, nb, s, cl, bs)
    x0, h2, ei, ga = _mix0(xf, sc, of, ob, z, mods[0], sc_conv_w[0], row(dn_onorm_g[0]),
                           ev_w_out[0].astype(BF16), row(norm2_g[0]), rw, rb, ts, bs, s, cl, nb)
    y1, y2 = _moe(h2, ei[:, 0:2], moe_w_gate[0], moe_w_up[0], moe_w_down[0], tmm)

    perm = np.concatenate([np.arange(HEAD_DIM) + HEAD_DIM * (g + 4 * a) for g in range(4) for a in range(2)])
    w1 = od_w_in[0]
    w_in1 = jnp.concatenate([w1[:, 0:512][:, perm], w1[:, 512:]], axis=1).astype(BF16)
    wo1 = od_w_out[0]
    w_out1 = jnp.concatenate([wo1[0:512][perm], wo1[512:]], axis=0).astype(BF16)
    sink_row = jnp.pad(swa_sink[0][np.array([g + 4 * a for g in range(4) for a in range(2)])],
                       (0, 128 - SWA_HEADS)).reshape(1, 128)
    cos, sin = _rope_tables(s)
    x1, cq, ckv, dq, dk, dv = _inproj1(x0, y1, y2, ga, mods[0], mods[1], row(norm1_g[1]), w_in1, cos, sin,
                                       tm, bs, s, nb)
    oc = _swa(cq, ckv, sink_row, nb, s, cl, bs)
    od = _na(dq, dk, dv, _na_bias_table(na_rpb[0]), nb, s, cl, bs)
    x2, h2, ei, ga = _mix1(x1, oc, od, mods[1], w_out1, row(norm2_g[1]), rw, rb, tm, bs, s)
    y1, y2 = _moe(h2, ei[:, 0:2], moe_w_gate[1], moe_w_up[1], moe_w_down[1], tmm)
    out = _final(x2, y1, y2, ga, mods[1], row(final_g), tm, s)
    return out.reshape(nb, s, d)
```

```python
import functools
import math

import numpy as np
import jax
import jax.numpy as jnp
from jax import lax
from jax.experimental import pallas as pl
from jax.experimental.pallas import tpu as pltpu

F32 = jnp.float32
BF16 = jnp.bfloat16
I32 = jnp.int32
HI = lax.Precision.HIGHEST

EPS = 1e-6
N_MOD = 6
GRID_W = 64
HEAD_DIM = 64
DN_HEADS = 4
DN_HD = 128
DN_CHUNK = 64
SWA_HEADS = 8
SWA_KV = 2
SWA_BLOCK = 128
SWA_WINDOW = 128
NA_HEADS = 8
NA_KH = 8
NA_KW = 16
ROPE_THETA = 10000.0
N_EXPERTS = 16
N_GROUPS = 4
NEG = -1e30
VMEM_LIMIT = 56 * 1024 * 1024


def _cp(sem, vmem=VMEM_LIMIT):
    return pltpu.CompilerParams(dimension_semantics=sem, vmem_limit_bytes=vmem)


def _dot(a, b, precision=None):
    return jnp.dot(a, b, preferred_element_type=F32, precision=precision)


def _dot_nt(a, b, precision=None):
    return lax.dot_general(a, b, (((1,), (1,)), ((), ())), preferred_element_type=F32, precision=precision)


def _dot_tn(a, b, precision=None):
    return lax.dot_general(a, b, (((0,), (0,)), ((), ())), preferred_element_type=F32, precision=precision)


def _silu(x):
    return x * jax.nn.sigmoid(x)


def _rms(x, g):
    return x * lax.rsqrt(jnp.mean(x * x, axis=-1, keepdims=True) + EPS) * g


def _ada_kernel(cc_ref, w_ref, b_ref, o_ref):
    a = _silu(cc_ref[...])
    o_ref[0] = _dot(a, w_ref[0], HI) + b_ref[0]


def _ada(cc, ada_w, ada_b):
    depth, d, n = ada_w.shape
    tn = 1536
    return pl.pallas_call(
        _ada_kernel,
        grid=(depth, n // tn),
        in_specs=[pl.BlockSpec((8, d), lambda l, j: (0, 0)),
                  pl.BlockSpec((1, d, tn), lambda l, j: (l, 0, j)),
                  pl.BlockSpec((1, 1, tn), lambda l, j: (l, 0, j))],
        out_specs=pl.BlockSpec((1, 8, tn), lambda l, j: (l, 0, j)),
        out_shape=jax.ShapeDtypeStruct((depth, 8, n), F32),
        compiler_params=_cp(("parallel", "parallel")),
        name="ada",
    )(cc, ada_w, ada_b.reshape(depth, 1, n))


def _mod_index(i, tm, bs, s, nb):
    row0 = i * tm
    return jnp.where(row0 < bs, row0 // s, nb)


def _inproj0_kernel(x_ref, mod_ref, g_ref, w_ref, sc_ref, qkv_ref, z_ref, bg_ref):
    m = mod_ref[0]
    h = (_rms(x_ref[...], g_ref[...]) * (1.0 + m[1:2]) + m[0:1]).astype(BF16)
    sc_ref[...] = _dot(h, w_ref[:, 0:1536])
    qkv_ref[...] = _dot(h, w_ref[:, 1536:3072])
    z_ref[...] = _dot(h, w_ref[:, 3072:3584])
    bg_ref[...] = _dot(h, w_ref[:, 3584:3712])


def _inproj0(x, mods, g, w, tm, bs, s, nb):
    r, d = x.shape
    mi = functools.partial(_mod_index, tm=tm, bs=bs, s=s, nb=nb)
    return pl.pallas_call(
        _inproj0_kernel,
        grid=(r // tm,),
        in_specs=[pl.BlockSpec((tm, d), lambda i: (i, 0)),
                  pl.BlockSpec((1, N_MOD, d), lambda i: (mi(i), 0, 0)),
                  pl.BlockSpec((1, d), lambda i: (0, 0)),
                  pl.BlockSpec(w.shape, lambda i: (0, 0))],
        out_specs=[pl.BlockSpec((tm, 1536), lambda i: (i, 0)),
                   pl.BlockSpec((tm, 1536), lambda i: (i, 0)),
                   pl.BlockSpec((tm, 512), lambda i: (i, 0)),
                   pl.BlockSpec((tm, 128), lambda i: (i, 0))],
        out_shape=[jax.ShapeDtypeStruct((r, 1536), F32), jax.ShapeDtypeStruct((r, 1536), F32),
                   jax.ShapeDtypeStruct((r, 512), F32), jax.ShapeDtypeStruct((r, 128), F32)],
        compiler_params=_cp(("parallel",)),
        name="inproj0",
    )(x, mods, g, w)


def _seq_edges(i, ts, bs, s, cl):
    row0 = i * ts
    in_lat = row0 < bs
    r_in = jnp.where(in_lat, row0 % s, (row0 - bs) % cl)
    seqlen = jnp.where(in_lat, s, cl)
    return r_in == 0, r_in + ts == seqlen


def _shifted(x, prev_row, next_row):
    n = x.shape[0]
    rows = lax.broadcasted_iota(I32, x.shape, 0)
    xp = jnp.where(rows == 0, prev_row, pltpu.roll(x, 1, 0))
    xn = jnp.where(rows == n - 1, next_row, pltpu.roll(x, n - 1, 0))
    return xp, xn


def _halo_specs(ts, width, r):
    nb8 = r // 8
    k = ts // 8
    return [pl.BlockSpec((8, width), lambda i: (jnp.maximum(i * k - 1, 0), 0)),
            pl.BlockSpec((8, width), lambda i: (jnp.minimum((i + 1) * k, nb8 - 1), 0))]


def _dnprep_kernel(x_ref, prev_ref, next_ref, bg_ref, cw_ref, alog_ref, dt_ref,
                   q_ref, k_ref, v_ref, bga_ref, *, ts, bs, s, cl):
    first, last = _seq_edges(pl.program_id(0), ts, bs, s, cl)
    for c in range(12):
        sl = slice(128 * c, 128 * c + 128)
        x = x_ref[:, sl]
        pr = jnp.where(first, 0.0, prev_ref[7:8, sl])
        nx = jnp.where(last, 0.0, next_ref[0:1, sl])
        xp, xn = _shifted(x, pr, nx)
        w = cw_ref[:, sl]
        y = _silu(xp * w[0:1] + x * w[1:2] + xn * w[2:3])
        hs = slice(128 * (c % 4), 128 * (c % 4) + 128)
        if c < 8:
            y = y * lax.rsqrt(jnp.sum(y * y, axis=-1, keepdims=True) + EPS)
        if c < 4:
            q_ref[:, hs] = y * DN_HD ** -0.5
        elif c < 8:
            k_ref[:, hs] = y
        else:
            v_ref[:, hs] = y
    b = bg_ref[...]
    cols = lax.broadcasted_iota(I32, b.shape, 1)
    beta = jax.nn.sigmoid(b)
    t = b + dt_ref[...]
    softplus = jnp.maximum(t, 0.0) + jnp.log1p(jnp.exp(-jnp.abs(t)))
    g = -jnp.exp(alog_ref[...]) * softplus
    bga_ref[...] = jnp.where(cols < 8, beta, jnp.where(cols < 16, g, 0.0))


def _dnprep(qkv, bg, cw, alog_row, dt_row, ts, bs, s, cl):
    r = qkv.shape[0]
    kern = functools.partial(_dnprep_kernel, ts=ts, bs=bs, s=s, cl=cl)
    return pl.pallas_call(
        kern,
        grid=(r // ts,),
        in_specs=[pl.BlockSpec((ts, 1536), lambda i: (i, 0))] + _halo_specs(ts, 1536, r) + [
            pl.BlockSpec((ts, 128), lambda i: (i, 0)),
            pl.BlockSpec((3, 1536), lambda i: (0, 0)),
            pl.BlockSpec((1, 128), lambda i: (0, 0)),
            pl.BlockSpec((1, 128), lambda i: (0, 0))],
        out_specs=[pl.BlockSpec((ts, 512), lambda i: (i, 0))] * 3 + [pl.BlockSpec((ts, 128), lambda i: (i, 0))],
        out_shape=[jax.ShapeDtypeStruct((r, 512), F32)] * 3 + [jax.ShapeDtypeStruct((r, 128), F32)],
        compiler_params=_cp(("parallel",)),
        name="dnprep",
    )(qkv, qkv, qkv, bg, cw, alog_row, dt_row)


def _dnchunk_kernel(q_ref, k_ref, v_ref, bg_ref,
                    uf_ref, ub_ref, wf_ref, wb_ref, qf_ref, qb_ref, kf_ref, kb_ref, af_ref, ab_ref, gc_ref):
    c = DN_CHUNK
    bg = bg_ref[...]
    ii = lax.broadcasted_iota(I32, (c, c), 0)
    jj = lax.broadcasted_iota(I32, (c, c), 1)
    lower = ii >= jj
    upper = ii <= jj
    cols = lax.broadcasted_iota(I32, bg.shape, 1)
    gcf = _dot(lower.astype(F32), bg, HI)
    gcb = _dot(upper.astype(F32), bg, HI)
    gc = jnp.where(cols >= 12, gcb, gcf)
    gc_ref[...] = gc
    gct = gc.T
    outs = ((uf_ref, wf_ref, qf_ref, kf_ref, af_ref), (ub_ref, wb_ref, qb_ref, kb_ref, ab_ref))
    for d in range(2):
        u_ref, w_ref, qd_ref, kd_ref, at_ref = outs[d]
        incl = lower if d == 0 else upper
        strict = (ii > jj) if d == 0 else (ii < jj)
        last = c - 1 if d == 0 else 0
        ats = []
        for h in range(DN_HEADS):
            hs = slice(DN_HD * h, DN_HD * h + DN_HD)
            cb, cg = 4 * d + h, 8 + 4 * d + h
            beta = bg[:, cb:cb + 1]
            gcol = gc[:, cg:cg + 1]
            grow = gct[cg:cg + 1, :]
            glast = gc[last:last + 1, cg:cg + 1]
            kh, qh, vh = k_ref[:, hs], q_ref[:, hs], v_ref[:, hs]
            decay = jnp.exp(jnp.where(incl, gcol - grow, NEG))
            kb = kh.astype(BF16)
            kq = _dot_nt(jnp.concatenate([kb, qh.astype(BF16)], axis=0), kb)
            nmat = jnp.where(strict, -(beta * kq[0:c] * decay), 0.0)
            tm = nmat
            npow = nmat
            for _ in range(5):
                nb16 = npow.astype(BF16)
                npow = _dot(nb16, nb16)
                tm = tm + npow + _dot(tm.astype(BF16), npow.astype(BF16))
            eg = jnp.exp(gcol)
            rhs = jnp.concatenate([beta * vh, (beta * eg) * kh], axis=1)
            uw = rhs + _dot(tm.astype(BF16), rhs.astype(BF16))
            u_ref[:, hs] = uw[:, 0:DN_HD]
            w_ref[:, hs] = uw[:, DN_HD:2 * DN_HD].astype(BF16)
            qd_ref[:, hs] = (qh * eg).astype(BF16)
            kd_ref[:, hs] = (kh * jnp.exp(glast - gcol)).astype(BF16)
            ats.append(kq[c:2 * c] * decay)
        at_ref[...] = jnp.concatenate(ats, axis=1).astype(BF16)


def _dnchunk(q, k, v, bga):
    r = q.shape[0]
    c = DN_CHUNK
    row = lambda w: pl.BlockSpec((c, w), lambda i: (i, 0))
    shp = lambda w, dt: jax.ShapeDtypeStruct((r, w), dt)
    return pl.pallas_call(
        _dnchunk_kernel,
        grid=(r // c,),
        in_specs=[row(512), row(512), row(512), row(128)],
        out_specs=[row(512)] * 8 + [row(256), row(256), row(128)],
        out_shape=[shp(512, F32)] * 2 + [shp(512, BF16)] * 6 + [shp(256, BF16)] * 2 + [shp(128, F32)],
        compiler_params=_cp(("parallel",)),
        name="dnchunk",
    )(q, k, v, bga)


def _dnscan_kernel(uf_ref, wf_ref, qf_ref, kf_ref, af_ref, gf_ref,
                   ub_ref, wb_ref, qb_ref, kb_ref, ab_ref, gb_ref,
                   of_ref, ob_ref, s_ref):
    @pl.when(pl.program_id(1) == 0)
    def _():
        s_ref[...] = jnp.zeros_like(s_ref)

    c = DN_CHUNK
    dirs = ((uf_ref, wf_ref, qf_ref, kf_ref, af_ref, gf_ref, of_ref, c - 1),
            (ub_ref, wb_ref, qb_ref, kb_ref, ab_ref, gb_ref, ob_ref, 0))
    for d, (u_ref, w_ref, qd_ref, kd_ref, at_ref, g_ref, o_ref, last) in enumerate(dirs):
        for h in range(DN_HEADS):
            hs = slice(DN_HD * h, DN_HD * h + DN_HD)
            cg = 8 + 4 * d + h
            decay = jnp.exp(g_ref[last:last + 1, cg:cg + 1])
            st = s_ref[4 * d + h]
            stb = st.astype(BF16)
            vnew = u_ref[:, hs] - _dot(w_ref[:, hs], stb)
            vnb = vnew.astype(BF16)
            o_ref[:, hs] = _dot(qd_ref[:, hs], stb) + _dot(at_ref[:, c * h:c * h + c], vnb)
            s_ref[4 * d + h] = st * decay + _dot_tn(kd_ref[:, hs], vnb)


def _dnscan(uf, ub, wf, wb, qf, qb, kf, kb, af, ab, gc, nb, s, cl, bs):
    r = uf.shape[0]
    c = DN_CHUNK
    ncc, ncl = cl // c, s // c
    ns = ncc + ncl

    def fwd(b, t):
        return jnp.where(t < ncc, bs // c + b * ncc + t, b * ncl + t - ncc)

    def bwd(b, t):
        return jnp.where(t < ncc, bs // c + b * ncc + (ncc - 1 - t), b * ncl + (ncl - 1 - (t - ncc)))

    def specs(idx):
        blk = lambda w: pl.BlockSpec((c, w), lambda b, t: (idx(b, t), 0))
        return [blk(512), blk(512), blk(512), blk(512), blk(256), blk(128)]

    return pl.pallas_call(
        _dnscan_kernel,
        grid=(nb, ns),
        in_specs=specs(fwd) + specs(bwd),
        out_specs=[pl.BlockSpec((c, 512), lambda b, t: (fwd(b, t), 0)),
                   pl.BlockSpec((c, 512), lambda b, t: (bwd(b, t), 0))],
        out_shape=[jax.ShapeDtypeStruct((r, 512), F32)] * 2,
        scratch_shapes=[pltpu.VMEM((2 * DN_HEADS, DN_HD, DN_HD), F32)],
        compiler_params=_cp(("arbitrary", "arbitrary")),
        name="dnscan",
    )(uf, wf, qf, kf, af, gc, ub, wb, qb, kb, ab, gc)


def _route(logits, bias_row):
    epg = N_EXPERTS // N_GROUPS
    lane = lax.broadcasted_iota(I32, logits.shape, 1).astype(F32)
    scores = jax.nn.sigmoid(logits)
    gsel = scores + bias_row
    big = float(1 << 20)

    def first_argmax(vals, mask):
        mx = jnp.max(jnp.where(mask, vals, -jnp.inf), axis=-1, keepdims=True)
        idx = jnp.min(jnp.where(mask & (vals == mx), lane, big), axis=-1, keepdims=True)
        return mx, idx

    best = None
    gidx = None
    for g in range(N_GROUPS):
        mask = (lane >= epg * g) & (lane < epg * (g + 1))
        t1, i1 = first_argmax(gsel, mask)
        t2, _ = first_argmax(gsel, mask & (lane != i1))
        gs = t1 + t2
        if g == 0:
            best, gidx = gs, jnp.zeros_like(i1)
        else:
            better = gs > best
            best = jnp.where(better, gs, best)
            gidx = jnp.where(better, float(g), gidx)
    mask = (lane >= epg * gidx) & (lane < epg * (gidx + 1.0))
    _, e1 = first_argmax(gsel, mask)
    _, e2 = first_argmax(gsel, mask & (lane != e1))
    w1 = jnp.sum(jnp.where(lane == e1, scores, 0.0), axis=-1, keepdims=True)
    w2 = jnp.sum(jnp.where(lane == e2, scores, 0.0), axis=-1, keepdims=True)
    tot = w1 + w2
    eidx = jnp.where(lane == 0, e1, jnp.where(lane == 1, e2, 0.0)).astype(I32)
    gates = jnp.where(lane == 0, w1 / tot, jnp.where(lane == 1, w2 / tot, 0.0))
    return eidx, gates


def _post_mixer(x, y, m, g2, rw, rb):
    xn = x + m[2:3] * y
    h2 = _rms(xn, g2) * (1.0 + m[4:5]) + m[3:4]
    eidx, gates = _route(_dot(h2, rw, HI), rb)
    return xn, h2, eidx, gates


def _mix0_kernel(x_ref, sc_ref, prev_ref, next_ref, of_ref, ob_ref, z_ref, mod_ref, cw_ref, on_ref, wo_ref,
                 g2_ref, rw_ref, rb_ref, xn_ref, h2_ref, ei_ref, ga_ref, *, ts, bs, s, cl):
    first, last = _seq_edges(pl.program_id(0), ts, bs, s, cl)
    ya = []
    for c in range(4):
        sl = slice(128 * c, 128 * c + 128)
        sg = slice(512 + 128 * c, 512 + 128 * c + 128)
        sx = slice(1024 + 128 * c, 1024 + 128 * c + 128)
        u = sc_ref[:, sg] * sc_ref[:, sx]
        pr = jnp.where(first, 0.0, prev_ref[7:8, sg] * prev_ref[7:8, sx])
        nx = jnp.where(last, 0.0, next_ref[0:1, sg] * next_ref[0:1, sx])
        up, un = _shifted(u, pr, nx)
        w = cw_ref[:, sl]
        ya.append((sc_ref[:, sl] * (up * w[0:1] + u * w[1:2] + un * w[2:3])).astype(BF16))
    yb = []
    for h in range(DN_HEADS):
        hs = slice(DN_HD * h, DN_HD * h + DN_HD)
        o = of_ref[:, hs] + ob_ref[:, hs]
        yb.append((_rms(o, on_ref[...]) * _silu(z_ref[:, hs])).astype(BF16))
    ycat = jnp.concatenate(ya + yb, axis=1)
    y = _dot(ycat, wo_ref[...])
    xn, h2, eidx, gates = _post_mixer(x_ref[...], y, mod_ref[0], g2_ref[...], rw_ref[...], rb_ref[...])
    xn_ref[...] = xn
    h2_ref[...] = h2.astype(BF16)
    ei_ref[...] = eidx
    ga_ref[...] = gates


def _mix0(x, sc, of, ob, z, mods, cw, on, wo, g2, rw, rb, ts, bs, s, cl, nb):
    r, d = x.shape
    kern = functools.partial(_mix0_kernel, ts=ts, bs=bs, s=s, cl=cl)
    mi = functools.partial(_mod_index, tm=ts, bs=bs, s=s, nb=nb)
    row = lambda w: pl.BlockSpec((ts, w), lambda i: (i, 0))
    full = lambda a: pl.BlockSpec(a.shape, lambda i: (0,) * a.ndim)
    return pl.pallas_call(
        kern,
        grid=(r // ts,),
        in_specs=[row(d), row(1536)] + _halo_specs(ts, 1536, r) + [
            row(512), row(512), row(512),
            pl.BlockSpec((1, N_MOD, d), lambda i: (mi(i), 0, 0)),
            full(cw), full(on), full(wo), full(g2), full(rw), full(rb)],
        out_specs=[row(d), row(d), row(128), row(128)],
        out_shape=[jax.ShapeDtypeStruct((r, d), F32), jax.ShapeDtypeStruct((r, d), BF16),
                   jax.ShapeDtypeStruct((r, 128), I32), jax.ShapeDtypeStruct((r, 128), F32)],
        compiler_params=_cp(("parallel",)),
        name="mix0",
    )(x, sc, sc, sc, of, ob, z, mods, cw, on, wo, g2, rw, rb)


def _gmm_kernel(te_ref, tf_ref, tv_ref, x_ref, wg_ref, wu_ref, wd_ref, y_ref, wgb, wub, wdb):
    t = pl.program_id(0)

    @pl.when(tf_ref[t] == 1)
    def _():
        wgb[...] = wg_ref[0, 0].astype(BF16)
        wub[...] = wu_ref[0, 0].astype(BF16)
        wdb[...] = wd_ref[0, 0].astype(BF16)

    @pl.when(tv_ref[t] == 1)
    def _():
        x = x_ref[...]
        a = (_silu(_dot(x, wgb[...])) * _dot(x, wub[...])).astype(BF16)
        y_ref[...] = _dot(a, wdb[...]).astype(BF16)

    @pl.when(tv_ref[t] == 0)
    def _():
        y_ref[...] = jnp.zeros_like(y_ref)


def _gmm(xs, w_gate, w_up, w_down, layer, tile_expert, tile_first, tile_valid, tmm):
    p, d = xs.shape
    de = w_gate.shape[-1]
    nt = p // tmm
    grid_spec = pltpu.PrefetchScalarGridSpec(
        num_scalar_prefetch=3,
        grid=(nt,),
        in_specs=[pl.BlockSpec((tmm, d), lambda t, te, tf, tv: (t, 0)),
                  pl.BlockSpec((1, 1, d, de), lambda t, te, tf, tv: (layer, te[t], 0, 0)),
                  pl.BlockSpec((1, 1, d, de), lambda t, te, tf, tv: (layer, te[t], 0, 0)),
                  pl.BlockSpec((1, 1, de, d), lambda t, te, tf, tv: (layer, te[t], 0, 0))],
        out_specs=pl.BlockSpec((tmm, d), lambda t, te, tf, tv: (t, 0)),
        scratch_shapes=[pltpu.VMEM((d, de), BF16), pltpu.VMEM((d, de), BF16), pltpu.VMEM((de, d), BF16)],
    )
    return pl.pallas_call(
        _gmm_kernel,
        grid_spec=grid_spec,
        out_shape=jax.ShapeDtypeStruct((p, d), BF16),
        compiler_params=_cp(("arbitrary",)),
        name="gmm",
    )(tile_expert, tile_first, tile_valid, xs, w_gate, w_up, w_down)


def _moe(h2, eidx2, w_gate, w_up, w_down, layer, tmm):
    t_tok = h2.shape[0]
    n = 2 * t_tok
    e_flat = eidx2.reshape(-1)
    onehot = (e_flat[:, None] == jnp.arange(N_EXPERTS, dtype=I32)[None, :]).astype(I32)
    csum = jnp.cumsum(onehot, axis=0)
    counts = csum[-1]
    ptiles = (counts + tmm - 1) // tmm
    tile_end = jnp.cumsum(ptiles)
    dest = jnp.sum(onehot * (csum - 1 + ((tile_end - ptiles) * tmm)[None, :]), axis=1)
    nt = n // tmm + N_EXPERTS
    src = jnp.zeros((nt * tmm,), I32).at[dest].set(jnp.arange(n, dtype=I32) // 2)
    tid = jnp.arange(nt, dtype=I32)
    tile_valid = (tid < tile_end[-1]).astype(I32)
    te = jnp.minimum(jnp.sum((tile_end[None, :] <= tid[:, None]).astype(I32), axis=1), N_EXPERTS - 1)
    last_used = jnp.max(jnp.where(tile_valid == 1, te, 0))
    te = jnp.where(tile_valid == 1, te, last_used)
    tile_first = jnp.concatenate([jnp.ones((1,), I32), (te[1:] != te[:-1]).astype(I32)])
    xs = jnp.take(jnp.pad(h2, ((0, nt * tmm - t_tok), (0, 0))), src, axis=0, mode="clip")
    ys = _gmm(xs, w_gate, w_up, w_down, layer, te, tile_first, tile_valid, tmm)
    d2 = dest.reshape(t_tok, 2)
    return jnp.take(ys, d2[:, 0], axis=0, mode="clip"), jnp.take(ys, d2[:, 1], axis=0, mode="clip")


def _moe_combine(x, y1, y2, gates, m5):
    g = gates
    f = g[:, 0:1] * y1.astype(F32) + g[:, 1:2] * y2.astype(F32)
    return x + m5 * f


def _rope(x, cos, sin):
    n = x.shape[1]
    lane = lax.broadcasted_iota(I32, x.shape, 1)
    sw = jnp.where(lane % 32 < 16, pltpu.roll(x, n - 16, 1), pltpu.roll(x, 16, 1))
    reps = n // 128
    if reps > 1:
        cos = jnp.concatenate([cos] * reps, axis=1)
        sin = jnp.concatenate([sin] * reps, axis=1)
    return x * cos + sw * sin


def _inproj1_kernel(x_ref, y1_ref, y2_ref, ga_ref, m0_ref, m1_ref, g_ref, w_ref, cos_ref, sin_ref,
                    x1_ref, cq_ref, ckv_ref, dq_ref, dk_ref, dv_ref, *, tm, bs):
    x1 = _moe_combine(x_ref[...], y1_ref[...], y2_ref[...], ga_ref[...], m0_ref[0][5:6])
    x1_ref[...] = x1
    m = m1_ref[0]
    h = (_rms(x1, g_ref[...]) * (1.0 + m[1:2]) + m[0:1]).astype(BF16)
    in_lat = pl.program_id(0) * tm < bs
    cos, sin = cos_ref[...], sin_ref[...]
    scale = HEAD_DIM ** -0.5
    cq = _dot(h, w_ref[:, 0:512])
    cq_ref[...] = (jnp.where(in_lat, _rope(cq, cos, sin), cq) * scale).astype(BF16)
    ck = _dot(h, w_ref[:, 512:640])
    ckv_ref[:, 0:128] = jnp.where(in_lat, _rope(ck, cos, sin), ck).astype(BF16)
    ckv_ref[:, 128:256] = _dot(h, w_ref[:, 640:768]).astype(BF16)
    dq_ref[...] = (_dot(h, w_ref[:, 768:1280]) * scale).astype(BF16)
    dk_ref[...] = _dot(h, w_ref[:, 1280:1792]).astype(BF16)
    dv_ref[...] = _dot(h, w_ref[:, 1792:2304]).astype(BF16)


def _inproj1(x, y1, y2, gates, mods0, mods1, g, w, cos, sin, tm, bs, s, nb):
    r, d = x.shape
    kern = functools.partial(_inproj1_kernel, tm=tm, bs=bs)
    mi = functools.partial(_mod_index, tm=tm, bs=bs, s=s, nb=nb)
    row = lambda wd: pl.BlockSpec((tm, wd), lambda i: (i, 0))
    modspec = pl.BlockSpec((1, N_MOD, d), lambda i: (mi(i), 0, 0))
    tab = pl.BlockSpec((tm, 128), lambda i: (jnp.where(i * tm < bs, (i * tm % s) // tm, 0), 0))
    shp = lambda wd, dt: jax.ShapeDtypeStruct((r, wd), dt)
    return pl.pallas_call(
        kern,
        grid=(r // tm,),
        in_specs=[row(d), row(d), row(d), row(128), modspec, modspec,
                  pl.BlockSpec((1, d), lambda i: (0, 0)), pl.BlockSpec(w.shape, lambda i: (0, 0)), tab, tab],
        out_specs=[row(d), row(512), row(256), row(512), row(512), row(512)],
        out_shape=[shp(d, F32), shp(512, BF16), shp(256, BF16), shp(512, BF16), shp(512, BF16), shp(512, BF16)],
        compiler_params=_cp(("parallel",)),
        name="inproj1",
    )(x, y1, y2, gates, mods0, mods1, g, w, cos, sin)


def _swa_kernel(q_ref, kp_ref, kc_ref, kn_ref, kx_ref, sink_ref, o_ref, *, nblk):
    i = pl.program_id(1)
    wb = SWA_BLOCK
    kv = jnp.concatenate([kp_ref[...], kc_ref[...], kn_ref[...]], axis=0)
    kk, vv = kv[:, 0:128], kv[:, 128:256]
    kx, vx = kx_ref[:, 0:128], kx_ref[:, 128:256]
    a_i = lax.broadcasted_iota(I32, (wb, 3 * wb), 0)
    c_i = lax.broadcasted_iota(I32, (wb, 3 * wb), 1)
    ok = (c_i >= a_i) & (c_i <= a_i + 2 * SWA_WINDOW)
    lo = jnp.where(i > 0, 0, wb)
    hi = jnp.where(i < nblk - 1, 3 * wb, 2 * wb)
    ok = ok & (c_i >= lo) & (c_i < hi)
    half = lax.broadcasted_iota(I32, (1, 128), 1) // HEAD_DIM
    zero = jnp.zeros((), BF16)
    vm = [jnp.where(half == a, vv, zero) for a in range(2)]
    vxm = [jnp.where(half == a, vx, zero) for a in range(2)]
    sink = sink_ref[...]
    for g in range(4):
        q2 = q_ref[:, 128 * g:128 * g + 128]
        acc = None
        for a in range(2):
            qa = jnp.where(half == a, q2, zero)
            s_loc = jnp.where(ok, _dot_nt(qa, kk), NEG)
            s_ctx = _dot_nt(qa, kx)
            sk = sink[0:1, 2 * g + a:2 * g + a + 1]
            m = jnp.maximum(jnp.maximum(jnp.max(s_loc, axis=-1, keepdims=True),
                                        jnp.max(s_ctx, axis=-1, keepdims=True)), sk)
            p_loc = jnp.exp(s_loc - m)
            p_ctx = jnp.exp(s_ctx - m)
            den = (jnp.sum(p_loc, axis=-1, keepdims=True) + jnp.sum(p_ctx, axis=-1, keepdims=True)
                   + jnp.exp(sk - m))
            o = (_dot(p_loc.astype(BF16), vm[a]) + _dot(p_ctx.astype(BF16), vxm[a])) / den
            acc = o if acc is None else acc + o
        o_ref[:, 128 * g:128 * g + 128] = acc.astype(BF16)


def _swa(cq, ckv, sink_row, nb, s, cl, bs):
    wb = SWA_BLOCK
    nblk = s // wb
    kern = functools.partial(_swa_kernel, nblk=nblk)
    return pl.pallas_call(
        kern,
        grid=(nb, nblk),
        in_specs=[pl.BlockSpec((wb, 512), lambda b, i: (b * nblk + i, 0)),
                  pl.BlockSpec((wb, 256), lambda b, i: (b * nblk + jnp.maximum(i - 1, 0), 0)),
                  pl.BlockSpec((wb, 256), lambda b, i: (b * nblk + i, 0)),
                  pl.BlockSpec((wb, 256), lambda b, i: (b * nblk + jnp.minimum(i + 1, nblk - 1), 0)),
                  pl.BlockSpec((cl, 256), lambda b, i: (bs // cl + b, 0)),
                  pl.BlockSpec((1, 128), lambda b, i: (0, 0))],
        out_specs=pl.BlockSpec((wb, 512), lambda b, i: (b * nblk + i, 0)),
        out_shape=jax.ShapeDtypeStruct((bs, 512), BF16),
        compiler_params=_cp(("parallel", "parallel")),
        name="swa",
    )(cq, ckv, ckv, ckv, ckv, sink_row)


def _na_kernel(q_ref, k_ref, v_ref, kx_ref, vx_ref, bias_ref, o_ref, *, rows):
    half = lax.broadcasted_iota(I32, (1, 128), 1) // HEAD_DIM
    zero = jnp.zeros((), BF16)
    kx, vx = kx_ref[...], vx_ref[...]
    vxm = [jnp.where(half == a, vx, zero) for a in range(2)]
    span = NA_KH * GRID_W

    def body(r, carry):
        rs = jnp.clip(r - NA_KH // 2, 0, rows - NA_KH)
        off = rs - r + NA_KH - 1
        q0 = pl.multiple_of(r * GRID_W, GRID_W)
        k0 = pl.multiple_of(rs * GRID_W, GRID_W)
        q2 = q_ref[pl.ds(q0, GRID_W), :]
        kk = k_ref[pl.ds(k0, span), :]
        vv = v_ref[pl.ds(k0, span), :]
        acc = None
        for a in range(2):
            qa = jnp.where(half == a, q2, zero)
            s_loc = _dot_nt(qa, kk) + bias_ref[0, off, a]
            s_ctx = _dot_nt(qa, kx)
            m = jnp.maximum(jnp.max(s_loc, axis=-1, keepdims=True), jnp.max(s_ctx, axis=-1, keepdims=True))
            p_loc = jnp.exp(s_loc - m)
            p_ctx = jnp.exp(s_ctx - m)
            den = jnp.sum(p_loc, axis=-1, keepdims=True) + jnp.sum(p_ctx, axis=-1, keepdims=True)
            va = jnp.where(half == a, vv, zero)
            o = (_dot(p_loc.astype(BF16), va) + _dot(p_ctx.astype(BF16), vxm[a])) / den
            acc = o if acc is None else acc + o
        o_ref[pl.ds(q0, GRID_W), :] = acc.astype(BF16)
        return carry

    lax.fori_loop(0, rows, body, 0)


def _na(dq, dk, dv, bias, nb, s, cl, bs):
    rows = s // GRID_W
    kern = functools.partial(_na_kernel, rows=rows)
    seq = pl.BlockSpec((s, 128), lambda b, p: (b, p))
    ctx = pl.BlockSpec((cl, 128), lambda b, p: (bs // cl + b, p))
    return pl.pallas_call(
        kern,
        grid=(nb, NA_HEADS // 2),
        in_specs=[seq, seq, seq, ctx, ctx,
                  pl.BlockSpec((1, NA_KH, 2, GRID_W, NA_KH * GRID_W), lambda b, p: (p, 0, 0, 0, 0))],
        out_specs=seq,
        out_shape=jax.ShapeDtypeStruct((bs, 512), BF16),
        compiler_params=_cp(("parallel", "parallel")),
        name="na",
    )(dq, dk, dv, dk, dv, bias)


def _na_bias_table(rpb):
    c = np.arange(GRID_W)
    qs = np.clip(c - NA_KW // 2, 0, GRID_W - NA_KW)
    kc = np.arange(GRID_W)
    ok = (kc[None, :] >= qs[:, None]) & (kc[None, :] < qs[:, None] + NA_KW)
    dc = np.clip(kc[None, :] - c[:, None] + NA_KW - 1, 0, 2 * NA_KW - 2)
    sel = (np.arange(2 * NA_KW - 1)[:, None, None] == dc[None]).astype(np.float32)
    cols = jnp.einsum("hab,bck->hack", rpb.astype(F32), sel, precision=HI)
    cols = jnp.where(ok[None, None], cols, NEG)
    t = jnp.stack([cols[:, off:off + NA_KH] for off in range(NA_KH)], axis=1)
    t = jnp.transpose(t, (0, 1, 3, 2, 4)).reshape(NA_HEADS, NA_KH, GRID_W, NA_KH * GRID_W)
    t = t.reshape(NA_HEADS // 2, 2, NA_KH, GRID_W, NA_KH * GRID_W)
    return jnp.transpose(t, (0, 2, 1, 3, 4))


def _mix1_kernel(x_ref, oc_ref, od_ref, mod_ref, wo_ref, g2_ref, rw_ref, rb_ref,
                 xn_ref, h2_ref, ei_ref, ga_ref):
    y = _dot(oc_ref[...], wo_ref[0:512, :]) + _dot(od_ref[...], wo_ref[512:1024, :])
    xn, h2, eidx, gates = _post_mixer(x_ref[...], y, mod_ref[0], g2_ref[...], rw_ref[...], rb_ref[...])
    xn_ref[...] = xn
    h2_ref[...] = h2.astype(BF16)
    ei_ref[...] = eidx
    ga_ref[...] = gates


def _mix1(x, oc, od, mods, wo, g2, rw, rb, tm, bs, s):
    d = x.shape[1]
    row = lambda w: pl.BlockSpec((tm, w), lambda i: (i, 0))
    full = lambda a: pl.BlockSpec(a.shape, lambda i: (0,) * a.ndim)
    return pl.pallas_call(
        _mix1_kernel,
        grid=(bs // tm,),
        in_specs=[row(d), row(512), row(512), pl.BlockSpec((1, N_MOD, d), lambda i: (i * tm // s, 0, 0)),
                  full(wo), full(g2), full(rw), full(rb)],
        out_specs=[row(d), row(d), row(128), row(128)],
        out_shape=[jax.ShapeDtypeStruct((bs, d), F32), jax.ShapeDtypeStruct((bs, d), BF16),
                   jax.ShapeDtypeStruct((bs, 128), I32), jax.ShapeDtypeStruct((bs, 128), F32)],
        compiler_params=_cp(("parallel",)),
        name="mix1",
    )(x, oc, od, mods, wo, g2, rw, rb)


def _final_kernel(x_ref, y1_ref, y2_ref, ga_ref, mod_ref, g_ref, o_ref):
    x = _moe_combine(x_ref[...], y1_ref[...], y2_ref[...], ga_ref[...], mod_ref[0][5:6])
    o_ref[...] = _rms(x, g_ref[...])


def _final(x, y1, y2, gates, mods, g, tm, s):
    r, d = x.shape
    row = lambda w: pl.BlockSpec((tm, w), lambda i: (i, 0))
    return pl.pallas_call(
        _final_kernel,
        grid=(r // tm,),
        in_specs=[row(d), row(d), row(d), row(128), pl.BlockSpec((1, N_MOD, d), lambda i: (i * tm // s, 0, 0)),
                  pl.BlockSpec((1, d), lambda i: (0, 0))],
        out_specs=row(d),
        out_shape=jax.ShapeDtypeStruct((r, d), F32),
        compiler_params=_cp(("parallel",)),
        name="final",
    )(x, y1, y2, gates, mods, g)


def _rope_tables(s):
    nf = HEAD_DIM // 4
    t = np.arange(s)
    inv = ROPE_THETA ** (-np.arange(nf, dtype=np.float64) / nf)
    ar = (t // GRID_W)[:, None] * inv
    ac = (t % GRID_W)[:, None] * inv
    cos = np.concatenate([np.cos(ar), np.cos(ar), np.cos(ac), np.cos(ac)], axis=1)
    sin = np.concatenate([-np.sin(ar), np.sin(ar), -np.sin(ac), np.sin(ac)], axis=1)
    return (jnp.asarray(np.concatenate([cos, cos], axis=1), F32),
            jnp.asarray(np.concatenate([sin, sin], axis=1), F32))


def kernel(x, c, ctx, c_ctx, ada_w, ada_b, norm1_g, norm2_g, ev_w_in, ev_w_out, sc_conv_w, dn_conv_w, dn_a_log, dn_dt_bias, dn_onorm_g, od_w_in, od_w_out, swa_sink, na_rpb, router_w, router_b, moe_w_gate, moe_w_up, moe_w_down, final_g):
    nb, s, d = x.shape
    cl = ctx.shape[1]
    bs = nb * s
    tm = 512
    ts = 256
    tmm = 512
    assert d == 1024 and s % tm == 0 and (nb * cl) % tm == 0 and cl % ts == 0 and s % ts == 0
    assert s // GRID_W >= NA_KH and bs % cl == 0 and nb + 1 <= 8

    xf = jnp.concatenate([x.reshape(bs, d), ctx.reshape(nb * cl, d)], axis=0)
    cc = jnp.zeros((8, d), F32).at[:nb].set(c).at[nb].set(c_ctx)
    mods = _ada(cc, ada_w, ada_b).reshape(ada_w.shape[0], 8, N_MOD, d)
    rw = jnp.pad(router_w, ((0, 0), (0, 128 - N_EXPERTS)))
    rb = jnp.pad(router_b, (0, 128 - N_EXPERTS)).reshape(1, 128)
    row = lambda v: v.reshape(1, -1)

    w_in0 = jnp.pad(ev_w_in[0], ((0, 0), (0, 3712 - ev_w_in.shape[-1]))).astype(BF16)
    sc, qkv, z, bg = _inproj0(xf, mods[0], row(norm1_g[0]), w_in0, tm, bs, s, nb)
    pad16 = lambda v: jnp.pad(v.reshape(-1), (8, 128 - 16)).reshape(1, 128)
    q, k, v, bga = _dnprep(qkv, bg, dn_conv_w[0], pad16(dn_a_log[0]), pad16(dn_dt_bias[0]), ts, bs, s, cl)
    uf, ub, wf, wb, qf, qb, kf, kb, af, ab, gc = _dnchunk(q, k, v, bga)
    of, ob = _dnscan(uf, ub, wf, wb, qf, qb, kf, kb, af, ab, gc, nb, s, cl, bs)
    x0, h2, ei, ga = _mix0(xf, sc, of, ob, z, mods[0], sc_conv_w[0], row(dn_onorm_g[0]),
                           ev_w_out[0].astype(BF16), row(norm2_g[0]), rw, rb, ts, bs, s, cl, nb)
    y1, y2 = _moe(h2, ei[:, 0:2], moe_w_gate, moe_w_up, moe_w_down, 0, tmm)

    perm = np.concatenate([np.arange(HEAD_DIM) + HEAD_DIM * (g + 4 * a) for g in range(4) for a in range(2)])
    w1 = od_w_in[0]
    w_in1 = jnp.concatenate([w1[:, 0:512][:, perm], w1[:, 512:]], axis=1).astype(BF16)
    wo1 = od_w_out[0]
    w_out1 = jnp.concatenate([wo1[0:512][perm], wo1[512:]], axis=0).astype(BF16)
    sink_row = jnp.pad(swa_sink[0][np.array([g + 4 * a for g in range(4) for a in range(2)])],
                       (0, 128 - SWA_HEADS)).reshape(1, 128)
    cos, sin = _rope_tables(s)
    x1, cq, ckv, dq, dk, dv = _inproj1(x0, y1, y2, ga, mods[0], mods[1], row(norm1_g[1]), w_in1, cos, sin,
                                       tm, bs, s, nb)
    oc = _swa(cq, ckv, sink_row, nb, s, cl, bs)
    od = _na(dq, dk, dv, _na_bias_table(na_rpb[0]), nb, s, cl, bs)
    x2, h2, ei, ga = _mix1(x1, oc, od, mods[1], w_out1, row(norm2_g[1]), rw, rb, tm, bs, s)
    y1, y2 = _moe(h2, ei[:, 0:2], moe_w_gate, moe_w_up, moe_w_down, 1, tmm)
    out = _final(x2, y1, y2, ga, mods[1], row(final_g), tm, s)
    return out.reshape(nb, s, d)
```

```python
import functools
import math

import numpy as np
import jax
import jax.numpy as jnp
from jax import lax
from jax.experimental import pallas as pl
from jax.experimental.pallas import tpu as pltpu

F32 = jnp.float32
BF16 = jnp.bfloat16
I32 = jnp.int32
HI = lax.Precision.HIGHEST

EPS = 1e-6
N_MOD = 6
GRID_W = 64
HEAD_DIM = 64
DN_HEADS = 4
DN_HD = 128
DN_CHUNK = 64
SWA_HEADS = 8
SWA_KV = 2
SWA_BLOCK = 128
SWA_WINDOW = 128
NA_HEADS = 8
NA_KH = 8
NA_KW = 16
ROPE_THETA = 10000.0
N_EXPERTS = 16
N_GROUPS = 4
NEG = -1e30
VMEM_LIMIT = 56 * 1024 * 1024


def _cp(sem, vmem=VMEM_LIMIT):
    return pltpu.CompilerParams(dimension_semantics=sem, vmem_limit_bytes=vmem)


def _dot(a, b, precision=None):
    return jnp.dot(a, b, preferred_element_type=F32, precision=precision)


def _dot_nt(a, b, precision=None):
    return lax.dot_general(a, b, (((1,), (1,)), ((), ())), preferred_element_type=F32, precision=precision)


def _dot_tn(a, b, precision=None):
    return lax.dot_general(a, b, (((0,), (0,)), ((), ())), preferred_element_type=F32, precision=precision)


def _silu(x):
    return x * jax.nn.sigmoid(x)


def _rms(x, g):
    return x * lax.rsqrt(jnp.mean(x * x, axis=-1, keepdims=True) + EPS) * g


def _ada_kernel(cc_ref, w_ref, b_ref, o_ref):
    a = _silu(cc_ref[...])
    o_ref[0] = _dot(a, w_ref[0], HI) + b_ref[0]


def _ada(cc, ada_w, ada_b):
    depth, d, n = ada_w.shape
    tn = 1536
    return pl.pallas_call(
        _ada_kernel,
        grid=(depth, n // tn),
        in_specs=[pl.BlockSpec((8, d), lambda l, j: (0, 0)),
                  pl.BlockSpec((1, d, tn), lambda l, j: (l, 0, j)),
                  pl.BlockSpec((1, 1, tn), lambda l, j: (l, 0, j))],
        out_specs=pl.BlockSpec((1, 8, tn), lambda l, j: (l, 0, j)),
        out_shape=jax.ShapeDtypeStruct((depth, 8, n), F32),
        compiler_params=_cp(("parallel", "parallel")),
        name="ada",
    )(cc, ada_w, ada_b.reshape(depth, 1, n))


def _mod_index(i, tm, bs, s, nb):
    row0 = i * tm
    return jnp.where(row0 < bs, row0 // s, nb)


def _inproj0_kernel(x_ref, mod_ref, g_ref, w_ref, sc_ref, qkv_ref, z_ref, bg_ref):
    m = mod_ref[0]
    h = (_rms(x_ref[...], g_ref[...]) * (1.0 + m[1:2]) + m[0:1]).astype(BF16)
    sc_ref[...] = _dot(h, w_ref[:, 0:1536])
    qkv_ref[...] = _dot(h, w_ref[:, 1536:3072])
    z_ref[...] = _dot(h, w_ref[:, 3072:3584])
    bg_ref[...] = _dot(h, w_ref[:, 3584:3712])


def _inproj0(x, mods, g, w, tm, bs, s, nb):
    r, d = x.shape
    mi = functools.partial(_mod_index, tm=tm, bs=bs, s=s, nb=nb)
    return pl.pallas_call(
        _inproj0_kernel,
        grid=(r // tm,),
        in_specs=[pl.BlockSpec((tm, d), lambda i: (i, 0)),
                  pl.BlockSpec((1, N_MOD, d), lambda i: (mi(i), 0, 0)),
                  pl.BlockSpec((1, d), lambda i: (0, 0)),
                  pl.BlockSpec(w.shape, lambda i: (0, 0))],
        out_specs=[pl.BlockSpec((tm, 1536), lambda i: (i, 0)),
                   pl.BlockSpec((tm, 1536), lambda i: (i, 0)),
                   pl.BlockSpec((tm, 512), lambda i: (i, 0)),
                   pl.BlockSpec((tm, 128), lambda i: (i, 0))],
        out_shape=[jax.ShapeDtypeStruct((r, 1536), F32), jax.ShapeDtypeStruct((r, 1536), F32),
                   jax.ShapeDtypeStruct((r, 512), F32), jax.ShapeDtypeStruct((r, 128), F32)],
        compiler_params=_cp(("parallel",)),
        name="inproj0",
    )(x, mods, g, w)


def _seq_edges(i, ts, bs, s, cl):
    row0 = i * ts
    in_lat = row0 < bs
    r_in = jnp.where(in_lat, row0 % s, (row0 - bs) % cl)
    seqlen = jnp.where(in_lat, s, cl)
    return r_in == 0, r_in + ts == seqlen


def _shifted(x, prev_row, next_row):
    n = x.shape[0]
    rows = lax.broadcasted_iota(I32, x.shape, 0)
    xp = jnp.where(rows == 0, prev_row, pltpu.roll(x, 1, 0))
    xn = jnp.where(rows == n - 1, next_row, pltpu.roll(x, n - 1, 0))
    return xp, xn


def _halo_specs(ts, width, r):
    nb8 = r // 8
    k = ts // 8
    return [pl.BlockSpec((8, width), lambda i: (jnp.maximum(i * k - 1, 0), 0)),
            pl.BlockSpec((8, width), lambda i: (jnp.minimum((i + 1) * k, nb8 - 1), 0))]


def _dnprep_kernel(x_ref, prev_ref, next_ref, bg_ref, cw_ref, alog_ref, dt_ref,
                   q_ref, k_ref, v_ref, bga_ref, *, ts, bs, s, cl):
    first, last = _seq_edges(pl.program_id(0), ts, bs, s, cl)
    for c in range(12):
        sl = slice(128 * c, 128 * c + 128)
        x = x_ref[:, sl]
        pr = jnp.where(first, 0.0, prev_ref[7:8, sl])
        nx = jnp.where(last, 0.0, next_ref[0:1, sl])
        xp, xn = _shifted(x, pr, nx)
        w = cw_ref[:, sl]
        y = _silu(xp * w[0:1] + x * w[1:2] + xn * w[2:3])
        hs = slice(128 * (c % 4), 128 * (c % 4) + 128)
        if c < 8:
            y = y * lax.rsqrt(jnp.sum(y * y, axis=-1, keepdims=True) + EPS)
        if c < 4:
            q_ref[:, hs] = y * DN_HD ** -0.5
        elif c < 8:
            k_ref[:, hs] = y
        else:
            v_ref[:, hs] = y
    b = bg_ref[...]
    cols = lax.broadcasted_iota(I32, b.shape, 1)
    beta = jax.nn.sigmoid(b)
    t = b + dt_ref[...]
    softplus = jnp.maximum(t, 0.0) + jnp.log1p(jnp.exp(-jnp.abs(t)))
    g = -jnp.exp(alog_ref[...]) * softplus
    bga_ref[...] = jnp.where(cols < 8, beta, jnp.where(cols < 16, g, 0.0))


def _dnprep(qkv, bg, cw, alog_row, dt_row, ts, bs, s, cl):
    r = qkv.shape[0]
    kern = functools.partial(_dnprep_kernel, ts=ts, bs=bs, s=s, cl=cl)
    return pl.pallas_call(
        kern,
        grid=(r // ts,),
        in_specs=[pl.BlockSpec((ts, 1536), lambda i: (i, 0))] + _halo_specs(ts, 1536, r) + [
            pl.BlockSpec((ts, 128), lambda i: (i, 0)),
            pl.BlockSpec((3, 1536), lambda i: (0, 0)),
            pl.BlockSpec((1, 128), lambda i: (0, 0)),
            pl.BlockSpec((1, 128), lambda i: (0, 0))],
        out_specs=[pl.BlockSpec((ts, 512), lambda i: (i, 0))] * 3 + [pl.BlockSpec((ts, 128), lambda i: (i, 0))],
        out_shape=[jax.ShapeDtypeStruct((r, 512), F32)] * 3 + [jax.ShapeDtypeStruct((r, 128), F32)],
        compiler_params=_cp(("parallel",)),
        name="dnprep",
    )(qkv, qkv, qkv, bg, cw, alog_row, dt_row)


def _dnchunk_kernel(q_ref, k_ref, v_ref, bg_ref,
                    uf_ref, ub_ref, wf_ref, wb_ref, qf_ref, qb_ref, kf_ref, kb_ref, af_ref, ab_ref, gc_ref):
    c, nh = DN_CHUNK, DN_HEADS
    n = c * nh
    bg = bg_ref[...]
    i64 = lax.broadcasted_iota(I32, (c, c), 0)
    j64 = lax.broadcasted_iota(I32, (c, c), 1)
    cols = lax.broadcasted_iota(I32, bg.shape, 1)
    gcf = _dot((i64 >= j64).astype(F32), bg, HI)
    gcb = _dot((i64 <= j64).astype(F32), bg, HI)
    gc = jnp.where(cols >= 12, gcb, gcf)
    gc_ref[...] = gc
    gct = gc.T
    ii = lax.broadcasted_iota(I32, (n, n), 0)
    jj = lax.broadcasted_iota(I32, (n, n), 1)
    same = (ii // c) == (jj // c)

    def stack(ref):
        return jnp.concatenate([ref[:, DN_HD * h:DN_HD * h + DN_HD] for h in range(nh)], axis=0)

    def col_bcast(arr, r0, r1, col0):
        return jnp.concatenate([jnp.broadcast_to(arr[r0:r1, col0 + h:col0 + h + 1], (c, n)) for h in range(nh)],
                               axis=0)

    kst, qst, vst = stack(k_ref), stack(q_ref), stack(v_ref)
    kb = kst.astype(BF16)
    kq = _dot_nt(jnp.concatenate([kb, qst.astype(BF16)], axis=0), kb)
    outs = ((uf_ref, wf_ref, qf_ref, kf_ref, af_ref), (ub_ref, wb_ref, qb_ref, kb_ref, ab_ref))
    for d in range(2):
        u_ref, w_ref, qd_ref, kd_ref, at_ref = outs[d]
        incl = same & ((ii >= jj) if d == 0 else (ii <= jj))
        strict = same & ((ii > jj) if d == 0 else (ii < jj))
        last = c - 1 if d == 0 else 0
        beta = col_bcast(bg, 0, c, 4 * d)
        gcol = col_bcast(gc, 0, c, 8 + 4 * d)
        glast = col_bcast(gc, last, last + 1, 8 + 4 * d)
        grow = jnp.concatenate([gct[8 + 4 * d + h:9 + 4 * d + h, :] for h in range(nh)], axis=1)
        decay = jnp.exp(jnp.where(incl, gcol - grow, NEG))
        nmat = jnp.where(strict, -(beta * kq[0:n] * decay), 0.0)
        tm = nmat
        nb16 = nmat.astype(BF16)
        npow = _dot(nb16, nb16)
        for _ in range(4):
            nb16 = npow.astype(BF16)
            both = _dot(jnp.concatenate([nb16, tm.astype(BF16)], axis=0), nb16)
            tm = tm + npow + both[n:2 * n]
            npow = both[0:n]
        tm = tm + npow + _dot(tm.astype(BF16), npow.astype(BF16))
        eg = jnp.exp(gcol)
        b1, e1 = beta[:, 0:DN_HD], eg[:, 0:DN_HD]
        rhs = jnp.concatenate([b1 * vst, (b1 * e1) * kst], axis=1)
        uw = rhs + _dot(tm.astype(BF16), rhs.astype(BF16))
        qd = (qst * e1).astype(BF16)
        kd = (kst * jnp.exp(glast - gcol)[:, 0:DN_HD]).astype(BF16)
        att = kq[n:2 * n] * decay
        for h in range(nh):
            hs = slice(DN_HD * h, DN_HD * h + DN_HD)
            rs = slice(c * h, c * h + c)
            u_ref[:, hs] = uw[rs, 0:DN_HD]
            w_ref[:, hs] = uw[rs, DN_HD:2 * DN_HD].astype(BF16)
            qd_ref[:, hs] = qd[rs]
            kd_ref[:, hs] = kd[rs]
        at_ref[...] = (att[0:c] + att[c:2 * c] + att[2 * c:3 * c] + att[3 * c:4 * c]).astype(BF16)


def _dnchunk(q, k, v, bga):
    r = q.shape[0]
    c = DN_CHUNK
    row = lambda w: pl.BlockSpec((c, w), lambda i: (i, 0))
    shp = lambda w, dt: jax.ShapeDtypeStruct((r, w), dt)
    return pl.pallas_call(
        _dnchunk_kernel,
        grid=(r // c,),
        in_specs=[row(512), row(512), row(512), row(128)],
        out_specs=[row(512)] * 8 + [row(256), row(256), row(128)],
        out_shape=[shp(512, F32)] * 2 + [shp(512, BF16)] * 6 + [shp(256, BF16)] * 2 + [shp(128, F32)],
        compiler_params=_cp(("parallel",)),
        name="dnchunk",
    )(q, k, v, bga)


def _dnscan_kernel(uf_ref, wf_ref, qf_ref, kf_ref, af_ref, gf_ref,
                   ub_ref, wb_ref, qb_ref, kb_ref, ab_ref, gb_ref,
                   of_ref, ob_ref, s_ref):
    @pl.when(pl.program_id(1) == 0)
    def _():
        s_ref[...] = jnp.zeros_like(s_ref)

    c, nh = DN_CHUNK, DN_HEADS
    head_of_lane = lax.broadcasted_iota(I32, (1, nh * DN_HD), 1) // DN_HD
    head_of_col = lax.broadcasted_iota(I32, (1, nh * c), 1) // c
    zero = jnp.zeros((), BF16)

    def block_diag(tile, head_ids):
        return jnp.concatenate([jnp.where(head_ids == h, tile, zero) for h in range(nh)], axis=0)

    dirs = ((uf_ref, wf_ref, qf_ref, kf_ref, af_ref, gf_ref, of_ref, c - 1),
            (ub_ref, wb_ref, qb_ref, kb_ref, ab_ref, gb_ref, ob_ref, 0))
    for d, (u_ref, w_ref, qd_ref, kd_ref, at_ref, g_ref, o_ref, last) in enumerate(dirs):
        g = g_ref[...]
        decay = jnp.concatenate(
            [jnp.broadcast_to(jnp.exp(g[last:last + 1, 8 + 4 * d + h:9 + 4 * d + h]), (DN_HD, DN_HD))
             for h in range(nh)], axis=0)
        ust = jnp.concatenate([u_ref[:, DN_HD * h:DN_HD * h + DN_HD] for h in range(nh)], axis=0)
        st = s_ref[d]
        stb = st.astype(BF16)
        vnew = ust - _dot(block_diag(w_ref[...], head_of_lane), stb)
        vnb = vnew.astype(BF16)
        o = _dot(block_diag(qd_ref[...], head_of_lane), stb) + _dot(block_diag(at_ref[...], head_of_col), vnb)
        s_ref[d] = st * decay + _dot_tn(block_diag(kd_ref[...], head_of_lane), vnb)
        for h in range(nh):
            o_ref[:, DN_HD * h:DN_HD * h + DN_HD] = o[c * h:c * h + c]


def _dnscan(uf, ub, wf, wb, qf, qb, kf, kb, af, ab, gc, nb, s, cl, bs):
    r = uf.shape[0]
    c = DN_CHUNK
    ncc, ncl = cl // c, s // c
    ns = ncc + ncl

    def fwd(b, t):
        return jnp.where(t < ncc, bs // c + b * ncc + t, b * ncl + t - ncc)

    def bwd(b, t):
        return jnp.where(t < ncc, bs // c + b * ncc + (ncc - 1 - t), b * ncl + (ncl - 1 - (t - ncc)))

    def specs(idx):
        blk = lambda w: pl.BlockSpec((c, w), lambda b, t: (idx(b, t), 0))
        return [blk(512), blk(512), blk(512), blk(512), blk(256), blk(128)]

    return pl.pallas_call(
        _dnscan_kernel,
        grid=(nb, ns),
        in_specs=specs(fwd) + specs(bwd),
        out_specs=[pl.BlockSpec((c, 512), lambda b, t: (fwd(b, t), 0)),
                   pl.BlockSpec((c, 512), lambda b, t: (bwd(b, t), 0))],
        out_shape=[jax.ShapeDtypeStruct((r, 512), F32)] * 2,
        scratch_shapes=[pltpu.VMEM((2, DN_HEADS * DN_HD, DN_HD), F32)],
        compiler_params=_cp(("arbitrary", "arbitrary")),
        name="dnscan",
    )(uf, wf, qf, kf, af, gc, ub, wb, qb, kb, ab, gc)


def _route(logits, bias_row):
    epg = N_EXPERTS // N_GROUPS
    lane = lax.broadcasted_iota(I32, logits.shape, 1).astype(F32)
    scores = jax.nn.sigmoid(logits)
    gsel = scores + bias_row
    big = float(1 << 20)

    def first_argmax(vals, mask):
        mx = jnp.max(jnp.where(mask, vals, -jnp.inf), axis=-1, keepdims=True)
        idx = jnp.min(jnp.where(mask & (vals == mx), lane, big), axis=-1, keepdims=True)
        return mx, idx

    best = None
    gidx = None
    for g in range(N_GROUPS):
        mask = (lane >= epg * g) & (lane < epg * (g + 1))
        t1, i1 = first_argmax(gsel, mask)
        t2, _ = first_argmax(gsel, mask & (lane != i1))
        gs = t1 + t2
        if g == 0:
            best, gidx = gs, jnp.zeros_like(i1)
        else:
            better = gs > best
            best = jnp.where(better, gs, best)
            gidx = jnp.where(better, float(g), gidx)
    mask = (lane >= epg * gidx) & (lane < epg * (gidx + 1.0))
    _, e1 = first_argmax(gsel, mask)
    _, e2 = first_argmax(gsel, mask & (lane != e1))
    w1 = jnp.sum(jnp.where(lane == e1, scores, 0.0), axis=-1, keepdims=True)
    w2 = jnp.sum(jnp.where(lane == e2, scores, 0.0), axis=-1, keepdims=True)
    tot = w1 + w2
    eidx = jnp.where(lane == 0, e1, jnp.where(lane == 1, e2, 0.0)).astype(I32)
    gates = jnp.where(lane == 0, w1 / tot, jnp.where(lane == 1, w2 / tot, 0.0))
    return eidx, gates


def _post_mixer(x, y, m, g2, rw, rb):
    xn = x + m[2:3] * y
    h2 = _rms(xn, g2) * (1.0 + m[4:5]) + m[3:4]
    eidx, gates = _route(_dot(h2, rw, HI), rb)
    return xn, h2, eidx, gates


def _mix0_kernel(x_ref, sc_ref, prev_ref, next_ref, of_ref, ob_ref, z_ref, mod_ref, cw_ref, on_ref, wo_ref,
                 g2_ref, rw_ref, rb_ref, xn_ref, h2_ref, ei_ref, ga_ref, *, ts, bs, s, cl):
    first, last = _seq_edges(pl.program_id(0), ts, bs, s, cl)
    ya = []
    for c in range(4):
        sl = slice(128 * c, 128 * c + 128)
        sg = slice(512 + 128 * c, 512 + 128 * c + 128)
        sx = slice(1024 + 128 * c, 1024 + 128 * c + 128)
        u = sc_ref[:, sg] * sc_ref[:, sx]
        pr = jnp.where(first, 0.0, prev_ref[7:8, sg] * prev_ref[7:8, sx])
        nx = jnp.where(last, 0.0, next_ref[0:1, sg] * next_ref[0:1, sx])
        up, un = _shifted(u, pr, nx)
        w = cw_ref[:, sl]
        ya.append((sc_ref[:, sl] * (up * w[0:1] + u * w[1:2] + un * w[2:3])).astype(BF16))
    yb = []
    for h in range(DN_HEADS):
        hs = slice(DN_HD * h, DN_HD * h + DN_HD)
        o = of_ref[:, hs] + ob_ref[:, hs]
        yb.append((_rms(o, on_ref[...]) * _silu(z_ref[:, hs])).astype(BF16))
    ycat = jnp.concatenate(ya + yb, axis=1)
    y = _dot(ycat, wo_ref[...])
    xn, h2, eidx, gates = _post_mixer(x_ref[...], y, mod_ref[0], g2_ref[...], rw_ref[...], rb_ref[...])
    xn_ref[...] = xn
    h2_ref[...] = h2.astype(BF16)
    ei_ref[...] = eidx
    ga_ref[...] = gates


def _mix0(x, sc, of, ob, z, mods, cw, on, wo, g2, rw, rb, ts, bs, s, cl, nb):
    r, d = x.shape
    kern = functools.partial(_mix0_kernel, ts=ts, bs=bs, s=s, cl=cl)
    mi = functools.partial(_mod_index, tm=ts, bs=bs, s=s, nb=nb)
    row = lambda w: pl.BlockSpec((ts, w), lambda i: (i, 0))
    full = lambda a: pl.BlockSpec(a.shape, lambda i: (0,) * a.ndim)
    return pl.pallas_call(
        kern,
        grid=(r // ts,),
        in_specs=[row(d), row(1536)] + _halo_specs(ts, 1536, r) + [
            row(512), row(512), row(512),
            pl.BlockSpec((1, N_MOD, d), lambda i: (mi(i), 0, 0)),
            full(cw), full(on), full(wo), full(g2), full(rw), full(rb)],
        out_specs=[row(d), row(d), row(128), row(128)],
        out_shape=[jax.ShapeDtypeStruct((r, d), F32), jax.ShapeDtypeStruct((r, d), BF16),
                   jax.ShapeDtypeStruct((r, 128), I32), jax.ShapeDtypeStruct((r, 128), F32)],
        compiler_params=_cp(("parallel",)),
        name="mix0",
    )(x, sc, sc, sc, of, ob, z, mods, cw, on, wo, g2, rw, rb)


def _gmm_kernel(te_ref, tf_ref, tv_ref, x_ref, wg_ref, wu_ref, wd_ref, y_ref, wgb, wub, wdb):
    t = pl.program_id(0)

    @pl.when(tf_ref[t] == 1)
    def _():
        wgb[...] = wg_ref[0, 0].astype(BF16)
        wub[...] = wu_ref[0, 0].astype(BF16)
        wdb[...] = wd_ref[0, 0].astype(BF16)

    @pl.when(tv_ref[t] == 1)
    def _():
        x = x_ref[...]
        a = (_silu(_dot(x, wgb[...])) * _dot(x, wub[...])).astype(BF16)
        y_ref[...] = _dot(a, wdb[...]).astype(BF16)

    @pl.when(tv_ref[t] == 0)
    def _():
        y_ref[...] = jnp.zeros_like(y_ref)


def _gmm(xs, w_gate, w_up, w_down, layer, tile_expert, tile_first, tile_valid, tmm):
    p, d = xs.shape
    de = w_gate.shape[-1]
    nt = p // tmm
    grid_spec = pltpu.PrefetchScalarGridSpec(
        num_scalar_prefetch=3,
        grid=(nt,),
        in_specs=[pl.BlockSpec((tmm, d), lambda t, te, tf, tv: (t, 0)),
                  pl.BlockSpec((1, 1, d, de), lambda t, te, tf, tv: (layer, te[t], 0, 0)),
                  pl.BlockSpec((1, 1, d, de), lambda t, te, tf, tv: (layer, te[t], 0, 0)),
                  pl.BlockSpec((1, 1, de, d), lambda t, te, tf, tv: (layer, te[t], 0, 0))],
        out_specs=pl.BlockSpec((tmm, d), lambda t, te, tf, tv: (t, 0)),
        scratch_shapes=[pltpu.VMEM((d, de), BF16), pltpu.VMEM((d, de), BF16), pltpu.VMEM((de, d), BF16)],
    )
    return pl.pallas_call(
        _gmm_kernel,
        grid_spec=grid_spec,
        out_shape=jax.ShapeDtypeStruct((p, d), BF16),
        compiler_params=_cp(("arbitrary",)),
        name="gmm",
    )(tile_expert, tile_first, tile_valid, xs, w_gate, w_up, w_down)


def _moe(h2, eidx2, w_gate, w_up, w_down, layer, tmm):
    t_tok = h2.shape[0]
    n = 2 * t_tok
    e_flat = eidx2.reshape(-1)
    onehot = (e_flat[:, None] == jnp.arange(N_EXPERTS, dtype=I32)[None, :]).astype(I32)
    csum = jnp.cumsum(onehot, axis=0)
    counts = csum[-1]
    ptiles = (counts + tmm - 1) // tmm
    tile_end = jnp.cumsum(ptiles)
    dest = jnp.sum(onehot * (csum - 1 + ((tile_end - ptiles) * tmm)[None, :]), axis=1)
    nt = n // tmm + N_EXPERTS
    src = (jnp.arange(nt * tmm, dtype=I32) % t_tok).at[dest].set(jnp.arange(n, dtype=I32) // 2)
    tid = jnp.arange(nt, dtype=I32)
    tile_valid = (tid < tile_end[-1]).astype(I32)
    te = jnp.minimum(jnp.sum((tile_end[None, :] <= tid[:, None]).astype(I32), axis=1), N_EXPERTS - 1)
    last_used = jnp.max(jnp.where(tile_valid == 1, te, 0))
    te = jnp.where(tile_valid == 1, te, last_used)
    tile_first = jnp.concatenate([jnp.ones((1,), I32), (te[1:] != te[:-1]).astype(I32)])
    xs = jnp.take(jnp.pad(h2, ((0, nt * tmm - t_tok), (0, 0))), src, axis=0, mode="clip")
    ys = _gmm(xs, w_gate, w_up, w_down, layer, te, tile_first, tile_valid, tmm)
    d2 = dest.reshape(t_tok, 2)
    return jnp.take(ys, d2[:, 0], axis=0, mode="clip"), jnp.take(ys, d2[:, 1], axis=0, mode="clip")


def _moe_combine(x, y1, y2, gates, m5):
    g = gates
    f = g[:, 0:1] * y1.astype(F32) + g[:, 1:2] * y2.astype(F32)
    return x + m5 * f


def _rope(x, cos, sin):
    n = x.shape[1]
    lane = lax.broadcasted_iota(I32, x.shape, 1)
    sw = jnp.where(lane % 32 < 16, pltpu.roll(x, n - 16, 1), pltpu.roll(x, 16, 1))
    reps = n // 128
    if reps > 1:
        cos = jnp.concatenate([cos] * reps, axis=1)
        sin = jnp.concatenate([sin] * reps, axis=1)
    return x * cos + sw * sin


def _inproj1_kernel(x_ref, y1_ref, y2_ref, ga_ref, m0_ref, m1_ref, g_ref, w_ref, cos_ref, sin_ref,
                    x1_ref, cq_ref, ckv_ref, dq_ref, dk_ref, dv_ref, *, tm, bs):
    x1 = _moe_combine(x_ref[...], y1_ref[...], y2_ref[...], ga_ref[...], m0_ref[0][5:6])
    x1_ref[...] = x1
    m = m1_ref[0]
    h = (_rms(x1, g_ref[...]) * (1.0 + m[1:2]) + m[0:1]).astype(BF16)
    in_lat = pl.program_id(0) * tm < bs
    cos, sin = cos_ref[...], sin_ref[...]
    scale = HEAD_DIM ** -0.5
    cq = _dot(h, w_ref[:, 0:512])
    cq_ref[...] = (jnp.where(in_lat, _rope(cq, cos, sin), cq) * scale).astype(BF16)
    ck = _dot(h, w_ref[:, 512:640])
    ckv_ref[:, 0:128] = jnp.where(in_lat, _rope(ck, cos, sin), ck).astype(BF16)
    ckv_ref[:, 128:256] = _dot(h, w_ref[:, 640:768]).astype(BF16)
    dq_ref[...] = (_dot(h, w_ref[:, 768:1280]) * scale).astype(BF16)
    dk_ref[...] = _dot(h, w_ref[:, 1280:1792]).astype(BF16)
    dv_ref[...] = _dot(h, w_ref[:, 1792:2304]).astype(BF16)


def _inproj1(x, y1, y2, gates, mods0, mods1, g, w, cos, sin, tm, bs, s, nb):
    r, d = x.shape
    kern = functools.partial(_inproj1_kernel, tm=tm, bs=bs)
    mi = functools.partial(_mod_index, tm=tm, bs=bs, s=s, nb=nb)
    row = lambda wd: pl.BlockSpec((tm, wd), lambda i: (i, 0))
    modspec = pl.BlockSpec((1, N_MOD, d), lambda i: (mi(i), 0, 0))
    tab = pl.BlockSpec((tm, 128), lambda i: (jnp.where(i * tm < bs, (i * tm % s) // tm, 0), 0))
    shp = lambda wd, dt: jax.ShapeDtypeStruct((r, wd), dt)
    return pl.pallas_call(
        kern,
        grid=(r // tm,),
        in_specs=[row(d), row(d), row(d), row(128), modspec, modspec,
                  pl.BlockSpec((1, d), lambda i: (0, 0)), pl.BlockSpec(w.shape, lambda i: (0, 0)), tab, tab],
        out_specs=[row(d), row(512), row(256), row(512), row(512), row(512)],
        out_shape=[shp(d, F32), shp(512, BF16), shp(256, BF16), shp(512, BF16), shp(512, BF16), shp(512, BF16)],
        compiler_params=_cp(("parallel",)),
        name="inproj1",
    )(x, y1, y2, gates, mods0, mods1, g, w, cos, sin)


def _swa_kernel(q_ref, kp_ref, kc_ref, kn_ref, kx_ref, sink_ref, o_ref, *, nblk):
    i = pl.program_id(1)
    wb = SWA_BLOCK
    kv = jnp.concatenate([kp_ref[...], kc_ref[...], kn_ref[...]], axis=0)
    kk, vv = kv[:, 0:128], kv[:, 128:256]
    kx, vx = kx_ref[:, 0:128], kx_ref[:, 128:256]
    a_i = lax.broadcasted_iota(I32, (wb, 3 * wb), 0)
    c_i = lax.broadcasted_iota(I32, (wb, 3 * wb), 1)
    ok = (c_i >= a_i) & (c_i <= a_i + 2 * SWA_WINDOW)
    lo = jnp.where(i > 0, 0, wb)
    hi = jnp.where(i < nblk - 1, 3 * wb, 2 * wb)
    ok = ok & (c_i >= lo) & (c_i < hi)
    half = lax.broadcasted_iota(I32, (1, 128), 1) // HEAD_DIM
    zero = jnp.zeros((), BF16)
    vm = [jnp.where(half == a, vv, zero) for a in range(2)]
    vxm = [jnp.where(half == a, vx, zero) for a in range(2)]
    sink = sink_ref[...]
    for g in range(4):
        q2 = q_ref[:, 128 * g:128 * g + 128]
        acc = None
        for a in range(2):
            qa = jnp.where(half == a, q2, zero)
            s_loc = jnp.where(ok, _dot_nt(qa, kk), NEG)
            s_ctx = _dot_nt(qa, kx)
            sk = sink[0:1, 2 * g + a:2 * g + a + 1]
            m = jnp.maximum(jnp.maximum(jnp.max(s_loc, axis=-1, keepdims=True),
                                        jnp.max(s_ctx, axis=-1, keepdims=True)), sk)
            p_loc = jnp.exp(s_loc - m)
            p_ctx = jnp.exp(s_ctx - m)
            den = (jnp.sum(p_loc, axis=-1, keepdims=True) + jnp.sum(p_ctx, axis=-1, keepdims=True)
                   + jnp.exp(sk - m))
            o = (_dot(p_loc.astype(BF16), vm[a]) + _dot(p_ctx.astype(BF16), vxm[a])) / den
            acc = o if acc is None else acc + o
        o_ref[:, 128 * g:128 * g + 128] = acc.astype(BF16)


def _swa(cq, ckv, sink_row, nb, s, cl, bs):
    wb = SWA_BLOCK
    nblk = s // wb
    kern = functools.partial(_swa_kernel, nblk=nblk)
    return pl.pallas_call(
        kern,
        grid=(nb, nblk),
        in_specs=[pl.BlockSpec((wb, 512), lambda b, i: (b * nblk + i, 0)),
                  pl.BlockSpec((wb, 256), lambda b, i: (b * nblk + jnp.maximum(i - 1, 0), 0)),
                  pl.BlockSpec((wb, 256), lambda b, i: (b * nblk + i, 0)),
                  pl.BlockSpec((wb, 256), lambda b, i: (b * nblk + jnp.minimum(i + 1, nblk - 1), 0)),
                  pl.BlockSpec((cl, 256), lambda b, i: (bs // cl + b, 0)),
                  pl.BlockSpec((1, 128), lambda b, i: (0, 0))],
        out_specs=pl.BlockSpec((wb, 512), lambda b, i: (b * nblk + i, 0)),
        out_shape=jax.ShapeDtypeStruct((bs, 512), BF16),
        compiler_params=_cp(("parallel", "parallel")),
        name="swa",
    )(cq, ckv, ckv, ckv, ckv, sink_row)


def _na_kernel(q_ref, k_ref, v_ref, kx_ref, vx_ref, bias_ref, o_ref, *, rows):
    half = lax.broadcasted_iota(I32, (1, 128), 1) // HEAD_DIM
    zero = jnp.zeros((), BF16)
    kx, vx = kx_ref[...], vx_ref[...]
    vxm = [jnp.where(half == a, vx, zero) for a in range(2)]
    span = NA_KH * GRID_W

    def body(r, carry):
        rs = jnp.clip(r - NA_KH // 2, 0, rows - NA_KH)
        off = rs - r + NA_KH - 1
        q0 = pl.multiple_of(r * GRID_W, GRID_W)
        k0 = pl.multiple_of(rs * GRID_W, GRID_W)
        q2 = q_ref[pl.ds(q0, GRID_W), :]
        kk = k_ref[pl.ds(k0, span), :]
        vv = v_ref[pl.ds(k0, span), :]
        acc = None
        for a in range(2):
            qa = jnp.where(half == a, q2, zero)
            s_loc = _dot_nt(qa, kk) + bias_ref[0, off, a]
            s_ctx = _dot_nt(qa, kx)
            m = jnp.maximum(jnp.max(s_loc, axis=-1, keepdims=True), jnp.max(s_ctx, axis=-1, keepdims=True))
            p_loc = jnp.exp(s_loc - m)
            p_ctx = jnp.exp(s_ctx - m)
            den = jnp.sum(p_loc, axis=-1, keepdims=True) + jnp.sum(p_ctx, axis=-1, keepdims=True)
            va = jnp.where(half == a, vv, zero)
            o = (_dot(p_loc.astype(BF16), va) + _dot(p_ctx.astype(BF16), vxm[a])) / den
            acc = o if acc is None else acc + o
        o_ref[pl.ds(q0, GRID_W), :] = acc.astype(BF16)
        return carry

    lax.fori_loop(0, rows, body, 0)


def _na(dq, dk, dv, bias, nb, s, cl, bs):
    rows = s // GRID_W
    kern = functools.partial(_na_kernel, rows=rows)
    seq = pl.BlockSpec((s, 128), lambda b, p: (b, p))
    ctx = pl.BlockSpec((cl, 128), lambda b, p: (bs // cl + b, p))
    return pl.pallas_call(
        kern,
        grid=(nb, NA_HEADS // 2),
        in_specs=[seq, seq, seq, ctx, ctx,
                  pl.BlockSpec((1, NA_KH, 2, GRID_W, NA_KH * GRID_W), lambda b, p: (p, 0, 0, 0, 0))],
        out_specs=seq,
        out_shape=jax.ShapeDtypeStruct((bs, 512), BF16),
        compiler_params=_cp(("parallel", "parallel")),
        name="na",
    )(dq, dk, dv, dk, dv, bias)


def _na_bias_table(rpb):
    c = np.arange(GRID_W)
    qs = np.clip(c - NA_KW // 2, 0, GRID_W - NA_KW)
    kc = np.arange(GRID_W)
    ok = (kc[None, :] >= qs[:, None]) & (kc[None, :] < qs[:, None] + NA_KW)
    dc = np.clip(kc[None, :] - c[:, None] + NA_KW - 1, 0, 2 * NA_KW - 2)
    sel = (np.arange(2 * NA_KW - 1)[:, None, None] == dc[None]).astype(np.float32)
    cols = jnp.einsum("hab,bck->hack", rpb.astype(F32), sel, precision=HI)
    cols = jnp.where(ok[None, None], cols, NEG)
    t = jnp.stack([cols[:, off:off + NA_KH] for off in range(NA_KH)], axis=1)
    t = jnp.transpose(t, (0, 1, 3, 2, 4)).reshape(NA_HEADS, NA_KH, GRID_W, NA_KH * GRID_W)
    t = t.reshape(NA_HEADS // 2, 2, NA_KH, GRID_W, NA_KH * GRID_W)
    return jnp.transpose(t, (0, 2, 1, 3, 4))


def _mix1_kernel(x_ref, oc_ref, od_ref, mod_ref, wo_ref, g2_ref, rw_ref, rb_ref,
                 xn_ref, h2_ref, ei_ref, ga_ref):
    y = _dot(oc_ref[...], wo_ref[0:512, :]) + _dot(od_ref[...], wo_ref[512:1024, :])
    xn, h2, eidx, gates = _post_mixer(x_ref[...], y, mod_ref[0], g2_ref[...], rw_ref[...], rb_ref[...])
    xn_ref[...] = xn
    h2_ref[...] = h2.astype(BF16)
    ei_ref[...] = eidx
    ga_ref[...] = gates


def _mix1(x, oc, od, mods, wo, g2, rw, rb, tm, bs, s):
    d = x.shape[1]
    row = lambda w: pl.BlockSpec((tm, w), lambda i: (i, 0))
    full = lambda a: pl.BlockSpec(a.shape, lambda i: (0,) * a.ndim)
    return pl.pallas_call(
        _mix1_kernel,
        grid=(bs // tm,),
        in_specs=[row(d), row(512), row(512), pl.BlockSpec((1, N_MOD, d), lambda i: (i * tm // s, 0, 0)),
                  full(wo), full(g2), full(rw), full(rb)],
        out_specs=[row(d), row(d), row(128), row(128)],
        out_shape=[jax.ShapeDtypeStruct((bs, d), F32), jax.ShapeDtypeStruct((bs, d), BF16),
                   jax.ShapeDtypeStruct((bs, 128), I32), jax.ShapeDtypeStruct((bs, 128), F32)],
        compiler_params=_cp(("parallel",)),
        name="mix1",
    )(x, oc, od, mods, wo, g2, rw, rb)


def _final_kernel(x_ref, y1_ref, y2_ref, ga_ref, mod_ref, g_ref, o_ref):
    x = _moe_combine(x_ref[...], y1_ref[...], y2_ref[...], ga_ref[...], mod_ref[0][5:6])
    o_ref[...] = _rms(x, g_ref[...])


def _final(x, y1, y2, gates, mods, g, tm, s):
    r, d = x.shape
    row = lambda w: pl.BlockSpec((tm, w), lambda i: (i, 0))
    return pl.pallas_call(
        _final_kernel,
        grid=(r // tm,),
        in_specs=[row(d), row(d), row(d), row(128), pl.BlockSpec((1, N_MOD, d), lambda i: (i * tm // s, 0, 0)),
                  pl.BlockSpec((1, d), lambda i: (0, 0))],
        out_specs=row(d),
        out_shape=jax.ShapeDtypeStruct((r, d), F32),
        compiler_params=_cp(("parallel",)),
        name="final",
    )(x, y1, y2, gates, mods, g)


def _rope_tables(s):
    nf = HEAD_DIM // 4
    t = np.arange(s)
    inv = ROPE_THETA ** (-np.arange(nf, dtype=np.float64) / nf)
    ar = (t // GRID_W)[:, None] * inv
    ac = (t % GRID_W)[:, None] * inv
    cos = np.concatenate([np.cos(ar), np.cos(ar), np.cos(ac), np.cos(ac)], axis=1)
    sin = np.concatenate([-np.sin(ar), np.sin(ar), -np.sin(ac), np.sin(ac)], axis=1)
    return (jnp.asarray(np.concatenate([cos, cos], axis=1), F32),
            jnp.asarray(np.concatenate([sin, sin], axis=1), F32))


def kernel(x, c, ctx, c_ctx, ada_w, ada_b, norm1_g, norm2_g, ev_w_in, ev_w_out, sc_conv_w, dn_conv_w, dn_a_log, dn_dt_bias, dn_onorm_g, od_w_in, od_w_out, swa_sink, na_rpb, router_w, router_b, moe_w_gate, moe_w_up, moe_w_down, final_g):
    nb, s, d = x.shape
    cl = ctx.shape[1]
    bs = nb * s
    tm = 512
    ts = 256
    tmm = 512
    assert d == 1024 and s % tm == 0 and (nb * cl) % tm == 0 and cl % ts == 0 and s % ts == 0
    assert s // GRID_W >= NA_KH and bs % cl == 0 and nb + 1 <= 8

    xf = jnp.concatenate([x.reshape(bs, d), ctx.reshape(nb * cl, d)], axis=0)
    cc = jnp.zeros((8, d), F32).at[:nb].set(c).at[nb].set(c_ctx)
    mods = _ada(cc, ada_w, ada_b).reshape(ada_w.shape[0], 8, N_MOD, d)
    rw = jnp.pad(router_w, ((0, 0), (0, 128 - N_EXPERTS)))
    rb = jnp.pad(router_b, (0, 128 - N_EXPERTS)).reshape(1, 128)
    row = lambda v: v.reshape(1, -1)

    w_in0 = jnp.pad(ev_w_in[0], ((0, 0), (0, 3712 - ev_w_in.shape[-1]))).astype(BF16)
    sc, qkv, z, bg = _inproj0(xf, mods[0], row(norm1_g[0]), w_in0, tm, bs, s, nb)
    pad16 = lambda v: jnp.pad(v.reshape(-1), (8, 128 - 16)).reshape(1, 128)
    q, k, v, bga = _dnprep(qkv, bg, dn_conv_w[0], pad16(dn_a_log[0]), pad16(dn_dt_bias[0]), ts, bs, s, cl)
    uf, ub, wf, wb, qf, qb, kf, kb, af, ab, gc = _dnchunk(q, k, v, bga)
    of, ob = _dnscan(uf, ub, wf, wb, qf, qb, kf, kb, af, ab, gc, nb, s, cl, bs)
    x0, h2, ei, ga = _mix0(xf, sc, of, ob, z, mods[0], sc_conv_w[0], row(dn_onorm_g[0]),
                           ev_w_out[0].astype(BF16), row(norm2_g[0]), rw, rb, ts, bs, s, cl, nb)
    y1, y2 = _moe(h2, ei[:, 0:2], moe_w_gate, moe_w_up, moe_w_down, 0, tmm)

    perm = np.concatenate([np.arange(HEAD_DIM) + HEAD_DIM * (g + 4 * a) for g in range(4) for a in range(2)])
    w1 = od_w_in[0]
    w_in1 = jnp.concatenate([w1[:, 0:512][:, perm], w1[:, 512:]], axis=1).astype(BF16)
    wo1 = od_w_out[0]
    w_out1 = jnp.concatenate([wo1[0:512][perm], wo1[512:]], axis=0).astype(BF16)
    sink_row = jnp.pad(swa_sink[0][np.array([g + 4 * a for g in range(4) for a in range(2)])],
                       (0, 128 - SWA_HEADS)).reshape(1, 128)
    cos, sin = _rope_tables(s)
    x1, cq, ckv, dq, dk, dv = _inproj1(x0, y1, y2, ga, mods[0], mods[1], row(norm1_g[1]), w_in1, cos, sin,
                                       tm, bs, s, nb)
    oc = _swa(cq, ckv, sink_row, nb, s, cl, bs)
    od = _na(dq, dk, dv, _na_bias_table(na_rpb[0]), nb, s, cl, bs)
    x2, h2, ei, ga = _mix1(x1, oc, od, mods[1], w_out1, row(norm2_g[1]), rw, rb, tm, bs, s)
    y1, y2 = _moe(h2, ei[:, 0:2], moe_w_gate, moe_w_up, moe_w_down, 1, tmm)
    out = _final(x2, y1, y2, ga, mods[1], row(final_g), tm, s)
    return out.reshape(nb, s, d)
```

```python
import functools
import math

import numpy as np
import jax
import jax.numpy as jnp
from jax import lax
from jax.experimental import pallas as pl
from jax.experimental.pallas import tpu as pltpu

F32 = jnp.float32
BF16 = jnp.bfloat16
I32 = jnp.int32
HI = lax.Precision.HIGHEST

EPS = 1e-6
N_MOD = 6
GRID_W = 64
HEAD_DIM = 64
DN_HEADS = 4
DN_HD = 128
DN_CHUNK = 64
SWA_HEADS = 8
SWA_KV = 2
SWA_BLOCK = 128
SWA_WINDOW = 128
NA_HEADS = 8
NA_KH = 8
NA_KW = 16
ROPE_THETA = 10000.0
N_EXPERTS = 16
N_GROUPS = 4
NEG = -1e30
VMEM_LIMIT = 56 * 1024 * 1024


def _cp(sem, vmem=VMEM_LIMIT):
    return pltpu.CompilerParams(dimension_semantics=sem, vmem_limit_bytes=vmem)


def _dot(a, b, precision=None):
    return jnp.dot(a, b, preferred_element_type=F32, precision=precision)


def _dot_nt(a, b, precision=None):
    return lax.dot_general(a, b, (((1,), (1,)), ((), ())), preferred_element_type=F32, precision=precision)


def _dot_tn(a, b, precision=None):
    return lax.dot_general(a, b, (((0,), (0,)), ((), ())), preferred_element_type=F32, precision=precision)


def _silu(x):
    return x * jax.nn.sigmoid(x)


def _rms(x, g):
    return x * lax.rsqrt(jnp.mean(x * x, axis=-1, keepdims=True) + EPS) * g


def _ada_kernel(cc_ref, w_ref, b_ref, o_ref):
    a = _silu(cc_ref[...])
    o_ref[0] = _dot(a, w_ref[0], HI) + b_ref[0]


def _ada(cc, ada_w, ada_b):
    depth, d, n = ada_w.shape
    tn = 1536
    return pl.pallas_call(
        _ada_kernel,
        grid=(depth, n // tn),
        in_specs=[pl.BlockSpec((8, d), lambda l, j: (0, 0)),
                  pl.BlockSpec((1, d, tn), lambda l, j: (l, 0, j)),
                  pl.BlockSpec((1, 1, tn), lambda l, j: (l, 0, j))],
        out_specs=pl.BlockSpec((1, 8, tn), lambda l, j: (l, 0, j)),
        out_shape=jax.ShapeDtypeStruct((depth, 8, n), F32),
        compiler_params=_cp(("parallel", "parallel")),
        name="ada",
    )(cc, ada_w, ada_b.reshape(depth, 1, n))


def _mod_index(i, tm, bs, s, nb):
    row0 = i * tm
    return jnp.where(row0 < bs, row0 // s, nb)


def _inproj0_kernel(x_ref, mod_ref, g_ref, w_ref, sc_ref, qkv_ref, z_ref, bg_ref):
    m = mod_ref[0]
    h = (_rms(x_ref[...], g_ref[...]) * (1.0 + m[1:2]) + m[0:1]).astype(BF16)
    sc_ref[...] = _dot(h, w_ref[:, 0:1536])
    qkv_ref[...] = _dot(h, w_ref[:, 1536:3072])
    z_ref[...] = _dot(h, w_ref[:, 3072:3584])
    bg_ref[...] = _dot(h, w_ref[:, 3584:3712])


def _inproj0(x, mods, g, w, tm, bs, s, nb):
    r, d = x.shape
    mi = functools.partial(_mod_index, tm=tm, bs=bs, s=s, nb=nb)
    return pl.pallas_call(
        _inproj0_kernel,
        grid=(r // tm,),
        in_specs=[pl.BlockSpec((tm, d), lambda i: (i, 0)),
                  pl.BlockSpec((1, N_MOD, d), lambda i: (mi(i), 0, 0)),
                  pl.BlockSpec((1, d), lambda i: (0, 0)),
                  pl.BlockSpec(w.shape, lambda i: (0, 0))],
        out_specs=[pl.BlockSpec((tm, 1536), lambda i: (i, 0)),
                   pl.BlockSpec((tm, 1536), lambda i: (i, 0)),
                   pl.BlockSpec((tm, 512), lambda i: (i, 0)),
                   pl.BlockSpec((tm, 128), lambda i: (i, 0))],
        out_shape=[jax.ShapeDtypeStruct((r, 1536), F32), jax.ShapeDtypeStruct((r, 1536), F32),
                   jax.ShapeDtypeStruct((r, 512), F32), jax.ShapeDtypeStruct((r, 128), F32)],
        compiler_params=_cp(("parallel",)),
        name="inproj0",
    )(x, mods, g, w)


def _seq_edges(i, ts, bs, s, cl):
    row0 = i * ts
    in_lat = row0 < bs
    r_in = jnp.where(in_lat, row0 % s, (row0 - bs) % cl)
    seqlen = jnp.where(in_lat, s, cl)
    return r_in == 0, r_in + ts == seqlen


def _shifted(x, prev_row, next_row):
    n = x.shape[0]
    rows = lax.broadcasted_iota(I32, x.shape, 0)
    xp = jnp.where(rows == 0, prev_row, pltpu.roll(x, 1, 0))
    xn = jnp.where(rows == n - 1, next_row, pltpu.roll(x, n - 1, 0))
    return xp, xn


def _halo_specs(ts, width, r):
    nb8 = r // 8
    k = ts // 8
    return [pl.BlockSpec((8, width), lambda i: (jnp.maximum(i * k - 1, 0), 0)),
            pl.BlockSpec((8, width), lambda i: (jnp.minimum((i + 1) * k, nb8 - 1), 0))]


def _dnprep_kernel(x_ref, prev_ref, next_ref, bg_ref, cw_ref, alog_ref, dt_ref,
                   q_ref, k_ref, v_ref, bga_ref, *, ts, bs, s, cl):
    first, last = _seq_edges(pl.program_id(0), ts, bs, s, cl)
    for c in range(12):
        sl = slice(128 * c, 128 * c + 128)
        x = x_ref[:, sl]
        pr = jnp.where(first, 0.0, prev_ref[7:8, sl])
        nx = jnp.where(last, 0.0, next_ref[0:1, sl])
        xp, xn = _shifted(x, pr, nx)
        w = cw_ref[:, sl]
        y = _silu(xp * w[0:1] + x * w[1:2] + xn * w[2:3])
        hs = slice(128 * (c % 4), 128 * (c % 4) + 128)
        if c < 8:
            y = y * lax.rsqrt(jnp.sum(y * y, axis=-1, keepdims=True) + EPS)
        if c < 4:
            q_ref[:, hs] = y * DN_HD ** -0.5
        elif c < 8:
            k_ref[:, hs] = y
        else:
            v_ref[:, hs] = y
    b = bg_ref[...]
    cols = lax.broadcasted_iota(I32, b.shape, 1)
    beta = jax.nn.sigmoid(b)
    t = b + dt_ref[...]
    softplus = jnp.maximum(t, 0.0) + jnp.log1p(jnp.exp(-jnp.abs(t)))
    g = -jnp.exp(alog_ref[...]) * softplus
    bga_ref[...] = jnp.where(cols < 8, beta, jnp.where(cols < 16, g, 0.0))


def _dnprep(qkv, bg, cw, alog_row, dt_row, ts, bs, s, cl):
    r = qkv.shape[0]
    kern = functools.partial(_dnprep_kernel, ts=ts, bs=bs, s=s, cl=cl)
    return pl.pallas_call(
        kern,
        grid=(r // ts,),
        in_specs=[pl.BlockSpec((ts, 1536), lambda i: (i, 0))] + _halo_specs(ts, 1536, r) + [
            pl.BlockSpec((ts, 128), lambda i: (i, 0)),
            pl.BlockSpec((3, 1536), lambda i: (0, 0)),
            pl.BlockSpec((1, 128), lambda i: (0, 0)),
            pl.BlockSpec((1, 128), lambda i: (0, 0))],
        out_specs=[pl.BlockSpec((ts, 512), lambda i: (i, 0))] * 3 + [pl.BlockSpec((ts, 128), lambda i: (i, 0))],
        out_shape=[jax.ShapeDtypeStruct((r, 512), F32)] * 3 + [jax.ShapeDtypeStruct((r, 128), F32)],
        compiler_params=_cp(("parallel",)),
        name="dnprep",
    )(qkv, qkv, qkv, bg, cw, alog_row, dt_row)


def _dnchunk_kernel(q_ref, k_ref, v_ref, bg_ref,
                    uf_ref, ub_ref, wf_ref, wb_ref, qf_ref, qb_ref, kf_ref, kb_ref, af_ref, ab_ref, gc_ref, *, nchunks):
    outs = ((uf_ref, wf_ref, qf_ref, kf_ref, af_ref), (ub_ref, wb_ref, qb_ref, kb_ref, ab_ref))
    n = DN_CHUNK * DN_HEADS
    chains = []
    for cc in range(nchunks):
        chains += _dnchunk_setup(slice(cc * DN_CHUNK, (cc + 1) * DN_CHUNK), q_ref, k_ref, v_ref, bg_ref, gc_ref)
    for ch in chains:
        ch["tm"] = ch["nmat"]
        nb16 = ch["nmat"].astype(BF16)
        ch["npow"] = _dot(nb16, nb16)
    for _ in range(4):
        for ch in chains:
            nb16 = ch["npow"].astype(BF16)
            ch["both"] = _dot(jnp.concatenate([nb16, ch["tm"].astype(BF16)], axis=0), nb16)
        for ch in chains:
            ch["tm"] = ch["tm"] + ch["npow"] + ch["both"][n:2 * n]
            ch["npow"] = ch["both"][0:n]
    for ch in chains:
        ch["both"] = _dot(ch["tm"].astype(BF16), ch["npow"].astype(BF16))
    for ch in chains:
        ch["tm"] = ch["tm"] + ch["npow"] + ch["both"]
    for ch in chains:
        ch["uw"] = ch["rhs"] + _dot(ch["tm"].astype(BF16), ch["rhs"].astype(BF16))
    for ch in chains:
        u_ref, w_ref, qd_ref, kd_ref, at_ref = outs[ch["d"]]
        rows, uw, c = ch["rows"], ch["uw"], DN_CHUNK
        for h in range(DN_HEADS):
            hs = slice(DN_HD * h, DN_HD * h + DN_HD)
            rs = slice(c * h, c * h + c)
            u_ref[rows, hs] = uw[rs, 0:DN_HD]
            w_ref[rows, hs] = uw[rs, DN_HD:2 * DN_HD].astype(BF16)
            qd_ref[rows, hs] = ch["qd"][rs]
            kd_ref[rows, hs] = ch["kd"][rs]
        att = ch["att"]
        at_ref[rows] = (att[0:c] + att[c:2 * c] + att[2 * c:3 * c] + att[3 * c:4 * c]).astype(BF16)


def _dnchunk_setup(rows, q_ref, k_ref, v_ref, bg_ref, gc_ref):
    c, nh = DN_CHUNK, DN_HEADS
    n = c * nh
    bg = bg_ref[rows]
    i64 = lax.broadcasted_iota(I32, (c, c), 0)
    j64 = lax.broadcasted_iota(I32, (c, c), 1)
    cols = lax.broadcasted_iota(I32, bg.shape, 1)
    gcf = _dot((i64 >= j64).astype(F32), bg, HI)
    gcb = _dot((i64 <= j64).astype(F32), bg, HI)
    gc = jnp.where(cols >= 12, gcb, gcf)
    gc_ref[rows] = gc
    gct = gc.T
    ii = lax.broadcasted_iota(I32, (n, n), 0)
    jj = lax.broadcasted_iota(I32, (n, n), 1)
    same = (ii // c) == (jj // c)

    def stack(ref):
        return jnp.concatenate([ref[rows, DN_HD * h:DN_HD * h + DN_HD] for h in range(nh)], axis=0)

    def col_bcast(arr, r0, r1, col0):
        return jnp.concatenate([jnp.broadcast_to(arr[r0:r1, col0 + h:col0 + h + 1], (c, n)) for h in range(nh)],
                               axis=0)

    kst, qst, vst = stack(k_ref), stack(q_ref), stack(v_ref)
    kb = kst.astype(BF16)
    kq = _dot_nt(jnp.concatenate([kb, qst.astype(BF16)], axis=0), kb)
    chains = []
    for d in range(2):
        incl = same & ((ii >= jj) if d == 0 else (ii <= jj))
        strict = same & ((ii > jj) if d == 0 else (ii < jj))
        last = c - 1 if d == 0 else 0
        beta = col_bcast(bg, 0, c, 4 * d)
        gcol = col_bcast(gc, 0, c, 8 + 4 * d)
        glast = col_bcast(gc, last, last + 1, 8 + 4 * d)
        grow = jnp.concatenate([gct[8 + 4 * d + h:9 + 4 * d + h, :] for h in range(nh)], axis=1)
        decay = jnp.exp(jnp.where(incl, gcol - grow, NEG))
        nmat = jnp.where(strict, -(beta * kq[0:n] * decay), 0.0)
        eg = jnp.exp(gcol)
        b1, e1 = beta[:, 0:DN_HD], eg[:, 0:DN_HD]
        chains.append(dict(
            d=d, rows=rows, nmat=nmat,
            rhs=jnp.concatenate([b1 * vst, (b1 * e1) * kst], axis=1),
            qd=(qst * e1).astype(BF16),
            kd=(kst * jnp.exp(glast - gcol)[:, 0:DN_HD]).astype(BF16),
            att=kq[n:2 * n] * decay))
    return chains


def _dnchunk(q, k, v, bga):
    r = q.shape[0]
    nchunks = 2
    c = DN_CHUNK * nchunks
    row = lambda w: pl.BlockSpec((c, w), lambda i: (i, 0))
    shp = lambda w, dt: jax.ShapeDtypeStruct((r, w), dt)
    return pl.pallas_call(
        functools.partial(_dnchunk_kernel, nchunks=nchunks),
        grid=(r // c,),
        in_specs=[row(512), row(512), row(512), row(128)],
        out_specs=[row(512)] * 8 + [row(256), row(256), row(128)],
        out_shape=[shp(512, F32)] * 2 + [shp(512, BF16)] * 6 + [shp(256, BF16)] * 2 + [shp(128, F32)],
        compiler_params=_cp(("parallel",)),
        name="dnchunk",
    )(q, k, v, bga)


def _dnscan_kernel(uf_ref, wf_ref, qf_ref, kf_ref, af_ref, gf_ref,
                   ub_ref, wb_ref, qb_ref, kb_ref, ab_ref, gb_ref,
                   of_ref, ob_ref, s_ref, *, nsub):
    @pl.when(pl.program_id(1) == 0)
    def _():
        s_ref[...] = jnp.zeros_like(s_ref)

    c, nh = DN_CHUNK, DN_HEADS
    head_of_lane = lax.broadcasted_iota(I32, (1, nh * DN_HD), 1) // DN_HD
    head_of_col = lax.broadcasted_iota(I32, (1, nh * c), 1) // c
    zero = jnp.zeros((), BF16)

    def block_diag(tile, head_ids):
        return jnp.concatenate([jnp.where(head_ids == h, tile, zero) for h in range(nh)], axis=0)

    dirs = ((uf_ref, wf_ref, qf_ref, kf_ref, af_ref, gf_ref, of_ref, c - 1),
            (ub_ref, wb_ref, qb_ref, kb_ref, ab_ref, gb_ref, ob_ref, 0))
    states = [s_ref[0], s_ref[1]]
    for sub in range(nsub):
        work = []
        for d, (u_ref, w_ref, qd_ref, kd_ref, at_ref, g_ref, o_ref, last) in enumerate(dirs):
            k = sub if d == 0 else nsub - 1 - sub
            rows = slice(c * k, c * k + c)
            g = g_ref[rows]
            decay = jnp.concatenate(
                [jnp.broadcast_to(jnp.exp(g[last:last + 1, 8 + 4 * d + h:9 + 4 * d + h]), (DN_HD, DN_HD))
                 for h in range(nh)], axis=0)
            ust = jnp.concatenate([u_ref[rows, DN_HD * h:DN_HD * h + DN_HD] for h in range(nh)], axis=0)
            stb = states[d].astype(BF16)
            vnew = ust - _dot(block_diag(w_ref[rows], head_of_lane), stb)
            work.append((rows, decay, stb, vnew.astype(BF16)))
        for d, (u_ref, w_ref, qd_ref, kd_ref, at_ref, g_ref, o_ref, last) in enumerate(dirs):
            rows, decay, stb, vnb = work[d]
            o = (_dot(block_diag(qd_ref[rows], head_of_lane), stb)
                 + _dot(block_diag(at_ref[rows], head_of_col), vnb))
            states[d] = states[d] * decay + _dot_tn(block_diag(kd_ref[rows], head_of_lane), vnb)
            for h in range(nh):
                o_ref[rows, DN_HD * h:DN_HD * h + DN_HD] = o[c * h:c * h + c]
    s_ref[0] = states[0]
    s_ref[1] = states[1]


def _dnscan(uf, ub, wf, wb, qf, qb, kf, kb, af, ab, gc, nb, s, cl, bs):
    r = uf.shape[0]
    nsub = 2
    c = DN_CHUNK * nsub
    assert cl % c == 0 and s % c == 0 and bs % c == 0
    ncc, ncl = cl // c, s // c
    ns = ncc + ncl

    def fwd(b, t):
        return jnp.where(t < ncc, bs // c + b * ncc + t, b * ncl + t - ncc)

    def bwd(b, t):
        return jnp.where(t < ncc, bs // c + b * ncc + (ncc - 1 - t), b * ncl + (ncl - 1 - (t - ncc)))

    def specs(idx):
        blk = lambda w: pl.BlockSpec((c, w), lambda b, t: (idx(b, t), 0))
        return [blk(512), blk(512), blk(512), blk(512), blk(256), blk(128)]

    return pl.pallas_call(
        functools.partial(_dnscan_kernel, nsub=nsub),
        grid=(nb, ns),
        in_specs=specs(fwd) + specs(bwd),
        out_specs=[pl.BlockSpec((c, 512), lambda b, t: (fwd(b, t), 0)),
                   pl.BlockSpec((c, 512), lambda b, t: (bwd(b, t), 0))],
        out_shape=[jax.ShapeDtypeStruct((r, 512), F32)] * 2,
        scratch_shapes=[pltpu.VMEM((2, DN_HEADS * DN_HD, DN_HD), F32)],
        compiler_params=_cp(("arbitrary", "arbitrary")),
        name="dnscan",
    )(uf, wf, qf, kf, af, gc, ub, wb, qb, kb, ab, gc)


def _route(logits, bias_row):
    epg = N_EXPERTS // N_GROUPS
    lane = lax.broadcasted_iota(I32, logits.shape, 1).astype(F32)
    scores = jax.nn.sigmoid(logits)
    gsel = scores + bias_row
    big = float(1 << 20)

    def first_argmax(vals, mask):
        mx = jnp.max(jnp.where(mask, vals, -jnp.inf), axis=-1, keepdims=True)
        idx = jnp.min(jnp.where(mask & (vals == mx), lane, big), axis=-1, keepdims=True)
        return mx, idx

    best = None
    gidx = None
    for g in range(N_GROUPS):
        mask = (lane >= epg * g) & (lane < epg * (g + 1))
        t1, i1 = first_argmax(gsel, mask)
        t2, _ = first_argmax(gsel, mask & (lane != i1))
        gs = t1 + t2
        if g == 0:
            best, gidx = gs, jnp.zeros_like(i1)
        else:
            better = gs > best
            best = jnp.where(better, gs, best)
            gidx = jnp.where(better, float(g), gidx)
    mask = (lane >= epg * gidx) & (lane < epg * (gidx + 1.0))
    _, e1 = first_argmax(gsel, mask)
    _, e2 = first_argmax(gsel, mask & (lane != e1))
    w1 = jnp.sum(jnp.where(lane == e1, scores, 0.0), axis=-1, keepdims=True)
    w2 = jnp.sum(jnp.where(lane == e2, scores, 0.0), axis=-1, keepdims=True)
    tot = w1 + w2
    eidx = jnp.where(lane == 0, e1, jnp.where(lane == 1, e2, 0.0)).astype(I32)
    gates = jnp.where(lane == 0, w1 / tot, jnp.where(lane == 1, w2 / tot, 0.0))
    return eidx, gates


def _post_mixer(x, y, m, g2, rw, rb):
    xn = x + m[2:3] * y
    h2 = _rms(xn, g2) * (1.0 + m[4:5]) + m[3:4]
    hi = h2.astype(BF16)
    lo = (h2 - hi.astype(F32)).astype(BF16)
    hw = _dot(hi, rw)
    logits = hw[:, 0:128] + (hw[:, 128:256] + _dot(lo, rw[:, 0:128]))
    eidx, gates = _route(logits, rb)
    return xn, hi, eidx, gates


def _mix0_kernel(x_ref, sc_ref, prev_ref, next_ref, of_ref, ob_ref, z_ref, mod_ref, cw_ref, on_ref, wo_ref,
                 g2_ref, rw_ref, rb_ref, xn_ref, h2_ref, ei_ref, ga_ref, *, ts, bs, s, cl):
    first, last = _seq_edges(pl.program_id(0), ts, bs, s, cl)
    ya = []
    for c in range(4):
        sl = slice(128 * c, 128 * c + 128)
        sg = slice(512 + 128 * c, 512 + 128 * c + 128)
        sx = slice(1024 + 128 * c, 1024 + 128 * c + 128)
        u = sc_ref[:, sg] * sc_ref[:, sx]
        pr = jnp.where(first, 0.0, prev_ref[7:8, sg] * prev_ref[7:8, sx])
        nx = jnp.where(last, 0.0, next_ref[0:1, sg] * next_ref[0:1, sx])
        up, un = _shifted(u, pr, nx)
        w = cw_ref[:, sl]
        ya.append((sc_ref[:, sl] * (up * w[0:1] + u * w[1:2] + un * w[2:3])).astype(BF16))
    yb = []
    for h in range(DN_HEADS):
        hs = slice(DN_HD * h, DN_HD * h + DN_HD)
        o = of_ref[:, hs] + ob_ref[:, hs]
        yb.append((_rms(o, on_ref[...]) * _silu(z_ref[:, hs])).astype(BF16))
    ycat = jnp.concatenate(ya + yb, axis=1)
    y = _dot(ycat, wo_ref[...])
    xn, h2, eidx, gates = _post_mixer(x_ref[...], y, mod_ref[0], g2_ref[...], rw_ref[...], rb_ref[...])
    xn_ref[...] = xn
    h2_ref[...] = h2.astype(BF16)
    ei_ref[...] = eidx
    ga_ref[...] = gates


def _mix0(x, sc, of, ob, z, mods, cw, on, wo, g2, rw, rb, ts, bs, s, cl, nb):
    r, d = x.shape
    kern = functools.partial(_mix0_kernel, ts=ts, bs=bs, s=s, cl=cl)
    mi = functools.partial(_mod_index, tm=ts, bs=bs, s=s, nb=nb)
    row = lambda w: pl.BlockSpec((ts, w), lambda i: (i, 0))
    full = lambda a: pl.BlockSpec(a.shape, lambda i: (0,) * a.ndim)
    return pl.pallas_call(
        kern,
        grid=(r // ts,),
        in_specs=[row(d), row(1536)] + _halo_specs(ts, 1536, r) + [
            row(512), row(512), row(512),
            pl.BlockSpec((1, N_MOD, d), lambda i: (mi(i), 0, 0)),
            full(cw), full(on), full(wo), full(g2), full(rw), full(rb)],
        out_specs=[row(d), row(d), row(128), row(128)],
        out_shape=[jax.ShapeDtypeStruct((r, d), F32), jax.ShapeDtypeStruct((r, d), BF16),
                   jax.ShapeDtypeStruct((r, 128), I32), jax.ShapeDtypeStruct((r, 128), F32)],
        compiler_params=_cp(("parallel",)),
        name="mix0",
    )(x, sc, sc, sc, of, ob, z, mods, cw, on, wo, g2, rw, rb)


def _gmm_kernel(te_ref, tf_ref, tv_ref, x_ref, wg_ref, wu_ref, wd_ref, y_ref, wgb, wub, wdb):
    t = pl.program_id(0)

    @pl.when(tf_ref[t] == 1)
    def _():
        wgb[...] = wg_ref[0, 0].astype(BF16)
        wub[...] = wu_ref[0, 0].astype(BF16)
        wdb[...] = wd_ref[0, 0].astype(BF16)

    @pl.when(tv_ref[t] == 1)
    def _():
        x = x_ref[...]
        a = (_silu(_dot(x, wgb[...])) * _dot(x, wub[...])).astype(BF16)
        y_ref[...] = _dot(a, wdb[...]).astype(BF16)

    @pl.when(tv_ref[t] == 0)
    def _():
        y_ref[...] = jnp.zeros_like(y_ref)


def _gmm(xs, w_gate, w_up, w_down, layer, tile_expert, tile_first, tile_valid, tmm):
    p, d = xs.shape
    de = w_gate.shape[-1]
    nt = p // tmm
    grid_spec = pltpu.PrefetchScalarGridSpec(
        num_scalar_prefetch=3,
        grid=(nt,),
        in_specs=[pl.BlockSpec((tmm, d), lambda t, te, tf, tv: (t, 0)),
                  pl.BlockSpec((1, 1, d, de), lambda t, te, tf, tv: (layer, te[t], 0, 0)),
                  pl.BlockSpec((1, 1, d, de), lambda t, te, tf, tv: (layer, te[t], 0, 0)),
                  pl.BlockSpec((1, 1, de, d), lambda t, te, tf, tv: (layer, te[t], 0, 0))],
        out_specs=pl.BlockSpec((tmm, d), lambda t, te, tf, tv: (t, 0)),
        scratch_shapes=[pltpu.VMEM((d, de), BF16), pltpu.VMEM((d, de), BF16), pltpu.VMEM((de, d), BF16)],
    )
    return pl.pallas_call(
        _gmm_kernel,
        grid_spec=grid_spec,
        out_shape=jax.ShapeDtypeStruct((p, d), BF16),
        compiler_params=_cp(("arbitrary",)),
        name="gmm",
    )(tile_expert, tile_first, tile_valid, xs, w_gate, w_up, w_down)


def _moe(h2, eidx2, w_gate, w_up, w_down, layer, tmm):
    t_tok = h2.shape[0]
    n = 2 * t_tok
    e_flat = eidx2.reshape(-1)
    onehot = (e_flat[:, None] == jnp.arange(N_EXPERTS, dtype=I32)[None, :]).astype(I32)
    csum = jnp.cumsum(onehot, axis=0)
    counts = csum[-1]
    ptiles = (counts + tmm - 1) // tmm
    tile_end = jnp.cumsum(ptiles)
    dest = jnp.sum(onehot * (csum - 1 + ((tile_end - ptiles) * tmm)[None, :]), axis=1)
    nt = n // tmm + N_EXPERTS
    src = (jnp.arange(nt * tmm, dtype=I32) % t_tok).at[dest].set(jnp.arange(n, dtype=I32) // 2)
    tid = jnp.arange(nt, dtype=I32)
    tile_valid = (tid < tile_end[-1]).astype(I32)
    te = jnp.minimum(jnp.sum((tile_end[None, :] <= tid[:, None]).astype(I32), axis=1), N_EXPERTS - 1)
    last_used = jnp.max(jnp.where(tile_valid == 1, te, 0))
    te = jnp.where(tile_valid == 1, te, last_used)
    tile_first = jnp.concatenate([jnp.ones((1,), I32), (te[1:] != te[:-1]).astype(I32)])
    xs = jnp.take(jnp.pad(h2, ((0, nt * tmm - t_tok), (0, 0))), src, axis=0, mode="clip")
    ys = _gmm(xs, w_gate, w_up, w_down, layer, te, tile_first, tile_valid, tmm)
    d2 = dest.reshape(t_tok, 2)
    return jnp.take(ys, d2[:, 0], axis=0, mode="clip"), jnp.take(ys, d2[:, 1], axis=0, mode="clip")


def _moe_combine(x, y1, y2, gates, m5):
    g = gates
    f = g[:, 0:1] * y1.astype(F32) + g[:, 1:2] * y2.astype(F32)
    return x + m5 * f


def _rope(x, cos, sin):
    n = x.shape[1]
    lane = lax.broadcasted_iota(I32, x.shape, 1)
    sw = jnp.where(lane % 32 < 16, pltpu.roll(x, n - 16, 1), pltpu.roll(x, 16, 1))
    reps = n // 128
    if reps > 1:
        cos = jnp.concatenate([cos] * reps, axis=1)
        sin = jnp.concatenate([sin] * reps, axis=1)
    return x * cos + sw * sin


def _inproj1_kernel(x_ref, y1_ref, y2_ref, ga_ref, m0_ref, m1_ref, g_ref, w_ref, cos_ref, sin_ref,
                    x1_ref, cq_ref, ckv_ref, dq_ref, dk_ref, dv_ref, *, tm, bs):
    x1 = _moe_combine(x_ref[...], y1_ref[...], y2_ref[...], ga_ref[...], m0_ref[0][5:6])
    x1_ref[...] = x1
    m = m1_ref[0]
    h = (_rms(x1, g_ref[...]) * (1.0 + m[1:2]) + m[0:1]).astype(BF16)
    in_lat = pl.program_id(0) * tm < bs
    cos, sin = cos_ref[...], sin_ref[...]
    scale = HEAD_DIM ** -0.5
    cq = _dot(h, w_ref[:, 0:512])
    cq_ref[...] = (jnp.where(in_lat, _rope(cq, cos, sin), cq) * scale).astype(BF16)
    ck = _dot(h, w_ref[:, 512:640])
    ckv_ref[:, 0:128] = jnp.where(in_lat, _rope(ck, cos, sin), ck).astype(BF16)
    ckv_ref[:, 128:256] = _dot(h, w_ref[:, 640:768]).astype(BF16)
    dq_ref[...] = (_dot(h, w_ref[:, 768:1280]) * scale).astype(BF16)
    dk_ref[...] = _dot(h, w_ref[:, 1280:1792]).astype(BF16)
    dv_ref[...] = _dot(h, w_ref[:, 1792:2304]).astype(BF16)


def _inproj1(x, y1, y2, gates, mods0, mods1, g, w, cos, sin, tm, bs, s, nb):
    r, d = x.shape
    kern = functools.partial(_inproj1_kernel, tm=tm, bs=bs)
    mi = functools.partial(_mod_index, tm=tm, bs=bs, s=s, nb=nb)
    row = lambda wd: pl.BlockSpec((tm, wd), lambda i: (i, 0))
    modspec = pl.BlockSpec((1, N_MOD, d), lambda i: (mi(i), 0, 0))
    tab = pl.BlockSpec((tm, 128), lambda i: (jnp.where(i * tm < bs, (i * tm % s) // tm, 0), 0))
    shp = lambda wd, dt: jax.ShapeDtypeStruct((r, wd), dt)
    return pl.pallas_call(
        kern,
        grid=(r // tm,),
        in_specs=[row(d), row(d), row(d), row(128), modspec, modspec,
                  pl.BlockSpec((1, d), lambda i: (0, 0)), pl.BlockSpec(w.shape, lambda i: (0, 0)), tab, tab],
        out_specs=[row(d), row(512), row(256), row(512), row(512), row(512)],
        out_shape=[shp(d, F32), shp(512, BF16), shp(256, BF16), shp(512, BF16), shp(512, BF16), shp(512, BF16)],
        compiler_params=_cp(("parallel",)),
        name="inproj1",
    )(x, y1, y2, gates, mods0, mods1, g, w, cos, sin)


def _swa_kernel(q_ref, kp_ref, kc_ref, kn_ref, kx_ref, sink_ref, o_ref, *, nblk):
    i = pl.program_id(1)
    wb = SWA_BLOCK
    kv = jnp.concatenate([kp_ref[...], kc_ref[...], kn_ref[...]], axis=0)
    kk, vv = kv[:, 0:128], kv[:, 128:256]
    kx, vx = kx_ref[:, 0:128], kx_ref[:, 128:256]
    a_i = lax.broadcasted_iota(I32, (2 * wb, 3 * wb), 0) % wb
    c_i = lax.broadcasted_iota(I32, (2 * wb, 3 * wb), 1)
    lo = jnp.where(i > 0, 0, wb)
    hi = jnp.where(i < nblk - 1, 3 * wb, 2 * wb)
    ok = (c_i >= a_i) & (c_i <= a_i + 2 * SWA_WINDOW) & (c_i >= lo) & (c_i < hi)
    half = lax.broadcasted_iota(I32, (1, 128), 1) // HEAD_DIM
    zero = jnp.zeros((), BF16)
    sink = sink_ref[...]
    for g in range(4):
        q2 = q_ref[:, 128 * g:128 * g + 128]
        qst = jnp.concatenate([jnp.where(half == 0, q2, zero), jnp.where(half == 1, q2, zero)], axis=0)
        s_loc = jnp.where(ok, _dot_nt(qst, kk), NEG)
        s_ctx = _dot_nt(qst, kx)
        sk = jnp.concatenate([jnp.broadcast_to(sink[0:1, 2 * g + a:2 * g + a + 1], (wb, 1)) for a in range(2)],
                             axis=0)
        m = jnp.maximum(jnp.maximum(jnp.max(s_loc, axis=-1, keepdims=True),
                                    jnp.max(s_ctx, axis=-1, keepdims=True)), sk)
        p_loc = jnp.exp(s_loc - m)
        p_ctx = jnp.exp(s_ctx - m)
        den = (jnp.sum(p_loc, axis=-1, keepdims=True) + jnp.sum(p_ctx, axis=-1, keepdims=True)
               + jnp.exp(sk - m))
        ost = (_dot(p_loc.astype(BF16), vv) + _dot(p_ctx.astype(BF16), vx)) / den
        o_ref[:, 128 * g:128 * g + 128] = jnp.where(half == 0, ost[0:wb], ost[wb:2 * wb]).astype(BF16)


def _swa(cq, ckv, sink_row, nb, s, cl, bs):
    wb = SWA_BLOCK
    nblk = s // wb
    kern = functools.partial(_swa_kernel, nblk=nblk)
    return pl.pallas_call(
        kern,
        grid=(nb, nblk),
        in_specs=[pl.BlockSpec((wb, 512), lambda b, i: (b * nblk + i, 0)),
                  pl.BlockSpec((wb, 256), lambda b, i: (b * nblk + jnp.maximum(i - 1, 0), 0)),
                  pl.BlockSpec((wb, 256), lambda b, i: (b * nblk + i, 0)),
                  pl.BlockSpec((wb, 256), lambda b, i: (b * nblk + jnp.minimum(i + 1, nblk - 1), 0)),
                  pl.BlockSpec((cl, 256), lambda b, i: (bs // cl + b, 0)),
                  pl.BlockSpec((1, 128), lambda b, i: (0, 0))],
        out_specs=pl.BlockSpec((wb, 512), lambda b, i: (b * nblk + i, 0)),
        out_shape=jax.ShapeDtypeStruct((bs, 512), BF16),
        compiler_params=_cp(("parallel", "parallel")),
        name="swa",
    )(cq, ckv, ckv, ckv, ckv, sink_row)


def _na_kernel(q_ref, k_ref, v_ref, kx_ref, vx_ref, bias_ref, o_ref, *, rows, unroll):
    half = lax.broadcasted_iota(I32, (1, 128), 1) // HEAD_DIM
    zero = jnp.zeros((), BF16)
    kx, vx = kx_ref[...], vx_ref[...]
    span = NA_KH * GRID_W

    def scores(r):
        rs = jnp.clip(r - NA_KH // 2, 0, rows - NA_KH)
        off = rs - r + NA_KH - 1
        q0 = pl.multiple_of(r * GRID_W, GRID_W)
        k0 = pl.multiple_of(rs * GRID_W, GRID_W)
        q2 = q_ref[pl.ds(q0, GRID_W), :]
        qst = jnp.concatenate([jnp.where(half == 0, q2, zero), jnp.where(half == 1, q2, zero)], axis=0)
        s_loc = _dot_nt(qst, k_ref[pl.ds(k0, span), :]) + bias_ref[0, off].reshape(2 * GRID_W, span)
        return q0, k0, s_loc, _dot_nt(qst, kx)

    def softmax(s_loc, s_ctx):
        m = jnp.maximum(jnp.max(s_loc, axis=-1, keepdims=True), jnp.max(s_ctx, axis=-1, keepdims=True))
        p_loc = jnp.exp(s_loc - m)
        p_ctx = jnp.exp(s_ctx - m)
        den = jnp.sum(p_loc, axis=-1, keepdims=True) + jnp.sum(p_ctx, axis=-1, keepdims=True)
        return p_loc.astype(BF16), p_ctx.astype(BF16), den

    def body(i, carry):
        sc = [scores(i * unroll + j) for j in range(unroll)]
        pr = [softmax(s_loc, s_ctx) for (_, _, s_loc, s_ctx) in sc]
        for (q0, k0, _, _), (p_loc, p_ctx, den) in zip(sc, pr):
            ost = (_dot(p_loc, v_ref[pl.ds(k0, span), :]) + _dot(p_ctx, vx)) / den
            o = jnp.where(half == 0, ost[0:GRID_W], ost[GRID_W:2 * GRID_W])
            o_ref[pl.ds(q0, GRID_W), :] = o.astype(BF16)
        return carry

    lax.fori_loop(0, rows // unroll, body, 0)


def _na(dq, dk, dv, bias, nb, s, cl, bs):
    rows = s // GRID_W
    unroll = 4
    assert rows % unroll == 0
    kern = functools.partial(_na_kernel, rows=rows, unroll=unroll)
    seq = pl.BlockSpec((s, 128), lambda b, p: (b, p))
    ctx = pl.BlockSpec((cl, 128), lambda b, p: (bs // cl + b, p))
    return pl.pallas_call(
        kern,
        grid=(nb, NA_HEADS // 2),
        in_specs=[seq, seq, seq, ctx, ctx,
                  pl.BlockSpec((1, NA_KH, 2, GRID_W, NA_KH * GRID_W), lambda b, p: (p, 0, 0, 0, 0))],
        out_specs=seq,
        out_shape=jax.ShapeDtypeStruct((bs, 512), BF16),
        compiler_params=_cp(("parallel", "parallel")),
        name="na",
    )(dq, dk, dv, dk, dv, bias)


def _na_bias_table(rpb):
    c = np.arange(GRID_W)
    qs = np.clip(c - NA_KW // 2, 0, GRID_W - NA_KW)
    kc = np.arange(GRID_W)
    ok = (kc[None, :] >= qs[:, None]) & (kc[None, :] < qs[:, None] + NA_KW)
    dc = np.clip(kc[None, :] - c[:, None] + NA_KW - 1, 0, 2 * NA_KW - 2)
    sel = (np.arange(2 * NA_KW - 1)[:, None, None] == dc[None]).astype(np.float32)
    cols = jnp.einsum("hab,bck->hack", rpb.astype(F32), sel, precision=HI)
    cols = jnp.where(ok[None, None], cols, NEG)
    t = jnp.stack([cols[:, off:off + NA_KH] for off in range(NA_KH)], axis=1)
    t = jnp.transpose(t, (0, 1, 3, 2, 4)).reshape(NA_HEADS, NA_KH, GRID_W, NA_KH * GRID_W)
    t = t.reshape(NA_HEADS // 2, 2, NA_KH, GRID_W, NA_KH * GRID_W)
    return jnp.transpose(t, (0, 2, 1, 3, 4))


def _mix1_kernel(x_ref, oc_ref, od_ref, mod_ref, wo_ref, g2_ref, rw_ref, rb_ref,
                 xn_ref, h2_ref, ei_ref, ga_ref):
    y = _dot(oc_ref[...], wo_ref[0:512, :]) + _dot(od_ref[...], wo_ref[512:1024, :])
    xn, h2, eidx, gates = _post_mixer(x_ref[...], y, mod_ref[0], g2_ref[...], rw_ref[...], rb_ref[...])
    xn_ref[...] = xn
    h2_ref[...] = h2.astype(BF16)
    ei_ref[...] = eidx
    ga_ref[...] = gates


def _mix1(x, oc, od, mods, wo, g2, rw, rb, tm, bs, s):
    d = x.shape[1]
    row = lambda w: pl.BlockSpec((tm, w), lambda i: (i, 0))
    full = lambda a: pl.BlockSpec(a.shape, lambda i: (0,) * a.ndim)
    return pl.pallas_call(
        _mix1_kernel,
        grid=(bs // tm,),
        in_specs=[row(d), row(512), row(512), pl.BlockSpec((1, N_MOD, d), lambda i: (i * tm // s, 0, 0)),
                  full(wo), full(g2), full(rw), full(rb)],
        out_specs=[row(d), row(d), row(128), row(128)],
        out_shape=[jax.ShapeDtypeStruct((bs, d), F32), jax.ShapeDtypeStruct((bs, d), BF16),
                   jax.ShapeDtypeStruct((bs, 128), I32), jax.ShapeDtypeStruct((bs, 128), F32)],
        compiler_params=_cp(("parallel",)),
        name="mix1",
    )(x, oc, od, mods, wo, g2, rw, rb)


def _final_kernel(x_ref, y1_ref, y2_ref, ga_ref, mod_ref, g_ref, o_ref):
    x = _moe_combine(x_ref[...], y1_ref[...], y2_ref[...], ga_ref[...], mod_ref[0][5:6])
    o_ref[...] = _rms(x, g_ref[...])


def _final(x, y1, y2, gates, mods, g, tm, s):
    r, d = x.shape
    row = lambda w: pl.BlockSpec((tm, w), lambda i: (i, 0))
    return pl.pallas_call(
        _final_kernel,
        grid=(r // tm,),
        in_specs=[row(d), row(d), row(d), row(128), pl.BlockSpec((1, N_MOD, d), lambda i: (i * tm // s, 0, 0)),
                  pl.BlockSpec((1, d), lambda i: (0, 0))],
        out_specs=row(d),
        out_shape=jax.ShapeDtypeStruct((r, d), F32),
        compiler_params=_cp(("parallel",)),
        name="final",
    )(x, y1, y2, gates, mods, g)


def _rope_tables(s):
    nf = HEAD_DIM // 4
    t = np.arange(s)
    inv = ROPE_THETA ** (-np.arange(nf, dtype=np.float64) / nf)
    ar = (t // GRID_W)[:, None] * inv
    ac = (t % GRID_W)[:, None] * inv
    cos = np.concatenate([np.cos(ar), np.cos(ar), np.cos(ac), np.cos(ac)], axis=1)
    sin = np.concatenate([-np.sin(ar), np.sin(ar), -np.sin(ac), np.sin(ac)], axis=1)
    return (jnp.asarray(np.concatenate([cos, cos], axis=1), F32),
            jnp.asarray(np.concatenate([sin, sin], axis=1), F32))


def kernel(x, c, ctx, c_ctx, ada_w, ada_b, norm1_g, norm2_g, ev_w_in, ev_w_out, sc_conv_w, dn_conv_w, dn_a_log, dn_dt_bias, dn_onorm_g, od_w_in, od_w_out, swa_sink, na_rpb, router_w, router_b, moe_w_gate, moe_w_up, moe_w_down, final_g):
    nb, s, d = x.shape
    cl = ctx.shape[1]
    bs = nb * s
    tm = 512
    ts = 256
    tmm = 512
    assert d == 1024 and s % tm == 0 and (nb * cl) % tm == 0 and cl % ts == 0 and s % ts == 0
    assert s // GRID_W >= NA_KH and bs % cl == 0 and nb + 1 <= 8

    xf = jnp.concatenate([x.reshape(bs, d), ctx.reshape(nb * cl, d)], axis=0)
    cc = jnp.zeros((8, d), F32).at[:nb].set(c).at[nb].set(c_ctx)
    mods = _ada(cc, ada_w, ada_b).reshape(ada_w.shape[0], 8, N_MOD, d)
    rw32 = jnp.pad(router_w, ((0, 0), (0, 128 - N_EXPERTS)))
    rw_hi = rw32.astype(BF16)
    rw = jnp.concatenate([rw_hi, (rw32 - rw_hi.astype(F32)).astype(BF16)], axis=1)
    rb = jnp.pad(router_b, (0, 128 - N_EXPERTS)).reshape(1, 128)
    row = lambda v: v.reshape(1, -1)

    w_in0 = jnp.pad(ev_w_in[0], ((0, 0), (0, 3712 - ev_w_in.shape[-1]))).astype(BF16)
    sc, qkv, z, bg = _inproj0(xf, mods[0], row(norm1_g[0]), w_in0, tm, bs, s, nb)
    pad16 = lambda v: jnp.pad(v.reshape(-1), (8, 128 - 16)).reshape(1, 128)
    q, k, v, bga = _dnprep(qkv, bg, dn_conv_w[0], pad16(dn_a_log[0]), pad16(dn_dt_bias[0]), ts, bs, s, cl)
    uf, ub, wf, wb, qf, qb, kf, kb, af, ab, gc = _dnchunk(q, k, v, bga)
    of, ob = _dnscan(uf, ub, wf, wb, qf, qb, kf, kb, af, ab, gc, nb, s, cl, bs)
    x0, h2, ei, ga = _mix0(xf, sc, of, ob, z, mods[0], sc_conv_w[0], row(dn_onorm_g[0]),
                           ev_w_out[0].astype(BF16), row(norm2_g[0]), rw, rb, ts, bs, s, cl, nb)
    y1, y2 = _moe(h2, ei[:, 0:2], moe_w_gate, moe_w_up, moe_w_down, 0, tmm)

    perm = np.concatenate([np.arange(HEAD_DIM) + HEAD_DIM * (g + 4 * a) for g in range(4) for a in range(2)])
    w1 = od_w_in[0]
    w_in1 = jnp.concatenate([w1[:, 0:512][:, perm], w1[:, 512:]], axis=1).astype(BF16)
    wo1 = od_w_out[0]
    w_out1 = jnp.concatenate([wo1[0:512][perm], wo1[512:]], axis=0).astype(BF16)
    sink_row = jnp.pad(swa_sink[0][np.array([g + 4 * a for g in range(4) for a in range(2)])],
                       (0, 128 - SWA_HEADS)).reshape(1, 128)
    cos, sin = _rope_tables(s)
    x1, cq, ckv, dq, dk, dv = _inproj1(x0, y1, y2, ga, mods[0], mods[1], row(norm1_g[1]), w_in1, cos, sin,
                                       tm, bs, s, nb)
    oc = _swa(cq, ckv, sink_row, nb, s, cl, bs)
    od = _na(dq, dk, dv, _na_bias_table(na_rpb[0]), nb, s, cl, bs)
    x2, h2, ei, ga = _mix1(x1, oc, od, mods[1], w_out1, row(norm2_g[1]), rw, rb, tm, bs, s)
    y1, y2 = _moe(h2, ei[:, 0:2], moe_w_gate, moe_w_up, moe_w_down, 1, tmm)
    out = _final(x2, y1, y2, ga, mods[1], row(final_g), tm, s)
    return out.reshape(nb, s, d)
```

```python
import functools
import math

import numpy as np
import jax
import jax.numpy as jnp
from jax import lax
from jax.experimental import pallas as pl
from jax.experimental.pallas import tpu as pltpu

F32 = jnp.float32
BF16 = jnp.bfloat16
I32 = jnp.int32
HI = lax.Precision.HIGHEST

EPS = 1e-6
N_MOD = 6
GRID_W = 64
HEAD_DIM = 64
DN_HEADS = 4
DN_HD = 128
DN_CHUNK = 64
SWA_HEADS = 8
SWA_KV = 2
SWA_BLOCK = 128
SWA_WINDOW = 128
NA_HEADS = 8
NA_KH = 8
NA_KW = 16
ROPE_THETA = 10000.0
N_EXPERTS = 16
N_GROUPS = 4
NEG = -1e30
LOG2E = 1.4426950408889634
VMEM_LIMIT = 56 * 1024 * 1024


def _cp(sem, vmem=VMEM_LIMIT):
    return pltpu.CompilerParams(dimension_semantics=sem, vmem_limit_bytes=vmem)


def _dot(a, b, precision=None):
    return jnp.dot(a, b, preferred_element_type=F32, precision=precision)


def _dot_nt(a, b, precision=None):
    return lax.dot_general(a, b, (((1,), (1,)), ((), ())), preferred_element_type=F32, precision=precision)


def _dot_tn(a, b, precision=None):
    return lax.dot_general(a, b, (((0,), (0,)), ((), ())), preferred_element_type=F32, precision=precision)


def _silu(x):
    return x * jax.nn.sigmoid(x)


def _rms(x, g):
    return x * lax.rsqrt(jnp.mean(x * x, axis=-1, keepdims=True) + EPS) * g


def _ada_kernel(cc_ref, w_ref, b_ref, o_ref):
    a = _silu(cc_ref[...])
    o_ref[0] = _dot(a, w_ref[0], HI) + b_ref[0]


def _ada(cc, ada_w, ada_b):
    depth, d, n = ada_w.shape
    tn = 1536
    return pl.pallas_call(
        _ada_kernel,
        grid=(depth, n // tn),
        in_specs=[pl.BlockSpec((8, d), lambda l, j: (0, 0)),
                  pl.BlockSpec((1, d, tn), lambda l, j: (l, 0, j)),
                  pl.BlockSpec((1, 1, tn), lambda l, j: (l, 0, j))],
        out_specs=pl.BlockSpec((1, 8, tn), lambda l, j: (l, 0, j)),
        out_shape=jax.ShapeDtypeStruct((depth, 8, n), F32),
        compiler_params=_cp(("parallel", "parallel")),
        name="ada",
    )(cc, ada_w, ada_b.reshape(depth, 1, n))


def _mod_index(i, tm, bs, s, nb):
    row0 = i * tm
    return jnp.where(row0 < bs, row0 // s, nb)


def _inproj0_kernel(x_ref, mod_ref, g_ref, w_ref, sc_ref, qkv_ref, z_ref, bg_ref):
    m = mod_ref[0]
    h = (_rms(x_ref[...], g_ref[...]) * (1.0 + m[1:2]) + m[0:1]).astype(BF16)
    sc_ref[...] = _dot(h, w_ref[:, 0:1536])
    qkv_ref[...] = _dot(h, w_ref[:, 1536:3072])
    z_ref[...] = _dot(h, w_ref[:, 3072:3584])
    bg_ref[...] = _dot(h, w_ref[:, 3584:3712])


def _inproj0(x, mods, g, w, tm, bs, s, nb):
    r, d = x.shape
    mi = functools.partial(_mod_index, tm=tm, bs=bs, s=s, nb=nb)
    return pl.pallas_call(
        _inproj0_kernel,
        grid=(r // tm,),
        in_specs=[pl.BlockSpec((tm, d), lambda i: (i, 0)),
                  pl.BlockSpec((1, N_MOD, d), lambda i: (mi(i), 0, 0)),
                  pl.BlockSpec((1, d), lambda i: (0, 0)),
                  pl.BlockSpec(w.shape, lambda i: (0, 0))],
        out_specs=[pl.BlockSpec((tm, 1536), lambda i: (i, 0)),
                   pl.BlockSpec((tm, 1536), lambda i: (i, 0)),
                   pl.BlockSpec((tm, 512), lambda i: (i, 0)),
                   pl.BlockSpec((tm, 128), lambda i: (i, 0))],
        out_shape=[jax.ShapeDtypeStruct((r, 1536), F32), jax.ShapeDtypeStruct((r, 1536), F32),
                   jax.ShapeDtypeStruct((r, 512), F32), jax.ShapeDtypeStruct((r, 128), F32)],
        compiler_params=_cp(("parallel",)),
        name="inproj0",
    )(x, mods, g, w)


def _seq_edges(i, ts, bs, s, cl):
    row0 = i * ts
    in_lat = row0 < bs
    r_in = jnp.where(in_lat, row0 % s, (row0 - bs) % cl)
    seqlen = jnp.where(in_lat, s, cl)
    return r_in == 0, r_in + ts == seqlen


def _shifted(x, prev_row, next_row):
    n = x.shape[0]
    rows = lax.broadcasted_iota(I32, x.shape, 0)
    xp = jnp.where(rows == 0, prev_row, pltpu.roll(x, 1, 0))
    xn = jnp.where(rows == n - 1, next_row, pltpu.roll(x, n - 1, 0))
    return xp, xn


def _halo_specs(ts, width, r):
    nb8 = r // 8
    k = ts // 8
    return [pl.BlockSpec((8, width), lambda i: (jnp.maximum(i * k - 1, 0), 0)),
            pl.BlockSpec((8, width), lambda i: (jnp.minimum((i + 1) * k, nb8 - 1), 0))]


def _dnprep_kernel(x_ref, prev_ref, next_ref, bg_ref, cw_ref, alog_ref, dt_ref,
                   q_ref, k_ref, v_ref, bga_ref, *, ts, bs, s, cl):
    first, last = _seq_edges(pl.program_id(0), ts, bs, s, cl)
    for c in range(12):
        sl = slice(128 * c, 128 * c + 128)
        x = x_ref[:, sl]
        pr = jnp.where(first, 0.0, prev_ref[7:8, sl])
        nx = jnp.where(last, 0.0, next_ref[0:1, sl])
        xp, xn = _shifted(x, pr, nx)
        w = cw_ref[:, sl]
        y = _silu(xp * w[0:1] + x * w[1:2] + xn * w[2:3])
        hs = slice(128 * (c % 4), 128 * (c % 4) + 128)
        if c < 8:
            y = y * lax.rsqrt(jnp.sum(y * y, axis=-1, keepdims=True) + EPS)
        if c < 4:
            q_ref[:, hs] = y * DN_HD ** -0.5
        elif c < 8:
            k_ref[:, hs] = y
        else:
            v_ref[:, hs] = y
    b = bg_ref[...]
    cols = lax.broadcasted_iota(I32, b.shape, 1)
    beta = jax.nn.sigmoid(b)
    t = b + dt_ref[...]
    softplus = jnp.maximum(t, 0.0) + jnp.log1p(jnp.exp(-jnp.abs(t)))
    g = -jnp.exp(alog_ref[...]) * softplus
    bga_ref[...] = jnp.where(cols < 8, beta, jnp.where(cols < 16, g, 0.0))


def _dnprep(qkv, bg, cw, alog_row, dt_row, ts, bs, s, cl):
    r = qkv.shape[0]
    kern = functools.partial(_dnprep_kernel, ts=ts, bs=bs, s=s, cl=cl)
    return pl.pallas_call(
        kern,
        grid=(r // ts,),
        in_specs=[pl.BlockSpec((ts, 1536), lambda i: (i, 0))] + _halo_specs(ts, 1536, r) + [
            pl.BlockSpec((ts, 128), lambda i: (i, 0)),
            pl.BlockSpec((3, 1536), lambda i: (0, 0)),
            pl.BlockSpec((1, 128), lambda i: (0, 0)),
            pl.BlockSpec((1, 128), lambda i: (0, 0))],
        out_specs=[pl.BlockSpec((ts, 512), lambda i: (i, 0))] * 3 + [pl.BlockSpec((ts, 128), lambda i: (i, 0))],
        out_shape=[jax.ShapeDtypeStruct((r, 512), F32)] * 3 + [jax.ShapeDtypeStruct((r, 128), F32)],
        compiler_params=_cp(("parallel",)),
        name="dnprep",
    )(qkv, qkv, qkv, bg, cw, alog_row, dt_row)


def _dnchunk_kernel(q_ref, k_ref, v_ref, bg_ref,
                    uf_ref, ub_ref, wf_ref, wb_ref, qf_ref, qb_ref, kf_ref, kb_ref, af_ref, ab_ref, gc_ref, *, nchunks):
    outs = ((uf_ref, wf_ref, qf_ref, kf_ref, af_ref), (ub_ref, wb_ref, qb_ref, kb_ref, ab_ref))
    n = DN_CHUNK * DN_HEADS
    chains = []
    for cc in range(nchunks):
        chains += _dnchunk_setup(slice(cc * DN_CHUNK, (cc + 1) * DN_CHUNK), q_ref, k_ref, v_ref, bg_ref, gc_ref)
    for ch in chains:
        ch["tm"] = ch["nmat"]
        nb16 = ch["nmat"].astype(BF16)
        ch["npow"] = _dot(nb16, nb16)
    for _ in range(4):
        for ch in chains:
            nb16 = ch["npow"].astype(BF16)
            ch["both"] = _dot(jnp.concatenate([nb16, ch["tm"].astype(BF16)], axis=0), nb16)
        for ch in chains:
            ch["tm"] = ch["tm"] + ch["npow"] + ch["both"][n:2 * n]
            ch["npow"] = ch["both"][0:n]
    for ch in chains:
        ch["both"] = _dot(ch["tm"].astype(BF16), ch["npow"].astype(BF16))
    for ch in chains:
        ch["tm"] = ch["tm"] + ch["npow"] + ch["both"]
    for ch in chains:
        ch["uw"] = ch["rhs"] + _dot(ch["tm"].astype(BF16), ch["rhs"].astype(BF16))
    for ch in chains:
        u_ref, w_ref, qd_ref, kd_ref, at_ref = outs[ch["d"]]
        rows, uw, c = ch["rows"], ch["uw"], DN_CHUNK
        for h in range(DN_HEADS):
            hs = slice(DN_HD * h, DN_HD * h + DN_HD)
            rs = slice(c * h, c * h + c)
            u_ref[rows, hs] = uw[rs, 0:DN_HD]
            w_ref[rows, hs] = uw[rs, DN_HD:2 * DN_HD].astype(BF16)
            qd_ref[rows, hs] = ch["qd"][rs]
        kd_ref[slice(2 * rows.start, 2 * rows.stop)] = ch["kd"].T.astype(BF16)
        att = ch["att"]
        at_ref[rows] = (att[0:c] + att[c:2 * c] + att[2 * c:3 * c] + att[3 * c:4 * c]).astype(BF16)


def _dnchunk_setup(rows, q_ref, k_ref, v_ref, bg_ref, gc_ref):
    c, nh = DN_CHUNK, DN_HEADS
    n = c * nh
    bg = bg_ref[rows]
    i64 = lax.broadcasted_iota(I32, (c, c), 0)
    j64 = lax.broadcasted_iota(I32, (c, c), 1)
    cols = lax.broadcasted_iota(I32, bg.shape, 1)
    gcf = _dot((i64 >= j64).astype(F32), bg, HI)
    gcb = _dot((i64 <= j64).astype(F32), bg, HI)
    gc = jnp.where(cols >= 12, gcb, gcf)
    gc_ref[rows] = gc
    gct = gc.T
    ii = lax.broadcasted_iota(I32, (n, n), 0)
    jj = lax.broadcasted_iota(I32, (n, n), 1)
    same = (ii // c) == (jj // c)

    def stack(ref):
        return jnp.concatenate([ref[rows, DN_HD * h:DN_HD * h + DN_HD] for h in range(nh)], axis=0)

    def col_bcast(arr, r0, r1, col0):
        return jnp.concatenate([jnp.broadcast_to(arr[r0:r1, col0 + h:col0 + h + 1], (c, n)) for h in range(nh)],
                               axis=0)

    kst, qst, vst = stack(k_ref), stack(q_ref), stack(v_ref)
    kb = kst.astype(BF16)
    kq = _dot_nt(jnp.concatenate([kb, qst.astype(BF16)], axis=0), kb)
    chains = []
    for d in range(2):
        incl = same & ((ii >= jj) if d == 0 else (ii <= jj))
        strict = same & ((ii > jj) if d == 0 else (ii < jj))
        last = c - 1 if d == 0 else 0
        beta = col_bcast(bg, 0, c, 4 * d)
        gcol = col_bcast(gc, 0, c, 8 + 4 * d)
        glast = col_bcast(gc, last, last + 1, 8 + 4 * d)
        grow = jnp.concatenate([gct[8 + 4 * d + h:9 + 4 * d + h, :] for h in range(nh)], axis=1)
        decay = jnp.exp(jnp.where(incl, gcol - grow, NEG))
        nmat = jnp.where(strict, -(beta * kq[0:n] * decay), 0.0)
        eg = jnp.exp(gcol)
        b1, e1 = beta[:, 0:DN_HD], eg[:, 0:DN_HD]
        chains.append(dict(
            d=d, rows=rows, nmat=nmat,
            rhs=jnp.concatenate([b1 * vst, (b1 * e1) * kst], axis=1),
            qd=(qst * e1).astype(BF16),
            kd=kst * jnp.exp(glast - gcol)[:, 0:DN_HD],
            att=kq[n:2 * n] * decay))
    return chains


def _dnchunk(q, k, v, bga):
    r = q.shape[0]
    nchunks = 4
    c = DN_CHUNK * nchunks
    row = lambda w: pl.BlockSpec((c, w), lambda i: (i, 0))
    shp = lambda w, dt: jax.ShapeDtypeStruct((r, w), dt)
    return pl.pallas_call(
        functools.partial(_dnchunk_kernel, nchunks=nchunks),
        grid=(r // c,),
        in_specs=[row(512), row(512), row(512), row(128)],
        out_specs=[row(512)] * 6 + [pl.BlockSpec((2 * c, 256), lambda i: (i, 0))] * 2 + [row(256), row(256), row(128)],
        out_shape=([shp(512, F32)] * 2 + [shp(512, BF16)] * 4 + [jax.ShapeDtypeStruct((2 * r, 256), BF16)] * 2
                   + [shp(256, BF16)] * 2 + [shp(128, F32)]),
        compiler_params=_cp(("parallel",)),
        name="dnchunk",
    )(q, k, v, bga)


def _dnscan_kernel(uf_ref, wf_ref, qf_ref, kf_ref, af_ref, gf_ref,
                   ub_ref, wb_ref, qb_ref, kb_ref, ab_ref, gb_ref,
                   of_ref, ob_ref, s_ref, *, nsub):
    @pl.when(pl.program_id(1) == 0)
    def _():
        s_ref[...] = jnp.zeros_like(s_ref)

    c, nh = DN_CHUNK, DN_HEADS
    head_of_lane = lax.broadcasted_iota(I32, (1, nh * DN_HD), 1) // DN_HD
    head_of_col = lax.broadcasted_iota(I32, (1, nh * c), 1) // c
    zero = jnp.zeros((), BF16)

    def block_diag(tile, head_ids):
        return jnp.concatenate([jnp.where(head_ids == h, tile, zero) for h in range(nh)], axis=0)

    dirs = ((uf_ref, wf_ref, qf_ref, kf_ref, af_ref, gf_ref, of_ref, c - 1),
            (ub_ref, wb_ref, qb_ref, kb_ref, ab_ref, gb_ref, ob_ref, 0))
    states = [s_ref[0], s_ref[1]]
    for sub in range(nsub):
        work = []
        for d, (u_ref, w_ref, qd_ref, kd_ref, at_ref, g_ref, o_ref, last) in enumerate(dirs):
            k = sub if d == 0 else nsub - 1 - sub
            rows = slice(c * k, c * k + c)
            g = g_ref[rows]
            decay = jnp.concatenate(
                [jnp.broadcast_to(jnp.exp(g[last:last + 1, 8 + 4 * d + h:9 + 4 * d + h]), (DN_HD, DN_HD))
                 for h in range(nh)], axis=0)
            ust = jnp.concatenate([u_ref[rows, DN_HD * h:DN_HD * h + DN_HD] for h in range(nh)], axis=0)
            stb = states[d].astype(BF16)
            vnew = ust - _dot(block_diag(w_ref[rows], head_of_lane), stb)
            work.append((rows, decay, stb, vnew.astype(BF16)))
        for d, (u_ref, w_ref, qd_ref, kd_ref, at_ref, g_ref, o_ref, last) in enumerate(dirs):
            rows, decay, stb, vnb = work[d]
            o = (_dot(block_diag(qd_ref[rows], head_of_lane), stb)
                 + _dot(block_diag(at_ref[rows], head_of_col), vnb))
            kdt = kd_ref[slice(2 * rows.start, 2 * rows.stop)]
            states[d] = states[d] * decay + _dot(block_diag(kdt, head_of_col), vnb)
            for h in range(nh):
                o_ref[rows, DN_HD * h:DN_HD * h + DN_HD] = o[c * h:c * h + c]
    s_ref[0] = states[0]
    s_ref[1] = states[1]


def _dnscan(uf, ub, wf, wb, qf, qb, kf, kb, af, ab, gc, nb, s, cl, bs):
    r = uf.shape[0]
    nsub = 2
    c = DN_CHUNK * nsub
    assert cl % c == 0 and s % c == 0 and bs % c == 0
    ncc, ncl = cl // c, s // c
    ns = ncc + ncl

    def fwd(b, t):
        return jnp.where(t < ncc, bs // c + b * ncc + t, b * ncl + t - ncc)

    def bwd(b, t):
        return jnp.where(t < ncc, bs // c + b * ncc + (ncc - 1 - t), b * ncl + (ncl - 1 - (t - ncc)))

    def specs(idx):
        blk = lambda w: pl.BlockSpec((c, w), lambda b, t: (idx(b, t), 0))
        return [blk(512), blk(512), blk(512), pl.BlockSpec((2 * c, 256), lambda b, t: (idx(b, t), 0)), blk(256),
                blk(128)]

    return pl.pallas_call(
        functools.partial(_dnscan_kernel, nsub=nsub),
        grid=(nb, ns),
        in_specs=specs(fwd) + specs(bwd),
        out_specs=[pl.BlockSpec((c, 512), lambda b, t: (fwd(b, t), 0)),
                   pl.BlockSpec((c, 512), lambda b, t: (bwd(b, t), 0))],
        out_shape=[jax.ShapeDtypeStruct((r, 512), F32)] * 2,
        scratch_shapes=[pltpu.VMEM((2, DN_HEADS * DN_HD, DN_HD), F32)],
        compiler_params=_cp(("arbitrary", "arbitrary")),
        name="dnscan",
    )(uf, wf, qf, kf, af, gc, ub, wb, qb, kb, ab, gc)


def _route(logits, bias_row):
    epg = N_EXPERTS // N_GROUPS
    lane = lax.broadcasted_iota(I32, logits.shape, 1).astype(F32)
    scores = jax.nn.sigmoid(logits)
    gsel = scores + bias_row
    big = float(1 << 20)

    def first_argmax(vals, mask):
        mx = jnp.max(jnp.where(mask, vals, -jnp.inf), axis=-1, keepdims=True)
        idx = jnp.min(jnp.where(mask & (vals == mx), lane, big), axis=-1, keepdims=True)
        return mx, idx

    best = None
    gidx = None
    for g in range(N_GROUPS):
        mask = (lane >= epg * g) & (lane < epg * (g + 1))
        t1, i1 = first_argmax(gsel, mask)
        t2, _ = first_argmax(gsel, mask & (lane != i1))
        gs = t1 + t2
        if g == 0:
            best, gidx = gs, jnp.zeros_like(i1)
        else:
            better = gs > best
            best = jnp.where(better, gs, best)
            gidx = jnp.where(better, float(g), gidx)
    mask = (lane >= epg * gidx) & (lane < epg * (gidx + 1.0))
    _, e1 = first_argmax(gsel, mask)
    _, e2 = first_argmax(gsel, mask & (lane != e1))
    w1 = jnp.sum(jnp.where(lane == e1, scores, 0.0), axis=-1, keepdims=True)
    w2 = jnp.sum(jnp.where(lane == e2, scores, 0.0), axis=-1, keepdims=True)
    tot = w1 + w2
    eidx = jnp.where(lane == 0, e1, jnp.where(lane == 1, e2, 0.0)).astype(I32)
    gates = jnp.where(lane == 0, w1 / tot, jnp.where(lane == 1, w2 / tot, 0.0))
    return eidx, gates


def _post_mixer(x, y, m, g2, rw, rb):
    xn = x + m[2:3] * y
    h2 = _rms(xn, g2) * (1.0 + m[4:5]) + m[3:4]
    hi = h2.astype(BF16)
    lo = (h2 - hi.astype(F32)).astype(BF16)
    hw = _dot(hi, rw)
    logits = hw[:, 0:128] + (hw[:, 128:256] + _dot(lo, rw[:, 0:128]))
    eidx, gates = _route(logits, rb)
    return xn, hi, eidx, gates


def _mix0_kernel(x_ref, sc_ref, prev_ref, next_ref, of_ref, ob_ref, z_ref, mod_ref, cw_ref, on_ref, wo_ref,
                 g2_ref, rw_ref, rb_ref, xn_ref, h2_ref, ei_ref, ga_ref, *, ts, bs, s, cl):
    first, last = _seq_edges(pl.program_id(0), ts, bs, s, cl)
    ya = []
    for c in range(4):
        sl = slice(128 * c, 128 * c + 128)
        sg = slice(512 + 128 * c, 512 + 128 * c + 128)
        sx = slice(1024 + 128 * c, 1024 + 128 * c + 128)
        u = sc_ref[:, sg] * sc_ref[:, sx]
        pr = jnp.where(first, 0.0, prev_ref[7:8, sg] * prev_ref[7:8, sx])
        nx = jnp.where(last, 0.0, next_ref[0:1, sg] * next_ref[0:1, sx])
        up, un = _shifted(u, pr, nx)
        w = cw_ref[:, sl]
        ya.append((sc_ref[:, sl] * (up * w[0:1] + u * w[1:2] + un * w[2:3])).astype(BF16))
    yb = []
    for h in range(DN_HEADS):
        hs = slice(DN_HD * h, DN_HD * h + DN_HD)
        o = of_ref[:, hs] + ob_ref[:, hs]
        yb.append((_rms(o, on_ref[...]) * _silu(z_ref[:, hs])).astype(BF16))
    ycat = jnp.concatenate(ya + yb, axis=1)
    y = _dot(ycat, wo_ref[...])
    xn, h2, eidx, gates = _post_mixer(x_ref[...], y, mod_ref[0], g2_ref[...], rw_ref[...], rb_ref[...])
    xn_ref[...] = xn
    h2_ref[...] = h2.astype(BF16)
    ei_ref[...] = eidx
    ga_ref[...] = gates


def _mix0(x, sc, of, ob, z, mods, cw, on, wo, g2, rw, rb, ts, bs, s, cl, nb):
    r, d = x.shape
    kern = functools.partial(_mix0_kernel, ts=ts, bs=bs, s=s, cl=cl)
    mi = functools.partial(_mod_index, tm=ts, bs=bs, s=s, nb=nb)
    row = lambda w: pl.BlockSpec((ts, w), lambda i: (i, 0))
    full = lambda a: pl.BlockSpec(a.shape, lambda i: (0,) * a.ndim)
    return pl.pallas_call(
        kern,
        grid=(r // ts,),
        in_specs=[row(d), row(1536)] + _halo_specs(ts, 1536, r) + [
            row(512), row(512), row(512),
            pl.BlockSpec((1, N_MOD, d), lambda i: (mi(i), 0, 0)),
            full(cw), full(on), full(wo), full(g2), full(rw), full(rb)],
        out_specs=[row(d), row(d), row(128), row(128)],
        out_shape=[jax.ShapeDtypeStruct((r, d), F32), jax.ShapeDtypeStruct((r, d), BF16),
                   jax.ShapeDtypeStruct((r, 128), I32), jax.ShapeDtypeStruct((r, 128), F32)],
        compiler_params=_cp(("parallel",)),
        name="mix0",
    )(x, sc, sc, sc, of, ob, z, mods, cw, on, wo, g2, rw, rb)


def _gmm_kernel(te_ref, tf_ref, tv_ref, x_ref, wg_ref, wu_ref, wd_ref, y_ref, wgb, wub, wdb):
    t = pl.program_id(0)

    @pl.when(tf_ref[t] == 1)
    def _():
        wgb[...] = wg_ref[0, 0].astype(BF16)
        wub[...] = wu_ref[0, 0].astype(BF16)
        wdb[...] = wd_ref[0, 0].astype(BF16)

    @pl.when(tv_ref[t] == 1)
    def _():
        x = x_ref[...]
        a = (_silu(_dot(x, wgb[...])) * _dot(x, wub[...])).astype(BF16)
        y_ref[...] = _dot(a, wdb[...]).astype(BF16)

    @pl.when(tv_ref[t] == 0)
    def _():
        y_ref[...] = jnp.zeros_like(y_ref)


def _gmm(xs, w_gate, w_up, w_down, layer, tile_expert, tile_first, tile_valid, tmm):
    p, d = xs.shape
    de = w_gate.shape[-1]
    nt = p // tmm
    grid_spec = pltpu.PrefetchScalarGridSpec(
        num_scalar_prefetch=3,
        grid=(nt,),
        in_specs=[pl.BlockSpec((tmm, d), lambda t, te, tf, tv: (t, 0)),
                  pl.BlockSpec((1, 1, d, de), lambda t, te, tf, tv: (layer, te[t], 0, 0)),
                  pl.BlockSpec((1, 1, d, de), lambda t, te, tf, tv: (layer, te[t], 0, 0)),
                  pl.BlockSpec((1, 1, de, d), lambda t, te, tf, tv: (layer, te[t], 0, 0))],
        out_specs=pl.BlockSpec((tmm, d), lambda t, te, tf, tv: (t, 0)),
        scratch_shapes=[pltpu.VMEM((d, de), BF16), pltpu.VMEM((d, de), BF16), pltpu.VMEM((de, d), BF16)],
    )
    return pl.pallas_call(
        _gmm_kernel,
        grid_spec=grid_spec,
        out_shape=jax.ShapeDtypeStruct((p, d), BF16),
        compiler_params=_cp(("arbitrary",)),
        name="gmm",
    )(tile_expert, tile_first, tile_valid, xs, w_gate, w_up, w_down)


def _moe(h2, eidx2, w_gate, w_up, w_down, layer, tmm):
    t_tok = h2.shape[0]
    n = 2 * t_tok
    e_flat = eidx2.reshape(-1)
    onehot = (e_flat[:, None] == jnp.arange(N_EXPERTS, dtype=I32)[None, :]).astype(I32)
    csum = jnp.cumsum(onehot, axis=0)
    counts = csum[-1]
    ptiles = (counts + tmm - 1) // tmm
    tile_end = jnp.cumsum(ptiles)
    dest = jnp.sum(onehot * (csum - 1 + ((tile_end - ptiles) * tmm)[None, :]), axis=1)
    nt = n // tmm + N_EXPERTS
    src = (jnp.arange(nt * tmm, dtype=I32) % t_tok).at[dest].set(
        jnp.arange(n, dtype=I32) // 2, unique_indices=True, mode="promise_in_bounds")
    tid = jnp.arange(nt, dtype=I32)
    tile_valid = (tid < tile_end[-1]).astype(I32)
    te = jnp.minimum(jnp.sum((tile_end[None, :] <= tid[:, None]).astype(I32), axis=1), N_EXPERTS - 1)
    last_used = jnp.max(jnp.where(tile_valid == 1, te, 0))
    te = jnp.where(tile_valid == 1, te, last_used)
    tile_first = jnp.concatenate([jnp.ones((1,), I32), (te[1:] != te[:-1]).astype(I32)])
    xs = jnp.take(jnp.pad(h2, ((0, nt * tmm - t_tok), (0, 0))), src, axis=0, mode="clip")
    ys = _gmm(xs, w_gate, w_up, w_down, layer, te, tile_first, tile_valid, tmm)
    d2 = dest.reshape(t_tok, 2)
    return jnp.take(ys, d2[:, 0], axis=0, mode="clip"), jnp.take(ys, d2[:, 1], axis=0, mode="clip")


def _moe_combine(x, y1, y2, gates, m5):
    g = gates
    f = g[:, 0:1] * y1.astype(F32) + g[:, 1:2] * y2.astype(F32)
    return x + m5 * f


def _rope(x, cos, sin):
    n = x.shape[1]
    lane = lax.broadcasted_iota(I32, x.shape, 1)
    sw = jnp.where(lane % 32 < 16, pltpu.roll(x, n - 16, 1), pltpu.roll(x, 16, 1))
    reps = n // 128
    if reps > 1:
        cos = jnp.concatenate([cos] * reps, axis=1)
        sin = jnp.concatenate([sin] * reps, axis=1)
    return x * cos + sw * sin


def _inproj1_kernel(x_ref, y1_ref, y2_ref, ga_ref, m0_ref, m1_ref, g_ref, w_ref, cos_ref, sin_ref,
                    x1_ref, cq_ref, ckt_ref, cv_ref, dq_ref, dk_ref, dv_ref, *, tm, bs):
    x1 = _moe_combine(x_ref[...], y1_ref[...], y2_ref[...], ga_ref[...], m0_ref[0][5:6])
    x1_ref[...] = x1
    m = m1_ref[0]
    h = (_rms(x1, g_ref[...]) * (1.0 + m[1:2]) + m[0:1]).astype(BF16)
    in_lat = pl.program_id(0) * tm < bs
    cos, sin = cos_ref[...], sin_ref[...]
    scale = HEAD_DIM ** -0.5 * LOG2E
    cq = _dot(h, w_ref[:, 0:512])
    cq_ref[...] = (jnp.where(in_lat, _rope(cq, cos, sin), cq) * scale).astype(BF16)
    ck = _dot(h, w_ref[:, 512:640])
    ckt_ref[...] = jnp.where(in_lat, _rope(ck, cos, sin), ck).T.astype(BF16)
    cv_ref[...] = _dot(h, w_ref[:, 640:768]).astype(BF16)
    dq_ref[...] = (_dot(h, w_ref[:, 768:1280]) * scale).astype(BF16)
    dk_ref[...] = _dot(h, w_ref[:, 1280:1792]).astype(BF16)
    dv_ref[...] = _dot(h, w_ref[:, 1792:2304]).astype(BF16)


def _inproj1(x, y1, y2, gates, mods0, mods1, g, w, cos, sin, tm, bs, s, nb):
    r, d = x.shape
    kern = functools.partial(_inproj1_kernel, tm=tm, bs=bs)
    mi = functools.partial(_mod_index, tm=tm, bs=bs, s=s, nb=nb)
    row = lambda wd: pl.BlockSpec((tm, wd), lambda i: (i, 0))
    modspec = pl.BlockSpec((1, N_MOD, d), lambda i: (mi(i), 0, 0))
    tab = pl.BlockSpec((tm, 128), lambda i: (jnp.where(i * tm < bs, (i * tm % s) // tm, 0), 0))
    shp = lambda wd, dt: jax.ShapeDtypeStruct((r, wd), dt)
    return pl.pallas_call(
        kern,
        grid=(r // tm,),
        in_specs=[row(d), row(d), row(d), row(128), modspec, modspec,
                  pl.BlockSpec((1, d), lambda i: (0, 0)), pl.BlockSpec(w.shape, lambda i: (0, 0)), tab, tab],
        out_specs=[row(d), row(512), pl.BlockSpec((128, tm), lambda i: (0, i)), row(128), row(512), row(512),
                   row(512)],
        out_shape=[shp(d, F32), shp(512, BF16), jax.ShapeDtypeStruct((128, r), BF16), shp(128, BF16),
                   shp(512, BF16), shp(512, BF16), shp(512, BF16)],
        compiler_params=_cp(("parallel",)),
        name="inproj1",
    )(x, y1, y2, gates, mods0, mods1, g, w, cos, sin)


def _swa_kernel(q_ref, ktp_ref, ktc_ref, ktn_ref, ktx_ref, vp_ref, vc_ref, vn_ref, vx_ref, sink_ref, o_ref,
                *, nblk, cl):
    i = pl.program_id(1)
    wb = SWA_BLOCK
    nloc = 3 * wb
    kt = jnp.concatenate([ktp_ref[...], ktc_ref[...], ktn_ref[...], ktx_ref[...]], axis=1)
    vv = jnp.concatenate([vp_ref[...], vc_ref[...], vn_ref[...], vx_ref[...]], axis=0)
    a_i = lax.broadcasted_iota(I32, (2 * wb, nloc), 0) % wb
    c_i = lax.broadcasted_iota(I32, (2 * wb, nloc), 1)
    lo = jnp.where(i > 0, 0, wb)
    hi = jnp.where(i < nblk - 1, 3 * wb, 2 * wb)
    ok = (c_i >= a_i) & (c_i <= a_i + 2 * SWA_WINDOW) & (c_i >= lo) & (c_i < hi)
    half = lax.broadcasted_iota(I32, (1, 128), 1) // HEAD_DIM
    zero = jnp.zeros((), BF16)
    sink = sink_ref[...]
    def scores(g):
        q2 = q_ref[:, 128 * g:128 * g + 128]
        qst = jnp.concatenate([jnp.where(half == 0, q2, zero), jnp.where(half == 1, q2, zero)], axis=0)
        s_all = _dot(qst, kt)
        return jnp.concatenate([jnp.where(ok, s_all[:, 0:nloc], NEG), s_all[:, nloc:]], axis=1)

    def softmax(g, sc):
        sk = jnp.concatenate([jnp.broadcast_to(sink[0:1, 2 * g + a:2 * g + a + 1], (wb, 1)) for a in range(2)],
                             axis=0)
        m = jnp.maximum(jnp.max(sc, axis=-1, keepdims=True), sk)
        p = jnp.exp2(sc - m)
        return p.astype(BF16), jnp.sum(p, axis=-1, keepdims=True) + jnp.exp2(sk - m)

    def output(g, p, den):
        ost = _dot(p, vv) / den
        o_ref[:, 128 * g:128 * g + 128] = jnp.where(half == 0, ost[0:wb], ost[wb:2 * wb]).astype(BF16)

    sc, pr = {}, {}
    for step in range(6):
        if step < 4:
            sc[step] = scores(step)
        if 1 <= step < 5:
            pr[step - 1] = softmax(step - 1, sc.pop(step - 1))
        if step >= 2:
            output(step - 2, *pr.pop(step - 2))


def _swa(cq, ckt, cv, sink_row, nb, s, cl, bs):
    wb = SWA_BLOCK
    nblk = s // wb
    kern = functools.partial(_swa_kernel, nblk=nblk, cl=cl)
    prev = lambda b, i: b * nblk + jnp.maximum(i - 1, 0)
    own = lambda b, i: b * nblk + i
    nxt = lambda b, i: b * nblk + jnp.minimum(i + 1, nblk - 1)
    ktspec = lambda f: pl.BlockSpec((128, wb), lambda b, i: (0, f(b, i)))
    vspec = lambda f: pl.BlockSpec((wb, 128), lambda b, i: (f(b, i), 0))
    return pl.pallas_call(
        kern,
        grid=(nb, nblk),
        in_specs=[pl.BlockSpec((wb, 512), lambda b, i: (own(b, i), 0)),
                  ktspec(prev), ktspec(own), ktspec(nxt),
                  pl.BlockSpec((128, cl), lambda b, i: (0, bs // cl + b)),
                  vspec(prev), vspec(own), vspec(nxt),
                  pl.BlockSpec((cl, 128), lambda b, i: (bs // cl + b, 0)),
                  pl.BlockSpec((1, 128), lambda b, i: (0, 0))],
        out_specs=pl.BlockSpec((wb, 512), lambda b, i: (own(b, i), 0)),
        out_shape=jax.ShapeDtypeStruct((bs, 512), BF16),
        compiler_params=_cp(("parallel", "parallel")),
        name="swa",
    )(cq, ckt, ckt, ckt, ckt, cv, cv, cv, cv, sink_row)


def _na_kernel(q_ref, k_ref, v_ref, kx_ref, vx_ref, bias_ref, o_ref, *, rows, unroll):
    half = lax.broadcasted_iota(I32, (1, 128), 1) // HEAD_DIM
    zero = jnp.zeros((), BF16)
    kx, vx = kx_ref[...], vx_ref[...]
    span = NA_KH * GRID_W

    def scores(r):
        rs = jnp.clip(r - NA_KH // 2, 0, rows - NA_KH)
        off = rs - r + NA_KH - 1
        q0 = pl.multiple_of(r * GRID_W, GRID_W)
        k0 = pl.multiple_of(rs * GRID_W, GRID_W)
        q2 = q_ref[pl.ds(q0, GRID_W), :]
        qst = jnp.concatenate([jnp.where(half == 0, q2, zero), jnp.where(half == 1, q2, zero)], axis=0)
        s_loc = _dot_nt(qst, k_ref[pl.ds(k0, span), :]) + bias_ref[0, off].reshape(2 * GRID_W, span)
        return q0, k0, s_loc, _dot_nt(qst, kx)

    def softmax(s_loc, s_ctx):
        m = jnp.maximum(jnp.max(s_loc, axis=-1, keepdims=True), jnp.max(s_ctx, axis=-1, keepdims=True))
        p_loc = jnp.exp2(s_loc - m)
        p_ctx = jnp.exp2(s_ctx - m)
        den = jnp.sum(p_loc, axis=-1, keepdims=True) + jnp.sum(p_ctx, axis=-1, keepdims=True)
        return p_loc.astype(BF16), p_ctx.astype(BF16), den

    def body(i, carry):
        sc = [scores(i * unroll + j) for j in range(unroll)]
        pr = [softmax(s_loc, s_ctx) for (_, _, s_loc, s_ctx) in sc]
        for (q0, k0, _, _), (p_loc, p_ctx, den) in zip(sc, pr):
            ost = (_dot(p_loc, v_ref[pl.ds(k0, span), :]) + _dot(p_ctx, vx)) / den
            o = jnp.where(half == 0, ost[0:GRID_W], ost[GRID_W:2 * GRID_W])
            o_ref[pl.ds(q0, GRID_W), :] = o.astype(BF16)
        return carry

    lax.fori_loop(0, rows // unroll, body, 0)


def _na(dq, dk, dv, bias, nb, s, cl, bs):
    rows = s // GRID_W
    unroll = 4
    assert rows % unroll == 0
    kern = functools.partial(_na_kernel, rows=rows, unroll=unroll)
    seq = pl.BlockSpec((s, 128), lambda b, p: (b, p))
    ctx = pl.BlockSpec((cl, 128), lambda b, p: (bs // cl + b, p))
    return pl.pallas_call(
        kern,
        grid=(nb, NA_HEADS // 2),
        in_specs=[seq, seq, seq, ctx, ctx,
                  pl.BlockSpec((1, NA_KH, 2, GRID_W, NA_KH * GRID_W), lambda b, p: (p, 0, 0, 0, 0))],
        out_specs=seq,
        out_shape=jax.ShapeDtypeStruct((bs, 512), BF16),
        compiler_params=_cp(("parallel", "parallel")),
        name="na",
    )(dq, dk, dv, dk, dv, bias)


def _na_bias_table(rpb):
    c = np.arange(GRID_W)
    qs = np.clip(c - NA_KW // 2, 0, GRID_W - NA_KW)
    kc = np.arange(GRID_W)
    ok = (kc[None, :] >= qs[:, None]) & (kc[None, :] < qs[:, None] + NA_KW)
    dc = np.clip(kc[None, :] - c[:, None] + NA_KW - 1, 0, 2 * NA_KW - 2)
    sel = (np.arange(2 * NA_KW - 1)[:, None, None] == dc[None]).astype(np.float32)
    cols = jnp.einsum("hab,bck->hack", rpb.astype(F32), sel, precision=HI)
    cols = jnp.where(ok[None, None], cols * LOG2E, NEG)
    t = jnp.stack([cols[:, off:off + NA_KH] for off in range(NA_KH)], axis=1)
    t = jnp.transpose(t, (0, 1, 3, 2, 4)).reshape(NA_HEADS, NA_KH, GRID_W, NA_KH * GRID_W)
    t = t.reshape(NA_HEADS // 2, 2, NA_KH, GRID_W, NA_KH * GRID_W)
    return jnp.transpose(t, (0, 2, 1, 3, 4))


def _mix1_kernel(x_ref, oc_ref, od_ref, mod_ref, wo_ref, g2_ref, rw_ref, rb_ref,
                 xn_ref, h2_ref, ei_ref, ga_ref):
    y = _dot(oc_ref[...], wo_ref[0:512, :]) + _dot(od_ref[...], wo_ref[512:1024, :])
    xn, h2, eidx, gates = _post_mixer(x_ref[...], y, mod_ref[0], g2_ref[...], rw_ref[...], rb_ref[...])
    xn_ref[...] = xn
    h2_ref[...] = h2.astype(BF16)
    ei_ref[...] = eidx
    ga_ref[...] = gates


def _mix1(x, oc, od, mods, wo, g2, rw, rb, tm, bs, s):
    d = x.shape[1]
    row = lambda w: pl.BlockSpec((tm, w), lambda i: (i, 0))
    full = lambda a: pl.BlockSpec(a.shape, lambda i: (0,) * a.ndim)
    return pl.pallas_call(
        _mix1_kernel,
        grid=(bs // tm,),
        in_specs=[row(d), row(512), row(512), pl.BlockSpec((1, N_MOD, d), lambda i: (i * tm // s, 0, 0)),
                  full(wo), full(g2), full(rw), full(rb)],
        out_specs=[row(d), row(d), row(128), row(128)],
        out_shape=[jax.ShapeDtypeStruct((bs, d), F32), jax.ShapeDtypeStruct((bs, d), BF16),
                   jax.ShapeDtypeStruct((bs, 128), I32), jax.ShapeDtypeStruct((bs, 128), F32)],
        compiler_params=_cp(("parallel",)),
        name="mix1",
    )(x, oc, od, mods, wo, g2, rw, rb)


def _final_kernel(x_ref, y1_ref, y2_ref, ga_ref, mod_ref, g_ref, o_ref):
    x = _moe_combine(x_ref[...], y1_ref[...], y2_ref[...], ga_ref[...], mod_ref[0][5:6])
    o_ref[...] = _rms(x, g_ref[...])


def _final(x, y1, y2, gates, mods, g, tm, s):
    r, d = x.shape
    row = lambda w: pl.BlockSpec((tm, w), lambda i: (i, 0))
    return pl.pallas_call(
        _final_kernel,
        grid=(r // tm,),
        in_specs=[row(d), row(d), row(d), row(128), pl.BlockSpec((1, N_MOD, d), lambda i: (i * tm // s, 0, 0)),
                  pl.BlockSpec((1, d), lambda i: (0, 0))],
        out_specs=row(d),
        out_shape=jax.ShapeDtypeStruct((r, d), F32),
        compiler_params=_cp(("parallel",)),
        name="final",
    )(x, y1, y2, gates, mods, g)


def _rope_tables(s):
    nf = HEAD_DIM // 4
    t = np.arange(s)
    inv = ROPE_THETA ** (-np.arange(nf, dtype=np.float64) / nf)
    ar = (t // GRID_W)[:, None] * inv
    ac = (t % GRID_W)[:, None] * inv
    cos = np.concatenate([np.cos(ar), np.cos(ar), np.cos(ac), np.cos(ac)], axis=1)
    sin = np.concatenate([-np.sin(ar), np.sin(ar), -np.sin(ac), np.sin(ac)], axis=1)
    return (jnp.asarray(np.concatenate([cos, cos], axis=1), F32),
            jnp.asarray(np.concatenate([sin, sin], axis=1), F32))


def kernel(x, c, ctx, c_ctx, ada_w, ada_b, norm1_g, norm2_g, ev_w_in, ev_w_out, sc_conv_w, dn_conv_w, dn_a_log, dn_dt_bias, dn_onorm_g, od_w_in, od_w_out, swa_sink, na_rpb, router_w, router_b, moe_w_gate, moe_w_up, moe_w_down, final_g):
    nb, s, d = x.shape
    cl = ctx.shape[1]
    bs = nb * s
    tm = 512
    ts = 256
    tmm = 512
    assert d == 1024 and s % tm == 0 and (nb * cl) % tm == 0 and cl % ts == 0 and s % ts == 0
    assert s // GRID_W >= NA_KH and bs % cl == 0 and nb + 1 <= 8

    xf = jnp.concatenate([x.reshape(bs, d), ctx.reshape(nb * cl, d)], axis=0)
    cc = jnp.zeros((8, d), F32).at[:nb].set(c).at[nb].set(c_ctx)
    mods = _ada(cc, ada_w, ada_b).reshape(ada_w.shape[0], 8, N_MOD, d)
    rw32 = jnp.pad(router_w, ((0, 0), (0, 128 - N_EXPERTS)))
    rw_hi = rw32.astype(BF16)
    rw = jnp.concatenate([rw_hi, (rw32 - rw_hi.astype(F32)).astype(BF16)], axis=1)
    rb = jnp.pad(router_b, (0, 128 - N_EXPERTS)).reshape(1, 128)
    row = lambda v: v.reshape(1, -1)

    w_in0 = jnp.pad(ev_w_in[0], ((0, 0), (0, 3712 - ev_w_in.shape[-1]))).astype(BF16)
    sc, qkv, z, bg = _inproj0(xf, mods[0], row(norm1_g[0]), w_in0, tm, bs, s, nb)
    pad16 = lambda v: jnp.pad(v.reshape(-1), (8, 128 - 16)).reshape(1, 128)
    q, k, v, bga = _dnprep(qkv, bg, dn_conv_w[0], pad16(dn_a_log[0]), pad16(dn_dt_bias[0]), ts, bs, s, cl)
    uf, ub, wf, wb, qf, qb, kf, kb, af, ab, gc = _dnchunk(q, k, v, bga)
    of, ob = _dnscan(uf, ub, wf, wb, qf, qb, kf, kb, af, ab, gc, nb, s, cl, bs)
    x0, h2, ei, ga = _mix0(xf, sc, of, ob, z, mods[0], sc_conv_w[0], row(dn_onorm_g[0]),
                           ev_w_out[0].astype(BF16), row(norm2_g[0]), rw, rb, ts, bs, s, cl, nb)
    y1, y2 = _moe(h2, ei[:, 0:2], moe_w_gate, moe_w_up, moe_w_down, 0, tmm)

    perm = np.concatenate([np.arange(HEAD_DIM) + HEAD_DIM * (g + 4 * a) for g in range(4) for a in range(2)])
    w1 = od_w_in[0]
    w_in1 = jnp.concatenate([w1[:, 0:512][:, perm], w1[:, 512:]], axis=1).astype(BF16)
    wo1 = od_w_out[0]
    w_out1 = jnp.concatenate([wo1[0:512][perm], wo1[512:]], axis=0).astype(BF16)
    sink_row = jnp.pad(swa_sink[0][np.array([g + 4 * a for g in range(4) for a in range(2)])] * LOG2E,
                       (0, 128 - SWA_HEADS)).reshape(1, 128)
    cos, sin = _rope_tables(s)
    x1, cq, ckt, cv, dq, dk, dv = _inproj1(x0, y1, y2, ga, mods[0], mods[1], row(norm1_g[1]), w_in1, cos, sin,
                                           tm, bs, s, nb)
    oc = _swa(cq, ckt, cv, sink_row, nb, s, cl, bs)
    od = _na(dq, dk, dv, _na_bias_table(na_rpb[0]), nb, s, cl, bs)
    x2, h2, ei, ga = _mix1(x1, oc, od, mods[1], w_out1, row(norm2_g[1]), rw, rb, tm, bs, s)
    y1, y2 = _moe(h2, ei[:, 0:2], moe_w_gate, moe_w_up, moe_w_down, 1, tmm)
    out = _final(x2, y1, y2, ga, mods[1], row(final_g), tm, s)
    return out.reshape(nb, s, d)
```

```python
import functools
import math

import numpy as np
import jax
import jax.numpy as jnp
from jax import lax
from jax.experimental import pallas as pl
from jax.experimental.pallas import tpu as pltpu

F32 = jnp.float32
BF16 = jnp.bfloat16
I32 = jnp.int32
HI = lax.Precision.HIGHEST

EPS = 1e-6
N_MOD = 6
GRID_W = 64
HEAD_DIM = 64
DN_HEADS = 4
DN_HD = 128
DN_CHUNK = 64
SWA_HEADS = 8
SWA_KV = 2
SWA_BLOCK = 128
SWA_WINDOW = 128
NA_HEADS = 8
NA_KH = 8
NA_KW = 16
ROPE_THETA = 10000.0
N_EXPERTS = 16
N_GROUPS = 4
NEG = -1e30
LOG2E = 1.4426950408889634
VMEM_LIMIT = 56 * 1024 * 1024


def _cp(sem, vmem=VMEM_LIMIT):
    return pltpu.CompilerParams(dimension_semantics=sem, vmem_limit_bytes=vmem)


def _dot(a, b, precision=None):
    return jnp.dot(a, b, preferred_element_type=F32, precision=precision)


def _dot_nt(a, b, precision=None):
    return lax.dot_general(a, b, (((1,), (1,)), ((), ())), preferred_element_type=F32, precision=precision)


def _dot_tn(a, b, precision=None):
    return lax.dot_general(a, b, (((0,), (0,)), ((), ())), preferred_element_type=F32, precision=precision)


def _silu(x):
    return x * jax.nn.sigmoid(x)


def _rms(x, g):
    return x * lax.rsqrt(jnp.mean(x * x, axis=-1, keepdims=True) + EPS) * g


def _ada_kernel(cc_ref, w_ref, b_ref, o_ref):
    a = _silu(cc_ref[...])
    o_ref[0] = _dot(a, w_ref[0], HI) + b_ref[0]


def _ada(cc, ada_w, ada_b):
    depth, d, n = ada_w.shape
    tn = 1536
    return pl.pallas_call(
        _ada_kernel,
        grid=(depth, n // tn),
        in_specs=[pl.BlockSpec((8, d), lambda l, j: (0, 0)),
                  pl.BlockSpec((1, d, tn), lambda l, j: (l, 0, j)),
                  pl.BlockSpec((1, 1, tn), lambda l, j: (l, 0, j))],
        out_specs=pl.BlockSpec((1, 8, tn), lambda l, j: (l, 0, j)),
        out_shape=jax.ShapeDtypeStruct((depth, 8, n), F32),
        compiler_params=_cp(("parallel", "parallel")),
        name="ada",
    )(cc, ada_w, ada_b.reshape(depth, 1, n))


def _mod_index(i, tm, bs, s, nb):
    row0 = i * tm
    return jnp.where(row0 < bs, row0 // s, nb)


def _token_specs(tm, bs, d):
    nlat = bs // tm
    return [pl.BlockSpec((tm, d), lambda i: (jnp.minimum(i, nlat - 1), 0)),
            pl.BlockSpec((tm, d), lambda i: (jnp.maximum(i - nlat, 0), 0))]


def _token_rows(xl_ref, xc_ref, tm, bs):
    return jnp.where(pl.program_id(0) * tm < bs, xl_ref[...], xc_ref[...])


def _inproj0_kernel(xl_ref, xc_ref, mod_ref, g_ref, w_ref, sc_ref, qkv_ref, z_ref, bg_ref, *, tm, bs):
    m = mod_ref[0]
    h = (_rms(_token_rows(xl_ref, xc_ref, tm, bs), g_ref[...]) * (1.0 + m[1:2]) + m[0:1]).astype(BF16)
    sc_ref[...] = _dot(h, w_ref[:, 0:1536])
    qkv_ref[...] = _dot(h, w_ref[:, 1536:3072])
    z_ref[...] = _dot(h, w_ref[:, 3072:3584])
    bg_ref[...] = _dot(h, w_ref[:, 3584:3712])


def _inproj0(xl, xc, mods, g, w, tm, bs, s, nb):
    d = xl.shape[1]
    r = bs + xc.shape[0]
    mi = functools.partial(_mod_index, tm=tm, bs=bs, s=s, nb=nb)
    return pl.pallas_call(
        functools.partial(_inproj0_kernel, tm=tm, bs=bs),
        grid=(r // tm,),
        in_specs=_token_specs(tm, bs, d) + [
            pl.BlockSpec((1, N_MOD, d), lambda i: (mi(i), 0, 0)),
            pl.BlockSpec((1, d), lambda i: (0, 0)),
            pl.BlockSpec(w.shape, lambda i: (0, 0))],
        out_specs=[pl.BlockSpec((tm, 1536), lambda i: (i, 0)),
                   pl.BlockSpec((tm, 1536), lambda i: (i, 0)),
                   pl.BlockSpec((tm, 512), lambda i: (i, 0)),
                   pl.BlockSpec((tm, 128), lambda i: (i, 0))],
        out_shape=[jax.ShapeDtypeStruct((r, 1536), F32), jax.ShapeDtypeStruct((r, 1536), F32),
                   jax.ShapeDtypeStruct((r, 512), F32), jax.ShapeDtypeStruct((r, 128), F32)],
        compiler_params=_cp(("parallel",)),
        name="inproj0",
    )(xl, xc, mods, g, w)


def _seq_edges(i, ts, bs, s, cl):
    row0 = i * ts
    in_lat = row0 < bs
    r_in = jnp.where(in_lat, row0 % s, (row0 - bs) % cl)
    seqlen = jnp.where(in_lat, s, cl)
    return r_in == 0, r_in + ts == seqlen


def _shifted(x, prev_row, next_row):
    n = x.shape[0]
    rows = lax.broadcasted_iota(I32, x.shape, 0)
    xp = jnp.where(rows == 0, prev_row, pltpu.roll(x, 1, 0))
    xn = jnp.where(rows == n - 1, next_row, pltpu.roll(x, n - 1, 0))
    return xp, xn


def _halo_specs(ts, width, r):
    nb8 = r // 8
    k = ts // 8
    return [pl.BlockSpec((8, width), lambda i: (jnp.maximum(i * k - 1, 0), 0)),
            pl.BlockSpec((8, width), lambda i: (jnp.minimum((i + 1) * k, nb8 - 1), 0))]


def _dnprep_kernel(x_ref, prev_ref, next_ref, bg_ref, cw_ref, alog_ref, dt_ref,
                   q_ref, k_ref, v_ref, bga_ref, *, ts, bs, s, cl):
    first, last = _seq_edges(pl.program_id(0), ts, bs, s, cl)
    for c in range(12):
        sl = slice(128 * c, 128 * c + 128)
        x = x_ref[:, sl]
        pr = jnp.where(first, 0.0, prev_ref[7:8, sl])
        nx = jnp.where(last, 0.0, next_ref[0:1, sl])
        xp, xn = _shifted(x, pr, nx)
        w = cw_ref[:, sl]
        y = _silu(xp * w[0:1] + x * w[1:2] + xn * w[2:3])
        hs = slice(128 * (c % 4), 128 * (c % 4) + 128)
        if c < 8:
            y = y * lax.rsqrt(jnp.sum(y * y, axis=-1, keepdims=True) + EPS)
        if c < 4:
            q_ref[:, hs] = y * DN_HD ** -0.5
        elif c < 8:
            k_ref[:, hs] = y
        else:
            v_ref[:, hs] = y
    b = bg_ref[...]
    cols = lax.broadcasted_iota(I32, b.shape, 1)
    beta = jax.nn.sigmoid(b)
    t = b + dt_ref[...]
    softplus = jnp.maximum(t, 0.0) + jnp.log1p(jnp.exp(-jnp.abs(t)))
    g = -jnp.exp(alog_ref[...]) * softplus
    bga_ref[...] = jnp.where(cols < 8, beta, jnp.where(cols < 16, g, 0.0))


def _dnprep(qkv, bg, cw, alog_row, dt_row, ts, bs, s, cl):
    r = qkv.shape[0]
    kern = functools.partial(_dnprep_kernel, ts=ts, bs=bs, s=s, cl=cl)
    return pl.pallas_call(
        kern,
        grid=(r // ts,),
        in_specs=[pl.BlockSpec((ts, 1536), lambda i: (i, 0))] + _halo_specs(ts, 1536, r) + [
            pl.BlockSpec((ts, 128), lambda i: (i, 0)),
            pl.BlockSpec((3, 1536), lambda i: (0, 0)),
            pl.BlockSpec((1, 128), lambda i: (0, 0)),
            pl.BlockSpec((1, 128), lambda i: (0, 0))],
        out_specs=[pl.BlockSpec((ts, 512), lambda i: (i, 0))] * 3 + [pl.BlockSpec((ts, 128), lambda i: (i, 0))],
        out_shape=[jax.ShapeDtypeStruct((r, 512), F32)] * 3 + [jax.ShapeDtypeStruct((r, 128), F32)],
        compiler_params=_cp(("parallel",)),
        name="dnprep",
    )(qkv, qkv, qkv, bg, cw, alog_row, dt_row)


def _dnchunk_kernel(q_ref, k_ref, v_ref, bg_ref,
                    uf_ref, ub_ref, wf_ref, wb_ref, qf_ref, qb_ref, kf_ref, kb_ref, af_ref, ab_ref, gc_ref, *, nchunks):
    outs = ((uf_ref, wf_ref, qf_ref, kf_ref, af_ref), (ub_ref, wb_ref, qb_ref, kb_ref, ab_ref))
    n = DN_CHUNK * DN_HEADS
    chains = []
    for cc in range(nchunks):
        chains += _dnchunk_setup(slice(cc * DN_CHUNK, (cc + 1) * DN_CHUNK), q_ref, k_ref, v_ref, bg_ref, gc_ref)
    for ch in chains:
        ch["tm"] = ch["nmat"]
        nb16 = ch["nmat"].astype(BF16)
        ch["npow"] = _dot(nb16, nb16)
    for _ in range(4):
        for ch in chains:
            nb16 = ch["npow"].astype(BF16)
            ch["both"] = _dot(jnp.concatenate([nb16, ch["tm"].astype(BF16)], axis=0), nb16)
        for ch in chains:
            ch["tm"] = ch["tm"] + ch["npow"] + ch["both"][n:2 * n]
            ch["npow"] = ch["both"][0:n]
    for ch in chains:
        ch["both"] = _dot(ch["tm"].astype(BF16), ch["npow"].astype(BF16))
    for ch in chains:
        ch["tm"] = ch["tm"] + ch["npow"] + ch["both"]
    for ch in chains:
        ch["uw"] = ch["rhs"] + _dot(ch["tm"].astype(BF16), ch["rhs"].astype(BF16))
    for ch in chains:
        u_ref, w_ref, qd_ref, kd_ref, at_ref = outs[ch["d"]]
        rows, uw, c = ch["rows"], ch["uw"], DN_CHUNK
        for h in range(DN_HEADS):
            hs = slice(DN_HD * h, DN_HD * h + DN_HD)
            rs = slice(c * h, c * h + c)
            u_ref[rows, hs] = uw[rs, 0:DN_HD]
            w_ref[rows, hs] = uw[rs, DN_HD:2 * DN_HD].astype(BF16)
            qd_ref[rows, hs] = ch["qd"][rs]
        kd_ref[slice(2 * rows.start, 2 * rows.stop)] = ch["kd"].T.astype(BF16)
        att = ch["att"]
        at_ref[rows] = (att[0:c] + att[c:2 * c] + att[2 * c:3 * c] + att[3 * c:4 * c]).astype(BF16)


def _dnchunk_setup(rows, q_ref, k_ref, v_ref, bg_ref, gc_ref):
    c, nh = DN_CHUNK, DN_HEADS
    n = c * nh
    bg = bg_ref[rows]
    i64 = lax.broadcasted_iota(I32, (c, c), 0)
    j64 = lax.broadcasted_iota(I32, (c, c), 1)
    cols = lax.broadcasted_iota(I32, bg.shape, 1)
    gcf = _dot((i64 >= j64).astype(F32), bg, HI)
    gcb = _dot((i64 <= j64).astype(F32), bg, HI)
    gc = jnp.where(cols >= 12, gcb, gcf)
    gc_ref[rows] = gc
    gct = gc.T
    ii = lax.broadcasted_iota(I32, (n, n), 0)
    jj = lax.broadcasted_iota(I32, (n, n), 1)
    same = (ii // c) == (jj // c)

    def stack(ref):
        return jnp.concatenate([ref[rows, DN_HD * h:DN_HD * h + DN_HD] for h in range(nh)], axis=0)

    def col_bcast(arr, r0, r1, col0):
        return jnp.concatenate([jnp.broadcast_to(arr[r0:r1, col0 + h:col0 + h + 1], (c, n)) for h in range(nh)],
                               axis=0)

    kst, qst, vst = stack(k_ref), stack(q_ref), stack(v_ref)
    kb = kst.astype(BF16)
    kq = _dot_nt(jnp.concatenate([kb, qst.astype(BF16)], axis=0), kb)
    chains = []
    for d in range(2):
        incl = same & ((ii >= jj) if d == 0 else (ii <= jj))
        strict = same & ((ii > jj) if d == 0 else (ii < jj))
        last = c - 1 if d == 0 else 0
        beta = col_bcast(bg, 0, c, 4 * d)
        gcol = col_bcast(gc, 0, c, 8 + 4 * d)
        glast = col_bcast(gc, last, last + 1, 8 + 4 * d)
        grow = jnp.concatenate([gct[8 + 4 * d + h:9 + 4 * d + h, :] for h in range(nh)], axis=1)
        decay = jnp.exp(jnp.where(incl, gcol - grow, NEG))
        nmat = jnp.where(strict, -(beta * kq[0:n] * decay), 0.0)
        eg = jnp.exp(gcol)
        b1, e1 = beta[:, 0:DN_HD], eg[:, 0:DN_HD]
        chains.append(dict(
            d=d, rows=rows, nmat=nmat,
            rhs=jnp.concatenate([b1 * vst, (b1 * e1) * kst], axis=1),
            qd=(qst * e1).astype(BF16),
            kd=kst * jnp.exp(glast - gcol)[:, 0:DN_HD],
            att=kq[n:2 * n] * decay))
    return chains


def _dnchunk(q, k, v, bga):
    r = q.shape[0]
    nchunks = 4
    c = DN_CHUNK * nchunks
    row = lambda w: pl.BlockSpec((c, w), lambda i: (i, 0))
    shp = lambda w, dt: jax.ShapeDtypeStruct((r, w), dt)
    return pl.pallas_call(
        functools.partial(_dnchunk_kernel, nchunks=nchunks),
        grid=(r // c,),
        in_specs=[row(512), row(512), row(512), row(128)],
        out_specs=[row(512)] * 6 + [pl.BlockSpec((2 * c, 256), lambda i: (i, 0))] * 2 + [row(256), row(256), row(128)],
        out_shape=([shp(512, F32)] * 2 + [shp(512, BF16)] * 4 + [jax.ShapeDtypeStruct((2 * r, 256), BF16)] * 2
                   + [shp(256, BF16)] * 2 + [shp(128, F32)]),
        compiler_params=_cp(("parallel",)),
        name="dnchunk",
    )(q, k, v, bga)


def _dnscan_kernel(uf_ref, wf_ref, qf_ref, kf_ref, af_ref, gf_ref,
                   ub_ref, wb_ref, qb_ref, kb_ref, ab_ref, gb_ref,
                   of_ref, ob_ref, s_ref, *, nsub):
    @pl.when(pl.program_id(1) == 0)
    def _():
        s_ref[...] = jnp.zeros_like(s_ref)

    c, nh = DN_CHUNK, DN_HEADS
    head_of_lane = lax.broadcasted_iota(I32, (1, nh * DN_HD), 1) // DN_HD
    head_of_col = lax.broadcasted_iota(I32, (1, nh * c), 1) // c
    zero = jnp.zeros((), BF16)

    def block_diag(tile, head_ids):
        return jnp.concatenate([jnp.where(head_ids == h, tile, zero) for h in range(nh)], axis=0)

    dirs = ((uf_ref, wf_ref, qf_ref, kf_ref, af_ref, gf_ref, of_ref, c - 1),
            (ub_ref, wb_ref, qb_ref, kb_ref, ab_ref, gb_ref, ob_ref, 0))
    states = [s_ref[0], s_ref[1]]
    for sub in range(nsub):
        work = []
        for d, (u_ref, w_ref, qd_ref, kd_ref, at_ref, g_ref, o_ref, last) in enumerate(dirs):
            k = sub if d == 0 else nsub - 1 - sub
            rows = slice(c * k, c * k + c)
            g = g_ref[rows]
            decay = jnp.concatenate(
                [jnp.broadcast_to(jnp.exp(g[last:last + 1, 8 + 4 * d + h:9 + 4 * d + h]), (DN_HD, DN_HD))
                 for h in range(nh)], axis=0)
            ust = jnp.concatenate([u_ref[rows, DN_HD * h:DN_HD * h + DN_HD] for h in range(nh)], axis=0)
            stb = states[d].astype(BF16)
            vnew = ust - _dot(block_diag(w_ref[rows], head_of_lane), stb)
            work.append((rows, decay, stb, vnew.astype(BF16)))
        for d, (u_ref, w_ref, qd_ref, kd_ref, at_ref, g_ref, o_ref, last) in enumerate(dirs):
            rows, decay, stb, vnb = work[d]
            o = (_dot(block_diag(qd_ref[rows], head_of_lane), stb)
                 + _dot(block_diag(at_ref[rows], head_of_col), vnb))
            kdt = kd_ref[slice(2 * rows.start, 2 * rows.stop)]
            states[d] = states[d] * decay + _dot(block_diag(kdt, head_of_col), vnb)
            for h in range(nh):
                o_ref[rows, DN_HD * h:DN_HD * h + DN_HD] = o[c * h:c * h + c]
    s_ref[0] = states[0]
    s_ref[1] = states[1]


def _dnscan(uf, ub, wf, wb, qf, qb, kf, kb, af, ab, gc, nb, s, cl, bs):
    r = uf.shape[0]
    nsub = 2
    c = DN_CHUNK * nsub
    assert cl % c == 0 and s % c == 0 and bs % c == 0
    ncc, ncl = cl // c, s // c
    ns = ncc + ncl

    def fwd(b, t):
        return jnp.where(t < ncc, bs // c + b * ncc + t, b * ncl + t - ncc)

    def bwd(b, t):
        return jnp.where(t < ncc, bs // c + b * ncc + (ncc - 1 - t), b * ncl + (ncl - 1 - (t - ncc)))

    def specs(idx):
        blk = lambda w: pl.BlockSpec((c, w), lambda b, t: (idx(b, t), 0))
        return [blk(512), blk(512), blk(512), pl.BlockSpec((2 * c, 256), lambda b, t: (idx(b, t), 0)), blk(256),
                blk(128)]

    return pl.pallas_call(
        functools.partial(_dnscan_kernel, nsub=nsub),
        grid=(nb, ns),
        in_specs=specs(fwd) + specs(bwd),
        out_specs=[pl.BlockSpec((c, 512), lambda b, t: (fwd(b, t), 0)),
                   pl.BlockSpec((c, 512), lambda b, t: (bwd(b, t), 0))],
        out_shape=[jax.ShapeDtypeStruct((r, 512), F32)] * 2,
        scratch_shapes=[pltpu.VMEM((2, DN_HEADS * DN_HD, DN_HD), F32)],
        compiler_params=_cp(("arbitrary", "arbitrary")),
        name="dnscan",
    )(uf, wf, qf, kf, af, gc, ub, wb, qb, kb, ab, gc)


def _route(logits, bias_col):
    epg = N_EXPERTS // N_GROUPS
    tm = logits.shape[0]
    scores = jax.nn.sigmoid(logits.T[0:N_EXPERTS])
    gsel = scores + bias_col
    row = lambda a, k: a[k:k + 1]
    best = gidx = None
    for g in range(N_GROUPS):
        a = [row(gsel, epg * g + k) for k in range(epg)]
        m01, n01 = jnp.maximum(a[0], a[1]), jnp.minimum(a[0], a[1])
        m23, n23 = jnp.maximum(a[2], a[3]), jnp.minimum(a[2], a[3])
        gs = jnp.maximum(m01, m23) + jnp.maximum(jnp.minimum(m01, m23), jnp.maximum(n01, n23))
        if g == 0:
            best, gidx = gs, jnp.zeros_like(gs)
        else:
            better = gs > best
            best = jnp.where(better, gs, best)
            gidx = jnp.where(better, float(g), gidx)
    sel = [None] * epg
    raw = [None] * epg
    for g in range(N_GROUPS):
        for k in range(epg):
            v, u = row(gsel, epg * g + k), row(scores, epg * g + k)
            sel[k] = v if g == 0 else jnp.where(gidx == g, v, sel[k])
            raw[k] = u if g == 0 else jnp.where(gidx == g, u, raw[k])
    v1, e1, w1 = sel[0], jnp.zeros_like(gidx), raw[0]
    for k in range(1, epg):
        better = sel[k] > v1
        v1 = jnp.where(better, sel[k], v1)
        e1 = jnp.where(better, float(k), e1)
        w1 = jnp.where(better, raw[k], w1)
    v2 = e2 = w2 = None
    for k in range(epg):
        cand = jnp.where(e1 == k, -jnp.inf, sel[k])
        if k == 0:
            v2, e2, w2 = cand, jnp.zeros_like(gidx), raw[0]
        else:
            better = cand > v2
            v2 = jnp.where(better, cand, v2)
            e2 = jnp.where(better, float(k), e2)
            w2 = jnp.where(better, raw[k], w2)
    tot = w1 + w2
    eidx = jnp.concatenate([gidx * epg + e1, gidx * epg + e2, jnp.zeros((6, tm), F32)], axis=0).astype(I32)
    gates_t = jnp.concatenate([w1 / tot, w2 / tot, jnp.zeros((126, tm), F32)], axis=0)
    return eidx, gates_t.T


def _post_mixer(x, y, m, g2, rw, rb):
    xn = x + m[2:3] * y
    h2 = _rms(xn, g2) * (1.0 + m[4:5]) + m[3:4]
    hi = h2.astype(BF16)
    lo = (h2 - hi.astype(F32)).astype(BF16)
    hw = _dot(hi, rw)
    logits = hw[:, 0:128] + (hw[:, 128:256] + _dot(lo, rw[:, 0:128]))
    eidx, gates = _route(logits, rb)
    return xn, hi, eidx, gates


def _mix0_kernel(xl_ref, xc_ref, sc_ref, prev_ref, next_ref, of_ref, ob_ref, z_ref, mod_ref, cw_ref, on_ref, wo_ref,
                 g2_ref, rw_ref, rb_ref, xn_ref, h2_ref, ei_ref, ga_ref, *, ts, bs, s, cl):
    first, last = _seq_edges(pl.program_id(0), ts, bs, s, cl)
    ya = []
    for c in range(4):
        sl = slice(128 * c, 128 * c + 128)
        sg = slice(512 + 128 * c, 512 + 128 * c + 128)
        sx = slice(1024 + 128 * c, 1024 + 128 * c + 128)
        u = sc_ref[:, sg] * sc_ref[:, sx]
        pr = jnp.where(first, 0.0, prev_ref[7:8, sg] * prev_ref[7:8, sx])
        nx = jnp.where(last, 0.0, next_ref[0:1, sg] * next_ref[0:1, sx])
        up, un = _shifted(u, pr, nx)
        w = cw_ref[:, sl]
        ya.append((sc_ref[:, sl] * (up * w[0:1] + u * w[1:2] + un * w[2:3])).astype(BF16))
    yb = []
    for h in range(DN_HEADS):
        hs = slice(DN_HD * h, DN_HD * h + DN_HD)
        o = of_ref[:, hs] + ob_ref[:, hs]
        yb.append((_rms(o, on_ref[...]) * _silu(z_ref[:, hs])).astype(BF16))
    ycat = jnp.concatenate(ya + yb, axis=1)
    y = _dot(ycat, wo_ref[...])
    xn, h2, eidx, gates = _post_mixer(_token_rows(xl_ref, xc_ref, ts, bs), y, mod_ref[0], g2_ref[...], rw_ref[...],
                                      rb_ref[...])
    xn_ref[...] = xn
    h2_ref[...] = h2.astype(BF16)
    ei_ref[...] = eidx
    ga_ref[...] = gates


def _mix0(xl, xc, sc, of, ob, z, mods, cw, on, wo, g2, rw, rb, ts, bs, s, cl, nb):
    d = xl.shape[1]
    r = bs + xc.shape[0]
    kern = functools.partial(_mix0_kernel, ts=ts, bs=bs, s=s, cl=cl)
    mi = functools.partial(_mod_index, tm=ts, bs=bs, s=s, nb=nb)
    row = lambda w: pl.BlockSpec((ts, w), lambda i: (i, 0))
    full = lambda a: pl.BlockSpec(a.shape, lambda i: (0,) * a.ndim)
    return pl.pallas_call(
        kern,
        grid=(r // ts,),
        in_specs=_token_specs(ts, bs, d) + [row(1536)] + _halo_specs(ts, 1536, r) + [
            row(512), row(512), row(512),
            pl.BlockSpec((1, N_MOD, d), lambda i: (mi(i), 0, 0)),
            full(cw), full(on), full(wo), full(g2), full(rw), full(rb)],
        out_specs=[row(d), row(d), pl.BlockSpec((8, ts), lambda i: (0, i)), row(128)],
        out_shape=[jax.ShapeDtypeStruct((r, d), F32), jax.ShapeDtypeStruct((r, d), BF16),
                   jax.ShapeDtypeStruct((8, r), I32), jax.ShapeDtypeStruct((r, 128), F32)],
        compiler_params=_cp(("parallel",)),
        name="mix0",
    )(xl, xc, sc, sc, sc, of, ob, z, mods, cw, on, wo, g2, rw, rb)


def _gmm_kernel(te_ref, tf_ref, tv_ref, x_ref, wg_ref, wu_ref, wd_ref, y_ref, wgb, wub, wdb):
    t = pl.program_id(0)

    @pl.when(tf_ref[t] == 1)
    def _():
        wgb[...] = wg_ref[0, 0].astype(BF16)
        wub[...] = wu_ref[0, 0].astype(BF16)
        wdb[...] = wd_ref[0, 0].astype(BF16)

    @pl.when(tv_ref[t] == 1)
    def _():
        x = x_ref[...]
        a = (_silu(_dot(x, wgb[...])) * _dot(x, wub[...])).astype(BF16)
        y_ref[...] = _dot(a, wdb[...]).astype(BF16)

    @pl.when(tv_ref[t] == 0)
    def _():
        y_ref[...] = jnp.zeros_like(y_ref)


def _gmm(xs, w_gate, w_up, w_down, layer, tile_expert, tile_first, tile_valid, tmm):
    p, d = xs.shape
    de = w_gate.shape[-1]
    nt = p // tmm
    grid_spec = pltpu.PrefetchScalarGridSpec(
        num_scalar_prefetch=3,
        grid=(nt,),
        in_specs=[pl.BlockSpec((tmm, d), lambda t, te, tf, tv: (t, 0)),
                  pl.BlockSpec((1, 1, d, de), lambda t, te, tf, tv: (layer, te[t], 0, 0)),
                  pl.BlockSpec((1, 1, d, de), lambda t, te, tf, tv: (layer, te[t], 0, 0)),
                  pl.BlockSpec((1, 1, de, d), lambda t, te, tf, tv: (layer, te[t], 0, 0))],
        out_specs=pl.BlockSpec((tmm, d), lambda t, te, tf, tv: (t, 0)),
        scratch_shapes=[pltpu.VMEM((d, de), BF16), pltpu.VMEM((d, de), BF16), pltpu.VMEM((de, d), BF16)],
    )
    return pl.pallas_call(
        _gmm_kernel,
        grid_spec=grid_spec,
        out_shape=jax.ShapeDtypeStruct((p, d), BF16),
        compiler_params=_cp(("arbitrary",)),
        name="gmm",
    )(tile_expert, tile_first, tile_valid, xs, w_gate, w_up, w_down)


def _moe(h2, e_first, e_second, w_gate, w_up, w_down, layer, tmm):
    t_tok = h2.shape[0]
    n = 2 * t_tok
    e_flat = jnp.concatenate([e_first, e_second])
    onehot = (e_flat[:, None] == jnp.arange(N_EXPERTS, dtype=I32)[None, :]).astype(I32)
    csum = jnp.cumsum(onehot, axis=0)
    counts = csum[-1]
    ptiles = (counts + tmm - 1) // tmm
    tile_end = jnp.cumsum(ptiles)
    dest = jnp.sum(onehot * (csum - 1 + ((tile_end - ptiles) * tmm)[None, :]), axis=1)
    nt = n // tmm + N_EXPERTS
    src = (jnp.arange(nt * tmm, dtype=I32) % t_tok).at[dest].set(
        jnp.arange(n, dtype=I32) % t_tok, unique_indices=True, mode="promise_in_bounds")
    tid = jnp.arange(nt, dtype=I32)
    tile_valid = (tid < tile_end[-1]).astype(I32)
    te = jnp.minimum(jnp.sum((tile_end[None, :] <= tid[:, None]).astype(I32), axis=1), N_EXPERTS - 1)
    last_used = jnp.max(jnp.where(tile_valid == 1, te, 0))
    te = jnp.where(tile_valid == 1, te, last_used)
    tile_first = jnp.concatenate([jnp.ones((1,), I32), (te[1:] != te[:-1]).astype(I32)])
    xs = jnp.take(jnp.pad(h2, ((0, nt * tmm - t_tok), (0, 0))), src, axis=0, mode="clip")
    ys = _gmm(xs, w_gate, w_up, w_down, layer, te, tile_first, tile_valid, tmm)
    return jnp.take(ys, dest, axis=0, mode="clip")


def _moe_combine(x, y1, y2, gates, m5):
    g = gates
    f = g[:, 0:1] * y1.astype(F32) + g[:, 1:2] * y2.astype(F32)
    return x + m5 * f


def _rope(x, cos, sin):
    n = x.shape[1]
    lane = lax.broadcasted_iota(I32, x.shape, 1)
    sw = jnp.where(lane % 32 < 16, pltpu.roll(x, n - 16, 1), pltpu.roll(x, 16, 1))
    reps = n // 128
    if reps > 1:
        cos = jnp.concatenate([cos] * reps, axis=1)
        sin = jnp.concatenate([sin] * reps, axis=1)
    return x * cos + sw * sin


def _inproj1_kernel(x_ref, y1_ref, y2_ref, ga_ref, m0_ref, m1_ref, g_ref, w_ref, cos_ref, sin_ref,
                    x1_ref, cq_ref, ckt_ref, cv_ref, dq_ref, dk_ref, dv_ref, *, tm, bs):
    x1 = _moe_combine(x_ref[...], y1_ref[...], y2_ref[...], ga_ref[...], m0_ref[0][5:6])
    x1_ref[...] = x1
    m = m1_ref[0]
    h = (_rms(x1, g_ref[...]) * (1.0 + m[1:2]) + m[0:1]).astype(BF16)
    in_lat = pl.program_id(0) * tm < bs
    cos, sin = cos_ref[...], sin_ref[...]
    scale = HEAD_DIM ** -0.5 * LOG2E
    cq = _dot(h, w_ref[:, 0:512])
    cq_ref[...] = (jnp.where(in_lat, _rope(cq, cos, sin), cq) * scale).astype(BF16)
    ck = _dot(h, w_ref[:, 512:640])
    ckt_ref[...] = jnp.where(in_lat, _rope(ck, cos, sin), ck).T.astype(BF16)
    cv_ref[...] = _dot(h, w_ref[:, 640:768]).astype(BF16)
    dq_ref[...] = (_dot(h, w_ref[:, 768:1280]) * scale).astype(BF16)
    dk_ref[...] = _dot(h, w_ref[:, 1280:1792]).astype(BF16)
    dv_ref[...] = _dot(h, w_ref[:, 1792:2304]).astype(BF16)


def _inproj1(x, y, gates, mods0, mods1, g, w, cos, sin, tm, bs, s, nb):
    r, d = x.shape
    second = pl.BlockSpec((tm, d), lambda i: (i + r // tm, 0))
    kern = functools.partial(_inproj1_kernel, tm=tm, bs=bs)
    mi = functools.partial(_mod_index, tm=tm, bs=bs, s=s, nb=nb)
    row = lambda wd: pl.BlockSpec((tm, wd), lambda i: (i, 0))
    modspec = pl.BlockSpec((1, N_MOD, d), lambda i: (mi(i), 0, 0))
    tab = pl.BlockSpec((tm, 128), lambda i: (jnp.where(i * tm < bs, (i * tm % s) // tm, 0), 0))
    shp = lambda wd, dt: jax.ShapeDtypeStruct((r, wd), dt)
    return pl.pallas_call(
        kern,
        grid=(r // tm,),
        in_specs=[row(d), row(d), second, row(128), modspec, modspec,
                  pl.BlockSpec((1, d), lambda i: (0, 0)), pl.BlockSpec(w.shape, lambda i: (0, 0)), tab, tab],
        out_specs=[row(d), row(512), pl.BlockSpec((128, tm), lambda i: (0, i)), row(128), row(512), row(512),
                   row(512)],
        out_shape=[shp(d, F32), shp(512, BF16), jax.ShapeDtypeStruct((128, r), BF16), shp(128, BF16),
                   shp(512, BF16), shp(512, BF16), shp(512, BF16)],
        compiler_params=_cp(("parallel",)),
        name="inproj1",
    )(x, y, y, gates, mods0, mods1, g, w, cos, sin)


def _swa_kernel(q_ref, ktp_ref, ktc_ref, ktn_ref, ktx_ref, vp_ref, vc_ref, vn_ref, vx_ref, sink_ref, o_ref,
                *, nblk, cl):
    i = pl.program_id(1)
    wb = SWA_BLOCK
    nloc = 3 * wb
    kt = jnp.concatenate([ktp_ref[...], ktc_ref[...], ktn_ref[...], ktx_ref[...]], axis=1)
    vv = jnp.concatenate([vp_ref[...], vc_ref[...], vn_ref[...], vx_ref[...]], axis=0)
    a_i = lax.broadcasted_iota(I32, (2 * wb, nloc), 0) % wb
    c_i = lax.broadcasted_iota(I32, (2 * wb, nloc), 1)
    lo = jnp.where(i > 0, 0, wb)
    hi = jnp.where(i < nblk - 1, 3 * wb, 2 * wb)
    ok = (c_i >= a_i) & (c_i <= a_i + 2 * SWA_WINDOW) & (c_i >= lo) & (c_i < hi)
    half = lax.broadcasted_iota(I32, (1, 128), 1) // HEAD_DIM
    zero = jnp.zeros((), BF16)
    sink = sink_ref[...]
    def scores(g):
        q2 = q_ref[:, 128 * g:128 * g + 128]
        qst = jnp.concatenate([jnp.where(half == 0, q2, zero), jnp.where(half == 1, q2, zero)], axis=0)
        s_all = _dot(qst, kt)
        return jnp.concatenate([jnp.where(ok, s_all[:, 0:nloc], NEG), s_all[:, nloc:]], axis=1)

    def softmax(g, sc):
        sk = jnp.concatenate([jnp.broadcast_to(sink[0:1, 2 * g + a:2 * g + a + 1], (wb, 1)) for a in range(2)],
                             axis=0)
        m = jnp.maximum(jnp.max(sc, axis=-1, keepdims=True), sk)
        p = jnp.exp2(sc - m)
        return p.astype(BF16), jnp.sum(p, axis=-1, keepdims=True) + jnp.exp2(sk - m)

    def output(g, p, den):
        ost = _dot(p, vv) / den
        o_ref[:, 128 * g:128 * g + 128] = jnp.where(half == 0, ost[0:wb], ost[wb:2 * wb]).astype(BF16)

    sc, pr = {}, {}
    for step in range(6):
        if step < 4:
            sc[step] = scores(step)
        if 1 <= step < 5:
            pr[step - 1] = softmax(step - 1, sc.pop(step - 1))
        if step >= 2:
            output(step - 2, *pr.pop(step - 2))


def _swa(cq, ckt, cv, sink_row, nb, s, cl, bs):
    wb = SWA_BLOCK
    nblk = s // wb
    kern = functools.partial(_swa_kernel, nblk=nblk, cl=cl)
    prev = lambda b, i: b * nblk + jnp.maximum(i - 1, 0)
    own = lambda b, i: b * nblk + i
    nxt = lambda b, i: b * nblk + jnp.minimum(i + 1, nblk - 1)
    ktspec = lambda f: pl.BlockSpec((128, wb), lambda b, i: (0, f(b, i)))
    vspec = lambda f: pl.BlockSpec((wb, 128), lambda b, i: (f(b, i), 0))
    return pl.pallas_call(
        kern,
        grid=(nb, nblk),
        in_specs=[pl.BlockSpec((wb, 512), lambda b, i: (own(b, i), 0)),
                  ktspec(prev), ktspec(own), ktspec(nxt),
                  pl.BlockSpec((128, cl), lambda b, i: (0, bs // cl + b)),
                  vspec(prev), vspec(own), vspec(nxt),
                  pl.BlockSpec((cl, 128), lambda b, i: (bs // cl + b, 0)),
                  pl.BlockSpec((1, 128), lambda b, i: (0, 0))],
        out_specs=pl.BlockSpec((wb, 512), lambda b, i: (own(b, i), 0)),
        out_shape=jax.ShapeDtypeStruct((bs, 512), BF16),
        compiler_params=_cp(("parallel", "parallel")),
        name="swa",
    )(cq, ckt, ckt, ckt, ckt, cv, cv, cv, cv, sink_row)


def _na_kernel(q_ref, k_ref, v_ref, kx_ref, vx_ref, bias_ref, o_ref, *, rows, unroll):
    half = lax.broadcasted_iota(I32, (1, 128), 1) // HEAD_DIM
    zero = jnp.zeros((), BF16)
    kx, vx = kx_ref[...], vx_ref[...]
    span = NA_KH * GRID_W

    def scores(r):
        rs = jnp.clip(r - NA_KH // 2, 0, rows - NA_KH)
        off = rs - r + NA_KH - 1
        q0 = pl.multiple_of(r * GRID_W, GRID_W)
        k0 = pl.multiple_of(rs * GRID_W, GRID_W)
        q2 = q_ref[pl.ds(q0, GRID_W), :]
        qst = jnp.concatenate([jnp.where(half == 0, q2, zero), jnp.where(half == 1, q2, zero)], axis=0)
        s_loc = (_dot_nt(qst, k_ref[pl.ds(k0, span), :])
                 + jnp.concatenate([bias_ref[0, off], bias_ref[1, off]], axis=0))
        return q0, k0, s_loc, _dot_nt(qst, kx)

    def softmax(s_loc, s_ctx):
        m = jnp.maximum(jnp.max(s_loc, axis=-1, keepdims=True), jnp.max(s_ctx, axis=-1, keepdims=True))
        p_loc = jnp.exp2(s_loc - m)
        p_ctx = jnp.exp2(s_ctx - m)
        den = jnp.sum(p_loc, axis=-1, keepdims=True) + jnp.sum(p_ctx, axis=-1, keepdims=True)
        return p_loc.astype(BF16), p_ctx.astype(BF16), den

    def body(i, carry):
        sc = [scores(i * unroll + j) for j in range(unroll)]
        pr = [softmax(s_loc, s_ctx) for (_, _, s_loc, s_ctx) in sc]
        for (q0, k0, _, _), (p_loc, p_ctx, den) in zip(sc, pr):
            ost = (_dot(p_loc, v_ref[pl.ds(k0, span), :]) + _dot(p_ctx, vx)) / den
            o = jnp.where(half == 0, ost[0:GRID_W], ost[GRID_W:2 * GRID_W])
            o_ref[pl.ds(q0, GRID_W), :] = o.astype(BF16)
        return carry

    lax.fori_loop(0, rows // unroll, body, 0)


def _na(dq, dk, dv, bias, nb, s, cl, bs):
    rows = s // GRID_W
    unroll = 4
    assert rows % unroll == 0
    kern = functools.partial(_na_kernel, rows=rows, unroll=unroll)
    seq = pl.BlockSpec((s, 128), lambda b, p: (b, p))
    ctx = pl.BlockSpec((cl, 128), lambda b, p: (bs // cl + b, p))
    return pl.pallas_call(
        kern,
        grid=(nb, NA_HEADS // 2),
        in_specs=[seq, seq, seq, ctx, ctx,
                  pl.BlockSpec((2, NA_KH, GRID_W, NA_KH * GRID_W), lambda b, p: (p, 0, 0, 0))],
        out_specs=seq,
        out_shape=jax.ShapeDtypeStruct((bs, 512), BF16),
        compiler_params=_cp(("parallel", "parallel")),
        name="na",
    )(dq, dk, dv, dk, dv, bias)


def _na_bias_table(rpb):
    c = np.arange(GRID_W)
    qs = np.clip(c - NA_KW // 2, 0, GRID_W - NA_KW)
    kc = np.arange(GRID_W)
    ok = (kc[None, :] >= qs[:, None]) & (kc[None, :] < qs[:, None] + NA_KW)
    dc = np.clip(kc[None, :] - c[:, None] + NA_KW - 1, 0, 2 * NA_KW - 2)
    sel = (np.arange(2 * NA_KW - 1)[:, None, None] == dc[None]).astype(np.float32)
    cols = jnp.einsum("hab,bck->hcak", rpb.astype(F32), sel, precision=HI)
    cols = jnp.where(ok[None, :, None, :], cols * LOG2E, NEG)
    return jnp.stack([cols[:, :, off:off + NA_KH].reshape(NA_HEADS, GRID_W, NA_KH * GRID_W)
                      for off in range(NA_KH)], axis=1)


def _mix1_kernel(x_ref, oc_ref, od_ref, mod_ref, wo_ref, g2_ref, rw_ref, rb_ref,
                 xn_ref, h2_ref, ei_ref, ga_ref):
    y = _dot(oc_ref[...], wo_ref[0:512, :]) + _dot(od_ref[...], wo_ref[512:1024, :])
    xn, h2, eidx, gates = _post_mixer(x_ref[...], y, mod_ref[0], g2_ref[...], rw_ref[...], rb_ref[...])
    xn_ref[...] = xn
    h2_ref[...] = h2.astype(BF16)
    ei_ref[...] = eidx
    ga_ref[...] = gates


def _mix1(x, oc, od, mods, wo, g2, rw, rb, tm, bs, s):
    d = x.shape[1]
    row = lambda w: pl.BlockSpec((tm, w), lambda i: (i, 0))
    full = lambda a: pl.BlockSpec(a.shape, lambda i: (0,) * a.ndim)
    return pl.pallas_call(
        _mix1_kernel,
        grid=(bs // tm,),
        in_specs=[row(d), row(512), row(512), pl.BlockSpec((1, N_MOD, d), lambda i: (i * tm // s, 0, 0)),
                  full(wo), full(g2), full(rw), full(rb)],
        out_specs=[row(d), row(d), pl.BlockSpec((8, tm), lambda i: (0, i)), row(128)],
        out_shape=[jax.ShapeDtypeStruct((bs, d), F32), jax.ShapeDtypeStruct((bs, d), BF16),
                   jax.ShapeDtypeStruct((8, bs), I32), jax.ShapeDtypeStruct((bs, 128), F32)],
        compiler_params=_cp(("parallel",)),
        name="mix1",
    )(x, oc, od, mods, wo, g2, rw, rb)


def _final_kernel(x_ref, y1_ref, y2_ref, ga_ref, mod_ref, g_ref, o_ref):
    x = _moe_combine(x_ref[...], y1_ref[...], y2_ref[...], ga_ref[...], mod_ref[0][5:6])
    o_ref[...] = _rms(x, g_ref[...])


def _final(x, y, gates, mods, g, tm, s):
    r, d = x.shape
    second = pl.BlockSpec((tm, d), lambda i: (i + r // tm, 0))
    row = lambda w: pl.BlockSpec((tm, w), lambda i: (i, 0))
    return pl.pallas_call(
        _final_kernel,
        grid=(r // tm,),
        in_specs=[row(d), row(d), second, row(128), pl.BlockSpec((1, N_MOD, d), lambda i: (i * tm // s, 0, 0)),
                  pl.BlockSpec((1, d), lambda i: (0, 0))],
        out_specs=row(d),
        out_shape=jax.ShapeDtypeStruct((r, d), F32),
        compiler_params=_cp(("parallel",)),
        name="final",
    )(x, y, y, gates, mods, g)


def _rope_tables(s):
    nf = HEAD_DIM // 4
    t = np.arange(s)
    inv = ROPE_THETA ** (-np.arange(nf, dtype=np.float64) / nf)
    ar = (t // GRID_W)[:, None] * inv
    ac = (t % GRID_W)[:, None] * inv
    cos = np.concatenate([np.cos(ar), np.cos(ar), np.cos(ac), np.cos(ac)], axis=1)
    sin = np.concatenate([-np.sin(ar), np.sin(ar), -np.sin(ac), np.sin(ac)], axis=1)
    return (jnp.asarray(np.concatenate([cos, cos], axis=1), F32),
            jnp.asarray(np.concatenate([sin, sin], axis=1), F32))


def kernel(x, c, ctx, c_ctx, ada_w, ada_b, norm1_g, norm2_g, ev_w_in, ev_w_out, sc_conv_w, dn_conv_w, dn_a_log, dn_dt_bias, dn_onorm_g, od_w_in, od_w_out, swa_sink, na_rpb, router_w, router_b, moe_w_gate, moe_w_up, moe_w_down, final_g):
    nb, s, d = x.shape
    cl = ctx.shape[1]
    bs = nb * s
    tm = 512
    ts = 256
    tmm = 512
    assert d == 1024 and s % tm == 0 and (nb * cl) % tm == 0 and cl % ts == 0 and s % ts == 0
    assert s // GRID_W >= NA_KH and bs % cl == 0 and nb + 1 <= 8

    xl, xc = x.reshape(bs, d), ctx.reshape(nb * cl, d)
    cc = jnp.zeros((8, d), F32).at[:nb].set(c).at[nb].set(c_ctx)
    mods = _ada(cc, ada_w, ada_b).reshape(ada_w.shape[0], 8, N_MOD, d)
    rw32 = jnp.pad(router_w, ((0, 0), (0, 128 - N_EXPERTS)))
    rw_hi = rw32.astype(BF16)
    rw = jnp.concatenate([rw_hi, (rw32 - rw_hi.astype(F32)).astype(BF16)], axis=1)
    rb = router_b.reshape(N_EXPERTS, 1)
    row = lambda v: v.reshape(1, -1)

    w_in0 = jnp.pad(ev_w_in[0], ((0, 0), (0, 3712 - ev_w_in.shape[-1]))).astype(BF16)
    sc, qkv, z, bg = _inproj0(xl, xc, mods[0], row(norm1_g[0]), w_in0, tm, bs, s, nb)
    pad16 = lambda v: jnp.pad(v.reshape(-1), (8, 128 - 16)).reshape(1, 128)
    q, k, v, bga = _dnprep(qkv, bg, dn_conv_w[0], pad16(dn_a_log[0]), pad16(dn_dt_bias[0]), ts, bs, s, cl)
    uf, ub, wf, wb, qf, qb, kf, kb, af, ab, gc = _dnchunk(q, k, v, bga)
    of, ob = _dnscan(uf, ub, wf, wb, qf, qb, kf, kb, af, ab, gc, nb, s, cl, bs)
    x0, h2, ei, ga = _mix0(xl, xc, sc, of, ob, z, mods[0], sc_conv_w[0], row(dn_onorm_g[0]),
                           ev_w_out[0].astype(BF16), row(norm2_g[0]), rw, rb, ts, bs, s, cl, nb)
    y = _moe(h2, ei[0], ei[1], moe_w_gate, moe_w_up, moe_w_down, 0, tmm)

    perm = np.concatenate([np.arange(HEAD_DIM) + HEAD_DIM * (g + 4 * a) for g in range(4) for a in range(2)])
    w1 = od_w_in[0]
    w_in1 = jnp.concatenate([w1[:, 0:512][:, perm], w1[:, 512:]], axis=1).astype(BF16)
    wo1 = od_w_out[0]
    w_out1 = jnp.concatenate([wo1[0:512][perm], wo1[512:]], axis=0).astype(BF16)
    sink_row = jnp.pad(swa_sink[0][np.array([g + 4 * a for g in range(4) for a in range(2)])] * LOG2E,
                       (0, 128 - SWA_HEADS)).reshape(1, 128)
    cos, sin = _rope_tables(s)
    x1, cq, ckt, cv, dq, dk, dv = _inproj1(x0, y, ga, mods[0], mods[1], row(norm1_g[1]), w_in1, cos, sin,
                                           tm, bs, s, nb)
    oc = _swa(cq, ckt, cv, sink_row, nb, s, cl, bs)
    od = _na(dq, dk, dv, _na_bias_table(na_rpb[0]), nb, s, cl, bs)
    x2, h2, ei, ga = _mix1(x1, oc, od, mods[1], w_out1, row(norm2_g[1]), rw, rb, tm, bs, s)
    y = _moe(h2, ei[0], ei[1], moe_w_gate, moe_w_up, moe_w_down, 1, tmm)
    out = _final(x2, y, ga, mods[1], row(final_g), tm, s)
    return out.reshape(nb, s, d)
```

```python
import functools
import math

import numpy as np
import jax
import jax.numpy as jnp
from jax import lax
from jax.experimental import pallas as pl
from jax.experimental.pallas import tpu as pltpu

F32 = jnp.float32
BF16 = jnp.bfloat16
I32 = jnp.int32
HI = lax.Precision.HIGHEST

EPS = 1e-6
N_MOD = 6
GRID_W = 64
HEAD_DIM = 64
DN_HEADS = 4
DN_HD = 128
DN_CHUNK = 64
SWA_HEADS = 8
SWA_KV = 2
SWA_BLOCK = 128
SWA_WINDOW = 128
NA_HEADS = 8
NA_KH = 8
NA_KW = 16
ROPE_THETA = 10000.0
N_EXPERTS = 16
N_GROUPS = 4
NEG = -1e30
LOG2E = 1.4426950408889634
VMEM_LIMIT = 56 * 1024 * 1024


def _cp(sem, vmem=VMEM_LIMIT):
    return pltpu.CompilerParams(dimension_semantics=sem, vmem_limit_bytes=vmem)


def _dot(a, b, precision=None):
    return jnp.dot(a, b, preferred_element_type=F32, precision=precision)


def _dot_nt(a, b, precision=None):
    return lax.dot_general(a, b, (((1,), (1,)), ((), ())), preferred_element_type=F32, precision=precision)


def _dot_tn(a, b, precision=None):
    return lax.dot_general(a, b, (((0,), (0,)), ((), ())), preferred_element_type=F32, precision=precision)


def _silu(x):
    return x * jax.nn.sigmoid(x)


def _rms(x, g):
    return x * lax.rsqrt(jnp.mean(x * x, axis=-1, keepdims=True) + EPS) * g


def _ada_kernel(cc_ref, w_ref, b_ref, o_ref):
    a = _silu(cc_ref[...])
    o_ref[0] = _dot(a, w_ref[0], HI) + b_ref[0]


def _ada(cc, ada_w, ada_b):
    depth, d, n = ada_w.shape
    tn = 1536
    return pl.pallas_call(
        _ada_kernel,
        grid=(depth, n // tn),
        in_specs=[pl.BlockSpec((8, d), lambda l, j: (0, 0)),
                  pl.BlockSpec((1, d, tn), lambda l, j: (l, 0, j)),
                  pl.BlockSpec((1, 1, tn), lambda l, j: (l, 0, j))],
        out_specs=pl.BlockSpec((1, 8, tn), lambda l, j: (l, 0, j)),
        out_shape=jax.ShapeDtypeStruct((depth, 8, n), F32),
        compiler_params=_cp(("parallel", "parallel")),
        name="ada",
    )(cc, ada_w, ada_b.reshape(depth, 1, n))


def _mod_index(i, tm, bs, s, nb):
    row0 = i * tm
    return jnp.where(row0 < bs, row0 // s, nb)


def _token_specs(tm, bs, d):
    nlat = bs // tm
    return [pl.BlockSpec((tm, d), lambda i: (jnp.minimum(i, nlat - 1), 0)),
            pl.BlockSpec((tm, d), lambda i: (jnp.maximum(i - nlat, 0), 0))]


def _token_rows(xl_ref, xc_ref, tm, bs):
    return jnp.where(pl.program_id(0) * tm < bs, xl_ref[...], xc_ref[...])


def _inproj0_kernel(xl_ref, xc_ref, mod_ref, g_ref, w_ref, sc_ref, qkv_ref, z_ref, bg_ref, *, tm, bs):
    m = mod_ref[0]
    h = (_rms(_token_rows(xl_ref, xc_ref, tm, bs), g_ref[...]) * (1.0 + m[1:2]) + m[0:1]).astype(BF16)
    sc_ref[...] = _dot(h, w_ref[:, 0:1536])
    qkv_ref[...] = _dot(h, w_ref[:, 1536:3072])
    z_ref[...] = _dot(h, w_ref[:, 3072:3584])
    bg_ref[...] = _dot(h, w_ref[:, 3584:3712])


def _inproj0(xl, xc, mods, g, w, tm, bs, s, nb):
    d = xl.shape[1]
    r = bs + xc.shape[0]
    mi = functools.partial(_mod_index, tm=tm, bs=bs, s=s, nb=nb)
    return pl.pallas_call(
        functools.partial(_inproj0_kernel, tm=tm, bs=bs),
        grid=(r // tm,),
        in_specs=_token_specs(tm, bs, d) + [
            pl.BlockSpec((1, N_MOD, d), lambda i: (mi(i), 0, 0)),
            pl.BlockSpec((1, d), lambda i: (0, 0)),
            pl.BlockSpec(w.shape, lambda i: (0, 0))],
        out_specs=[pl.BlockSpec((tm, 1536), lambda i: (i, 0)),
                   pl.BlockSpec((tm, 1536), lambda i: (i, 0)),
                   pl.BlockSpec((tm, 512), lambda i: (i, 0)),
                   pl.BlockSpec((tm, 128), lambda i: (i, 0))],
        out_shape=[jax.ShapeDtypeStruct((r, 1536), F32), jax.ShapeDtypeStruct((r, 1536), F32),
                   jax.ShapeDtypeStruct((r, 512), F32), jax.ShapeDtypeStruct((r, 128), F32)],
        compiler_params=_cp(("parallel",)),
        name="inproj0",
    )(xl, xc, mods, g, w)


def _seq_edges(i, ts, bs, s, cl):
    row0 = i * ts
    in_lat = row0 < bs
    r_in = jnp.where(in_lat, row0 % s, (row0 - bs) % cl)
    seqlen = jnp.where(in_lat, s, cl)
    return r_in == 0, r_in + ts == seqlen


def _shifted(x, prev_row, next_row):
    n = x.shape[0]
    rows = lax.broadcasted_iota(I32, x.shape, 0)
    xp = jnp.where(rows == 0, prev_row, pltpu.roll(x, 1, 0))
    xn = jnp.where(rows == n - 1, next_row, pltpu.roll(x, n - 1, 0))
    return xp, xn


def _halo_specs(ts, width, r):
    nb8 = r // 8
    k = ts // 8
    return [pl.BlockSpec((8, width), lambda i: (jnp.maximum(i * k - 1, 0), 0)),
            pl.BlockSpec((8, width), lambda i: (jnp.minimum((i + 1) * k, nb8 - 1), 0))]


def _dnprep_kernel(x_ref, prev_ref, next_ref, bg_ref, cw_ref, alog_ref, dt_ref,
                   q_ref, k_ref, v_ref, bga_ref, *, ts, bs, s, cl):
    first, last = _seq_edges(pl.program_id(0), ts, bs, s, cl)
    for c in range(12):
        sl = slice(128 * c, 128 * c + 128)
        x = x_ref[:, sl]
        pr = jnp.where(first, 0.0, prev_ref[7:8, sl])
        nx = jnp.where(last, 0.0, next_ref[0:1, sl])
        xp, xn = _shifted(x, pr, nx)
        w = cw_ref[:, sl]
        y = _silu(xp * w[0:1] + x * w[1:2] + xn * w[2:3])
        hs = slice(128 * (c % 4), 128 * (c % 4) + 128)
        if c < 8:
            y = y * lax.rsqrt(jnp.sum(y * y, axis=-1, keepdims=True) + EPS)
        if c < 4:
            q_ref[:, hs] = y * DN_HD ** -0.5
        elif c < 8:
            k_ref[:, hs] = y
        else:
            v_ref[:, hs] = y
    b = bg_ref[...]
    cols = lax.broadcasted_iota(I32, b.shape, 1)
    beta = jax.nn.sigmoid(b)
    t = b + dt_ref[...]
    softplus = jnp.maximum(t, 0.0) + jnp.log1p(jnp.exp(-jnp.abs(t)))
    g = -jnp.exp(alog_ref[...]) * softplus
    bga_ref[...] = jnp.where(cols < 8, beta, jnp.where(cols < 16, g, 0.0))


def _dnprep(qkv, bg, cw, alog_row, dt_row, ts, bs, s, cl):
    r = qkv.shape[0]
    kern = functools.partial(_dnprep_kernel, ts=ts, bs=bs, s=s, cl=cl)
    return pl.pallas_call(
        kern,
        grid=(r // ts,),
        in_specs=[pl.BlockSpec((ts, 1536), lambda i: (i, 0))] + _halo_specs(ts, 1536, r) + [
            pl.BlockSpec((ts, 128), lambda i: (i, 0)),
            pl.BlockSpec((3, 1536), lambda i: (0, 0)),
            pl.BlockSpec((1, 128), lambda i: (0, 0)),
            pl.BlockSpec((1, 128), lambda i: (0, 0))],
        out_specs=[pl.BlockSpec((ts, 512), lambda i: (i, 0))] * 3 + [pl.BlockSpec((ts, 128), lambda i: (i, 0))],
        out_shape=[jax.ShapeDtypeStruct((r, 512), F32)] * 3 + [jax.ShapeDtypeStruct((r, 128), F32)],
        compiler_params=_cp(("parallel",)),
        name="dnprep",
    )(qkv, qkv, qkv, bg, cw, alog_row, dt_row)


def _dnchunk_kernel(q_ref, k_ref, v_ref, bg_ref,
                    uf_ref, ub_ref, wf_ref, wb_ref, qf_ref, qb_ref, kf_ref, kb_ref, af_ref, ab_ref, gc_ref, *, nchunks):
    outs = ((uf_ref, wf_ref, qf_ref, kf_ref, af_ref), (ub_ref, wb_ref, qb_ref, kb_ref, ab_ref))
    c, nh = DN_CHUNK, DN_HEADS
    head_of_col = lax.broadcasted_iota(I32, (1, nh * c), 1) // c
    zero = jnp.zeros((), BF16)

    def block_diag(x):
        return jnp.concatenate([jnp.where(head_of_col == h, x, zero) for h in range(nh)], axis=0)

    chains = []
    for cc in range(nchunks):
        chains += _dnchunk_setup(slice(cc * c, (cc + 1) * c), q_ref, k_ref, v_ref, bg_ref, gc_ref)
    for ch in chains:
        ch["tm"] = ch["nmat"]
        nb16 = ch["nmat"].astype(BF16)
        ch["npow"] = _dot(nb16, block_diag(nb16))
    for _ in range(4):
        for ch in chains:
            nb16 = ch["npow"].astype(BF16)
            ch["both"] = _dot(jnp.concatenate([nb16, ch["tm"].astype(BF16)], axis=0), block_diag(nb16))
        for ch in chains:
            ch["tm"] = ch["tm"] + ch["npow"] + ch["both"][c:2 * c]
            ch["npow"] = ch["both"][0:c]
    for ch in chains:
        ch["both"] = _dot(ch["tm"].astype(BF16), block_diag(ch["npow"].astype(BF16)))
    for ch in chains:
        ch["tm"] = ch["tm"] + ch["npow"] + ch["both"]
    for ch in chains:
        ch["uw"] = ch["rhs"] + _dot(block_diag(ch["tm"].astype(BF16)), ch["rhs"].astype(BF16))
    for ch in chains:
        u_ref, w_ref, qd_ref, kd_ref, at_ref = outs[ch["d"]]
        rows, uw = ch["rows"], ch["uw"]
        for h in range(nh):
            hs = slice(DN_HD * h, DN_HD * h + DN_HD)
            rs = slice(c * h, c * h + c)
            u_ref[rows, hs] = uw[rs, 0:DN_HD]
            w_ref[rows, hs] = uw[rs, DN_HD:2 * DN_HD].astype(BF16)
            qd_ref[rows, hs] = ch["qd"][rs]
        kd_ref[slice(2 * rows.start, 2 * rows.stop)] = ch["kd"].T.astype(BF16)
        at_ref[rows] = ch["att"].astype(BF16)


def _dnchunk_setup(rows, q_ref, k_ref, v_ref, bg_ref, gc_ref):
    c, nh = DN_CHUNK, DN_HEADS
    n = c * nh
    bg = bg_ref[rows]
    i64 = lax.broadcasted_iota(I32, (c, c), 0)
    j64 = lax.broadcasted_iota(I32, (c, c), 1)
    cols = lax.broadcasted_iota(I32, bg.shape, 1)
    gcf = _dot((i64 >= j64).astype(F32), bg, HI)
    gcb = _dot((i64 <= j64).astype(F32), bg, HI)
    gc = jnp.where(cols >= 12, gcb, gcf)
    gc_ref[rows] = gc
    gct = gc.T
    ii = lax.broadcasted_iota(I32, (c, n), 0)
    jj = lax.broadcasted_iota(I32, (c, n), 1)
    head_of_col = jj // c
    jj = jj % c

    def stack(ref):
        return jnp.concatenate([ref[rows, DN_HD * h:DN_HD * h + DN_HD] for h in range(nh)], axis=0)

    def stacked_cols(arr, r0, r1, col0):
        return jnp.concatenate([jnp.broadcast_to(arr[r0:r1, col0 + h:col0 + h + 1], (c, DN_HD)) for h in range(nh)],
                               axis=0)

    def side_by_side_cols(arr, col0):
        out = arr[:, col0 + nh - 1:col0 + nh]
        for h in range(nh - 2, -1, -1):
            out = jnp.where(head_of_col == h, arr[:, col0 + h:col0 + h + 1], out)
        return out

    def diag_blocks(x):
        out = x[(nh - 1) * c:nh * c]
        for h in range(nh - 2, -1, -1):
            out = jnp.where(head_of_col == h, x[h * c:(h + 1) * c], out)
        return out

    kst, qst, vst = stack(k_ref), stack(q_ref), stack(v_ref)
    kb = kst.astype(BF16)
    kq = _dot_nt(jnp.concatenate([kb, qst.astype(BF16)], axis=0), kb)
    kkt, qkt = diag_blocks(kq[0:n]), diag_blocks(kq[n:2 * n])
    chains = []
    for d in range(2):
        incl = (ii >= jj) if d == 0 else (ii <= jj)
        strict = (ii > jj) if d == 0 else (ii < jj)
        last = c - 1 if d == 0 else 0
        grow = jnp.concatenate([gct[8 + 4 * d + h:9 + 4 * d + h, :] for h in range(nh)], axis=1)
        decay = jnp.exp(jnp.where(incl, side_by_side_cols(gc, 8 + 4 * d) - grow, NEG))
        nmat = jnp.where(strict, -(side_by_side_cols(bg, 4 * d) * kkt * decay), 0.0)
        b1 = stacked_cols(bg, 0, c, 4 * d)
        gcol = stacked_cols(gc, 0, c, 8 + 4 * d)
        glast = stacked_cols(gc, last, last + 1, 8 + 4 * d)
        e1 = jnp.exp(gcol)
        chains.append(dict(
            d=d, rows=rows, nmat=nmat,
            rhs=jnp.concatenate([b1 * vst, (b1 * e1) * kst], axis=1),
            qd=(qst * e1).astype(BF16),
            kd=kst * jnp.exp(glast - gcol),
            att=qkt * decay))
    return chains


def _dnchunk(q, k, v, bga):
    r = q.shape[0]
    nchunks = 4
    c = DN_CHUNK * nchunks
    row = lambda w: pl.BlockSpec((c, w), lambda i: (i, 0))
    shp = lambda w, dt: jax.ShapeDtypeStruct((r, w), dt)
    return pl.pallas_call(
        functools.partial(_dnchunk_kernel, nchunks=nchunks),
        grid=(r // c,),
        in_specs=[row(512), row(512), row(512), row(128)],
        out_specs=[row(512)] * 6 + [pl.BlockSpec((2 * c, 256), lambda i: (i, 0))] * 2 + [row(256), row(256), row(128)],
        out_shape=([shp(512, F32)] * 2 + [shp(512, BF16)] * 4 + [jax.ShapeDtypeStruct((2 * r, 256), BF16)] * 2
                   + [shp(256, BF16)] * 2 + [shp(128, F32)]),
        compiler_params=_cp(("parallel",)),
        name="dnchunk",
    )(q, k, v, bga)


def _dnscan_kernel(uf_ref, wf_ref, qf_ref, kf_ref, af_ref, gf_ref,
                   ub_ref, wb_ref, qb_ref, kb_ref, ab_ref, gb_ref,
                   of_ref, ob_ref, s_ref, *, nsub):
    @pl.when(pl.program_id(1) == 0)
    def _():
        s_ref[...] = jnp.zeros_like(s_ref)

    c, nh = DN_CHUNK, DN_HEADS
    head_of_lane = lax.broadcasted_iota(I32, (1, nh * DN_HD), 1) // DN_HD
    head_of_col = lax.broadcasted_iota(I32, (1, nh * c), 1) // c
    zero = jnp.zeros((), BF16)

    def block_diag(tile, head_ids):
        return jnp.concatenate([jnp.where(head_ids == h, tile, zero) for h in range(nh)], axis=0)

    dirs = ((uf_ref, wf_ref, qf_ref, kf_ref, af_ref, gf_ref, of_ref, c - 1),
            (ub_ref, wb_ref, qb_ref, kb_ref, ab_ref, gb_ref, ob_ref, 0))
    states = [s_ref[0], s_ref[1]]
    for sub in range(nsub):
        work = []
        for d, (u_ref, w_ref, qd_ref, kd_ref, at_ref, g_ref, o_ref, last) in enumerate(dirs):
            k = sub if d == 0 else nsub - 1 - sub
            rows = slice(c * k, c * k + c)
            g = g_ref[rows]
            decay = jnp.concatenate(
                [jnp.broadcast_to(jnp.exp(g[last:last + 1, 8 + 4 * d + h:9 + 4 * d + h]), (DN_HD, DN_HD))
                 for h in range(nh)], axis=0)
            ust = jnp.concatenate([u_ref[rows, DN_HD * h:DN_HD * h + DN_HD] for h in range(nh)], axis=0)
            stb = states[d].astype(BF16)
            vnew = ust - _dot(block_diag(w_ref[rows], head_of_lane), stb)
            work.append((rows, decay, stb, vnew.astype(BF16)))
        for d, (u_ref, w_ref, qd_ref, kd_ref, at_ref, g_ref, o_ref, last) in enumerate(dirs):
            rows, decay, stb, vnb = work[d]
            o = (_dot(block_diag(qd_ref[rows], head_of_lane), stb)
                 + _dot(block_diag(at_ref[rows], head_of_col), vnb))
            kdt = kd_ref[slice(2 * rows.start, 2 * rows.stop)]
            states[d] = states[d] * decay + _dot(block_diag(kdt, head_of_col), vnb)
            for h in range(nh):
                o_ref[rows, DN_HD * h:DN_HD * h + DN_HD] = o[c * h:c * h + c]
    s_ref[0] = states[0]
    s_ref[1] = states[1]


def _dnscan(uf, ub, wf, wb, qf, qb, kf, kb, af, ab, gc, nb, s, cl, bs):
    r = uf.shape[0]
    nsub = 2
    c = DN_CHUNK * nsub
    assert cl % c == 0 and s % c == 0 and bs % c == 0
    ncc, ncl = cl // c, s // c
    ns = ncc + ncl

    def fwd(b, t):
        return jnp.where(t < ncc, bs // c + b * ncc + t, b * ncl + t - ncc)

    def bwd(b, t):
        return jnp.where(t < ncc, bs // c + b * ncc + (ncc - 1 - t), b * ncl + (ncl - 1 - (t - ncc)))

    def specs(idx):
        blk = lambda w: pl.BlockSpec((c, w), lambda b, t: (idx(b, t), 0))
        return [blk(512), blk(512), blk(512), pl.BlockSpec((2 * c, 256), lambda b, t: (idx(b, t), 0)), blk(256),
                blk(128)]

    return pl.pallas_call(
        functools.partial(_dnscan_kernel, nsub=nsub),
        grid=(nb, ns),
        in_specs=specs(fwd) + specs(bwd),
        out_specs=[pl.BlockSpec((c, 512), lambda b, t: (fwd(b, t), 0)),
                   pl.BlockSpec((c, 512), lambda b, t: (bwd(b, t), 0))],
        out_shape=[jax.ShapeDtypeStruct((r, 512), F32)] * 2,
        scratch_shapes=[pltpu.VMEM((2, DN_HEADS * DN_HD, DN_HD), F32)],
        compiler_params=_cp(("arbitrary", "arbitrary")),
        name="dnscan",
    )(uf, wf, qf, kf, af, gc, ub, wb, qb, kb, ab, gc)


def _route(logits, bias_col):
    epg = N_EXPERTS // N_GROUPS
    tm = logits.shape[0]
    scores = jax.nn.sigmoid(logits.T[0:N_EXPERTS])
    gsel = scores + bias_col
    row = lambda a, k: a[k:k + 1]
    best = gidx = None
    for g in range(N_GROUPS):
        a = [row(gsel, epg * g + k) for k in range(epg)]
        m01, n01 = jnp.maximum(a[0], a[1]), jnp.minimum(a[0], a[1])
        m23, n23 = jnp.maximum(a[2], a[3]), jnp.minimum(a[2], a[3])
        gs = jnp.maximum(m01, m23) + jnp.maximum(jnp.minimum(m01, m23), jnp.maximum(n01, n23))
        if g == 0:
            best, gidx = gs, jnp.zeros_like(gs)
        else:
            better = gs > best
            best = jnp.where(better, gs, best)
            gidx = jnp.where(better, float(g), gidx)
    sel = [None] * epg
    raw = [None] * epg
    for g in range(N_GROUPS):
        for k in range(epg):
            v, u = row(gsel, epg * g + k), row(scores, epg * g + k)
            sel[k] = v if g == 0 else jnp.where(gidx == g, v, sel[k])
            raw[k] = u if g == 0 else jnp.where(gidx == g, u, raw[k])
    v1, e1, w1 = sel[0], jnp.zeros_like(gidx), raw[0]
    for k in range(1, epg):
        better = sel[k] > v1
        v1 = jnp.where(better, sel[k], v1)
        e1 = jnp.where(better, float(k), e1)
        w1 = jnp.where(better, raw[k], w1)
    v2 = e2 = w2 = None
    for k in range(epg):
        cand = jnp.where(e1 == k, -jnp.inf, sel[k])
        if k == 0:
            v2, e2, w2 = cand, jnp.zeros_like(gidx), raw[0]
        else:
            better = cand > v2
            v2 = jnp.where(better, cand, v2)
            e2 = jnp.where(better, float(k), e2)
            w2 = jnp.where(better, raw[k], w2)
    tot = w1 + w2
    eidx = jnp.concatenate([gidx * epg + e1, gidx * epg + e2, jnp.zeros((6, tm), F32)], axis=0).astype(I32)
    gates_t = jnp.concatenate([w1 / tot, w2 / tot, jnp.zeros((126, tm), F32)], axis=0)
    return eidx, gates_t.T


def _post_mixer(x, y, m, g2, rw, rb):
    xn = x + m[2:3] * y
    h2 = _rms(xn, g2) * (1.0 + m[4:5]) + m[3:4]
    hi = h2.astype(BF16)
    lo = (h2 - hi.astype(F32)).astype(BF16)
    hw = _dot(hi, rw)
    logits = hw[:, 0:128] + (hw[:, 128:256] + _dot(lo, rw[:, 0:128]))
    eidx, gates = _route(logits, rb)
    return xn, hi, eidx, gates


def _mix0_kernel(xl_ref, xc_ref, sc_ref, prev_ref, next_ref, of_ref, ob_ref, z_ref, mod_ref, cw_ref, on_ref, wo_ref,
                 g2_ref, rw_ref, rb_ref, xn_ref, h2_ref, ei_ref, ga_ref, *, ts, bs, s, cl):
    first, last = _seq_edges(pl.program_id(0), ts, bs, s, cl)
    ya = []
    for c in range(4):
        sl = slice(128 * c, 128 * c + 128)
        sg = slice(512 + 128 * c, 512 + 128 * c + 128)
        sx = slice(1024 + 128 * c, 1024 + 128 * c + 128)
        u = sc_ref[:, sg] * sc_ref[:, sx]
        pr = jnp.where(first, 0.0, prev_ref[7:8, sg] * prev_ref[7:8, sx])
        nx = jnp.where(last, 0.0, next_ref[0:1, sg] * next_ref[0:1, sx])
        up, un = _shifted(u, pr, nx)
        w = cw_ref[:, sl]
        ya.append((sc_ref[:, sl] * (up * w[0:1] + u * w[1:2] + un * w[2:3])).astype(BF16))
    yb = []
    for h in range(DN_HEADS):
        hs = slice(DN_HD * h, DN_HD * h + DN_HD)
        o = of_ref[:, hs] + ob_ref[:, hs]
        yb.append((_rms(o, on_ref[...]) * _silu(z_ref[:, hs])).astype(BF16))
    ycat = jnp.concatenate(ya + yb, axis=1)
    y = _dot(ycat, wo_ref[...])
    xn, h2, eidx, gates = _post_mixer(_token_rows(xl_ref, xc_ref, ts, bs), y, mod_ref[0], g2_ref[...], rw_ref[...],
                                      rb_ref[...])
    xn_ref[...] = xn
    h2_ref[...] = h2.astype(BF16)
    ei_ref[...] = eidx
    ga_ref[...] = gates


def _mix0(xl, xc, sc, of, ob, z, mods, cw, on, wo, g2, rw, rb, ts, bs, s, cl, nb):
    d = xl.shape[1]
    r = bs + xc.shape[0]
    kern = functools.partial(_mix0_kernel, ts=ts, bs=bs, s=s, cl=cl)
    mi = functools.partial(_mod_index, tm=ts, bs=bs, s=s, nb=nb)
    row = lambda w: pl.BlockSpec((ts, w), lambda i: (i, 0))
    full = lambda a: pl.BlockSpec(a.shape, lambda i: (0,) * a.ndim)
    return pl.pallas_call(
        kern,
        grid=(r // ts,),
        in_specs=_token_specs(ts, bs, d) + [row(1536)] + _halo_specs(ts, 1536, r) + [
            row(512), row(512), row(512),
            pl.BlockSpec((1, N_MOD, d), lambda i: (mi(i), 0, 0)),
            full(cw), full(on), full(wo), full(g2), full(rw), full(rb)],
        out_specs=[row(d), row(d), pl.BlockSpec((8, ts), lambda i: (0, i)), row(128)],
        out_shape=[jax.ShapeDtypeStruct((r, d), F32), jax.ShapeDtypeStruct((r, d), BF16),
                   jax.ShapeDtypeStruct((8, r), I32), jax.ShapeDtypeStruct((r, 128), F32)],
        compiler_params=_cp(("parallel",)),
        name="mix0",
    )(xl, xc, sc, sc, sc, of, ob, z, mods, cw, on, wo, g2, rw, rb)


def _gmm_kernel(te_ref, tf_ref, tv_ref, x_ref, wg_ref, wu_ref, wd_ref, y_ref, wgb, wub, wdb):
    t = pl.program_id(0)

    @pl.when(tf_ref[t] == 1)
    def _():
        wgb[...] = wg_ref[0, 0].astype(BF16)
        wub[...] = wu_ref[0, 0].astype(BF16)
        wdb[...] = wd_ref[0, 0].astype(BF16)

    @pl.when(tv_ref[t] == 1)
    def _():
        x = x_ref[...]
        a = (_silu(_dot(x, wgb[...])) * _dot(x, wub[...])).astype(BF16)
        y_ref[...] = _dot(a, wdb[...]).astype(BF16)

    @pl.when(tv_ref[t] == 0)
    def _():
        y_ref[...] = jnp.zeros_like(y_ref)


def _gmm(xs, w_gate, w_up, w_down, layer, tile_expert, tile_first, tile_valid, tmm):
    p, d = xs.shape
    de = w_gate.shape[-1]
    nt = p // tmm
    grid_spec = pltpu.PrefetchScalarGridSpec(
        num_scalar_prefetch=3,
        grid=(nt,),
        in_specs=[pl.BlockSpec((tmm, d), lambda t, te, tf, tv: (t, 0)),
                  pl.BlockSpec((1, 1, d, de), lambda t, te, tf, tv: (layer, te[t], 0, 0)),
                  pl.BlockSpec((1, 1, d, de), lambda t, te, tf, tv: (layer, te[t], 0, 0)),
                  pl.BlockSpec((1, 1, de, d), lambda t, te, tf, tv: (layer, te[t], 0, 0))],
        out_specs=pl.BlockSpec((tmm, d), lambda t, te, tf, tv: (t, 0)),
        scratch_shapes=[pltpu.VMEM((d, de), BF16), pltpu.VMEM((d, de), BF16), pltpu.VMEM((de, d), BF16)],
    )
    return pl.pallas_call(
        _gmm_kernel,
        grid_spec=grid_spec,
        out_shape=jax.ShapeDtypeStruct((p, d), BF16),
        compiler_params=_cp(("arbitrary",)),
        name="gmm",
    )(tile_expert, tile_first, tile_valid, xs, w_gate, w_up, w_down)


def _moe(h2, e_first, e_second, w_gate, w_up, w_down, layer, tmm):
    t_tok = h2.shape[0]
    n = 2 * t_tok
    e_flat = jnp.concatenate([e_first, e_second])
    onehot = (e_flat[:, None] == jnp.arange(N_EXPERTS, dtype=I32)[None, :]).astype(I32)
    csum = jnp.cumsum(onehot, axis=0)
    counts = csum[-1]
    ptiles = (counts + tmm - 1) // tmm
    tile_end = jnp.cumsum(ptiles)
    dest = jnp.sum(onehot * (csum - 1 + ((tile_end - ptiles) * tmm)[None, :]), axis=1)
    nt = n // tmm + N_EXPERTS
    tid = jnp.arange(nt, dtype=I32)
    tile_valid = (tid < tile_end[-1]).astype(I32)
    te = jnp.minimum(jnp.sum((tile_end[None, :] <= tid[:, None]).astype(I32), axis=1), N_EXPERTS - 1)
    last_used = jnp.max(jnp.where(tile_valid == 1, te, 0))
    te = jnp.where(tile_valid == 1, te, last_used)
    tile_first = jnp.concatenate([jnp.ones((1,), I32), (te[1:] != te[:-1]).astype(I32)])
    order = jnp.argsort(e_flat, stable=True).astype(I32)
    seg_start = (tile_end - ptiles) * tmm
    shift = seg_start - (jnp.cumsum(counts) - counts)
    pos = jnp.arange(nt * tmm, dtype=I32)
    per_tile = lambda v: jnp.repeat(v[te], tmm)
    used = (pos - per_tile(seg_start) < per_tile(counts)) & (jnp.repeat(tile_valid, tmm) == 1)
    src = jnp.where(used, jnp.take(order, jnp.clip(pos - per_tile(shift), 0, n - 1)), pos) % t_tok
    xs = jnp.take(jnp.pad(h2, ((0, nt * tmm - t_tok), (0, 0))), src, axis=0, mode="clip")
    ys = _gmm(xs, w_gate, w_up, w_down, layer, te, tile_first, tile_valid, tmm)
    return jnp.take(ys, dest, axis=0, mode="clip")


def _moe_combine(x, y1, y2, gates, m5):
    g = gates
    f = g[:, 0:1] * y1.astype(F32) + g[:, 1:2] * y2.astype(F32)
    return x + m5 * f


def _rope(x, cos, sin):
    n = x.shape[1]
    lane = lax.broadcasted_iota(I32, x.shape, 1)
    sw = jnp.where(lane % 32 < 16, pltpu.roll(x, n - 16, 1), pltpu.roll(x, 16, 1))
    reps = n // 128
    if reps > 1:
        cos = jnp.concatenate([cos] * reps, axis=1)
        sin = jnp.concatenate([sin] * reps, axis=1)
    return x * cos + sw * sin


def _inproj1_kernel(x_ref, y1_ref, y2_ref, ga_ref, m0_ref, m1_ref, g_ref, w_ref, cos_ref, sin_ref,
                    x1_ref, cq_ref, ckt_ref, cv_ref, dq_ref, dk_ref, dv_ref, *, tm, bs):
    x1 = _moe_combine(x_ref[...], y1_ref[...], y2_ref[...], ga_ref[...], m0_ref[0][5:6])
    x1_ref[...] = x1
    m = m1_ref[0]
    h = (_rms(x1, g_ref[...]) * (1.0 + m[1:2]) + m[0:1]).astype(BF16)
    in_lat = pl.program_id(0) * tm < bs
    cos, sin = cos_ref[...], sin_ref[...]
    scale = HEAD_DIM ** -0.5 * LOG2E
    cq = _dot(h, w_ref[:, 0:512])
    cq_ref[...] = (jnp.where(in_lat, _rope(cq, cos, sin), cq) * scale).astype(BF16)
    ck = _dot(h, w_ref[:, 512:640])
    ckt_ref[...] = jnp.where(in_lat, _rope(ck, cos, sin), ck).T.astype(BF16)
    cv_ref[...] = _dot(h, w_ref[:, 640:768]).astype(BF16)
    dq_ref[...] = (_dot(h, w_ref[:, 768:1280]) * scale).astype(BF16)
    dk_ref[...] = _dot(h, w_ref[:, 1280:1792]).astype(BF16)
    dv_ref[...] = _dot(h, w_ref[:, 1792:2304]).astype(BF16)


def _inproj1(x, y, gates, mods0, mods1, g, w, cos, sin, tm, bs, s, nb):
    r, d = x.shape
    second = pl.BlockSpec((tm, d), lambda i: (i + r // tm, 0))
    kern = functools.partial(_inproj1_kernel, tm=tm, bs=bs)
    mi = functools.partial(_mod_index, tm=tm, bs=bs, s=s, nb=nb)
    row = lambda wd: pl.BlockSpec((tm, wd), lambda i: (i, 0))
    modspec = pl.BlockSpec((1, N_MOD, d), lambda i: (mi(i), 0, 0))
    tab = pl.BlockSpec((tm, 128), lambda i: (jnp.where(i * tm < bs, (i * tm % s) // tm, 0), 0))
    shp = lambda wd, dt: jax.ShapeDtypeStruct((r, wd), dt)
    return pl.pallas_call(
        kern,
        grid=(r // tm,),
        in_specs=[row(d), row(d), second, row(128), modspec, modspec,
                  pl.BlockSpec((1, d), lambda i: (0, 0)), pl.BlockSpec(w.shape, lambda i: (0, 0)), tab, tab],
        out_specs=[row(d), row(512), pl.BlockSpec((128, tm), lambda i: (0, i)), row(128), row(512), row(512),
                   row(512)],
        out_shape=[shp(d, F32), shp(512, BF16), jax.ShapeDtypeStruct((128, r), BF16), shp(128, BF16),
                   shp(512, BF16), shp(512, BF16), shp(512, BF16)],
        compiler_params=_cp(("parallel",)),
        name="inproj1",
    )(x, y, y, gates, mods0, mods1, g, w, cos, sin)


def _swa_kernel(q_ref, ktp_ref, ktc_ref, ktn_ref, ktx_ref, vp_ref, vc_ref, vn_ref, vx_ref, sink_ref, o_ref,
                *, nblk, cl):
    i = pl.program_id(1)
    wb = SWA_BLOCK
    nloc = 3 * wb
    kt = jnp.concatenate([ktp_ref[...], ktc_ref[...], ktn_ref[...], ktx_ref[...]], axis=1)
    vv = jnp.concatenate([vp_ref[...], vc_ref[...], vn_ref[...], vx_ref[...]], axis=0)
    a_i = lax.broadcasted_iota(I32, (2 * wb, nloc), 0) % wb
    c_i = lax.broadcasted_iota(I32, (2 * wb, nloc), 1)
    lo = jnp.where(i > 0, 0, wb)
    hi = jnp.where(i < nblk - 1, 3 * wb, 2 * wb)
    ok = (c_i >= a_i) & (c_i <= a_i + 2 * SWA_WINDOW) & (c_i >= lo) & (c_i < hi)
    half = lax.broadcasted_iota(I32, (1, 128), 1) // HEAD_DIM
    zero = jnp.zeros((), BF16)
    sink = sink_ref[...]
    def scores(g):
        q2 = q_ref[:, 128 * g:128 * g + 128]
        qst = jnp.concatenate([jnp.where(half == 0, q2, zero), jnp.where(half == 1, q2, zero)], axis=0)
        s_all = _dot(qst, kt)
        return jnp.concatenate([jnp.where(ok, s_all[:, 0:nloc], NEG), s_all[:, nloc:]], axis=1)

    def softmax(g, sc):
        sk = jnp.concatenate([jnp.broadcast_to(sink[0:1, 2 * g + a:2 * g + a + 1], (wb, 1)) for a in range(2)],
                             axis=0)
        m = jnp.maximum(jnp.max(sc, axis=-1, keepdims=True), sk)
        p = jnp.exp2(sc - m)
        return p.astype(BF16), jnp.sum(p, axis=-1, keepdims=True) + jnp.exp2(sk - m)

    def output(g, p, den):
        ost = _dot(p, vv) / den
        o_ref[:, 128 * g:128 * g + 128] = jnp.where(half == 0, ost[0:wb], ost[wb:2 * wb]).astype(BF16)

    sc, pr = {}, {}
    for step in range(6):
        if step < 4:
            sc[step] = scores(step)
        if 1 <= step < 5:
            pr[step - 1] = softmax(step - 1, sc.pop(step - 1))
        if step >= 2:
            output(step - 2, *pr.pop(step - 2))


def _swa(cq, ckt, cv, sink_row, nb, s, cl, bs):
    wb = SWA_BLOCK
    nblk = s // wb
    kern = functools.partial(_swa_kernel, nblk=nblk, cl=cl)
    prev = lambda b, i: b * nblk + jnp.maximum(i - 1, 0)
    own = lambda b, i: b * nblk + i
    nxt = lambda b, i: b * nblk + jnp.minimum(i + 1, nblk - 1)
    ktspec = lambda f: pl.BlockSpec((128, wb), lambda b, i: (0, f(b, i)))
    vspec = lambda f: pl.BlockSpec((wb, 128), lambda b, i: (f(b, i), 0))
    return pl.pallas_call(
        kern,
        grid=(nb, nblk),
        in_specs=[pl.BlockSpec((wb, 512), lambda b, i: (own(b, i), 0)),
                  ktspec(prev), ktspec(own), ktspec(nxt),
                  pl.BlockSpec((128, cl), lambda b, i: (0, bs // cl + b)),
                  vspec(prev), vspec(own), vspec(nxt),
                  pl.BlockSpec((cl, 128), lambda b, i: (bs // cl + b, 0)),
                  pl.BlockSpec((1, 128), lambda b, i: (0, 0))],
        out_specs=pl.BlockSpec((wb, 512), lambda b, i: (own(b, i), 0)),
        out_shape=jax.ShapeDtypeStruct((bs, 512), BF16),
        compiler_params=_cp(("parallel", "parallel")),
        name="swa",
    )(cq, ckt, ckt, ckt, ckt, cv, cv, cv, cv, sink_row)


def _na_kernel(q_ref, k_ref, v_ref, kx_ref, vx_ref, bias_ref, o_ref, *, rows, unroll):
    half = lax.broadcasted_iota(I32, (1, 128), 1) // HEAD_DIM
    zero = jnp.zeros((), BF16)
    kx, vx = kx_ref[...], vx_ref[...]
    span = NA_KH * GRID_W

    def scores(r):
        rs = jnp.clip(r - NA_KH // 2, 0, rows - NA_KH)
        off = rs - r + NA_KH - 1
        q0 = pl.multiple_of(r * GRID_W, GRID_W)
        k0 = pl.multiple_of(rs * GRID_W, GRID_W)
        q2 = q_ref[pl.ds(q0, GRID_W), :]
        qst = jnp.concatenate([jnp.where(half == 0, q2, zero), jnp.where(half == 1, q2, zero)], axis=0)
        s_loc = (_dot_nt(qst, k_ref[pl.ds(k0, span), :])
                 + jnp.concatenate([bias_ref[0, off], bias_ref[1, off]], axis=0))
        return q0, k0, s_loc, _dot_nt(qst, kx)

    def softmax(s_loc, s_ctx):
        m = jnp.maximum(jnp.max(s_loc, axis=-1, keepdims=True), jnp.max(s_ctx, axis=-1, keepdims=True))
        p_loc = jnp.exp2(s_loc - m)
        p_ctx = jnp.exp2(s_ctx - m)
        den = jnp.sum(p_loc, axis=-1, keepdims=True) + jnp.sum(p_ctx, axis=-1, keepdims=True)
        return p_loc.astype(BF16), p_ctx.astype(BF16), den

    def body(i, carry):
        sc = [scores(i * unroll + j) for j in range(unroll)]
        pr = [softmax(s_loc, s_ctx) for (_, _, s_loc, s_ctx) in sc]
        for (q0, k0, _, _), (p_loc, p_ctx, den) in zip(sc, pr):
            ost = (_dot(p_loc, v_ref[pl.ds(k0, span), :]) + _dot(p_ctx, vx)) / den
            o = jnp.where(half == 0, ost[0:GRID_W], ost[GRID_W:2 * GRID_W])
            o_ref[pl.ds(q0, GRID_W), :] = o.astype(BF16)
        return carry

    lax.fori_loop(0, rows // unroll, body, 0)


def _na(dq, dk, dv, bias, nb, s, cl, bs):
    rows = s // GRID_W
    unroll = 4
    assert rows % unroll == 0
    kern = functools.partial(_na_kernel, rows=rows, unroll=unroll)
    seq = pl.BlockSpec((s, 128), lambda b, p: (b, p))
    ctx = pl.BlockSpec((cl, 128), lambda b, p: (bs // cl + b, p))
    return pl.pallas_call(
        kern,
        grid=(nb, NA_HEADS // 2),
        in_specs=[seq, seq, seq, ctx, ctx,
                  pl.BlockSpec((2, NA_KH, GRID_W, NA_KH * GRID_W), lambda b, p: (p, 0, 0, 0))],
        out_specs=seq,
        out_shape=jax.ShapeDtypeStruct((bs, 512), BF16),
        compiler_params=_cp(("parallel", "parallel")),
        name="na",
    )(dq, dk, dv, dk, dv, bias)


def _na_bias_table(rpb):
    c = np.arange(GRID_W)
    qs = np.clip(c - NA_KW // 2, 0, GRID_W - NA_KW)
    kc = np.arange(GRID_W)
    ok = (kc[None, :] >= qs[:, None]) & (kc[None, :] < qs[:, None] + NA_KW)
    dc = np.clip(kc[None, :] - c[:, None] + NA_KW - 1, 0, 2 * NA_KW - 2)
    sel = (np.arange(2 * NA_KW - 1)[:, None, None] == dc[None]).astype(np.float32)
    cols = jnp.einsum("hab,bck->hcak", rpb.astype(F32), sel, precision=HI)
    cols = jnp.where(ok[None, :, None, :], cols * LOG2E, NEG)
    return jnp.stack([cols[:, :, off:off + NA_KH].reshape(NA_HEADS, GRID_W, NA_KH * GRID_W)
                      for off in range(NA_KH)], axis=1)


def _mix1_kernel(x_ref, oc_ref, od_ref, mod_ref, wo_ref, g2_ref, rw_ref, rb_ref,
                 xn_ref, h2_ref, ei_ref, ga_ref):
    y = _dot(oc_ref[...], wo_ref[0:512, :]) + _dot(od_ref[...], wo_ref[512:1024, :])
    xn, h2, eidx, gates = _post_mixer(x_ref[...], y, mod_ref[0], g2_ref[...], rw_ref[...], rb_ref[...])
    xn_ref[...] = xn
    h2_ref[...] = h2.astype(BF16)
    ei_ref[...] = eidx
    ga_ref[...] = gates


def _mix1(x, oc, od, mods, wo, g2, rw, rb, tm, bs, s):
    d = x.shape[1]
    row = lambda w: pl.BlockSpec((tm, w), lambda i: (i, 0))
    full = lambda a: pl.BlockSpec(a.shape, lambda i: (0,) * a.ndim)
    return pl.pallas_call(
        _mix1_kernel,
        grid=(bs // tm,),
        in_specs=[row(d), row(512), row(512), pl.BlockSpec((1, N_MOD, d), lambda i: (i * tm // s, 0, 0)),
                  full(wo), full(g2), full(rw), full(rb)],
        out_specs=[row(d), row(d), pl.BlockSpec((8, tm), lambda i: (0, i)), row(128)],
        out_shape=[jax.ShapeDtypeStruct((bs, d), F32), jax.ShapeDtypeStruct((bs, d), BF16),
                   jax.ShapeDtypeStruct((8, bs), I32), jax.ShapeDtypeStruct((bs, 128), F32)],
        compiler_params=_cp(("parallel",)),
        name="mix1",
    )(x, oc, od, mods, wo, g2, rw, rb)


def _final_kernel(x_ref, y1_ref, y2_ref, ga_ref, mod_ref, g_ref, o_ref):
    x = _moe_combine(x_ref[...], y1_ref[...], y2_ref[...], ga_ref[...], mod_ref[0][5:6])
    o_ref[...] = _rms(x, g_ref[...])


def _final(x, y, gates, mods, g, tm, s):
    r, d = x.shape
    second = pl.BlockSpec((tm, d), lambda i: (i + r // tm, 0))
    row = lambda w: pl.BlockSpec((tm, w), lambda i: (i, 0))
    return pl.pallas_call(
        _final_kernel,
        grid=(r // tm,),
        in_specs=[row(d), row(d), second, row(128), pl.BlockSpec((1, N_MOD, d), lambda i: (i * tm // s, 0, 0)),
                  pl.BlockSpec((1, d), lambda i: (0, 0))],
        out_specs=row(d),
        out_shape=jax.ShapeDtypeStruct((r, d), F32),
        compiler_params=_cp(("parallel",)),
        name="final",
    )(x, y, y, gates, mods, g)


def _rope_tables(s):
    nf = HEAD_DIM // 4
    t = np.arange(s)
    inv = ROPE_THETA ** (-np.arange(nf, dtype=np.float64) / nf)
    ar = (t // GRID_W)[:, None] * inv
    ac = (t % GRID_W)[:, None] * inv
    cos = np.concatenate([np.cos(ar), np.cos(ar), np.cos(ac), np.cos(ac)], axis=1)
    sin = np.concatenate([-np.sin(ar), np.sin(ar), -np.sin(ac), np.sin(ac)], axis=1)
    return (jnp.asarray(np.concatenate([cos, cos], axis=1), F32),
            jnp.asarray(np.concatenate([sin, sin], axis=1), F32))


def kernel(x, c, ctx, c_ctx, ada_w, ada_b, norm1_g, norm2_g, ev_w_in, ev_w_out, sc_conv_w, dn_conv_w, dn_a_log, dn_dt_bias, dn_onorm_g, od_w_in, od_w_out, swa_sink, na_rpb, router_w, router_b, moe_w_gate, moe_w_up, moe_w_down, final_g):
    nb, s, d = x.shape
    cl = ctx.shape[1]
    bs = nb * s
    tm = 512
    ts = 256
    tmm = 512
    assert d == 1024 and s % tm == 0 and (nb * cl) % tm == 0 and cl % ts == 0 and s % ts == 0
    assert s // GRID_W >= NA_KH and bs % cl == 0 and nb + 1 <= 8

    xl, xc = x.reshape(bs, d), ctx.reshape(nb * cl, d)
    cc = jnp.zeros((8, d), F32).at[:nb].set(c).at[nb].set(c_ctx)
    mods = _ada(cc, ada_w, ada_b).reshape(ada_w.shape[0], 8, N_MOD, d)
    rw32 = jnp.pad(router_w, ((0, 0), (0, 128 - N_EXPERTS)))
    rw_hi = rw32.astype(BF16)
    rw = jnp.concatenate([rw_hi, (rw32 - rw_hi.astype(F32)).astype(BF16)], axis=1)
    rb = router_b.reshape(N_EXPERTS, 1)
    row = lambda v: v.reshape(1, -1)

    w_in0 = jnp.pad(ev_w_in[0], ((0, 0), (0, 3712 - ev_w_in.shape[-1]))).astype(BF16)
    sc, qkv, z, bg = _inproj0(xl, xc, mods[0], row(norm1_g[0]), w_in0, tm, bs, s, nb)
    pad16 = lambda v: jnp.pad(v.reshape(-1), (8, 128 - 16)).reshape(1, 128)
    q, k, v, bga = _dnprep(qkv, bg, dn_conv_w[0], pad16(dn_a_log[0]), pad16(dn_dt_bias[0]), ts, bs, s, cl)
    uf, ub, wf, wb, qf, qb, kf, kb, af, ab, gc = _dnchunk(q, k, v, bga)
    of, ob = _dnscan(uf, ub, wf, wb, qf, qb, kf, kb, af, ab, gc, nb, s, cl, bs)
    x0, h2, ei, ga = _mix0(xl, xc, sc, of, ob, z, mods[0], sc_conv_w[0], row(dn_onorm_g[0]),
                           ev_w_out[0].astype(BF16), row(norm2_g[0]), rw, rb, ts, bs, s, cl, nb)
    y = _moe(h2, ei[0], ei[1], moe_w_gate, moe_w_up, moe_w_down, 0, tmm)

    perm = np.concatenate([np.arange(HEAD_DIM) + HEAD_DIM * (g + 4 * a) for g in range(4) for a in range(2)])
    w1 = od_w_in[0]
    w_in1 = jnp.concatenate([w1[:, 0:512][:, perm], w1[:, 512:]], axis=1).astype(BF16)
    wo1 = od_w_out[0]
    w_out1 = jnp.concatenate([wo1[0:512][perm], wo1[512:]], axis=0).astype(BF16)
    sink_row = jnp.pad(swa_sink[0][np.array([g + 4 * a for g in range(4) for a in range(2)])] * LOG2E,
                       (0, 128 - SWA_HEADS)).reshape(1, 128)
    cos, sin = _rope_tables(s)
    x1, cq, ckt, cv, dq, dk, dv = _inproj1(x0, y, ga, mods[0], mods[1], row(norm1_g[1]), w_in1, cos, sin,
                                           tm, bs, s, nb)
    oc = _swa(cq, ckt, cv, sink_row, nb, s, cl, bs)
    od = _na(dq, dk, dv, _na_bias_table(na_rpb[0]), nb, s, cl, bs)
    x2, h2, ei, ga = _mix1(x1, oc, od, mods[1], w_out1, row(norm2_g[1]), rw, rb, tm, bs, s)
    y = _moe(h2, ei[0], ei[1], moe_w_gate, moe_w_up, moe_w_down, 1, tmm)
    out = _final(x2, y, ga, mods[1], row(final_g), tm, s)
    return out.reshape(nb, s, d)
```

```python
import functools
import math

import numpy as np
import jax
import jax.numpy as jnp
from jax import lax
from jax.experimental import pallas as pl
from jax.experimental.pallas import tpu as pltpu

F32 = jnp.float32
BF16 = jnp.bfloat16
I32 = jnp.int32
HI = lax.Precision.HIGHEST

EPS = 1e-6
N_MOD = 6
GRID_W = 64
HEAD_DIM = 64
DN_HEADS = 4
DN_HD = 128
DN_CHUNK = 64
SWA_HEADS = 8
SWA_KV = 2
SWA_BLOCK = 128
SWA_WINDOW = 128
NA_HEADS = 8
NA_KH = 8
NA_KW = 16
ROPE_THETA = 10000.0
N_EXPERTS = 16
N_GROUPS = 4
NEG = -1e30
LOG2E = 1.4426950408889634
VMEM_LIMIT = 56 * 1024 * 1024


def _cp(sem, vmem=VMEM_LIMIT):
    return pltpu.CompilerParams(dimension_semantics=sem, vmem_limit_bytes=vmem)


def _dot(a, b, precision=None):
    return jnp.dot(a, b, preferred_element_type=F32, precision=precision)


def _dot_nt(a, b, precision=None):
    return lax.dot_general(a, b, (((1,), (1,)), ((), ())), preferred_element_type=F32, precision=precision)


def _dot_tn(a, b, precision=None):
    return lax.dot_general(a, b, (((0,), (0,)), ((), ())), preferred_element_type=F32, precision=precision)


def _silu(x):
    return x * jax.nn.sigmoid(x)


def _rms(x, g):
    return x * lax.rsqrt(jnp.mean(x * x, axis=-1, keepdims=True) + EPS) * g


def _ada_kernel(cc_ref, w_ref, b_ref, o_ref):
    a = _silu(cc_ref[...])
    o_ref[0] = _dot(a, w_ref[0], HI) + b_ref[0]


def _ada(cc, ada_w, ada_b):
    depth, d, n = ada_w.shape
    tn = 1536
    return pl.pallas_call(
        _ada_kernel,
        grid=(depth, n // tn),
        in_specs=[pl.BlockSpec((8, d), lambda l, j: (0, 0)),
                  pl.BlockSpec((1, d, tn), lambda l, j: (l, 0, j)),
                  pl.BlockSpec((1, 1, tn), lambda l, j: (l, 0, j))],
        out_specs=pl.BlockSpec((1, 8, tn), lambda l, j: (l, 0, j)),
        out_shape=jax.ShapeDtypeStruct((depth, 8, n), F32),
        compiler_params=_cp(("parallel", "parallel")),
        name="ada",
    )(cc, ada_w, ada_b.reshape(depth, 1, n))


def _mod_index(i, tm, bs, s, nb):
    row0 = i * tm
    return jnp.where(row0 < bs, row0 // s, nb)


def _token_specs(tm, bs, d):
    nlat = bs // tm
    return [pl.BlockSpec((tm, d), lambda i: (jnp.minimum(i, nlat - 1), 0)),
            pl.BlockSpec((tm, d), lambda i: (jnp.maximum(i - nlat, 0), 0))]


def _token_rows(xl_ref, xc_ref, tm, bs):
    return jnp.where(pl.program_id(0) * tm < bs, xl_ref[...], xc_ref[...])


def _inproj0_kernel(xl_ref, xc_ref, mod_ref, g_ref, w_ref, sc_ref, qkv_ref, z_ref, bg_ref, *, tm, bs):
    m = mod_ref[0]
    h = (_rms(_token_rows(xl_ref, xc_ref, tm, bs), g_ref[...]) * (1.0 + m[1:2]) + m[0:1]).astype(BF16)
    sc_ref[...] = _dot(h, w_ref[:, 0:1536])
    qkv_ref[...] = _dot(h, w_ref[:, 1536:3072])
    z_ref[...] = _dot(h, w_ref[:, 3072:3584])
    bg_ref[...] = _dot(h, w_ref[:, 3584:3712])


def _inproj0(xl, xc, mods, g, w, tm, bs, s, nb):
    d = xl.shape[1]
    r = bs + xc.shape[0]
    mi = functools.partial(_mod_index, tm=tm, bs=bs, s=s, nb=nb)
    return pl.pallas_call(
        functools.partial(_inproj0_kernel, tm=tm, bs=bs),
        grid=(r // tm,),
        in_specs=_token_specs(tm, bs, d) + [
            pl.BlockSpec((1, N_MOD, d), lambda i: (mi(i), 0, 0)),
            pl.BlockSpec((1, d), lambda i: (0, 0)),
            pl.BlockSpec(w.shape, lambda i: (0, 0))],
        out_specs=[pl.BlockSpec((tm, 1536), lambda i: (i, 0)),
                   pl.BlockSpec((tm, 1536), lambda i: (i, 0)),
                   pl.BlockSpec((tm, 512), lambda i: (i, 0)),
                   pl.BlockSpec((tm, 128), lambda i: (i, 0))],
        out_shape=[jax.ShapeDtypeStruct((r, 1536), F32), jax.ShapeDtypeStruct((r, 1536), F32),
                   jax.ShapeDtypeStruct((r, 512), F32), jax.ShapeDtypeStruct((r, 128), F32)],
        compiler_params=_cp(("parallel",)),
        name="inproj0",
    )(xl, xc, mods, g, w)


def _seq_edges(i, ts, bs, s, cl):
    row0 = i * ts
    in_lat = row0 < bs
    r_in = jnp.where(in_lat, row0 % s, (row0 - bs) % cl)
    seqlen = jnp.where(in_lat, s, cl)
    return r_in == 0, r_in + ts == seqlen


def _shifted(x, prev_row, next_row):
    n = x.shape[0]
    rows = lax.broadcasted_iota(I32, x.shape, 0)
    xp = jnp.where(rows == 0, prev_row, pltpu.roll(x, 1, 0))
    xn = jnp.where(rows == n - 1, next_row, pltpu.roll(x, n - 1, 0))
    return xp, xn


def _halo_specs(ts, width, r):
    nb8 = r // 8
    k = ts // 8
    return [pl.BlockSpec((8, width), lambda i: (jnp.maximum(i * k - 1, 0), 0)),
            pl.BlockSpec((8, width), lambda i: (jnp.minimum((i + 1) * k, nb8 - 1), 0))]


def _dnprep_tile(x_ref, prev_ref, next_ref, bg_ref, cw_ref, alog_ref, dt_ref,
                 q_ref, k_ref, v_ref, bga_ref, *, ts, bs, s, cl):
    first, last = _seq_edges(pl.program_id(0), ts, bs, s, cl)
    for c in range(12):
        sl = slice(128 * c, 128 * c + 128)
        x = x_ref[:, sl]
        pr = jnp.where(first, 0.0, prev_ref[7:8, sl])
        nx = jnp.where(last, 0.0, next_ref[0:1, sl])
        xp, xn = _shifted(x, pr, nx)
        w = cw_ref[:, sl]
        y = _silu(xp * w[0:1] + x * w[1:2] + xn * w[2:3])
        hs = slice(128 * (c % 4), 128 * (c % 4) + 128)
        if c < 8:
            y = y * lax.rsqrt(jnp.sum(y * y, axis=-1, keepdims=True) + EPS)
        if c < 4:
            q_ref[:, hs] = y * DN_HD ** -0.5
        elif c < 8:
            k_ref[:, hs] = y
        else:
            v_ref[:, hs] = y
    b = bg_ref[...]
    cols = lax.broadcasted_iota(I32, b.shape, 1)
    beta = jax.nn.sigmoid(b)
    t = b + dt_ref[...]
    softplus = jnp.maximum(t, 0.0) + jnp.log1p(jnp.exp(-jnp.abs(t)))
    g = -jnp.exp(alog_ref[...]) * softplus
    bga_ref[...] = jnp.where(cols < 8, beta, jnp.where(cols < 16, g, 0.0))


def _dnchunk_kernel(x_ref, prev_ref, next_ref, bgraw_ref, cw_ref, alog_ref, dt_ref,
                    uf_ref, ub_ref, wf_ref, wb_ref, qf_ref, qb_ref, kf_ref, kb_ref, af_ref, ab_ref, gc_ref,
                    q_ref, k_ref, v_ref, bg_ref, *, nchunks, bs, s, cl):
    _dnprep_tile(x_ref, prev_ref, next_ref, bgraw_ref, cw_ref, alog_ref, dt_ref, q_ref, k_ref, v_ref, bg_ref,
                 ts=nchunks * DN_CHUNK, bs=bs, s=s, cl=cl)
    outs = ((uf_ref, wf_ref, qf_ref, kf_ref, af_ref), (ub_ref, wb_ref, qb_ref, kb_ref, ab_ref))
    c, nh = DN_CHUNK, DN_HEADS
    head_of_col = lax.broadcasted_iota(I32, (1, nh * c), 1) // c
    zero = jnp.zeros((), BF16)

    def block_diag(x):
        return jnp.concatenate([jnp.where(head_of_col == h, x, zero) for h in range(nh)], axis=0)

    chains = []
    for cc in range(nchunks):
        chains += _dnchunk_setup(slice(cc * c, (cc + 1) * c), q_ref, k_ref, v_ref, bg_ref, gc_ref)
    for ch in chains:
        ch["tm"] = ch["nmat"]
        nb16 = ch["nmat"].astype(BF16)
        ch["npow"] = _dot(nb16, block_diag(nb16))
    for _ in range(4):
        for ch in chains:
            nb16 = ch["npow"].astype(BF16)
            ch["both"] = _dot(jnp.concatenate([nb16, ch["tm"].astype(BF16)], axis=0), block_diag(nb16))
        for ch in chains:
            ch["tm"] = ch["tm"] + ch["npow"] + ch["both"][c:2 * c]
            ch["npow"] = ch["both"][0:c]
    for ch in chains:
        ch["both"] = _dot(ch["tm"].astype(BF16), block_diag(ch["npow"].astype(BF16)))
    for ch in chains:
        ch["tm"] = ch["tm"] + ch["npow"] + ch["both"]
    for ch in chains:
        ch["uw"] = ch["rhs"] + _dot(block_diag(ch["tm"].astype(BF16)), ch["rhs"].astype(BF16))
    for ch in chains:
        u_ref, w_ref, qd_ref, kd_ref, at_ref = outs[ch["d"]]
        rows, uw = ch["rows"], ch["uw"]
        for h in range(nh):
            hs = slice(DN_HD * h, DN_HD * h + DN_HD)
            rs = slice(c * h, c * h + c)
            u_ref[rows, hs] = uw[rs, 0:DN_HD]
            w_ref[rows, hs] = uw[rs, DN_HD:2 * DN_HD].astype(BF16)
            qd_ref[rows, hs] = ch["qd"][rs]
        kd_ref[slice(2 * rows.start, 2 * rows.stop)] = ch["kd"].T.astype(BF16)
        at_ref[rows] = ch["att"].astype(BF16)


def _dnchunk_setup(rows, q_ref, k_ref, v_ref, bg_ref, gc_ref):
    c, nh = DN_CHUNK, DN_HEADS
    n = c * nh
    bg = bg_ref[rows]
    i64 = lax.broadcasted_iota(I32, (c, c), 0)
    j64 = lax.broadcasted_iota(I32, (c, c), 1)
    cols = lax.broadcasted_iota(I32, bg.shape, 1)
    gcf = _dot((i64 >= j64).astype(F32), bg, HI)
    gcb = _dot((i64 <= j64).astype(F32), bg, HI)
    gc = jnp.where(cols >= 12, gcb, gcf)
    gc_ref[rows] = gc
    gct = gc.T
    ii = lax.broadcasted_iota(I32, (c, n), 0)
    jj = lax.broadcasted_iota(I32, (c, n), 1)
    head_of_col = jj // c
    jj = jj % c

    def stack(ref):
        return jnp.concatenate([ref[rows, DN_HD * h:DN_HD * h + DN_HD] for h in range(nh)], axis=0)

    def stacked_cols(arr, r0, r1, col0):
        return jnp.concatenate([jnp.broadcast_to(arr[r0:r1, col0 + h:col0 + h + 1], (c, DN_HD)) for h in range(nh)],
                               axis=0)

    def side_by_side_cols(arr, col0):
        out = arr[:, col0 + nh - 1:col0 + nh]
        for h in range(nh - 2, -1, -1):
            out = jnp.where(head_of_col == h, arr[:, col0 + h:col0 + h + 1], out)
        return out

    def diag_blocks(x):
        out = x[(nh - 1) * c:nh * c]
        for h in range(nh - 2, -1, -1):
            out = jnp.where(head_of_col == h, x[h * c:(h + 1) * c], out)
        return out

    kst, qst, vst = stack(k_ref), stack(q_ref), stack(v_ref)
    kb = kst.astype(BF16)
    kq = _dot_nt(jnp.concatenate([kb, qst.astype(BF16)], axis=0), kb)
    kkt, qkt = diag_blocks(kq[0:n]), diag_blocks(kq[n:2 * n])
    chains = []
    for d in range(2):
        incl = (ii >= jj) if d == 0 else (ii <= jj)
        strict = (ii > jj) if d == 0 else (ii < jj)
        last = c - 1 if d == 0 else 0
        grow = jnp.concatenate([gct[8 + 4 * d + h:9 + 4 * d + h, :] for h in range(nh)], axis=1)
        decay = jnp.exp(jnp.where(incl, side_by_side_cols(gc, 8 + 4 * d) - grow, NEG))
        nmat = jnp.where(strict, -(side_by_side_cols(bg, 4 * d) * kkt * decay), 0.0)
        b1 = stacked_cols(bg, 0, c, 4 * d)
        gcol = stacked_cols(gc, 0, c, 8 + 4 * d)
        glast = stacked_cols(gc, last, last + 1, 8 + 4 * d)
        e1 = jnp.exp(gcol)
        chains.append(dict(
            d=d, rows=rows, nmat=nmat,
            rhs=jnp.concatenate([b1 * vst, (b1 * e1) * kst], axis=1),
            qd=(qst * e1).astype(BF16),
            kd=kst * jnp.exp(glast - gcol),
            att=qkt * decay))
    return chains


def _dnchunk(qkv, bg, cw, alog_row, dt_row, ts, bs, s, cl):
    r = qkv.shape[0]
    nchunks = ts // DN_CHUNK
    c = ts
    row = lambda w: pl.BlockSpec((c, w), lambda i: (i, 0))
    shp = lambda w, dt: jax.ShapeDtypeStruct((r, w), dt)
    return pl.pallas_call(
        functools.partial(_dnchunk_kernel, nchunks=nchunks, bs=bs, s=s, cl=cl),
        grid=(r // c,),
        in_specs=[row(1536)] + _halo_specs(ts, 1536, r) + [
            row(128),
            pl.BlockSpec((3, 1536), lambda i: (0, 0)),
            pl.BlockSpec((1, 128), lambda i: (0, 0)),
            pl.BlockSpec((1, 128), lambda i: (0, 0))],
        scratch_shapes=[pltpu.VMEM((c, 512), F32)] * 3 + [pltpu.VMEM((c, 128), F32)],
        out_specs=[row(512)] * 6 + [pl.BlockSpec((2 * c, 256), lambda i: (i, 0))] * 2 + [row(256), row(256), row(128)],
        out_shape=([shp(512, F32)] * 2 + [shp(512, BF16)] * 4 + [jax.ShapeDtypeStruct((2 * r, 256), BF16)] * 2
                   + [shp(256, BF16)] * 2 + [shp(128, F32)]),
        compiler_params=_cp(("parallel",)),
        name="dnchunk",
    )(qkv, qkv, qkv, bg, cw, alog_row, dt_row)


def _dnscan_kernel(uf_ref, wf_ref, qf_ref, kf_ref, af_ref, gf_ref,
                   ub_ref, wb_ref, qb_ref, kb_ref, ab_ref, gb_ref,
                   of_ref, ob_ref, s_ref, *, nsub):
    @pl.when(pl.program_id(1) == 0)
    def _():
        s_ref[...] = jnp.zeros_like(s_ref)

    c, nh = DN_CHUNK, DN_HEADS
    head_of_lane = lax.broadcasted_iota(I32, (1, nh * DN_HD), 1) // DN_HD
    head_of_col = lax.broadcasted_iota(I32, (1, nh * c), 1) // c
    zero = jnp.zeros((), BF16)

    def block_diag(tile, head_ids):
        return jnp.concatenate([jnp.where(head_ids == h, tile, zero) for h in range(nh)], axis=0)

    dirs = ((uf_ref, wf_ref, qf_ref, kf_ref, af_ref, gf_ref, of_ref, c - 1),
            (ub_ref, wb_ref, qb_ref, kb_ref, ab_ref, gb_ref, ob_ref, 0))
    states = [s_ref[0], s_ref[1]]
    for sub in range(nsub):
        work = []
        for d, (u_ref, w_ref, qd_ref, kd_ref, at_ref, g_ref, o_ref, last) in enumerate(dirs):
            k = sub if d == 0 else nsub - 1 - sub
            rows = slice(c * k, c * k + c)
            g = g_ref[rows]
            decay = jnp.concatenate(
                [jnp.broadcast_to(jnp.exp(g[last:last + 1, 8 + 4 * d + h:9 + 4 * d + h]), (DN_HD, DN_HD))
                 for h in range(nh)], axis=0)
            ust = jnp.concatenate([u_ref[rows, DN_HD * h:DN_HD * h + DN_HD] for h in range(nh)], axis=0)
            stb = states[d].astype(BF16)
            vnew = ust - _dot(block_diag(w_ref[rows], head_of_lane), stb)
            work.append((rows, decay, stb, vnew.astype(BF16)))
        for d, (u_ref, w_ref, qd_ref, kd_ref, at_ref, g_ref, o_ref, last) in enumerate(dirs):
            rows, decay, stb, vnb = work[d]
            o = (_dot(block_diag(qd_ref[rows], head_of_lane), stb)
                 + _dot(block_diag(at_ref[rows], head_of_col), vnb))
            kdt = kd_ref[slice(2 * rows.start, 2 * rows.stop)]
            states[d] = states[d] * decay + _dot(block_diag(kdt, head_of_col), vnb)
            for h in range(nh):
                o_ref[rows, DN_HD * h:DN_HD * h + DN_HD] = o[c * h:c * h + c]
    s_ref[0] = states[0]
    s_ref[1] = states[1]


def _dnscan(uf, ub, wf, wb, qf, qb, kf, kb, af, ab, gc, nb, s, cl, bs):
    r = uf.shape[0]
    nsub = 2
    c = DN_CHUNK * nsub
    assert cl % c == 0 and s % c == 0 and bs % c == 0
    ncc, ncl = cl // c, s // c
    ns = ncc + ncl

    def fwd(b, t):
        return jnp.where(t < ncc, bs // c + b * ncc + t, b * ncl + t - ncc)

    def bwd(b, t):
        return jnp.where(t < ncc, bs // c + b * ncc + (ncc - 1 - t), b * ncl + (ncl - 1 - (t - ncc)))

    def specs(idx):
        blk = lambda w: pl.BlockSpec((c, w), lambda b, t: (idx(b, t), 0))
        return [blk(512), blk(512), blk(512), pl.BlockSpec((2 * c, 256), lambda b, t: (idx(b, t), 0)), blk(256),
                blk(128)]

    return pl.pallas_call(
        functools.partial(_dnscan_kernel, nsub=nsub),
        grid=(nb, ns),
        in_specs=specs(fwd) + specs(bwd),
        out_specs=[pl.BlockSpec((c, 512), lambda b, t: (fwd(b, t), 0)),
                   pl.BlockSpec((c, 512), lambda b, t: (bwd(b, t), 0))],
        out_shape=[jax.ShapeDtypeStruct((r, 512), F32)] * 2,
        scratch_shapes=[pltpu.VMEM((2, DN_HEADS * DN_HD, DN_HD), F32)],
        compiler_params=_cp(("arbitrary", "arbitrary")),
        name="dnscan",
    )(uf, wf, qf, kf, af, gc, ub, wb, qb, kb, ab, gc)


def _route(logits, bias_col):
    epg = N_EXPERTS // N_GROUPS
    tm = logits.shape[0]
    scores = jax.nn.sigmoid(logits.T[0:N_EXPERTS])
    gsel = scores + bias_col
    row = lambda a, k: a[k:k + 1]
    best = gidx = None
    for g in range(N_GROUPS):
        a = [row(gsel, epg * g + k) for k in range(epg)]
        m01, n01 = jnp.maximum(a[0], a[1]), jnp.minimum(a[0], a[1])
        m23, n23 = jnp.maximum(a[2], a[3]), jnp.minimum(a[2], a[3])
        gs = jnp.maximum(m01, m23) + jnp.maximum(jnp.minimum(m01, m23), jnp.maximum(n01, n23))
        if g == 0:
            best, gidx = gs, jnp.zeros_like(gs)
        else:
            better = gs > best
            best = jnp.where(better, gs, best)
            gidx = jnp.where(better, float(g), gidx)
    sel = [None] * epg
    raw = [None] * epg
    for g in range(N_GROUPS):
        for k in range(epg):
            v, u = row(gsel, epg * g + k), row(scores, epg * g + k)
            sel[k] = v if g == 0 else jnp.where(gidx == g, v, sel[k])
            raw[k] = u if g == 0 else jnp.where(gidx == g, u, raw[k])
    v1, e1, w1 = sel[0], jnp.zeros_like(gidx), raw[0]
    for k in range(1, epg):
        better = sel[k] > v1
        v1 = jnp.where(better, sel[k], v1)
        e1 = jnp.where(better, float(k), e1)
        w1 = jnp.where(better, raw[k], w1)
    v2 = e2 = w2 = None
    for k in range(epg):
        cand = jnp.where(e1 == k, -jnp.inf, sel[k])
        if k == 0:
            v2, e2, w2 = cand, jnp.zeros_like(gidx), raw[0]
        else:
            better = cand > v2
            v2 = jnp.where(better, cand, v2)
            e2 = jnp.where(better, float(k), e2)
            w2 = jnp.where(better, raw[k], w2)
    tot = w1 + w2
    eidx = jnp.concatenate([gidx * epg + e1, gidx * epg + e2, jnp.zeros((6, tm), F32)], axis=0).astype(I32)
    gates_t = jnp.concatenate([w1 / tot, w2 / tot, jnp.zeros((126, tm), F32)], axis=0)
    return eidx, gates_t.T


def _post_mixer(x, y, m, g2, rw, rb):
    xn = x + m[2:3] * y
    h2 = _rms(xn, g2) * (1.0 + m[4:5]) + m[3:4]
    hi = h2.astype(BF16)
    lo = (h2 - hi.astype(F32)).astype(BF16)
    hw = _dot(hi, rw)
    logits = hw[:, 0:128] + (hw[:, 128:256] + _dot(lo, rw[:, 0:128]))
    eidx, gates = _route(logits, rb)
    return xn, hi, eidx, gates


def _mix0_kernel(xl_ref, xc_ref, sc_ref, prev_ref, next_ref, of_ref, ob_ref, z_ref, mod_ref, cw_ref, on_ref, wo_ref,
                 g2_ref, rw_ref, rb_ref, h2buf_ref, xn_ref, h2_ref, ei_ref, ga_ref, *, ts, bs, s, cl):
    del h2buf_ref
    first, last = _seq_edges(pl.program_id(0), ts, bs, s, cl)
    ya = []
    for c in range(4):
        sl = slice(128 * c, 128 * c + 128)
        sg = slice(512 + 128 * c, 512 + 128 * c + 128)
        sx = slice(1024 + 128 * c, 1024 + 128 * c + 128)
        u = sc_ref[:, sg] * sc_ref[:, sx]
        pr = jnp.where(first, 0.0, prev_ref[7:8, sg] * prev_ref[7:8, sx])
        nx = jnp.where(last, 0.0, next_ref[0:1, sg] * next_ref[0:1, sx])
        up, un = _shifted(u, pr, nx)
        w = cw_ref[:, sl]
        ya.append((sc_ref[:, sl] * (up * w[0:1] + u * w[1:2] + un * w[2:3])).astype(BF16))
    yb = []
    for h in range(DN_HEADS):
        hs = slice(DN_HD * h, DN_HD * h + DN_HD)
        o = of_ref[:, hs] + ob_ref[:, hs]
        yb.append((_rms(o, on_ref[...]) * _silu(z_ref[:, hs])).astype(BF16))
    ycat = jnp.concatenate(ya + yb, axis=1)
    y = _dot(ycat, wo_ref[...])
    xn, h2, eidx, gates = _post_mixer(_token_rows(xl_ref, xc_ref, ts, bs), y, mod_ref[0], g2_ref[...], rw_ref[...],
                                      rb_ref[...])
    xn_ref[...] = xn
    h2_ref[...] = h2.astype(BF16)
    ei_ref[...] = eidx
    ga_ref[...] = gates


def _mix0(xl, xc, sc, of, ob, z, mods, cw, on, wo, g2, rw, rb, h2buf, ts, bs, s, cl, nb):
    d = xl.shape[1]
    r = bs + xc.shape[0]
    kern = functools.partial(_mix0_kernel, ts=ts, bs=bs, s=s, cl=cl)
    mi = functools.partial(_mod_index, tm=ts, bs=bs, s=s, nb=nb)
    row = lambda w: pl.BlockSpec((ts, w), lambda i: (i, 0))
    full = lambda a: pl.BlockSpec(a.shape, lambda i: (0,) * a.ndim)
    return pl.pallas_call(
        kern,
        grid=(r // ts,),
        in_specs=_token_specs(ts, bs, d) + [row(1536)] + _halo_specs(ts, 1536, r) + [
            row(512), row(512), row(512),
            pl.BlockSpec((1, N_MOD, d), lambda i: (mi(i), 0, 0)),
            full(cw), full(on), full(wo), full(g2), full(rw), full(rb), pl.BlockSpec(memory_space=pl.ANY)],
        out_specs=[row(d), row(d), pl.BlockSpec((8, ts), lambda i: (0, i)), row(128)],
        out_shape=[jax.ShapeDtypeStruct((r, d), F32), jax.ShapeDtypeStruct(h2buf.shape, BF16),
                   jax.ShapeDtypeStruct((8, r), I32), jax.ShapeDtypeStruct((r, 128), F32)],
        input_output_aliases={15: 1},
        compiler_params=_cp(("parallel",)),
        name="mix0",
    )(xl, xc, sc, sc, sc, of, ob, z, mods, cw, on, wo, g2, rw, rb, h2buf)


def _gmm_kernel(te_ref, tf_ref, tv_ref, x_ref, wg_ref, wu_ref, wd_ref, y_ref, wgb, wub, wdb):
    t = pl.program_id(0)

    @pl.when(tf_ref[t] == 1)
    def _():
        wgb[...] = wg_ref[0, 0].astype(BF16)
        wub[...] = wu_ref[0, 0].astype(BF16)
        wdb[...] = wd_ref[0, 0].astype(BF16)

    @pl.when(tv_ref[t] == 1)
    def _():
        x = x_ref[...]
        a = (_silu(_dot(x, wgb[...])) * _dot(x, wub[...])).astype(BF16)
        y_ref[...] = _dot(a, wdb[...]).astype(BF16)

    @pl.when(tv_ref[t] == 0)
    def _():
        y_ref[...] = jnp.zeros_like(y_ref)


def _gmm(xs, w_gate, w_up, w_down, layer, tile_expert, tile_first, tile_valid, tmm):
    p, d = xs.shape
    de = w_gate.shape[-1]
    nt = p // tmm
    grid_spec = pltpu.PrefetchScalarGridSpec(
        num_scalar_prefetch=3,
        grid=(nt,),
        in_specs=[pl.BlockSpec((tmm, d), lambda t, te, tf, tv: (t, 0)),
                  pl.BlockSpec((1, 1, d, de), lambda t, te, tf, tv: (layer, te[t], 0, 0)),
                  pl.BlockSpec((1, 1, d, de), lambda t, te, tf, tv: (layer, te[t], 0, 0)),
                  pl.BlockSpec((1, 1, de, d), lambda t, te, tf, tv: (layer, te[t], 0, 0))],
        out_specs=pl.BlockSpec((tmm, d), lambda t, te, tf, tv: (t, 0)),
        scratch_shapes=[pltpu.VMEM((d, de), BF16), pltpu.VMEM((d, de), BF16), pltpu.VMEM((de, d), BF16)],
    )
    return pl.pallas_call(
        _gmm_kernel,
        grid_spec=grid_spec,
        out_shape=jax.ShapeDtypeStruct((p, d), BF16),
        compiler_params=_cp(("arbitrary",)),
        name="gmm",
    )(tile_expert, tile_first, tile_valid, xs, w_gate, w_up, w_down)


def _moe(h2, t_tok, e_first, e_second, w_gate, w_up, w_down, layer, tmm):
    n = 2 * t_tok
    e_flat = jnp.concatenate([e_first, e_second])
    onehot = (e_flat[:, None] == jnp.arange(N_EXPERTS, dtype=I32)[None, :]).astype(I32)
    csum = jnp.cumsum(onehot, axis=0)
    counts = csum[-1]
    ptiles = (counts + tmm - 1) // tmm
    tile_end = jnp.cumsum(ptiles)
    dest = jnp.sum(onehot * (csum - 1 + ((tile_end - ptiles) * tmm)[None, :]), axis=1)
    nt = n // tmm + N_EXPERTS
    tid = jnp.arange(nt, dtype=I32)
    tile_valid = (tid < tile_end[-1]).astype(I32)
    te = jnp.minimum(jnp.sum((tile_end[None, :] <= tid[:, None]).astype(I32), axis=1), N_EXPERTS - 1)
    last_used = jnp.max(jnp.where(tile_valid == 1, te, 0))
    te = jnp.where(tile_valid == 1, te, last_used)
    tile_first = jnp.concatenate([jnp.ones((1,), I32), (te[1:] != te[:-1]).astype(I32)])
    order = jnp.argsort(e_flat, stable=True).astype(I32)
    seg_start = (tile_end - ptiles) * tmm
    shift = seg_start - (jnp.cumsum(counts) - counts)
    pos = jnp.arange(nt * tmm, dtype=I32)
    per_tile = lambda v: jnp.repeat(v[te], tmm)
    used = (pos - per_tile(seg_start) < per_tile(counts)) & (jnp.repeat(tile_valid, tmm) == 1)
    src = jnp.where(used, jnp.take(order, jnp.clip(pos - per_tile(shift), 0, n - 1)), pos) % t_tok
    assert h2.shape[0] == nt * tmm
    xs = jnp.take(h2, src, axis=0, mode="clip")
    ys = _gmm(xs, w_gate, w_up, w_down, layer, te, tile_first, tile_valid, tmm)
    return jnp.take(ys, dest, axis=0, mode="clip")


def _moe_combine(x, y1, y2, gates, m5):
    g = gates
    f = g[:, 0:1] * y1.astype(F32) + g[:, 1:2] * y2.astype(F32)
    return x + m5 * f


def _rope(x, cos, sin):
    n = x.shape[1]
    lane = lax.broadcasted_iota(I32, x.shape, 1)
    sw = jnp.where(lane % 32 < 16, pltpu.roll(x, n - 16, 1), pltpu.roll(x, 16, 1))
    reps = n // 128
    if reps > 1:
        cos = jnp.concatenate([cos] * reps, axis=1)
        sin = jnp.concatenate([sin] * reps, axis=1)
    return x * cos + sw * sin


def _inproj1_kernel(x_ref, y1_ref, y2_ref, ga_ref, m0_ref, m1_ref, g_ref, w_ref, cos_ref, sin_ref,
                    x1_ref, cq_ref, ckt_ref, cv_ref, dq_ref, dk_ref, dv_ref, *, tm, bs):
    x1 = _moe_combine(x_ref[...], y1_ref[...], y2_ref[...], ga_ref[...], m0_ref[0][5:6])
    x1_ref[...] = x1
    m = m1_ref[0]
    h = (_rms(x1, g_ref[...]) * (1.0 + m[1:2]) + m[0:1]).astype(BF16)
    in_lat = pl.program_id(0) * tm < bs
    cos, sin = cos_ref[...], sin_ref[...]
    scale = HEAD_DIM ** -0.5 * LOG2E
    cq = _dot(h, w_ref[:, 0:512])
    cq_ref[...] = (jnp.where(in_lat, _rope(cq, cos, sin), cq) * scale).astype(BF16)
    ck = _dot(h, w_ref[:, 512:640])
    ckt_ref[...] = jnp.where(in_lat, _rope(ck, cos, sin), ck).T.astype(BF16)
    cv_ref[...] = _dot(h, w_ref[:, 640:768]).astype(BF16)
    dq_ref[...] = (_dot(h, w_ref[:, 768:1280]) * scale).astype(BF16)
    dk_ref[...] = _dot(h, w_ref[:, 1280:1792]).astype(BF16)
    dv_ref[...] = _dot(h, w_ref[:, 1792:2304]).astype(BF16)


def _inproj1(x, y, gates, mods0, mods1, g, w, cos, sin, tm, bs, s, nb):
    r, d = x.shape
    second = pl.BlockSpec((tm, d), lambda i: (i + r // tm, 0))
    kern = functools.partial(_inproj1_kernel, tm=tm, bs=bs)
    mi = functools.partial(_mod_index, tm=tm, bs=bs, s=s, nb=nb)
    row = lambda wd: pl.BlockSpec((tm, wd), lambda i: (i, 0))
    modspec = pl.BlockSpec((1, N_MOD, d), lambda i: (mi(i), 0, 0))
    tab = pl.BlockSpec((tm, 128), lambda i: (jnp.where(i * tm < bs, (i * tm % s) // tm, 0), 0))
    shp = lambda wd, dt: jax.ShapeDtypeStruct((r, wd), dt)
    return pl.pallas_call(
        kern,
        grid=(r // tm,),
        in_specs=[row(d), row(d), second, row(128), modspec, modspec,
                  pl.BlockSpec((1, d), lambda i: (0, 0)), pl.BlockSpec(w.shape, lambda i: (0, 0)), tab, tab],
        out_specs=[row(d), row(512), pl.BlockSpec((128, tm), lambda i: (0, i)), row(128), row(512), row(512),
                   row(512)],
        out_shape=[shp(d, F32), shp(512, BF16), jax.ShapeDtypeStruct((128, r), BF16), shp(128, BF16),
                   shp(512, BF16), shp(512, BF16), shp(512, BF16)],
        compiler_params=_cp(("parallel",)),
        name="inproj1",
    )(x, y, y, gates, mods0, mods1, g, w, cos, sin)


def _swa_kernel(q_ref, ktp_ref, ktc_ref, ktn_ref, ktx_ref, vp_ref, vc_ref, vn_ref, vx_ref, sink_ref, o_ref,
                *, nblk, cl):
    i = pl.program_id(1)
    wb = SWA_BLOCK
    nloc = 3 * wb
    kt = jnp.concatenate([ktp_ref[...], ktc_ref[...], ktn_ref[...], ktx_ref[...]], axis=1)
    vv = jnp.concatenate([vp_ref[...], vc_ref[...], vn_ref[...], vx_ref[...]], axis=0)
    a_i = lax.broadcasted_iota(I32, (2 * wb, nloc), 0) % wb
    c_i = lax.broadcasted_iota(I32, (2 * wb, nloc), 1)
    lo = jnp.where(i > 0, 0, wb)
    hi = jnp.where(i < nblk - 1, 3 * wb, 2 * wb)
    ok = (c_i >= a_i) & (c_i <= a_i + 2 * SWA_WINDOW) & (c_i >= lo) & (c_i < hi)
    half = lax.broadcasted_iota(I32, (1, 128), 1) // HEAD_DIM
    zero = jnp.zeros((), BF16)
    sink = sink_ref[...]
    def scores(g):
        q2 = q_ref[:, 128 * g:128 * g + 128]
        qst = jnp.concatenate([jnp.where(half == 0, q2, zero), jnp.where(half == 1, q2, zero)], axis=0)
        s_all = _dot(qst, kt)
        return jnp.concatenate([jnp.where(ok, s_all[:, 0:nloc], NEG), s_all[:, nloc:]], axis=1)

    def softmax(g, sc):
        sk = jnp.concatenate([jnp.broadcast_to(sink[0:1, 2 * g + a:2 * g + a + 1], (wb, 1)) for a in range(2)],
                             axis=0)
        m = jnp.maximum(jnp.max(sc, axis=-1, keepdims=True), sk)
        p = jnp.exp2(sc - m)
        return p.astype(BF16), jnp.sum(p, axis=-1, keepdims=True) + jnp.exp2(sk - m)

    def output(g, p, den):
        ost = _dot(p, vv) / den
        o_ref[:, 128 * g:128 * g + 128] = jnp.where(half == 0, ost[0:wb], ost[wb:2 * wb]).astype(BF16)

    sc, pr = {}, {}
    for step in range(6):
        if step < 4:
            sc[step] = scores(step)
        if 1 <= step < 5:
            pr[step - 1] = softmax(step - 1, sc.pop(step - 1))
        if step >= 2:
            output(step - 2, *pr.pop(step - 2))


def _swa(cq, ckt, cv, sink_row, nb, s, cl, bs):
    wb = SWA_BLOCK
    nblk = s // wb
    kern = functools.partial(_swa_kernel, nblk=nblk, cl=cl)
    prev = lambda b, i: b * nblk + jnp.maximum(i - 1, 0)
    own = lambda b, i: b * nblk + i
    nxt = lambda b, i: b * nblk + jnp.minimum(i + 1, nblk - 1)
    ktspec = lambda f: pl.BlockSpec((128, wb), lambda b, i: (0, f(b, i)))
    vspec = lambda f: pl.BlockSpec((wb, 128), lambda b, i: (f(b, i), 0))
    return pl.pallas_call(
        kern,
        grid=(nb, nblk),
        in_specs=[pl.BlockSpec((wb, 512), lambda b, i: (own(b, i), 0)),
                  ktspec(prev), ktspec(own), ktspec(nxt),
                  pl.BlockSpec((128, cl), lambda b, i: (0, bs // cl + b)),
                  vspec(prev), vspec(own), vspec(nxt),
                  pl.BlockSpec((cl, 128), lambda b, i: (bs // cl + b, 0)),
                  pl.BlockSpec((1, 128), lambda b, i: (0, 0))],
        out_specs=pl.BlockSpec((wb, 512), lambda b, i: (own(b, i), 0)),
        out_shape=jax.ShapeDtypeStruct((bs, 512), BF16),
        compiler_params=_cp(("parallel", "parallel")),
        name="swa",
    )(cq, ckt, ckt, ckt, ckt, cv, cv, cv, cv, sink_row)


def _na_kernel(q_ref, k_ref, v_ref, kx_ref, vx_ref, bias_ref, o_ref, *, rows, unroll):
    half = lax.broadcasted_iota(I32, (1, 128), 1) // HEAD_DIM
    zero = jnp.zeros((), BF16)
    kx, vx = kx_ref[...], vx_ref[...]
    span = NA_KH * GRID_W

    def scores(r):
        rs = jnp.clip(r - NA_KH // 2, 0, rows - NA_KH)
        off = rs - r + NA_KH - 1
        q0 = pl.multiple_of(r * GRID_W, GRID_W)
        k0 = pl.multiple_of(rs * GRID_W, GRID_W)
        q2 = q_ref[pl.ds(q0, GRID_W), :]
        qst = jnp.concatenate([jnp.where(half == 0, q2, zero), jnp.where(half == 1, q2, zero)], axis=0)
        s_loc = (_dot_nt(qst, k_ref[pl.ds(k0, span), :])
                 + jnp.concatenate([bias_ref[0, off], bias_ref[1, off]], axis=0))
        return q0, k0, s_loc, _dot_nt(qst, kx)

    def softmax(s_loc, s_ctx):
        m = jnp.maximum(jnp.max(s_loc, axis=-1, keepdims=True), jnp.max(s_ctx, axis=-1, keepdims=True))
        p_loc = jnp.exp2(s_loc - m)
        p_ctx = jnp.exp2(s_ctx - m)
        den = jnp.sum(p_loc, axis=-1, keepdims=True) + jnp.sum(p_ctx, axis=-1, keepdims=True)
        return p_loc.astype(BF16), p_ctx.astype(BF16), den

    def body(i, carry):
        sc = [scores(i * unroll + j) for j in range(unroll)]
        pr = [softmax(s_loc, s_ctx) for (_, _, s_loc, s_ctx) in sc]
        for (q0, k0, _, _), (p_loc, p_ctx, den) in zip(sc, pr):
            ost = (_dot(p_loc, v_ref[pl.ds(k0, span), :]) + _dot(p_ctx, vx)) / den
            o = jnp.where(half == 0, ost[0:GRID_W], ost[GRID_W:2 * GRID_W])
            o_ref[pl.ds(q0, GRID_W), :] = o.astype(BF16)
        return carry

    lax.fori_loop(0, rows // unroll, body, 0)


def _na(dq, dk, dv, bias, nb, s, cl, bs):
    rows = s // GRID_W
    unroll = 4
    assert rows % unroll == 0
    kern = functools.partial(_na_kernel, rows=rows, unroll=unroll)
    seq = pl.BlockSpec((s, 128), lambda b, p: (b, p))
    ctx = pl.BlockSpec((cl, 128), lambda b, p: (bs // cl + b, p))
    return pl.pallas_call(
        kern,
        grid=(nb, NA_HEADS // 2),
        in_specs=[seq, seq, seq, ctx, ctx,
                  pl.BlockSpec((2, NA_KH, GRID_W, NA_KH * GRID_W), lambda b, p: (p, 0, 0, 0))],
        out_specs=seq,
        out_shape=jax.ShapeDtypeStruct((bs, 512), BF16),
        compiler_params=_cp(("parallel", "parallel")),
        name="na",
    )(dq, dk, dv, dk, dv, bias)


def _na_bias_table(rpb):
    c = np.arange(GRID_W)
    qs = np.clip(c - NA_KW // 2, 0, GRID_W - NA_KW)
    kc = np.arange(GRID_W)
    ok = (kc[None, :] >= qs[:, None]) & (kc[None, :] < qs[:, None] + NA_KW)
    dc = np.clip(kc[None, :] - c[:, None] + NA_KW - 1, 0, 2 * NA_KW - 2)
    sel = (np.arange(2 * NA_KW - 1)[:, None, None] == dc[None]).astype(np.float32)
    cols = jnp.einsum("hab,bck->hcak", rpb.astype(F32), sel, precision=HI)
    cols = jnp.where(ok[None, :, None, :], cols * LOG2E, NEG)
    return jnp.stack([cols[:, :, off:off + NA_KH].reshape(NA_HEADS, GRID_W, NA_KH * GRID_W)
                      for off in range(NA_KH)], axis=1)


def _mix1_kernel(x_ref, oc_ref, od_ref, mod_ref, wo_ref, g2_ref, rw_ref, rb_ref, h2buf_ref,
                 xn_ref, h2_ref, ei_ref, ga_ref):
    del h2buf_ref
    y = _dot(oc_ref[...], wo_ref[0:512, :]) + _dot(od_ref[...], wo_ref[512:1024, :])
    xn, h2, eidx, gates = _post_mixer(x_ref[...], y, mod_ref[0], g2_ref[...], rw_ref[...], rb_ref[...])
    xn_ref[...] = xn
    h2_ref[...] = h2.astype(BF16)
    ei_ref[...] = eidx
    ga_ref[...] = gates


def _mix1(x, oc, od, mods, wo, g2, rw, rb, h2buf, tm, bs, s):
    d = x.shape[1]
    row = lambda w: pl.BlockSpec((tm, w), lambda i: (i, 0))
    full = lambda a: pl.BlockSpec(a.shape, lambda i: (0,) * a.ndim)
    return pl.pallas_call(
        _mix1_kernel,
        grid=(bs // tm,),
        in_specs=[row(d), row(512), row(512), pl.BlockSpec((1, N_MOD, d), lambda i: (i * tm // s, 0, 0)),
                  full(wo), full(g2), full(rw), full(rb), pl.BlockSpec(memory_space=pl.ANY)],
        out_specs=[row(d), row(d), pl.BlockSpec((8, tm), lambda i: (0, i)), row(128)],
        out_shape=[jax.ShapeDtypeStruct((bs, d), F32), jax.ShapeDtypeStruct(h2buf.shape, BF16),
                   jax.ShapeDtypeStruct((8, bs), I32), jax.ShapeDtypeStruct((bs, 128), F32)],
        input_output_aliases={8: 1},
        compiler_params=_cp(("parallel",)),
        name="mix1",
    )(x, oc, od, mods, wo, g2, rw, rb, h2buf)


def _final_kernel(x_ref, y1_ref, y2_ref, ga_ref, mod_ref, g_ref, o_ref):
    x = _moe_combine(x_ref[...], y1_ref[...], y2_ref[...], ga_ref[...], mod_ref[0][5:6])
    o_ref[...] = _rms(x, g_ref[...])


def _final(x, y, gates, mods, g, tm, s):
    r, d = x.shape
    second = pl.BlockSpec((tm, d), lambda i: (i + r // tm, 0))
    row = lambda w: pl.BlockSpec((tm, w), lambda i: (i, 0))
    return pl.pallas_call(
        _final_kernel,
        grid=(r // tm,),
        in_specs=[row(d), row(d), second, row(128), pl.BlockSpec((1, N_MOD, d), lambda i: (i * tm // s, 0, 0)),
                  pl.BlockSpec((1, d), lambda i: (0, 0))],
        out_specs=row(d),
        out_shape=jax.ShapeDtypeStruct((r, d), F32),
        compiler_params=_cp(("parallel",)),
        name="final",
    )(x, y, y, gates, mods, g)


def _rope_tables(s):
    nf = HEAD_DIM // 4
    t = np.arange(s)
    inv = ROPE_THETA ** (-np.arange(nf, dtype=np.float64) / nf)
    ar = (t // GRID_W)[:, None] * inv
    ac = (t % GRID_W)[:, None] * inv
    cos = np.concatenate([np.cos(ar), np.cos(ar), np.cos(ac), np.cos(ac)], axis=1)
    sin = np.concatenate([-np.sin(ar), np.sin(ar), -np.sin(ac), np.sin(ac)], axis=1)
    return (jnp.asarray(np.concatenate([cos, cos], axis=1), F32),
            jnp.asarray(np.concatenate([sin, sin], axis=1), F32))


def kernel(x, c, ctx, c_ctx, ada_w, ada_b, norm1_g, norm2_g, ev_w_in, ev_w_out, sc_conv_w, dn_conv_w, dn_a_log, dn_dt_bias, dn_onorm_g, od_w_in, od_w_out, swa_sink, na_rpb, router_w, router_b, moe_w_gate, moe_w_up, moe_w_down, final_g):
    nb, s, d = x.shape
    cl = ctx.shape[1]
    bs = nb * s
    tm = 512
    ts = 256
    tmm = 512
    assert d == 1024 and s % tm == 0 and (nb * cl) % tm == 0 and cl % ts == 0 and s % ts == 0
    assert s // GRID_W >= NA_KH and bs % cl == 0 and nb + 1 <= 8

    xl, xc = x.reshape(bs, d), ctx.reshape(nb * cl, d)
    cc = jnp.zeros((8, d), F32).at[:nb].set(c).at[nb].set(c_ctx)
    mods = _ada(cc, ada_w, ada_b).reshape(ada_w.shape[0], 8, N_MOD, d)
    rw32 = jnp.pad(router_w, ((0, 0), (0, 128 - N_EXPERTS)))
    rw_hi = rw32.astype(BF16)
    rw = jnp.concatenate([rw_hi, (rw32 - rw_hi.astype(F32)).astype(BF16)], axis=1)
    rb = router_b.reshape(N_EXPERTS, 1)
    row = lambda v: v.reshape(1, -1)

    w_in0 = jnp.pad(ev_w_in[0], ((0, 0), (0, 3712 - ev_w_in.shape[-1]))).astype(BF16)
    sc, qkv, z, bg = _inproj0(xl, xc, mods[0], row(norm1_g[0]), w_in0, tm, bs, s, nb)
    pad16 = lambda v: jnp.pad(v.reshape(-1), (8, 128 - 16)).reshape(1, 128)
    uf, ub, wf, wb, qf, qb, kf, kb, af, ab, gc = _dnchunk(qkv, bg, dn_conv_w[0], pad16(dn_a_log[0]),
                                                          pad16(dn_dt_bias[0]), ts, bs, s, cl)
    of, ob = _dnscan(uf, ub, wf, wb, qf, qb, kf, kb, af, ab, gc, nb, s, cl, bs)
    r_all = bs + nb * cl
    moe_rows = lambda t: (2 * t // tmm + N_EXPERTS) * tmm
    x0, h2, ei, ga = _mix0(xl, xc, sc, of, ob, z, mods[0], sc_conv_w[0], row(dn_onorm_g[0]),
                           ev_w_out[0].astype(BF16), row(norm2_g[0]), rw, rb,
                           jnp.zeros((moe_rows(r_all), d), BF16), ts, bs, s, cl, nb)
    y = _moe(h2, r_all, ei[0], ei[1], moe_w_gate, moe_w_up, moe_w_down, 0, tmm)

    perm = np.concatenate([np.arange(HEAD_DIM) + HEAD_DIM * (g + 4 * a) for g in range(4) for a in range(2)])
    w1 = od_w_in[0]
    w_in1 = jnp.concatenate([w1[:, 0:512][:, perm], w1[:, 512:]], axis=1).astype(BF16)
    wo1 = od_w_out[0]
    w_out1 = jnp.concatenate([wo1[0:512][perm], wo1[512:]], axis=0).astype(BF16)
    sink_row = jnp.pad(swa_sink[0][np.array([g + 4 * a for g in range(4) for a in range(2)])] * LOG2E,
                       (0, 128 - SWA_HEADS)).reshape(1, 128)
    cos, sin = _rope_tables(s)
    x1, cq, ckt, cv, dq, dk, dv = _inproj1(x0, y, ga, mods[0], mods[1], row(norm1_g[1]), w_in1, cos, sin,
                                           tm, bs, s, nb)
    oc = _swa(cq, ckt, cv, sink_row, nb, s, cl, bs)
    od = _na(dq, dk, dv, _na_bias_table(na_rpb[0]), nb, s, cl, bs)
    x2, h2, ei, ga = _mix1(x1, oc, od, mods[1], w_out1, row(norm2_g[1]), rw, rb,
                           jnp.zeros((moe_rows(bs), d), BF16), tm, bs, s)
    y = _moe(h2, bs, ei[0], ei[1], moe_w_gate, moe_w_up, moe_w_down, 1, tmm)
    out = _final(x2, y, ga, mods[1], row(final_g), tm, s)
    return out.reshape(nb, s, d)
```

```python
import functools
import math

import numpy as np
import jax
import jax.numpy as jnp
from jax import lax
from jax.experimental import pallas as pl
from jax.experimental.pallas import tpu as pltpu

F32 = jnp.float32
BF16 = jnp.bfloat16
I32 = jnp.int32
HI = lax.Precision.HIGHEST

EPS = 1e-6
N_MOD = 6
GRID_W = 64
HEAD_DIM = 64
DN_HEADS = 4
DN_HD = 128
DN_CHUNK = 64
SWA_HEADS = 8
SWA_KV = 2
SWA_BLOCK = 128
SWA_WINDOW = 128
NA_HEADS = 8
NA_KH = 8
NA_KW = 16
ROPE_THETA = 10000.0
N_EXPERTS = 16
N_GROUPS = 4
NEG = -1e30
LOG2E = 1.4426950408889634
VMEM_LIMIT = 56 * 1024 * 1024


def _cp(sem, vmem=VMEM_LIMIT):
    return pltpu.CompilerParams(dimension_semantics=sem, vmem_limit_bytes=vmem)


def _dot(a, b, precision=None):
    return jnp.dot(a, b, preferred_element_type=F32, precision=precision)


def _dot_nt(a, b, precision=None):
    return lax.dot_general(a, b, (((1,), (1,)), ((), ())), preferred_element_type=F32, precision=precision)


def _dot_tn(a, b, precision=None):
    return lax.dot_general(a, b, (((0,), (0,)), ((), ())), preferred_element_type=F32, precision=precision)


def _silu(x):
    return x * jax.nn.sigmoid(x)


def _rms(x, g):
    return x * lax.rsqrt(jnp.mean(x * x, axis=-1, keepdims=True) + EPS) * g


def _ada_kernel(cc_ref, w_ref, b_ref, o_ref):
    a = _silu(cc_ref[...])
    o_ref[0] = _dot(a, w_ref[0], HI) + b_ref[0]


def _ada(cc, ada_w, ada_b):
    depth, d, n = ada_w.shape
    tn = 1536
    return pl.pallas_call(
        _ada_kernel,
        grid=(depth, n // tn),
        in_specs=[pl.BlockSpec((8, d), lambda l, j: (0, 0)),
                  pl.BlockSpec((1, d, tn), lambda l, j: (l, 0, j)),
                  pl.BlockSpec((1, 1, tn), lambda l, j: (l, 0, j))],
        out_specs=pl.BlockSpec((1, 8, tn), lambda l, j: (l, 0, j)),
        out_shape=jax.ShapeDtypeStruct((depth, 8, n), F32),
        compiler_params=_cp(("parallel", "parallel")),
        name="ada",
    )(cc, ada_w, ada_b.reshape(depth, 1, n))


def _mod_index(i, tm, bs, s, nb):
    row0 = i * tm
    return jnp.where(row0 < bs, row0 // s, nb)


def _token_specs(tm, bs, d):
    nlat = bs // tm
    return [pl.BlockSpec((tm, d), lambda i: (jnp.minimum(i, nlat - 1), 0)),
            pl.BlockSpec((tm, d), lambda i: (jnp.maximum(i - nlat, 0), 0))]


def _token_rows(xl_ref, xc_ref, tm, bs):
    return jnp.where(pl.program_id(0) * tm < bs, xl_ref[...], xc_ref[...])


def _inproj0_kernel(xl_ref, xc_ref, mod_ref, g_ref, w_ref, sc_ref, qkv_ref, z_ref, bg_ref, *, tm, bs):
    m = mod_ref[0]
    h = (_rms(_token_rows(xl_ref, xc_ref, tm, bs), g_ref[...]) * (1.0 + m[1:2]) + m[0:1]).astype(BF16)
    sc_ref[...] = _dot(h, w_ref[:, 0:1536]).astype(BF16)
    qkv_ref[...] = _dot(h, w_ref[:, 1536:3072]).astype(BF16)
    z_ref[...] = _dot(h, w_ref[:, 3072:3584]).astype(BF16)
    bg_ref[...] = _dot(h, w_ref[:, 3584:3712])


def _inproj0(xl, xc, mods, g, w, tm, bs, s, nb):
    d = xl.shape[1]
    r = bs + xc.shape[0]
    mi = functools.partial(_mod_index, tm=tm, bs=bs, s=s, nb=nb)
    return pl.pallas_call(
        functools.partial(_inproj0_kernel, tm=tm, bs=bs),
        grid=(r // tm,),
        in_specs=_token_specs(tm, bs, d) + [
            pl.BlockSpec((1, N_MOD, d), lambda i: (mi(i), 0, 0)),
            pl.BlockSpec((1, d), lambda i: (0, 0)),
            pl.BlockSpec(w.shape, lambda i: (0, 0))],
        out_specs=[pl.BlockSpec((tm, 1536), lambda i: (i, 0)),
                   pl.BlockSpec((tm, 1536), lambda i: (i, 0)),
                   pl.BlockSpec((tm, 512), lambda i: (i, 0)),
                   pl.BlockSpec((tm, 128), lambda i: (i, 0))],
        out_shape=[jax.ShapeDtypeStruct((r, 1536), BF16), jax.ShapeDtypeStruct((r, 1536), BF16),
                   jax.ShapeDtypeStruct((r, 512), BF16), jax.ShapeDtypeStruct((r, 128), F32)],
        compiler_params=_cp(("parallel",)),
        name="inproj0",
    )(xl, xc, mods, g, w)


def _seq_edges(i, ts, bs, s, cl):
    row0 = i * ts
    in_lat = row0 < bs
    r_in = jnp.where(in_lat, row0 % s, (row0 - bs) % cl)
    seqlen = jnp.where(in_lat, s, cl)
    return r_in == 0, r_in + ts == seqlen


def _shifted(x, prev_row, next_row):
    n = x.shape[0]
    rows = lax.broadcasted_iota(I32, x.shape, 0)
    xp = jnp.where(rows == 0, prev_row, pltpu.roll(x, 1, 0))
    xn = jnp.where(rows == n - 1, next_row, pltpu.roll(x, n - 1, 0))
    return xp, xn


HALO = 16


def _halo_specs(ts, width, r):
    nblk = r // HALO
    k = ts // HALO
    return [pl.BlockSpec((HALO, width), lambda i: (jnp.maximum(i * k - 1, 0), 0)),
            pl.BlockSpec((HALO, width), lambda i: (jnp.minimum((i + 1) * k, nblk - 1), 0))]


def _dnprep_tile(x_ref, prev_ref, next_ref, bg_ref, cw_ref, alog_ref, dt_ref,
                 q_ref, k_ref, v_ref, bga_ref, *, ts, bs, s, cl):
    first, last = _seq_edges(pl.program_id(0), ts, bs, s, cl)
    for c in range(12):
        sl = slice(128 * c, 128 * c + 128)
        x = x_ref[:, sl].astype(F32)
        pr = jnp.where(first, 0.0, prev_ref[HALO - 1:HALO, sl].astype(F32))
        nx = jnp.where(last, 0.0, next_ref[0:1, sl].astype(F32))
        xp, xn = _shifted(x, pr, nx)
        w = cw_ref[:, sl]
        y = _silu(xp * w[0:1] + x * w[1:2] + xn * w[2:3])
        hs = slice(128 * (c % 4), 128 * (c % 4) + 128)
        if c < 8:
            y = y * lax.rsqrt(jnp.sum(y * y, axis=-1, keepdims=True) + EPS)
        if c < 4:
            q_ref[:, hs] = y * DN_HD ** -0.5
        elif c < 8:
            k_ref[:, hs] = y
        else:
            v_ref[:, hs] = y
    b = bg_ref[...]
    cols = lax.broadcasted_iota(I32, b.shape, 1)
    beta = jax.nn.sigmoid(b)
    t = b + dt_ref[...]
    softplus = jnp.maximum(t, 0.0) + jnp.log1p(jnp.exp(-jnp.abs(t)))
    g = -jnp.exp(alog_ref[...]) * softplus
    bga_ref[...] = jnp.where(cols < 8, beta, jnp.where(cols < 16, g, 0.0))


def _dnchunk_kernel(x_ref, prev_ref, next_ref, bgraw_ref, cw_ref, alog_ref, dt_ref,
                    uf_ref, ub_ref, wf_ref, wb_ref, qf_ref, qb_ref, kf_ref, kb_ref, af_ref, ab_ref, gc_ref,
                    q_ref, k_ref, v_ref, bg_ref, *, nchunks, bs, s, cl):
    _dnprep_tile(x_ref, prev_ref, next_ref, bgraw_ref, cw_ref, alog_ref, dt_ref, q_ref, k_ref, v_ref, bg_ref,
                 ts=nchunks * DN_CHUNK, bs=bs, s=s, cl=cl)
    outs = ((uf_ref, wf_ref, qf_ref, kf_ref, af_ref), (ub_ref, wb_ref, qb_ref, kb_ref, ab_ref))
    c, nh = DN_CHUNK, DN_HEADS
    head_of_col = lax.broadcasted_iota(I32, (1, nh * c), 1) // c
    zero = jnp.zeros((), BF16)

    def block_diag(x):
        return jnp.concatenate([jnp.where(head_of_col == h, x, zero) for h in range(nh)], axis=0)

    chains = []
    for cc in range(nchunks):
        chains += _dnchunk_setup(slice(cc * c, (cc + 1) * c), q_ref, k_ref, v_ref, bg_ref, gc_ref)
    for ch in chains:
        ch["tm"] = ch["nmat"]
        nb16 = ch["nmat"].astype(BF16)
        ch["npow"] = _dot(nb16, block_diag(nb16))
    for _ in range(4):
        for ch in chains:
            nb16 = ch["npow"].astype(BF16)
            ch["both"] = _dot(jnp.concatenate([nb16, ch["tm"].astype(BF16)], axis=0), block_diag(nb16))
        for ch in chains:
            ch["tm"] = ch["tm"] + ch["npow"] + ch["both"][c:2 * c]
            ch["npow"] = ch["both"][0:c]
    for ch in chains:
        ch["both"] = _dot(ch["tm"].astype(BF16), block_diag(ch["npow"].astype(BF16)))
    for ch in chains:
        ch["tm"] = ch["tm"] + ch["npow"] + ch["both"]
    for ch in chains:
        ch["uw"] = ch["rhs"] + _dot(block_diag(ch["tm"].astype(BF16)), ch["rhs"].astype(BF16))
    for ch in chains:
        u_ref, w_ref, qd_ref, kd_ref, at_ref = outs[ch["d"]]
        rows, uw = ch["rows"], ch["uw"]
        for h in range(nh):
            hs = slice(DN_HD * h, DN_HD * h + DN_HD)
            rs = slice(c * h, c * h + c)
            u_ref[rows, hs] = uw[rs, 0:DN_HD]
            w_ref[rows, hs] = uw[rs, DN_HD:2 * DN_HD].astype(BF16)
            qd_ref[rows, hs] = ch["qd"][rs]
        kd_ref[slice(2 * rows.start, 2 * rows.stop)] = ch["kd"].T.astype(BF16)
        at_ref[rows] = ch["att"].astype(BF16)


def _dnchunk_setup(rows, q_ref, k_ref, v_ref, bg_ref, gc_ref):
    c, nh = DN_CHUNK, DN_HEADS
    n = c * nh
    bg = bg_ref[rows]
    i64 = lax.broadcasted_iota(I32, (c, c), 0)
    j64 = lax.broadcasted_iota(I32, (c, c), 1)
    cols = lax.broadcasted_iota(I32, bg.shape, 1)
    gcf = _dot((i64 >= j64).astype(F32), bg, HI)
    gcb = _dot((i64 <= j64).astype(F32), bg, HI)
    gc = jnp.where(cols >= 12, gcb, gcf)
    gc_ref[rows] = gc
    gct = gc.T
    ii = lax.broadcasted_iota(I32, (c, n), 0)
    jj = lax.broadcasted_iota(I32, (c, n), 1)
    head_of_col = jj // c
    jj = jj % c

    def stack(ref):
        return jnp.concatenate([ref[rows, DN_HD * h:DN_HD * h + DN_HD] for h in range(nh)], axis=0)

    def stacked_cols(arr, r0, r1, col0):
        return jnp.concatenate([jnp.broadcast_to(arr[r0:r1, col0 + h:col0 + h + 1], (c, DN_HD)) for h in range(nh)],
                               axis=0)

    def side_by_side_cols(arr, col0):
        out = arr[:, col0 + nh - 1:col0 + nh]
        for h in range(nh - 2, -1, -1):
            out = jnp.where(head_of_col == h, arr[:, col0 + h:col0 + h + 1], out)
        return out

    def diag_blocks(x):
        out = x[(nh - 1) * c:nh * c]
        for h in range(nh - 2, -1, -1):
            out = jnp.where(head_of_col == h, x[h * c:(h + 1) * c], out)
        return out

    kst, qst, vst = stack(k_ref), stack(q_ref), stack(v_ref)
    kb = kst.astype(BF16)
    kq = _dot_nt(jnp.concatenate([kb, qst.astype(BF16)], axis=0), kb)
    kkt, qkt = diag_blocks(kq[0:n]), diag_blocks(kq[n:2 * n])
    chains = []
    for d in range(2):
        incl = (ii >= jj) if d == 0 else (ii <= jj)
        strict = (ii > jj) if d == 0 else (ii < jj)
        last = c - 1 if d == 0 else 0
        grow = jnp.concatenate([gct[8 + 4 * d + h:9 + 4 * d + h, :] for h in range(nh)], axis=1)
        decay = jnp.exp(jnp.where(incl, side_by_side_cols(gc, 8 + 4 * d) - grow, NEG))
        nmat = jnp.where(strict, -(side_by_side_cols(bg, 4 * d) * kkt * decay), 0.0)
        b1 = stacked_cols(bg, 0, c, 4 * d)
        gcol = stacked_cols(gc, 0, c, 8 + 4 * d)
        glast = stacked_cols(gc, last, last + 1, 8 + 4 * d)
        e1 = jnp.exp(gcol)
        chains.append(dict(
            d=d, rows=rows, nmat=nmat,
            rhs=jnp.concatenate([b1 * vst, (b1 * e1) * kst], axis=1),
            qd=(qst * e1).astype(BF16),
            kd=kst * jnp.exp(glast - gcol),
            att=qkt * decay))
    return chains


def _dnchunk(qkv, bg, cw, alog_row, dt_row, ts, bs, s, cl):
    r = qkv.shape[0]
    nchunks = ts // DN_CHUNK
    c = ts
    row = lambda w: pl.BlockSpec((c, w), lambda i: (i, 0))
    shp = lambda w, dt: jax.ShapeDtypeStruct((r, w), dt)
    return pl.pallas_call(
        functools.partial(_dnchunk_kernel, nchunks=nchunks, bs=bs, s=s, cl=cl),
        grid=(r // c,),
        in_specs=[row(1536)] + _halo_specs(ts, 1536, r) + [
            row(128),
            pl.BlockSpec((3, 1536), lambda i: (0, 0)),
            pl.BlockSpec((1, 128), lambda i: (0, 0)),
            pl.BlockSpec((1, 128), lambda i: (0, 0))],
        scratch_shapes=[pltpu.VMEM((c, 512), F32)] * 3 + [pltpu.VMEM((c, 128), F32)],
        out_specs=[row(512)] * 6 + [pl.BlockSpec((2 * c, 256), lambda i: (i, 0))] * 2 + [row(256), row(256), row(128)],
        out_shape=([shp(512, F32)] * 2 + [shp(512, BF16)] * 4 + [jax.ShapeDtypeStruct((2 * r, 256), BF16)] * 2
                   + [shp(256, BF16)] * 2 + [shp(128, F32)]),
        compiler_params=_cp(("parallel",)),
        name="dnchunk",
    )(qkv, qkv, qkv, bg, cw, alog_row, dt_row)


def _dnscan_kernel(uf_ref, wf_ref, qf_ref, kf_ref, af_ref, gf_ref,
                   ub_ref, wb_ref, qb_ref, kb_ref, ab_ref, gb_ref,
                   of_ref, ob_ref, s_ref, *, nsub):
    @pl.when(pl.program_id(1) == 0)
    def _():
        s_ref[...] = jnp.zeros_like(s_ref)

    c, nh = DN_CHUNK, DN_HEADS
    head_of_lane = lax.broadcasted_iota(I32, (1, nh * DN_HD), 1) // DN_HD
    head_of_col = lax.broadcasted_iota(I32, (1, nh * c), 1) // c
    zero = jnp.zeros((), BF16)

    def block_diag(tile, head_ids):
        return jnp.concatenate([jnp.where(head_ids == h, tile, zero) for h in range(nh)], axis=0)

    dirs = ((uf_ref, wf_ref, qf_ref, kf_ref, af_ref, gf_ref, of_ref, c - 1),
            (ub_ref, wb_ref, qb_ref, kb_ref, ab_ref, gb_ref, ob_ref, 0))
    states = [s_ref[0], s_ref[1]]
    for sub in range(nsub):
        work = []
        for d, (u_ref, w_ref, qd_ref, kd_ref, at_ref, g_ref, o_ref, last) in enumerate(dirs):
            k = sub if d == 0 else nsub - 1 - sub
            rows = slice(c * k, c * k + c)
            g = g_ref[rows]
            decay = jnp.concatenate(
                [jnp.broadcast_to(jnp.exp(g[last:last + 1, 8 + 4 * d + h:9 + 4 * d + h]), (DN_HD, DN_HD))
                 for h in range(nh)], axis=0)
            ust = jnp.concatenate([u_ref[rows, DN_HD * h:DN_HD * h + DN_HD] for h in range(nh)], axis=0)
            stb = states[d].astype(BF16)
            vnew = ust - _dot(block_diag(w_ref[rows], head_of_lane), stb)
            work.append((rows, decay, stb, vnew.astype(BF16)))
        for d, (u_ref, w_ref, qd_ref, kd_ref, at_ref, g_ref, o_ref, last) in enumerate(dirs):
            rows, decay, stb, vnb = work[d]
            o = (_dot(block_diag(qd_ref[rows], head_of_lane), stb)
                 + _dot(block_diag(at_ref[rows], head_of_col), vnb))
            kdt = kd_ref[slice(2 * rows.start, 2 * rows.stop)]
            states[d] = states[d] * decay + _dot(block_diag(kdt, head_of_col), vnb)
            for h in range(nh):
                o_ref[rows, DN_HD * h:DN_HD * h + DN_HD] = o[c * h:c * h + c].astype(BF16)
    s_ref[0] = states[0]
    s_ref[1] = states[1]


def _dnscan(uf, ub, wf, wb, qf, qb, kf, kb, af, ab, gc, nb, s, cl, bs):
    r = uf.shape[0]
    nsub = 2
    c = DN_CHUNK * nsub
    assert cl % c == 0 and s % c == 0 and bs % c == 0
    ncc, ncl = cl // c, s // c
    ns = ncc + ncl

    def fwd(b, t):
        return jnp.where(t < ncc, bs // c + b * ncc + t, b * ncl + t - ncc)

    def bwd(b, t):
        return jnp.where(t < ncc, bs // c + b * ncc + (ncc - 1 - t), b * ncl + (ncl - 1 - (t - ncc)))

    def specs(idx):
        blk = lambda w: pl.BlockSpec((c, w), lambda b, t: (idx(b, t), 0))
        return [blk(512), blk(512), blk(512), pl.BlockSpec((2 * c, 256), lambda b, t: (idx(b, t), 0)), blk(256),
                blk(128)]

    return pl.pallas_call(
        functools.partial(_dnscan_kernel, nsub=nsub),
        grid=(nb, ns),
        in_specs=specs(fwd) + specs(bwd),
        out_specs=[pl.BlockSpec((c, 512), lambda b, t: (fwd(b, t), 0)),
                   pl.BlockSpec((c, 512), lambda b, t: (bwd(b, t), 0))],
        out_shape=[jax.ShapeDtypeStruct((r, 512), BF16)] * 2,
        scratch_shapes=[pltpu.VMEM((2, DN_HEADS * DN_HD, DN_HD), F32)],
        compiler_params=_cp(("arbitrary", "arbitrary")),
        name="dnscan",
    )(uf, wf, qf, kf, af, gc, ub, wb, qb, kb, ab, gc)


def _route(logits, bias_col):
    epg = N_EXPERTS // N_GROUPS
    tm = logits.shape[0]
    scores = jax.nn.sigmoid(logits.T[0:N_EXPERTS])
    gsel = scores + bias_col
    row = lambda a, k: a[k:k + 1]
    best = gidx = None
    for g in range(N_GROUPS):
        a = [row(gsel, epg * g + k) for k in range(epg)]
        m01, n01 = jnp.maximum(a[0], a[1]), jnp.minimum(a[0], a[1])
        m23, n23 = jnp.maximum(a[2], a[3]), jnp.minimum(a[2], a[3])
        gs = jnp.maximum(m01, m23) + jnp.maximum(jnp.minimum(m01, m23), jnp.maximum(n01, n23))
        if g == 0:
            best, gidx = gs, jnp.zeros_like(gs)
        else:
            better = gs > best
            best = jnp.where(better, gs, best)
            gidx = jnp.where(better, float(g), gidx)
    sel = [None] * epg
    raw = [None] * epg
    for g in range(N_GROUPS):
        for k in range(epg):
            v, u = row(gsel, epg * g + k), row(scores, epg * g + k)
            sel[k] = v if g == 0 else jnp.where(gidx == g, v, sel[k])
            raw[k] = u if g == 0 else jnp.where(gidx == g, u, raw[k])
    v1, e1, w1 = sel[0], jnp.zeros_like(gidx), raw[0]
    for k in range(1, epg):
        better = sel[k] > v1
        v1 = jnp.where(better, sel[k], v1)
        e1 = jnp.where(better, float(k), e1)
        w1 = jnp.where(better, raw[k], w1)
    v2 = e2 = w2 = None
    for k in range(epg):
        cand = jnp.where(e1 == k, -jnp.inf, sel[k])
        if k == 0:
            v2, e2, w2 = cand, jnp.zeros_like(gidx), raw[0]
        else:
            better = cand > v2
            v2 = jnp.where(better, cand, v2)
            e2 = jnp.where(better, float(k), e2)
            w2 = jnp.where(better, raw[k], w2)
    tot = w1 + w2
    eidx = jnp.concatenate([gidx * epg + e1, gidx * epg + e2, jnp.zeros((6, tm), F32)], axis=0).astype(I32)
    gates_t = jnp.concatenate([w1 / tot, w2 / tot, jnp.zeros((126, tm), F32)], axis=0)
    return eidx, gates_t.T


def _post_mixer(x, y, m, g2, rw, rb):
    xn = x + m[2:3] * y
    h2 = _rms(xn, g2) * (1.0 + m[4:5]) + m[3:4]
    hi = h2.astype(BF16)
    lo = (h2 - hi.astype(F32)).astype(BF16)
    hw = _dot(hi, rw)
    logits = hw[:, 0:128] + (hw[:, 128:256] + _dot(lo, rw[:, 0:128]))
    eidx, gates = _route(logits, rb)
    return xn, hi, eidx, gates


def _mix0_kernel(xl_ref, xc_ref, sc_ref, prev_ref, next_ref, of_ref, ob_ref, z_ref, mod_ref, cw_ref, on_ref, wo_ref,
                 g2_ref, rw_ref, rb_ref, h2buf_ref, xn_ref, h2_ref, ei_ref, ga_ref, *, ts, bs, s, cl):
    del h2buf_ref
    first, last = _seq_edges(pl.program_id(0), ts, bs, s, cl)
    ya = []
    for c in range(4):
        sl = slice(128 * c, 128 * c + 128)
        sg = slice(512 + 128 * c, 512 + 128 * c + 128)
        sx = slice(1024 + 128 * c, 1024 + 128 * c + 128)
        f32 = lambda ref, rows, cols: ref[rows, cols].astype(F32)
        u = f32(sc_ref, slice(None), sg) * f32(sc_ref, slice(None), sx)
        pr = jnp.where(first, 0.0, f32(prev_ref, slice(HALO - 1, HALO), sg) * f32(prev_ref, slice(HALO - 1, HALO), sx))
        nx = jnp.where(last, 0.0, f32(next_ref, slice(0, 1), sg) * f32(next_ref, slice(0, 1), sx))
        up, un = _shifted(u, pr, nx)
        w = cw_ref[:, sl]
        ya.append((f32(sc_ref, slice(None), sl) * (up * w[0:1] + u * w[1:2] + un * w[2:3])).astype(BF16))
    yb = []
    for h in range(DN_HEADS):
        hs = slice(DN_HD * h, DN_HD * h + DN_HD)
        o = of_ref[:, hs].astype(F32) + ob_ref[:, hs].astype(F32)
        yb.append((_rms(o, on_ref[...]) * _silu(z_ref[:, hs].astype(F32))).astype(BF16))
    ycat = jnp.concatenate(ya + yb, axis=1)
    y = _dot(ycat, wo_ref[...])
    xn, h2, eidx, gates = _post_mixer(_token_rows(xl_ref, xc_ref, ts, bs), y, mod_ref[0], g2_ref[...], rw_ref[...],
                                      rb_ref[...])
    xn_ref[...] = xn
    h2_ref[...] = h2.astype(BF16)
    ei_ref[...] = eidx
    ga_ref[...] = gates


def _mix0(xl, xc, sc, of, ob, z, mods, cw, on, wo, g2, rw, rb, h2buf, ts, bs, s, cl, nb):
    d = xl.shape[1]
    r = bs + xc.shape[0]
    kern = functools.partial(_mix0_kernel, ts=ts, bs=bs, s=s, cl=cl)
    mi = functools.partial(_mod_index, tm=ts, bs=bs, s=s, nb=nb)
    row = lambda w: pl.BlockSpec((ts, w), lambda i: (i, 0))
    full = lambda a: pl.BlockSpec(a.shape, lambda i: (0,) * a.ndim)
    return pl.pallas_call(
        kern,
        grid=(r // ts,),
        in_specs=_token_specs(ts, bs, d) + [row(1536)] + _halo_specs(ts, 1536, r) + [
            row(512), row(512), row(512),
            pl.BlockSpec((1, N_MOD, d), lambda i: (mi(i), 0, 0)),
            full(cw), full(on), full(wo), full(g2), full(rw), full(rb), pl.BlockSpec(memory_space=pl.ANY)],
        out_specs=[row(d), row(d), pl.BlockSpec((8, ts), lambda i: (0, i)), row(128)],
        out_shape=[jax.ShapeDtypeStruct((r, d), F32), jax.ShapeDtypeStruct(h2buf.shape, BF16),
                   jax.ShapeDtypeStruct((8, r), I32), jax.ShapeDtypeStruct((r, 128), F32)],
        input_output_aliases={15: 1},
        compiler_params=_cp(("parallel",)),
        name="mix0",
    )(xl, xc, sc, sc, sc, of, ob, z, mods, cw, on, wo, g2, rw, rb, h2buf)


def _gmm_kernel(te_ref, tf_ref, tv_ref, x_ref, wg_ref, wu_ref, wd_ref, y_ref, wgb, wub, wdb):
    t = pl.program_id(0)

    @pl.when(tf_ref[t] == 1)
    def _():
        wgb[...] = wg_ref[0, 0].astype(BF16)
        wub[...] = wu_ref[0, 0].astype(BF16)
        wdb[...] = wd_ref[0, 0].astype(BF16)

    @pl.when(tv_ref[t] == 1)
    def _():
        x = x_ref[...]
        a = (_silu(_dot(x, wgb[...])) * _dot(x, wub[...])).astype(BF16)
        y_ref[...] = _dot(a, wdb[...]).astype(BF16)

    @pl.when(tv_ref[t] == 0)
    def _():
        y_ref[...] = jnp.zeros_like(y_ref)


def _gmm(xs, w_gate, w_up, w_down, layer, tile_expert, tile_first, tile_valid, tmm):
    p, d = xs.shape
    de = w_gate.shape[-1]
    nt = p // tmm
    grid_spec = pltpu.PrefetchScalarGridSpec(
        num_scalar_prefetch=3,
        grid=(nt,),
        in_specs=[pl.BlockSpec((tmm, d), lambda t, te, tf, tv: (t, 0)),
                  pl.BlockSpec((1, 1, d, de), lambda t, te, tf, tv: (layer, te[t], 0, 0)),
                  pl.BlockSpec((1, 1, d, de), lambda t, te, tf, tv: (layer, te[t], 0, 0)),
                  pl.BlockSpec((1, 1, de, d), lambda t, te, tf, tv: (layer, te[t], 0, 0))],
        out_specs=pl.BlockSpec((tmm, d), lambda t, te, tf, tv: (t, 0)),
        scratch_shapes=[pltpu.VMEM((d, de), BF16), pltpu.VMEM((d, de), BF16), pltpu.VMEM((de, d), BF16)],
    )
    return pl.pallas_call(
        _gmm_kernel,
        grid_spec=grid_spec,
        out_shape=jax.ShapeDtypeStruct((p, d), BF16),
        compiler_params=_cp(("arbitrary",)),
        name="gmm",
    )(tile_expert, tile_first, tile_valid, xs, w_gate, w_up, w_down)


def _moe(h2, t_tok, e_first, e_second, w_gate, w_up, w_down, layer, tmm):
    n = 2 * t_tok
    e_flat = jnp.concatenate([e_first, e_second])
    onehot = (e_flat[:, None] == jnp.arange(N_EXPERTS, dtype=I32)[None, :]).astype(I32)
    csum = jnp.cumsum(onehot, axis=0)
    counts = csum[-1]
    ptiles = (counts + tmm - 1) // tmm
    tile_end = jnp.cumsum(ptiles)
    dest = jnp.sum(onehot * (csum - 1 + ((tile_end - ptiles) * tmm)[None, :]), axis=1)
    nt = n // tmm + N_EXPERTS
    tid = jnp.arange(nt, dtype=I32)
    tile_valid = (tid < tile_end[-1]).astype(I32)
    te = jnp.minimum(jnp.sum((tile_end[None, :] <= tid[:, None]).astype(I32), axis=1), N_EXPERTS - 1)
    last_used = jnp.max(jnp.where(tile_valid == 1, te, 0))
    te = jnp.where(tile_valid == 1, te, last_used)
    tile_first = jnp.concatenate([jnp.ones((1,), I32), (te[1:] != te[:-1]).astype(I32)])
    order = jnp.argsort(e_flat, stable=True).astype(I32)
    seg_start = (tile_end - ptiles) * tmm
    shift = seg_start - (jnp.cumsum(counts) - counts)
    pos = tid[:, None] * tmm + jnp.arange(tmm, dtype=I32)[None, :]
    per_tile = lambda v: v[te][:, None]
    used = (pos - per_tile(seg_start) < per_tile(counts)) & (tile_valid[:, None] == 1)
    src = (jnp.where(used, jnp.take(order, jnp.clip(pos - per_tile(shift), 0, n - 1)), pos) % t_tok).reshape(-1)
    assert h2.shape[0] == nt * tmm
    xs = jnp.take(h2, src, axis=0, mode="clip")
    ys = _gmm(xs, w_gate, w_up, w_down, layer, te, tile_first, tile_valid, tmm)
    return jnp.take(ys, dest, axis=0, mode="clip")


def _moe_combine(x, y1, y2, gates, m5):
    g = gates
    f = g[:, 0:1] * y1.astype(F32) + g[:, 1:2] * y2.astype(F32)
    return x + m5 * f


def _rope(x, cos, sin):
    n = x.shape[1]
    lane = lax.broadcasted_iota(I32, x.shape, 1)
    sw = jnp.where(lane % 32 < 16, pltpu.roll(x, n - 16, 1), pltpu.roll(x, 16, 1))
    reps = n // 128
    if reps > 1:
        cos = jnp.concatenate([cos] * reps, axis=1)
        sin = jnp.concatenate([sin] * reps, axis=1)
    return x * cos + sw * sin


def _inproj1_kernel(x_ref, y1_ref, y2_ref, ga_ref, m0_ref, m1_ref, g_ref, w_ref, cos_ref, sin_ref,
                    x1_ref, cq_ref, ckt_ref, cv_ref, dq_ref, dk_ref, dv_ref, *, tm, bs):
    x1 = _moe_combine(x_ref[...], y1_ref[...], y2_ref[...], ga_ref[...], m0_ref[0][5:6])
    x1_ref[...] = x1
    m = m1_ref[0]
    h = (_rms(x1, g_ref[...]) * (1.0 + m[1:2]) + m[0:1]).astype(BF16)
    in_lat = pl.program_id(0) * tm < bs
    cos, sin = cos_ref[...], sin_ref[...]
    scale = HEAD_DIM ** -0.5 * LOG2E
    cq = _dot(h, w_ref[:, 0:512])
    cq_ref[...] = (jnp.where(in_lat, _rope(cq, cos, sin), cq) * scale).astype(BF16)
    ck = _dot(h, w_ref[:, 512:640])
    ckt_ref[...] = jnp.where(in_lat, _rope(ck, cos, sin), ck).T.astype(BF16)
    cv_ref[...] = _dot(h, w_ref[:, 640:768]).astype(BF16)
    dq_ref[...] = (_dot(h, w_ref[:, 768:1280]) * scale).astype(BF16)
    dk_ref[...] = _dot(h, w_ref[:, 1280:1792]).astype(BF16)
    dv_ref[...] = _dot(h, w_ref[:, 1792:2304]).astype(BF16)


def _inproj1(x, y, gates, mods0, mods1, g, w, cos, sin, tm, bs, s, nb):
    r, d = x.shape
    second = pl.BlockSpec((tm, d), lambda i: (i + r // tm, 0))
    kern = functools.partial(_inproj1_kernel, tm=tm, bs=bs)
    mi = functools.partial(_mod_index, tm=tm, bs=bs, s=s, nb=nb)
    row = lambda wd: pl.BlockSpec((tm, wd), lambda i: (i, 0))
    modspec = pl.BlockSpec((1, N_MOD, d), lambda i: (mi(i), 0, 0))
    tab = pl.BlockSpec((tm, 128), lambda i: (jnp.where(i * tm < bs, (i * tm % s) // tm, 0), 0))
    shp = lambda wd, dt: jax.ShapeDtypeStruct((r, wd), dt)
    return pl.pallas_call(
        kern,
        grid=(r // tm,),
        in_specs=[row(d), row(d), second, row(128), modspec, modspec,
                  pl.BlockSpec((1, d), lambda i: (0, 0)), pl.BlockSpec(w.shape, lambda i: (0, 0)), tab, tab],
        out_specs=[row(d), row(512), pl.BlockSpec((128, tm), lambda i: (0, i)), row(128), row(512), row(512),
                   row(512)],
        out_shape=[shp(d, F32), shp(512, BF16), jax.ShapeDtypeStruct((128, r), BF16), shp(128, BF16),
                   shp(512, BF16), shp(512, BF16), shp(512, BF16)],
        compiler_params=_cp(("parallel",)),
        name="inproj1",
    )(x, y, y, gates, mods0, mods1, g, w, cos, sin)


def _swa_kernel(q_ref, ktp_ref, ktc_ref, ktn_ref, ktx_ref, vp_ref, vc_ref, vn_ref, vx_ref, sink_ref, o_ref,
                *, nblk, cl):
    i = pl.program_id(1)
    wb = SWA_BLOCK
    nloc = 3 * wb
    kt = jnp.concatenate([ktp_ref[...], ktc_ref[...], ktn_ref[...], ktx_ref[...]], axis=1)
    vv = jnp.concatenate([vp_ref[...], vc_ref[...], vn_ref[...], vx_ref[...]], axis=0)
    a_i = lax.broadcasted_iota(I32, (2 * wb, nloc), 0) % wb
    c_i = lax.broadcasted_iota(I32, (2 * wb, nloc), 1)
    lo = jnp.where(i > 0, 0, wb)
    hi = jnp.where(i < nblk - 1, 3 * wb, 2 * wb)
    ok = (c_i >= a_i) & (c_i <= a_i + 2 * SWA_WINDOW) & (c_i >= lo) & (c_i < hi)
    half = lax.broadcasted_iota(I32, (1, 128), 1) // HEAD_DIM
    zero = jnp.zeros((), BF16)
    sink = sink_ref[...]
    def scores(g):
        q2 = q_ref[:, 128 * g:128 * g + 128]
        qst = jnp.concatenate([jnp.where(half == 0, q2, zero), jnp.where(half == 1, q2, zero)], axis=0)
        s_all = _dot(qst, kt)
        return jnp.concatenate([jnp.where(ok, s_all[:, 0:nloc], NEG), s_all[:, nloc:]], axis=1)

    def softmax(g, sc):
        sk = jnp.concatenate([jnp.broadcast_to(sink[0:1, 2 * g + a:2 * g + a + 1], (wb, 1)) for a in range(2)],
                             axis=0)
        m = jnp.maximum(jnp.max(sc, axis=-1, keepdims=True), sk)
        p = jnp.exp2(sc - m)
        return p.astype(BF16), jnp.sum(p, axis=-1, keepdims=True) + jnp.exp2(sk - m)

    def output(g, p, den):
        ost = _dot(p, vv) / den
        o_ref[:, 128 * g:128 * g + 128] = jnp.where(half == 0, ost[0:wb], ost[wb:2 * wb]).astype(BF16)

    sc, pr = {}, {}
    for step in range(6):
        if step < 4:
            sc[step] = scores(step)
        if 1 <= step < 5:
            pr[step - 1] = softmax(step - 1, sc.pop(step - 1))
        if step >= 2:
            output(step - 2, *pr.pop(step - 2))


def _swa(cq, ckt, cv, sink_row, nb, s, cl, bs):
    wb = SWA_BLOCK
    nblk = s // wb
    kern = functools.partial(_swa_kernel, nblk=nblk, cl=cl)
    prev = lambda b, i: b * nblk + jnp.maximum(i - 1, 0)
    own = lambda b, i: b * nblk + i
    nxt = lambda b, i: b * nblk + jnp.minimum(i + 1, nblk - 1)
    ktspec = lambda f: pl.BlockSpec((128, wb), lambda b, i: (0, f(b, i)))
    vspec = lambda f: pl.BlockSpec((wb, 128), lambda b, i: (f(b, i), 0))
    return pl.pallas_call(
        kern,
        grid=(nb, nblk),
        in_specs=[pl.BlockSpec((wb, 512), lambda b, i: (own(b, i), 0)),
                  ktspec(prev), ktspec(own), ktspec(nxt),
                  pl.BlockSpec((128, cl), lambda b, i: (0, bs // cl + b)),
                  vspec(prev), vspec(own), vspec(nxt),
                  pl.BlockSpec((cl, 128), lambda b, i: (bs // cl + b, 0)),
                  pl.BlockSpec((1, 128), lambda b, i: (0, 0))],
        out_specs=pl.BlockSpec((wb, 512), lambda b, i: (own(b, i), 0)),
        out_shape=jax.ShapeDtypeStruct((bs, 512), BF16),
        compiler_params=_cp(("parallel", "parallel")),
        name="swa",
    )(cq, ckt, ckt, ckt, ckt, cv, cv, cv, cv, sink_row)


def _na_kernel(q_ref, k_ref, v_ref, kx_ref, vx_ref, bias_ref, o_ref, *, rows, unroll):
    half = lax.broadcasted_iota(I32, (1, 128), 1) // HEAD_DIM
    zero = jnp.zeros((), BF16)
    kx, vx = kx_ref[...], vx_ref[...]
    span = NA_KH * GRID_W

    def scores(r):
        rs = jnp.clip(r - NA_KH // 2, 0, rows - NA_KH)
        off = rs - r + NA_KH - 1
        q0 = pl.multiple_of(r * GRID_W, GRID_W)
        k0 = pl.multiple_of(rs * GRID_W, GRID_W)
        q2 = q_ref[pl.ds(q0, GRID_W), :]
        qst = jnp.concatenate([jnp.where(half == 0, q2, zero), jnp.where(half == 1, q2, zero)], axis=0)
        s_loc = (_dot_nt(qst, k_ref[pl.ds(k0, span), :])
                 + jnp.concatenate([bias_ref[0, off], bias_ref[1, off]], axis=0))
        return q0, k0, s_loc, _dot_nt(qst, kx)

    def softmax(s_loc, s_ctx):
        m = jnp.maximum(jnp.max(s_loc, axis=-1, keepdims=True), jnp.max(s_ctx, axis=-1, keepdims=True))
        p_loc = jnp.exp2(s_loc - m)
        p_ctx = jnp.exp2(s_ctx - m)
        den = jnp.sum(p_loc, axis=-1, keepdims=True) + jnp.sum(p_ctx, axis=-1, keepdims=True)
        return p_loc.astype(BF16), p_ctx.astype(BF16), den

    def body(i, carry):
        sc = [scores(i * unroll + j) for j in range(unroll)]
        pr = [softmax(s_loc, s_ctx) for (_, _, s_loc, s_ctx) in sc]
        for (q0, k0, _, _), (p_loc, p_ctx, den) in zip(sc, pr):
            ost = (_dot(p_loc, v_ref[pl.ds(k0, span), :]) + _dot(p_ctx, vx)) / den
            o = jnp.where(half == 0, ost[0:GRID_W], ost[GRID_W:2 * GRID_W])
            o_ref[pl.ds(q0, GRID_W), :] = o.astype(BF16)
        return carry

    lax.fori_loop(0, rows // unroll, body, 0)


def _na(dq, dk, dv, bias, nb, s, cl, bs):
    rows = s // GRID_W
    unroll = 4
    assert rows % unroll == 0
    kern = functools.partial(_na_kernel, rows=rows, unroll=unroll)
    seq = pl.BlockSpec((s, 128), lambda b, p: (b, p))
    ctx = pl.BlockSpec((cl, 128), lambda b, p: (bs // cl + b, p))
    return pl.pallas_call(
        kern,
        grid=(nb, NA_HEADS // 2),
        in_specs=[seq, seq, seq, ctx, ctx,
                  pl.BlockSpec((2, NA_KH, GRID_W, NA_KH * GRID_W), lambda b, p: (p, 0, 0, 0))],
        out_specs=seq,
        out_shape=jax.ShapeDtypeStruct((bs, 512), BF16),
        compiler_params=_cp(("parallel", "parallel")),
        name="na",
    )(dq, dk, dv, dk, dv, bias)


def _na_bias_table(rpb):
    c = np.arange(GRID_W)
    qs = np.clip(c - NA_KW // 2, 0, GRID_W - NA_KW)
    kc = np.arange(GRID_W)
    ok = (kc[None, :] >= qs[:, None]) & (kc[None, :] < qs[:, None] + NA_KW)
    dc = np.clip(kc[None, :] - c[:, None] + NA_KW - 1, 0, 2 * NA_KW - 2)
    sel = (np.arange(2 * NA_KW - 1)[:, None, None] == dc[None]).astype(np.float32)
    cols = jnp.einsum("hab,bck->hcak", rpb.astype(F32), sel, precision=HI)
    cols = jnp.where(ok[None, :, None, :], cols * LOG2E, NEG)
    return jnp.stack([cols[:, :, off:off + NA_KH].reshape(NA_HEADS, GRID_W, NA_KH * GRID_W)
                      for off in range(NA_KH)], axis=1)


def _mix1_kernel(x_ref, oc_ref, od_ref, mod_ref, wo_ref, g2_ref, rw_ref, rb_ref, h2buf_ref,
                 xn_ref, h2_ref, ei_ref, ga_ref):
    del h2buf_ref
    y = _dot(oc_ref[...], wo_ref[0:512, :]) + _dot(od_ref[...], wo_ref[512:1024, :])
    xn, h2, eidx, gates = _post_mixer(x_ref[...], y, mod_ref[0], g2_ref[...], rw_ref[...], rb_ref[...])
    xn_ref[...] = xn
    h2_ref[...] = h2.astype(BF16)
    ei_ref[...] = eidx
    ga_ref[...] = gates


def _mix1(x, oc, od, mods, wo, g2, rw, rb, h2buf, tm, bs, s):
    d = x.shape[1]
    row = lambda w: pl.BlockSpec((tm, w), lambda i: (i, 0))
    full = lambda a: pl.BlockSpec(a.shape, lambda i: (0,) * a.ndim)
    return pl.pallas_call(
        _mix1_kernel,
        grid=(bs // tm,),
        in_specs=[row(d), row(512), row(512), pl.BlockSpec((1, N_MOD, d), lambda i: (i * tm // s, 0, 0)),
                  full(wo), full(g2), full(rw), full(rb), pl.BlockSpec(memory_space=pl.ANY)],
        out_specs=[row(d), row(d), pl.BlockSpec((8, tm), lambda i: (0, i)), row(128)],
        out_shape=[jax.ShapeDtypeStruct((bs, d), F32), jax.ShapeDtypeStruct(h2buf.shape, BF16),
                   jax.ShapeDtypeStruct((8, bs), I32), jax.ShapeDtypeStruct((bs, 128), F32)],
        input_output_aliases={8: 1},
        compiler_params=_cp(("parallel",)),
        name="mix1",
    )(x, oc, od, mods, wo, g2, rw, rb, h2buf)


def _final_kernel(x_ref, y1_ref, y2_ref, ga_ref, mod_ref, g_ref, o_ref):
    x = _moe_combine(x_ref[...], y1_ref[...], y2_ref[...], ga_ref[...], mod_ref[0][5:6])
    o_ref[...] = _rms(x, g_ref[...])


def _final(x, y, gates, mods, g, tm, s):
    r, d = x.shape
    second = pl.BlockSpec((tm, d), lambda i: (i + r // tm, 0))
    row = lambda w: pl.BlockSpec((tm, w), lambda i: (i, 0))
    return pl.pallas_call(
        _final_kernel,
        grid=(r // tm,),
        in_specs=[row(d), row(d), second, row(128), pl.BlockSpec((1, N_MOD, d), lambda i: (i * tm // s, 0, 0)),
                  pl.BlockSpec((1, d), lambda i: (0, 0))],
        out_specs=row(d),
        out_shape=jax.ShapeDtypeStruct((r, d), F32),
        compiler_params=_cp(("parallel",)),
        name="final",
    )(x, y, y, gates, mods, g)


def _rope_tables(s):
    nf = HEAD_DIM // 4
    t = np.arange(s)
    inv = ROPE_THETA ** (-np.arange(nf, dtype=np.float64) / nf)
    ar = (t // GRID_W)[:, None] * inv
    ac = (t % GRID_W)[:, None] * inv
    cos = np.concatenate([np.cos(ar), np.cos(ar), np.cos(ac), np.cos(ac)], axis=1)
    sin = np.concatenate([-np.sin(ar), np.sin(ar), -np.sin(ac), np.sin(ac)], axis=1)
    return (jnp.asarray(np.concatenate([cos, cos], axis=1), F32),
            jnp.asarray(np.concatenate([sin, sin], axis=1), F32))


def kernel(x, c, ctx, c_ctx, ada_w, ada_b, norm1_g, norm2_g, ev_w_in, ev_w_out, sc_conv_w, dn_conv_w, dn_a_log, dn_dt_bias, dn_onorm_g, od_w_in, od_w_out, swa_sink, na_rpb, router_w, router_b, moe_w_gate, moe_w_up, moe_w_down, final_g):
    nb, s, d = x.shape
    cl = ctx.shape[1]
    bs = nb * s
    tm = 512
    ts = 256
    tmm = 512
    assert d == 1024 and s % tm == 0 and (nb * cl) % tm == 0 and cl % ts == 0 and s % ts == 0
    assert s // GRID_W >= NA_KH and bs % cl == 0 and nb + 1 <= 8

    xl, xc = x.reshape(bs, d), ctx.reshape(nb * cl, d)
    cc = jnp.zeros((8, d), F32).at[:nb].set(c).at[nb].set(c_ctx)
    mods = _ada(cc, ada_w, ada_b).reshape(ada_w.shape[0], 8, N_MOD, d)
    rw32 = jnp.pad(router_w, ((0, 0), (0, 128 - N_EXPERTS)))
    rw_hi = rw32.astype(BF16)
    rw = jnp.concatenate([rw_hi, (rw32 - rw_hi.astype(F32)).astype(BF16)], axis=1)
    rb = router_b.reshape(N_EXPERTS, 1)
    row = lambda v: v.reshape(1, -1)

    w_in0 = jnp.pad(ev_w_in[0], ((0, 0), (0, 3712 - ev_w_in.shape[-1]))).astype(BF16)
    sc, qkv, z, bg = _inproj0(xl, xc, mods[0], row(norm1_g[0]), w_in0, tm, bs, s, nb)
    pad16 = lambda v: jnp.pad(v.reshape(-1), (8, 128 - 16)).reshape(1, 128)
    uf, ub, wf, wb, qf, qb, kf, kb, af, ab, gc = _dnchunk(qkv, bg, dn_conv_w[0], pad16(dn_a_log[0]),
                                                          pad16(dn_dt_bias[0]), ts, bs, s, cl)
    of, ob = _dnscan(uf, ub, wf, wb, qf, qb, kf, kb, af, ab, gc, nb, s, cl, bs)
    r_all = bs + nb * cl
    moe_rows = lambda t: (2 * t // tmm + N_EXPERTS) * tmm
    x0, h2, ei, ga = _mix0(xl, xc, sc, of, ob, z, mods[0], sc_conv_w[0], row(dn_onorm_g[0]),
                           ev_w_out[0].astype(BF16), row(norm2_g[0]), rw, rb,
                           jnp.zeros((moe_rows(r_all), d), BF16), ts, bs, s, cl, nb)
    y = _moe(h2, r_all, ei[0], ei[1], moe_w_gate, moe_w_up, moe_w_down, 0, tmm)

    perm = np.concatenate([np.arange(HEAD_DIM) + HEAD_DIM * (g + 4 * a) for g in range(4) for a in range(2)])
    w1 = od_w_in[0]
    w_in1 = jnp.concatenate([w1[:, 0:512][:, perm], w1[:, 512:]], axis=1).astype(BF16)
    wo1 = od_w_out[0]
    w_out1 = jnp.concatenate([wo1[0:512][perm], wo1[512:]], axis=0).astype(BF16)
    sink_row = jnp.pad(swa_sink[0][np.array([g + 4 * a for g in range(4) for a in range(2)])] * LOG2E,
                       (0, 128 - SWA_HEADS)).reshape(1, 128)
    cos, sin = _rope_tables(s)
    x1, cq, ckt, cv, dq, dk, dv = _inproj1(x0, y, ga, mods[0], mods[1], row(norm1_g[1]), w_in1, cos, sin,
                                           tm, bs, s, nb)
    oc = _swa(cq, ckt, cv, sink_row, nb, s, cl, bs)
    od = _na(dq, dk, dv, _na_bias_table(na_rpb[0]), nb, s, cl, bs)
    x2, h2, ei, ga = _mix1(x1, oc, od, mods[1], w_out1, row(norm2_g[1]), rw, rb,
                           jnp.zeros((moe_rows(bs), d), BF16), tm, bs, s)
    y = _moe(h2, bs, ei[0], ei[1], moe_w_gate, moe_w_up, moe_w_down, 1, tmm)
    out = _final(x2, y, ga, mods[1], row(final_g), tm, s)
    return out.reshape(nb, s, d)
```

```python
import functools
import math

import numpy as np
import jax
import jax.numpy as jnp
from jax import lax
from jax.experimental import pallas as pl
from jax.experimental.pallas import tpu as pltpu

F32 = jnp.float32
BF16 = jnp.bfloat16
I32 = jnp.int32
HI = lax.Precision.HIGHEST

EPS = 1e-6
N_MOD = 6
GRID_W = 64
HEAD_DIM = 64
DN_HEADS = 4
DN_HD = 128
DN_CHUNK = 64
SWA_HEADS = 8
SWA_KV = 2
SWA_BLOCK = 128
SWA_WINDOW = 128
NA_HEADS = 8
NA_KH = 8
NA_KW = 16
ROPE_THETA = 10000.0
N_EXPERTS = 16
N_GROUPS = 4
NEG = -1e30
LOG2E = 1.4426950408889634
VMEM_LIMIT = 56 * 1024 * 1024


def _cp(sem, vmem=VMEM_LIMIT):
    return pltpu.CompilerParams(dimension_semantics=sem, vmem_limit_bytes=vmem)


def _dot(a, b, precision=None):
    return jnp.dot(a, b, preferred_element_type=F32, precision=precision)


def _dot_nt(a, b, precision=None):
    return lax.dot_general(a, b, (((1,), (1,)), ((), ())), preferred_element_type=F32, precision=precision)


def _dot_tn(a, b, precision=None):
    return lax.dot_general(a, b, (((0,), (0,)), ((), ())), preferred_element_type=F32, precision=precision)


def _silu(x):
    return x * jax.nn.sigmoid(x)


def _rms(x, g):
    return x * lax.rsqrt(jnp.mean(x * x, axis=-1, keepdims=True) + EPS) * g


def _ada_kernel(cc_ref, w_ref, b_ref, o_ref):
    a = _silu(cc_ref[...])
    o_ref[0] = _dot(a, w_ref[0], HI) + b_ref[0]


def _ada(cc, ada_w, ada_b):
    depth, d, n = ada_w.shape
    tn = 1536
    return pl.pallas_call(
        _ada_kernel,
        grid=(depth, n // tn),
        in_specs=[pl.BlockSpec((8, d), lambda l, j: (0, 0)),
                  pl.BlockSpec((1, d, tn), lambda l, j: (l, 0, j)),
                  pl.BlockSpec((1, 1, tn), lambda l, j: (l, 0, j))],
        out_specs=pl.BlockSpec((1, 8, tn), lambda l, j: (l, 0, j)),
        out_shape=jax.ShapeDtypeStruct((depth, 8, n), F32),
        compiler_params=_cp(("parallel", "parallel")),
        name="ada",
    )(cc, ada_w, ada_b.reshape(depth, 1, n))


def _mod_index(i, tm, bs, s, nb):
    row0 = i * tm
    return jnp.where(row0 < bs, row0 // s, nb)


def _token_specs(tm, bs, d):
    nlat = bs // tm
    return [pl.BlockSpec((tm, d), lambda i: (jnp.minimum(i, nlat - 1), 0)),
            pl.BlockSpec((tm, d), lambda i: (jnp.maximum(i - nlat, 0), 0))]


def _token_rows(xl_ref, xc_ref, tm, bs):
    return jnp.where(pl.program_id(0) * tm < bs, xl_ref[...], xc_ref[...])


def _inproj0_kernel(xl_ref, xc_ref, mod_ref, g_ref, w_ref, sc_ref, qkv_ref, z_ref, bg_ref, *, tm, bs):
    m = mod_ref[0]
    h = (_rms(_token_rows(xl_ref, xc_ref, tm, bs), g_ref[...]) * (1.0 + m[1:2]) + m[0:1]).astype(BF16)
    sc_ref[...] = _dot(h, w_ref[:, 0:1536]).astype(BF16)
    qkv_ref[...] = _dot(h, w_ref[:, 1536:3072]).astype(BF16)
    z_ref[...] = _dot(h, w_ref[:, 3072:3584]).astype(BF16)
    bg_ref[...] = _dot(h, w_ref[:, 3584:3712])


def _inproj0(xl, xc, mods, g, w, tm, bs, s, nb):
    d = xl.shape[1]
    r = bs + xc.shape[0]
    mi = functools.partial(_mod_index, tm=tm, bs=bs, s=s, nb=nb)
    return pl.pallas_call(
        functools.partial(_inproj0_kernel, tm=tm, bs=bs),
        grid=(r // tm,),
        in_specs=_token_specs(tm, bs, d) + [
            pl.BlockSpec((1, N_MOD, d), lambda i: (mi(i), 0, 0)),
            pl.BlockSpec((1, d), lambda i: (0, 0)),
            pl.BlockSpec(w.shape, lambda i: (0, 0))],
        out_specs=[pl.BlockSpec((tm, 1536), lambda i: (i, 0)),
                   pl.BlockSpec((tm, 1536), lambda i: (i, 0)),
                   pl.BlockSpec((tm, 512), lambda i: (i, 0)),
                   pl.BlockSpec((tm, 128), lambda i: (i, 0))],
        out_shape=[jax.ShapeDtypeStruct((r, 1536), BF16), jax.ShapeDtypeStruct((r, 1536), BF16),
                   jax.ShapeDtypeStruct((r, 512), BF16), jax.ShapeDtypeStruct((r, 128), F32)],
        compiler_params=_cp(("parallel",)),
        name="inproj0",
    )(xl, xc, mods, g, w)


def _seq_edges(i, ts, bs, s, cl):
    row0 = i * ts
    in_lat = row0 < bs
    r_in = jnp.where(in_lat, row0 % s, (row0 - bs) % cl)
    seqlen = jnp.where(in_lat, s, cl)
    return r_in == 0, r_in + ts == seqlen


def _shifted(x, prev_row, next_row):
    n = x.shape[0]
    rows = lax.broadcasted_iota(I32, x.shape, 0)
    xp = jnp.where(rows == 0, prev_row, pltpu.roll(x, 1, 0))
    xn = jnp.where(rows == n - 1, next_row, pltpu.roll(x, n - 1, 0))
    return xp, xn


HALO = 16


def _halo_specs(ts, width, r):
    nblk = r // HALO
    k = ts // HALO
    return [pl.BlockSpec((HALO, width), lambda i: (jnp.maximum(i * k - 1, 0), 0)),
            pl.BlockSpec((HALO, width), lambda i: (jnp.minimum((i + 1) * k, nblk - 1), 0))]


def _dnprep_tile(x_ref, prev_ref, next_ref, bg_ref, cw_ref, alog_ref, dt_ref,
                 q_ref, k_ref, v_ref, bga_ref, *, ts, bs, s, cl):
    first, last = _seq_edges(pl.program_id(0), ts, bs, s, cl)
    for c in range(12):
        sl = slice(128 * c, 128 * c + 128)
        x = x_ref[:, sl].astype(F32)
        pr = jnp.where(first, 0.0, prev_ref[HALO - 1:HALO, sl].astype(F32))
        nx = jnp.where(last, 0.0, next_ref[0:1, sl].astype(F32))
        xp, xn = _shifted(x, pr, nx)
        w = cw_ref[:, sl]
        y = _silu(xp * w[0:1] + x * w[1:2] + xn * w[2:3])
        hs = slice(128 * (c % 4), 128 * (c % 4) + 128)
        if c < 8:
            y = y * lax.rsqrt(jnp.sum(y * y, axis=-1, keepdims=True) + EPS)
        if c < 4:
            q_ref[:, hs] = y * DN_HD ** -0.5
        elif c < 8:
            k_ref[:, hs] = y
        else:
            v_ref[:, hs] = y
    b = bg_ref[...]
    cols = lax.broadcasted_iota(I32, b.shape, 1)
    beta = jax.nn.sigmoid(b)
    t = b + dt_ref[...]
    softplus = jnp.maximum(t, 0.0) + jnp.log1p(jnp.exp(-jnp.abs(t)))
    g = -jnp.exp(alog_ref[...]) * softplus
    bga_ref[...] = jnp.where(cols < 8, beta, jnp.where(cols < 16, g, 0.0))


def _dnchunk_kernel(x_ref, prev_ref, next_ref, bgraw_ref, cw_ref, alog_ref, dt_ref,
                    uf_ref, ub_ref, wf_ref, wb_ref, qf_ref, qb_ref, kf_ref, kb_ref, af_ref, ab_ref, gc_ref,
                    q_ref, k_ref, v_ref, bg_ref, *, nchunks, bs, s, cl):
    _dnprep_tile(x_ref, prev_ref, next_ref, bgraw_ref, cw_ref, alog_ref, dt_ref, q_ref, k_ref, v_ref, bg_ref,
                 ts=nchunks * DN_CHUNK, bs=bs, s=s, cl=cl)
    outs = ((uf_ref, wf_ref, qf_ref, kf_ref, af_ref), (ub_ref, wb_ref, qb_ref, kb_ref, ab_ref))
    c, nh = DN_CHUNK, DN_HEADS
    head_of_col = lax.broadcasted_iota(I32, (1, nh * c), 1) // c
    zero = jnp.zeros((), BF16)

    def block_diag(x):
        return jnp.concatenate([jnp.where(head_of_col == h, x, zero) for h in range(nh)], axis=0)

    chains = []
    for cc in range(nchunks):
        chains += _dnchunk_setup(slice(cc * c, (cc + 1) * c), q_ref, k_ref, v_ref, bg_ref, gc_ref)
    for ch in chains:
        ch["tm"] = ch["nmat"]
        nb16 = ch["nmat"].astype(BF16)
        ch["npow"] = _dot(nb16, block_diag(nb16))
    for _ in range(4):
        for ch in chains:
            nb16 = ch["npow"].astype(BF16)
            ch["both"] = _dot(jnp.concatenate([nb16, ch["tm"].astype(BF16)], axis=0), block_diag(nb16))
        for ch in chains:
            ch["tm"] = ch["tm"] + ch["npow"] + ch["both"][c:2 * c]
            ch["npow"] = ch["both"][0:c]
    for ch in chains:
        ch["both"] = _dot(ch["tm"].astype(BF16), block_diag(ch["npow"].astype(BF16)))
    for ch in chains:
        ch["tm"] = ch["tm"] + ch["npow"] + ch["both"]
    for ch in chains:
        ch["uw"] = ch["rhs"] + _dot(block_diag(ch["tm"].astype(BF16)), ch["rhs"].astype(BF16))
    for ch in chains:
        u_ref, w_ref, qd_ref, kd_ref, at_ref = outs[ch["d"]]
        rows, uw = ch["rows"], ch["uw"]
        for h in range(nh):
            hs = slice(DN_HD * h, DN_HD * h + DN_HD)
            rs = slice(c * h, c * h + c)
            u_ref[rows, hs] = uw[rs, 0:DN_HD]
            w_ref[rows, hs] = uw[rs, DN_HD:2 * DN_HD].astype(BF16)
            qd_ref[rows, hs] = ch["qd"][rs]
        kd_ref[slice(2 * rows.start, 2 * rows.stop)] = ch["kd"].T.astype(BF16)
        at_ref[rows] = ch["att"].astype(BF16)


def _dnchunk_setup(rows, q_ref, k_ref, v_ref, bg_ref, gc_ref):
    c, nh = DN_CHUNK, DN_HEADS
    n = c * nh
    bg = bg_ref[rows]
    i64 = lax.broadcasted_iota(I32, (c, c), 0)
    j64 = lax.broadcasted_iota(I32, (c, c), 1)
    cols = lax.broadcasted_iota(I32, bg.shape, 1)
    gcf = _dot((i64 >= j64).astype(F32), bg, HI)
    gcb = _dot((i64 <= j64).astype(F32), bg, HI)
    gc = jnp.where(cols >= 12, gcb, gcf)
    gc_ref[rows] = gc
    gct = gc.T
    ii = lax.broadcasted_iota(I32, (c, n), 0)
    jj = lax.broadcasted_iota(I32, (c, n), 1)
    head_of_col = jj // c
    jj = jj % c

    def stack(ref):
        return jnp.concatenate([ref[rows, DN_HD * h:DN_HD * h + DN_HD] for h in range(nh)], axis=0)

    def stacked_cols(arr, r0, r1, col0):
        return jnp.concatenate([jnp.broadcast_to(arr[r0:r1, col0 + h:col0 + h + 1], (c, DN_HD)) for h in range(nh)],
                               axis=0)

    def side_by_side_cols(arr, col0):
        out = arr[:, col0 + nh - 1:col0 + nh]
        for h in range(nh - 2, -1, -1):
            out = jnp.where(head_of_col == h, arr[:, col0 + h:col0 + h + 1], out)
        return out

    def diag_blocks(x):
        out = x[(nh - 1) * c:nh * c]
        for h in range(nh - 2, -1, -1):
            out = jnp.where(head_of_col == h, x[h * c:(h + 1) * c], out)
        return out

    kst, qst, vst = stack(k_ref), stack(q_ref), stack(v_ref)
    kb = kst.astype(BF16)
    kq = _dot_nt(jnp.concatenate([kb, qst.astype(BF16)], axis=0), kb)
    kkt, qkt = diag_blocks(kq[0:n]), diag_blocks(kq[n:2 * n])
    chains = []
    for d in range(2):
        incl = (ii >= jj) if d == 0 else (ii <= jj)
        strict = (ii > jj) if d == 0 else (ii < jj)
        last = c - 1 if d == 0 else 0
        grow = jnp.concatenate([gct[8 + 4 * d + h:9 + 4 * d + h, :] for h in range(nh)], axis=1)
        decay = jnp.exp(jnp.where(incl, side_by_side_cols(gc, 8 + 4 * d) - grow, NEG))
        nmat = jnp.where(strict, -(side_by_side_cols(bg, 4 * d) * kkt * decay), 0.0)
        b1 = stacked_cols(bg, 0, c, 4 * d)
        gcol = stacked_cols(gc, 0, c, 8 + 4 * d)
        glast = stacked_cols(gc, last, last + 1, 8 + 4 * d)
        e1 = jnp.exp(gcol)
        chains.append(dict(
            d=d, rows=rows, nmat=nmat,
            rhs=jnp.concatenate([b1 * vst, (b1 * e1) * kst], axis=1),
            qd=(qst * e1).astype(BF16),
            kd=kst * jnp.exp(glast - gcol),
            att=qkt * decay))
    return chains


def _dnchunk(qkv, bg, cw, alog_row, dt_row, ts, bs, s, cl):
    r = qkv.shape[0]
    nchunks = ts // DN_CHUNK
    c = ts
    row = lambda w: pl.BlockSpec((c, w), lambda i: (i, 0))
    shp = lambda w, dt: jax.ShapeDtypeStruct((r, w), dt)
    return pl.pallas_call(
        functools.partial(_dnchunk_kernel, nchunks=nchunks, bs=bs, s=s, cl=cl),
        grid=(r // c,),
        in_specs=[row(1536)] + _halo_specs(ts, 1536, r) + [
            row(128),
            pl.BlockSpec((3, 1536), lambda i: (0, 0)),
            pl.BlockSpec((1, 128), lambda i: (0, 0)),
            pl.BlockSpec((1, 128), lambda i: (0, 0))],
        scratch_shapes=[pltpu.VMEM((c, 512), F32)] * 3 + [pltpu.VMEM((c, 128), F32)],
        out_specs=[row(512)] * 6 + [pl.BlockSpec((2 * c, 256), lambda i: (i, 0))] * 2 + [row(256), row(256), row(128)],
        out_shape=([shp(512, F32)] * 2 + [shp(512, BF16)] * 4 + [jax.ShapeDtypeStruct((2 * r, 256), BF16)] * 2
                   + [shp(256, BF16)] * 2 + [shp(128, F32)]),
        compiler_params=_cp(("parallel",)),
        name="dnchunk",
    )(qkv, qkv, qkv, bg, cw, alog_row, dt_row)


def _dnscan_kernel(*refs, nsub, nb):
    nchain = 2 * nb
    ins, outs, s_ref = refs[:6 * nchain], refs[6 * nchain:7 * nchain], refs[7 * nchain]

    @pl.when(pl.program_id(0) == 0)
    def _():
        s_ref[...] = jnp.zeros_like(s_ref)

    c, nh = DN_CHUNK, DN_HEADS
    head_of_lane = lax.broadcasted_iota(I32, (1, nh * DN_HD), 1) // DN_HD
    head_of_col = lax.broadcasted_iota(I32, (1, nh * c), 1) // c
    zero = jnp.zeros((), BF16)

    def block_diag(tile, head_ids):
        return jnp.concatenate([jnp.where(head_ids == h, tile, zero) for h in range(nh)], axis=0)

    states = [s_ref[ci] for ci in range(nchain)]
    for sub in range(nsub):
        work = []
        for ci in range(nchain):
            d = ci % 2
            u_ref, w_ref, qd_ref, kd_ref, at_ref, g_ref = ins[6 * ci:6 * ci + 6]
            last = c - 1 if d == 0 else 0
            k = sub if d == 0 else nsub - 1 - sub
            rows = slice(c * k, c * k + c)
            g = g_ref[rows]
            decay = jnp.concatenate(
                [jnp.broadcast_to(jnp.exp(g[last:last + 1, 8 + 4 * d + h:9 + 4 * d + h]), (DN_HD, DN_HD))
                 for h in range(nh)], axis=0)
            ust = jnp.concatenate([u_ref[rows, DN_HD * h:DN_HD * h + DN_HD] for h in range(nh)], axis=0)
            stb = states[ci].astype(BF16)
            vnew = ust - _dot(block_diag(w_ref[rows], head_of_lane), stb)
            work.append((rows, decay, stb, vnew.astype(BF16)))
        for ci in range(nchain):
            u_ref, w_ref, qd_ref, kd_ref, at_ref, g_ref = ins[6 * ci:6 * ci + 6]
            rows, decay, stb, vnb = work[ci]
            o = (_dot(block_diag(qd_ref[rows], head_of_lane), stb)
                 + _dot(block_diag(at_ref[rows], head_of_col), vnb))
            kdt = kd_ref[slice(2 * rows.start, 2 * rows.stop)]
            states[ci] = states[ci] * decay + _dot(block_diag(kdt, head_of_col), vnb)
            for h in range(nh):
                outs[ci][rows, DN_HD * h:DN_HD * h + DN_HD] = o[c * h:c * h + c].astype(BF16)
    for ci in range(nchain):
        s_ref[ci] = states[ci]


def _dnscan(uf, ub, wf, wb, qf, qb, kf, kb, af, ab, gc, nb, s, cl, bs):
    nsub = 2
    c = DN_CHUNK * nsub
    assert cl % c == 0 and s % c == 0 and bs % c == 0
    ncc, ncl = cl // c, s // c
    ns = ncc + ncl

    def src_block(b, d):
        if d == 0:
            return lambda t: jnp.where(t < ncc, bs // c + b * ncc + t, b * ncl + t - ncc)
        return lambda t: jnp.where(t < ncc, bs // c + b * ncc + (ncc - 1 - t), b * ncl + (ncl - 1 - (t - ncc)))

    def dst_block(d):
        if d == 0:
            return lambda t: jnp.where(t < ncc, ncl + t, t - ncc)
        return lambda t: jnp.where(t < ncc, ncl + (ncc - 1 - t), ncl - 1 - (t - ncc))

    in_specs, args, out_specs = [], [], []
    for b in range(nb):
        for d, group in enumerate(((uf, wf, qf, kf, af, gc), (ub, wb, qb, kb, ab, gc))):
            idx = src_block(b, d)
            blk = lambda w, idx=idx: pl.BlockSpec((c, w), lambda t: (idx(t), 0))
            in_specs += [blk(512), blk(512), blk(512), pl.BlockSpec((2 * c, 256), lambda t, idx=idx: (idx(t), 0)),
                         blk(256), blk(128)]
            args += list(group)
            out_specs.append(pl.BlockSpec((c, 512), lambda t, f=dst_block(d): (f(t), 0)))
    outs = pl.pallas_call(
        functools.partial(_dnscan_kernel, nsub=nsub, nb=nb),
        grid=(ns,),
        in_specs=in_specs,
        out_specs=out_specs,
        out_shape=[jax.ShapeDtypeStruct((s + cl, 512), BF16)] * (2 * nb),
        scratch_shapes=[pltpu.VMEM((2 * nb, DN_HEADS * DN_HD, DN_HD), F32)],
        compiler_params=_cp(("arbitrary",)),
        name="dnscan",
    )(*args)
    return [(outs[2 * b], outs[2 * b + 1]) for b in range(nb)]


def _route(logits, bias_col):
    epg = N_EXPERTS // N_GROUPS
    tm = logits.shape[0]
    scores = jax.nn.sigmoid(logits.T[0:N_EXPERTS])
    gsel = scores + bias_col
    row = lambda a, k: a[k:k + 1]
    best = gidx = None
    for g in range(N_GROUPS):
        a = [row(gsel, epg * g + k) for k in range(epg)]
        m01, n01 = jnp.maximum(a[0], a[1]), jnp.minimum(a[0], a[1])
        m23, n23 = jnp.maximum(a[2], a[3]), jnp.minimum(a[2], a[3])
        gs = jnp.maximum(m01, m23) + jnp.maximum(jnp.minimum(m01, m23), jnp.maximum(n01, n23))
        if g == 0:
            best, gidx = gs, jnp.zeros_like(gs)
        else:
            better = gs > best
            best = jnp.where(better, gs, best)
            gidx = jnp.where(better, float(g), gidx)
    sel = [None] * epg
    raw = [None] * epg
    for g in range(N_GROUPS):
        for k in range(epg):
            v, u = row(gsel, epg * g + k), row(scores, epg * g + k)
            sel[k] = v if g == 0 else jnp.where(gidx == g, v, sel[k])
            raw[k] = u if g == 0 else jnp.where(gidx == g, u, raw[k])
    v1, e1, w1 = sel[0], jnp.zeros_like(gidx), raw[0]
    for k in range(1, epg):
        better = sel[k] > v1
        v1 = jnp.where(better, sel[k], v1)
        e1 = jnp.where(better, float(k), e1)
        w1 = jnp.where(better, raw[k], w1)
    v2 = e2 = w2 = None
    for k in range(epg):
        cand = jnp.where(e1 == k, -jnp.inf, sel[k])
        if k == 0:
            v2, e2, w2 = cand, jnp.zeros_like(gidx), raw[0]
        else:
            better = cand > v2
            v2 = jnp.where(better, cand, v2)
            e2 = jnp.where(better, float(k), e2)
            w2 = jnp.where(better, raw[k], w2)
    tot = w1 + w2
    eidx = jnp.concatenate([gidx * epg + e1, gidx * epg + e2, jnp.zeros((6, tm), F32)], axis=0).astype(I32)
    gates_t = jnp.concatenate([w1 / tot, w2 / tot, jnp.zeros((126, tm), F32)], axis=0)
    return eidx, gates_t.T


def _post_mixer(x, y, m, g2, rw, rb):
    xn = x + m[2:3] * y
    h2 = _rms(xn, g2) * (1.0 + m[4:5]) + m[3:4]
    hi = h2.astype(BF16)
    lo = (h2 - hi.astype(F32)).astype(BF16)
    hw = _dot(hi, rw)
    logits = hw[:, 0:128] + (hw[:, 128:256] + _dot(lo, rw[:, 0:128]))
    eidx, gates = _route(logits, rb)
    return xn, hi, eidx, gates


def _mix0_kernel(xl_ref, xc_ref, sc_ref, prev_ref, next_ref, z_ref, mod_ref, cw_ref, on_ref, wo_ref,
                 g2_ref, rw_ref, rb_ref, h2buf_ref, *rest, ts, bs, s, cl, nb):
    del h2buf_ref
    o_refs = rest[:2 * nb]
    xn_ref, h2_ref, ei_ref, ga_ref = rest[2 * nb:]
    first, last = _seq_edges(pl.program_id(0), ts, bs, s, cl)
    row0 = pl.program_id(0) * ts
    batch = jnp.where(row0 < bs, row0 // s, (row0 - bs) // cl)
    ya = []
    for c in range(4):
        sl = slice(128 * c, 128 * c + 128)
        sg = slice(512 + 128 * c, 512 + 128 * c + 128)
        sx = slice(1024 + 128 * c, 1024 + 128 * c + 128)
        f32 = lambda ref, rows, cols: ref[rows, cols].astype(F32)
        u = f32(sc_ref, slice(None), sg) * f32(sc_ref, slice(None), sx)
        pr = jnp.where(first, 0.0, f32(prev_ref, slice(HALO - 1, HALO), sg) * f32(prev_ref, slice(HALO - 1, HALO), sx))
        nx = jnp.where(last, 0.0, f32(next_ref, slice(0, 1), sg) * f32(next_ref, slice(0, 1), sx))
        up, un = _shifted(u, pr, nx)
        w = cw_ref[:, sl]
        ya.append((f32(sc_ref, slice(None), sl) * (up * w[0:1] + u * w[1:2] + un * w[2:3])).astype(BF16))
    yb = []
    for h in range(DN_HEADS):
        hs = slice(DN_HD * h, DN_HD * h + DN_HD)
        o = o_refs[2 * nb - 2][:, hs].astype(F32) + o_refs[2 * nb - 1][:, hs].astype(F32)
        for b in range(nb - 2, -1, -1):
            o = jnp.where(batch == b, o_refs[2 * b][:, hs].astype(F32) + o_refs[2 * b + 1][:, hs].astype(F32), o)
        yb.append((_rms(o, on_ref[...]) * _silu(z_ref[:, hs].astype(F32))).astype(BF16))
    ycat = jnp.concatenate(ya + yb, axis=1)
    y = _dot(ycat, wo_ref[...])
    xn, h2, eidx, gates = _post_mixer(_token_rows(xl_ref, xc_ref, ts, bs), y, mod_ref[0], g2_ref[...], rw_ref[...],
                                      rb_ref[...])
    xn_ref[...] = xn
    h2_ref[...] = h2.astype(BF16)
    ei_ref[...] = eidx
    ga_ref[...] = gates


def _mix0(xl, xc, sc, o_pairs, z, mods, cw, on, wo, g2, rw, rb, h2buf, ts, bs, s, cl, nb):
    d = xl.shape[1]
    r = bs + xc.shape[0]
    kern = functools.partial(_mix0_kernel, ts=ts, bs=bs, s=s, cl=cl, nb=nb)

    def o_spec(b):
        def index(i):
            row0 = i * ts
            in_lat = row0 < bs
            owner = jnp.where(in_lat, row0 // s, (row0 - bs) // cl)
            own = jnp.where(in_lat, (row0 % s) // ts, s // ts + ((row0 - bs) % cl) // ts)
            nxt = jnp.where(in_lat, 0, s // ts)
            prv = jnp.where(in_lat, s // ts - 1, (s + cl) // ts - 1)
            return jnp.where(owner == b, own, jnp.where(owner < b, nxt, prv)), 0
        return pl.BlockSpec((ts, 512), index)

    mi = functools.partial(_mod_index, tm=ts, bs=bs, s=s, nb=nb)
    row = lambda w: pl.BlockSpec((ts, w), lambda i: (i, 0))
    full = lambda a: pl.BlockSpec(a.shape, lambda i: (0,) * a.ndim)
    return pl.pallas_call(
        kern,
        grid=(r // ts,),
        in_specs=_token_specs(ts, bs, d) + [row(1536)] + _halo_specs(ts, 1536, r) + [
            row(512),
            pl.BlockSpec((1, N_MOD, d), lambda i: (mi(i), 0, 0)),
            full(cw), full(on), full(wo), full(g2), full(rw), full(rb), pl.BlockSpec(memory_space=pl.ANY)]
        + [o_spec(b) for b in range(nb) for _ in range(2)],
        out_specs=[row(d), row(d), pl.BlockSpec((8, ts), lambda i: (0, i)), row(128)],
        out_shape=[jax.ShapeDtypeStruct((r, d), F32), jax.ShapeDtypeStruct(h2buf.shape, BF16),
                   jax.ShapeDtypeStruct((8, r), I32), jax.ShapeDtypeStruct((r, 128), F32)],
        input_output_aliases={13: 1},
        compiler_params=_cp(("parallel",)),
        name="mix0",
    )(xl, xc, sc, sc, sc, z, mods, cw, on, wo, g2, rw, rb, h2buf, *[a for pair in o_pairs for a in pair])


def _gmm_kernel(te_ref, tf_ref, tv_ref, x_ref, wg_ref, wu_ref, wd_ref, y_ref, wgb, wub, wdb):
    t = pl.program_id(0)

    @pl.when(tf_ref[t] == 1)
    def _():
        wgb[...] = wg_ref[0, 0].astype(BF16)
        wub[...] = wu_ref[0, 0].astype(BF16)
        wdb[...] = wd_ref[0, 0].astype(BF16)

    @pl.when(tv_ref[t] == 1)
    def _():
        x = x_ref[...]
        a = (_silu(_dot(x, wgb[...])) * _dot(x, wub[...])).astype(BF16)
        y_ref[...] = _dot(a, wdb[...]).astype(BF16)

    @pl.when(tv_ref[t] == 0)
    def _():
        y_ref[...] = jnp.zeros_like(y_ref)


def _gmm(xs, w_gate, w_up, w_down, layer, tile_expert, tile_first, tile_valid, tmm):
    p, d = xs.shape
    de = w_gate.shape[-1]
    nt = p // tmm
    grid_spec = pltpu.PrefetchScalarGridSpec(
        num_scalar_prefetch=3,
        grid=(nt,),
        in_specs=[pl.BlockSpec((tmm, d), lambda t, te, tf, tv: (t, 0)),
                  pl.BlockSpec((1, 1, d, de), lambda t, te, tf, tv: (layer, te[t], 0, 0)),
                  pl.BlockSpec((1, 1, d, de), lambda t, te, tf, tv: (layer, te[t], 0, 0)),
                  pl.BlockSpec((1, 1, de, d), lambda t, te, tf, tv: (layer, te[t], 0, 0))],
        out_specs=pl.BlockSpec((tmm, d), lambda t, te, tf, tv: (t, 0)),
        scratch_shapes=[pltpu.VMEM((d, de), BF16), pltpu.VMEM((d, de), BF16), pltpu.VMEM((de, d), BF16)],
    )
    return pl.pallas_call(
        _gmm_kernel,
        grid_spec=grid_spec,
        out_shape=jax.ShapeDtypeStruct((p, d), BF16),
        compiler_params=_cp(("arbitrary",)),
        name="gmm",
    )(tile_expert, tile_first, tile_valid, xs, w_gate, w_up, w_down)


def _moe(h2, t_tok, e_first, e_second, w_gate, w_up, w_down, layer, tmm):
    n = 2 * t_tok
    e_flat = jnp.concatenate([e_first, e_second])
    onehot = (e_flat[:, None] == jnp.arange(N_EXPERTS, dtype=I32)[None, :]).astype(I32)
    csum = jnp.cumsum(onehot, axis=0)
    counts = csum[-1]
    ptiles = (counts + tmm - 1) // tmm
    tile_end = jnp.cumsum(ptiles)
    dest = jnp.sum(onehot * (csum - 1 + ((tile_end - ptiles) * tmm)[None, :]), axis=1)
    nt = n // tmm + N_EXPERTS
    tid = jnp.arange(nt, dtype=I32)
    tile_valid = (tid < tile_end[-1]).astype(I32)
    te = jnp.minimum(jnp.sum((tile_end[None, :] <= tid[:, None]).astype(I32), axis=1), N_EXPERTS - 1)
    last_used = jnp.max(jnp.where(tile_valid == 1, te, 0))
    te = jnp.where(tile_valid == 1, te, last_used)
    tile_first = jnp.concatenate([jnp.ones((1,), I32), (te[1:] != te[:-1]).astype(I32)])
    order = jnp.argsort(e_flat, stable=True).astype(I32)
    seg_start = (tile_end - ptiles) * tmm
    shift = seg_start - (jnp.cumsum(counts) - counts)
    pos = tid[:, None] * tmm + jnp.arange(tmm, dtype=I32)[None, :]
    te_onehot = te[:, None] == jnp.arange(N_EXPERTS, dtype=I32)[None, :]
    per_tile = lambda v: jnp.sum(jnp.where(te_onehot, v[None, :], 0), axis=1, keepdims=True)
    used = (pos - per_tile(seg_start) < per_tile(counts)) & (tile_valid[:, None] == 1)
    src = (jnp.where(used, jnp.take(order, jnp.clip(pos - per_tile(shift), 0, n - 1)), pos) % t_tok).reshape(-1)
    assert h2.shape[0] == nt * tmm
    xs = jnp.take(h2, src, axis=0, mode="clip")
    ys = _gmm(xs, w_gate, w_up, w_down, layer, te, tile_first, tile_valid, tmm)
    return jnp.take(ys, dest, axis=0, mode="clip")


def _moe_combine(x, y1, y2, gates, m5):
    g = gates
    f = g[:, 0:1] * y1.astype(F32) + g[:, 1:2] * y2.astype(F32)
    return x + m5 * f


def _rope(x, cos, sin):
    n = x.shape[1]
    lane = lax.broadcasted_iota(I32, x.shape, 1)
    sw = jnp.where(lane % 32 < 16, pltpu.roll(x, n - 16, 1), pltpu.roll(x, 16, 1))
    reps = n // 128
    if reps > 1:
        cos = jnp.concatenate([cos] * reps, axis=1)
        sin = jnp.concatenate([sin] * reps, axis=1)
    return x * cos + sw * sin


def _inproj1_kernel(x_ref, y1_ref, y2_ref, ga_ref, m0_ref, m1_ref, g_ref, w_ref, cos_ref, sin_ref,
                    x1_ref, cq_ref, ckt_ref, cv_ref, dq_ref, dk_ref, dv_ref, *, tm, bs):
    x1 = _moe_combine(x_ref[...], y1_ref[...], y2_ref[...], ga_ref[...], m0_ref[0][5:6])
    x1_ref[...] = x1
    m = m1_ref[0]
    h = (_rms(x1, g_ref[...]) * (1.0 + m[1:2]) + m[0:1]).astype(BF16)
    in_lat = pl.program_id(0) * tm < bs
    cos, sin = cos_ref[...], sin_ref[...]
    scale = HEAD_DIM ** -0.5 * LOG2E
    cq = _dot(h, w_ref[:, 0:512])
    cq_ref[...] = (jnp.where(in_lat, _rope(cq, cos, sin), cq) * scale).astype(BF16)
    ck = _dot(h, w_ref[:, 512:640])
    ckt_ref[...] = jnp.where(in_lat, _rope(ck, cos, sin), ck).T.astype(BF16)
    cv_ref[...] = _dot(h, w_ref[:, 640:768]).astype(BF16)
    dq_ref[...] = (_dot(h, w_ref[:, 768:1280]) * scale).astype(BF16)
    dk_ref[...] = _dot(h, w_ref[:, 1280:1792]).astype(BF16)
    dv_ref[...] = _dot(h, w_ref[:, 1792:2304]).astype(BF16)


def _inproj1(x, y, gates, mods0, mods1, g, w, cos, sin, tm, bs, s, nb):
    r, d = x.shape
    second = pl.BlockSpec((tm, d), lambda i: (i + r // tm, 0))
    kern = functools.partial(_inproj1_kernel, tm=tm, bs=bs)
    mi = functools.partial(_mod_index, tm=tm, bs=bs, s=s, nb=nb)
    row = lambda wd: pl.BlockSpec((tm, wd), lambda i: (i, 0))
    modspec = pl.BlockSpec((1, N_MOD, d), lambda i: (mi(i), 0, 0))
    tab = pl.BlockSpec((tm, 128), lambda i: (jnp.where(i * tm < bs, (i * tm % s) // tm, 0), 0))
    shp = lambda wd, dt: jax.ShapeDtypeStruct((r, wd), dt)
    return pl.pallas_call(
        kern,
        grid=(r // tm,),
        in_specs=[row(d), row(d), second, row(128), modspec, modspec,
                  pl.BlockSpec((1, d), lambda i: (0, 0)), pl.BlockSpec(w.shape, lambda i: (0, 0)), tab, tab],
        out_specs=[row(d), row(512), pl.BlockSpec((128, tm), lambda i: (0, i)), row(128), row(512), row(512),
                   row(512)],
        out_shape=[shp(d, F32), shp(512, BF16), jax.ShapeDtypeStruct((128, r), BF16), shp(128, BF16),
                   shp(512, BF16), shp(512, BF16), shp(512, BF16)],
        compiler_params=_cp(("parallel",)),
        name="inproj1",
    )(x, y, y, gates, mods0, mods1, g, w, cos, sin)


def _swa_kernel(q_ref, ktp_ref, ktc_ref, ktn_ref, ktx_ref, vp_ref, vc_ref, vn_ref, vx_ref, sink_ref, o_ref,
                *, nblk, cl):
    i = pl.program_id(1)
    wb = SWA_BLOCK
    nloc = 3 * wb
    kt = jnp.concatenate([ktp_ref[...], ktc_ref[...], ktn_ref[...], ktx_ref[...]], axis=1)
    vv = jnp.concatenate([vp_ref[...], vc_ref[...], vn_ref[...], vx_ref[...]], axis=0)
    a_i = lax.broadcasted_iota(I32, (2 * wb, nloc), 0) % wb
    c_i = lax.broadcasted_iota(I32, (2 * wb, nloc), 1)
    lo = jnp.where(i > 0, 0, wb)
    hi = jnp.where(i < nblk - 1, 3 * wb, 2 * wb)
    ok = (c_i >= a_i) & (c_i <= a_i + 2 * SWA_WINDOW) & (c_i >= lo) & (c_i < hi)
    half = lax.broadcasted_iota(I32, (1, 128), 1) // HEAD_DIM
    zero = jnp.zeros((), BF16)
    sink = sink_ref[...]
    def scores(g):
        q2 = q_ref[:, 128 * g:128 * g + 128]
        qst = jnp.concatenate([jnp.where(half == 0, q2, zero), jnp.where(half == 1, q2, zero)], axis=0)
        s_all = _dot(qst, kt)
        return jnp.concatenate([jnp.where(ok, s_all[:, 0:nloc], NEG), s_all[:, nloc:]], axis=1)

    def softmax(g, sc):
        sk = jnp.concatenate([jnp.broadcast_to(sink[0:1, 2 * g + a:2 * g + a + 1], (wb, 1)) for a in range(2)],
                             axis=0)
        m = jnp.maximum(jnp.max(sc, axis=-1, keepdims=True), sk)
        p = jnp.exp2(sc - m)
        return p.astype(BF16), jnp.sum(p, axis=-1, keepdims=True) + jnp.exp2(sk - m)

    def output(g, p, den):
        ost = _dot(p, vv) / den
        o_ref[:, 128 * g:128 * g + 128] = jnp.where(half == 0, ost[0:wb], ost[wb:2 * wb]).astype(BF16)

    sc, pr = {}, {}
    for step in range(6):
        if step < 4:
            sc[step] = scores(step)
        if 1 <= step < 5:
            pr[step - 1] = softmax(step - 1, sc.pop(step - 1))
        if step >= 2:
            output(step - 2, *pr.pop(step - 2))


def _swa(cq, ckt, cv, sink_row, nb, s, cl, bs):
    wb = SWA_BLOCK
    nblk = s // wb
    kern = functools.partial(_swa_kernel, nblk=nblk, cl=cl)
    prev = lambda b, i: b * nblk + jnp.maximum(i - 1, 0)
    own = lambda b, i: b * nblk + i
    nxt = lambda b, i: b * nblk + jnp.minimum(i + 1, nblk - 1)
    ktspec = lambda f: pl.BlockSpec((128, wb), lambda b, i: (0, f(b, i)))
    vspec = lambda f: pl.BlockSpec((wb, 128), lambda b, i: (f(b, i), 0))
    return pl.pallas_call(
        kern,
        grid=(nb, nblk),
        in_specs=[pl.BlockSpec((wb, 512), lambda b, i: (own(b, i), 0)),
                  ktspec(prev), ktspec(own), ktspec(nxt),
                  pl.BlockSpec((128, cl), lambda b, i: (0, bs // cl + b)),
                  vspec(prev), vspec(own), vspec(nxt),
                  pl.BlockSpec((cl, 128), lambda b, i: (bs // cl + b, 0)),
                  pl.BlockSpec((1, 128), lambda b, i: (0, 0))],
        out_specs=pl.BlockSpec((wb, 512), lambda b, i: (own(b, i), 0)),
        out_shape=jax.ShapeDtypeStruct((bs, 512), BF16),
        compiler_params=_cp(("parallel", "parallel")),
        name="swa",
    )(cq, ckt, ckt, ckt, ckt, cv, cv, cv, cv, sink_row)


def _na_kernel(q_ref, k_ref, v_ref, kx_ref, vx_ref, bias_ref, o_ref, *, rows, unroll):
    half = lax.broadcasted_iota(I32, (1, 128), 1) // HEAD_DIM
    zero = jnp.zeros((), BF16)
    kx, vx = kx_ref[...], vx_ref[...]
    span = NA_KH * GRID_W

    def scores(r):
        rs = jnp.clip(r - NA_KH // 2, 0, rows - NA_KH)
        off = rs - r + NA_KH - 1
        q0 = pl.multiple_of(r * GRID_W, GRID_W)
        k0 = pl.multiple_of(rs * GRID_W, GRID_W)
        q2 = q_ref[pl.ds(q0, GRID_W), :]
        qst = jnp.concatenate([jnp.where(half == 0, q2, zero), jnp.where(half == 1, q2, zero)], axis=0)
        s_loc = (_dot_nt(qst, k_ref[pl.ds(k0, span), :])
                 + jnp.concatenate([bias_ref[0, off], bias_ref[1, off]], axis=0))
        return q0, k0, s_loc, _dot_nt(qst, kx)

    def softmax(s_loc, s_ctx):
        m = jnp.maximum(jnp.max(s_loc, axis=-1, keepdims=True), jnp.max(s_ctx, axis=-1, keepdims=True))
        p_loc = jnp.exp2(s_loc - m)
        p_ctx = jnp.exp2(s_ctx - m)
        den = jnp.sum(p_loc, axis=-1, keepdims=True) + jnp.sum(p_ctx, axis=-1, keepdims=True)
        return p_loc.astype(BF16), p_ctx.astype(BF16), den

    def body(i, carry):
        sc = [scores(i * unroll + j) for j in range(unroll)]
        pr = [softmax(s_loc, s_ctx) for (_, _, s_loc, s_ctx) in sc]
        for (q0, k0, _, _), (p_loc, p_ctx, den) in zip(sc, pr):
            ost = (_dot(p_loc, v_ref[pl.ds(k0, span), :]) + _dot(p_ctx, vx)) / den
            o = jnp.where(half == 0, ost[0:GRID_W], ost[GRID_W:2 * GRID_W])
            o_ref[pl.ds(q0, GRID_W), :] = o.astype(BF16)
        return carry

    lax.fori_loop(0, rows // unroll, body, 0)


def _na(dq, dk, dv, bias, nb, s, cl, bs):
    rows = s // GRID_W
    unroll = 4
    assert rows % unroll == 0
    kern = functools.partial(_na_kernel, rows=rows, unroll=unroll)
    seq = pl.BlockSpec((s, 128), lambda b, p: (b, p))
    ctx = pl.BlockSpec((cl, 128), lambda b, p: (bs // cl + b, p))
    return pl.pallas_call(
        kern,
        grid=(nb, NA_HEADS // 2),
        in_specs=[seq, seq, seq, ctx, ctx,
                  pl.BlockSpec((2, NA_KH, GRID_W, NA_KH * GRID_W), lambda b, p: (p, 0, 0, 0))],
        out_specs=seq,
        out_shape=jax.ShapeDtypeStruct((bs, 512), BF16),
        compiler_params=_cp(("parallel", "parallel")),
        name="na",
    )(dq, dk, dv, dk, dv, bias)


def _na_bias_table(rpb):
    c = np.arange(GRID_W)
    qs = np.clip(c - NA_KW // 2, 0, GRID_W - NA_KW)
    kc = np.arange(GRID_W)
    ok = (kc[None, :] >= qs[:, None]) & (kc[None, :] < qs[:, None] + NA_KW)
    dc = np.clip(kc[None, :] - c[:, None] + NA_KW - 1, 0, 2 * NA_KW - 2)
    sel = (np.arange(2 * NA_KW - 1)[:, None, None] == dc[None]).astype(np.float32)
    cols = jnp.einsum("hab,bck->hcak", rpb.astype(F32), sel, precision=HI)
    cols = jnp.where(ok[None, :, None, :], cols * LOG2E, NEG)
    return jnp.stack([cols[:, :, off:off + NA_KH].reshape(NA_HEADS, GRID_W, NA_KH * GRID_W)
                      for off in range(NA_KH)], axis=1)


def _mix1_kernel(x_ref, oc_ref, od_ref, mod_ref, wo_ref, g2_ref, rw_ref, rb_ref, h2buf_ref,
                 xn_ref, h2_ref, ei_ref, ga_ref):
    del h2buf_ref
    y = _dot(oc_ref[...], wo_ref[0:512, :]) + _dot(od_ref[...], wo_ref[512:1024, :])
    xn, h2, eidx, gates = _post_mixer(x_ref[...], y, mod_ref[0], g2_ref[...], rw_ref[...], rb_ref[...])
    xn_ref[...] = xn
    h2_ref[...] = h2.astype(BF16)
    ei_ref[...] = eidx
    ga_ref[...] = gates


def _mix1(x, oc, od, mods, wo, g2, rw, rb, h2buf, tm, bs, s):
    d = x.shape[1]
    row = lambda w: pl.BlockSpec((tm, w), lambda i: (i, 0))
    full = lambda a: pl.BlockSpec(a.shape, lambda i: (0,) * a.ndim)
    return pl.pallas_call(
        _mix1_kernel,
        grid=(bs // tm,),
        in_specs=[row(d), row(512), row(512), pl.BlockSpec((1, N_MOD, d), lambda i: (i * tm // s, 0, 0)),
                  full(wo), full(g2), full(rw), full(rb), pl.BlockSpec(memory_space=pl.ANY)],
        out_specs=[row(d), row(d), pl.BlockSpec((8, tm), lambda i: (0, i)), row(128)],
        out_shape=[jax.ShapeDtypeStruct((bs, d), F32), jax.ShapeDtypeStruct(h2buf.shape, BF16),
                   jax.ShapeDtypeStruct((8, bs), I32), jax.ShapeDtypeStruct((bs, 128), F32)],
        input_output_aliases={8: 1},
        compiler_params=_cp(("parallel",)),
        name="mix1",
    )(x, oc, od, mods, wo, g2, rw, rb, h2buf)


def _final_kernel(x_ref, y1_ref, y2_ref, ga_ref, mod_ref, g_ref, o_ref):
    x = _moe_combine(x_ref[...], y1_ref[...], y2_ref[...], ga_ref[...], mod_ref[0][5:6])
    o_ref[...] = _rms(x, g_ref[...])


def _final(x, y, gates, mods, g, tm, s):
    r, d = x.shape
    second = pl.BlockSpec((tm, d), lambda i: (i + r // tm, 0))
    row = lambda w: pl.BlockSpec((tm, w), lambda i: (i, 0))
    return pl.pallas_call(
        _final_kernel,
        grid=(r // tm,),
        in_specs=[row(d), row(d), second, row(128), pl.BlockSpec((1, N_MOD, d), lambda i: (i * tm // s, 0, 0)),
                  pl.BlockSpec((1, d), lambda i: (0, 0))],
        out_specs=row(d),
        out_shape=jax.ShapeDtypeStruct((r, d), F32),
        compiler_params=_cp(("parallel",)),
        name="final",
    )(x, y, y, gates, mods, g)


def _rope_tables(s):
    nf = HEAD_DIM // 4
    t = np.arange(s)
    inv = ROPE_THETA ** (-np.arange(nf, dtype=np.float64) / nf)
    ar = (t // GRID_W)[:, None] * inv
    ac = (t % GRID_W)[:, None] * inv
    cos = np.concatenate([np.cos(ar), np.cos(ar), np.cos(ac), np.cos(ac)], axis=1)
    sin = np.concatenate([-np.sin(ar), np.sin(ar), -np.sin(ac), np.sin(ac)], axis=1)
    return (jnp.asarray(np.concatenate([cos, cos], axis=1), F32),
            jnp.asarray(np.concatenate([sin, sin], axis=1), F32))


def kernel(x, c, ctx, c_ctx, ada_w, ada_b, norm1_g, norm2_g, ev_w_in, ev_w_out, sc_conv_w, dn_conv_w, dn_a_log, dn_dt_bias, dn_onorm_g, od_w_in, od_w_out, swa_sink, na_rpb, router_w, router_b, moe_w_gate, moe_w_up, moe_w_down, final_g):
    nb, s, d = x.shape
    cl = ctx.shape[1]
    bs = nb * s
    tm = 512
    ts = 256
    tmm = 512
    assert d == 1024 and s % tm == 0 and (nb * cl) % tm == 0 and cl % ts == 0 and s % ts == 0
    assert s // GRID_W >= NA_KH and bs % cl == 0 and nb + 1 <= 8

    xl, xc = x.reshape(bs, d), ctx.reshape(nb * cl, d)
    cc = jnp.zeros((8, d), F32).at[:nb].set(c).at[nb].set(c_ctx)
    mods = _ada(cc, ada_w, ada_b).reshape(ada_w.shape[0], 8, N_MOD, d)
    rw32 = jnp.pad(router_w, ((0, 0), (0, 128 - N_EXPERTS)))
    rw_hi = rw32.astype(BF16)
    rw = jnp.concatenate([rw_hi, (rw32 - rw_hi.astype(F32)).astype(BF16)], axis=1)
    rb = router_b.reshape(N_EXPERTS, 1)
    row = lambda v: v.reshape(1, -1)

    w_in0 = jnp.pad(ev_w_in[0], ((0, 0), (0, 3712 - ev_w_in.shape[-1]))).astype(BF16)
    sc, qkv, z, bg = _inproj0(xl, xc, mods[0], row(norm1_g[0]), w_in0, tm, bs, s, nb)
    pad16 = lambda v: jnp.pad(v.reshape(-1), (8, 128 - 16)).reshape(1, 128)
    uf, ub, wf, wb, qf, qb, kf, kb, af, ab, gc = _dnchunk(qkv, bg, dn_conv_w[0], pad16(dn_a_log[0]),
                                                          pad16(dn_dt_bias[0]), ts, bs, s, cl)
    o_pairs = _dnscan(uf, ub, wf, wb, qf, qb, kf, kb, af, ab, gc, nb, s, cl, bs)
    r_all = bs + nb * cl
    moe_rows = lambda t: (2 * t // tmm + N_EXPERTS) * tmm
    x0, h2, ei, ga = _mix0(xl, xc, sc, o_pairs, z, mods[0], sc_conv_w[0], row(dn_onorm_g[0]),
                           ev_w_out[0].astype(BF16), row(norm2_g[0]), rw, rb,
                           jnp.zeros((moe_rows(r_all), d), BF16), ts, bs, s, cl, nb)
    y = _moe(h2, r_all, ei[0], ei[1], moe_w_gate, moe_w_up, moe_w_down, 0, tmm)

    perm = np.concatenate([np.arange(HEAD_DIM) + HEAD_DIM * (g + 4 * a) for g in range(4) for a in range(2)])
    w1 = od_w_in[0]
    w_in1 = jnp.concatenate([w1[:, 0:512][:, perm], w1[:, 512:]], axis=1).astype(BF16)
    wo1 = od_w_out[0]
    w_out1 = jnp.concatenate([wo1[0:512][perm], wo1[512:]], axis=0).astype(BF16)
    sink_row = jnp.pad(swa_sink[0][np.array([g + 4 * a for g in range(4) for a in range(2)])] * LOG2E,
                       (0, 128 - SWA_HEADS)).reshape(1, 128)
    cos, sin = _rope_tables(s)
    x1, cq, ckt, cv, dq, dk, dv = _inproj1(x0, y, ga, mods[0], mods[1], row(norm1_g[1]), w_in1, cos, sin,
                                           tm, bs, s, nb)
    oc = _swa(cq, ckt, cv, sink_row, nb, s, cl, bs)
    od = _na(dq, dk, dv, _na_bias_table(na_rpb[0]), nb, s, cl, bs)
    x2, h2, ei, ga = _mix1(x1, oc, od, mods[1], w_out1, row(norm2_g[1]), rw, rb,
                           jnp.zeros((moe_rows(bs), d), BF16), tm, bs, s)
    y = _moe(h2, bs, ei[0], ei[1], moe_w_gate, moe_w_up, moe_w_down, 1, tmm)
    out = _final(x2, y, ga, mods[1], row(final_g), tm, s)
    return out.reshape(nb, s, d)
```

```python
import functools
import math

import numpy as np
import jax
import jax.numpy as jnp
from jax import lax
from jax.experimental import pallas as pl
from jax.experimental.pallas import tpu as pltpu

F32 = jnp.float32
BF16 = jnp.bfloat16
I32 = jnp.int32
HI = lax.Precision.HIGHEST

EPS = 1e-6
N_MOD = 6
GRID_W = 64
HEAD_DIM = 64
DN_HEADS = 4
DN_HD = 128
DN_CHUNK = 64
SWA_HEADS = 8
SWA_KV = 2
SWA_BLOCK = 128
SWA_WINDOW = 128
NA_HEADS = 8
NA_KH = 8
NA_KW = 16
ROPE_THETA = 10000.0
N_EXPERTS = 16
N_GROUPS = 4
NEG = -1e30
LOG2E = 1.4426950408889634
VMEM_LIMIT = 56 * 1024 * 1024


def _cp(sem, vmem=VMEM_LIMIT):
    return pltpu.CompilerParams(dimension_semantics=sem, vmem_limit_bytes=vmem)


def _dot(a, b, precision=None):
    return jnp.dot(a, b, preferred_element_type=F32, precision=precision)


def _dot_nt(a, b, precision=None):
    return lax.dot_general(a, b, (((1,), (1,)), ((), ())), preferred_element_type=F32, precision=precision)


def _dot_tn(a, b, precision=None):
    return lax.dot_general(a, b, (((0,), (0,)), ((), ())), preferred_element_type=F32, precision=precision)


def _silu(x):
    return x * jax.nn.sigmoid(x)


def _rms(x, g):
    return x * lax.rsqrt(jnp.mean(x * x, axis=-1, keepdims=True) + EPS) * g


def _ada_kernel(cc_ref, w_ref, b_ref, o_ref):
    a = _silu(cc_ref[...])
    o_ref[0] = _dot(a, w_ref[0], HI) + b_ref[0]


def _ada(cc, ada_w, ada_b):
    depth, d, n = ada_w.shape
    tn = 1536
    return pl.pallas_call(
        _ada_kernel,
        grid=(depth, n // tn),
        in_specs=[pl.BlockSpec((8, d), lambda l, j: (0, 0)),
                  pl.BlockSpec((1, d, tn), lambda l, j: (l, 0, j)),
                  pl.BlockSpec((1, 1, tn), lambda l, j: (l, 0, j))],
        out_specs=pl.BlockSpec((1, 8, tn), lambda l, j: (l, 0, j)),
        out_shape=jax.ShapeDtypeStruct((depth, 8, n), F32),
        compiler_params=_cp(("parallel", "parallel")),
        name="ada",
    )(cc, ada_w, ada_b.reshape(depth, 1, n))


def _mod_index(i, tm, bs, s, nb):
    row0 = i * tm
    return jnp.where(row0 < bs, row0 // s, nb)


def _token_specs(tm, bs, d):
    nlat = bs // tm
    return [pl.BlockSpec((tm, d), lambda i: (jnp.minimum(i, nlat - 1), 0)),
            pl.BlockSpec((tm, d), lambda i: (jnp.maximum(i - nlat, 0), 0))]


def _token_rows(xl_ref, xc_ref, tm, bs):
    return jnp.where(pl.program_id(0) * tm < bs, xl_ref[...], xc_ref[...])


def _inproj0_kernel(xl_ref, xc_ref, mod_ref, g_ref, w_ref, sc_ref, qkv_ref, z_ref, bg_ref, *, tm, bs):
    m = mod_ref[0]
    h = (_rms(_token_rows(xl_ref, xc_ref, tm, bs), g_ref[...]) * (1.0 + m[1:2]) + m[0:1]).astype(BF16)
    sc_ref[...] = _dot(h, w_ref[:, 0:1536]).astype(BF16)
    qkv_ref[...] = _dot(h, w_ref[:, 1536:3072]).astype(BF16)
    z_ref[...] = _dot(h, w_ref[:, 3072:3584]).astype(BF16)
    bg_ref[...] = _dot(h, w_ref[:, 3584:3712])


def _inproj0(xl, xc, mods, g, w, tm, bs, s, nb):
    d = xl.shape[1]
    r = bs + xc.shape[0]
    mi = functools.partial(_mod_index, tm=tm, bs=bs, s=s, nb=nb)
    return pl.pallas_call(
        functools.partial(_inproj0_kernel, tm=tm, bs=bs),
        grid=(r // tm,),
        in_specs=_token_specs(tm, bs, d) + [
            pl.BlockSpec((1, N_MOD, d), lambda i: (mi(i), 0, 0)),
            pl.BlockSpec((1, d), lambda i: (0, 0)),
            pl.BlockSpec(w.shape, lambda i: (0, 0))],
        out_specs=[pl.BlockSpec((tm, 1536), lambda i: (i, 0)),
                   pl.BlockSpec((tm, 1536), lambda i: (i, 0)),
                   pl.BlockSpec((tm, 512), lambda i: (i, 0)),
                   pl.BlockSpec((tm, 128), lambda i: (i, 0))],
        out_shape=[jax.ShapeDtypeStruct((r, 1536), BF16), jax.ShapeDtypeStruct((r, 1536), BF16),
                   jax.ShapeDtypeStruct((r, 512), BF16), jax.ShapeDtypeStruct((r, 128), F32)],
        compiler_params=_cp(("parallel",)),
        name="inproj0",
    )(xl, xc, mods, g, w)


def _seq_edges(i, ts, bs, s, cl):
    row0 = i * ts
    in_lat = row0 < bs
    r_in = jnp.where(in_lat, row0 % s, (row0 - bs) % cl)
    seqlen = jnp.where(in_lat, s, cl)
    return r_in == 0, r_in + ts == seqlen


def _shifted(x, prev_row, next_row):
    n = x.shape[0]
    rows = lax.broadcasted_iota(I32, x.shape, 0)
    xp = jnp.where(rows == 0, prev_row, pltpu.roll(x, 1, 0))
    xn = jnp.where(rows == n - 1, next_row, pltpu.roll(x, n - 1, 0))
    return xp, xn


HALO = 16


def _halo_specs(ts, width, r):
    nblk = r // HALO
    k = ts // HALO
    return [pl.BlockSpec((HALO, width), lambda i: (jnp.maximum(i * k - 1, 0), 0)),
            pl.BlockSpec((HALO, width), lambda i: (jnp.minimum((i + 1) * k, nblk - 1), 0))]


def _dnprep_tile(x_ref, prev_ref, next_ref, bg_ref, cw_ref, alog_ref, dt_ref,
                 q_ref, k_ref, v_ref, bga_ref, *, ts, bs, s, cl):
    first, last = _seq_edges(pl.program_id(0), ts, bs, s, cl)
    for c in range(12):
        sl = slice(128 * c, 128 * c + 128)
        x = x_ref[:, sl].astype(F32)
        pr = jnp.where(first, 0.0, prev_ref[HALO - 1:HALO, sl].astype(F32))
        nx = jnp.where(last, 0.0, next_ref[0:1, sl].astype(F32))
        xp, xn = _shifted(x, pr, nx)
        w = cw_ref[:, sl]
        y = _silu(xp * w[0:1] + x * w[1:2] + xn * w[2:3])
        hs = slice(128 * (c % 4), 128 * (c % 4) + 128)
        if c < 8:
            y = y * lax.rsqrt(jnp.sum(y * y, axis=-1, keepdims=True) + EPS)
        if c < 4:
            q_ref[:, hs] = y * DN_HD ** -0.5
        elif c < 8:
            k_ref[:, hs] = y
        else:
            v_ref[:, hs] = y
    b = bg_ref[...]
    cols = lax.broadcasted_iota(I32, b.shape, 1)
    beta = jax.nn.sigmoid(b)
    t = b + dt_ref[...]
    softplus = jnp.maximum(t, 0.0) + jnp.log1p(jnp.exp(-jnp.abs(t)))
    g = -jnp.exp(alog_ref[...]) * softplus
    bga_ref[...] = jnp.where(cols < 8, beta, jnp.where(cols < 16, g, 0.0))


def _dnchunk_kernel(x_ref, prev_ref, next_ref, bgraw_ref, cw_ref, alog_ref, dt_ref,
                    uf_ref, ub_ref, wf_ref, wb_ref, qf_ref, qb_ref, kf_ref, kb_ref, af_ref, ab_ref, gc_ref,
                    q_ref, k_ref, v_ref, bg_ref, *, nchunks, bs, s, cl):
    _dnprep_tile(x_ref, prev_ref, next_ref, bgraw_ref, cw_ref, alog_ref, dt_ref, q_ref, k_ref, v_ref, bg_ref,
                 ts=nchunks * DN_CHUNK, bs=bs, s=s, cl=cl)
    outs = ((uf_ref, wf_ref, qf_ref, kf_ref, af_ref), (ub_ref, wb_ref, qb_ref, kb_ref, ab_ref))
    c, nh = DN_CHUNK, DN_HEADS
    head_of_col = lax.broadcasted_iota(I32, (1, nh * c), 1) // c
    zero = jnp.zeros((), BF16)

    def block_diag(x):
        return jnp.concatenate([jnp.where(head_of_col == h, x, zero) for h in range(nh)], axis=0)

    chains = []
    for cc in range(nchunks):
        chains += _dnchunk_setup(slice(cc * c, (cc + 1) * c), q_ref, k_ref, v_ref, bg_ref, gc_ref)
    for ch in chains:
        ch["tm"] = ch["nmat"]
        nb16 = ch["nmat"].astype(BF16)
        ch["npow"] = _dot(nb16, block_diag(nb16))
    for _ in range(4):
        for ch in chains:
            nb16 = ch["npow"].astype(BF16)
            ch["both"] = _dot(jnp.concatenate([nb16, ch["tm"].astype(BF16)], axis=0), block_diag(nb16))
        for ch in chains:
            ch["tm"] = ch["tm"] + ch["npow"] + ch["both"][c:2 * c]
            ch["npow"] = ch["both"][0:c]
    for ch in chains:
        ch["both"] = _dot(ch["tm"].astype(BF16), block_diag(ch["npow"].astype(BF16)))
    for ch in chains:
        ch["tm"] = ch["tm"] + ch["npow"] + ch["both"]
    for ch in chains:
        ch["uw"] = ch["rhs"] + _dot(block_diag(ch["tm"].astype(BF16)), ch["rhs"].astype(BF16))
    for ch in chains:
        u_ref, w_ref, qd_ref, kd_ref, at_ref = outs[ch["d"]]
        rows, uw = ch["rows"], ch["uw"]
        for h in range(nh):
            hs = slice(DN_HD * h, DN_HD * h + DN_HD)
            rs = slice(c * h, c * h + c)
            u_ref[rows, hs] = uw[rs, 0:DN_HD]
            w_ref[rows, hs] = uw[rs, DN_HD:2 * DN_HD].astype(BF16)
            qd_ref[rows, hs] = ch["qd"][rs]
        kd_ref[slice(2 * rows.start, 2 * rows.stop)] = ch["kd"].T.astype(BF16)
        at_ref[rows] = ch["att"].astype(BF16)


def _dnchunk_setup(rows, q_ref, k_ref, v_ref, bg_ref, gc_ref):
    c, nh = DN_CHUNK, DN_HEADS
    n = c * nh
    bg = bg_ref[rows]
    i64 = lax.broadcasted_iota(I32, (c, c), 0)
    j64 = lax.broadcasted_iota(I32, (c, c), 1)
    cols = lax.broadcasted_iota(I32, bg.shape, 1)
    gcf = _dot((i64 >= j64).astype(F32), bg, HI)
    gcb = _dot((i64 <= j64).astype(F32), bg, HI)
    gc = jnp.where(cols >= 12, gcb, gcf)
    gc_ref[rows] = gc
    gct = gc.T
    ii = lax.broadcasted_iota(I32, (c, n), 0)
    jj = lax.broadcasted_iota(I32, (c, n), 1)
    head_of_col = jj // c
    jj = jj % c

    def stack(ref):
        return jnp.concatenate([ref[rows, DN_HD * h:DN_HD * h + DN_HD] for h in range(nh)], axis=0)

    def stacked_cols(arr, r0, r1, col0):
        return jnp.concatenate([jnp.broadcast_to(arr[r0:r1, col0 + h:col0 + h + 1], (c, DN_HD)) for h in range(nh)],
                               axis=0)

    def side_by_side_cols(arr, col0):
        out = arr[:, col0 + nh - 1:col0 + nh]
        for h in range(nh - 2, -1, -1):
            out = jnp.where(head_of_col == h, arr[:, col0 + h:col0 + h + 1], out)
        return out

    def diag_blocks(x):
        out = x[(nh - 1) * c:nh * c]
        for h in range(nh - 2, -1, -1):
            out = jnp.where(head_of_col == h, x[h * c:(h + 1) * c], out)
        return out

    kst, qst, vst = stack(k_ref), stack(q_ref), stack(v_ref)
    kb = kst.astype(BF16)
    kq = _dot_nt(jnp.concatenate([kb, qst.astype(BF16)], axis=0), kb)
    kkt, qkt = diag_blocks(kq[0:n]), diag_blocks(kq[n:2 * n])
    chains = []
    for d in range(2):
        incl = (ii >= jj) if d == 0 else (ii <= jj)
        strict = (ii > jj) if d == 0 else (ii < jj)
        last = c - 1 if d == 0 else 0
        grow = jnp.concatenate([gct[8 + 4 * d + h:9 + 4 * d + h, :] for h in range(nh)], axis=1)
        decay = jnp.exp(jnp.where(incl, side_by_side_cols(gc, 8 + 4 * d) - grow, NEG))
        nmat = jnp.where(strict, -(side_by_side_cols(bg, 4 * d) * kkt * decay), 0.0)
        b1 = stacked_cols(bg, 0, c, 4 * d)
        gcol = stacked_cols(gc, 0, c, 8 + 4 * d)
        glast = stacked_cols(gc, last, last + 1, 8 + 4 * d)
        e1 = jnp.exp(gcol)
        chains.append(dict(
            d=d, rows=rows, nmat=nmat,
            rhs=jnp.concatenate([b1 * vst, (b1 * e1) * kst], axis=1),
            qd=(qst * e1).astype(BF16),
            kd=kst * jnp.exp(glast - gcol),
            att=qkt * decay))
    return chains


def _dnchunk(qkv, bg, cw, alog_row, dt_row, ts, bs, s, cl):
    r = qkv.shape[0]
    nchunks = ts // DN_CHUNK
    c = ts
    row = lambda w: pl.BlockSpec((c, w), lambda i: (i, 0))
    shp = lambda w, dt: jax.ShapeDtypeStruct((r, w), dt)
    return pl.pallas_call(
        functools.partial(_dnchunk_kernel, nchunks=nchunks, bs=bs, s=s, cl=cl),
        grid=(r // c,),
        in_specs=[row(1536)] + _halo_specs(ts, 1536, r) + [
            row(128),
            pl.BlockSpec((3, 1536), lambda i: (0, 0)),
            pl.BlockSpec((1, 128), lambda i: (0, 0)),
            pl.BlockSpec((1, 128), lambda i: (0, 0))],
        scratch_shapes=[pltpu.VMEM((c, 512), F32)] * 3 + [pltpu.VMEM((c, 128), F32)],
        out_specs=[row(512)] * 6 + [pl.BlockSpec((2 * c, 256), lambda i: (i, 0))] * 2 + [row(256), row(256), row(128)],
        out_shape=([shp(512, F32)] * 2 + [shp(512, BF16)] * 4 + [jax.ShapeDtypeStruct((2 * r, 256), BF16)] * 2
                   + [shp(256, BF16)] * 2 + [shp(128, F32)]),
        compiler_params=_cp(("parallel",)),
        name="dnchunk",
    )(qkv, qkv, qkv, bg, cw, alog_row, dt_row)


def _dnscan_kernel(*refs, nsub, nb):
    nchain = 2 * nb
    ins, outs, s_ref = refs[:6 * nchain], refs[6 * nchain:7 * nchain], refs[7 * nchain]

    @pl.when(pl.program_id(0) == 0)
    def _():
        s_ref[...] = jnp.zeros_like(s_ref)

    c, nh = DN_CHUNK, DN_HEADS
    head_of_lane = lax.broadcasted_iota(I32, (1, nh * DN_HD), 1) // DN_HD
    head_of_col = lax.broadcasted_iota(I32, (1, nh * c), 1) // c
    zero = jnp.zeros((), BF16)

    def block_diag(tile, head_ids):
        return jnp.concatenate([jnp.where(head_ids == h, tile, zero) for h in range(nh)], axis=0)

    states = [s_ref[ci] for ci in range(nchain)]
    for sub in range(nsub):
        work = []
        for ci in range(nchain):
            d = ci % 2
            u_ref, w_ref, qd_ref, kd_ref, at_ref, g_ref = ins[6 * ci:6 * ci + 6]
            last = c - 1 if d == 0 else 0
            k = sub if d == 0 else nsub - 1 - sub
            rows = slice(c * k, c * k + c)
            g = g_ref[rows]
            decay = jnp.concatenate(
                [jnp.broadcast_to(jnp.exp(g[last:last + 1, 8 + 4 * d + h:9 + 4 * d + h]), (DN_HD, DN_HD))
                 for h in range(nh)], axis=0)
            ust = jnp.concatenate([u_ref[rows, DN_HD * h:DN_HD * h + DN_HD] for h in range(nh)], axis=0)
            stb = states[ci].astype(BF16)
            vnew = ust - _dot(block_diag(w_ref[rows], head_of_lane), stb)
            work.append((rows, decay, stb, vnew.astype(BF16)))
        for ci in range(nchain):
            u_ref, w_ref, qd_ref, kd_ref, at_ref, g_ref = ins[6 * ci:6 * ci + 6]
            rows, decay, stb, vnb = work[ci]
            o = (_dot(block_diag(qd_ref[rows], head_of_lane), stb)
                 + _dot(block_diag(at_ref[rows], head_of_col), vnb))
            kdt = kd_ref[slice(2 * rows.start, 2 * rows.stop)]
            states[ci] = states[ci] * decay + _dot(block_diag(kdt, head_of_col), vnb)
            for h in range(nh):
                outs[ci][rows, DN_HD * h:DN_HD * h + DN_HD] = o[c * h:c * h + c].astype(BF16)
    for ci in range(nchain):
        s_ref[ci] = states[ci]


def _dnscan(uf, ub, wf, wb, qf, qb, kf, kb, af, ab, gc, nb, s, cl, bs):
    nsub = 2
    c = DN_CHUNK * nsub
    assert cl % c == 0 and s % c == 0 and bs % c == 0
    ncc, ncl = cl // c, s // c
    ns = ncc + ncl

    def src_block(b, d):
        if d == 0:
            return lambda t: jnp.where(t < ncc, bs // c + b * ncc + t, b * ncl + t - ncc)
        return lambda t: jnp.where(t < ncc, bs // c + b * ncc + (ncc - 1 - t), b * ncl + (ncl - 1 - (t - ncc)))

    def dst_block(d):
        if d == 0:
            return lambda t: jnp.where(t < ncc, ncl + t, t - ncc)
        return lambda t: jnp.where(t < ncc, ncl + (ncc - 1 - t), ncl - 1 - (t - ncc))

    in_specs, args, out_specs = [], [], []
    for b in range(nb):
        for d, group in enumerate(((uf, wf, qf, kf, af, gc), (ub, wb, qb, kb, ab, gc))):
            idx = src_block(b, d)
            blk = lambda w, idx=idx: pl.BlockSpec((c, w), lambda t: (idx(t), 0))
            in_specs += [blk(512), blk(512), blk(512), pl.BlockSpec((2 * c, 256), lambda t, idx=idx: (idx(t), 0)),
                         blk(256), blk(128)]
            args += list(group)
            out_specs.append(pl.BlockSpec((c, 512), lambda t, f=dst_block(d): (f(t), 0)))
    outs = pl.pallas_call(
        functools.partial(_dnscan_kernel, nsub=nsub, nb=nb),
        grid=(ns,),
        in_specs=in_specs,
        out_specs=out_specs,
        out_shape=[jax.ShapeDtypeStruct((s + cl, 512), BF16)] * (2 * nb),
        scratch_shapes=[pltpu.VMEM((2 * nb, DN_HEADS * DN_HD, DN_HD), F32)],
        compiler_params=_cp(("arbitrary",)),
        name="dnscan",
    )(*args)
    return [(outs[2 * b], outs[2 * b + 1]) for b in range(nb)]


def _route(logits, bias_col):
    epg = N_EXPERTS // N_GROUPS
    tm = logits.shape[0]
    scores = jax.nn.sigmoid(logits.T[0:N_EXPERTS])
    gsel = scores + bias_col
    row = lambda a, k: a[k:k + 1]
    best = gidx = None
    for g in range(N_GROUPS):
        a = [row(gsel, epg * g + k) for k in range(epg)]
        m01, n01 = jnp.maximum(a[0], a[1]), jnp.minimum(a[0], a[1])
        m23, n23 = jnp.maximum(a[2], a[3]), jnp.minimum(a[2], a[3])
        gs = jnp.maximum(m01, m23) + jnp.maximum(jnp.minimum(m01, m23), jnp.maximum(n01, n23))
        if g == 0:
            best, gidx = gs, jnp.zeros_like(gs)
        else:
            better = gs > best
            best = jnp.where(better, gs, best)
            gidx = jnp.where(better, float(g), gidx)
    sel = [None] * epg
    raw = [None] * epg
    for g in range(N_GROUPS):
        for k in range(epg):
            v, u = row(gsel, epg * g + k), row(scores, epg * g + k)
            sel[k] = v if g == 0 else jnp.where(gidx == g, v, sel[k])
            raw[k] = u if g == 0 else jnp.where(gidx == g, u, raw[k])
    v1, e1, w1 = sel[0], jnp.zeros_like(gidx), raw[0]
    for k in range(1, epg):
        better = sel[k] > v1
        v1 = jnp.where(better, sel[k], v1)
        e1 = jnp.where(better, float(k), e1)
        w1 = jnp.where(better, raw[k], w1)
    v2 = e2 = w2 = None
    for k in range(epg):
        cand = jnp.where(e1 == k, -jnp.inf, sel[k])
        if k == 0:
            v2, e2, w2 = cand, jnp.zeros_like(gidx), raw[0]
        else:
            better = cand > v2
            v2 = jnp.where(better, cand, v2)
            e2 = jnp.where(better, float(k), e2)
            w2 = jnp.where(better, raw[k], w2)
    tot = w1 + w2
    eidx = jnp.concatenate([gidx * epg + e1, gidx * epg + e2, jnp.zeros((6, tm), F32)], axis=0).astype(I32)
    gates_t = jnp.concatenate([w1 / tot, w2 / tot, jnp.zeros((126, tm), F32)], axis=0)
    return eidx, gates_t.T


def _post_mixer(x, y, m, g2, rw, rb):
    xn = x + m[2:3] * y
    h2 = _rms(xn, g2) * (1.0 + m[4:5]) + m[3:4]
    hi = h2.astype(BF16)
    lo = (h2 - hi.astype(F32)).astype(BF16)
    hw = _dot(hi, rw)
    logits = hw[:, 0:128] + (hw[:, 128:256] + _dot(lo, rw[:, 0:128]))
    eidx, gates = _route(logits, rb)
    return xn, hi, eidx, gates


def _mix0_kernel(xl_ref, xc_ref, sc_ref, prev_ref, next_ref, z_ref, mod_ref, cw_ref, on_ref, wo_ref,
                 g2_ref, rw_ref, rb_ref, h2buf_ref, *rest, ts, bs, s, cl, nb):
    del h2buf_ref
    o_refs = rest[:2 * nb]
    xn_ref, h2_ref, ei_ref, ga_ref = rest[2 * nb:]
    first, last = _seq_edges(pl.program_id(0), ts, bs, s, cl)
    row0 = pl.program_id(0) * ts
    batch = jnp.where(row0 < bs, row0 // s, (row0 - bs) // cl)
    ya = []
    for c in range(4):
        sl = slice(128 * c, 128 * c + 128)
        sg = slice(512 + 128 * c, 512 + 128 * c + 128)
        sx = slice(1024 + 128 * c, 1024 + 128 * c + 128)
        f32 = lambda ref, rows, cols: ref[rows, cols].astype(F32)
        u = f32(sc_ref, slice(None), sg) * f32(sc_ref, slice(None), sx)
        pr = jnp.where(first, 0.0, f32(prev_ref, slice(HALO - 1, HALO), sg) * f32(prev_ref, slice(HALO - 1, HALO), sx))
        nx = jnp.where(last, 0.0, f32(next_ref, slice(0, 1), sg) * f32(next_ref, slice(0, 1), sx))
        up, un = _shifted(u, pr, nx)
        w = cw_ref[:, sl]
        ya.append((f32(sc_ref, slice(None), sl) * (up * w[0:1] + u * w[1:2] + un * w[2:3])).astype(BF16))
    yb = []
    for h in range(DN_HEADS):
        hs = slice(DN_HD * h, DN_HD * h + DN_HD)
        o = o_refs[2 * nb - 2][:, hs].astype(F32) + o_refs[2 * nb - 1][:, hs].astype(F32)
        for b in range(nb - 2, -1, -1):
            o = jnp.where(batch == b, o_refs[2 * b][:, hs].astype(F32) + o_refs[2 * b + 1][:, hs].astype(F32), o)
        yb.append((_rms(o, on_ref[...]) * _silu(z_ref[:, hs].astype(F32))).astype(BF16))
    ycat = jnp.concatenate(ya + yb, axis=1)
    y = _dot(ycat, wo_ref[...])
    xn, h2, eidx, gates = _post_mixer(_token_rows(xl_ref, xc_ref, ts, bs), y, mod_ref[0], g2_ref[...], rw_ref[...],
                                      rb_ref[...])
    xn_ref[...] = xn
    h2_ref[...] = h2.astype(BF16)
    ei_ref[...] = eidx
    ga_ref[...] = gates


def _mix0(xl, xc, sc, o_pairs, z, mods, cw, on, wo, g2, rw, rb, h2buf, ts, bs, s, cl, nb):
    d = xl.shape[1]
    r = bs + xc.shape[0]
    kern = functools.partial(_mix0_kernel, ts=ts, bs=bs, s=s, cl=cl, nb=nb)

    def o_spec(b):
        def index(i):
            row0 = i * ts
            in_lat = row0 < bs
            owner = jnp.where(in_lat, row0 // s, (row0 - bs) // cl)
            own = jnp.where(in_lat, (row0 % s) // ts, s // ts + ((row0 - bs) % cl) // ts)
            nxt = jnp.where(in_lat, 0, s // ts)
            prv = jnp.where(in_lat, s // ts - 1, (s + cl) // ts - 1)
            return jnp.where(owner == b, own, jnp.where(owner < b, nxt, prv)), 0
        return pl.BlockSpec((ts, 512), index)

    mi = functools.partial(_mod_index, tm=ts, bs=bs, s=s, nb=nb)
    row = lambda w: pl.BlockSpec((ts, w), lambda i: (i, 0))
    full = lambda a: pl.BlockSpec(a.shape, lambda i: (0,) * a.ndim)
    return pl.pallas_call(
        kern,
        grid=(r // ts,),
        in_specs=_token_specs(ts, bs, d) + [row(1536)] + _halo_specs(ts, 1536, r) + [
            row(512),
            pl.BlockSpec((1, N_MOD, d), lambda i: (mi(i), 0, 0)),
            full(cw), full(on), full(wo), full(g2), full(rw), full(rb), pl.BlockSpec(memory_space=pl.ANY)]
        + [o_spec(b) for b in range(nb) for _ in range(2)],
        out_specs=[row(d), row(d), pl.BlockSpec((8, ts), lambda i: (0, i)), row(128)],
        out_shape=[jax.ShapeDtypeStruct((r, d), F32), jax.ShapeDtypeStruct(h2buf.shape, BF16),
                   jax.ShapeDtypeStruct((8, r), I32), jax.ShapeDtypeStruct((r, 128), F32)],
        input_output_aliases={13: 1},
        compiler_params=_cp(("parallel",)),
        name="mix0",
    )(xl, xc, sc, sc, sc, z, mods, cw, on, wo, g2, rw, rb, h2buf, *[a for pair in o_pairs for a in pair])


def _gmm_kernel(te_ref, tf_ref, tv_ref, x_ref, wg_ref, wu_ref, wd_ref, y_ref, wgb, wub, wdb):
    t = pl.program_id(0)

    @pl.when(tf_ref[t] == 1)
    def _():
        wgb[...] = wg_ref[0, 0].astype(BF16)
        wub[...] = wu_ref[0, 0].astype(BF16)
        wdb[...] = wd_ref[0, 0].astype(BF16)

    @pl.when(tv_ref[t] == 1)
    def _():
        x = x_ref[...]
        a = (_silu(_dot(x, wgb[...])) * _dot(x, wub[...])).astype(BF16)
        y_ref[...] = _dot(a, wdb[...]).astype(BF16)

    @pl.when(tv_ref[t] == 0)
    def _():
        y_ref[...] = jnp.zeros_like(y_ref)


def _gmm(xs, w_gate, w_up, w_down, layer, tile_expert, tile_first, tile_valid, tmm):
    p, d = xs.shape
    de = w_gate.shape[-1]
    nt = p // tmm
    grid_spec = pltpu.PrefetchScalarGridSpec(
        num_scalar_prefetch=3,
        grid=(nt,),
        in_specs=[pl.BlockSpec((tmm, d), lambda t, te, tf, tv: (t, 0)),
                  pl.BlockSpec((1, 1, d, de), lambda t, te, tf, tv: (layer, te[t], 0, 0)),
                  pl.BlockSpec((1, 1, d, de), lambda t, te, tf, tv: (layer, te[t], 0, 0)),
                  pl.BlockSpec((1, 1, de, d), lambda t, te, tf, tv: (layer, te[t], 0, 0))],
        out_specs=pl.BlockSpec((tmm, d), lambda t, te, tf, tv: (t, 0)),
        scratch_shapes=[pltpu.VMEM((d, de), BF16), pltpu.VMEM((d, de), BF16), pltpu.VMEM((de, d), BF16)],
    )
    return pl.pallas_call(
        _gmm_kernel,
        grid_spec=grid_spec,
        out_shape=jax.ShapeDtypeStruct((p, d), BF16),
        compiler_params=_cp(("arbitrary",)),
        name="gmm",
    )(tile_expert, tile_first, tile_valid, xs, w_gate, w_up, w_down)


def _moe(h2, t_tok, e_first, e_second, w_gate, w_up, w_down, layer, tmm):
    n = 2 * t_tok
    e_flat = jnp.concatenate([e_first, e_second])
    onehot = (e_flat[:, None] == jnp.arange(N_EXPERTS, dtype=I32)[None, :]).astype(I32)
    csum = jnp.cumsum(onehot, axis=0)
    counts = csum[-1]
    ptiles = (counts + tmm - 1) // tmm
    tile_end = jnp.cumsum(ptiles)
    dest = jnp.sum(onehot * (csum - 1 + ((tile_end - ptiles) * tmm)[None, :]), axis=1)
    nt = n // tmm + N_EXPERTS
    tid = jnp.arange(nt, dtype=I32)
    tile_valid = (tid < tile_end[-1]).astype(I32)
    te = jnp.minimum(jnp.sum((tile_end[None, :] <= tid[:, None]).astype(I32), axis=1), N_EXPERTS - 1)
    last_used = jnp.max(jnp.where(tile_valid == 1, te, 0))
    te = jnp.where(tile_valid == 1, te, last_used)
    tile_first = jnp.concatenate([jnp.ones((1,), I32), (te[1:] != te[:-1]).astype(I32)])
    order = jnp.argsort(e_flat, stable=True).astype(I32)
    seg_start = (tile_end - ptiles) * tmm
    shift = seg_start - (jnp.cumsum(counts) - counts)
    pos = tid[:, None] * tmm + jnp.arange(tmm, dtype=I32)[None, :]
    te_onehot = te[:, None] == jnp.arange(N_EXPERTS, dtype=I32)[None, :]
    per_tile = lambda v: jnp.sum(jnp.where(te_onehot, v[None, :], 0), axis=1, keepdims=True)
    used = (pos - per_tile(seg_start) < per_tile(counts)) & (tile_valid[:, None] == 1)
    src = (jnp.where(used, jnp.take(order, jnp.clip(pos - per_tile(shift), 0, n - 1)), pos) % t_tok).reshape(-1)
    assert h2.shape[0] >= nt * tmm
    xs = jnp.take(h2, src, axis=0, mode="clip")
    ys = _gmm(xs, w_gate, w_up, w_down, layer, te, tile_first, tile_valid, tmm)
    return jnp.take(ys, dest, axis=0, mode="clip")


def _moe_combine(x, y1, y2, gates, m5):
    g = gates
    f = g[:, 0:1] * y1.astype(F32) + g[:, 1:2] * y2.astype(F32)
    return x + m5 * f


def _rope(x, cos, sin):
    n = x.shape[1]
    lane = lax.broadcasted_iota(I32, x.shape, 1)
    sw = jnp.where(lane % 32 < 16, pltpu.roll(x, n - 16, 1), pltpu.roll(x, 16, 1))
    reps = n // 128
    if reps > 1:
        cos = jnp.concatenate([cos] * reps, axis=1)
        sin = jnp.concatenate([sin] * reps, axis=1)
    return x * cos + sw * sin


def _inproj1_kernel(x_ref, y1_ref, y2_ref, ga_ref, m0_ref, m1_ref, g_ref, w_ref, cos_ref, sin_ref,
                    x1_ref, cq_ref, ckt_ref, cv_ref, dq_ref, dk_ref, dv_ref, *, tm, bs):
    m = m1_ref[0]
    in_lat = pl.program_id(0) * tm < bs
    scale = HEAD_DIM ** -0.5 * LOG2E
    for rows in (slice(0, tm // 2), slice(tm // 2, tm)):
        x1 = _moe_combine(x_ref[rows], y1_ref[rows], y2_ref[rows], ga_ref[rows], m0_ref[0][5:6])
        x1_ref[rows] = x1
        h = (_rms(x1, g_ref[...]) * (1.0 + m[1:2]) + m[0:1]).astype(BF16)
        cos, sin = cos_ref[rows], sin_ref[rows]
        cq = _dot(h, w_ref[:, 0:512])
        cq_ref[rows] = (jnp.where(in_lat, _rope(cq, cos, sin), cq) * scale).astype(BF16)
        ck = _dot(h, w_ref[:, 512:640])
        ckt_ref[:, rows] = jnp.where(in_lat, _rope(ck, cos, sin), ck).T.astype(BF16)
        cv_ref[rows] = _dot(h, w_ref[:, 640:768]).astype(BF16)
        dq_ref[rows] = (_dot(h, w_ref[:, 768:1280]) * scale).astype(BF16)
        dk_ref[rows] = _dot(h, w_ref[:, 1280:1792]).astype(BF16)
        dv_ref[rows] = _dot(h, w_ref[:, 1792:2304]).astype(BF16)


def _inproj1(x, y, gates, mods0, mods1, g, w, cos, sin, tm, bs, s, nb):
    r, d = x.shape
    second = pl.BlockSpec((tm, d), lambda i: (i + r // tm, 0))
    kern = functools.partial(_inproj1_kernel, tm=tm, bs=bs)
    mi = functools.partial(_mod_index, tm=tm, bs=bs, s=s, nb=nb)
    row = lambda wd: pl.BlockSpec((tm, wd), lambda i: (i, 0))
    modspec = pl.BlockSpec((1, N_MOD, d), lambda i: (mi(i), 0, 0))
    tab = pl.BlockSpec((tm, 128), lambda i: (jnp.where(i * tm < bs, (i * tm % s) // tm, 0), 0))
    shp = lambda wd, dt: jax.ShapeDtypeStruct((r, wd), dt)
    return pl.pallas_call(
        kern,
        grid=(r // tm,),
        in_specs=[row(d), row(d), second, row(128), modspec, modspec,
                  pl.BlockSpec((1, d), lambda i: (0, 0)), pl.BlockSpec(w.shape, lambda i: (0, 0)), tab, tab],
        out_specs=[row(d), row(512), pl.BlockSpec((128, tm), lambda i: (0, i)), row(128), row(512), row(512),
                   row(512)],
        out_shape=[shp(d, F32), shp(512, BF16), jax.ShapeDtypeStruct((128, r), BF16), shp(128, BF16),
                   shp(512, BF16), shp(512, BF16), shp(512, BF16)],
        compiler_params=_cp(("parallel",)),
        name="inproj1",
    )(x, y, y, gates, mods0, mods1, g, w, cos, sin)


def _swa_kernel(q_ref, ktp_ref, ktc_ref, ktn_ref, ktx_ref, vp_ref, vc_ref, vn_ref, vx_ref, sink_ref, o_ref,
                *, nblk, cl):
    i = pl.program_id(1)
    wb = SWA_BLOCK
    nloc = 3 * wb
    kt = jnp.concatenate([ktp_ref[...], ktc_ref[...], ktn_ref[...], ktx_ref[...]], axis=1)
    vv = jnp.concatenate([vp_ref[...], vc_ref[...], vn_ref[...], vx_ref[...]], axis=0)
    a_i = lax.broadcasted_iota(I32, (2 * wb, nloc), 0) % wb
    c_i = lax.broadcasted_iota(I32, (2 * wb, nloc), 1)
    lo = jnp.where(i > 0, 0, wb)
    hi = jnp.where(i < nblk - 1, 3 * wb, 2 * wb)
    ok = (c_i >= a_i) & (c_i <= a_i + 2 * SWA_WINDOW) & (c_i >= lo) & (c_i < hi)
    half = lax.broadcasted_iota(I32, (1, 128), 1) // HEAD_DIM
    zero = jnp.zeros((), BF16)
    sink = sink_ref[...]
    def scores(g):
        q2 = q_ref[:, 128 * g:128 * g + 128]
        qst = jnp.concatenate([jnp.where(half == 0, q2, zero), jnp.where(half == 1, q2, zero)], axis=0)
        s_all = _dot(qst, kt)
        return jnp.concatenate([jnp.where(ok, s_all[:, 0:nloc], NEG), s_all[:, nloc:]], axis=1)

    def softmax(g, sc):
        sk = jnp.concatenate([jnp.broadcast_to(sink[0:1, 2 * g + a:2 * g + a + 1], (wb, 1)) for a in range(2)],
                             axis=0)
        m = jnp.maximum(jnp.max(sc, axis=-1, keepdims=True), sk)
        p = jnp.exp2(sc - m)
        return p.astype(BF16), jnp.sum(p, axis=-1, keepdims=True) + jnp.exp2(sk - m)

    def output(g, p, den):
        ost = _dot(p, vv) / den
        o_ref[:, 128 * g:128 * g + 128] = jnp.where(half == 0, ost[0:wb], ost[wb:2 * wb]).astype(BF16)

    sc, pr = {}, {}
    for step in range(6):
        if step < 4:
            sc[step] = scores(step)
        if 1 <= step < 5:
            pr[step - 1] = softmax(step - 1, sc.pop(step - 1))
        if step >= 2:
            output(step - 2, *pr.pop(step - 2))


def _swa(cq, ckt, cv, sink_row, nb, s, cl, bs):
    wb = SWA_BLOCK
    nblk = s // wb
    kern = functools.partial(_swa_kernel, nblk=nblk, cl=cl)
    prev = lambda b, i: b * nblk + jnp.maximum(i - 1, 0)
    own = lambda b, i: b * nblk + i
    nxt = lambda b, i: b * nblk + jnp.minimum(i + 1, nblk - 1)
    ktspec = lambda f: pl.BlockSpec((128, wb), lambda b, i: (0, f(b, i)))
    vspec = lambda f: pl.BlockSpec((wb, 128), lambda b, i: (f(b, i), 0))
    return pl.pallas_call(
        kern,
        grid=(nb, nblk),
        in_specs=[pl.BlockSpec((wb, 512), lambda b, i: (own(b, i), 0)),
                  ktspec(prev), ktspec(own), ktspec(nxt),
                  pl.BlockSpec((128, cl), lambda b, i: (0, bs // cl + b)),
                  vspec(prev), vspec(own), vspec(nxt),
                  pl.BlockSpec((cl, 128), lambda b, i: (bs // cl + b, 0)),
                  pl.BlockSpec((1, 128), lambda b, i: (0, 0))],
        out_specs=pl.BlockSpec((wb, 512), lambda b, i: (own(b, i), 0)),
        out_shape=jax.ShapeDtypeStruct((bs, 512), BF16),
        compiler_params=_cp(("parallel", "parallel")),
        name="swa",
    )(cq, ckt, ckt, ckt, ckt, cv, cv, cv, cv, sink_row)


def _na_kernel(q_ref, k_ref, v_ref, kx_ref, vx_ref, bias_ref, o_ref, *, rows, unroll):
    half = lax.broadcasted_iota(I32, (1, 128), 1) // HEAD_DIM
    zero = jnp.zeros((), BF16)
    kx, vx = kx_ref[...], vx_ref[...]
    span = NA_KH * GRID_W

    def scores(r):
        rs = jnp.clip(r - NA_KH // 2, 0, rows - NA_KH)
        off = rs - r + NA_KH - 1
        q0 = pl.multiple_of(r * GRID_W, GRID_W)
        k0 = pl.multiple_of(rs * GRID_W, GRID_W)
        q2 = q_ref[pl.ds(q0, GRID_W), :]
        qst = jnp.concatenate([jnp.where(half == 0, q2, zero), jnp.where(half == 1, q2, zero)], axis=0)
        s_loc = (_dot_nt(qst, k_ref[pl.ds(k0, span), :])
                 + jnp.concatenate([bias_ref[0, off], bias_ref[1, off]], axis=0))
        return q0, k0, s_loc, _dot_nt(qst, kx)

    def softmax(s_loc, s_ctx):
        m = jnp.maximum(jnp.max(s_loc, axis=-1, keepdims=True), jnp.max(s_ctx, axis=-1, keepdims=True))
        p_loc = jnp.exp2(s_loc - m)
        p_ctx = jnp.exp2(s_ctx - m)
        den = jnp.sum(p_loc, axis=-1, keepdims=True) + jnp.sum(p_ctx, axis=-1, keepdims=True)
        return p_loc.astype(BF16), p_ctx.astype(BF16), den

    def body(i, carry):
        sc = [scores(i * unroll + j) for j in range(unroll)]
        pr = [softmax(s_loc, s_ctx) for (_, _, s_loc, s_ctx) in sc]
        for (q0, k0, _, _), (p_loc, p_ctx, den) in zip(sc, pr):
            ost = (_dot(p_loc, v_ref[pl.ds(k0, span), :]) + _dot(p_ctx, vx)) / den
            o = jnp.where(half == 0, ost[0:GRID_W], ost[GRID_W:2 * GRID_W])
            o_ref[pl.ds(q0, GRID_W), :] = o.astype(BF16)
        return carry

    lax.fori_loop(0, rows // unroll, body, 0)


def _na(dq, dk, dv, bias, nb, s, cl, bs):
    rows = s // GRID_W
    unroll = 4
    assert rows % unroll == 0
    kern = functools.partial(_na_kernel, rows=rows, unroll=unroll)
    seq = pl.BlockSpec((s, 128), lambda b, p: (b, p))
    ctx = pl.BlockSpec((cl, 128), lambda b, p: (bs // cl + b, p))
    return pl.pallas_call(
        kern,
        grid=(nb, NA_HEADS // 2),
        in_specs=[seq, seq, seq, ctx, ctx,
                  pl.BlockSpec((2, NA_KH, GRID_W, NA_KH * GRID_W), lambda b, p: (p, 0, 0, 0))],
        out_specs=seq,
        out_shape=jax.ShapeDtypeStruct((bs, 512), BF16),
        compiler_params=_cp(("parallel", "parallel")),
        name="na",
    )(dq, dk, dv, dk, dv, bias)


def _na_bias_table(rpb):
    c = np.arange(GRID_W)
    qs = np.clip(c - NA_KW // 2, 0, GRID_W - NA_KW)
    kc = np.arange(GRID_W)
    ok = (kc[None, :] >= qs[:, None]) & (kc[None, :] < qs[:, None] + NA_KW)
    dc = np.clip(kc[None, :] - c[:, None] + NA_KW - 1, 0, 2 * NA_KW - 2)
    sel = (np.arange(2 * NA_KW - 1)[:, None, None] == dc[None]).astype(np.float32)
    cols = jnp.einsum("hab,bck->hcak", rpb.astype(F32), sel, precision=HI)
    cols = jnp.where(ok[None, :, None, :], cols * LOG2E, NEG)
    return jnp.stack([cols[:, :, off:off + NA_KH].reshape(NA_HEADS, GRID_W, NA_KH * GRID_W)
                      for off in range(NA_KH)], axis=1)


def _mix1_kernel(x_ref, oc_ref, od_ref, mod_ref, wo_ref, g2_ref, rw_ref, rb_ref, h2buf_ref,
                 xn_ref, h2_ref, ei_ref, ga_ref):
    del h2buf_ref
    y = _dot(oc_ref[...], wo_ref[0:512, :]) + _dot(od_ref[...], wo_ref[512:1024, :])
    xn, h2, eidx, gates = _post_mixer(x_ref[...], y, mod_ref[0], g2_ref[...], rw_ref[...], rb_ref[...])
    xn_ref[...] = xn
    h2_ref[...] = h2.astype(BF16)
    ei_ref[...] = eidx
    ga_ref[...] = gates


def _mix1(x, oc, od, mods, wo, g2, rw, rb, h2buf, tm, bs, s):
    d = x.shape[1]
    row = lambda w: pl.BlockSpec((tm, w), lambda i: (i, 0))
    full = lambda a: pl.BlockSpec(a.shape, lambda i: (0,) * a.ndim)
    return pl.pallas_call(
        _mix1_kernel,
        grid=(bs // tm,),
        in_specs=[row(d), row(512), row(512), pl.BlockSpec((1, N_MOD, d), lambda i: (i * tm // s, 0, 0)),
                  full(wo), full(g2), full(rw), full(rb), pl.BlockSpec(memory_space=pl.ANY)],
        out_specs=[row(d), row(d), pl.BlockSpec((8, tm), lambda i: (0, i)), row(128)],
        out_shape=[jax.ShapeDtypeStruct((bs, d), F32), jax.ShapeDtypeStruct(h2buf.shape, BF16),
                   jax.ShapeDtypeStruct((8, bs), I32), jax.ShapeDtypeStruct((bs, 128), F32)],
        input_output_aliases={8: 1},
        compiler_params=_cp(("parallel",)),
        name="mix1",
    )(x, oc, od, mods, wo, g2, rw, rb, h2buf)


def _final_kernel(x_ref, y1_ref, y2_ref, ga_ref, mod_ref, g_ref, o_ref):
    x = _moe_combine(x_ref[...], y1_ref[...], y2_ref[...], ga_ref[...], mod_ref[0][5:6])
    o_ref[...] = _rms(x, g_ref[...])


def _final(x, y, gates, mods, g, tm, s):
    r, d = x.shape
    second = pl.BlockSpec((tm, d), lambda i: (i + r // tm, 0))
    row = lambda w: pl.BlockSpec((tm, w), lambda i: (i, 0))
    return pl.pallas_call(
        _final_kernel,
        grid=(r // tm,),
        in_specs=[row(d), row(d), second, row(128), pl.BlockSpec((1, N_MOD, d), lambda i: (i * tm // s, 0, 0)),
                  pl.BlockSpec((1, d), lambda i: (0, 0))],
        out_specs=row(d),
        out_shape=jax.ShapeDtypeStruct((r, d), F32),
        compiler_params=_cp(("parallel",)),
        name="final",
    )(x, y, y, gates, mods, g)


def _rope_tables(s):
    nf = HEAD_DIM // 4
    t = np.arange(s)
    inv = ROPE_THETA ** (-np.arange(nf, dtype=np.float64) / nf)
    ar = (t // GRID_W)[:, None] * inv
    ac = (t % GRID_W)[:, None] * inv
    cos = np.concatenate([np.cos(ar), np.cos(ar), np.cos(ac), np.cos(ac)], axis=1)
    sin = np.concatenate([-np.sin(ar), np.sin(ar), -np.sin(ac), np.sin(ac)], axis=1)
    return (jnp.asarray(np.concatenate([cos, cos], axis=1), F32),
            jnp.asarray(np.concatenate([sin, sin], axis=1), F32))


def kernel(x, c, ctx, c_ctx, ada_w, ada_b, norm1_g, norm2_g, ev_w_in, ev_w_out, sc_conv_w, dn_conv_w, dn_a_log, dn_dt_bias, dn_onorm_g, od_w_in, od_w_out, swa_sink, na_rpb, router_w, router_b, moe_w_gate, moe_w_up, moe_w_down, final_g):
    nb, s, d = x.shape
    cl = ctx.shape[1]
    bs = nb * s
    tm = 512
    ts = 256
    tmm = 512
    assert d == 1024 and s % tm == 0 and (nb * cl) % tm == 0 and cl % ts == 0 and s % ts == 0
    assert s // GRID_W >= NA_KH and bs % cl == 0 and nb + 1 <= 8

    xl, xc = x.reshape(bs, d), ctx.reshape(nb * cl, d)
    cc = jnp.zeros((8, d), F32).at[:nb].set(c).at[nb].set(c_ctx)
    mods = _ada(cc, ada_w, ada_b).reshape(ada_w.shape[0], 8, N_MOD, d)
    rw32 = jnp.pad(router_w, ((0, 0), (0, 128 - N_EXPERTS)))
    rw_hi = rw32.astype(BF16)
    rw = jnp.concatenate([rw_hi, (rw32 - rw_hi.astype(F32)).astype(BF16)], axis=1)
    rb = router_b.reshape(N_EXPERTS, 1)
    row = lambda v: v.reshape(1, -1)

    w_in0 = jnp.pad(ev_w_in[0], ((0, 0), (0, 3712 - ev_w_in.shape[-1]))).astype(BF16)
    sc, qkv, z, bg = _inproj0(xl, xc, mods[0], row(norm1_g[0]), w_in0, tm, bs, s, nb)
    pad16 = lambda v: jnp.pad(v.reshape(-1), (8, 128 - 16)).reshape(1, 128)
    uf, ub, wf, wb, qf, qb, kf, kb, af, ab, gc = _dnchunk(qkv, bg, dn_conv_w[0], pad16(dn_a_log[0]),
                                                          pad16(dn_dt_bias[0]), ts, bs, s, cl)
    o_pairs = _dnscan(uf, ub, wf, wb, qf, qb, kf, kb, af, ab, gc, nb, s, cl, bs)
    r_all = bs + nb * cl
    moe_rows = lambda t: (2 * t // tmm + N_EXPERTS) * tmm
    x0, h2, ei, ga = _mix0(xl, xc, sc, o_pairs, z, mods[0], sc_conv_w[0], row(dn_onorm_g[0]),
                           ev_w_out[0].astype(BF16), row(norm2_g[0]), rw, rb,
                           jnp.zeros((moe_rows(r_all), d), BF16), ts, bs, s, cl, nb)
    y = _moe(h2, r_all, ei[0], ei[1], moe_w_gate, moe_w_up, moe_w_down, 0, tmm)

    perm = np.concatenate([np.arange(HEAD_DIM) + HEAD_DIM * (g + 4 * a) for g in range(4) for a in range(2)])
    w1 = od_w_in[0]
    w_in1 = jnp.concatenate([w1[:, 0:512][:, perm], w1[:, 512:]], axis=1).astype(BF16)
    wo1 = od_w_out[0]
    w_out1 = jnp.concatenate([wo1[0:512][perm], wo1[512:]], axis=0).astype(BF16)
    sink_row = jnp.pad(swa_sink[0][np.array([g + 4 * a for g in range(4) for a in range(2)])] * LOG2E,
                       (0, 128 - SWA_HEADS)).reshape(1, 128)
    cos, sin = _rope_tables(s)
    x1, cq, ckt, cv, dq, dk, dv = _inproj1(x0, y, ga, mods[0], mods[1], row(norm1_g[1]), w_in1, cos, sin,
                                           tm, bs, s, nb)
    oc = _swa(cq, ckt, cv, sink_row, nb, s, cl, bs)
    od = _na(dq, dk, dv, _na_bias_table(na_rpb[0]), nb, s, cl, bs)
    assert moe_rows(bs) <= h2.shape[0]
    x2, h2, ei, ga = _mix1(x1, oc, od, mods[1], w_out1, row(norm2_g[1]), rw, rb, h2, tm, bs, s)
    y = _moe(h2, bs, ei[0], ei[1], moe_w_gate, moe_w_up, moe_w_down, 1, tmm)
    out = _final(x2, y, ga, mods[1], row(final_g), tm, s)
    return out.reshape(nb, s, d)
```

```python
import functools
import math

import numpy as np
import jax
import jax.numpy as jnp
from jax import lax
from jax.experimental import pallas as pl
from jax.experimental.pallas import tpu as pltpu

F32 = jnp.float32
BF16 = jnp.bfloat16
I32 = jnp.int32
HI = lax.Precision.HIGHEST

EPS = 1e-6
N_MOD = 6
GRID_W = 64
HEAD_DIM = 64
DN_HEADS = 4
DN_HD = 128
DN_CHUNK = 64
SWA_HEADS = 8
SWA_KV = 2
SWA_BLOCK = 128
SWA_WINDOW = 128
NA_HEADS = 8
NA_KH = 8
NA_KW = 16
ROPE_THETA = 10000.0
N_EXPERTS = 16
N_GROUPS = 4
NEG = -1e30
LOG2E = 1.4426950408889634
VMEM_LIMIT = 56 * 1024 * 1024


def _cp(sem, vmem=VMEM_LIMIT):
    return pltpu.CompilerParams(dimension_semantics=sem, vmem_limit_bytes=vmem)


def _dot(a, b, precision=None):
    return jnp.dot(a, b, preferred_element_type=F32, precision=precision)


def _dot_nt(a, b, precision=None):
    return lax.dot_general(a, b, (((1,), (1,)), ((), ())), preferred_element_type=F32, precision=precision)


def _dot_tn(a, b, precision=None):
    return lax.dot_general(a, b, (((0,), (0,)), ((), ())), preferred_element_type=F32, precision=precision)


def _silu(x):
    return x * jax.nn.sigmoid(x)


def _rms(x, g):
    return x * lax.rsqrt(jnp.mean(x * x, axis=-1, keepdims=True) + EPS) * g


def _ada_kernel(cc_ref, w_ref, b_ref, o_ref):
    a = _silu(cc_ref[...])
    o_ref[0] = _dot(a, w_ref[0], HI) + b_ref[0]


def _ada(cc, ada_w, ada_b):
    depth, d, n = ada_w.shape
    tn = 1536
    return pl.pallas_call(
        _ada_kernel,
        grid=(depth, n // tn),
        in_specs=[pl.BlockSpec((8, d), lambda l, j: (0, 0)),
                  pl.BlockSpec((1, d, tn), lambda l, j: (l, 0, j)),
                  pl.BlockSpec((1, 1, tn), lambda l, j: (l, 0, j))],
        out_specs=pl.BlockSpec((1, 8, tn), lambda l, j: (l, 0, j)),
        out_shape=jax.ShapeDtypeStruct((depth, 8, n), F32),
        compiler_params=_cp(("parallel", "parallel")),
        name="ada",
    )(cc, ada_w, ada_b.reshape(depth, 1, n))


def _mod_index(i, tm, bs, s, nb):
    row0 = i * tm
    return jnp.where(row0 < bs, row0 // s, nb)


def _token_specs(tm, bs, d):
    nlat = bs // tm
    return [pl.BlockSpec((tm, d), lambda i: (jnp.minimum(i, nlat - 1), 0)),
            pl.BlockSpec((tm, d), lambda i: (jnp.maximum(i - nlat, 0), 0))]


def _token_rows(xl_ref, xc_ref, tm, bs):
    return jnp.where(pl.program_id(0) * tm < bs, xl_ref[...], xc_ref[...])


def _inproj0_kernel(xl_ref, xc_ref, mod_ref, g_ref, w_ref, sc_ref, qkv_ref, z_ref, bg_ref, *, tm, bs):
    m = mod_ref[0]
    h = (_rms(_token_rows(xl_ref, xc_ref, tm, bs), g_ref[...]) * (1.0 + m[1:2]) + m[0:1]).astype(BF16)
    sc_ref[...] = _dot(h, w_ref[:, 0:1536]).astype(BF16)
    qkv_ref[...] = _dot(h, w_ref[:, 1536:3072]).astype(BF16)
    z_ref[...] = _dot(h, w_ref[:, 3072:3584]).astype(BF16)
    bg_ref[...] = _dot(h, w_ref[:, 3584:3712])


def _inproj0(xl, xc, mods, g, w, tm, bs, s, nb):
    d = xl.shape[1]
    r = bs + xc.shape[0]
    mi = functools.partial(_mod_index, tm=tm, bs=bs, s=s, nb=nb)
    return pl.pallas_call(
        functools.partial(_inproj0_kernel, tm=tm, bs=bs),
        grid=(r // tm,),
        in_specs=_token_specs(tm, bs, d) + [
            pl.BlockSpec((1, N_MOD, d), lambda i: (mi(i), 0, 0)),
            pl.BlockSpec((1, d), lambda i: (0, 0)),
            pl.BlockSpec(w.shape, lambda i: (0, 0))],
        out_specs=[pl.BlockSpec((tm, 1536), lambda i: (i, 0)),
                   pl.BlockSpec((tm, 1536), lambda i: (i, 0)),
                   pl.BlockSpec((tm, 512), lambda i: (i, 0)),
                   pl.BlockSpec((tm, 128), lambda i: (i, 0))],
        out_shape=[jax.ShapeDtypeStruct((r, 1536), BF16), jax.ShapeDtypeStruct((r, 1536), BF16),
                   jax.ShapeDtypeStruct((r, 512), BF16), jax.ShapeDtypeStruct((r, 128), F32)],
        compiler_params=_cp(("parallel",)),
        name="inproj0",
    )(xl, xc, mods, g, w)


def _seq_edges(i, ts, bs, s, cl):
    row0 = i * ts
    in_lat = row0 < bs
    r_in = jnp.where(in_lat, row0 % s, (row0 - bs) % cl)
    seqlen = jnp.where(in_lat, s, cl)
    return r_in == 0, r_in + ts == seqlen


def _shifted(x, prev_row, next_row):
    n = x.shape[0]
    rows = lax.broadcasted_iota(I32, x.shape, 0)
    xp = jnp.where(rows == 0, prev_row, pltpu.roll(x, 1, 0))
    xn = jnp.where(rows == n - 1, next_row, pltpu.roll(x, n - 1, 0))
    return xp, xn


HALO = 16


def _halo_specs(ts, width, r):
    nblk = r // HALO
    k = ts // HALO
    return [pl.BlockSpec((HALO, width), lambda i: (jnp.maximum(i * k - 1, 0), 0)),
            pl.BlockSpec((HALO, width), lambda i: (jnp.minimum((i + 1) * k, nblk - 1), 0))]


def _dnprep_tile(x_ref, prev_ref, next_ref, bg_ref, cw_ref, alog_ref, dt_ref,
                 q_ref, k_ref, v_ref, bga_ref, *, ts, bs, s, cl):
    first, last = _seq_edges(pl.program_id(0), ts, bs, s, cl)
    for c in range(12):
        sl = slice(128 * c, 128 * c + 128)
        x = x_ref[:, sl].astype(F32)
        pr = jnp.where(first, 0.0, prev_ref[HALO - 1:HALO, sl].astype(F32))
        nx = jnp.where(last, 0.0, next_ref[0:1, sl].astype(F32))
        xp, xn = _shifted(x, pr, nx)
        w = cw_ref[:, sl]
        y = _silu(xp * w[0:1] + x * w[1:2] + xn * w[2:3])
        hs = slice(128 * (c % 4), 128 * (c % 4) + 128)
        if c < 8:
            y = y * lax.rsqrt(jnp.sum(y * y, axis=-1, keepdims=True) + EPS)
        if c < 4:
            q_ref[:, hs] = y * DN_HD ** -0.5
        elif c < 8:
            k_ref[:, hs] = y
        else:
            v_ref[:, hs] = y
    b = bg_ref[...]
    cols = lax.broadcasted_iota(I32, b.shape, 1)
    beta = jax.nn.sigmoid(b)
    t = b + dt_ref[...]
    softplus = jnp.maximum(t, 0.0) + jnp.log1p(jnp.exp(-jnp.abs(t)))
    g = -jnp.exp(alog_ref[...]) * softplus
    bga_ref[...] = jnp.where(cols < 8, beta, jnp.where(cols < 16, g, 0.0))


def _dnchunk_kernel(x_ref, prev_ref, next_ref, bgraw_ref, cw_ref, alog_ref, dt_ref,
                    uf_ref, ub_ref, wf_ref, wb_ref, qf_ref, qb_ref, kf_ref, kb_ref, af_ref, ab_ref, gc_ref,
                    q_ref, k_ref, v_ref, bg_ref, *, nchunks, bs, s, cl):
    _dnprep_tile(x_ref, prev_ref, next_ref, bgraw_ref, cw_ref, alog_ref, dt_ref, q_ref, k_ref, v_ref, bg_ref,
                 ts=nchunks * DN_CHUNK, bs=bs, s=s, cl=cl)
    outs = ((uf_ref, wf_ref, qf_ref, kf_ref, af_ref), (ub_ref, wb_ref, qb_ref, kb_ref, ab_ref))
    c, nh = DN_CHUNK, DN_HEADS
    head_of_col = lax.broadcasted_iota(I32, (1, nh * c), 1) // c
    zero = jnp.zeros((), BF16)

    def block_diag(x):
        return jnp.concatenate([jnp.where(head_of_col == h, x, zero) for h in range(nh)], axis=0)

    chains = []
    for cc in range(nchunks):
        chains += _dnchunk_setup(slice(cc * c, (cc + 1) * c), q_ref, k_ref, v_ref, bg_ref, gc_ref)
    for ch in chains:
        ch["tm"] = ch["nmat"]
        nb16 = ch["nmat"].astype(BF16)
        ch["npow"] = _dot(nb16, block_diag(nb16))
    for _ in range(4):
        for ch in chains:
            nb16 = ch["npow"].astype(BF16)
            ch["both"] = _dot(jnp.concatenate([nb16, ch["tm"].astype(BF16)], axis=0), block_diag(nb16))
        for ch in chains:
            ch["tm"] = ch["tm"] + ch["npow"] + ch["both"][c:2 * c]
            ch["npow"] = ch["both"][0:c]
    for ch in chains:
        ch["both"] = _dot(ch["tm"].astype(BF16), block_diag(ch["npow"].astype(BF16)))
    for ch in chains:
        ch["tm"] = ch["tm"] + ch["npow"] + ch["both"]
    for ch in chains:
        ch["uw"] = ch["rhs"] + _dot(block_diag(ch["tm"].astype(BF16)), ch["rhs"].astype(BF16))
    for ch in chains:
        u_ref, w_ref, qd_ref, kd_ref, at_ref = outs[ch["d"]]
        rows, uw = ch["rows"], ch["uw"]
        for h in range(nh):
            hs = slice(DN_HD * h, DN_HD * h + DN_HD)
            rs = slice(c * h, c * h + c)
            u_ref[rows, hs] = uw[rs, 0:DN_HD]
            w_ref[rows, hs] = uw[rs, DN_HD:2 * DN_HD].astype(BF16)
            qd_ref[rows, hs] = ch["qd"][rs]
        kd_ref[slice(2 * rows.start, 2 * rows.stop)] = ch["kd"].T.astype(BF16)
        at_ref[rows] = ch["att"].astype(BF16)


def _dnchunk_setup(rows, q_ref, k_ref, v_ref, bg_ref, gc_ref):
    c, nh = DN_CHUNK, DN_HEADS
    n = c * nh
    bg = bg_ref[rows]
    i64 = lax.broadcasted_iota(I32, (c, c), 0)
    j64 = lax.broadcasted_iota(I32, (c, c), 1)
    cols = lax.broadcasted_iota(I32, bg.shape, 1)
    gcf = _dot((i64 >= j64).astype(F32), bg, HI)
    gcb = _dot((i64 <= j64).astype(F32), bg, HI)
    gc = jnp.where(cols >= 12, gcb, gcf)
    gc_ref[rows] = gc
    gct = gc.T
    ii = lax.broadcasted_iota(I32, (c, n), 0)
    jj = lax.broadcasted_iota(I32, (c, n), 1)
    head_of_col = jj // c
    jj = jj % c

    def stack(ref):
        return jnp.concatenate([ref[rows, DN_HD * h:DN_HD * h + DN_HD] for h in range(nh)], axis=0)

    def stacked_cols(arr, r0, r1, col0):
        return jnp.concatenate([jnp.broadcast_to(arr[r0:r1, col0 + h:col0 + h + 1], (c, DN_HD)) for h in range(nh)],
                               axis=0)

    def side_by_side_cols(arr, col0):
        out = arr[:, col0 + nh - 1:col0 + nh]
        for h in range(nh - 2, -1, -1):
            out = jnp.where(head_of_col == h, arr[:, col0 + h:col0 + h + 1], out)
        return out

    def diag_blocks(x):
        out = x[(nh - 1) * c:nh * c]
        for h in range(nh - 2, -1, -1):
            out = jnp.where(head_of_col == h, x[h * c:(h + 1) * c], out)
        return out

    kst, qst, vst = stack(k_ref), stack(q_ref), stack(v_ref)
    kb = kst.astype(BF16)
    kq = _dot_nt(jnp.concatenate([kb, qst.astype(BF16)], axis=0), kb)
    kkt, qkt = diag_blocks(kq[0:n]), diag_blocks(kq[n:2 * n])
    chains = []
    for d in range(2):
        incl = (ii >= jj) if d == 0 else (ii <= jj)
        strict = (ii > jj) if d == 0 else (ii < jj)
        last = c - 1 if d == 0 else 0
        grow = jnp.concatenate([gct[8 + 4 * d + h:9 + 4 * d + h, :] for h in range(nh)], axis=1)
        decay = jnp.exp(jnp.where(incl, side_by_side_cols(gc, 8 + 4 * d) - grow, NEG))
        nmat = jnp.where(strict, -(side_by_side_cols(bg, 4 * d) * kkt * decay), 0.0)
        b1 = stacked_cols(bg, 0, c, 4 * d)
        gcol = stacked_cols(gc, 0, c, 8 + 4 * d)
        glast = stacked_cols(gc, last, last + 1, 8 + 4 * d)
        e1 = jnp.exp(gcol)
        chains.append(dict(
            d=d, rows=rows, nmat=nmat,
            rhs=jnp.concatenate([b1 * vst, (b1 * e1) * kst], axis=1),
            qd=(qst * e1).astype(BF16),
            kd=kst * jnp.exp(glast - gcol),
            att=qkt * decay))
    return chains


def _dnchunk(qkv, bg, cw, alog_row, dt_row, ts, bs, s, cl):
    r = qkv.shape[0]
    nchunks = ts // DN_CHUNK
    c = ts
    row = lambda w: pl.BlockSpec((c, w), lambda i: (i, 0))
    shp = lambda w, dt: jax.ShapeDtypeStruct((r, w), dt)
    return pl.pallas_call(
        functools.partial(_dnchunk_kernel, nchunks=nchunks, bs=bs, s=s, cl=cl),
        grid=(r // c,),
        in_specs=[row(1536)] + _halo_specs(ts, 1536, r) + [
            row(128),
            pl.BlockSpec((3, 1536), lambda i: (0, 0)),
            pl.BlockSpec((1, 128), lambda i: (0, 0)),
            pl.BlockSpec((1, 128), lambda i: (0, 0))],
        scratch_shapes=[pltpu.VMEM((c, 512), F32)] * 3 + [pltpu.VMEM((c, 128), F32)],
        out_specs=[row(512)] * 6 + [pl.BlockSpec((2 * c, 256), lambda i: (i, 0))] * 2 + [row(256), row(256), row(128)],
        out_shape=([shp(512, F32)] * 2 + [shp(512, BF16)] * 4 + [jax.ShapeDtypeStruct((2 * r, 256), BF16)] * 2
                   + [shp(256, BF16)] * 2 + [shp(128, F32)]),
        compiler_params=_cp(("parallel",)),
        name="dnchunk",
    )(qkv, qkv, qkv, bg, cw, alog_row, dt_row)


def _dnscan_kernel(*refs, nsub, nb):
    nchain = 2 * nb
    ins, outs, s_ref = refs[:6 * nchain], refs[6 * nchain:7 * nchain], refs[7 * nchain]

    @pl.when(pl.program_id(0) == 0)
    def _():
        s_ref[...] = jnp.zeros_like(s_ref)

    c, nh = DN_CHUNK, DN_HEADS
    head_of_lane = lax.broadcasted_iota(I32, (1, nh * DN_HD), 1) // DN_HD
    head_of_col = lax.broadcasted_iota(I32, (1, nh * c), 1) // c
    zero = jnp.zeros((), BF16)

    def block_diag(tile, head_ids):
        return jnp.concatenate([jnp.where(head_ids == h, tile, zero) for h in range(nh)], axis=0)

    states = [s_ref[ci] for ci in range(nchain)]
    for sub in range(nsub):
        work = []
        for ci in range(nchain):
            d = ci % 2
            u_ref, w_ref, qd_ref, kd_ref, at_ref, g_ref = ins[6 * ci:6 * ci + 6]
            last = c - 1 if d == 0 else 0
            k = sub if d == 0 else nsub - 1 - sub
            rows = slice(c * k, c * k + c)
            g = g_ref[rows]
            decay = jnp.concatenate(
                [jnp.broadcast_to(jnp.exp(g[last:last + 1, 8 + 4 * d + h:9 + 4 * d + h]), (DN_HD, DN_HD))
                 for h in range(nh)], axis=0)
            ust = jnp.concatenate([u_ref[rows, DN_HD * h:DN_HD * h + DN_HD] for h in range(nh)], axis=0)
            stb = states[ci].astype(BF16)
            vnew = ust - _dot(block_diag(w_ref[rows], head_of_lane), stb)
            work.append((rows, decay, stb, vnew.astype(BF16)))
        for ci in range(nchain):
            u_ref, w_ref, qd_ref, kd_ref, at_ref, g_ref = ins[6 * ci:6 * ci + 6]
            rows, decay, stb, vnb = work[ci]
            o = (_dot(block_diag(qd_ref[rows], head_of_lane), stb)
                 + _dot(block_diag(at_ref[rows], head_of_col), vnb))
            kdt = kd_ref[slice(2 * rows.start, 2 * rows.stop)]
            states[ci] = states[ci] * decay + _dot(block_diag(kdt, head_of_col), vnb)
            for h in range(nh):
                outs[ci][rows, DN_HD * h:DN_HD * h + DN_HD] = o[c * h:c * h + c].astype(BF16)
    for ci in range(nchain):
        s_ref[ci] = states[ci]


def _dnscan(uf, ub, wf, wb, qf, qb, kf, kb, af, ab, gc, nb, s, cl, bs):
    nsub = 4
    c = DN_CHUNK * nsub
    assert cl % c == 0 and s % c == 0 and bs % c == 0
    ncc, ncl = cl // c, s // c
    ns = ncc + ncl

    def src_block(b, d):
        if d == 0:
            return lambda t: jnp.where(t < ncc, bs // c + b * ncc + t, b * ncl + t - ncc)
        return lambda t: jnp.where(t < ncc, bs // c + b * ncc + (ncc - 1 - t), b * ncl + (ncl - 1 - (t - ncc)))

    def dst_block(d):
        if d == 0:
            return lambda t: jnp.where(t < ncc, ncl + t, t - ncc)
        return lambda t: jnp.where(t < ncc, ncl + (ncc - 1 - t), ncl - 1 - (t - ncc))

    in_specs, args, out_specs = [], [], []
    for b in range(nb):
        for d, group in enumerate(((uf, wf, qf, kf, af, gc), (ub, wb, qb, kb, ab, gc))):
            idx = src_block(b, d)
            blk = lambda w, idx=idx: pl.BlockSpec((c, w), lambda t: (idx(t), 0))
            in_specs += [blk(512), blk(512), blk(512), pl.BlockSpec((2 * c, 256), lambda t, idx=idx: (idx(t), 0)),
                         blk(256), blk(128)]
            args += list(group)
            out_specs.append(pl.BlockSpec((c, 512), lambda t, f=dst_block(d): (f(t), 0)))
    outs = pl.pallas_call(
        functools.partial(_dnscan_kernel, nsub=nsub, nb=nb),
        grid=(ns,),
        in_specs=in_specs,
        out_specs=out_specs,
        out_shape=[jax.ShapeDtypeStruct((s + cl, 512), BF16)] * (2 * nb),
        scratch_shapes=[pltpu.VMEM((2 * nb, DN_HEADS * DN_HD, DN_HD), F32)],
        compiler_params=_cp(("arbitrary",)),
        name="dnscan",
    )(*args)
    return [(outs[2 * b], outs[2 * b + 1]) for b in range(nb)]


def _route(logits, bias_col):
    epg = N_EXPERTS // N_GROUPS
    tm = logits.shape[0]
    scores = jax.nn.sigmoid(logits.T[0:N_EXPERTS])
    gsel = scores + bias_col
    row = lambda a, k: a[k:k + 1]
    best = gidx = None
    for g in range(N_GROUPS):
        a = [row(gsel, epg * g + k) for k in range(epg)]
        m01, n01 = jnp.maximum(a[0], a[1]), jnp.minimum(a[0], a[1])
        m23, n23 = jnp.maximum(a[2], a[3]), jnp.minimum(a[2], a[3])
        gs = jnp.maximum(m01, m23) + jnp.maximum(jnp.minimum(m01, m23), jnp.maximum(n01, n23))
        if g == 0:
            best, gidx = gs, jnp.zeros_like(gs)
        else:
            better = gs > best
            best = jnp.where(better, gs, best)
            gidx = jnp.where(better, float(g), gidx)
    sel = [None] * epg
    raw = [None] * epg
    for g in range(N_GROUPS):
        for k in range(epg):
            v, u = row(gsel, epg * g + k), row(scores, epg * g + k)
            sel[k] = v if g == 0 else jnp.where(gidx == g, v, sel[k])
            raw[k] = u if g == 0 else jnp.where(gidx == g, u, raw[k])
    v1, e1, w1 = sel[0], jnp.zeros_like(gidx), raw[0]
    for k in range(1, epg):
        better = sel[k] > v1
        v1 = jnp.where(better, sel[k], v1)
        e1 = jnp.where(better, float(k), e1)
        w1 = jnp.where(better, raw[k], w1)
    v2 = e2 = w2 = None
    for k in range(epg):
        cand = jnp.where(e1 == k, -jnp.inf, sel[k])
        if k == 0:
            v2, e2, w2 = cand, jnp.zeros_like(gidx), raw[0]
        else:
            better = cand > v2
            v2 = jnp.where(better, cand, v2)
            e2 = jnp.where(better, float(k), e2)
            w2 = jnp.where(better, raw[k], w2)
    tot = w1 + w2
    eidx = jnp.concatenate([gidx * epg + e1, gidx * epg + e2, jnp.zeros((6, tm), F32)], axis=0).astype(I32)
    gates_t = jnp.concatenate([w1 / tot, w2 / tot, jnp.zeros((126, tm), F32)], axis=0)
    return eidx, gates_t.T


def _post_mixer(x, y, m, g2, rw, rb):
    xn = x + m[2:3] * y
    h2 = _rms(xn, g2) * (1.0 + m[4:5]) + m[3:4]
    hi = h2.astype(BF16)
    lo = (h2 - hi.astype(F32)).astype(BF16)
    hw = _dot(hi, rw)
    logits = hw[:, 0:128] + (hw[:, 128:256] + _dot(lo, rw[:, 0:128]))
    eidx, gates = _route(logits, rb)
    return xn, hi, eidx, gates


def _mix0_kernel(xl_ref, xc_ref, sc_ref, prev_ref, next_ref, z_ref, mod_ref, cw_ref, on_ref, wo_ref,
                 g2_ref, rw_ref, rb_ref, h2buf_ref, *rest, ts, bs, s, cl, nb):
    del h2buf_ref
    o_refs = rest[:2 * nb]
    xn_ref, h2_ref, ei_ref, ga_ref = rest[2 * nb:]
    first, last = _seq_edges(pl.program_id(0), ts, bs, s, cl)
    row0 = pl.program_id(0) * ts
    batch = jnp.where(row0 < bs, row0 // s, (row0 - bs) // cl)
    ya = []
    for c in range(4):
        sl = slice(128 * c, 128 * c + 128)
        sg = slice(512 + 128 * c, 512 + 128 * c + 128)
        sx = slice(1024 + 128 * c, 1024 + 128 * c + 128)
        f32 = lambda ref, rows, cols: ref[rows, cols].astype(F32)
        u = f32(sc_ref, slice(None), sg) * f32(sc_ref, slice(None), sx)
        pr = jnp.where(first, 0.0, f32(prev_ref, slice(HALO - 1, HALO), sg) * f32(prev_ref, slice(HALO - 1, HALO), sx))
        nx = jnp.where(last, 0.0, f32(next_ref, slice(0, 1), sg) * f32(next_ref, slice(0, 1), sx))
        up, un = _shifted(u, pr, nx)
        w = cw_ref[:, sl]
        ya.append((f32(sc_ref, slice(None), sl) * (up * w[0:1] + u * w[1:2] + un * w[2:3])).astype(BF16))
    yb = []
    for h in range(DN_HEADS):
        hs = slice(DN_HD * h, DN_HD * h + DN_HD)
        o = o_refs[2 * nb - 2][:, hs].astype(F32) + o_refs[2 * nb - 1][:, hs].astype(F32)
        for b in range(nb - 2, -1, -1):
            o = jnp.where(batch == b, o_refs[2 * b][:, hs].astype(F32) + o_refs[2 * b + 1][:, hs].astype(F32), o)
        yb.append((_rms(o, on_ref[...]) * _silu(z_ref[:, hs].astype(F32))).astype(BF16))
    ycat = jnp.concatenate(ya + yb, axis=1)
    y = _dot(ycat, wo_ref[...])
    xn, h2, eidx, gates = _post_mixer(_token_rows(xl_ref, xc_ref, ts, bs), y, mod_ref[0], g2_ref[...], rw_ref[...],
                                      rb_ref[...])
    xn_ref[...] = xn
    h2_ref[...] = h2.astype(BF16)
    ei_ref[...] = eidx
    ga_ref[...] = gates


def _mix0(xl, xc, sc, o_pairs, z, mods, cw, on, wo, g2, rw, rb, h2buf, ts, bs, s, cl, nb):
    d = xl.shape[1]
    r = bs + xc.shape[0]
    kern = functools.partial(_mix0_kernel, ts=ts, bs=bs, s=s, cl=cl, nb=nb)

    def o_spec(b):
        def index(i):
            row0 = i * ts
            in_lat = row0 < bs
            owner = jnp.where(in_lat, row0 // s, (row0 - bs) // cl)
            own = jnp.where(in_lat, (row0 % s) // ts, s // ts + ((row0 - bs) % cl) // ts)
            nxt = jnp.where(in_lat, 0, s // ts)
            prv = jnp.where(in_lat, s // ts - 1, (s + cl) // ts - 1)
            return jnp.where(owner == b, own, jnp.where(owner < b, nxt, prv)), 0
        return pl.BlockSpec((ts, 512), index)

    mi = functools.partial(_mod_index, tm=ts, bs=bs, s=s, nb=nb)
    row = lambda w: pl.BlockSpec((ts, w), lambda i: (i, 0))
    full = lambda a: pl.BlockSpec(a.shape, lambda i: (0,) * a.ndim)
    return pl.pallas_call(
        kern,
        grid=(r // ts,),
        in_specs=_token_specs(ts, bs, d) + [row(1536)] + _halo_specs(ts, 1536, r) + [
            row(512),
            pl.BlockSpec((1, N_MOD, d), lambda i: (mi(i), 0, 0)),
            full(cw), full(on), full(wo), full(g2), full(rw), full(rb), pl.BlockSpec(memory_space=pl.ANY)]
        + [o_spec(b) for b in range(nb) for _ in range(2)],
        out_specs=[row(d), row(d), pl.BlockSpec((8, ts), lambda i: (0, i)), row(128)],
        out_shape=[jax.ShapeDtypeStruct((r, d), F32), jax.ShapeDtypeStruct(h2buf.shape, BF16),
                   jax.ShapeDtypeStruct((8, r), I32), jax.ShapeDtypeStruct((r, 128), F32)],
        input_output_aliases={13: 1},
        compiler_params=_cp(("parallel",)),
        name="mix0",
    )(xl, xc, sc, sc, sc, z, mods, cw, on, wo, g2, rw, rb, h2buf, *[a for pair in o_pairs for a in pair])


def _gmm_kernel(te_ref, tf_ref, tv_ref, x_ref, wg_ref, wu_ref, wd_ref, y_ref, wgb, wub, wdb):
    t = pl.program_id(0)

    @pl.when(tf_ref[t] == 1)
    def _():
        wgb[...] = wg_ref[0, 0].astype(BF16)
        wub[...] = wu_ref[0, 0].astype(BF16)
        wdb[...] = wd_ref[0, 0].astype(BF16)

    @pl.when(tv_ref[t] == 1)
    def _():
        x = x_ref[...]
        a = (_silu(_dot(x, wgb[...])) * _dot(x, wub[...])).astype(BF16)
        y_ref[...] = _dot(a, wdb[...]).astype(BF16)

    @pl.when(tv_ref[t] == 0)
    def _():
        y_ref[...] = jnp.zeros_like(y_ref)


def _gmm(xs, w_gate, w_up, w_down, layer, tile_expert, tile_first, tile_valid, tmm):
    p, d = xs.shape
    de = w_gate.shape[-1]
    nt = p // tmm
    grid_spec = pltpu.PrefetchScalarGridSpec(
        num_scalar_prefetch=3,
        grid=(nt,),
        in_specs=[pl.BlockSpec((tmm, d), lambda t, te, tf, tv: (t, 0)),
                  pl.BlockSpec((1, 1, d, de), lambda t, te, tf, tv: (layer, te[t], 0, 0)),
                  pl.BlockSpec((1, 1, d, de), lambda t, te, tf, tv: (layer, te[t], 0, 0)),
                  pl.BlockSpec((1, 1, de, d), lambda t, te, tf, tv: (layer, te[t], 0, 0))],
        out_specs=pl.BlockSpec((tmm, d), lambda t, te, tf, tv: (t, 0)),
        scratch_shapes=[pltpu.VMEM((d, de), BF16), pltpu.VMEM((d, de), BF16), pltpu.VMEM((de, d), BF16)],
    )
    return pl.pallas_call(
        _gmm_kernel,
        grid_spec=grid_spec,
        out_shape=jax.ShapeDtypeStruct((p, d), BF16),
        compiler_params=_cp(("arbitrary",)),
        name="gmm",
    )(tile_expert, tile_first, tile_valid, xs, w_gate, w_up, w_down)


def _moe(h2, t_tok, e_first, e_second, w_gate, w_up, w_down, layer, tmm):
    n = 2 * t_tok
    e_flat = jnp.concatenate([e_first, e_second])
    onehot = (e_flat[:, None] == jnp.arange(N_EXPERTS, dtype=I32)[None, :]).astype(I32)
    csum = jnp.cumsum(onehot, axis=0)
    counts = csum[-1]
    ptiles = (counts + tmm - 1) // tmm
    tile_end = jnp.cumsum(ptiles)
    dest = jnp.sum(onehot * (csum - 1 + ((tile_end - ptiles) * tmm)[None, :]), axis=1)
    nt = n // tmm + N_EXPERTS
    tid = jnp.arange(nt, dtype=I32)
    tile_valid = (tid < tile_end[-1]).astype(I32)
    te = jnp.minimum(jnp.sum((tile_end[None, :] <= tid[:, None]).astype(I32), axis=1), N_EXPERTS - 1)
    last_used = jnp.max(jnp.where(tile_valid == 1, te, 0))
    te = jnp.where(tile_valid == 1, te, last_used)
    tile_first = jnp.concatenate([jnp.ones((1,), I32), (te[1:] != te[:-1]).astype(I32)])
    order = jnp.argsort(e_flat, stable=True).astype(I32)
    seg_start = (tile_end - ptiles) * tmm
    shift = seg_start - (jnp.cumsum(counts) - counts)
    pos = tid[:, None] * tmm + jnp.arange(tmm, dtype=I32)[None, :]
    te_onehot = te[:, None] == jnp.arange(N_EXPERTS, dtype=I32)[None, :]
    per_tile = lambda v: jnp.sum(jnp.where(te_onehot, v[None, :], 0), axis=1, keepdims=True)
    used = (pos - per_tile(seg_start) < per_tile(counts)) & (tile_valid[:, None] == 1)
    src = (jnp.where(used, jnp.take(order, jnp.clip(pos - per_tile(shift), 0, n - 1)), pos) % t_tok).reshape(-1)
    assert h2.shape[0] >= nt * tmm
    xs = jnp.take(h2, src, axis=0, mode="clip")
    ys = _gmm(xs, w_gate, w_up, w_down, layer, te, tile_first, tile_valid, tmm)
    return jnp.take(ys, dest, axis=0, mode="clip")


def _moe_combine(x, y1, y2, gates, m5):
    g = gates
    f = g[:, 0:1] * y1.astype(F32) + g[:, 1:2] * y2.astype(F32)
    return x + m5 * f


def _rope(x, cos, sin):
    n = x.shape[1]
    lane = lax.broadcasted_iota(I32, x.shape, 1)
    sw = jnp.where(lane % 32 < 16, pltpu.roll(x, n - 16, 1), pltpu.roll(x, 16, 1))
    reps = n // 128
    if reps > 1:
        cos = jnp.concatenate([cos] * reps, axis=1)
        sin = jnp.concatenate([sin] * reps, axis=1)
    return x * cos + sw * sin


def _inproj1_kernel(x_ref, y1_ref, y2_ref, ga_ref, m0_ref, m1_ref, g_ref, w_ref, cos_ref, sin_ref,
                    x1_ref, cq_ref, ckt_ref, cv_ref, dq_ref, dk_ref, dv_ref, *, tm, bs):
    m = m1_ref[0]
    in_lat = pl.program_id(0) * tm < bs
    scale = HEAD_DIM ** -0.5 * LOG2E
    for rows in (slice(0, tm // 2), slice(tm // 2, tm)):
        x1 = _moe_combine(x_ref[rows], y1_ref[rows], y2_ref[rows], ga_ref[rows], m0_ref[0][5:6])
        x1_ref[rows] = x1
        h = (_rms(x1, g_ref[...]) * (1.0 + m[1:2]) + m[0:1]).astype(BF16)
        cos, sin = cos_ref[rows], sin_ref[rows]
        cq = _dot(h, w_ref[:, 0:512])
        cq_ref[rows] = (jnp.where(in_lat, _rope(cq, cos, sin), cq) * scale).astype(BF16)
        ck = _dot(h, w_ref[:, 512:640])
        ckt_ref[:, rows] = jnp.where(in_lat, _rope(ck, cos, sin), ck).T.astype(BF16)
        cv_ref[rows] = _dot(h, w_ref[:, 640:768]).astype(BF16)
        dq_ref[rows] = (_dot(h, w_ref[:, 768:1280]) * scale).astype(BF16)
        dk_ref[rows] = _dot(h, w_ref[:, 1280:1792]).astype(BF16)
        dv_ref[rows] = _dot(h, w_ref[:, 1792:2304]).astype(BF16)


def _inproj1(x, y, gates, mods0, mods1, g, w, cos, sin, tm, bs, s, nb):
    r, d = x.shape
    second = pl.BlockSpec((tm, d), lambda i: (i + r // tm, 0))
    kern = functools.partial(_inproj1_kernel, tm=tm, bs=bs)
    mi = functools.partial(_mod_index, tm=tm, bs=bs, s=s, nb=nb)
    row = lambda wd: pl.BlockSpec((tm, wd), lambda i: (i, 0))
    modspec = pl.BlockSpec((1, N_MOD, d), lambda i: (mi(i), 0, 0))
    tab = pl.BlockSpec((tm, 128), lambda i: (jnp.where(i * tm < bs, (i * tm % s) // tm, 0), 0))
    shp = lambda wd, dt: jax.ShapeDtypeStruct((r, wd), dt)
    return pl.pallas_call(
        kern,
        grid=(r // tm,),
        in_specs=[row(d), row(d), second, row(128), modspec, modspec,
                  pl.BlockSpec((1, d), lambda i: (0, 0)), pl.BlockSpec(w.shape, lambda i: (0, 0)), tab, tab],
        out_specs=[row(d), row(512), pl.BlockSpec((128, tm), lambda i: (0, i)), row(128), row(512), row(512),
                   row(512)],
        out_shape=[shp(d, F32), shp(512, BF16), jax.ShapeDtypeStruct((128, r), BF16), shp(128, BF16),
                   shp(512, BF16), shp(512, BF16), shp(512, BF16)],
        compiler_params=_cp(("parallel",)),
        name="inproj1",
    )(x, y, y, gates, mods0, mods1, g, w, cos, sin)


def _swa_kernel(q_ref, ktp_ref, kto_ref, ktn_ref, ktx_ref, vp_ref, vo_ref, vn_ref, vx_ref, sink_ref, o_ref,
                *, nblk, cl):
    pair = pl.program_id(1)
    wb = SWA_BLOCK
    nloc = 3 * wb
    kto, vo, ktx, vx = kto_ref[...], vo_ref[...], ktx_ref[...], vx_ref[...]
    kts = (jnp.concatenate([ktp_ref[...], kto, ktx], axis=1), jnp.concatenate([kto, ktn_ref[...], ktx], axis=1))
    vvs = (jnp.concatenate([vp_ref[...], vo, vx], axis=0), jnp.concatenate([vo, vn_ref[...], vx], axis=0))
    a_i = lax.broadcasted_iota(I32, (2 * wb, nloc), 0) % wb
    c_i = lax.broadcasted_iota(I32, (2 * wb, nloc), 1)
    band = (c_i >= a_i) & (c_i <= a_i + 2 * SWA_WINDOW)
    oks = []
    for u in range(2):
        i = 2 * pair + u
        lo = jnp.where(i > 0, 0, wb)
        hi = jnp.where(i < nblk - 1, 3 * wb, 2 * wb)
        oks.append(band & (c_i >= lo) & (c_i < hi))
    half = lax.broadcasted_iota(I32, (1, 128), 1) // HEAD_DIM
    zero = jnp.zeros((), BF16)
    sink = sink_ref[...]
    items = [(u, g) for u in range(2) for g in range(4)]

    def scores(u, g):
        q2 = q_ref[wb * u:wb * u + wb, 128 * g:128 * g + 128]
        qst = jnp.concatenate([jnp.where(half == 0, q2, zero), jnp.where(half == 1, q2, zero)], axis=0)
        s_all = _dot(qst, kts[u])
        return jnp.concatenate([jnp.where(oks[u], s_all[:, 0:nloc], NEG), s_all[:, nloc:]], axis=1)

    def softmax(g, sc):
        sk = jnp.concatenate([jnp.broadcast_to(sink[0:1, 2 * g + a:2 * g + a + 1], (wb, 1)) for a in range(2)],
                             axis=0)
        m = jnp.maximum(jnp.max(sc, axis=-1, keepdims=True), sk)
        p = jnp.exp2(sc - m)
        return p.astype(BF16), jnp.sum(p, axis=-1, keepdims=True) + jnp.exp2(sk - m)

    def output(u, g, p, den):
        ost = _dot(p, vvs[u]) / den
        o_ref[wb * u:wb * u + wb, 128 * g:128 * g + 128] = jnp.where(half == 0, ost[0:wb], ost[wb:2 * wb]).astype(BF16)

    n = len(items)
    sc, pr = {}, {}
    for step in range(n + 2):
        if step < n:
            sc[step] = scores(*items[step])
        if 1 <= step < n + 1:
            pr[step - 1] = softmax(items[step - 1][1], sc.pop(step - 1))
        if step >= 2:
            output(*items[step - 2], *pr.pop(step - 2))


def _swa(cq, ckt, cv, sink_row, nb, s, cl, bs):
    wb = SWA_BLOCK
    nblk = s // wb
    assert nblk % 2 == 0
    npair = nblk // 2
    kern = functools.partial(_swa_kernel, nblk=nblk, cl=cl)
    prev = lambda b, j: b * nblk + jnp.maximum(2 * j - 1, 0)
    nxt = lambda b, j: b * nblk + jnp.minimum(2 * j + 2, nblk - 1)
    own = lambda b, j: b * npair + j
    return pl.pallas_call(
        kern,
        grid=(nb, npair),
        in_specs=[pl.BlockSpec((2 * wb, 512), lambda b, j: (own(b, j), 0)),
                  pl.BlockSpec((128, wb), lambda b, j: (0, prev(b, j))),
                  pl.BlockSpec((128, 2 * wb), lambda b, j: (0, own(b, j))),
                  pl.BlockSpec((128, wb), lambda b, j: (0, nxt(b, j))),
                  pl.BlockSpec((128, cl), lambda b, j: (0, bs // cl + b)),
                  pl.BlockSpec((wb, 128), lambda b, j: (prev(b, j), 0)),
                  pl.BlockSpec((2 * wb, 128), lambda b, j: (own(b, j), 0)),
                  pl.BlockSpec((wb, 128), lambda b, j: (nxt(b, j), 0)),
                  pl.BlockSpec((cl, 128), lambda b, j: (bs // cl + b, 0)),
                  pl.BlockSpec((1, 128), lambda b, j: (0, 0))],
        out_specs=pl.BlockSpec((2 * wb, 512), lambda b, j: (own(b, j), 0)),
        out_shape=jax.ShapeDtypeStruct((bs, 512), BF16),
        compiler_params=_cp(("parallel", "parallel")),
        name="swa",
    )(cq, ckt, ckt, ckt, ckt, cv, cv, cv, cv, sink_row)


def _na_kernel(q_ref, k_ref, v_ref, kx_ref, vx_ref, bias_ref, o_ref, *, rows, unroll):
    half = lax.broadcasted_iota(I32, (1, 128), 1) // HEAD_DIM
    zero = jnp.zeros((), BF16)
    kx, vx = kx_ref[...], vx_ref[...]
    span = NA_KH * GRID_W

    def scores(r):
        rs = jnp.clip(r - NA_KH // 2, 0, rows - NA_KH)
        off = rs - r + NA_KH - 1
        q0 = pl.multiple_of(r * GRID_W, GRID_W)
        k0 = pl.multiple_of(rs * GRID_W, GRID_W)
        q2 = q_ref[pl.ds(q0, GRID_W), :]
        qst = jnp.concatenate([jnp.where(half == 0, q2, zero), jnp.where(half == 1, q2, zero)], axis=0)
        s_loc = (_dot_nt(qst, k_ref[pl.ds(k0, span), :])
                 + jnp.concatenate([bias_ref[0, off], bias_ref[1, off]], axis=0))
        return q0, k0, s_loc, _dot_nt(qst, kx)

    def softmax(s_loc, s_ctx):
        m = jnp.maximum(jnp.max(s_loc, axis=-1, keepdims=True), jnp.max(s_ctx, axis=-1, keepdims=True))
        p_loc = jnp.exp2(s_loc - m)
        p_ctx = jnp.exp2(s_ctx - m)
        den = jnp.sum(p_loc, axis=-1, keepdims=True) + jnp.sum(p_ctx, axis=-1, keepdims=True)
        return p_loc.astype(BF16), p_ctx.astype(BF16), den

    def body(i, carry):
        sc = [scores(i * unroll + j) for j in range(unroll)]
        pr = [softmax(s_loc, s_ctx) for (_, _, s_loc, s_ctx) in sc]
        for (q0, k0, _, _), (p_loc, p_ctx, den) in zip(sc, pr):
            ost = (_dot(p_loc, v_ref[pl.ds(k0, span), :]) + _dot(p_ctx, vx)) / den
            o = jnp.where(half == 0, ost[0:GRID_W], ost[GRID_W:2 * GRID_W])
            o_ref[pl.ds(q0, GRID_W), :] = o.astype(BF16)
        return carry

    lax.fori_loop(0, rows // unroll, body, 0)


def _na(dq, dk, dv, bias, nb, s, cl, bs):
    rows = s // GRID_W
    unroll = 4
    assert rows % unroll == 0
    kern = functools.partial(_na_kernel, rows=rows, unroll=unroll)
    seq = pl.BlockSpec((s, 128), lambda b, p: (b, p))
    ctx = pl.BlockSpec((cl, 128), lambda b, p: (bs // cl + b, p))
    return pl.pallas_call(
        kern,
        grid=(nb, NA_HEADS // 2),
        in_specs=[seq, seq, seq, ctx, ctx,
                  pl.BlockSpec((2, NA_KH, GRID_W, NA_KH * GRID_W), lambda b, p: (p, 0, 0, 0))],
        out_specs=seq,
        out_shape=jax.ShapeDtypeStruct((bs, 512), BF16),
        compiler_params=_cp(("parallel", "parallel")),
        name="na",
    )(dq, dk, dv, dk, dv, bias)


def _na_bias_table(rpb):
    c = np.arange(GRID_W)
    qs = np.clip(c - NA_KW // 2, 0, GRID_W - NA_KW)
    kc = np.arange(GRID_W)
    ok = (kc[None, :] >= qs[:, None]) & (kc[None, :] < qs[:, None] + NA_KW)
    dc = np.clip(kc[None, :] - c[:, None] + NA_KW - 1, 0, 2 * NA_KW - 2)
    sel = (np.arange(2 * NA_KW - 1)[:, None, None] == dc[None]).astype(np.float32)
    cols = jnp.einsum("hab,bck->hcak", rpb.astype(F32), sel, precision=HI)
    cols = jnp.where(ok[None, :, None, :], cols * LOG2E, NEG)
    return jnp.stack([cols[:, :, off:off + NA_KH].reshape(NA_HEADS, GRID_W, NA_KH * GRID_W)
                      for off in range(NA_KH)], axis=1)


def _mix1_kernel(x_ref, oc_ref, od_ref, mod_ref, wo_ref, g2_ref, rw_ref, rb_ref, h2buf_ref,
                 xn_ref, h2_ref, ei_ref, ga_ref):
    del h2buf_ref
    y = _dot(oc_ref[...], wo_ref[0:512, :]) + _dot(od_ref[...], wo_ref[512:1024, :])
    xn, h2, eidx, gates = _post_mixer(x_ref[...], y, mod_ref[0], g2_ref[...], rw_ref[...], rb_ref[...])
    xn_ref[...] = xn
    h2_ref[...] = h2.astype(BF16)
    ei_ref[...] = eidx
    ga_ref[...] = gates


def _mix1(x, oc, od, mods, wo, g2, rw, rb, h2buf, tm, bs, s):
    d = x.shape[1]
    row = lambda w: pl.BlockSpec((tm, w), lambda i: (i, 0))
    full = lambda a: pl.BlockSpec(a.shape, lambda i: (0,) * a.ndim)
    return pl.pallas_call(
        _mix1_kernel,
        grid=(bs // tm,),
        in_specs=[row(d), row(512), row(512), pl.BlockSpec((1, N_MOD, d), lambda i: (i * tm // s, 0, 0)),
                  full(wo), full(g2), full(rw), full(rb), pl.BlockSpec(memory_space=pl.ANY)],
        out_specs=[row(d), row(d), pl.BlockSpec((8, tm), lambda i: (0, i)), row(128)],
        out_shape=[jax.ShapeDtypeStruct((bs, d), F32), jax.ShapeDtypeStruct(h2buf.shape, BF16),
                   jax.ShapeDtypeStruct((8, bs), I32), jax.ShapeDtypeStruct((bs, 128), F32)],
        input_output_aliases={8: 1},
        compiler_params=_cp(("parallel",)),
        name="mix1",
    )(x, oc, od, mods, wo, g2, rw, rb, h2buf)


def _final_kernel(x_ref, y1_ref, y2_ref, ga_ref, mod_ref, g_ref, o_ref):
    x = _moe_combine(x_ref[...], y1_ref[...], y2_ref[...], ga_ref[...], mod_ref[0][5:6])
    o_ref[...] = _rms(x, g_ref[...])


def _final(x, y, gates, mods, g, tm, s):
    r, d = x.shape
    second = pl.BlockSpec((tm, d), lambda i: (i + r // tm, 0))
    row = lambda w: pl.BlockSpec((tm, w), lambda i: (i, 0))
    return pl.pallas_call(
        _final_kernel,
        grid=(r // tm,),
        in_specs=[row(d), row(d), second, row(128), pl.BlockSpec((1, N_MOD, d), lambda i: (i * tm // s, 0, 0)),
                  pl.BlockSpec((1, d), lambda i: (0, 0))],
        out_specs=row(d),
        out_shape=jax.ShapeDtypeStruct((r, d), F32),
        compiler_params=_cp(("parallel",)),
        name="final",
    )(x, y, y, gates, mods, g)


def _rope_tables(s):
    nf = HEAD_DIM // 4
    t = np.arange(s)
    inv = ROPE_THETA ** (-np.arange(nf, dtype=np.float64) / nf)
    ar = (t // GRID_W)[:, None] * inv
    ac = (t % GRID_W)[:, None] * inv
    cos = np.concatenate([np.cos(ar), np.cos(ar), np.cos(ac), np.cos(ac)], axis=1)
    sin = np.concatenate([-np.sin(ar), np.sin(ar), -np.sin(ac), np.sin(ac)], axis=1)
    return (jnp.asarray(np.concatenate([cos, cos], axis=1), F32),
            jnp.asarray(np.concatenate([sin, sin], axis=1), F32))


def kernel(x, c, ctx, c_ctx, ada_w, ada_b, norm1_g, norm2_g, ev_w_in, ev_w_out, sc_conv_w, dn_conv_w, dn_a_log, dn_dt_bias, dn_onorm_g, od_w_in, od_w_out, swa_sink, na_rpb, router_w, router_b, moe_w_gate, moe_w_up, moe_w_down, final_g):
    nb, s, d = x.shape
    cl = ctx.shape[1]
    bs = nb * s
    tm = 512
    ts = 256
    tmm = 512
    assert d == 1024 and s % tm == 0 and (nb * cl) % tm == 0 and cl % ts == 0 and s % ts == 0
    assert s // GRID_W >= NA_KH and bs % cl == 0 and nb + 1 <= 8

    xl, xc = x.reshape(bs, d), ctx.reshape(nb * cl, d)
    cc = jnp.zeros((8, d), F32).at[:nb].set(c).at[nb].set(c_ctx)
    mods = _ada(cc, ada_w, ada_b).reshape(ada_w.shape[0], 8, N_MOD, d)
    rw32 = jnp.pad(router_w, ((0, 0), (0, 128 - N_EXPERTS)))
    rw_hi = rw32.astype(BF16)
    rw = jnp.concatenate([rw_hi, (rw32 - rw_hi.astype(F32)).astype(BF16)], axis=1)
    rb = router_b.reshape(N_EXPERTS, 1)
    row = lambda v: v.reshape(1, -1)

    w_in0 = jnp.pad(ev_w_in[0], ((0, 0), (0, 3712 - ev_w_in.shape[-1]))).astype(BF16)
    sc, qkv, z, bg = _inproj0(xl, xc, mods[0], row(norm1_g[0]), w_in0, tm, bs, s, nb)
    pad16 = lambda v: jnp.pad(v.reshape(-1), (8, 128 - 16)).reshape(1, 128)
    uf, ub, wf, wb, qf, qb, kf, kb, af, ab, gc = _dnchunk(qkv, bg, dn_conv_w[0], pad16(dn_a_log[0]),
                                                          pad16(dn_dt_bias[0]), ts, bs, s, cl)
    o_pairs = _dnscan(uf, ub, wf, wb, qf, qb, kf, kb, af, ab, gc, nb, s, cl, bs)
    r_all = bs + nb * cl
    moe_rows = lambda t: (2 * t // tmm + N_EXPERTS) * tmm
    x0, h2, ei, ga = _mix0(xl, xc, sc, o_pairs, z, mods[0], sc_conv_w[0], row(dn_onorm_g[0]),
                           ev_w_out[0].astype(BF16), row(norm2_g[0]), rw, rb,
                           jnp.zeros((moe_rows(r_all), d), BF16), ts, bs, s, cl, nb)
    y = _moe(h2, r_all, ei[0], ei[1], moe_w_gate, moe_w_up, moe_w_down, 0, tmm)

    perm = np.concatenate([np.arange(HEAD_DIM) + HEAD_DIM * (g + 4 * a) for g in range(4) for a in range(2)])
    w1 = od_w_in[0]
    w_in1 = jnp.concatenate([w1[:, 0:512][:, perm], w1[:, 512:]], axis=1).astype(BF16)
    wo1 = od_w_out[0]
    w_out1 = jnp.concatenate([wo1[0:512][perm], wo1[512:]], axis=0).astype(BF16)
    sink_row = jnp.pad(swa_sink[0][np.array([g + 4 * a for g in range(4) for a in range(2)])] * LOG2E,
                       (0, 128 - SWA_HEADS)).reshape(1, 128)
    cos, sin = _rope_tables(s)
    x1, cq, ckt, cv, dq, dk, dv = _inproj1(x0, y, ga, mods[0], mods[1], row(norm1_g[1]), w_in1, cos, sin,
                                           tm, bs, s, nb)
    oc = _swa(cq, ckt, cv, sink_row, nb, s, cl, bs)
    od = _na(dq, dk, dv, _na_bias_table(na_rpb[0]), nb, s, cl, bs)
    assert moe_rows(bs) <= h2.shape[0]
    x2, h2, ei, ga = _mix1(x1, oc, od, mods[1], w_out1, row(norm2_g[1]), rw, rb, h2, tm, bs, s)
    y = _moe(h2, bs, ei[0], ei[1], moe_w_gate, moe_w_up, moe_w_down, 1, tmm)
    out = _final(x2, y, ga, mods[1], row(final_g), tm, s)
    return out.reshape(nb, s, d)
```

```python
import functools
import math

import numpy as np
import jax
import jax.numpy as jnp
from jax import lax
from jax.experimental import pallas as pl
from jax.experimental.pallas import tpu as pltpu

F32 = jnp.float32
BF16 = jnp.bfloat16
I32 = jnp.int32
HI = lax.Precision.HIGHEST

EPS = 1e-6
N_MOD = 6
GRID_W = 64
HEAD_DIM = 64
DN_HEADS = 4
DN_HD = 128
DN_CHUNK = 64
SWA_HEADS = 8
SWA_KV = 2
SWA_BLOCK = 128
SWA_WINDOW = 128
NA_HEADS = 8
NA_KH = 8
NA_KW = 16
ROPE_THETA = 10000.0
N_EXPERTS = 16
N_GROUPS = 4
NEG = -1e30
LOG2E = 1.4426950408889634
VMEM_LIMIT = 56 * 1024 * 1024


def _cp(sem, vmem=VMEM_LIMIT):
    return pltpu.CompilerParams(dimension_semantics=sem, vmem_limit_bytes=vmem)


def _dot(a, b, precision=None):
    return jnp.dot(a, b, preferred_element_type=F32, precision=precision)


def _dot_nt(a, b, precision=None):
    return lax.dot_general(a, b, (((1,), (1,)), ((), ())), preferred_element_type=F32, precision=precision)


def _dot_tn(a, b, precision=None):
    return lax.dot_general(a, b, (((0,), (0,)), ((), ())), preferred_element_type=F32, precision=precision)


def _silu(x):
    return x * jax.nn.sigmoid(x)


def _rms(x, g):
    return x * lax.rsqrt(jnp.mean(x * x, axis=-1, keepdims=True) + EPS) * g


def _ada_kernel(cc_ref, w_ref, b_ref, o_ref):
    a = _silu(cc_ref[...])
    o_ref[0] = _dot(a, w_ref[0], HI) + b_ref[0]


def _ada(cc, ada_w, ada_b):
    depth, d, n = ada_w.shape
    tn = 1536
    return pl.pallas_call(
        _ada_kernel,
        grid=(depth, n // tn),
        in_specs=[pl.BlockSpec((8, d), lambda l, j: (0, 0)),
                  pl.BlockSpec((1, d, tn), lambda l, j: (l, 0, j)),
                  pl.BlockSpec((1, 1, tn), lambda l, j: (l, 0, j))],
        out_specs=pl.BlockSpec((1, 8, tn), lambda l, j: (l, 0, j)),
        out_shape=jax.ShapeDtypeStruct((depth, 8, n), F32),
        compiler_params=_cp(("parallel", "parallel")),
        name="ada",
    )(cc, ada_w, ada_b.reshape(depth, 1, n))


def _mod_index(i, tm, bs, s, nb):
    row0 = i * tm
    return jnp.where(row0 < bs, row0 // s, nb)


def _token_specs(tm, bs, d):
    nlat = bs // tm
    return [pl.BlockSpec((tm, d), lambda i: (jnp.minimum(i, nlat - 1), 0)),
            pl.BlockSpec((tm, d), lambda i: (jnp.maximum(i - nlat, 0), 0))]


def _token_rows(xl_ref, xc_ref, tm, bs):
    return jnp.where(pl.program_id(0) * tm < bs, xl_ref[...], xc_ref[...])


def _inproj0_kernel(xl_ref, xc_ref, mod_ref, g_ref, w_ref, sc_ref, qkv_ref, z_ref, bg_ref, *, tm, bs):
    m = mod_ref[0]
    h = (_rms(_token_rows(xl_ref, xc_ref, tm, bs), g_ref[...]) * (1.0 + m[1:2]) + m[0:1]).astype(BF16)
    sc_ref[...] = _dot(h, w_ref[:, 0:1536]).astype(BF16)
    qkv_ref[...] = _dot(h, w_ref[:, 1536:3072]).astype(BF16)
    z_ref[...] = _dot(h, w_ref[:, 3072:3584]).astype(BF16)
    bg_ref[...] = _dot(h, w_ref[:, 3584:3712])


def _inproj0(xl, xc, mods, g, w, tm, bs, s, nb):
    d = xl.shape[1]
    r = bs + xc.shape[0]
    mi = functools.partial(_mod_index, tm=tm, bs=bs, s=s, nb=nb)
    return pl.pallas_call(
        functools.partial(_inproj0_kernel, tm=tm, bs=bs),
        grid=(r // tm,),
        in_specs=_token_specs(tm, bs, d) + [
            pl.BlockSpec((1, N_MOD, d), lambda i: (mi(i), 0, 0)),
            pl.BlockSpec((1, d), lambda i: (0, 0)),
            pl.BlockSpec(w.shape, lambda i: (0, 0))],
        out_specs=[pl.BlockSpec((tm, 1536), lambda i: (i, 0)),
                   pl.BlockSpec((tm, 1536), lambda i: (i, 0)),
                   pl.BlockSpec((tm, 512), lambda i: (i, 0)),
                   pl.BlockSpec((tm, 128), lambda i: (i, 0))],
        out_shape=[jax.ShapeDtypeStruct((r, 1536), BF16), jax.ShapeDtypeStruct((r, 1536), BF16),
                   jax.ShapeDtypeStruct((r, 512), BF16), jax.ShapeDtypeStruct((r, 128), F32)],
        compiler_params=_cp(("parallel",)),
        name="inproj0",
    )(xl, xc, mods, g, w)


def _seq_edges(i, ts, bs, s, cl):
    row0 = i * ts
    in_lat = row0 < bs
    r_in = jnp.where(in_lat, row0 % s, (row0 - bs) % cl)
    seqlen = jnp.where(in_lat, s, cl)
    return r_in == 0, r_in + ts == seqlen


def _shifted(x, prev_row, next_row):
    n = x.shape[0]
    rows = lax.broadcasted_iota(I32, x.shape, 0)
    xp = jnp.where(rows == 0, prev_row, pltpu.roll(x, 1, 0))
    xn = jnp.where(rows == n - 1, next_row, pltpu.roll(x, n - 1, 0))
    return xp, xn


HALO = 16


def _halo_specs(ts, width, r):
    nblk = r // HALO
    k = ts // HALO
    return [pl.BlockSpec((HALO, width), lambda i: (jnp.maximum(i * k - 1, 0), 0)),
            pl.BlockSpec((HALO, width), lambda i: (jnp.minimum((i + 1) * k, nblk - 1), 0))]


def _dnprep_tile(x_ref, prev_ref, next_ref, bg_ref, cw_ref, alog_ref, dt_ref,
                 q_ref, k_ref, v_ref, bga_ref, *, ts, bs, s, cl):
    first, last = _seq_edges(pl.program_id(0), ts, bs, s, cl)
    for c in range(12):
        sl = slice(128 * c, 128 * c + 128)
        x = x_ref[:, sl].astype(F32)
        pr = jnp.where(first, 0.0, prev_ref[HALO - 1:HALO, sl].astype(F32))
        nx = jnp.where(last, 0.0, next_ref[0:1, sl].astype(F32))
        xp, xn = _shifted(x, pr, nx)
        w = cw_ref[:, sl]
        y = _silu(xp * w[0:1] + x * w[1:2] + xn * w[2:3])
        hs = slice(128 * (c % 4), 128 * (c % 4) + 128)
        if c < 8:
            y = y * lax.rsqrt(jnp.sum(y * y, axis=-1, keepdims=True) + EPS)
        if c < 4:
            q_ref[:, hs] = y * DN_HD ** -0.5
        elif c < 8:
            k_ref[:, hs] = y
        else:
            v_ref[:, hs] = y
    b = bg_ref[...]
    cols = lax.broadcasted_iota(I32, b.shape, 1)
    beta = jax.nn.sigmoid(b)
    t = b + dt_ref[...]
    softplus = jnp.maximum(t, 0.0) + jnp.log1p(jnp.exp(-jnp.abs(t)))
    g = -jnp.exp(alog_ref[...]) * softplus
    bga_ref[...] = jnp.where(cols < 8, beta, jnp.where(cols < 16, g, 0.0))


def _dnchunk_kernel(x_ref, prev_ref, next_ref, bgraw_ref, cw_ref, alog_ref, dt_ref,
                    uf_ref, ub_ref, wf_ref, wb_ref, qf_ref, qb_ref, kf_ref, kb_ref, af_ref, ab_ref, gc_ref,
                    q_ref, k_ref, v_ref, bg_ref, *, nchunks, bs, s, cl):
    _dnprep_tile(x_ref, prev_ref, next_ref, bgraw_ref, cw_ref, alog_ref, dt_ref, q_ref, k_ref, v_ref, bg_ref,
                 ts=nchunks * DN_CHUNK, bs=bs, s=s, cl=cl)
    outs = ((uf_ref, wf_ref, qf_ref, kf_ref, af_ref), (ub_ref, wb_ref, qb_ref, kb_ref, ab_ref))
    c, nh = DN_CHUNK, DN_HEADS
    head_of_col = lax.broadcasted_iota(I32, (1, nh * c), 1) // c
    zero = jnp.zeros((), BF16)

    def block_diag(x):
        return jnp.concatenate([jnp.where(head_of_col == h, x, zero) for h in range(nh)], axis=0)

    chains = []
    for cc in range(nchunks):
        chains += _dnchunk_setup(slice(cc * c, (cc + 1) * c), q_ref, k_ref, v_ref, bg_ref, gc_ref)
    for ch in chains:
        ch["tm"] = ch["nmat"]
        nb16 = ch["nmat"].astype(BF16)
        ch["npow"] = _dot(nb16, block_diag(nb16))
    for _ in range(4):
        for ch in chains:
            nb16 = ch["npow"].astype(BF16)
            ch["both"] = _dot(jnp.concatenate([nb16, ch["tm"].astype(BF16)], axis=0), block_diag(nb16))
        for ch in chains:
            ch["tm"] = ch["tm"] + ch["npow"] + ch["both"][c:2 * c]
            ch["npow"] = ch["both"][0:c]
    for ch in chains:
        ch["both"] = _dot(ch["tm"].astype(BF16), block_diag(ch["npow"].astype(BF16)))
    for ch in chains:
        ch["tm"] = ch["tm"] + ch["npow"] + ch["both"]
    for ch in chains:
        ch["uw"] = ch["rhs"] + _dot(block_diag(ch["tm"].astype(BF16)), ch["rhs"].astype(BF16))
    for ch in chains:
        u_ref, w_ref, qd_ref, kd_ref, at_ref = outs[ch["d"]]
        rows, uw = ch["rows"], ch["uw"]
        for h in range(nh):
            hs = slice(DN_HD * h, DN_HD * h + DN_HD)
            rs = slice(c * h, c * h + c)
            u_ref[rows, hs] = uw[rs, 0:DN_HD]
            w_ref[rows, hs] = uw[rs, DN_HD:2 * DN_HD].astype(BF16)
            qd_ref[rows, hs] = ch["qd"][rs]
        kd_ref[slice(2 * rows.start, 2 * rows.stop)] = ch["kd"].T.astype(BF16)
        at_ref[rows] = ch["att"].astype(BF16)


def _dnchunk_setup(rows, q_ref, k_ref, v_ref, bg_ref, gc_ref):
    c, nh = DN_CHUNK, DN_HEADS
    n = c * nh
    bg = bg_ref[rows]
    i64 = lax.broadcasted_iota(I32, (c, c), 0)
    j64 = lax.broadcasted_iota(I32, (c, c), 1)
    cols = lax.broadcasted_iota(I32, bg.shape, 1)
    gcf = _dot((i64 >= j64).astype(F32), bg, HI)
    gcb = _dot((i64 <= j64).astype(F32), bg, HI)
    gc = jnp.where(cols >= 12, gcb, gcf)
    gc_ref[rows] = gc
    gct = gc.T
    ii = lax.broadcasted_iota(I32, (c, n), 0)
    jj = lax.broadcasted_iota(I32, (c, n), 1)
    head_of_col = jj // c
    jj = jj % c

    def stack(ref):
        return jnp.concatenate([ref[rows, DN_HD * h:DN_HD * h + DN_HD] for h in range(nh)], axis=0)

    def stacked_cols(arr, r0, r1, col0):
        return jnp.concatenate([jnp.broadcast_to(arr[r0:r1, col0 + h:col0 + h + 1], (c, DN_HD)) for h in range(nh)],
                               axis=0)

    def side_by_side_cols(arr, col0):
        out = arr[:, col0 + nh - 1:col0 + nh]
        for h in range(nh - 2, -1, -1):
            out = jnp.where(head_of_col == h, arr[:, col0 + h:col0 + h + 1], out)
        return out

    def diag_blocks(x):
        out = x[(nh - 1) * c:nh * c]
        for h in range(nh - 2, -1, -1):
            out = jnp.where(head_of_col == h, x[h * c:(h + 1) * c], out)
        return out

    kst, qst, vst = stack(k_ref), stack(q_ref), stack(v_ref)
    kb = kst.astype(BF16)
    kq = _dot_nt(jnp.concatenate([kb, qst.astype(BF16)], axis=0), kb)
    kkt, qkt = diag_blocks(kq[0:n]), diag_blocks(kq[n:2 * n])
    chains = []
    for d in range(2):
        incl = (ii >= jj) if d == 0 else (ii <= jj)
        strict = (ii > jj) if d == 0 else (ii < jj)
        last = c - 1 if d == 0 else 0
        grow = jnp.concatenate([gct[8 + 4 * d + h:9 + 4 * d + h, :] for h in range(nh)], axis=1)
        decay = jnp.exp(jnp.where(incl, side_by_side_cols(gc, 8 + 4 * d) - grow, NEG))
        nmat = jnp.where(strict, -(side_by_side_cols(bg, 4 * d) * kkt * decay), 0.0)
        b1 = stacked_cols(bg, 0, c, 4 * d)
        gcol = stacked_cols(gc, 0, c, 8 + 4 * d)
        glast = stacked_cols(gc, last, last + 1, 8 + 4 * d)
        e1 = jnp.exp(gcol)
        chains.append(dict(
            d=d, rows=rows, nmat=nmat,
            rhs=jnp.concatenate([b1 * vst, (b1 * e1) * kst], axis=1),
            qd=(qst * e1).astype(BF16),
            kd=kst * jnp.exp(glast - gcol),
            att=qkt * decay))
    return chains


def _dnchunk(qkv, bg, cw, alog_row, dt_row, ts, bs, s, cl):
    r = qkv.shape[0]
    nchunks = ts // DN_CHUNK
    c = ts
    row = lambda w: pl.BlockSpec((c, w), lambda i: (i, 0))
    shp = lambda w, dt: jax.ShapeDtypeStruct((r, w), dt)
    return pl.pallas_call(
        functools.partial(_dnchunk_kernel, nchunks=nchunks, bs=bs, s=s, cl=cl),
        grid=(r // c,),
        in_specs=[row(1536)] + _halo_specs(ts, 1536, r) + [
            row(128),
            pl.BlockSpec((3, 1536), lambda i: (0, 0)),
            pl.BlockSpec((1, 128), lambda i: (0, 0)),
            pl.BlockSpec((1, 128), lambda i: (0, 0))],
        scratch_shapes=[pltpu.VMEM((c, 512), F32)] * 3 + [pltpu.VMEM((c, 128), F32)],
        out_specs=[row(512)] * 6 + [pl.BlockSpec((2 * c, 256), lambda i: (i, 0))] * 2 + [row(256), row(256), row(128)],
        out_shape=([shp(512, F32)] * 2 + [shp(512, BF16)] * 4 + [jax.ShapeDtypeStruct((2 * r, 256), BF16)] * 2
                   + [shp(256, BF16)] * 2 + [shp(128, F32)]),
        compiler_params=_cp(("parallel",)),
        name="dnchunk",
    )(qkv, qkv, qkv, bg, cw, alog_row, dt_row)


def _dnscan_kernel(*refs, nsub, nb):
    nchain = 2 * nb
    ins, outs, s_ref = refs[:6 * nchain], refs[6 * nchain:7 * nchain], refs[7 * nchain]

    @pl.when(pl.program_id(0) == 0)
    def _():
        s_ref[...] = jnp.zeros_like(s_ref)

    c, nh = DN_CHUNK, DN_HEADS
    head_of_lane = lax.broadcasted_iota(I32, (1, nh * DN_HD), 1) // DN_HD
    head_of_col = lax.broadcasted_iota(I32, (1, nh * c), 1) // c
    zero = jnp.zeros((), BF16)

    def block_diag(tile, head_ids):
        return jnp.concatenate([jnp.where(head_ids == h, tile, zero) for h in range(nh)], axis=0)

    states = [s_ref[ci] for ci in range(nchain)]
    for sub in range(nsub):
        work = []
        for ci in range(nchain):
            d = ci % 2
            u_ref, w_ref, qd_ref, kd_ref, at_ref, g_ref = ins[6 * ci:6 * ci + 6]
            last = c - 1 if d == 0 else 0
            k = sub if d == 0 else nsub - 1 - sub
            rows = slice(c * k, c * k + c)
            g = g_ref[rows]
            decay = jnp.concatenate(
                [jnp.broadcast_to(jnp.exp(g[last:last + 1, 8 + 4 * d + h:9 + 4 * d + h]), (DN_HD, DN_HD))
                 for h in range(nh)], axis=0)
            ust = jnp.concatenate([u_ref[rows, DN_HD * h:DN_HD * h + DN_HD] for h in range(nh)], axis=0)
            stb = states[ci].astype(BF16)
            vnew = ust - _dot(block_diag(w_ref[rows], head_of_lane), stb)
            work.append((rows, decay, stb, vnew.astype(BF16)))
        for ci in range(nchain):
            u_ref, w_ref, qd_ref, kd_ref, at_ref, g_ref = ins[6 * ci:6 * ci + 6]
            rows, decay, stb, vnb = work[ci]
            o = (_dot(block_diag(qd_ref[rows], head_of_lane), stb)
                 + _dot(block_diag(at_ref[rows], head_of_col), vnb))
            kdt = kd_ref[slice(2 * rows.start, 2 * rows.stop)]
            states[ci] = states[ci] * decay + _dot(block_diag(kdt, head_of_col), vnb)
            for h in range(nh):
                outs[ci][rows, DN_HD * h:DN_HD * h + DN_HD] = o[c * h:c * h + c].astype(BF16)
    for ci in range(nchain):
        s_ref[ci] = states[ci]


def _dnscan(uf, ub, wf, wb, qf, qb, kf, kb, af, ab, gc, nb, s, cl, bs):
    nsub = 4
    c = DN_CHUNK * nsub
    assert cl % c == 0 and s % c == 0 and bs % c == 0
    ncc, ncl = cl // c, s // c
    ns = ncc + ncl

    def src_block(b, d):
        if d == 0:
            return lambda t: jnp.where(t < ncc, bs // c + b * ncc + t, b * ncl + t - ncc)
        return lambda t: jnp.where(t < ncc, bs // c + b * ncc + (ncc - 1 - t), b * ncl + (ncl - 1 - (t - ncc)))

    def dst_block(d):
        if d == 0:
            return lambda t: jnp.where(t < ncc, ncl + t, t - ncc)
        return lambda t: jnp.where(t < ncc, ncl + (ncc - 1 - t), ncl - 1 - (t - ncc))

    in_specs, args, out_specs = [], [], []
    for b in range(nb):
        for d, group in enumerate(((uf, wf, qf, kf, af, gc), (ub, wb, qb, kb, ab, gc))):
            idx = src_block(b, d)
            blk = lambda w, idx=idx: pl.BlockSpec((c, w), lambda t: (idx(t), 0))
            in_specs += [blk(512), blk(512), blk(512), pl.BlockSpec((2 * c, 256), lambda t, idx=idx: (idx(t), 0)),
                         blk(256), blk(128)]
            args += list(group)
            out_specs.append(pl.BlockSpec((c, 512), lambda t, f=dst_block(d): (f(t), 0)))
    outs = pl.pallas_call(
        functools.partial(_dnscan_kernel, nsub=nsub, nb=nb),
        grid=(ns,),
        in_specs=in_specs,
        out_specs=out_specs,
        out_shape=[jax.ShapeDtypeStruct((s + cl, 512), BF16)] * (2 * nb),
        scratch_shapes=[pltpu.VMEM((2 * nb, DN_HEADS * DN_HD, DN_HD), F32)],
        compiler_params=_cp(("arbitrary",)),
        name="dnscan",
    )(*args)
    return [(outs[2 * b], outs[2 * b + 1]) for b in range(nb)]


def _route(logits, bias_col):
    epg = N_EXPERTS // N_GROUPS
    tm = logits.shape[0]
    scores = jax.nn.sigmoid(logits.T[0:N_EXPERTS])
    gsel = scores + bias_col
    row = lambda a, k: a[k:k + 1]
    best = gidx = None
    for g in range(N_GROUPS):
        a = [row(gsel, epg * g + k) for k in range(epg)]
        m01, n01 = jnp.maximum(a[0], a[1]), jnp.minimum(a[0], a[1])
        m23, n23 = jnp.maximum(a[2], a[3]), jnp.minimum(a[2], a[3])
        gs = jnp.maximum(m01, m23) + jnp.maximum(jnp.minimum(m01, m23), jnp.maximum(n01, n23))
        if g == 0:
            best, gidx = gs, jnp.zeros_like(gs)
        else:
            better = gs > best
            best = jnp.where(better, gs, best)
            gidx = jnp.where(better, float(g), gidx)
    sel = [None] * epg
    raw = [None] * epg
    for g in range(N_GROUPS):
        for k in range(epg):
            v, u = row(gsel, epg * g + k), row(scores, epg * g + k)
            sel[k] = v if g == 0 else jnp.where(gidx == g, v, sel[k])
            raw[k] = u if g == 0 else jnp.where(gidx == g, u, raw[k])
    v1, e1, w1 = sel[0], jnp.zeros_like(gidx), raw[0]
    for k in range(1, epg):
        better = sel[k] > v1
        v1 = jnp.where(better, sel[k], v1)
        e1 = jnp.where(better, float(k), e1)
        w1 = jnp.where(better, raw[k], w1)
    v2 = e2 = w2 = None
    for k in range(epg):
        cand = jnp.where(e1 == k, -jnp.inf, sel[k])
        if k == 0:
            v2, e2, w2 = cand, jnp.zeros_like(gidx), raw[0]
        else:
            better = cand > v2
            v2 = jnp.where(better, cand, v2)
            e2 = jnp.where(better, float(k), e2)
            w2 = jnp.where(better, raw[k], w2)
    tot = w1 + w2
    eidx = jnp.concatenate([gidx * epg + e1, gidx * epg + e2, jnp.zeros((6, tm), F32)], axis=0).astype(I32)
    gates_t = jnp.concatenate([w1 / tot, w2 / tot, jnp.zeros((126, tm), F32)], axis=0)
    return eidx, gates_t.T


def _post_mixer(x, y, m, g2, rw, rb):
    xn = x + m[2:3] * y
    h2 = _rms(xn, g2) * (1.0 + m[4:5]) + m[3:4]
    hi = h2.astype(BF16)
    lo = (h2 - hi.astype(F32)).astype(BF16)
    hw = _dot(hi, rw)
    logits = hw[:, 0:128] + (hw[:, 128:256] + _dot(lo, rw[:, 0:128]))
    eidx, gates = _route(logits, rb)
    return xn, hi, eidx, gates


def _mix0_kernel(xl_ref, xc_ref, sc_ref, prev_ref, next_ref, z_ref, mod_ref, cw_ref, on_ref, wo_ref,
                 g2_ref, rw_ref, rb_ref, h2buf_ref, *rest, ts, bs, s, cl, nb):
    del h2buf_ref
    o_refs = rest[:2 * nb]
    xn_ref, h2_ref, ei_ref, ga_ref = rest[2 * nb:]
    first, last = _seq_edges(pl.program_id(0), ts, bs, s, cl)
    row0 = pl.program_id(0) * ts
    batch = jnp.where(row0 < bs, row0 // s, (row0 - bs) // cl)
    ya = []
    for c in range(4):
        sl = slice(128 * c, 128 * c + 128)
        sg = slice(512 + 128 * c, 512 + 128 * c + 128)
        sx = slice(1024 + 128 * c, 1024 + 128 * c + 128)
        f32 = lambda ref, rows, cols: ref[rows, cols].astype(F32)
        u = f32(sc_ref, slice(None), sg) * f32(sc_ref, slice(None), sx)
        pr = jnp.where(first, 0.0, f32(prev_ref, slice(HALO - 1, HALO), sg) * f32(prev_ref, slice(HALO - 1, HALO), sx))
        nx = jnp.where(last, 0.0, f32(next_ref, slice(0, 1), sg) * f32(next_ref, slice(0, 1), sx))
        up, un = _shifted(u, pr, nx)
        w = cw_ref[:, sl]
        ya.append((f32(sc_ref, slice(None), sl) * (up * w[0:1] + u * w[1:2] + un * w[2:3])).astype(BF16))
    yb = []
    for h in range(DN_HEADS):
        hs = slice(DN_HD * h, DN_HD * h + DN_HD)
        o = o_refs[2 * nb - 2][:, hs].astype(F32) + o_refs[2 * nb - 1][:, hs].astype(F32)
        for b in range(nb - 2, -1, -1):
            o = jnp.where(batch == b, o_refs[2 * b][:, hs].astype(F32) + o_refs[2 * b + 1][:, hs].astype(F32), o)
        yb.append((_rms(o, on_ref[...]) * _silu(z_ref[:, hs].astype(F32))).astype(BF16))
    ycat = jnp.concatenate(ya + yb, axis=1)
    y = _dot(ycat, wo_ref[...])
    xn, h2, eidx, gates = _post_mixer(_token_rows(xl_ref, xc_ref, ts, bs), y, mod_ref[0], g2_ref[...], rw_ref[...],
                                      rb_ref[...])
    xn_ref[...] = xn
    h2_ref[...] = h2.astype(BF16)
    ei_ref[...] = eidx
    ga_ref[...] = gates


def _mix0(xl, xc, sc, o_pairs, z, mods, cw, on, wo, g2, rw, rb, h2buf, ts, bs, s, cl, nb):
    d = xl.shape[1]
    r = bs + xc.shape[0]
    kern = functools.partial(_mix0_kernel, ts=ts, bs=bs, s=s, cl=cl, nb=nb)

    def o_spec(b):
        def index(i):
            row0 = i * ts
            in_lat = row0 < bs
            owner = jnp.where(in_lat, row0 // s, (row0 - bs) // cl)
            own = jnp.where(in_lat, (row0 % s) // ts, s // ts + ((row0 - bs) % cl) // ts)
            nxt = jnp.where(in_lat, 0, s // ts)
            prv = jnp.where(in_lat, s // ts - 1, (s + cl) // ts - 1)
            return jnp.where(owner == b, own, jnp.where(owner < b, nxt, prv)), 0
        return pl.BlockSpec((ts, 512), index)

    mi = functools.partial(_mod_index, tm=ts, bs=bs, s=s, nb=nb)
    row = lambda w: pl.BlockSpec((ts, w), lambda i: (i, 0))
    full = lambda a: pl.BlockSpec(a.shape, lambda i: (0,) * a.ndim)
    return pl.pallas_call(
        kern,
        grid=(r // ts,),
        in_specs=_token_specs(ts, bs, d) + [row(1536)] + _halo_specs(ts, 1536, r) + [
            row(512),
            pl.BlockSpec((1, N_MOD, d), lambda i: (mi(i), 0, 0)),
            full(cw), full(on), full(wo), full(g2), full(rw), full(rb), pl.BlockSpec(memory_space=pl.ANY)]
        + [o_spec(b) for b in range(nb) for _ in range(2)],
        out_specs=[row(d), row(d), pl.BlockSpec((8, ts), lambda i: (0, i)), row(128)],
        out_shape=[jax.ShapeDtypeStruct((r, d), F32), jax.ShapeDtypeStruct(h2buf.shape, BF16),
                   jax.ShapeDtypeStruct((8, r), I32), jax.ShapeDtypeStruct((r, 128), F32)],
        input_output_aliases={13: 1},
        compiler_params=_cp(("parallel",)),
        name="mix0",
    )(xl, xc, sc, sc, sc, z, mods, cw, on, wo, g2, rw, rb, h2buf, *[a for pair in o_pairs for a in pair])


def _gmm_kernel(te_ref, tf_ref, tv_ref, ne_ref, sl_ref, x_ref, wg_hbm, wu_hbm, wd_hbm, y_ref,
                wbuf, wgb, wub, wdb, sem, *, layer):
    t = pl.program_id(0)

    def weight_copies(e, slot):
        return [pltpu.make_async_copy(w.at[layer, e], wbuf.at[slot, k], sem.at[slot, k])
                for k, w in enumerate((wg_hbm, wu_hbm, wd_hbm))]

    @pl.when(t == 0)
    def _():
        for cp in weight_copies(te_ref[0], sl_ref[0]):
            cp.start()

    @pl.when(tf_ref[t] == 1)
    def _():
        slot = sl_ref[t]
        for cp in weight_copies(te_ref[t], slot):
            cp.wait()
        wgb[...] = wbuf[slot, 0].astype(BF16)
        wub[...] = wbuf[slot, 1].astype(BF16)
        wdb[...] = wbuf[slot, 2].astype(BF16)

        @pl.when(ne_ref[t] >= 0)
        def _():
            for cp in weight_copies(ne_ref[t], 1 - slot):
                cp.start()

    @pl.when(tv_ref[t] == 1)
    def _():
        x = x_ref[...]
        a = (_silu(_dot(x, wgb[...])) * _dot(x, wub[...])).astype(BF16)
        y_ref[...] = _dot(a, wdb[...]).astype(BF16)

    @pl.when(tv_ref[t] == 0)
    def _():
        y_ref[...] = jnp.zeros_like(y_ref)


def _gmm(xs, w_gate, w_up, w_down, layer, tile_expert, tile_first, tile_valid, next_expert, tile_slot, tmm):
    p, d = xs.shape
    de = w_gate.shape[-1]
    assert d == de
    nt = p // tmm
    row = lambda t, *_: (t, 0)
    grid_spec = pltpu.PrefetchScalarGridSpec(
        num_scalar_prefetch=5,
        grid=(nt,),
        in_specs=[pl.BlockSpec((tmm, d), row)] + [pl.BlockSpec(memory_space=pl.ANY)] * 3,
        out_specs=pl.BlockSpec((tmm, d), row),
        scratch_shapes=[pltpu.VMEM((2, 3, d, de), F32),
                        pltpu.VMEM((d, de), BF16), pltpu.VMEM((d, de), BF16), pltpu.VMEM((de, d), BF16),
                        pltpu.SemaphoreType.DMA((2, 3))],
    )
    return pl.pallas_call(
        functools.partial(_gmm_kernel, layer=layer),
        grid_spec=grid_spec,
        out_shape=jax.ShapeDtypeStruct((p, d), BF16),
        compiler_params=_cp(("arbitrary",)),
        name="gmm",
    )(tile_expert, tile_first, tile_valid, next_expert, tile_slot, xs, w_gate, w_up, w_down)


def _moe(h2, t_tok, e_first, e_second, w_gate, w_up, w_down, layer, tmm):
    n = 2 * t_tok
    e_flat = jnp.concatenate([e_first, e_second])
    onehot = (e_flat[:, None] == jnp.arange(N_EXPERTS, dtype=I32)[None, :]).astype(I32)
    csum = jnp.cumsum(onehot, axis=0)
    counts = csum[-1]
    ptiles = (counts + tmm - 1) // tmm
    tile_end = jnp.cumsum(ptiles)
    dest = jnp.sum(onehot * (csum - 1 + ((tile_end - ptiles) * tmm)[None, :]), axis=1)
    nt = n // tmm + N_EXPERTS
    tid = jnp.arange(nt, dtype=I32)
    tile_valid = (tid < tile_end[-1]).astype(I32)
    te = jnp.minimum(jnp.sum((tile_end[None, :] <= tid[:, None]).astype(I32), axis=1), N_EXPERTS - 1)
    last_used = jnp.max(jnp.where(tile_valid == 1, te, 0))
    te = jnp.where(tile_valid == 1, te, last_used)
    tile_first = jnp.concatenate([jnp.ones((1,), I32), (te[1:] != te[:-1]).astype(I32)])
    tile_slot = (jnp.cumsum(tile_first) - 1) % 2
    experts = jnp.arange(N_EXPERTS, dtype=I32)
    later = (experts[None, :] > te[:, None]) & (counts[None, :] > 0)
    next_expert = jnp.min(jnp.where(later, experts[None, :], N_EXPERTS), axis=1)
    next_expert = jnp.where(next_expert < N_EXPERTS, next_expert, -1)
    order = jnp.argsort(e_flat, stable=True).astype(I32)
    seg_start = (tile_end - ptiles) * tmm
    shift = seg_start - (jnp.cumsum(counts) - counts)
    pos = tid[:, None] * tmm + jnp.arange(tmm, dtype=I32)[None, :]
    te_onehot = te[:, None] == jnp.arange(N_EXPERTS, dtype=I32)[None, :]
    per_tile = lambda v: jnp.sum(jnp.where(te_onehot, v[None, :], 0), axis=1, keepdims=True)
    used = (pos - per_tile(seg_start) < per_tile(counts)) & (tile_valid[:, None] == 1)
    src = (jnp.where(used, jnp.take(order, jnp.clip(pos - per_tile(shift), 0, n - 1)), pos) % t_tok).reshape(-1)
    assert h2.shape[0] >= nt * tmm
    xs = jnp.take(h2, src, axis=0, mode="clip")
    ys = _gmm(xs, w_gate, w_up, w_down, layer, te, tile_first, tile_valid, next_expert, tile_slot, tmm)
    return jnp.take(ys, dest, axis=0, mode="clip")


def _moe_combine(x, y1, y2, gates, m5):
    g = gates
    f = g[:, 0:1] * y1.astype(F32) + g[:, 1:2] * y2.astype(F32)
    return x + m5 * f


def _rope(x, cos, sin):
    n = x.shape[1]
    lane = lax.broadcasted_iota(I32, x.shape, 1)
    sw = jnp.where(lane % 32 < 16, pltpu.roll(x, n - 16, 1), pltpu.roll(x, 16, 1))
    reps = n // 128
    if reps > 1:
        cos = jnp.concatenate([cos] * reps, axis=1)
        sin = jnp.concatenate([sin] * reps, axis=1)
    return x * cos + sw * sin


def _inproj1_kernel(x_ref, y1_ref, y2_ref, ga_ref, m0_ref, m1_ref, g_ref, w_ref, cos_ref, sin_ref,
                    x1_ref, cq_ref, ckt_ref, cv_ref, dq_ref, dk_ref, dv_ref, *, tm, bs):
    m = m1_ref[0]
    in_lat = pl.program_id(0) * tm < bs
    scale = HEAD_DIM ** -0.5 * LOG2E
    for rows in (slice(0, tm // 2), slice(tm // 2, tm)):
        x1 = _moe_combine(x_ref[rows], y1_ref[rows], y2_ref[rows], ga_ref[rows], m0_ref[0][5:6])
        x1_ref[rows] = x1
        h = (_rms(x1, g_ref[...]) * (1.0 + m[1:2]) + m[0:1]).astype(BF16)
        cos, sin = cos_ref[rows], sin_ref[rows]
        cq = _dot(h, w_ref[:, 0:512])
        cq_ref[rows] = (jnp.where(in_lat, _rope(cq, cos, sin), cq) * scale).astype(BF16)
        ck = _dot(h, w_ref[:, 512:640])
        ckt_ref[:, rows] = jnp.where(in_lat, _rope(ck, cos, sin), ck).T.astype(BF16)
        cv_ref[rows] = _dot(h, w_ref[:, 640:768]).astype(BF16)
        dq_ref[rows] = (_dot(h, w_ref[:, 768:1280]) * scale).astype(BF16)
        dk_ref[rows] = _dot(h, w_ref[:, 1280:1792]).astype(BF16)
        dv_ref[rows] = _dot(h, w_ref[:, 1792:2304]).astype(BF16)


def _inproj1(x, y, gates, mods0, mods1, g, w, cos, sin, tm, bs, s, nb):
    r, d = x.shape
    second = pl.BlockSpec((tm, d), lambda i: (i + r // tm, 0))
    kern = functools.partial(_inproj1_kernel, tm=tm, bs=bs)
    mi = functools.partial(_mod_index, tm=tm, bs=bs, s=s, nb=nb)
    row = lambda wd: pl.BlockSpec((tm, wd), lambda i: (i, 0))
    modspec = pl.BlockSpec((1, N_MOD, d), lambda i: (mi(i), 0, 0))
    tab = pl.BlockSpec((tm, 128), lambda i: (jnp.where(i * tm < bs, (i * tm % s) // tm, 0), 0))
    shp = lambda wd, dt: jax.ShapeDtypeStruct((r, wd), dt)
    return pl.pallas_call(
        kern,
        grid=(r // tm,),
        in_specs=[row(d), row(d), second, row(128), modspec, modspec,
                  pl.BlockSpec((1, d), lambda i: (0, 0)), pl.BlockSpec(w.shape, lambda i: (0, 0)), tab, tab],
        out_specs=[row(d), row(512), pl.BlockSpec((128, tm), lambda i: (0, i)), row(128), row(512), row(512),
                   row(512)],
        out_shape=[shp(d, F32), shp(512, BF16), jax.ShapeDtypeStruct((128, r), BF16), shp(128, BF16),
                   shp(512, BF16), shp(512, BF16), shp(512, BF16)],
        compiler_params=_cp(("parallel",)),
        name="inproj1",
    )(x, y, y, gates, mods0, mods1, g, w, cos, sin)


def _swa_kernel(q_ref, ktp_ref, kto_ref, ktn_ref, ktx_ref, vp_ref, vo_ref, vn_ref, vx_ref, sink_ref, o_ref,
                *, nblk, cl):
    pair = pl.program_id(1)
    wb = SWA_BLOCK
    nloc = 3 * wb
    kto, vo, ktx, vx = kto_ref[...], vo_ref[...], ktx_ref[...], vx_ref[...]
    kts = (jnp.concatenate([ktp_ref[...], kto, ktx], axis=1), jnp.concatenate([kto, ktn_ref[...], ktx], axis=1))
    vvs = (jnp.concatenate([vp_ref[...], vo, vx], axis=0), jnp.concatenate([vo, vn_ref[...], vx], axis=0))
    a_i = lax.broadcasted_iota(I32, (2 * wb, nloc), 0) % wb
    c_i = lax.broadcasted_iota(I32, (2 * wb, nloc), 1)
    band = (c_i >= a_i) & (c_i <= a_i + 2 * SWA_WINDOW)
    oks = []
    for u in range(2):
        i = 2 * pair + u
        lo = jnp.where(i > 0, 0, wb)
        hi = jnp.where(i < nblk - 1, 3 * wb, 2 * wb)
        oks.append(band & (c_i >= lo) & (c_i < hi))
    half = lax.broadcasted_iota(I32, (1, 128), 1) // HEAD_DIM
    zero = jnp.zeros((), BF16)
    sink = sink_ref[...]
    items = [(u, g) for u in range(2) for g in range(4)]

    def scores(u, g):
        q2 = q_ref[wb * u:wb * u + wb, 128 * g:128 * g + 128]
        qst = jnp.concatenate([jnp.where(half == 0, q2, zero), jnp.where(half == 1, q2, zero)], axis=0)
        s_all = _dot(qst, kts[u])
        return jnp.concatenate([jnp.where(oks[u], s_all[:, 0:nloc], NEG), s_all[:, nloc:]], axis=1)

    def softmax(g, sc):
        sk = jnp.concatenate([jnp.broadcast_to(sink[0:1, 2 * g + a:2 * g + a + 1], (wb, 1)) for a in range(2)],
                             axis=0)
        m = jnp.maximum(jnp.max(sc, axis=-1, keepdims=True), sk)
        p = jnp.exp2(sc - m)
        return p.astype(BF16), jnp.sum(p, axis=-1, keepdims=True) + jnp.exp2(sk - m)

    def output(u, g, p, den):
        ost = _dot(p, vvs[u]) / den
        o_ref[wb * u:wb * u + wb, 128 * g:128 * g + 128] = jnp.where(half == 0, ost[0:wb], ost[wb:2 * wb]).astype(BF16)

    n = len(items)
    sc, pr = {}, {}
    for step in range(n + 2):
        if step < n:
            sc[step] = scores(*items[step])
        if 1 <= step < n + 1:
            pr[step - 1] = softmax(items[step - 1][1], sc.pop(step - 1))
        if step >= 2:
            output(*items[step - 2], *pr.pop(step - 2))


def _swa(cq, ckt, cv, sink_row, nb, s, cl, bs):
    wb = SWA_BLOCK
    nblk = s // wb
    assert nblk % 2 == 0
    npair = nblk // 2
    kern = functools.partial(_swa_kernel, nblk=nblk, cl=cl)
    prev = lambda b, j: b * nblk + jnp.maximum(2 * j - 1, 0)
    nxt = lambda b, j: b * nblk + jnp.minimum(2 * j + 2, nblk - 1)
    own = lambda b, j: b * npair + j
    return pl.pallas_call(
        kern,
        grid=(nb, npair),
        in_specs=[pl.BlockSpec((2 * wb, 512), lambda b, j: (own(b, j), 0)),
                  pl.BlockSpec((128, wb), lambda b, j: (0, prev(b, j))),
                  pl.BlockSpec((128, 2 * wb), lambda b, j: (0, own(b, j))),
                  pl.BlockSpec((128, wb), lambda b, j: (0, nxt(b, j))),
                  pl.BlockSpec((128, cl), lambda b, j: (0, bs // cl + b)),
                  pl.BlockSpec((wb, 128), lambda b, j: (prev(b, j), 0)),
                  pl.BlockSpec((2 * wb, 128), lambda b, j: (own(b, j), 0)),
                  pl.BlockSpec((wb, 128), lambda b, j: (nxt(b, j), 0)),
                  pl.BlockSpec((cl, 128), lambda b, j: (bs // cl + b, 0)),
                  pl.BlockSpec((1, 128), lambda b, j: (0, 0))],
        out_specs=pl.BlockSpec((2 * wb, 512), lambda b, j: (own(b, j), 0)),
        out_shape=jax.ShapeDtypeStruct((bs, 512), BF16),
        compiler_params=_cp(("parallel", "parallel")),
        name="swa",
    )(cq, ckt, ckt, ckt, ckt, cv, cv, cv, cv, sink_row)


def _na_kernel(q_ref, k_ref, v_ref, kx_ref, vx_ref, bias_ref, o_ref, *, rows, unroll):
    half = lax.broadcasted_iota(I32, (1, 128), 1) // HEAD_DIM
    zero = jnp.zeros((), BF16)
    vx = vx_ref[...]
    kxt = kx_ref[...].astype(F32).T.astype(BF16)
    span = NA_KH * GRID_W

    def scores(r):
        rs = jnp.clip(r - NA_KH // 2, 0, rows - NA_KH)
        off = rs - r + NA_KH - 1
        q0 = pl.multiple_of(r * GRID_W, GRID_W)
        k0 = pl.multiple_of(rs * GRID_W, GRID_W)
        q2 = q_ref[pl.ds(q0, GRID_W), :]
        qst = jnp.concatenate([jnp.where(half == 0, q2, zero), jnp.where(half == 1, q2, zero)], axis=0)
        s_loc = (_dot_nt(qst, k_ref[pl.ds(k0, span), :])
                 + jnp.concatenate([bias_ref[0, off], bias_ref[1, off]], axis=0))
        return q0, k0, s_loc, _dot(qst, kxt)

    def softmax(s_loc, s_ctx):
        m = jnp.maximum(jnp.max(s_loc, axis=-1, keepdims=True), jnp.max(s_ctx, axis=-1, keepdims=True))
        p_loc = jnp.exp2(s_loc - m)
        p_ctx = jnp.exp2(s_ctx - m)
        den = jnp.sum(p_loc, axis=-1, keepdims=True) + jnp.sum(p_ctx, axis=-1, keepdims=True)
        return p_loc.astype(BF16), p_ctx.astype(BF16), den

    def output(q0, k0, p_loc, p_ctx, den):
        ost = (_dot(p_loc, v_ref[pl.ds(k0, span), :]) + _dot(p_ctx, vx)) / den
        o = jnp.where(half == 0, ost[0:GRID_W], ost[GRID_W:2 * GRID_W])
        o_ref[pl.ds(q0, GRID_W), :] = o.astype(BF16)

    def body(i, carry):
        sc, pr = {}, {}
        for step in range(unroll + 2):
            if step < unroll:
                sc[step] = scores(i * unroll + step)
            if 1 <= step < unroll + 1:
                q0, k0, s_loc, s_ctx = sc.pop(step - 1)
                pr[step - 1] = (q0, k0) + softmax(s_loc, s_ctx)
            if step >= 2:
                output(*pr.pop(step - 2))
        return carry

    lax.fori_loop(0, rows // unroll, body, 0)


def _na(dq, dk, dv, bias, nb, s, cl, bs):
    rows = s // GRID_W
    unroll = 8
    assert rows % unroll == 0
    kern = functools.partial(_na_kernel, rows=rows, unroll=unroll)
    seq = pl.BlockSpec((s, 128), lambda b, p: (b, p))
    ctx = pl.BlockSpec((cl, 128), lambda b, p: (bs // cl + b, p))
    return pl.pallas_call(
        kern,
        grid=(nb, NA_HEADS // 2),
        in_specs=[seq, seq, seq, ctx, ctx,
                  pl.BlockSpec((2, NA_KH, GRID_W, NA_KH * GRID_W), lambda b, p: (p, 0, 0, 0))],
        out_specs=seq,
        out_shape=jax.ShapeDtypeStruct((bs, 512), BF16),
        compiler_params=_cp(("parallel", "parallel")),
        name="na",
    )(dq, dk, dv, dk, dv, bias)


def _na_bias_table(rpb):
    c = np.arange(GRID_W)
    qs = np.clip(c - NA_KW // 2, 0, GRID_W - NA_KW)
    kc = np.arange(GRID_W)
    ok = (kc[None, :] >= qs[:, None]) & (kc[None, :] < qs[:, None] + NA_KW)
    dc = np.clip(kc[None, :] - c[:, None] + NA_KW - 1, 0, 2 * NA_KW - 2)
    sel = (np.arange(2 * NA_KW - 1)[:, None, None] == dc[None]).astype(np.float32)
    cols = jnp.einsum("hab,bck->hcak", rpb.astype(F32), sel, precision=HI)
    cols = jnp.where(ok[None, :, None, :], cols * LOG2E, NEG)
    return jnp.stack([cols[:, :, off:off + NA_KH].reshape(NA_HEADS, GRID_W, NA_KH * GRID_W)
                      for off in range(NA_KH)], axis=1)


def _mix1_kernel(x_ref, oc_ref, od_ref, mod_ref, wo_ref, g2_ref, rw_ref, rb_ref, h2buf_ref,
                 xn_ref, h2_ref, ei_ref, ga_ref):
    del h2buf_ref
    y = _dot(oc_ref[...], wo_ref[0:512, :]) + _dot(od_ref[...], wo_ref[512:1024, :])
    xn, h2, eidx, gates = _post_mixer(x_ref[...], y, mod_ref[0], g2_ref[...], rw_ref[...], rb_ref[...])
    xn_ref[...] = xn
    h2_ref[...] = h2.astype(BF16)
    ei_ref[...] = eidx
    ga_ref[...] = gates


def _mix1(x, oc, od, mods, wo, g2, rw, rb, h2buf, tm, bs, s):
    d = x.shape[1]
    row = lambda w: pl.BlockSpec((tm, w), lambda i: (i, 0))
    full = lambda a: pl.BlockSpec(a.shape, lambda i: (0,) * a.ndim)
    return pl.pallas_call(
        _mix1_kernel,
        grid=(bs // tm,),
        in_specs=[row(d), row(512), row(512), pl.BlockSpec((1, N_MOD, d), lambda i: (i * tm // s, 0, 0)),
                  full(wo), full(g2), full(rw), full(rb), pl.BlockSpec(memory_space=pl.ANY)],
        out_specs=[row(d), row(d), pl.BlockSpec((8, tm), lambda i: (0, i)), row(128)],
        out_shape=[jax.ShapeDtypeStruct((bs, d), F32), jax.ShapeDtypeStruct(h2buf.shape, BF16),
                   jax.ShapeDtypeStruct((8, bs), I32), jax.ShapeDtypeStruct((bs, 128), F32)],
        input_output_aliases={8: 1},
        compiler_params=_cp(("parallel",)),
        name="mix1",
    )(x, oc, od, mods, wo, g2, rw, rb, h2buf)


def _final_kernel(x_ref, y1_ref, y2_ref, ga_ref, mod_ref, g_ref, o_ref):
    x = _moe_combine(x_ref[...], y1_ref[...], y2_ref[...], ga_ref[...], mod_ref[0][5:6])
    o_ref[...] = _rms(x, g_ref[...])


def _final(x, y, gates, mods, g, tm, s):
    r, d = x.shape
    second = pl.BlockSpec((tm, d), lambda i: (i + r // tm, 0))
    row = lambda w: pl.BlockSpec((tm, w), lambda i: (i, 0))
    return pl.pallas_call(
        _final_kernel,
        grid=(r // tm,),
        in_specs=[row(d), row(d), second, row(128), pl.BlockSpec((1, N_MOD, d), lambda i: (i * tm // s, 0, 0)),
                  pl.BlockSpec((1, d), lambda i: (0, 0))],
        out_specs=row(d),
        out_shape=jax.ShapeDtypeStruct((r, d), F32),
        compiler_params=_cp(("parallel",)),
        name="final",
    )(x, y, y, gates, mods, g)


def _rope_tables(s):
    nf = HEAD_DIM // 4
    t = np.arange(s)
    inv = ROPE_THETA ** (-np.arange(nf, dtype=np.float64) / nf)
    ar = (t // GRID_W)[:, None] * inv
    ac = (t % GRID_W)[:, None] * inv
    cos = np.concatenate([np.cos(ar), np.cos(ar), np.cos(ac), np.cos(ac)], axis=1)
    sin = np.concatenate([-np.sin(ar), np.sin(ar), -np.sin(ac), np.sin(ac)], axis=1)
    return (jnp.asarray(np.concatenate([cos, cos], axis=1), F32),
            jnp.asarray(np.concatenate([sin, sin], axis=1), F32))


def kernel(x, c, ctx, c_ctx, ada_w, ada_b, norm1_g, norm2_g, ev_w_in, ev_w_out, sc_conv_w, dn_conv_w, dn_a_log, dn_dt_bias, dn_onorm_g, od_w_in, od_w_out, swa_sink, na_rpb, router_w, router_b, moe_w_gate, moe_w_up, moe_w_down, final_g):
    nb, s, d = x.shape
    cl = ctx.shape[1]
    bs = nb * s
    tm = 512
    ts = 256
    tmm = 512
    assert d == 1024 and s % tm == 0 and (nb * cl) % tm == 0 and cl % ts == 0 and s % ts == 0
    assert s // GRID_W >= NA_KH and bs % cl == 0 and nb + 1 <= 8

    xl, xc = x.reshape(bs, d), ctx.reshape(nb * cl, d)
    cc = jnp.zeros((8, d), F32).at[:nb].set(c).at[nb].set(c_ctx)
    mods = _ada(cc, ada_w, ada_b).reshape(ada_w.shape[0], 8, N_MOD, d)
    rw32 = jnp.pad(router_w, ((0, 0), (0, 128 - N_EXPERTS)))
    rw_hi = rw32.astype(BF16)
    rw = jnp.concatenate([rw_hi, (rw32 - rw_hi.astype(F32)).astype(BF16)], axis=1)
    rb = router_b.reshape(N_EXPERTS, 1)
    row = lambda v: v.reshape(1, -1)

    w_in0 = jnp.pad(ev_w_in[0], ((0, 0), (0, 3712 - ev_w_in.shape[-1]))).astype(BF16)
    sc, qkv, z, bg = _inproj0(xl, xc, mods[0], row(norm1_g[0]), w_in0, tm, bs, s, nb)
    pad16 = lambda v: jnp.pad(v.reshape(-1), (8, 128 - 16)).reshape(1, 128)
    uf, ub, wf, wb, qf, qb, kf, kb, af, ab, gc = _dnchunk(qkv, bg, dn_conv_w[0], pad16(dn_a_log[0]),
                                                          pad16(dn_dt_bias[0]), ts, bs, s, cl)
    o_pairs = _dnscan(uf, ub, wf, wb, qf, qb, kf, kb, af, ab, gc, nb, s, cl, bs)
    r_all = bs + nb * cl
    moe_rows = lambda t: (2 * t // tmm + N_EXPERTS) * tmm
    x0, h2, ei, ga = _mix0(xl, xc, sc, o_pairs, z, mods[0], sc_conv_w[0], row(dn_onorm_g[0]),
                           ev_w_out[0].astype(BF16), row(norm2_g[0]), rw, rb,
                           jnp.zeros((moe_rows(r_all), d), BF16), ts, bs, s, cl, nb)
    y = _moe(h2, r_all, ei[0], ei[1], moe_w_gate, moe_w_up, moe_w_down, 0, tmm)

    perm = np.concatenate([np.arange(HEAD_DIM) + HEAD_DIM * (g + 4 * a) for g in range(4) for a in range(2)])
    w1 = od_w_in[0]
    w_in1 = jnp.concatenate([w1[:, 0:512][:, perm], w1[:, 512:]], axis=1).astype(BF16)
    wo1 = od_w_out[0]
    w_out1 = jnp.concatenate([wo1[0:512][perm], wo1[512:]], axis=0).astype(BF16)
    sink_row = jnp.pad(swa_sink[0][np.array([g + 4 * a for g in range(4) for a in range(2)])] * LOG2E,
                       (0, 128 - SWA_HEADS)).reshape(1, 128)
    cos, sin = _rope_tables(s)
    x1, cq, ckt, cv, dq, dk, dv = _inproj1(x0, y, ga, mods[0], mods[1], row(norm1_g[1]), w_in1, cos, sin,
                                           tm, bs, s, nb)
    oc = _swa(cq, ckt, cv, sink_row, nb, s, cl, bs)
    od = _na(dq, dk, dv, _na_bias_table(na_rpb[0]), nb, s, cl, bs)
    assert moe_rows(bs) <= h2.shape[0]
    x2, h2, ei, ga = _mix1(x1, oc, od, mods[1], w_out1, row(norm2_g[1]), rw, rb, h2, tm, bs, s)
    y = _moe(h2, bs, ei[0], ei[1], moe_w_gate, moe_w_up, moe_w_down, 1, tmm)
    out = _final(x2, y, ga, mods[1], row(final_g), tm, s)
    return out.reshape(nb, s, d)
```

```python
import functools
import math

import numpy as np
import jax
import jax.numpy as jnp
from jax import lax
from jax.experimental import pallas as pl
from jax.experimental.pallas import tpu as pltpu

F32 = jnp.float32
BF16 = jnp.bfloat16
I32 = jnp.int32
HI = lax.Precision.HIGHEST

EPS = 1e-6
N_MOD = 6
GRID_W = 64
HEAD_DIM = 64
DN_HEADS = 4
DN_HD = 128
DN_CHUNK = 64
SWA_HEADS = 8
SWA_KV = 2
SWA_BLOCK = 128
SWA_WINDOW = 128
NA_HEADS = 8
NA_KH = 8
NA_KW = 16
ROPE_THETA = 10000.0
N_EXPERTS = 16
N_GROUPS = 4
NEG = -1e30
LOG2E = 1.4426950408889634
VMEM_LIMIT = 56 * 1024 * 1024


def _cp(sem, vmem=VMEM_LIMIT):
    return pltpu.CompilerParams(dimension_semantics=sem, vmem_limit_bytes=vmem)


def _dot(a, b, precision=None):
    return jnp.dot(a, b, preferred_element_type=F32, precision=precision)


def _dot_nt(a, b, precision=None):
    return lax.dot_general(a, b, (((1,), (1,)), ((), ())), preferred_element_type=F32, precision=precision)


def _dot_tn(a, b, precision=None):
    return lax.dot_general(a, b, (((0,), (0,)), ((), ())), preferred_element_type=F32, precision=precision)


def _silu(x):
    return x * jax.nn.sigmoid(x)


def _rms(x, g):
    return x * lax.rsqrt(jnp.mean(x * x, axis=-1, keepdims=True) + EPS) * g


def _ada_kernel(cc_ref, w_ref, b_ref, o_ref):
    a = _silu(cc_ref[...])
    o_ref[0] = _dot(a, w_ref[0], HI) + b_ref[0]


def _ada(cc, ada_w, ada_b):
    depth, d, n = ada_w.shape
    tn = 1536
    return pl.pallas_call(
        _ada_kernel,
        grid=(depth, n // tn),
        in_specs=[pl.BlockSpec((8, d), lambda l, j: (0, 0)),
                  pl.BlockSpec((1, d, tn), lambda l, j: (l, 0, j)),
                  pl.BlockSpec((1, 1, tn), lambda l, j: (l, 0, j))],
        out_specs=pl.BlockSpec((1, 8, tn), lambda l, j: (l, 0, j)),
        out_shape=jax.ShapeDtypeStruct((depth, 8, n), F32),
        compiler_params=_cp(("parallel", "parallel")),
        name="ada",
    )(cc, ada_w, ada_b.reshape(depth, 1, n))


def _mod_index(i, tm, bs, s, nb):
    row0 = i * tm
    return jnp.where(row0 < bs, row0 // s, nb)


def _token_specs(tm, bs, d):
    nlat = bs // tm
    return [pl.BlockSpec((tm, d), lambda i: (jnp.minimum(i, nlat - 1), 0)),
            pl.BlockSpec((tm, d), lambda i: (jnp.maximum(i - nlat, 0), 0))]


def _token_rows(xl_ref, xc_ref, tm, bs):
    return jnp.where(pl.program_id(0) * tm < bs, xl_ref[...], xc_ref[...])


def _inproj0_kernel(xl_ref, xc_ref, mod_ref, g_ref, w_ref, sc_ref, qkv_ref, z_ref, bg_ref, *, tm, bs):
    m = mod_ref[0]
    h = (_rms(_token_rows(xl_ref, xc_ref, tm, bs), g_ref[...]) * (1.0 + m[1:2]) + m[0:1]).astype(BF16)
    sc_ref[...] = _dot(h, w_ref[:, 0:1536]).astype(BF16)
    qkv_ref[...] = _dot(h, w_ref[:, 1536:3072]).astype(BF16)
    z_ref[...] = _dot(h, w_ref[:, 3072:3584]).astype(BF16)
    bg_ref[...] = _dot(h, w_ref[:, 3584:3712])


def _inproj0(xl, xc, mods, g, w, tm, bs, s, nb):
    d = xl.shape[1]
    r = bs + xc.shape[0]
    mi = functools.partial(_mod_index, tm=tm, bs=bs, s=s, nb=nb)
    return pl.pallas_call(
        functools.partial(_inproj0_kernel, tm=tm, bs=bs),
        grid=(r // tm,),
        in_specs=_token_specs(tm, bs, d) + [
            pl.BlockSpec((1, N_MOD, d), lambda i: (mi(i), 0, 0)),
            pl.BlockSpec((1, d), lambda i: (0, 0)),
            pl.BlockSpec(w.shape, lambda i: (0, 0))],
        out_specs=[pl.BlockSpec((tm, 1536), lambda i: (i, 0)),
                   pl.BlockSpec((tm, 1536), lambda i: (i, 0)),
                   pl.BlockSpec((tm, 512), lambda i: (i, 0)),
                   pl.BlockSpec((tm, 128), lambda i: (i, 0))],
        out_shape=[jax.ShapeDtypeStruct((r, 1536), BF16), jax.ShapeDtypeStruct((r, 1536), BF16),
                   jax.ShapeDtypeStruct((r, 512), BF16), jax.ShapeDtypeStruct((r, 128), F32)],
        compiler_params=_cp(("parallel",)),
        name="inproj0",
    )(xl, xc, mods, g, w)


def _seq_edges(i, ts, bs, s, cl):
    row0 = i * ts
    in_lat = row0 < bs
    r_in = jnp.where(in_lat, row0 % s, (row0 - bs) % cl)
    seqlen = jnp.where(in_lat, s, cl)
    return r_in == 0, r_in + ts == seqlen


def _shifted(x, prev_row, next_row):
    n = x.shape[0]
    rows = lax.broadcasted_iota(I32, x.shape, 0)
    xp = jnp.where(rows == 0, prev_row, pltpu.roll(x, 1, 0))
    xn = jnp.where(rows == n - 1, next_row, pltpu.roll(x, n - 1, 0))
    return xp, xn


HALO = 16


def _halo_specs(ts, width, r):
    nblk = r // HALO
    k = ts // HALO
    return [pl.BlockSpec((HALO, width), lambda i: (jnp.maximum(i * k - 1, 0), 0)),
            pl.BlockSpec((HALO, width), lambda i: (jnp.minimum((i + 1) * k, nblk - 1), 0))]


def _dnprep_tile(x_ref, prev_ref, next_ref, bg_ref, cw_ref, alog_ref, dt_ref,
                 q_ref, k_ref, v_ref, bga_ref, *, ts, bs, s, cl):
    first, last = _seq_edges(pl.program_id(0), ts, bs, s, cl)
    for c in range(12):
        sl = slice(128 * c, 128 * c + 128)
        x = x_ref[:, sl].astype(F32)
        pr = jnp.where(first, 0.0, prev_ref[HALO - 1:HALO, sl].astype(F32))
        nx = jnp.where(last, 0.0, next_ref[0:1, sl].astype(F32))
        xp, xn = _shifted(x, pr, nx)
        w = cw_ref[:, sl]
        y = _silu(xp * w[0:1] + x * w[1:2] + xn * w[2:3])
        hs = slice(128 * (c % 4), 128 * (c % 4) + 128)
        if c < 8:
            y = y * lax.rsqrt(jnp.sum(y * y, axis=-1, keepdims=True) + EPS)
        if c < 4:
            q_ref[:, hs] = y * DN_HD ** -0.5
        elif c < 8:
            k_ref[:, hs] = y
        else:
            v_ref[:, hs] = y
    b = bg_ref[...]
    cols = lax.broadcasted_iota(I32, b.shape, 1)
    beta = jax.nn.sigmoid(b)
    t = b + dt_ref[...]
    softplus = jnp.maximum(t, 0.0) + jnp.log1p(jnp.exp(-jnp.abs(t)))
    g = -jnp.exp(alog_ref[...]) * softplus
    bga_ref[...] = jnp.where(cols < 8, beta, jnp.where(cols < 16, g, 0.0))


def _dnchunk_kernel(x_ref, prev_ref, next_ref, bgraw_ref, cw_ref, alog_ref, dt_ref,
                    uf_ref, ub_ref, wf_ref, wb_ref, qf_ref, qb_ref, kf_ref, kb_ref, af_ref, ab_ref, gc_ref,
                    q_ref, k_ref, v_ref, bg_ref, *, nchunks, bs, s, cl):
    _dnprep_tile(x_ref, prev_ref, next_ref, bgraw_ref, cw_ref, alog_ref, dt_ref, q_ref, k_ref, v_ref, bg_ref,
                 ts=nchunks * DN_CHUNK, bs=bs, s=s, cl=cl)
    outs = ((uf_ref, wf_ref, qf_ref, kf_ref, af_ref), (ub_ref, wb_ref, qb_ref, kb_ref, ab_ref))
    c, nh = DN_CHUNK, DN_HEADS
    head_of_col = lax.broadcasted_iota(I32, (1, nh * c), 1) // c
    zero = jnp.zeros((), BF16)

    def block_diag(x):
        return jnp.concatenate([jnp.where(head_of_col == h, x, zero) for h in range(nh)], axis=0)

    chains = []
    for cc in range(nchunks):
        chains += _dnchunk_setup(slice(cc * c, (cc + 1) * c), q_ref, k_ref, v_ref, bg_ref, gc_ref)
    for ch in chains:
        ch["tm"] = ch["nmat"]
        nb16 = ch["nmat"].astype(BF16)
        ch["npow"] = _dot(nb16, block_diag(nb16))
    for _ in range(4):
        for ch in chains:
            nb16 = ch["npow"].astype(BF16)
            ch["both"] = _dot(jnp.concatenate([nb16, ch["tm"].astype(BF16)], axis=0), block_diag(nb16))
        for ch in chains:
            ch["tm"] = ch["tm"] + ch["npow"] + ch["both"][c:2 * c]
            ch["npow"] = ch["both"][0:c]
    for ch in chains:
        ch["both"] = _dot(ch["tm"].astype(BF16), block_diag(ch["npow"].astype(BF16)))
    for ch in chains:
        ch["tm"] = ch["tm"] + ch["npow"] + ch["both"]
    for ch in chains:
        ch["uw"] = ch["rhs"] + _dot(block_diag(ch["tm"].astype(BF16)), ch["rhs"].astype(BF16))
    for ch in chains:
        u_ref, w_ref, qd_ref, kd_ref, at_ref = outs[ch["d"]]
        rows, uw = ch["rows"], ch["uw"]
        for h in range(nh):
            hs = slice(DN_HD * h, DN_HD * h + DN_HD)
            rs = slice(c * h, c * h + c)
            u_ref[rows, hs] = uw[rs, 0:DN_HD]
            w_ref[rows, hs] = uw[rs, DN_HD:2 * DN_HD].astype(BF16)
            qd_ref[rows, hs] = ch["qd"][rs]
        kd_ref[slice(2 * rows.start, 2 * rows.stop)] = ch["kd"].T.astype(BF16)
        at_ref[rows] = ch["att"].astype(BF16)


def _dnchunk_setup(rows, q_ref, k_ref, v_ref, bg_ref, gc_ref):
    c, nh = DN_CHUNK, DN_HEADS
    n = c * nh
    bg = bg_ref[rows]
    i64 = lax.broadcasted_iota(I32, (c, c), 0)
    j64 = lax.broadcasted_iota(I32, (c, c), 1)
    cols = lax.broadcasted_iota(I32, bg.shape, 1)
    gcf = _dot((i64 >= j64).astype(F32), bg, HI)
    gcb = _dot((i64 <= j64).astype(F32), bg, HI)
    gc = jnp.where(cols >= 12, gcb, gcf)
    gc_ref[rows] = gc
    gct = gc.T
    ii = lax.broadcasted_iota(I32, (c, n), 0)
    jj = lax.broadcasted_iota(I32, (c, n), 1)
    head_of_col = jj // c
    jj = jj % c

    def stack(ref):
        return jnp.concatenate([ref[rows, DN_HD * h:DN_HD * h + DN_HD] for h in range(nh)], axis=0)

    def stacked_cols(arr, r0, r1, col0):
        return jnp.concatenate([jnp.broadcast_to(arr[r0:r1, col0 + h:col0 + h + 1], (c, DN_HD)) for h in range(nh)],
                               axis=0)

    def side_by_side_cols(arr, col0):
        out = arr[:, col0 + nh - 1:col0 + nh]
        for h in range(nh - 2, -1, -1):
            out = jnp.where(head_of_col == h, arr[:, col0 + h:col0 + h + 1], out)
        return out

    def diag_blocks(x):
        out = x[(nh - 1) * c:nh * c]
        for h in range(nh - 2, -1, -1):
            out = jnp.where(head_of_col == h, x[h * c:(h + 1) * c], out)
        return out

    kst, qst, vst = stack(k_ref), stack(q_ref), stack(v_ref)
    kb = kst.astype(BF16)
    kq = _dot_nt(jnp.concatenate([kb, qst.astype(BF16)], axis=0), kb)
    kkt, qkt = diag_blocks(kq[0:n]), diag_blocks(kq[n:2 * n])
    chains = []
    for d in range(2):
        incl = (ii >= jj) if d == 0 else (ii <= jj)
        strict = (ii > jj) if d == 0 else (ii < jj)
        last = c - 1 if d == 0 else 0
        grow = jnp.concatenate([gct[8 + 4 * d + h:9 + 4 * d + h, :] for h in range(nh)], axis=1)
        decay = jnp.exp(jnp.where(incl, side_by_side_cols(gc, 8 + 4 * d) - grow, NEG))
        nmat = jnp.where(strict, -(side_by_side_cols(bg, 4 * d) * kkt * decay), 0.0)
        b1 = stacked_cols(bg, 0, c, 4 * d)
        gcol = stacked_cols(gc, 0, c, 8 + 4 * d)
        glast = stacked_cols(gc, last, last + 1, 8 + 4 * d)
        e1 = jnp.exp(gcol)
        chains.append(dict(
            d=d, rows=rows, nmat=nmat,
            rhs=jnp.concatenate([b1 * vst, (b1 * e1) * kst], axis=1),
            qd=(qst * e1).astype(BF16),
            kd=kst * jnp.exp(glast - gcol),
            att=qkt * decay))
    return chains


def _dnchunk(qkv, bg, cw, alog_row, dt_row, ts, bs, s, cl):
    r = qkv.shape[0]
    nchunks = ts // DN_CHUNK
    c = ts
    row = lambda w: pl.BlockSpec((c, w), lambda i: (i, 0))
    shp = lambda w, dt: jax.ShapeDtypeStruct((r, w), dt)
    return pl.pallas_call(
        functools.partial(_dnchunk_kernel, nchunks=nchunks, bs=bs, s=s, cl=cl),
        grid=(r // c,),
        in_specs=[row(1536)] + _halo_specs(ts, 1536, r) + [
            row(128),
            pl.BlockSpec((3, 1536), lambda i: (0, 0)),
            pl.BlockSpec((1, 128), lambda i: (0, 0)),
            pl.BlockSpec((1, 128), lambda i: (0, 0))],
        scratch_shapes=[pltpu.VMEM((c, 512), F32)] * 3 + [pltpu.VMEM((c, 128), F32)],
        out_specs=[row(512)] * 6 + [pl.BlockSpec((2 * c, 256), lambda i: (i, 0))] * 2 + [row(256), row(256), row(128)],
        out_shape=([shp(512, F32)] * 2 + [shp(512, BF16)] * 4 + [jax.ShapeDtypeStruct((2 * r, 256), BF16)] * 2
                   + [shp(256, BF16)] * 2 + [shp(128, F32)]),
        compiler_params=_cp(("parallel",)),
        name="dnchunk",
    )(qkv, qkv, qkv, bg, cw, alog_row, dt_row)


def _dnscan_kernel(*refs, nsub, nb):
    nchain = 2 * nb
    ins, outs, s_ref = refs[:6 * nchain], refs[6 * nchain:7 * nchain], refs[7 * nchain]

    @pl.when(pl.program_id(0) == 0)
    def _():
        s_ref[...] = jnp.zeros_like(s_ref)

    c, nh = DN_CHUNK, DN_HEADS
    head_of_lane = lax.broadcasted_iota(I32, (1, nh * DN_HD), 1) // DN_HD
    head_of_col = lax.broadcasted_iota(I32, (1, nh * c), 1) // c
    zero = jnp.zeros((), BF16)

    def block_diag(tile, head_ids):
        return jnp.concatenate([jnp.where(head_ids == h, tile, zero) for h in range(nh)], axis=0)

    states = [s_ref[ci] for ci in range(nchain)]
    for sub in range(nsub):
        work = []
        for ci in range(nchain):
            d = ci % 2
            u_ref, w_ref, qd_ref, kd_ref, at_ref, g_ref = ins[6 * ci:6 * ci + 6]
            last = c - 1 if d == 0 else 0
            k = sub if d == 0 else nsub - 1 - sub
            rows = slice(c * k, c * k + c)
            g = g_ref[rows]
            decay = jnp.concatenate(
                [jnp.broadcast_to(jnp.exp(g[last:last + 1, 8 + 4 * d + h:9 + 4 * d + h]), (DN_HD, DN_HD))
                 for h in range(nh)], axis=0)
            ust = jnp.concatenate([u_ref[rows, DN_HD * h:DN_HD * h + DN_HD] for h in range(nh)], axis=0)
            stb = states[ci].astype(BF16)
            vnew = ust - _dot(block_diag(w_ref[rows], head_of_lane), stb)
            work.append((rows, decay, stb, vnew.astype(BF16)))
        for ci in range(nchain):
            u_ref, w_ref, qd_ref, kd_ref, at_ref, g_ref = ins[6 * ci:6 * ci + 6]
            rows, decay, stb, vnb = work[ci]
            o = (_dot(block_diag(qd_ref[rows], head_of_lane), stb)
                 + _dot(block_diag(at_ref[rows], head_of_col), vnb))
            kdt = kd_ref[slice(2 * rows.start, 2 * rows.stop)]
            states[ci] = states[ci] * decay + _dot(block_diag(kdt, head_of_col), vnb)
            for h in range(nh):
                outs[ci][rows, DN_HD * h:DN_HD * h + DN_HD] = o[c * h:c * h + c].astype(BF16)
    for ci in range(nchain):
        s_ref[ci] = states[ci]


def _dnscan(uf, ub, wf, wb, qf, qb, kf, kb, af, ab, gc, nb, s, cl, bs):
    nsub = 4
    c = DN_CHUNK * nsub
    assert cl % c == 0 and s % c == 0 and bs % c == 0
    ncc, ncl = cl // c, s // c
    ns = ncc + ncl

    def src_block(b, d):
        if d == 0:
            return lambda t: jnp.where(t < ncc, bs // c + b * ncc + t, b * ncl + t - ncc)
        return lambda t: jnp.where(t < ncc, bs // c + b * ncc + (ncc - 1 - t), b * ncl + (ncl - 1 - (t - ncc)))

    def dst_block(d):
        if d == 0:
            return lambda t: jnp.where(t < ncc, ncl + t, t - ncc)
        return lambda t: jnp.where(t < ncc, ncl + (ncc - 1 - t), ncl - 1 - (t - ncc))

    in_specs, args, out_specs = [], [], []
    for b in range(nb):
        for d, group in enumerate(((uf, wf, qf, kf, af, gc), (ub, wb, qb, kb, ab, gc))):
            idx = src_block(b, d)
            blk = lambda w, idx=idx: pl.BlockSpec((c, w), lambda t: (idx(t), 0))
            in_specs += [blk(512), blk(512), blk(512), pl.BlockSpec((2 * c, 256), lambda t, idx=idx: (idx(t), 0)),
                         blk(256), blk(128)]
            args += list(group)
            out_specs.append(pl.BlockSpec((c, 512), lambda t, f=dst_block(d): (f(t), 0)))
    outs = pl.pallas_call(
        functools.partial(_dnscan_kernel, nsub=nsub, nb=nb),
        grid=(ns,),
        in_specs=in_specs,
        out_specs=out_specs,
        out_shape=[jax.ShapeDtypeStruct((s + cl, 512), BF16)] * (2 * nb),
        scratch_shapes=[pltpu.VMEM((2 * nb, DN_HEADS * DN_HD, DN_HD), F32)],
        compiler_params=_cp(("arbitrary",)),
        name="dnscan",
    )(*args)
    return [(outs[2 * b], outs[2 * b + 1]) for b in range(nb)]


def _route(logits, bias_col):
    epg = N_EXPERTS // N_GROUPS
    tm = logits.shape[0]
    scores = jax.nn.sigmoid(logits.T[0:N_EXPERTS])
    gsel = scores + bias_col
    row = lambda a, k: a[k:k + 1]
    best = gidx = None
    for g in range(N_GROUPS):
        a = [row(gsel, epg * g + k) for k in range(epg)]
        m01, n01 = jnp.maximum(a[0], a[1]), jnp.minimum(a[0], a[1])
        m23, n23 = jnp.maximum(a[2], a[3]), jnp.minimum(a[2], a[3])
        gs = jnp.maximum(m01, m23) + jnp.maximum(jnp.minimum(m01, m23), jnp.maximum(n01, n23))
        if g == 0:
            best, gidx = gs, jnp.zeros_like(gs)
        else:
            better = gs > best
            best = jnp.where(better, gs, best)
            gidx = jnp.where(better, float(g), gidx)
    sel = [None] * epg
    raw = [None] * epg
    for g in range(N_GROUPS):
        for k in range(epg):
            v, u = row(gsel, epg * g + k), row(scores, epg * g + k)
            sel[k] = v if g == 0 else jnp.where(gidx == g, v, sel[k])
            raw[k] = u if g == 0 else jnp.where(gidx == g, u, raw[k])
    v1, e1, w1 = sel[0], jnp.zeros_like(gidx), raw[0]
    for k in range(1, epg):
        better = sel[k] > v1
        v1 = jnp.where(better, sel[k], v1)
        e1 = jnp.where(better, float(k), e1)
        w1 = jnp.where(better, raw[k], w1)
    v2 = e2 = w2 = None
    for k in range(epg):
        cand = jnp.where(e1 == k, -jnp.inf, sel[k])
        if k == 0:
            v2, e2, w2 = cand, jnp.zeros_like(gidx), raw[0]
        else:
            better = cand > v2
            v2 = jnp.where(better, cand, v2)
            e2 = jnp.where(better, float(k), e2)
            w2 = jnp.where(better, raw[k], w2)
    tot = w1 + w2
    eidx = jnp.concatenate([gidx * epg + e1, gidx * epg + e2, jnp.zeros((6, tm), F32)], axis=0).astype(I32)
    gates_t = jnp.concatenate([w1 / tot, w2 / tot, jnp.zeros((126, tm), F32)], axis=0)
    return eidx, gates_t.T


def _post_mixer(x, y, m, g2, rw, rb):
    xn = x + m[2:3] * y
    h2 = _rms(xn, g2) * (1.0 + m[4:5]) + m[3:4]
    hi = h2.astype(BF16)
    lo = (h2 - hi.astype(F32)).astype(BF16)
    hw = _dot(hi, rw)
    logits = hw[:, 0:128] + (hw[:, 128:256] + _dot(lo, rw[:, 0:128]))
    eidx, gates = _route(logits, rb)
    return xn, hi, eidx, gates


def _mix0_kernel(xl_ref, xc_ref, sc_ref, prev_ref, next_ref, z_ref, mod_ref, cw_ref, on_ref, wo_ref,
                 g2_ref, rw_ref, rb_ref, h2buf_ref, *rest, ts, bs, s, cl, nb):
    del h2buf_ref
    o_refs = rest[:2 * nb]
    xn_ref, h2_ref, ei_ref, ga_ref = rest[2 * nb:]
    first, last = _seq_edges(pl.program_id(0), ts, bs, s, cl)
    row0 = pl.program_id(0) * ts
    batch = jnp.where(row0 < bs, row0 // s, (row0 - bs) // cl)
    ya = []
    for c in range(4):
        sl = slice(128 * c, 128 * c + 128)
        sg = slice(512 + 128 * c, 512 + 128 * c + 128)
        sx = slice(1024 + 128 * c, 1024 + 128 * c + 128)
        f32 = lambda ref, rows, cols: ref[rows, cols].astype(F32)
        u = f32(sc_ref, slice(None), sg) * f32(sc_ref, slice(None), sx)
        pr = jnp.where(first, 0.0, f32(prev_ref, slice(HALO - 1, HALO), sg) * f32(prev_ref, slice(HALO - 1, HALO), sx))
        nx = jnp.where(last, 0.0, f32(next_ref, slice(0, 1), sg) * f32(next_ref, slice(0, 1), sx))
        up, un = _shifted(u, pr, nx)
        w = cw_ref[:, sl]
        ya.append((f32(sc_ref, slice(None), sl) * (up * w[0:1] + u * w[1:2] + un * w[2:3])).astype(BF16))
    yb = []
    for h in range(DN_HEADS):
        hs = slice(DN_HD * h, DN_HD * h + DN_HD)
        o = o_refs[2 * nb - 2][:, hs].astype(F32) + o_refs[2 * nb - 1][:, hs].astype(F32)
        for b in range(nb - 2, -1, -1):
            o = jnp.where(batch == b, o_refs[2 * b][:, hs].astype(F32) + o_refs[2 * b + 1][:, hs].astype(F32), o)
        yb.append((_rms(o, on_ref[...]) * _silu(z_ref[:, hs].astype(F32))).astype(BF16))
    ycat = jnp.concatenate(ya + yb, axis=1)
    y = _dot(ycat, wo_ref[...])
    xn, h2, eidx, gates = _post_mixer(_token_rows(xl_ref, xc_ref, ts, bs), y, mod_ref[0], g2_ref[...], rw_ref[...],
                                      rb_ref[...])
    xn_ref[...] = xn
    h2_ref[...] = h2.astype(BF16)
    ei_ref[...] = eidx
    ga_ref[...] = gates


def _mix0(xl, xc, sc, o_pairs, z, mods, cw, on, wo, g2, rw, rb, h2buf, ts, bs, s, cl, nb):
    d = xl.shape[1]
    r = bs + xc.shape[0]
    kern = functools.partial(_mix0_kernel, ts=ts, bs=bs, s=s, cl=cl, nb=nb)

    def o_spec(b):
        def index(i):
            row0 = i * ts
            in_lat = row0 < bs
            owner = jnp.where(in_lat, row0 // s, (row0 - bs) // cl)
            own = jnp.where(in_lat, (row0 % s) // ts, s // ts + ((row0 - bs) % cl) // ts)
            nxt = jnp.where(in_lat, 0, s // ts)
            prv = jnp.where(in_lat, s // ts - 1, (s + cl) // ts - 1)
            return jnp.where(owner == b, own, jnp.where(owner < b, nxt, prv)), 0
        return pl.BlockSpec((ts, 512), index)

    mi = functools.partial(_mod_index, tm=ts, bs=bs, s=s, nb=nb)
    row = lambda w: pl.BlockSpec((ts, w), lambda i: (i, 0))
    full = lambda a: pl.BlockSpec(a.shape, lambda i: (0,) * a.ndim)
    return pl.pallas_call(
        kern,
        grid=(r // ts,),
        in_specs=_token_specs(ts, bs, d) + [row(1536)] + _halo_specs(ts, 1536, r) + [
            row(512),
            pl.BlockSpec((1, N_MOD, d), lambda i: (mi(i), 0, 0)),
            full(cw), full(on), full(wo), full(g2), full(rw), full(rb), pl.BlockSpec(memory_space=pl.ANY)]
        + [o_spec(b) for b in range(nb) for _ in range(2)],
        out_specs=[row(d), row(d), pl.BlockSpec((8, ts), lambda i: (0, i)), row(128)],
        out_shape=[jax.ShapeDtypeStruct((r, d), F32), jax.ShapeDtypeStruct(h2buf.shape, BF16),
                   jax.ShapeDtypeStruct((8, r), I32), jax.ShapeDtypeStruct((r, 128), F32)],
        input_output_aliases={13: 1},
        compiler_params=_cp(("parallel",)),
        name="mix0",
    )(xl, xc, sc, sc, sc, z, mods, cw, on, wo, g2, rw, rb, h2buf, *[a for pair in o_pairs for a in pair])


def _gmm_kernel(te_ref, tf_ref, tv_ref, ne_ref, sl_ref, x_ref, wg_hbm, wu_hbm, wd_hbm, y_ref,
                wbuf, wgb, wub, wdb, sem, *, layer):
    t = pl.program_id(0)

    def weight_copies(e, slot):
        return [pltpu.make_async_copy(w.at[layer, e], wbuf.at[slot, k], sem.at[slot, k])
                for k, w in enumerate((wg_hbm, wu_hbm, wd_hbm))]

    @pl.when(t == 0)
    def _():
        for cp in weight_copies(te_ref[0], sl_ref[0]):
            cp.start()

    @pl.when(tf_ref[t] == 1)
    def _():
        slot = sl_ref[t]
        for cp in weight_copies(te_ref[t], slot):
            cp.wait()
        wgb[...] = wbuf[slot, 0].astype(BF16)
        wub[...] = wbuf[slot, 1].astype(BF16)
        wdb[...] = wbuf[slot, 2].astype(BF16)

        @pl.when(ne_ref[t] >= 0)
        def _():
            for cp in weight_copies(ne_ref[t], 1 - slot):
                cp.start()

    @pl.when(tv_ref[t] == 1)
    def _():
        x = x_ref[...]
        a = (_silu(_dot(x, wgb[...])) * _dot(x, wub[...])).astype(BF16)
        y_ref[...] = _dot(a, wdb[...]).astype(BF16)

    @pl.when(tv_ref[t] == 0)
    def _():
        y_ref[...] = jnp.zeros_like(y_ref)


def _gmm(xs, w_gate, w_up, w_down, layer, tile_expert, tile_first, tile_valid, next_expert, tile_slot, tmm):
    p, d = xs.shape
    de = w_gate.shape[-1]
    assert d == de
    nt = p // tmm
    row = lambda t, *_: (t, 0)
    grid_spec = pltpu.PrefetchScalarGridSpec(
        num_scalar_prefetch=5,
        grid=(nt,),
        in_specs=[pl.BlockSpec((tmm, d), row)] + [pl.BlockSpec(memory_space=pl.ANY)] * 3,
        out_specs=pl.BlockSpec((tmm, d), row),
        scratch_shapes=[pltpu.VMEM((2, 3, d, de), F32),
                        pltpu.VMEM((d, de), BF16), pltpu.VMEM((d, de), BF16), pltpu.VMEM((de, d), BF16),
                        pltpu.SemaphoreType.DMA((2, 3))],
    )
    return pl.pallas_call(
        functools.partial(_gmm_kernel, layer=layer),
        grid_spec=grid_spec,
        out_shape=jax.ShapeDtypeStruct((p, d), BF16),
        compiler_params=_cp(("arbitrary",)),
        name="gmm",
    )(tile_expert, tile_first, tile_valid, next_expert, tile_slot, xs, w_gate, w_up, w_down)


def _moe(h2, t_tok, e_first, e_second, w_gate, w_up, w_down, layer, tmm):
    n = 2 * t_tok
    e_flat = jnp.concatenate([e_first, e_second])
    onehot = (e_flat[:, None] == jnp.arange(N_EXPERTS, dtype=I32)[None, :]).astype(I32)
    csum = jnp.cumsum(onehot, axis=0)
    counts = csum[-1]
    ptiles = (counts + tmm - 1) // tmm
    tile_end = jnp.cumsum(ptiles)
    dest = jnp.sum(onehot * (csum - 1 + ((tile_end - ptiles) * tmm)[None, :]), axis=1)
    nt = n // tmm + N_EXPERTS
    tid = jnp.arange(nt, dtype=I32)
    tile_valid = (tid < tile_end[-1]).astype(I32)
    te = jnp.minimum(jnp.sum((tile_end[None, :] <= tid[:, None]).astype(I32), axis=1), N_EXPERTS - 1)
    last_used = jnp.max(jnp.where(tile_valid == 1, te, 0))
    te = jnp.where(tile_valid == 1, te, last_used)
    tile_first = jnp.concatenate([jnp.ones((1,), I32), (te[1:] != te[:-1]).astype(I32)])
    tile_slot = (jnp.cumsum(tile_first) - 1) % 2
    experts = jnp.arange(N_EXPERTS, dtype=I32)
    later = (experts[None, :] > te[:, None]) & (counts[None, :] > 0)
    next_expert = jnp.min(jnp.where(later, experts[None, :], N_EXPERTS), axis=1)
    next_expert = jnp.where(next_expert < N_EXPERTS, next_expert, -1)
    order = jnp.argsort(e_flat, stable=True).astype(I32)
    seg_start = (tile_end - ptiles) * tmm
    shift = seg_start - (jnp.cumsum(counts) - counts)
    pos = tid[:, None] * tmm + jnp.arange(tmm, dtype=I32)[None, :]
    te_onehot = te[:, None] == jnp.arange(N_EXPERTS, dtype=I32)[None, :]
    per_tile = lambda v: jnp.sum(jnp.where(te_onehot, v[None, :], 0), axis=1, keepdims=True)
    used = (pos - per_tile(seg_start) < per_tile(counts)) & (tile_valid[:, None] == 1)
    src = (jnp.where(used, jnp.take(order, jnp.clip(pos - per_tile(shift), 0, n - 1)), pos) % t_tok).reshape(-1)
    assert h2.shape[0] >= nt * tmm
    xs = jnp.take(h2, src, axis=0, mode="clip")
    ys = _gmm(xs, w_gate, w_up, w_down, layer, te, tile_first, tile_valid, next_expert, tile_slot, tmm)
    return jnp.take(ys, dest, axis=0, mode="clip")


def _moe_combine(x, y1, y2, gates, m5):
    g = gates
    f = g[:, 0:1] * y1.astype(F32) + g[:, 1:2] * y2.astype(F32)
    return x + m5 * f


def _rope(x, cos, sin):
    n = x.shape[1]
    lane = lax.broadcasted_iota(I32, x.shape, 1)
    sw = jnp.where(lane % 32 < 16, pltpu.roll(x, n - 16, 1), pltpu.roll(x, 16, 1))
    reps = n // 128
    if reps > 1:
        cos = jnp.concatenate([cos] * reps, axis=1)
        sin = jnp.concatenate([sin] * reps, axis=1)
    return x * cos + sw * sin


def _inproj1_kernel(x_ref, y1_ref, y2_ref, ga_ref, m0_ref, m1_ref, g_ref, w_ref, cos_ref, sin_ref,
                    x1_ref, cq_ref, ckt_ref, cv_ref, dq_ref, dk_ref, dv_ref, *, tm, bs):
    m = m1_ref[0]
    in_lat = pl.program_id(0) * tm < bs
    scale = HEAD_DIM ** -0.5 * LOG2E
    for rows in (slice(0, tm // 2), slice(tm // 2, tm)):
        x1 = _moe_combine(x_ref[rows], y1_ref[rows], y2_ref[rows], ga_ref[rows], m0_ref[0][5:6])
        x1_ref[rows] = x1
        h = (_rms(x1, g_ref[...]) * (1.0 + m[1:2]) + m[0:1]).astype(BF16)
        cos, sin = cos_ref[rows], sin_ref[rows]
        cq = _dot(h, w_ref[:, 0:512])
        cq_ref[rows] = (jnp.where(in_lat, _rope(cq, cos, sin), cq) * scale).astype(BF16)
        ck = _dot(h, w_ref[:, 512:640])
        ckt_ref[:, rows] = jnp.where(in_lat, _rope(ck, cos, sin), ck).T.astype(BF16)
        cv_ref[rows] = _dot(h, w_ref[:, 640:768]).astype(BF16)
        dq_ref[rows] = (_dot(h, w_ref[:, 768:1280]) * scale).astype(BF16)
        dk_ref[rows] = _dot(h, w_ref[:, 1280:1792]).astype(BF16)
        dv_ref[rows] = _dot(h, w_ref[:, 1792:2304]).astype(BF16)


def _inproj1(x, y, gates, mods0, mods1, g, w, cos, sin, tm, bs, s, nb):
    r, d = x.shape
    second = pl.BlockSpec((tm, d), lambda i: (i + r // tm, 0))
    kern = functools.partial(_inproj1_kernel, tm=tm, bs=bs)
    mi = functools.partial(_mod_index, tm=tm, bs=bs, s=s, nb=nb)
    row = lambda wd: pl.BlockSpec((tm, wd), lambda i: (i, 0))
    modspec = pl.BlockSpec((1, N_MOD, d), lambda i: (mi(i), 0, 0))
    tab = pl.BlockSpec((tm, 128), lambda i: (jnp.where(i * tm < bs, (i * tm % s) // tm, 0), 0))
    shp = lambda wd, dt: jax.ShapeDtypeStruct((r, wd), dt)
    return pl.pallas_call(
        kern,
        grid=(r // tm,),
        in_specs=[row(d), row(d), second, row(128), modspec, modspec,
                  pl.BlockSpec((1, d), lambda i: (0, 0)), pl.BlockSpec(w.shape, lambda i: (0, 0)), tab, tab],
        out_specs=[row(d), row(512), pl.BlockSpec((128, tm), lambda i: (0, i)), row(128), row(512), row(512),
                   row(512)],
        out_shape=[shp(d, F32), shp(512, BF16), jax.ShapeDtypeStruct((128, r), BF16), shp(128, BF16),
                   shp(512, BF16), shp(512, BF16), shp(512, BF16)],
        compiler_params=_cp(("parallel",)),
        name="inproj1",
    )(x, y, y, gates, mods0, mods1, g, w, cos, sin)


def _swa_kernel(q_ref, ktp_ref, kto_ref, ktn_ref, ktx_ref, vp_ref, vo_ref, vn_ref, vx_ref, sink_ref, o_ref,
                *, nblk, cl):
    pair = pl.program_id(1)
    wb = SWA_BLOCK
    nloc = 3 * wb
    kto, vo, ktx, vx = kto_ref[...], vo_ref[...], ktx_ref[...], vx_ref[...]
    kts = (jnp.concatenate([ktp_ref[...], kto, ktx], axis=1), jnp.concatenate([kto, ktn_ref[...], ktx], axis=1))
    vvs = (jnp.concatenate([vp_ref[...], vo, vx], axis=0), jnp.concatenate([vo, vn_ref[...], vx], axis=0))
    a_i = lax.broadcasted_iota(I32, (2 * wb, nloc), 0) % wb
    c_i = lax.broadcasted_iota(I32, (2 * wb, nloc), 1)
    band = (c_i >= a_i) & (c_i <= a_i + 2 * SWA_WINDOW)
    oks = []
    for u in range(2):
        i = 2 * pair + u
        lo = jnp.where(i > 0, 0, wb)
        hi = jnp.where(i < nblk - 1, 3 * wb, 2 * wb)
        oks.append(band & (c_i >= lo) & (c_i < hi))
    half = lax.broadcasted_iota(I32, (1, 128), 1) // HEAD_DIM
    zero = jnp.zeros((), BF16)
    sink = sink_ref[...]
    items = [(u, g) for u in range(2) for g in range(4)]

    def scores(u, g):
        q2 = q_ref[wb * u:wb * u + wb, 128 * g:128 * g + 128]
        qst = jnp.concatenate([jnp.where(half == 0, q2, zero), jnp.where(half == 1, q2, zero)], axis=0)
        s_all = _dot(qst, kts[u])
        return jnp.concatenate([jnp.where(oks[u], s_all[:, 0:nloc], NEG), s_all[:, nloc:]], axis=1)

    def softmax(g, sc):
        sk = jnp.concatenate([jnp.broadcast_to(sink[0:1, 2 * g + a:2 * g + a + 1], (wb, 1)) for a in range(2)],
                             axis=0)
        m = jnp.maximum(jnp.max(sc, axis=-1, keepdims=True), sk)
        p = jnp.exp2(sc - m)
        return p.astype(BF16), jnp.sum(p, axis=-1, keepdims=True) + jnp.exp2(sk - m)

    def output(u, g, p, den):
        ost = _dot(p, vvs[u]) / den
        o_ref[wb * u:wb * u + wb, 128 * g:128 * g + 128] = jnp.where(half == 0, ost[0:wb], ost[wb:2 * wb]).astype(BF16)

    n = len(items)
    sc, pr = {}, {}
    for step in range(n + 2):
        if step < n:
            sc[step] = scores(*items[step])
        if 1 <= step < n + 1:
            pr[step - 1] = softmax(items[step - 1][1], sc.pop(step - 1))
        if step >= 2:
            output(*items[step - 2], *pr.pop(step - 2))


def _swa(cq, ckt, cv, sink_row, nb, s, cl, bs):
    wb = SWA_BLOCK
    nblk = s // wb
    assert nblk % 2 == 0
    npair = nblk // 2
    kern = functools.partial(_swa_kernel, nblk=nblk, cl=cl)
    prev = lambda b, j: b * nblk + jnp.maximum(2 * j - 1, 0)
    nxt = lambda b, j: b * nblk + jnp.minimum(2 * j + 2, nblk - 1)
    own = lambda b, j: b * npair + j
    return pl.pallas_call(
        kern,
        grid=(nb, npair),
        in_specs=[pl.BlockSpec((2 * wb, 512), lambda b, j: (own(b, j), 0)),
                  pl.BlockSpec((128, wb), lambda b, j: (0, prev(b, j))),
                  pl.BlockSpec((128, 2 * wb), lambda b, j: (0, own(b, j))),
                  pl.BlockSpec((128, wb), lambda b, j: (0, nxt(b, j))),
                  pl.BlockSpec((128, cl), lambda b, j: (0, bs // cl + b)),
                  pl.BlockSpec((wb, 128), lambda b, j: (prev(b, j), 0)),
                  pl.BlockSpec((2 * wb, 128), lambda b, j: (own(b, j), 0)),
                  pl.BlockSpec((wb, 128), lambda b, j: (nxt(b, j), 0)),
                  pl.BlockSpec((cl, 128), lambda b, j: (bs // cl + b, 0)),
                  pl.BlockSpec((1, 128), lambda b, j: (0, 0))],
        out_specs=pl.BlockSpec((2 * wb, 512), lambda b, j: (own(b, j), 0)),
        out_shape=jax.ShapeDtypeStruct((bs, 512), BF16),
        compiler_params=_cp(("parallel", "parallel")),
        name="swa",
    )(cq, ckt, ckt, ckt, ckt, cv, cv, cv, cv, sink_row)


def _na_kernel(q_ref, k_ref, v_ref, kx_ref, vx_ref, cols_ref, o_ref, bias_ref, *, rows, unroll):
    for h in range(2):
        for off in range(NA_KH):
            for i in range(NA_KH):
                bias_ref[h, off, :, GRID_W * i:GRID_W * (i + 1)] = cols_ref[h, off + i]
    half = lax.broadcasted_iota(I32, (1, 128), 1) // HEAD_DIM
    zero = jnp.zeros((), BF16)
    vx = vx_ref[...]
    kxt = kx_ref[...].astype(F32).T.astype(BF16)
    span = NA_KH * GRID_W

    def scores(r):
        rs = jnp.clip(r - NA_KH // 2, 0, rows - NA_KH)
        off = rs - r + NA_KH - 1
        q0 = pl.multiple_of(r * GRID_W, GRID_W)
        k0 = pl.multiple_of(rs * GRID_W, GRID_W)
        q2 = q_ref[pl.ds(q0, GRID_W), :]
        qst = jnp.concatenate([jnp.where(half == 0, q2, zero), jnp.where(half == 1, q2, zero)], axis=0)
        s_loc = (_dot_nt(qst, k_ref[pl.ds(k0, span), :])
                 + jnp.concatenate([bias_ref[0, off], bias_ref[1, off]], axis=0))
        return q0, k0, s_loc, _dot(qst, kxt)

    def softmax(s_loc, s_ctx):
        m = jnp.maximum(jnp.max(s_loc, axis=-1, keepdims=True), jnp.max(s_ctx, axis=-1, keepdims=True))
        p_loc = jnp.exp2(s_loc - m)
        p_ctx = jnp.exp2(s_ctx - m)
        den = jnp.sum(p_loc, axis=-1, keepdims=True) + jnp.sum(p_ctx, axis=-1, keepdims=True)
        return p_loc.astype(BF16), p_ctx.astype(BF16), den

    def output(q0, k0, p_loc, p_ctx, den):
        ost = (_dot(p_loc, v_ref[pl.ds(k0, span), :]) + _dot(p_ctx, vx)) / den
        o = jnp.where(half == 0, ost[0:GRID_W], ost[GRID_W:2 * GRID_W])
        o_ref[pl.ds(q0, GRID_W), :] = o.astype(BF16)

    def body(i, carry):
        sc, pr = {}, {}
        for step in range(unroll + 2):
            if step < unroll:
                sc[step] = scores(i * unroll + step)
            if 1 <= step < unroll + 1:
                q0, k0, s_loc, s_ctx = sc.pop(step - 1)
                pr[step - 1] = (q0, k0) + softmax(s_loc, s_ctx)
            if step >= 2:
                output(*pr.pop(step - 2))
        return carry

    lax.fori_loop(0, rows // unroll, body, 0)


def _na(dq, dk, dv, bias_cols, nb, s, cl, bs):
    rows = s // GRID_W
    unroll = 8
    assert rows % unroll == 0
    kern = functools.partial(_na_kernel, rows=rows, unroll=unroll)
    seq = pl.BlockSpec((s, 128), lambda b, p: (b, p))
    ctx = pl.BlockSpec((cl, 128), lambda b, p: (bs // cl + b, p))
    return pl.pallas_call(
        kern,
        grid=(nb, NA_HEADS // 2),
        in_specs=[seq, seq, seq, ctx, ctx,
                  pl.BlockSpec((2, 2 * NA_KH - 1, GRID_W, GRID_W), lambda b, p: (p, 0, 0, 0))],
        out_specs=seq,
        out_shape=jax.ShapeDtypeStruct((bs, 512), BF16),
        scratch_shapes=[pltpu.VMEM((2, NA_KH, GRID_W, NA_KH * GRID_W), F32)],
        compiler_params=_cp(("parallel", "parallel")),
        name="na",
    )(dq, dk, dv, dk, dv, bias_cols)


def _na_bias_cols(rpb):
    c = np.arange(GRID_W)
    qs = np.clip(c - NA_KW // 2, 0, GRID_W - NA_KW)
    kc = np.arange(GRID_W)
    ok = (kc[None, :] >= qs[:, None]) & (kc[None, :] < qs[:, None] + NA_KW)
    dc = np.clip(kc[None, :] - c[:, None] + NA_KW - 1, 0, 2 * NA_KW - 2)
    sel = (np.arange(2 * NA_KW - 1)[:, None, None] == dc[None]).astype(np.float32)
    cols = jnp.einsum("hab,bck->hack", rpb.astype(F32), sel, precision=HI)
    return jnp.where(ok[None, None], cols * LOG2E, NEG)


def _mix1_kernel(x_ref, oc_ref, od_ref, mod_ref, wo_ref, g2_ref, rw_ref, rb_ref, h2buf_ref,
                 xn_ref, h2_ref, ei_ref, ga_ref):
    del h2buf_ref
    y = _dot(oc_ref[...], wo_ref[0:512, :]) + _dot(od_ref[...], wo_ref[512:1024, :])
    xn, h2, eidx, gates = _post_mixer(x_ref[...], y, mod_ref[0], g2_ref[...], rw_ref[...], rb_ref[...])
    xn_ref[...] = xn
    h2_ref[...] = h2.astype(BF16)
    ei_ref[...] = eidx
    ga_ref[...] = gates


def _mix1(x, oc, od, mods, wo, g2, rw, rb, h2buf, tm, bs, s):
    d = x.shape[1]
    row = lambda w: pl.BlockSpec((tm, w), lambda i: (i, 0))
    full = lambda a: pl.BlockSpec(a.shape, lambda i: (0,) * a.ndim)
    return pl.pallas_call(
        _mix1_kernel,
        grid=(bs // tm,),
        in_specs=[row(d), row(512), row(512), pl.BlockSpec((1, N_MOD, d), lambda i: (i * tm // s, 0, 0)),
                  full(wo), full(g2), full(rw), full(rb), pl.BlockSpec(memory_space=pl.ANY)],
        out_specs=[row(d), row(d), pl.BlockSpec((8, tm), lambda i: (0, i)), row(128)],
        out_shape=[jax.ShapeDtypeStruct((bs, d), F32), jax.ShapeDtypeStruct(h2buf.shape, BF16),
                   jax.ShapeDtypeStruct((8, bs), I32), jax.ShapeDtypeStruct((bs, 128), F32)],
        input_output_aliases={8: 1},
        compiler_params=_cp(("parallel",)),
        name="mix1",
    )(x, oc, od, mods, wo, g2, rw, rb, h2buf)


def _final_kernel(x_ref, y1_ref, y2_ref, ga_ref, mod_ref, g_ref, o_ref):
    x = _moe_combine(x_ref[...], y1_ref[...], y2_ref[...], ga_ref[...], mod_ref[0][5:6])
    o_ref[...] = _rms(x, g_ref[...])


def _final(x, y, gates, mods, g, tm, s):
    r, d = x.shape
    second = pl.BlockSpec((tm, d), lambda i: (i + r // tm, 0))
    row = lambda w: pl.BlockSpec((tm, w), lambda i: (i, 0))
    return pl.pallas_call(
        _final_kernel,
        grid=(r // tm,),
        in_specs=[row(d), row(d), second, row(128), pl.BlockSpec((1, N_MOD, d), lambda i: (i * tm // s, 0, 0)),
                  pl.BlockSpec((1, d), lambda i: (0, 0))],
        out_specs=row(d),
        out_shape=jax.ShapeDtypeStruct((r, d), F32),
        compiler_params=_cp(("parallel",)),
        name="final",
    )(x, y, y, gates, mods, g)


def _rope_tables(s):
    nf = HEAD_DIM // 4
    t = np.arange(s)
    inv = ROPE_THETA ** (-np.arange(nf, dtype=np.float64) / nf)
    ar = (t // GRID_W)[:, None] * inv
    ac = (t % GRID_W)[:, None] * inv
    cos = np.concatenate([np.cos(ar), np.cos(ar), np.cos(ac), np.cos(ac)], axis=1)
    sin = np.concatenate([-np.sin(ar), np.sin(ar), -np.sin(ac), np.sin(ac)], axis=1)
    return (jnp.asarray(np.concatenate([cos, cos], axis=1), F32),
            jnp.asarray(np.concatenate([sin, sin], axis=1), F32))


def kernel(x, c, ctx, c_ctx, ada_w, ada_b, norm1_g, norm2_g, ev_w_in, ev_w_out, sc_conv_w, dn_conv_w, dn_a_log, dn_dt_bias, dn_onorm_g, od_w_in, od_w_out, swa_sink, na_rpb, router_w, router_b, moe_w_gate, moe_w_up, moe_w_down, final_g):
    nb, s, d = x.shape
    cl = ctx.shape[1]
    bs = nb * s
    tm = 512
    ts = 256
    tmm = 512
    assert d == 1024 and s % tm == 0 and (nb * cl) % tm == 0 and cl % ts == 0 and s % ts == 0
    assert s // GRID_W >= NA_KH and bs % cl == 0 and nb + 1 <= 8

    xl, xc = x.reshape(bs, d), ctx.reshape(nb * cl, d)
    cc = jnp.zeros((8, d), F32).at[:nb].set(c).at[nb].set(c_ctx)
    mods = _ada(cc, ada_w, ada_b).reshape(ada_w.shape[0], 8, N_MOD, d)
    rw32 = jnp.pad(router_w, ((0, 0), (0, 128 - N_EXPERTS)))
    rw_hi = rw32.astype(BF16)
    rw = jnp.concatenate([rw_hi, (rw32 - rw_hi.astype(F32)).astype(BF16)], axis=1)
    rb = router_b.reshape(N_EXPERTS, 1)
    row = lambda v: v.reshape(1, -1)

    w_in0 = jnp.pad(ev_w_in[0], ((0, 0), (0, 3712 - ev_w_in.shape[-1]))).astype(BF16)
    sc, qkv, z, bg = _inproj0(xl, xc, mods[0], row(norm1_g[0]), w_in0, tm, bs, s, nb)
    pad16 = lambda v: jnp.pad(v.reshape(-1), (8, 128 - 16)).reshape(1, 128)
    uf, ub, wf, wb, qf, qb, kf, kb, af, ab, gc = _dnchunk(qkv, bg, dn_conv_w[0], pad16(dn_a_log[0]),
                                                          pad16(dn_dt_bias[0]), ts, bs, s, cl)
    o_pairs = _dnscan(uf, ub, wf, wb, qf, qb, kf, kb, af, ab, gc, nb, s, cl, bs)
    r_all = bs + nb * cl
    moe_rows = lambda t: (2 * t // tmm + N_EXPERTS) * tmm
    x0, h2, ei, ga = _mix0(xl, xc, sc, o_pairs, z, mods[0], sc_conv_w[0], row(dn_onorm_g[0]),
                           ev_w_out[0].astype(BF16), row(norm2_g[0]), rw, rb,
                           jnp.zeros((moe_rows(r_all), d), BF16), ts, bs, s, cl, nb)
    y = _moe(h2, r_all, ei[0], ei[1], moe_w_gate, moe_w_up, moe_w_down, 0, tmm)

    perm = np.concatenate([np.arange(HEAD_DIM) + HEAD_DIM * (g + 4 * a) for g in range(4) for a in range(2)])
    w1 = od_w_in[0]
    w_in1 = jnp.concatenate([w1[:, 0:512][:, perm], w1[:, 512:]], axis=1).astype(BF16)
    wo1 = od_w_out[0]
    w_out1 = jnp.concatenate([wo1[0:512][perm], wo1[512:]], axis=0).astype(BF16)
    sink_row = jnp.pad(swa_sink[0][np.array([g + 4 * a for g in range(4) for a in range(2)])] * LOG2E,
                       (0, 128 - SWA_HEADS)).reshape(1, 128)
    cos, sin = _rope_tables(s)
    x1, cq, ckt, cv, dq, dk, dv = _inproj1(x0, y, ga, mods[0], mods[1], row(norm1_g[1]), w_in1, cos, sin,
                                           tm, bs, s, nb)
    oc = _swa(cq, ckt, cv, sink_row, nb, s, cl, bs)
    od = _na(dq, dk, dv, _na_bias_cols(na_rpb[0]), nb, s, cl, bs)
    assert moe_rows(bs) <= h2.shape[0]
    x2, h2, ei, ga = _mix1(x1, oc, od, mods[1], w_out1, row(norm2_g[1]), rw, rb, h2, tm, bs, s)
    y = _moe(h2, bs, ei[0], ei[1], moe_w_gate, moe_w_up, moe_w_down, 1, tmm)
    out = _final(x2, y, ga, mods[1], row(final_g), tm, s)
    return out.reshape(nb, s, d)
```

```python
import functools
import math

import numpy as np
import jax
import jax.numpy as jnp
from jax import lax
from jax.experimental import pallas as pl
from jax.experimental.pallas import tpu as pltpu

F32 = jnp.float32
BF16 = jnp.bfloat16
I32 = jnp.int32
HI = lax.Precision.HIGHEST

EPS = 1e-6
N_MOD = 6
GRID_W = 64
HEAD_DIM = 64
DN_HEADS = 4
DN_HD = 128
DN_CHUNK = 64
SWA_HEADS = 8
SWA_KV = 2
SWA_BLOCK = 128
SWA_WINDOW = 128
NA_HEADS = 8
NA_KH = 8
NA_KW = 16
ROPE_THETA = 10000.0
N_EXPERTS = 16
N_GROUPS = 4
NEG = -1e30
LOG2E = 1.4426950408889634
VMEM_LIMIT = 56 * 1024 * 1024


def _cp(sem, vmem=VMEM_LIMIT):
    return pltpu.CompilerParams(dimension_semantics=sem, vmem_limit_bytes=vmem)


def _dot(a, b, precision=None):
    return jnp.dot(a, b, preferred_element_type=F32, precision=precision)


def _dot_nt(a, b, precision=None):
    return lax.dot_general(a, b, (((1,), (1,)), ((), ())), preferred_element_type=F32, precision=precision)


def _dot_tn(a, b, precision=None):
    return lax.dot_general(a, b, (((0,), (0,)), ((), ())), preferred_element_type=F32, precision=precision)


def _silu(x):
    return x * jax.nn.sigmoid(x)


def _rms(x, g):
    return x * lax.rsqrt(jnp.mean(x * x, axis=-1, keepdims=True) + EPS) * g


def _ada_kernel(cc_ref, w_ref, b_ref, o_ref):
    a = _silu(cc_ref[...])
    o_ref[0] = _dot(a, w_ref[0], HI) + b_ref[0]


def _ada(cc, ada_w, ada_b):
    depth, d, n = ada_w.shape
    tn = 1536
    return pl.pallas_call(
        _ada_kernel,
        grid=(depth, n // tn),
        in_specs=[pl.BlockSpec((8, d), lambda l, j: (0, 0)),
                  pl.BlockSpec((1, d, tn), lambda l, j: (l, 0, j)),
                  pl.BlockSpec((1, 1, tn), lambda l, j: (l, 0, j))],
        out_specs=pl.BlockSpec((1, 8, tn), lambda l, j: (l, 0, j)),
        out_shape=jax.ShapeDtypeStruct((depth, 8, n), F32),
        compiler_params=_cp(("parallel", "parallel")),
        name="ada",
    )(cc, ada_w, ada_b.reshape(depth, 1, n))


def _mod_index(i, tm, bs, s, nb):
    row0 = i * tm
    return jnp.where(row0 < bs, row0 // s, nb)


def _token_specs(tm, bs, d):
    nlat = bs // tm
    return [pl.BlockSpec((tm, d), lambda i: (jnp.minimum(i, nlat - 1), 0)),
            pl.BlockSpec((tm, d), lambda i: (jnp.maximum(i - nlat, 0), 0))]


def _token_rows(xl_ref, xc_ref, tm, bs):
    return jnp.where(pl.program_id(0) * tm < bs, xl_ref[...], xc_ref[...])


def _inproj0_kernel(xl_ref, xc_ref, mod_ref, g_ref, w_ref, sc_ref, qkv_ref, z_ref, bg_ref, *, tm, bs):
    m = mod_ref[0]
    h = (_rms(_token_rows(xl_ref, xc_ref, tm, bs), g_ref[...]) * (1.0 + m[1:2]) + m[0:1]).astype(BF16)
    sc_ref[...] = _dot(h, w_ref[:, 0:1536]).astype(BF16)
    qkv_ref[...] = _dot(h, w_ref[:, 1536:3072]).astype(BF16)
    z_ref[...] = _dot(h, w_ref[:, 3072:3584]).astype(BF16)
    bg_ref[...] = _dot(h, w_ref[:, 3584:3712])


def _inproj0(xl, xc, mods, g, w, tm, bs, s, nb):
    d = xl.shape[1]
    r = bs + xc.shape[0]
    mi = functools.partial(_mod_index, tm=tm, bs=bs, s=s, nb=nb)
    return pl.pallas_call(
        functools.partial(_inproj0_kernel, tm=tm, bs=bs),
        grid=(r // tm,),
        in_specs=_token_specs(tm, bs, d) + [
            pl.BlockSpec((1, N_MOD, d), lambda i: (mi(i), 0, 0)),
            pl.BlockSpec((1, d), lambda i: (0, 0)),
            pl.BlockSpec(w.shape, lambda i: (0, 0))],
        out_specs=[pl.BlockSpec((tm, 1536), lambda i: (i, 0)),
                   pl.BlockSpec((tm, 1536), lambda i: (i, 0)),
                   pl.BlockSpec((tm, 512), lambda i: (i, 0)),
                   pl.BlockSpec((tm, 128), lambda i: (i, 0))],
        out_shape=[jax.ShapeDtypeStruct((r, 1536), BF16), jax.ShapeDtypeStruct((r, 1536), BF16),
                   jax.ShapeDtypeStruct((r, 512), BF16), jax.ShapeDtypeStruct((r, 128), F32)],
        compiler_params=_cp(("parallel",)),
        name="inproj0",
    )(xl, xc, mods, g, w)


def _seq_edges(i, ts, bs, s, cl):
    row0 = i * ts
    in_lat = row0 < bs
    r_in = jnp.where(in_lat, row0 % s, (row0 - bs) % cl)
    seqlen = jnp.where(in_lat, s, cl)
    return r_in == 0, r_in + ts == seqlen


def _shifted(x, prev_row, next_row):
    n = x.shape[0]
    rows = lax.broadcasted_iota(I32, x.shape, 0)
    xp = jnp.where(rows == 0, prev_row, pltpu.roll(x, 1, 0))
    xn = jnp.where(rows == n - 1, next_row, pltpu.roll(x, n - 1, 0))
    return xp, xn


HALO = 16


def _halo_specs(ts, width, r):
    nblk = r // HALO
    k = ts // HALO
    return [pl.BlockSpec((HALO, width), lambda i: (jnp.maximum(i * k - 1, 0), 0)),
            pl.BlockSpec((HALO, width), lambda i: (jnp.minimum((i + 1) * k, nblk - 1), 0))]


def _dnprep_tile(x_ref, prev_ref, next_ref, bg_ref, cw_ref, alog_ref, dt_ref,
                 q_ref, k_ref, v_ref, bga_ref, *, ts, bs, s, cl):
    first, last = _seq_edges(pl.program_id(0), ts, bs, s, cl)
    for c in range(12):
        sl = slice(128 * c, 128 * c + 128)
        x = x_ref[:, sl].astype(F32)
        pr = jnp.where(first, 0.0, prev_ref[HALO - 1:HALO, sl].astype(F32))
        nx = jnp.where(last, 0.0, next_ref[0:1, sl].astype(F32))
        xp, xn = _shifted(x, pr, nx)
        w = cw_ref[:, sl]
        y = _silu(xp * w[0:1] + x * w[1:2] + xn * w[2:3])
        hs = slice(128 * (c % 4), 128 * (c % 4) + 128)
        if c < 8:
            y = y * lax.rsqrt(jnp.sum(y * y, axis=-1, keepdims=True) + EPS)
        if c < 4:
            q_ref[:, hs] = y * DN_HD ** -0.5
        elif c < 8:
            k_ref[:, hs] = y
        else:
            v_ref[:, hs] = y
    b = bg_ref[...]
    cols = lax.broadcasted_iota(I32, b.shape, 1)
    beta = jax.nn.sigmoid(b)
    t = b + dt_ref[...]
    softplus = jnp.maximum(t, 0.0) + jnp.log1p(jnp.exp(-jnp.abs(t)))
    g = -jnp.exp(alog_ref[...]) * softplus
    bga_ref[...] = jnp.where(cols < 8, beta, jnp.where(cols < 16, g, 0.0))


def _dnchunk_kernel(x_ref, prev_ref, next_ref, bgraw_ref, cw_ref, alog_ref, dt_ref,
                    uf_ref, ub_ref, wf_ref, wb_ref, qf_ref, qb_ref, kf_ref, kb_ref, af_ref, ab_ref, gc_ref,
                    q_ref, k_ref, v_ref, bg_ref, *, nchunks, bs, s, cl):
    _dnprep_tile(x_ref, prev_ref, next_ref, bgraw_ref, cw_ref, alog_ref, dt_ref, q_ref, k_ref, v_ref, bg_ref,
                 ts=nchunks * DN_CHUNK, bs=bs, s=s, cl=cl)
    outs = ((uf_ref, wf_ref, qf_ref, kf_ref, af_ref), (ub_ref, wb_ref, qb_ref, kb_ref, ab_ref))
    c, nh = DN_CHUNK, DN_HEADS
    head_of_col = lax.broadcasted_iota(I32, (1, nh * c), 1) // c
    zero = jnp.zeros((), BF16)

    def block_diag(x):
        return jnp.concatenate([jnp.where(head_of_col == h, x, zero) for h in range(nh)], axis=0)

    chains = []
    for cc in range(nchunks):
        chains += _dnchunk_setup(slice(cc * c, (cc + 1) * c), q_ref, k_ref, v_ref, bg_ref, gc_ref)
    for ch in chains:
        ch["tm"] = ch["nmat"]
        nb16 = ch["nmat"].astype(BF16)
        ch["npow"] = _dot(nb16, block_diag(nb16))
    for _ in range(4):
        for ch in chains:
            nb16 = ch["npow"].astype(BF16)
            ch["both"] = _dot(jnp.concatenate([nb16, ch["tm"].astype(BF16)], axis=0), block_diag(nb16))
        for ch in chains:
            ch["tm"] = ch["tm"] + ch["npow"] + ch["both"][c:2 * c]
            ch["npow"] = ch["both"][0:c]
    for ch in chains:
        ch["both"] = _dot(ch["tm"].astype(BF16), block_diag(ch["npow"].astype(BF16)))
    for ch in chains:
        ch["tm"] = ch["tm"] + ch["npow"] + ch["both"]
    for ch in chains:
        ch["uw"] = ch["rhs"] + _dot(block_diag(ch["tm"].astype(BF16)), ch["rhs"].astype(BF16))
    for ch in chains:
        u_ref, w_ref, qd_ref, kd_ref, at_ref = outs[ch["d"]]
        rows, uw = ch["rows"], ch["uw"]
        for h in range(nh):
            hs = slice(DN_HD * h, DN_HD * h + DN_HD)
            rs = slice(c * h, c * h + c)
            u_ref[rows, hs] = uw[rs, 0:DN_HD]
            w_ref[rows, hs] = uw[rs, DN_HD:2 * DN_HD].astype(BF16)
            qd_ref[rows, hs] = ch["qd"][rs]
        kd_ref[slice(2 * rows.start, 2 * rows.stop)] = ch["kd"].T.astype(BF16)
        at_ref[rows] = ch["att"].astype(BF16)


def _dnchunk_setup(rows, q_ref, k_ref, v_ref, bg_ref, gc_ref):
    c, nh = DN_CHUNK, DN_HEADS
    n = c * nh
    bg = bg_ref[rows]
    i64 = lax.broadcasted_iota(I32, (c, c), 0)
    j64 = lax.broadcasted_iota(I32, (c, c), 1)
    cols = lax.broadcasted_iota(I32, bg.shape, 1)
    gcf = _dot((i64 >= j64).astype(F32), bg, HI)
    gcb = _dot((i64 <= j64).astype(F32), bg, HI)
    gc = jnp.where(cols >= 12, gcb, gcf)
    gc_ref[rows] = gc
    gct = gc.T
    ii = lax.broadcasted_iota(I32, (c, n), 0)
    jj = lax.broadcasted_iota(I32, (c, n), 1)
    head_of_col = jj // c
    jj = jj % c

    def stack(ref):
        return jnp.concatenate([ref[rows, DN_HD * h:DN_HD * h + DN_HD] for h in range(nh)], axis=0)

    def stacked_cols(arr, r0, r1, col0):
        return jnp.concatenate([jnp.broadcast_to(arr[r0:r1, col0 + h:col0 + h + 1], (c, DN_HD)) for h in range(nh)],
                               axis=0)

    def side_by_side_cols(arr, col0):
        out = arr[:, col0 + nh - 1:col0 + nh]
        for h in range(nh - 2, -1, -1):
            out = jnp.where(head_of_col == h, arr[:, col0 + h:col0 + h + 1], out)
        return out

    def diag_blocks(x):
        out = x[(nh - 1) * c:nh * c]
        for h in range(nh - 2, -1, -1):
            out = jnp.where(head_of_col == h, x[h * c:(h + 1) * c], out)
        return out

    kst, qst, vst = stack(k_ref), stack(q_ref), stack(v_ref)
    kb = kst.astype(BF16)
    kq = _dot_nt(jnp.concatenate([kb, qst.astype(BF16)], axis=0), kb)
    kkt, qkt = diag_blocks(kq[0:n]), diag_blocks(kq[n:2 * n])
    chains = []
    for d in range(2):
        incl = (ii >= jj) if d == 0 else (ii <= jj)
        strict = (ii > jj) if d == 0 else (ii < jj)
        last = c - 1 if d == 0 else 0
        grow = jnp.concatenate([gct[8 + 4 * d + h:9 + 4 * d + h, :] for h in range(nh)], axis=1)
        decay = jnp.exp(jnp.where(incl, side_by_side_cols(gc, 8 + 4 * d) - grow, NEG))
        nmat = jnp.where(strict, -(side_by_side_cols(bg, 4 * d) * kkt * decay), 0.0)
        b1 = stacked_cols(bg, 0, c, 4 * d)
        gcol = stacked_cols(gc, 0, c, 8 + 4 * d)
        glast = stacked_cols(gc, last, last + 1, 8 + 4 * d)
        e1 = jnp.exp(gcol)
        chains.append(dict(
            d=d, rows=rows, nmat=nmat,
            rhs=jnp.concatenate([b1 * vst, (b1 * e1) * kst], axis=1),
            qd=(qst * e1).astype(BF16),
            kd=kst * jnp.exp(glast - gcol),
            att=qkt * decay))
    return chains


def _dnchunk(qkv, bg, cw, alog_row, dt_row, ts, bs, s, cl):
    r = qkv.shape[0]
    nchunks = ts // DN_CHUNK
    c = ts
    row = lambda w: pl.BlockSpec((c, w), lambda i: (i, 0))
    shp = lambda w, dt: jax.ShapeDtypeStruct((r, w), dt)
    return pl.pallas_call(
        functools.partial(_dnchunk_kernel, nchunks=nchunks, bs=bs, s=s, cl=cl),
        grid=(r // c,),
        in_specs=[row(1536)] + _halo_specs(ts, 1536, r) + [
            row(128),
            pl.BlockSpec((3, 1536), lambda i: (0, 0)),
            pl.BlockSpec((1, 128), lambda i: (0, 0)),
            pl.BlockSpec((1, 128), lambda i: (0, 0))],
        scratch_shapes=[pltpu.VMEM((c, 512), F32)] * 3 + [pltpu.VMEM((c, 128), F32)],
        out_specs=[row(512)] * 6 + [pl.BlockSpec((2 * c, 256), lambda i: (i, 0))] * 2 + [row(256), row(256), row(128)],
        out_shape=([shp(512, F32)] * 2 + [shp(512, BF16)] * 4 + [jax.ShapeDtypeStruct((2 * r, 256), BF16)] * 2
                   + [shp(256, BF16)] * 2 + [shp(128, F32)]),
        compiler_params=_cp(("parallel",)),
        name="dnchunk",
    )(qkv, qkv, qkv, bg, cw, alog_row, dt_row)


def _dnscan_kernel(*refs, nsub, nb):
    nchain = 2 * nb
    ins, outs, s_ref = refs[:6 * nchain], refs[6 * nchain:7 * nchain], refs[7 * nchain]

    @pl.when(pl.program_id(0) == 0)
    def _():
        s_ref[...] = jnp.zeros_like(s_ref)

    c, nh = DN_CHUNK, DN_HEADS
    head_of_lane = lax.broadcasted_iota(I32, (1, nh * DN_HD), 1) // DN_HD
    head_of_col = lax.broadcasted_iota(I32, (1, nh * c), 1) // c
    zero = jnp.zeros((), BF16)

    def block_diag(tile, head_ids):
        return jnp.concatenate([jnp.where(head_ids == h, tile, zero) for h in range(nh)], axis=0)

    states = [s_ref[ci] for ci in range(nchain)]
    for sub in range(nsub):
        work = []
        for ci in range(nchain):
            d = ci % 2
            u_ref, w_ref, qd_ref, kd_ref, at_ref, g_ref = ins[6 * ci:6 * ci + 6]
            last = c - 1 if d == 0 else 0
            k = sub if d == 0 else nsub - 1 - sub
            rows = slice(c * k, c * k + c)
            g = g_ref[rows]
            decay = jnp.concatenate(
                [jnp.broadcast_to(jnp.exp(g[last:last + 1, 8 + 4 * d + h:9 + 4 * d + h]), (DN_HD, DN_HD))
                 for h in range(nh)], axis=0)
            ust = jnp.concatenate([u_ref[rows, DN_HD * h:DN_HD * h + DN_HD] for h in range(nh)], axis=0)
            stb = states[ci].astype(BF16)
            vnew = ust - _dot(block_diag(w_ref[rows], head_of_lane), stb)
            work.append((rows, decay, stb, vnew.astype(BF16)))
        for ci in range(nchain):
            u_ref, w_ref, qd_ref, kd_ref, at_ref, g_ref = ins[6 * ci:6 * ci + 6]
            rows, decay, stb, vnb = work[ci]
            o = (_dot(block_diag(qd_ref[rows], head_of_lane), stb)
                 + _dot(block_diag(at_ref[rows], head_of_col), vnb))
            kdt = kd_ref[slice(2 * rows.start, 2 * rows.stop)]
            states[ci] = states[ci] * decay + _dot(block_diag(kdt, head_of_col), vnb)
            for h in range(nh):
                outs[ci][rows, DN_HD * h:DN_HD * h + DN_HD] = o[c * h:c * h + c].astype(BF16)
    for ci in range(nchain):
        s_ref[ci] = states[ci]


def _dnscan(uf, ub, wf, wb, qf, qb, kf, kb, af, ab, gc, nb, s, cl, bs):
    nsub = 4
    c = DN_CHUNK * nsub
    assert cl % c == 0 and s % c == 0 and bs % c == 0
    ncc, ncl = cl // c, s // c
    ns = ncc + ncl

    def src_block(b, d):
        if d == 0:
            return lambda t: jnp.where(t < ncc, bs // c + b * ncc + t, b * ncl + t - ncc)
        return lambda t: jnp.where(t < ncc, bs // c + b * ncc + (ncc - 1 - t), b * ncl + (ncl - 1 - (t - ncc)))

    def dst_block(d):
        if d == 0:
            return lambda t: jnp.where(t < ncc, ncl + t, t - ncc)
        return lambda t: jnp.where(t < ncc, ncl + (ncc - 1 - t), ncl - 1 - (t - ncc))

    in_specs, args, out_specs = [], [], []
    for b in range(nb):
        for d, group in enumerate(((uf, wf, qf, kf, af, gc), (ub, wb, qb, kb, ab, gc))):
            idx = src_block(b, d)
            blk = lambda w, idx=idx: pl.BlockSpec((c, w), lambda t: (idx(t), 0))
            in_specs += [blk(512), blk(512), blk(512), pl.BlockSpec((2 * c, 256), lambda t, idx=idx: (idx(t), 0)),
                         blk(256), blk(128)]
            args += list(group)
            out_specs.append(pl.BlockSpec((c, 512), lambda t, f=dst_block(d): (f(t), 0)))
    outs = pl.pallas_call(
        functools.partial(_dnscan_kernel, nsub=nsub, nb=nb),
        grid=(ns,),
        in_specs=in_specs,
        out_specs=out_specs,
        out_shape=[jax.ShapeDtypeStruct((s + cl, 512), BF16)] * (2 * nb),
        scratch_shapes=[pltpu.VMEM((2 * nb, DN_HEADS * DN_HD, DN_HD), F32)],
        compiler_params=_cp(("arbitrary",)),
        name="dnscan",
    )(*args)
    return [(outs[2 * b], outs[2 * b + 1]) for b in range(nb)]


def _route(logits, bias_col):
    epg = N_EXPERTS // N_GROUPS
    tm = logits.shape[0]
    scores = jax.nn.sigmoid(logits.T[0:N_EXPERTS])
    gsel = scores + bias_col
    row = lambda a, k: a[k:k + 1]
    best = gidx = None
    for g in range(N_GROUPS):
        a = [row(gsel, epg * g + k) for k in range(epg)]
        m01, n01 = jnp.maximum(a[0], a[1]), jnp.minimum(a[0], a[1])
        m23, n23 = jnp.maximum(a[2], a[3]), jnp.minimum(a[2], a[3])
        gs = jnp.maximum(m01, m23) + jnp.maximum(jnp.minimum(m01, m23), jnp.maximum(n01, n23))
        if g == 0:
            best, gidx = gs, jnp.zeros_like(gs)
        else:
            better = gs > best
            best = jnp.where(better, gs, best)
            gidx = jnp.where(better, float(g), gidx)
    sel = [None] * epg
    raw = [None] * epg
    for g in range(N_GROUPS):
        for k in range(epg):
            v, u = row(gsel, epg * g + k), row(scores, epg * g + k)
            sel[k] = v if g == 0 else jnp.where(gidx == g, v, sel[k])
            raw[k] = u if g == 0 else jnp.where(gidx == g, u, raw[k])
    v1, e1, w1 = sel[0], jnp.zeros_like(gidx), raw[0]
    for k in range(1, epg):
        better = sel[k] > v1
        v1 = jnp.where(better, sel[k], v1)
        e1 = jnp.where(better, float(k), e1)
        w1 = jnp.where(better, raw[k], w1)
    v2 = e2 = w2 = None
    for k in range(epg):
        cand = jnp.where(e1 == k, -jnp.inf, sel[k])
        if k == 0:
            v2, e2, w2 = cand, jnp.zeros_like(gidx), raw[0]
        else:
            better = cand > v2
            v2 = jnp.where(better, cand, v2)
            e2 = jnp.where(better, float(k), e2)
            w2 = jnp.where(better, raw[k], w2)
    tot = w1 + w2
    eidx = jnp.concatenate([gidx * epg + e1, gidx * epg + e2, jnp.zeros((6, tm), F32)], axis=0).astype(I32)
    gates_t = jnp.concatenate([w1 / tot, w2 / tot, jnp.zeros((126, tm), F32)], axis=0)
    return eidx, gates_t.T


def _post_mixer(x, y, m, g2, rw, rb):
    xn = x + m[2:3] * y
    h2 = _rms(xn, g2) * (1.0 + m[4:5]) + m[3:4]
    hi = h2.astype(BF16)
    lo = (h2 - hi.astype(F32)).astype(BF16)
    hw = _dot(hi, rw)
    logits = hw[:, 0:128] + (hw[:, 128:256] + _dot(lo, rw[:, 0:128]))
    eidx, gates = _route(logits, rb)
    return xn, hi, eidx, gates


def _mix0_kernel(xl_ref, xc_ref, sc_ref, prev_ref, next_ref, z_ref, mod_ref, cw_ref, on_ref, wo_ref,
                 g2_ref, rw_ref, rb_ref, h2buf_ref, *rest, ts, bs, s, cl, nb):
    del h2buf_ref
    o_refs = rest[:2 * nb]
    xn_ref, h2_ref, ei_ref, ga_ref = rest[2 * nb:]
    first, last = _seq_edges(pl.program_id(0), ts, bs, s, cl)
    row0 = pl.program_id(0) * ts
    batch = jnp.where(row0 < bs, row0 // s, (row0 - bs) // cl)
    ya = []
    for c in range(4):
        sl = slice(128 * c, 128 * c + 128)
        sg = slice(512 + 128 * c, 512 + 128 * c + 128)
        sx = slice(1024 + 128 * c, 1024 + 128 * c + 128)
        f32 = lambda ref, rows, cols: ref[rows, cols].astype(F32)
        u = f32(sc_ref, slice(None), sg) * f32(sc_ref, slice(None), sx)
        pr = jnp.where(first, 0.0, f32(prev_ref, slice(HALO - 1, HALO), sg) * f32(prev_ref, slice(HALO - 1, HALO), sx))
        nx = jnp.where(last, 0.0, f32(next_ref, slice(0, 1), sg) * f32(next_ref, slice(0, 1), sx))
        up, un = _shifted(u, pr, nx)
        w = cw_ref[:, sl]
        ya.append((f32(sc_ref, slice(None), sl) * (up * w[0:1] + u * w[1:2] + un * w[2:3])).astype(BF16))
    yb = []
    for h in range(DN_HEADS):
        hs = slice(DN_HD * h, DN_HD * h + DN_HD)
        o = o_refs[2 * nb - 2][:, hs].astype(F32) + o_refs[2 * nb - 1][:, hs].astype(F32)
        for b in range(nb - 2, -1, -1):
            o = jnp.where(batch == b, o_refs[2 * b][:, hs].astype(F32) + o_refs[2 * b + 1][:, hs].astype(F32), o)
        yb.append((_rms(o, on_ref[...]) * _silu(z_ref[:, hs].astype(F32))).astype(BF16))
    ycat = jnp.concatenate(ya + yb, axis=1)
    y = _dot(ycat, wo_ref[...])
    xn, h2, eidx, gates = _post_mixer(_token_rows(xl_ref, xc_ref, ts, bs), y, mod_ref[0], g2_ref[...], rw_ref[...],
                                      rb_ref[...])
    xn_ref[...] = xn
    h2_ref[...] = h2.astype(BF16)
    ei_ref[...] = eidx
    ga_ref[...] = gates


def _mix0(xl, xc, sc, o_pairs, z, mods, cw, on, wo, g2, rw, rb, h2buf, ts, bs, s, cl, nb):
    d = xl.shape[1]
    r = bs + xc.shape[0]
    kern = functools.partial(_mix0_kernel, ts=ts, bs=bs, s=s, cl=cl, nb=nb)

    def o_spec(b):
        def index(i):
            row0 = i * ts
            in_lat = row0 < bs
            owner = jnp.where(in_lat, row0 // s, (row0 - bs) // cl)
            own = jnp.where(in_lat, (row0 % s) // ts, s // ts + ((row0 - bs) % cl) // ts)
            nxt = jnp.where(in_lat, 0, s // ts)
            prv = jnp.where(in_lat, s // ts - 1, (s + cl) // ts - 1)
            return jnp.where(owner == b, own, jnp.where(owner < b, nxt, prv)), 0
        return pl.BlockSpec((ts, 512), index)

    mi = functools.partial(_mod_index, tm=ts, bs=bs, s=s, nb=nb)
    row = lambda w: pl.BlockSpec((ts, w), lambda i: (i, 0))
    full = lambda a: pl.BlockSpec(a.shape, lambda i: (0,) * a.ndim)
    return pl.pallas_call(
        kern,
        grid=(r // ts,),
        in_specs=_token_specs(ts, bs, d) + [row(1536)] + _halo_specs(ts, 1536, r) + [
            row(512),
            pl.BlockSpec((1, N_MOD, d), lambda i: (mi(i), 0, 0)),
            full(cw), full(on), full(wo), full(g2), full(rw), full(rb), pl.BlockSpec(memory_space=pl.ANY)]
        + [o_spec(b) for b in range(nb) for _ in range(2)],
        out_specs=[row(d), row(d), pl.BlockSpec((8, ts), lambda i: (0, i)), row(128)],
        out_shape=[jax.ShapeDtypeStruct((r, d), F32), jax.ShapeDtypeStruct(h2buf.shape, BF16),
                   jax.ShapeDtypeStruct((8, r), I32), jax.ShapeDtypeStruct((r, 128), F32)],
        input_output_aliases={13: 1},
        compiler_params=_cp(("parallel",)),
        name="mix0",
    )(xl, xc, sc, sc, sc, z, mods, cw, on, wo, g2, rw, rb, h2buf, *[a for pair in o_pairs for a in pair])


def _gmm_kernel(te_ref, tl_ref, tv_ref, ne_ref, ne2_ref, sl_ref, x_ref, wg_hbm, wu_hbm, wd_hbm, y_ref,
                wbuf, wbf, sem, *, layer):
    t = pl.program_id(0)

    def weight_copies(e, slot):
        return [pltpu.make_async_copy(w.at[layer, e], wbuf.at[slot, k], sem.at[slot, k])
                for k, w in enumerate((wg_hbm, wu_hbm, wd_hbm))]

    def land(e, slot):
        for cp in weight_copies(e, slot):
            cp.wait()
        for k in range(3):
            wbf[slot, k] = wbuf[slot, k].astype(BF16)

    def start(e, slot):
        @pl.when(e >= 0)
        def _():
            for cp in weight_copies(e, slot):
                cp.start()

    @pl.when(t == 0)
    def _():
        slot = sl_ref[0]
        start(te_ref[0], slot)
        land(te_ref[0], slot)
        start(ne_ref[0], 1 - slot)

    slot = sl_ref[t]

    @pl.when(tv_ref[t] == 1)
    def _():
        x = x_ref[...]
        a = (_silu(_dot(x, wbf[slot, 0])) * _dot(x, wbf[slot, 1])).astype(BF16)
        y_ref[...] = _dot(a, wbf[slot, 2]).astype(BF16)

    @pl.when(tv_ref[t] == 0)
    def _():
        y_ref[...] = jnp.zeros_like(y_ref)

    @pl.when(tl_ref[t] == 1)
    def _():
        land(ne_ref[t], 1 - slot)
        start(ne2_ref[t], slot)


def _gmm(xs, w_gate, w_up, w_down, layer, tile_expert, tile_last, tile_valid, next_expert, next2_expert, tile_slot,
         tmm):
    p, d = xs.shape
    de = w_gate.shape[-1]
    assert d == de
    nt = p // tmm
    row = lambda t, *_: (t, 0)
    grid_spec = pltpu.PrefetchScalarGridSpec(
        num_scalar_prefetch=6,
        grid=(nt,),
        in_specs=[pl.BlockSpec((tmm, d), row)] + [pl.BlockSpec(memory_space=pl.ANY)] * 3,
        out_specs=pl.BlockSpec((tmm, d), row),
        scratch_shapes=[pltpu.VMEM((2, 3, d, de), F32), pltpu.VMEM((2, 3, d, de), BF16),
                        pltpu.SemaphoreType.DMA((2, 3))],
    )
    return pl.pallas_call(
        functools.partial(_gmm_kernel, layer=layer),
        grid_spec=grid_spec,
        out_shape=jax.ShapeDtypeStruct((p, d), BF16),
        compiler_params=_cp(("arbitrary",)),
        name="gmm",
    )(tile_expert, tile_last, tile_valid, next_expert, next2_expert, tile_slot, xs, w_gate, w_up, w_down)


def _moe(h2, t_tok, e_first, e_second, w_gate, w_up, w_down, layer, tmm):
    n = 2 * t_tok
    e_flat = jnp.concatenate([e_first, e_second])
    onehot = (e_flat[:, None] == jnp.arange(N_EXPERTS, dtype=I32)[None, :]).astype(I32)
    csum = jnp.cumsum(onehot, axis=0)
    counts = csum[-1]
    ptiles = (counts + tmm - 1) // tmm
    tile_end = jnp.cumsum(ptiles)
    dest = jnp.sum(onehot * (csum - 1 + ((tile_end - ptiles) * tmm)[None, :]), axis=1)
    nt = n // tmm + N_EXPERTS
    tid = jnp.arange(nt, dtype=I32)
    tile_valid = (tid < tile_end[-1]).astype(I32)
    te = jnp.minimum(jnp.sum((tile_end[None, :] <= tid[:, None]).astype(I32), axis=1), N_EXPERTS - 1)
    last_used = jnp.max(jnp.where(tile_valid == 1, te, 0))
    te = jnp.where(tile_valid == 1, te, last_used)
    tile_first = jnp.concatenate([jnp.ones((1,), I32), (te[1:] != te[:-1]).astype(I32)])
    tile_slot = (jnp.cumsum(tile_first) - 1) % 2
    experts = jnp.arange(N_EXPERTS, dtype=I32)
    def following(e):
        later = (experts[None, :] > e[:, None]) & (counts[None, :] > 0)
        nxt = jnp.min(jnp.where(later, experts[None, :], N_EXPERTS), axis=1)
        return jnp.where(nxt < N_EXPERTS, nxt, -1)

    next_expert = following(te)
    next2_expert = jnp.where(next_expert >= 0, following(jnp.maximum(next_expert, 0)), -1)
    tile_last = jnp.concatenate([tile_first[1:], jnp.zeros((1,), I32)])
    order = jnp.argsort(e_flat, stable=True).astype(I32)
    seg_start = (tile_end - ptiles) * tmm
    shift = seg_start - (jnp.cumsum(counts) - counts)
    pos = tid[:, None] * tmm + jnp.arange(tmm, dtype=I32)[None, :]
    te_onehot = te[:, None] == jnp.arange(N_EXPERTS, dtype=I32)[None, :]
    per_tile = lambda v: jnp.sum(jnp.where(te_onehot, v[None, :], 0), axis=1, keepdims=True)
    used = (pos - per_tile(seg_start) < per_tile(counts)) & (tile_valid[:, None] == 1)
    src = (jnp.where(used, jnp.take(order, jnp.clip(pos - per_tile(shift), 0, n - 1)), pos) % t_tok).reshape(-1)
    assert h2.shape[0] >= nt * tmm
    xs = jnp.take(h2, src, axis=0, mode="clip")
    ys = _gmm(xs, w_gate, w_up, w_down, layer, te, tile_last, tile_valid, next_expert, next2_expert, tile_slot, tmm)
    return jnp.take(ys, dest, axis=0, mode="clip")


def _moe_combine(x, y1, y2, gates, m5):
    g = gates
    f = g[:, 0:1] * y1.astype(F32) + g[:, 1:2] * y2.astype(F32)
    return x + m5 * f


def _rope(x, cos, sin):
    n = x.shape[1]
    lane = lax.broadcasted_iota(I32, x.shape, 1)
    sw = jnp.where(lane % 32 < 16, pltpu.roll(x, n - 16, 1), pltpu.roll(x, 16, 1))
    reps = n // 128
    if reps > 1:
        cos = jnp.concatenate([cos] * reps, axis=1)
        sin = jnp.concatenate([sin] * reps, axis=1)
    return x * cos + sw * sin


def _inproj1_kernel(x_ref, y1_ref, y2_ref, ga_ref, m0_ref, m1_ref, g_ref, w_ref, cos_ref, sin_ref,
                    x1_ref, cq_ref, ckt_ref, cv_ref, dq_ref, dk_ref, dv_ref, *, tm, bs):
    m = m1_ref[0]
    in_lat = pl.program_id(0) * tm < bs
    scale = HEAD_DIM ** -0.5 * LOG2E
    for rows in (slice(0, tm // 2), slice(tm // 2, tm)):
        x1 = _moe_combine(x_ref[rows], y1_ref[rows], y2_ref[rows], ga_ref[rows], m0_ref[0][5:6])
        x1_ref[rows] = x1
        h = (_rms(x1, g_ref[...]) * (1.0 + m[1:2]) + m[0:1]).astype(BF16)
        cos, sin = cos_ref[rows], sin_ref[rows]
        cq = _dot(h, w_ref[:, 0:512])
        cq_ref[rows] = (jnp.where(in_lat, _rope(cq, cos, sin), cq) * scale).astype(BF16)
        ck = _dot(h, w_ref[:, 512:640])
        ckt_ref[:, rows] = jnp.where(in_lat, _rope(ck, cos, sin), ck).T.astype(BF16)
        cv_ref[rows] = _dot(h, w_ref[:, 640:768]).astype(BF16)
        dq_ref[rows] = (_dot(h, w_ref[:, 768:1280]) * scale).astype(BF16)
        dk_ref[rows] = _dot(h, w_ref[:, 1280:1792]).astype(BF16)
        dv_ref[rows] = _dot(h, w_ref[:, 1792:2304]).astype(BF16)


def _inproj1(x, y, gates, mods0, mods1, g, w, cos, sin, tm, bs, s, nb):
    r, d = x.shape
    second = pl.BlockSpec((tm, d), lambda i: (i + r // tm, 0))
    kern = functools.partial(_inproj1_kernel, tm=tm, bs=bs)
    mi = functools.partial(_mod_index, tm=tm, bs=bs, s=s, nb=nb)
    row = lambda wd: pl.BlockSpec((tm, wd), lambda i: (i, 0))
    modspec = pl.BlockSpec((1, N_MOD, d), lambda i: (mi(i), 0, 0))
    tab = pl.BlockSpec((tm, 128), lambda i: (jnp.where(i * tm < bs, (i * tm % s) // tm, 0), 0))
    shp = lambda wd, dt: jax.ShapeDtypeStruct((r, wd), dt)
    return pl.pallas_call(
        kern,
        grid=(r // tm,),
        in_specs=[row(d), row(d), second, row(128), modspec, modspec,
                  pl.BlockSpec((1, d), lambda i: (0, 0)), pl.BlockSpec(w.shape, lambda i: (0, 0)), tab, tab],
        out_specs=[row(d), row(512), pl.BlockSpec((128, tm), lambda i: (0, i)), row(128), row(512), row(512),
                   row(512)],
        out_shape=[shp(d, F32), shp(512, BF16), jax.ShapeDtypeStruct((128, r), BF16), shp(128, BF16),
                   shp(512, BF16), shp(512, BF16), shp(512, BF16)],
        compiler_params=_cp(("parallel",)),
        name="inproj1",
    )(x, y, y, gates, mods0, mods1, g, w, cos, sin)


def _swa_kernel(q_ref, ktp_ref, kto_ref, ktn_ref, ktx_ref, vp_ref, vo_ref, vn_ref, vx_ref, sink_ref, o_ref,
                *, nblk, cl):
    pair = pl.program_id(1)
    wb = SWA_BLOCK
    nloc = 3 * wb
    kto, vo, ktx, vx = kto_ref[...], vo_ref[...], ktx_ref[...], vx_ref[...]
    kts = (jnp.concatenate([ktp_ref[...], kto, ktx], axis=1), jnp.concatenate([kto, ktn_ref[...], ktx], axis=1))
    vvs = (jnp.concatenate([vp_ref[...], vo, vx], axis=0), jnp.concatenate([vo, vn_ref[...], vx], axis=0))
    a_i = lax.broadcasted_iota(I32, (2 * wb, nloc), 0) % wb
    c_i = lax.broadcasted_iota(I32, (2 * wb, nloc), 1)
    band = (c_i >= a_i) & (c_i <= a_i + 2 * SWA_WINDOW)
    oks = []
    for u in range(2):
        i = 2 * pair + u
        lo = jnp.where(i > 0, 0, wb)
        hi = jnp.where(i < nblk - 1, 3 * wb, 2 * wb)
        oks.append(band & (c_i >= lo) & (c_i < hi))
    half = lax.broadcasted_iota(I32, (1, 128), 1) // HEAD_DIM
    zero = jnp.zeros((), BF16)
    sink = sink_ref[...]
    items = [(u, g) for u in range(2) for g in range(4)]

    def scores(u, g):
        q2 = q_ref[wb * u:wb * u + wb, 128 * g:128 * g + 128]
        qst = jnp.concatenate([jnp.where(half == 0, q2, zero), jnp.where(half == 1, q2, zero)], axis=0)
        s_all = _dot(qst, kts[u])
        return jnp.concatenate([jnp.where(oks[u], s_all[:, 0:nloc], NEG), s_all[:, nloc:]], axis=1)

    def softmax(g, sc):
        sk = jnp.concatenate([jnp.broadcast_to(sink[0:1, 2 * g + a:2 * g + a + 1], (wb, 1)) for a in range(2)],
                             axis=0)
        m = jnp.maximum(jnp.max(sc, axis=-1, keepdims=True), sk)
        p = jnp.exp2(sc - m)
        return p.astype(BF16), jnp.sum(p, axis=-1, keepdims=True) + jnp.exp2(sk - m)

    def output(u, g, p, den):
        ost = _dot(p, vvs[u]) / den
        o_ref[wb * u:wb * u + wb, 128 * g:128 * g + 128] = jnp.where(half == 0, ost[0:wb], ost[wb:2 * wb]).astype(BF16)

    n = len(items)
    sc, pr = {}, {}
    for step in range(n + 2):
        if step < n:
            sc[step] = scores(*items[step])
        if 1 <= step < n + 1:
            pr[step - 1] = softmax(items[step - 1][1], sc.pop(step - 1))
        if step >= 2:
            output(*items[step - 2], *pr.pop(step - 2))


def _swa(cq, ckt, cv, sink_row, nb, s, cl, bs):
    wb = SWA_BLOCK
    nblk = s // wb
    assert nblk % 2 == 0
    npair = nblk // 2
    kern = functools.partial(_swa_kernel, nblk=nblk, cl=cl)
    prev = lambda b, j: b * nblk + jnp.maximum(2 * j - 1, 0)
    nxt = lambda b, j: b * nblk + jnp.minimum(2 * j + 2, nblk - 1)
    own = lambda b, j: b * npair + j
    return pl.pallas_call(
        kern,
        grid=(nb, npair),
        in_specs=[pl.BlockSpec((2 * wb, 512), lambda b, j: (own(b, j), 0)),
                  pl.BlockSpec((128, wb), lambda b, j: (0, prev(b, j))),
                  pl.BlockSpec((128, 2 * wb), lambda b, j: (0, own(b, j))),
                  pl.BlockSpec((128, wb), lambda b, j: (0, nxt(b, j))),
                  pl.BlockSpec((128, cl), lambda b, j: (0, bs // cl + b)),
                  pl.BlockSpec((wb, 128), lambda b, j: (prev(b, j), 0)),
                  pl.BlockSpec((2 * wb, 128), lambda b, j: (own(b, j), 0)),
                  pl.BlockSpec((wb, 128), lambda b, j: (nxt(b, j), 0)),
                  pl.BlockSpec((cl, 128), lambda b, j: (bs // cl + b, 0)),
                  pl.BlockSpec((1, 128), lambda b, j: (0, 0))],
        out_specs=pl.BlockSpec((2 * wb, 512), lambda b, j: (own(b, j), 0)),
        out_shape=jax.ShapeDtypeStruct((bs, 512), BF16),
        compiler_params=_cp(("parallel", "parallel")),
        name="swa",
    )(cq, ckt, ckt, ckt, ckt, cv, cv, cv, cv, sink_row)


def _na_kernel(q_ref, k_ref, v_ref, kx_ref, vx_ref, cols_ref, o_ref, bias_ref, *, rows, unroll):
    for h in range(2):
        for off in range(NA_KH):
            for i in range(NA_KH):
                bias_ref[h, off, :, GRID_W * i:GRID_W * (i + 1)] = cols_ref[h, off + i]
    half = lax.broadcasted_iota(I32, (1, 128), 1) // HEAD_DIM
    zero = jnp.zeros((), BF16)
    vx = vx_ref[...]
    kxt = kx_ref[...].astype(F32).T.astype(BF16)
    span = NA_KH * GRID_W

    def scores(r):
        rs = jnp.clip(r - NA_KH // 2, 0, rows - NA_KH)
        off = rs - r + NA_KH - 1
        q0 = pl.multiple_of(r * GRID_W, GRID_W)
        k0 = pl.multiple_of(rs * GRID_W, GRID_W)
        q2 = q_ref[pl.ds(q0, GRID_W), :]
        qst = jnp.concatenate([jnp.where(half == 0, q2, zero), jnp.where(half == 1, q2, zero)], axis=0)
        s_loc = (_dot_nt(qst, k_ref[pl.ds(k0, span), :])
                 + jnp.concatenate([bias_ref[0, off], bias_ref[1, off]], axis=0))
        return q0, k0, s_loc, _dot(qst, kxt)

    def softmax(s_loc, s_ctx):
        m = jnp.maximum(jnp.max(s_loc, axis=-1, keepdims=True), jnp.max(s_ctx, axis=-1, keepdims=True))
        p_loc = jnp.exp2(s_loc - m)
        p_ctx = jnp.exp2(s_ctx - m)
        den = jnp.sum(p_loc, axis=-1, keepdims=True) + jnp.sum(p_ctx, axis=-1, keepdims=True)
        return p_loc.astype(BF16), p_ctx.astype(BF16), den

    def output(q0, k0, p_loc, p_ctx, den):
        ost = (_dot(p_loc, v_ref[pl.ds(k0, span), :]) + _dot(p_ctx, vx)) / den
        o = jnp.where(half == 0, ost[0:GRID_W], ost[GRID_W:2 * GRID_W])
        o_ref[pl.ds(q0, GRID_W), :] = o.astype(BF16)

    def body(i, carry):
        sc, pr = {}, {}
        for step in range(unroll + 2):
            if step < unroll:
                sc[step] = scores(i * unroll + step)
            if 1 <= step < unroll + 1:
                q0, k0, s_loc, s_ctx = sc.pop(step - 1)
                pr[step - 1] = (q0, k0) + softmax(s_loc, s_ctx)
            if step >= 2:
                output(*pr.pop(step - 2))
        return carry

    lax.fori_loop(0, rows // unroll, body, 0)


def _na(dq, dk, dv, bias_cols, nb, s, cl, bs):
    rows = s // GRID_W
    unroll = 8
    assert rows % unroll == 0
    kern = functools.partial(_na_kernel, rows=rows, unroll=unroll)
    seq = pl.BlockSpec((s, 128), lambda b, p: (b, p))
    ctx = pl.BlockSpec((cl, 128), lambda b, p: (bs // cl + b, p))
    return pl.pallas_call(
        kern,
        grid=(nb, NA_HEADS // 2),
        in_specs=[seq, seq, seq, ctx, ctx,
                  pl.BlockSpec((2, 2 * NA_KH - 1, GRID_W, GRID_W), lambda b, p: (p, 0, 0, 0))],
        out_specs=seq,
        out_shape=jax.ShapeDtypeStruct((bs, 512), BF16),
        scratch_shapes=[pltpu.VMEM((2, NA_KH, GRID_W, NA_KH * GRID_W), F32)],
        compiler_params=_cp(("parallel", "parallel")),
        name="na",
    )(dq, dk, dv, dk, dv, bias_cols)


def _na_bias_cols(rpb):
    c = np.arange(GRID_W)
    qs = np.clip(c - NA_KW // 2, 0, GRID_W - NA_KW)
    kc = np.arange(GRID_W)
    ok = (kc[None, :] >= qs[:, None]) & (kc[None, :] < qs[:, None] + NA_KW)
    dc = np.clip(kc[None, :] - c[:, None] + NA_KW - 1, 0, 2 * NA_KW - 2)
    sel = (np.arange(2 * NA_KW - 1)[:, None, None] == dc[None]).astype(np.float32)
    cols = jnp.einsum("hab,bck->hack", rpb.astype(F32), sel, precision=HI)
    return jnp.where(ok[None, None], cols * LOG2E, NEG)


def _mix1_kernel(x_ref, oc_ref, od_ref, mod_ref, wo_ref, g2_ref, rw_ref, rb_ref, h2buf_ref,
                 xn_ref, h2_ref, ei_ref, ga_ref):
    del h2buf_ref
    y = _dot(oc_ref[...], wo_ref[0:512, :]) + _dot(od_ref[...], wo_ref[512:1024, :])
    xn, h2, eidx, gates = _post_mixer(x_ref[...], y, mod_ref[0], g2_ref[...], rw_ref[...], rb_ref[...])
    xn_ref[...] = xn
    h2_ref[...] = h2.astype(BF16)
    ei_ref[...] = eidx
    ga_ref[...] = gates


def _mix1(x, oc, od, mods, wo, g2, rw, rb, h2buf, tm, bs, s):
    d = x.shape[1]
    row = lambda w: pl.BlockSpec((tm, w), lambda i: (i, 0))
    full = lambda a: pl.BlockSpec(a.shape, lambda i: (0,) * a.ndim)
    return pl.pallas_call(
        _mix1_kernel,
        grid=(bs // tm,),
        in_specs=[row(d), row(512), row(512), pl.BlockSpec((1, N_MOD, d), lambda i: (i * tm // s, 0, 0)),
                  full(wo), full(g2), full(rw), full(rb), pl.BlockSpec(memory_space=pl.ANY)],
        out_specs=[row(d), row(d), pl.BlockSpec((8, tm), lambda i: (0, i)), row(128)],
        out_shape=[jax.ShapeDtypeStruct((bs, d), F32), jax.ShapeDtypeStruct(h2buf.shape, BF16),
                   jax.ShapeDtypeStruct((8, bs), I32), jax.ShapeDtypeStruct((bs, 128), F32)],
        input_output_aliases={8: 1},
        compiler_params=_cp(("parallel",)),
        name="mix1",
    )(x, oc, od, mods, wo, g2, rw, rb, h2buf)


def _final_kernel(x_ref, y1_ref, y2_ref, ga_ref, mod_ref, g_ref, o_ref):
    x = _moe_combine(x_ref[...], y1_ref[...], y2_ref[...], ga_ref[...], mod_ref[0][5:6])
    o_ref[...] = _rms(x, g_ref[...])


def _final(x, y, gates, mods, g, tm, s):
    r, d = x.shape
    second = pl.BlockSpec((tm, d), lambda i: (i + r // tm, 0))
    row = lambda w: pl.BlockSpec((tm, w), lambda i: (i, 0))
    return pl.pallas_call(
        _final_kernel,
        grid=(r // tm,),
        in_specs=[row(d), row(d), second, row(128), pl.BlockSpec((1, N_MOD, d), lambda i: (i * tm // s, 0, 0)),
                  pl.BlockSpec((1, d), lambda i: (0, 0))],
        out_specs=row(d),
        out_shape=jax.ShapeDtypeStruct((r, d), F32),
        compiler_params=_cp(("parallel",)),
        name="final",
    )(x, y, y, gates, mods, g)


def _rope_tables(s):
    nf = HEAD_DIM // 4
    t = np.arange(s)
    inv = ROPE_THETA ** (-np.arange(nf, dtype=np.float64) / nf)
    ar = (t // GRID_W)[:, None] * inv
    ac = (t % GRID_W)[:, None] * inv
    cos = np.concatenate([np.cos(ar), np.cos(ar), np.cos(ac), np.cos(ac)], axis=1)
    sin = np.concatenate([-np.sin(ar), np.sin(ar), -np.sin(ac), np.sin(ac)], axis=1)
    return (jnp.asarray(np.concatenate([cos, cos], axis=1), F32),
            jnp.asarray(np.concatenate([sin, sin], axis=1), F32))


def kernel(x, c, ctx, c_ctx, ada_w, ada_b, norm1_g, norm2_g, ev_w_in, ev_w_out, sc_conv_w, dn_conv_w, dn_a_log, dn_dt_bias, dn_onorm_g, od_w_in, od_w_out, swa_sink, na_rpb, router_w, router_b, moe_w_gate, moe_w_up, moe_w_down, final_g):
    nb, s, d = x.shape
    cl = ctx.shape[1]
    bs = nb * s
    tm = 512
    ts = 256
    tmm = 512
    assert d == 1024 and s % tm == 0 and (nb * cl) % tm == 0 and cl % ts == 0 and s % ts == 0
    assert s // GRID_W >= NA_KH and bs % cl == 0 and nb + 1 <= 8

    xl, xc = x.reshape(bs, d), ctx.reshape(nb * cl, d)
    cc = jnp.zeros((8, d), F32).at[:nb].set(c).at[nb].set(c_ctx)
    mods = _ada(cc, ada_w, ada_b).reshape(ada_w.shape[0], 8, N_MOD, d)
    rw32 = jnp.pad(router_w, ((0, 0), (0, 128 - N_EXPERTS)))
    rw_hi = rw32.astype(BF16)
    rw = jnp.concatenate([rw_hi, (rw32 - rw_hi.astype(F32)).astype(BF16)], axis=1)
    rb = router_b.reshape(N_EXPERTS, 1)
    row = lambda v: v.reshape(1, -1)

    w_in0 = jnp.pad(ev_w_in[0], ((0, 0), (0, 3712 - ev_w_in.shape[-1]))).astype(BF16)
    sc, qkv, z, bg = _inproj0(xl, xc, mods[0], row(norm1_g[0]), w_in0, tm, bs, s, nb)
    pad16 = lambda v: jnp.pad(v.reshape(-1), (8, 128 - 16)).reshape(1, 128)
    uf, ub, wf, wb, qf, qb, kf, kb, af, ab, gc = _dnchunk(qkv, bg, dn_conv_w[0], pad16(dn_a_log[0]),
                                                          pad16(dn_dt_bias[0]), ts, bs, s, cl)
    o_pairs = _dnscan(uf, ub, wf, wb, qf, qb, kf, kb, af, ab, gc, nb, s, cl, bs)
    r_all = bs + nb * cl
    moe_rows = lambda t: (2 * t // tmm + N_EXPERTS) * tmm
    x0, h2, ei, ga = _mix0(xl, xc, sc, o_pairs, z, mods[0], sc_conv_w[0], row(dn_onorm_g[0]),
                           ev_w_out[0].astype(BF16), row(norm2_g[0]), rw, rb,
                           jnp.zeros((moe_rows(r_all), d), BF16), ts, bs, s, cl, nb)
    y = _moe(h2, r_all, ei[0], ei[1], moe_w_gate, moe_w_up, moe_w_down, 0, tmm)

    perm = np.concatenate([np.arange(HEAD_DIM) + HEAD_DIM * (g + 4 * a) for g in range(4) for a in range(2)])
    w1 = od_w_in[0]
    w_in1 = jnp.concatenate([w1[:, 0:512][:, perm], w1[:, 512:]], axis=1).astype(BF16)
    wo1 = od_w_out[0]
    w_out1 = jnp.concatenate([wo1[0:512][perm], wo1[512:]], axis=0).astype(BF16)
    sink_row = jnp.pad(swa_sink[0][np.array([g + 4 * a for g in range(4) for a in range(2)])] * LOG2E,
                       (0, 128 - SWA_HEADS)).reshape(1, 128)
    cos, sin = _rope_tables(s)
    x1, cq, ckt, cv, dq, dk, dv = _inproj1(x0, y, ga, mods[0], mods[1], row(norm1_g[1]), w_in1, cos, sin,
                                           tm, bs, s, nb)
    oc = _swa(cq, ckt, cv, sink_row, nb, s, cl, bs)
    od = _na(dq, dk, dv, _na_bias_cols(na_rpb[0]), nb, s, cl, bs)
    assert moe_rows(bs) <= h2.shape[0]
    x2, h2, ei, ga = _mix1(x1, oc, od, mods[1], w_out1, row(norm2_g[1]), rw, rb, h2, tm, bs, s)
    y = _moe(h2, bs, ei[0], ei[1], moe_w_gate, moe_w_up, moe_w_down, 1, tmm)
    out = _final(x2, y, ga, mods[1], row(final_g), tm, s)
    return out.reshape(nb, s, d)
```

```python
import functools
import math

import numpy as np
import jax
import jax.numpy as jnp
from jax import lax
from jax.experimental import pallas as pl
from jax.experimental.pallas import tpu as pltpu

F32 = jnp.float32
BF16 = jnp.bfloat16
I32 = jnp.int32
HI = lax.Precision.HIGHEST

EPS = 1e-6
N_MOD = 6
GRID_W = 64
HEAD_DIM = 64
DN_HEADS = 4
DN_HD = 128
DN_CHUNK = 64
SWA_HEADS = 8
SWA_KV = 2
SWA_BLOCK = 128
SWA_WINDOW = 128
NA_HEADS = 8
NA_KH = 8
NA_KW = 16
ROPE_THETA = 10000.0
N_EXPERTS = 16
N_GROUPS = 4
NEG = -1e30
LOG2E = 1.4426950408889634
VMEM_LIMIT = 56 * 1024 * 1024


def _cp(sem, vmem=VMEM_LIMIT):
    return pltpu.CompilerParams(dimension_semantics=sem, vmem_limit_bytes=vmem)


def _dot(a, b, precision=None):
    return jnp.dot(a, b, preferred_element_type=F32, precision=precision)


def _dot_nt(a, b, precision=None):
    return lax.dot_general(a, b, (((1,), (1,)), ((), ())), preferred_element_type=F32, precision=precision)


def _dot_tn(a, b, precision=None):
    return lax.dot_general(a, b, (((0,), (0,)), ((), ())), preferred_element_type=F32, precision=precision)


def _silu(x):
    return x * jax.nn.sigmoid(x)


def _rms(x, g):
    return x * lax.rsqrt(jnp.mean(x * x, axis=-1, keepdims=True) + EPS) * g


def _ada_kernel(cc_ref, w_ref, b_ref, o_ref):
    a = _silu(cc_ref[...])
    o_ref[0] = _dot(a, w_ref[0], HI) + b_ref[0]


def _ada(cc, ada_w, ada_b):
    depth, d, n = ada_w.shape
    tn = 1536
    return pl.pallas_call(
        _ada_kernel,
        grid=(depth, n // tn),
        in_specs=[pl.BlockSpec((8, d), lambda l, j: (0, 0)),
                  pl.BlockSpec((1, d, tn), lambda l, j: (l, 0, j)),
                  pl.BlockSpec((1, 1, tn), lambda l, j: (l, 0, j))],
        out_specs=pl.BlockSpec((1, 8, tn), lambda l, j: (l, 0, j)),
        out_shape=jax.ShapeDtypeStruct((depth, 8, n), F32),
        compiler_params=_cp(("parallel", "parallel")),
        name="ada",
    )(cc, ada_w, ada_b.reshape(depth, 1, n))


def _mod_index(i, tm, bs, s, nb):
    row0 = i * tm
    return jnp.where(row0 < bs, row0 // s, nb)


def _token_specs(tm, bs, d):
    nlat = bs // tm
    return [pl.BlockSpec((tm, d), lambda i: (jnp.minimum(i, nlat - 1), 0)),
            pl.BlockSpec((tm, d), lambda i: (jnp.maximum(i - nlat, 0), 0))]


def _token_rows(xl_ref, xc_ref, tm, bs):
    return jnp.where(pl.program_id(0) * tm < bs, xl_ref[...], xc_ref[...])


def _inproj0_kernel(xl_ref, xc_ref, mod_ref, g_ref, w_ref, sc_ref, qkv_ref, z_ref, bg_ref, *, tm, bs):
    m = mod_ref[0]
    in_lat = pl.program_id(0) * tm < bs
    for rows in (slice(0, tm // 2), slice(tm // 2, tm)):
        x = jnp.where(in_lat, xl_ref[rows], xc_ref[rows])
        h = (_rms(x, g_ref[...]) * (1.0 + m[1:2]) + m[0:1]).astype(BF16)
        sc_ref[rows] = _dot(h, w_ref[:, 0:1536]).astype(BF16)
        qkv_ref[rows] = _dot(h, w_ref[:, 1536:3072]).astype(BF16)
        z_ref[rows] = _dot(h, w_ref[:, 3072:3584]).astype(BF16)
        bg_ref[rows] = _dot(h, w_ref[:, 3584:3712])


def _inproj0(xl, xc, mods, g, w, tm, bs, s, nb):
    d = xl.shape[1]
    r = bs + xc.shape[0]
    mi = functools.partial(_mod_index, tm=tm, bs=bs, s=s, nb=nb)
    return pl.pallas_call(
        functools.partial(_inproj0_kernel, tm=tm, bs=bs),
        grid=(r // tm,),
        in_specs=_token_specs(tm, bs, d) + [
            pl.BlockSpec((1, N_MOD, d), lambda i: (mi(i), 0, 0)),
            pl.BlockSpec((1, d), lambda i: (0, 0)),
            pl.BlockSpec(w.shape, lambda i: (0, 0))],
        out_specs=[pl.BlockSpec((tm, 1536), lambda i: (i, 0)),
                   pl.BlockSpec((tm, 1536), lambda i: (i, 0)),
                   pl.BlockSpec((tm, 512), lambda i: (i, 0)),
                   pl.BlockSpec((tm, 128), lambda i: (i, 0))],
        out_shape=[jax.ShapeDtypeStruct((r, 1536), BF16), jax.ShapeDtypeStruct((r, 1536), BF16),
                   jax.ShapeDtypeStruct((r, 512), BF16), jax.ShapeDtypeStruct((r, 128), F32)],
        compiler_params=_cp(("parallel",)),
        name="inproj0",
    )(xl, xc, mods, g, w)


def _seq_edges(i, ts, bs, s, cl):
    row0 = i * ts
    in_lat = row0 < bs
    r_in = jnp.where(in_lat, row0 % s, (row0 - bs) % cl)
    seqlen = jnp.where(in_lat, s, cl)
    return r_in == 0, r_in + ts == seqlen


def _shifted(x, prev_row, next_row):
    n = x.shape[0]
    rows = lax.broadcasted_iota(I32, x.shape, 0)
    xp = jnp.where(rows == 0, prev_row, pltpu.roll(x, 1, 0))
    xn = jnp.where(rows == n - 1, next_row, pltpu.roll(x, n - 1, 0))
    return xp, xn


HALO = 16


def _halo_specs(ts, width, r):
    nblk = r // HALO
    k = ts // HALO
    return [pl.BlockSpec((HALO, width), lambda i: (jnp.maximum(i * k - 1, 0), 0)),
            pl.BlockSpec((HALO, width), lambda i: (jnp.minimum((i + 1) * k, nblk - 1), 0))]


def _dnprep_tile(x_ref, prev_ref, next_ref, bg_ref, cw_ref, alog_ref, dt_ref,
                 q_ref, k_ref, v_ref, bga_ref, *, ts, bs, s, cl):
    first, last = _seq_edges(pl.program_id(0), ts, bs, s, cl)
    for c in range(12):
        sl = slice(128 * c, 128 * c + 128)
        x = x_ref[:, sl].astype(F32)
        pr = jnp.where(first, 0.0, prev_ref[HALO - 1:HALO, sl].astype(F32))
        nx = jnp.where(last, 0.0, next_ref[0:1, sl].astype(F32))
        xp, xn = _shifted(x, pr, nx)
        w = cw_ref[:, sl]
        y = _silu(xp * w[0:1] + x * w[1:2] + xn * w[2:3])
        hs = slice(128 * (c % 4), 128 * (c % 4) + 128)
        if c < 8:
            y = y * lax.rsqrt(jnp.sum(y * y, axis=-1, keepdims=True) + EPS)
        if c < 4:
            q_ref[:, hs] = y * DN_HD ** -0.5
        elif c < 8:
            k_ref[:, hs] = y
        else:
            v_ref[:, hs] = y
    b = bg_ref[...]
    cols = lax.broadcasted_iota(I32, b.shape, 1)
    beta = jax.nn.sigmoid(b)
    t = b + dt_ref[...]
    softplus = jnp.maximum(t, 0.0) + jnp.log1p(jnp.exp(-jnp.abs(t)))
    g = -jnp.exp(alog_ref[...]) * softplus
    bga_ref[...] = jnp.where(cols < 8, beta, jnp.where(cols < 16, g, 0.0))


def _dnchunk_kernel(x_ref, prev_ref, next_ref, bgraw_ref, cw_ref, alog_ref, dt_ref,
                    uf_ref, ub_ref, wf_ref, wb_ref, qf_ref, qb_ref, kf_ref, kb_ref, af_ref, ab_ref, gc_ref,
                    q_ref, k_ref, v_ref, bg_ref, *, nchunks, bs, s, cl):
    _dnprep_tile(x_ref, prev_ref, next_ref, bgraw_ref, cw_ref, alog_ref, dt_ref, q_ref, k_ref, v_ref, bg_ref,
                 ts=nchunks * DN_CHUNK, bs=bs, s=s, cl=cl)
    outs = ((uf_ref, wf_ref, qf_ref, kf_ref, af_ref), (ub_ref, wb_ref, qb_ref, kb_ref, ab_ref))
    c, nh = DN_CHUNK, DN_HEADS
    head_of_col = lax.broadcasted_iota(I32, (1, nh * c), 1) // c
    zero = jnp.zeros((), BF16)

    def block_diag(x):
        return jnp.concatenate([jnp.where(head_of_col == h, x, zero) for h in range(nh)], axis=0)

    chains = []
    for cc in range(nchunks):
        chains += _dnchunk_setup(slice(cc * c, (cc + 1) * c), q_ref, k_ref, v_ref, bg_ref, gc_ref)
    for ch in chains:
        ch["tm"] = ch["nmat"]
        nb16 = ch["nmat"].astype(BF16)
        ch["npow"] = _dot(nb16, block_diag(nb16))
    for _ in range(4):
        for ch in chains:
            nb16 = ch["npow"].astype(BF16)
            ch["both"] = _dot(jnp.concatenate([nb16, ch["tm"].astype(BF16)], axis=0), block_diag(nb16))
        for ch in chains:
            ch["tm"] = ch["tm"] + ch["npow"] + ch["both"][c:2 * c]
            ch["npow"] = ch["both"][0:c]
    for ch in chains:
        ch["both"] = _dot(ch["tm"].astype(BF16), block_diag(ch["npow"].astype(BF16)))
    for ch in chains:
        ch["tm"] = ch["tm"] + ch["npow"] + ch["both"]
    for ch in chains:
        ch["uw"] = ch["rhs"] + _dot(block_diag(ch["tm"].astype(BF16)), ch["rhs"].astype(BF16))
    for ch in chains:
        u_ref, w_ref, qd_ref, kd_ref, at_ref = outs[ch["d"]]
        rows, uw = ch["rows"], ch["uw"]
        for h in range(nh):
            hs = slice(DN_HD * h, DN_HD * h + DN_HD)
            rs = slice(c * h, c * h + c)
            u_ref[rows, hs] = uw[rs, 0:DN_HD]
            w_ref[rows, hs] = uw[rs, DN_HD:2 * DN_HD].astype(BF16)
            qd_ref[rows, hs] = ch["qd"][rs]
        kd_ref[slice(2 * rows.start, 2 * rows.stop)] = ch["kd"].T.astype(BF16)
        at_ref[rows] = ch["att"].astype(BF16)


def _dnchunk_setup(rows, q_ref, k_ref, v_ref, bg_ref, gc_ref):
    c, nh = DN_CHUNK, DN_HEADS
    n = c * nh
    bg = bg_ref[rows]
    i64 = lax.broadcasted_iota(I32, (c, c), 0)
    j64 = lax.broadcasted_iota(I32, (c, c), 1)
    cols = lax.broadcasted_iota(I32, bg.shape, 1)
    gcf = _dot((i64 >= j64).astype(F32), bg, HI)
    gcb = _dot((i64 <= j64).astype(F32), bg, HI)
    gc = jnp.where(cols >= 12, gcb, gcf)
    gc_ref[rows] = gc
    gct = gc.T
    ii = lax.broadcasted_iota(I32, (c, n), 0)
    jj = lax.broadcasted_iota(I32, (c, n), 1)
    head_of_col = jj // c
    jj = jj % c

    def stack(ref):
        return jnp.concatenate([ref[rows, DN_HD * h:DN_HD * h + DN_HD] for h in range(nh)], axis=0)

    def stacked_cols(arr, r0, r1, col0):
        return jnp.concatenate([jnp.broadcast_to(arr[r0:r1, col0 + h:col0 + h + 1], (c, DN_HD)) for h in range(nh)],
                               axis=0)

    def side_by_side_cols(arr, col0):
        out = arr[:, col0 + nh - 1:col0 + nh]
        for h in range(nh - 2, -1, -1):
            out = jnp.where(head_of_col == h, arr[:, col0 + h:col0 + h + 1], out)
        return out

    def diag_blocks(x):
        out = x[(nh - 1) * c:nh * c]
        for h in range(nh - 2, -1, -1):
            out = jnp.where(head_of_col == h, x[h * c:(h + 1) * c], out)
        return out

    kst, qst, vst = stack(k_ref), stack(q_ref), stack(v_ref)
    kb = kst.astype(BF16)
    kq = _dot_nt(jnp.concatenate([kb, qst.astype(BF16)], axis=0), kb)
    kkt, qkt = diag_blocks(kq[0:n]), diag_blocks(kq[n:2 * n])
    chains = []
    for d in range(2):
        incl = (ii >= jj) if d == 0 else (ii <= jj)
        strict = (ii > jj) if d == 0 else (ii < jj)
        last = c - 1 if d == 0 else 0
        grow = jnp.concatenate([gct[8 + 4 * d + h:9 + 4 * d + h, :] for h in range(nh)], axis=1)
        decay = jnp.exp(jnp.where(incl, side_by_side_cols(gc, 8 + 4 * d) - grow, NEG))
        nmat = jnp.where(strict, -(side_by_side_cols(bg, 4 * d) * kkt * decay), 0.0)
        b1 = stacked_cols(bg, 0, c, 4 * d)
        gcol = stacked_cols(gc, 0, c, 8 + 4 * d)
        glast = stacked_cols(gc, last, last + 1, 8 + 4 * d)
        e1 = jnp.exp(gcol)
        chains.append(dict(
            d=d, rows=rows, nmat=nmat,
            rhs=jnp.concatenate([b1 * vst, (b1 * e1) * kst], axis=1),
            qd=(qst * e1).astype(BF16),
            kd=kst * jnp.exp(glast - gcol),
            att=qkt * decay))
    return chains


def _dnchunk(qkv, bg, cw, alog_row, dt_row, ts, bs, s, cl):
    r = qkv.shape[0]
    nchunks = ts // DN_CHUNK
    c = ts
    row = lambda w: pl.BlockSpec((c, w), lambda i: (i, 0))
    shp = lambda w, dt: jax.ShapeDtypeStruct((r, w), dt)
    return pl.pallas_call(
        functools.partial(_dnchunk_kernel, nchunks=nchunks, bs=bs, s=s, cl=cl),
        grid=(r // c,),
        in_specs=[row(1536)] + _halo_specs(ts, 1536, r) + [
            row(128),
            pl.BlockSpec((3, 1536), lambda i: (0, 0)),
            pl.BlockSpec((1, 128), lambda i: (0, 0)),
            pl.BlockSpec((1, 128), lambda i: (0, 0))],
        scratch_shapes=[pltpu.VMEM((c, 512), F32)] * 3 + [pltpu.VMEM((c, 128), F32)],
        out_specs=[row(512)] * 6 + [pl.BlockSpec((2 * c, 256), lambda i: (i, 0))] * 2 + [row(256), row(256), row(128)],
        out_shape=([shp(512, F32)] * 2 + [shp(512, BF16)] * 4 + [jax.ShapeDtypeStruct((2 * r, 256), BF16)] * 2
                   + [shp(256, BF16)] * 2 + [shp(128, F32)]),
        compiler_params=_cp(("parallel",)),
        name="dnchunk",
    )(qkv, qkv, qkv, bg, cw, alog_row, dt_row)


def _dnscan_kernel(*refs, nsub, nb):
    nchain = 2 * nb
    ins, outs, s_ref = refs[:6 * nchain], refs[6 * nchain:7 * nchain], refs[7 * nchain]

    @pl.when(pl.program_id(0) == 0)
    def _():
        s_ref[...] = jnp.zeros_like(s_ref)

    c, nh = DN_CHUNK, DN_HEADS
    head_of_lane = lax.broadcasted_iota(I32, (1, nh * DN_HD), 1) // DN_HD
    head_of_col = lax.broadcasted_iota(I32, (1, nh * c), 1) // c
    zero = jnp.zeros((), BF16)

    def block_diag(tile, head_ids):
        return jnp.concatenate([jnp.where(head_ids == h, tile, zero) for h in range(nh)], axis=0)

    states = [s_ref[ci] for ci in range(nchain)]
    for sub in range(nsub):
        work = []
        for ci in range(nchain):
            d = ci % 2
            u_ref, w_ref, qd_ref, kd_ref, at_ref, g_ref = ins[6 * ci:6 * ci + 6]
            last = c - 1 if d == 0 else 0
            k = sub if d == 0 else nsub - 1 - sub
            rows = slice(c * k, c * k + c)
            g = g_ref[rows]
            decay = jnp.concatenate(
                [jnp.broadcast_to(jnp.exp(g[last:last + 1, 8 + 4 * d + h:9 + 4 * d + h]), (DN_HD, DN_HD))
                 for h in range(nh)], axis=0)
            ust = jnp.concatenate([u_ref[rows, DN_HD * h:DN_HD * h + DN_HD] for h in range(nh)], axis=0)
            stb = states[ci].astype(BF16)
            vnew = ust - _dot(block_diag(w_ref[rows], head_of_lane), stb)
            work.append((rows, decay, stb, vnew.astype(BF16)))
        for ci in range(nchain):
            u_ref, w_ref, qd_ref, kd_ref, at_ref, g_ref = ins[6 * ci:6 * ci + 6]
            rows, decay, stb, vnb = work[ci]
            o = (_dot(block_diag(qd_ref[rows], head_of_lane), stb)
                 + _dot(block_diag(at_ref[rows], head_of_col), vnb))
            kdt = kd_ref[slice(2 * rows.start, 2 * rows.stop)]
            states[ci] = states[ci] * decay + _dot(block_diag(kdt, head_of_col), vnb)
            for h in range(nh):
                outs[ci][rows, DN_HD * h:DN_HD * h + DN_HD] = o[c * h:c * h + c].astype(BF16)
    for ci in range(nchain):
        s_ref[ci] = states[ci]


def _dnscan(uf, ub, wf, wb, qf, qb, kf, kb, af, ab, gc, nb, s, cl, bs):
    nsub = 4
    c = DN_CHUNK * nsub
    assert cl % c == 0 and s % c == 0 and bs % c == 0
    ncc, ncl = cl // c, s // c
    ns = ncc + ncl

    def src_block(b, d):
        if d == 0:
            return lambda t: jnp.where(t < ncc, bs // c + b * ncc + t, b * ncl + t - ncc)
        return lambda t: jnp.where(t < ncc, bs // c + b * ncc + (ncc - 1 - t), b * ncl + (ncl - 1 - (t - ncc)))

    def dst_block(d):
        if d == 0:
            return lambda t: jnp.where(t < ncc, ncl + t, t - ncc)
        return lambda t: jnp.where(t < ncc, ncl + (ncc - 1 - t), ncl - 1 - (t - ncc))

    in_specs, args, out_specs = [], [], []
    for b in range(nb):
        for d, group in enumerate(((uf, wf, qf, kf, af, gc), (ub, wb, qb, kb, ab, gc))):
            idx = src_block(b, d)
            blk = lambda w, idx=idx: pl.BlockSpec((c, w), lambda t: (idx(t), 0))
            in_specs += [blk(512), blk(512), blk(512), pl.BlockSpec((2 * c, 256), lambda t, idx=idx: (idx(t), 0)),
                         blk(256), blk(128)]
            args += list(group)
            out_specs.append(pl.BlockSpec((c, 512), lambda t, f=dst_block(d): (f(t), 0)))
    outs = pl.pallas_call(
        functools.partial(_dnscan_kernel, nsub=nsub, nb=nb),
        grid=(ns,),
        in_specs=in_specs,
        out_specs=out_specs,
        out_shape=[jax.ShapeDtypeStruct((s + cl, 512), BF16)] * (2 * nb),
        scratch_shapes=[pltpu.VMEM((2 * nb, DN_HEADS * DN_HD, DN_HD), F32)],
        compiler_params=_cp(("arbitrary",)),
        name="dnscan",
    )(*args)
    return [(outs[2 * b], outs[2 * b + 1]) for b in range(nb)]


def _route(logits, bias_col):
    epg = N_EXPERTS // N_GROUPS
    tm = logits.shape[0]
    scores = jax.nn.sigmoid(logits.T[0:N_EXPERTS])
    gsel = scores + bias_col
    row = lambda a, k: a[k:k + 1]
    best = gidx = None
    for g in range(N_GROUPS):
        a = [row(gsel, epg * g + k) for k in range(epg)]
        m01, n01 = jnp.maximum(a[0], a[1]), jnp.minimum(a[0], a[1])
        m23, n23 = jnp.maximum(a[2], a[3]), jnp.minimum(a[2], a[3])
        gs = jnp.maximum(m01, m23) + jnp.maximum(jnp.minimum(m01, m23), jnp.maximum(n01, n23))
        if g == 0:
            best, gidx = gs, jnp.zeros_like(gs)
        else:
            better = gs > best
            best = jnp.where(better, gs, best)
            gidx = jnp.where(better, float(g), gidx)
    sel = [None] * epg
    raw = [None] * epg
    for g in range(N_GROUPS):
        for k in range(epg):
            v, u = row(gsel, epg * g + k), row(scores, epg * g + k)
            sel[k] = v if g == 0 else jnp.where(gidx == g, v, sel[k])
            raw[k] = u if g == 0 else jnp.where(gidx == g, u, raw[k])
    v1, e1, w1 = sel[0], jnp.zeros_like(gidx), raw[0]
    for k in range(1, epg):
        better = sel[k] > v1
        v1 = jnp.where(better, sel[k], v1)
        e1 = jnp.where(better, float(k), e1)
        w1 = jnp.where(better, raw[k], w1)
    v2 = e2 = w2 = None
    for k in range(epg):
        cand = jnp.where(e1 == k, -jnp.inf, sel[k])
        if k == 0:
            v2, e2, w2 = cand, jnp.zeros_like(gidx), raw[0]
        else:
            better = cand > v2
            v2 = jnp.where(better, cand, v2)
            e2 = jnp.where(better, float(k), e2)
            w2 = jnp.where(better, raw[k], w2)
    tot = w1 + w2
    eidx = jnp.concatenate([gidx * epg + e1, gidx * epg + e2, jnp.zeros((6, tm), F32)], axis=0).astype(I32)
    gates_t = jnp.concatenate([w1 / tot, w2 / tot, jnp.zeros((126, tm), F32)], axis=0)
    return eidx, gates_t.T


def _post_mixer(x, y, m, g2, rw, rb):
    xn = x + m[2:3] * y
    h2 = _rms(xn, g2) * (1.0 + m[4:5]) + m[3:4]
    hi = h2.astype(BF16)
    lo = (h2 - hi.astype(F32)).astype(BF16)
    hw = _dot(hi, rw)
    logits = hw[:, 0:128] + (hw[:, 128:256] + _dot(lo, rw[:, 0:128]))
    eidx, gates = _route(logits, rb)
    return xn, hi, eidx, gates


def _mix0_kernel(xl_ref, xc_ref, sc_ref, prev_ref, next_ref, z_ref, mod_ref, cw_ref, on_ref, wo_ref,
                 g2_ref, rw_ref, rb_ref, h2buf_ref, *rest, ts, bs, s, cl, nb):
    del h2buf_ref
    o_refs = rest[:2 * nb]
    xn_ref, h2_ref, ei_ref, ga_ref = rest[2 * nb:]
    first, last = _seq_edges(pl.program_id(0), ts, bs, s, cl)
    row0 = pl.program_id(0) * ts
    batch = jnp.where(row0 < bs, row0 // s, (row0 - bs) // cl)
    ya = []
    for c in range(4):
        sl = slice(128 * c, 128 * c + 128)
        sg = slice(512 + 128 * c, 512 + 128 * c + 128)
        sx = slice(1024 + 128 * c, 1024 + 128 * c + 128)
        f32 = lambda ref, rows, cols: ref[rows, cols].astype(F32)
        u = f32(sc_ref, slice(None), sg) * f32(sc_ref, slice(None), sx)
        pr = jnp.where(first, 0.0, f32(prev_ref, slice(HALO - 1, HALO), sg) * f32(prev_ref, slice(HALO - 1, HALO), sx))
        nx = jnp.where(last, 0.0, f32(next_ref, slice(0, 1), sg) * f32(next_ref, slice(0, 1), sx))
        up, un = _shifted(u, pr, nx)
        w = cw_ref[:, sl]
        ya.append((f32(sc_ref, slice(None), sl) * (up * w[0:1] + u * w[1:2] + un * w[2:3])).astype(BF16))
    yb = []
    for h in range(DN_HEADS):
        hs = slice(DN_HD * h, DN_HD * h + DN_HD)
        o = o_refs[2 * nb - 2][:, hs].astype(F32) + o_refs[2 * nb - 1][:, hs].astype(F32)
        for b in range(nb - 2, -1, -1):
            o = jnp.where(batch == b, o_refs[2 * b][:, hs].astype(F32) + o_refs[2 * b + 1][:, hs].astype(F32), o)
        yb.append((_rms(o, on_ref[...]) * _silu(z_ref[:, hs].astype(F32))).astype(BF16))
    ycat = jnp.concatenate(ya + yb, axis=1)
    y = _dot(ycat, wo_ref[...])
    xn, h2, eidx, gates = _post_mixer(_token_rows(xl_ref, xc_ref, ts, bs), y, mod_ref[0], g2_ref[...], rw_ref[...],
                                      rb_ref[...])
    xn_ref[...] = xn
    h2_ref[...] = h2.astype(BF16)
    ei_ref[...] = eidx
    ga_ref[...] = gates


def _mix0(xl, xc, sc, o_pairs, z, mods, cw, on, wo, g2, rw, rb, h2buf, ts, bs, s, cl, nb):
    d = xl.shape[1]
    r = bs + xc.shape[0]
    kern = functools.partial(_mix0_kernel, ts=ts, bs=bs, s=s, cl=cl, nb=nb)

    def o_spec(b):
        def index(i):
            row0 = i * ts
            in_lat = row0 < bs
            owner = jnp.where(in_lat, row0 // s, (row0 - bs) // cl)
            own = jnp.where(in_lat, (row0 % s) // ts, s // ts + ((row0 - bs) % cl) // ts)
            nxt = jnp.where(in_lat, 0, s // ts)
            prv = jnp.where(in_lat, s // ts - 1, (s + cl) // ts - 1)
            return jnp.where(owner == b, own, jnp.where(owner < b, nxt, prv)), 0
        return pl.BlockSpec((ts, 512), index)

    mi = functools.partial(_mod_index, tm=ts, bs=bs, s=s, nb=nb)
    row = lambda w: pl.BlockSpec((ts, w), lambda i: (i, 0))
    full = lambda a: pl.BlockSpec(a.shape, lambda i: (0,) * a.ndim)
    return pl.pallas_call(
        kern,
        grid=(r // ts,),
        in_specs=_token_specs(ts, bs, d) + [row(1536)] + _halo_specs(ts, 1536, r) + [
            row(512),
            pl.BlockSpec((1, N_MOD, d), lambda i: (mi(i), 0, 0)),
            full(cw), full(on), full(wo), full(g2), full(rw), full(rb), pl.BlockSpec(memory_space=pl.ANY)]
        + [o_spec(b) for b in range(nb) for _ in range(2)],
        out_specs=[row(d), row(d), pl.BlockSpec((8, ts), lambda i: (0, i)), row(128)],
        out_shape=[jax.ShapeDtypeStruct((r, d), F32), jax.ShapeDtypeStruct(h2buf.shape, BF16),
                   jax.ShapeDtypeStruct((8, r), I32), jax.ShapeDtypeStruct((r, 128), F32)],
        input_output_aliases={13: 1},
        compiler_params=_cp(("parallel",)),
        name="mix0",
    )(xl, xc, sc, sc, sc, z, mods, cw, on, wo, g2, rw, rb, h2buf, *[a for pair in o_pairs for a in pair])


def _gmm_kernel(te_ref, tf_ref, tv_ref, ne_ref, sl_ref, x_ref, wg_hbm, wu_hbm, wd_hbm, y_ref,
                wbuf, wgb, wub, wdb, sem, *, layer):
    t = pl.program_id(0)

    def weight_copies(e, slot):
        return [pltpu.make_async_copy(w.at[layer, e], wbuf.at[slot, k], sem.at[slot, k])
                for k, w in enumerate((wg_hbm, wu_hbm, wd_hbm))]

    @pl.when(t == 0)
    def _():
        for cp in weight_copies(te_ref[0], sl_ref[0]):
            cp.start()

    @pl.when(tf_ref[t] == 1)
    def _():
        slot = sl_ref[t]
        for cp in weight_copies(te_ref[t], slot):
            cp.wait()
        wgb[...] = wbuf[slot, 0].astype(BF16)
        wub[...] = wbuf[slot, 1].astype(BF16)
        wdb[...] = wbuf[slot, 2].astype(BF16)

        @pl.when(ne_ref[t] >= 0)
        def _():
            for cp in weight_copies(ne_ref[t], 1 - slot):
                cp.start()

    @pl.when(tv_ref[t] == 1)
    def _():
        x = x_ref[...]
        a = (_silu(_dot(x, wgb[...])) * _dot(x, wub[...])).astype(BF16)
        y_ref[...] = _dot(a, wdb[...]).astype(BF16)

    @pl.when(tv_ref[t] == 0)
    def _():
        y_ref[...] = jnp.zeros_like(y_ref)


def _gmm(xs, w_gate, w_up, w_down, layer, tile_expert, tile_first, tile_valid, next_expert, tile_slot, tmm):
    p, d = xs.shape
    de = w_gate.shape[-1]
    assert d == de
    nt = p // tmm
    row = lambda t, *_: (t, 0)
    grid_spec = pltpu.PrefetchScalarGridSpec(
        num_scalar_prefetch=5,
        grid=(nt,),
        in_specs=[pl.BlockSpec((tmm, d), row)] + [pl.BlockSpec(memory_space=pl.ANY)] * 3,
        out_specs=pl.BlockSpec((tmm, d), row),
        scratch_shapes=[pltpu.VMEM((2, 3, d, de), F32),
                        pltpu.VMEM((d, de), BF16), pltpu.VMEM((d, de), BF16), pltpu.VMEM((de, d), BF16),
                        pltpu.SemaphoreType.DMA((2, 3))],
    )
    return pl.pallas_call(
        functools.partial(_gmm_kernel, layer=layer),
        grid_spec=grid_spec,
        out_shape=jax.ShapeDtypeStruct((p, d), BF16),
        compiler_params=_cp(("arbitrary",)),
        name="gmm",
    )(tile_expert, tile_first, tile_valid, next_expert, tile_slot, xs, w_gate, w_up, w_down)


def _moe(h2, t_tok, e_first, e_second, w_gate, w_up, w_down, layer, tmm):
    n = 2 * t_tok
    e_flat = jnp.concatenate([e_first, e_second])
    onehot = (e_flat[:, None] == jnp.arange(N_EXPERTS, dtype=I32)[None, :]).astype(I32)
    csum = jnp.cumsum(onehot, axis=0)
    counts = csum[-1]
    ptiles = (counts + tmm - 1) // tmm
    tile_end = jnp.cumsum(ptiles)
    dest = jnp.sum(onehot * (csum - 1 + ((tile_end - ptiles) * tmm)[None, :]), axis=1)
    nt = n // tmm + N_EXPERTS
    tid = jnp.arange(nt, dtype=I32)
    tile_valid = (tid < tile_end[-1]).astype(I32)
    te = jnp.minimum(jnp.sum((tile_end[None, :] <= tid[:, None]).astype(I32), axis=1), N_EXPERTS - 1)
    last_used = jnp.max(jnp.where(tile_valid == 1, te, 0))
    te = jnp.where(tile_valid == 1, te, last_used)
    tile_first = jnp.concatenate([jnp.ones((1,), I32), (te[1:] != te[:-1]).astype(I32)])
    tile_slot = (jnp.cumsum(tile_first) - 1) % 2
    experts = jnp.arange(N_EXPERTS, dtype=I32)
    later = (experts[None, :] > te[:, None]) & (counts[None, :] > 0)
    next_expert = jnp.min(jnp.where(later, experts[None, :], N_EXPERTS), axis=1)
    next_expert = jnp.where(next_expert < N_EXPERTS, next_expert, -1)
    order = jnp.argsort(e_flat, stable=True).astype(I32)
    seg_start = (tile_end - ptiles) * tmm
    shift = seg_start - (jnp.cumsum(counts) - counts)
    pos = tid[:, None] * tmm + jnp.arange(tmm, dtype=I32)[None, :]
    te_onehot = te[:, None] == jnp.arange(N_EXPERTS, dtype=I32)[None, :]
    per_tile = lambda v: jnp.sum(jnp.where(te_onehot, v[None, :], 0), axis=1, keepdims=True)
    used = (pos - per_tile(seg_start) < per_tile(counts)) & (tile_valid[:, None] == 1)
    src = (jnp.where(used, jnp.take(order, jnp.clip(pos - per_tile(shift), 0, n - 1)), pos) % t_tok).reshape(-1)
    assert h2.shape[0] >= nt * tmm
    xs = jnp.take(h2, src, axis=0, mode="clip")
    ys = _gmm(xs, w_gate, w_up, w_down, layer, te, tile_first, tile_valid, next_expert, tile_slot, tmm)
    return jnp.take(ys, dest, axis=0, mode="clip")


def _moe_combine(x, y1, y2, gates, m5):
    g = gates
    f = g[:, 0:1] * y1.astype(F32) + g[:, 1:2] * y2.astype(F32)
    return x + m5 * f


def _rope(x, cos, sin):
    n = x.shape[1]
    lane = lax.broadcasted_iota(I32, x.shape, 1)
    sw = jnp.where(lane % 32 < 16, pltpu.roll(x, n - 16, 1), pltpu.roll(x, 16, 1))
    reps = n // 128
    if reps > 1:
        cos = jnp.concatenate([cos] * reps, axis=1)
        sin = jnp.concatenate([sin] * reps, axis=1)
    return x * cos + sw * sin


def _inproj1_kernel(x_ref, y1_ref, y2_ref, ga_ref, m0_ref, m1_ref, g_ref, w_ref, cos_ref, sin_ref,
                    x1_ref, cq_ref, ckt_ref, cv_ref, dq_ref, dk_ref, dv_ref, *, tm, bs):
    m = m1_ref[0]
    in_lat = pl.program_id(0) * tm < bs
    scale = HEAD_DIM ** -0.5 * LOG2E
    for rows in (slice(0, tm // 2), slice(tm // 2, tm)):
        x1 = _moe_combine(x_ref[rows], y1_ref[rows], y2_ref[rows], ga_ref[rows], m0_ref[0][5:6])
        x1_ref[rows] = x1
        h = (_rms(x1, g_ref[...]) * (1.0 + m[1:2]) + m[0:1]).astype(BF16)
        cos, sin = cos_ref[rows], sin_ref[rows]
        cq = _dot(h, w_ref[:, 0:512])
        cq_ref[rows] = (jnp.where(in_lat, _rope(cq, cos, sin), cq) * scale).astype(BF16)
        ck = _dot(h, w_ref[:, 512:640])
        ckt_ref[:, rows] = jnp.where(in_lat, _rope(ck, cos, sin), ck).T.astype(BF16)
        cv_ref[rows] = _dot(h, w_ref[:, 640:768]).astype(BF16)
        dq_ref[rows] = (_dot(h, w_ref[:, 768:1280]) * scale).astype(BF16)
        dk_ref[rows] = _dot(h, w_ref[:, 1280:1792]).astype(BF16)
        dv_ref[rows] = _dot(h, w_ref[:, 1792:2304]).astype(BF16)


def _inproj1(x, y, gates, mods0, mods1, g, w, cos, sin, tm, bs, s, nb):
    r, d = x.shape
    second = pl.BlockSpec((tm, d), lambda i: (i + r // tm, 0))
    kern = functools.partial(_inproj1_kernel, tm=tm, bs=bs)
    mi = functools.partial(_mod_index, tm=tm, bs=bs, s=s, nb=nb)
    row = lambda wd: pl.BlockSpec((tm, wd), lambda i: (i, 0))
    modspec = pl.BlockSpec((1, N_MOD, d), lambda i: (mi(i), 0, 0))
    tab = pl.BlockSpec((tm, 128), lambda i: (jnp.where(i * tm < bs, (i * tm % s) // tm, 0), 0))
    shp = lambda wd, dt: jax.ShapeDtypeStruct((r, wd), dt)
    return pl.pallas_call(
        kern,
        grid=(r // tm,),
        in_specs=[row(d), row(d), second, row(128), modspec, modspec,
                  pl.BlockSpec((1, d), lambda i: (0, 0)), pl.BlockSpec(w.shape, lambda i: (0, 0)), tab, tab],
        out_specs=[row(d), row(512), pl.BlockSpec((128, tm), lambda i: (0, i)), row(128), row(512), row(512),
                   row(512)],
        out_shape=[shp(d, F32), shp(512, BF16), jax.ShapeDtypeStruct((128, r), BF16), shp(128, BF16),
                   shp(512, BF16), shp(512, BF16), shp(512, BF16)],
        compiler_params=_cp(("parallel",)),
        name="inproj1",
    )(x, y, y, gates, mods0, mods1, g, w, cos, sin)


def _swa_kernel(q_ref, ktp_ref, kto_ref, ktn_ref, ktx_ref, vp_ref, vo_ref, vn_ref, vx_ref, sink_ref, o_ref,
                *, nblk, cl):
    pair = pl.program_id(1)
    wb = SWA_BLOCK
    nloc = 3 * wb
    kto, vo, ktx, vx = kto_ref[...], vo_ref[...], ktx_ref[...], vx_ref[...]
    kts = (jnp.concatenate([ktp_ref[...], kto, ktx], axis=1), jnp.concatenate([kto, ktn_ref[...], ktx], axis=1))
    vvs = (jnp.concatenate([vp_ref[...], vo, vx], axis=0), jnp.concatenate([vo, vn_ref[...], vx], axis=0))
    a_i = lax.broadcasted_iota(I32, (2 * wb, nloc), 0) % wb
    c_i = lax.broadcasted_iota(I32, (2 * wb, nloc), 1)
    band = (c_i >= a_i) & (c_i <= a_i + 2 * SWA_WINDOW)
    oks = []
    for u in range(2):
        i = 2 * pair + u
        lo = jnp.where(i > 0, 0, wb)
        hi = jnp.where(i < nblk - 1, 3 * wb, 2 * wb)
        oks.append(band & (c_i >= lo) & (c_i < hi))
    half = lax.broadcasted_iota(I32, (1, 128), 1) // HEAD_DIM
    zero = jnp.zeros((), BF16)
    sink = sink_ref[...]
    items = [(u, g) for u in range(2) for g in range(4)]

    def scores(u, g):
        q2 = q_ref[wb * u:wb * u + wb, 128 * g:128 * g + 128]
        qst = jnp.concatenate([jnp.where(half == 0, q2, zero), jnp.where(half == 1, q2, zero)], axis=0)
        s_all = _dot(qst, kts[u])
        return jnp.concatenate([jnp.where(oks[u], s_all[:, 0:nloc], NEG), s_all[:, nloc:]], axis=1)

    def softmax(g, sc):
        sk = jnp.concatenate([jnp.broadcast_to(sink[0:1, 2 * g + a:2 * g + a + 1], (wb, 1)) for a in range(2)],
                             axis=0)
        m = jnp.maximum(jnp.max(sc, axis=-1, keepdims=True), sk)
        p = jnp.exp2(sc - m)
        return p.astype(BF16), jnp.sum(p, axis=-1, keepdims=True) + jnp.exp2(sk - m)

    def output(u, g, p, den):
        ost = _dot(p, vvs[u]) / den
        o_ref[wb * u:wb * u + wb, 128 * g:128 * g + 128] = jnp.where(half == 0, ost[0:wb], ost[wb:2 * wb]).astype(BF16)

    n = len(items)
    sc, pr = {}, {}
    for step in range(n + 2):
        if step < n:
            sc[step] = scores(*items[step])
        if 1 <= step < n + 1:
            pr[step - 1] = softmax(items[step - 1][1], sc.pop(step - 1))
        if step >= 2:
            output(*items[step - 2], *pr.pop(step - 2))


def _swa(cq, ckt, cv, sink_row, nb, s, cl, bs):
    wb = SWA_BLOCK
    nblk = s // wb
    assert nblk % 2 == 0
    npair = nblk // 2
    kern = functools.partial(_swa_kernel, nblk=nblk, cl=cl)
    prev = lambda b, j: b * nblk + jnp.maximum(2 * j - 1, 0)
    nxt = lambda b, j: b * nblk + jnp.minimum(2 * j + 2, nblk - 1)
    own = lambda b, j: b * npair + j
    return pl.pallas_call(
        kern,
        grid=(nb, npair),
        in_specs=[pl.BlockSpec((2 * wb, 512), lambda b, j: (own(b, j), 0)),
                  pl.BlockSpec((128, wb), lambda b, j: (0, prev(b, j))),
                  pl.BlockSpec((128, 2 * wb), lambda b, j: (0, own(b, j))),
                  pl.BlockSpec((128, wb), lambda b, j: (0, nxt(b, j))),
                  pl.BlockSpec((128, cl), lambda b, j: (0, bs // cl + b)),
                  pl.BlockSpec((wb, 128), lambda b, j: (prev(b, j), 0)),
                  pl.BlockSpec((2 * wb, 128), lambda b, j: (own(b, j), 0)),
                  pl.BlockSpec((wb, 128), lambda b, j: (nxt(b, j), 0)),
                  pl.BlockSpec((cl, 128), lambda b, j: (bs // cl + b, 0)),
                  pl.BlockSpec((1, 128), lambda b, j: (0, 0))],
        out_specs=pl.BlockSpec((2 * wb, 512), lambda b, j: (own(b, j), 0)),
        out_shape=jax.ShapeDtypeStruct((bs, 512), BF16),
        compiler_params=_cp(("parallel", "parallel")),
        name="swa",
    )(cq, ckt, ckt, ckt, ckt, cv, cv, cv, cv, sink_row)


def _na_kernel(q_ref, k_ref, v_ref, kx_ref, vx_ref, cols_ref, o_ref, bias_ref, *, rows, unroll):
    for h in range(2):
        for off in range(NA_KH):
            for i in range(NA_KH):
                bias_ref[h, off, :, GRID_W * i:GRID_W * (i + 1)] = cols_ref[h, off + i]
    half = lax.broadcasted_iota(I32, (1, 128), 1) // HEAD_DIM
    zero = jnp.zeros((), BF16)
    vx = vx_ref[...]
    kxt = kx_ref[...].astype(F32).T.astype(BF16)
    span = NA_KH * GRID_W

    def scores(r):
        rs = jnp.clip(r - NA_KH // 2, 0, rows - NA_KH)
        off = rs - r + NA_KH - 1
        q0 = pl.multiple_of(r * GRID_W, GRID_W)
        k0 = pl.multiple_of(rs * GRID_W, GRID_W)
        q2 = q_ref[pl.ds(q0, GRID_W), :]
        qst = jnp.concatenate([jnp.where(half == 0, q2, zero), jnp.where(half == 1, q2, zero)], axis=0)
        s_loc = (_dot_nt(qst, k_ref[pl.ds(k0, span), :])
                 + jnp.concatenate([bias_ref[0, off], bias_ref[1, off]], axis=0))
        return q0, k0, s_loc, _dot(qst, kxt)

    def softmax(s_loc, s_ctx):
        m = jnp.maximum(jnp.max(s_loc, axis=-1, keepdims=True), jnp.max(s_ctx, axis=-1, keepdims=True))
        p_loc = jnp.exp2(s_loc - m)
        p_ctx = jnp.exp2(s_ctx - m)
        den = jnp.sum(p_loc, axis=-1, keepdims=True) + jnp.sum(p_ctx, axis=-1, keepdims=True)
        return p_loc.astype(BF16), p_ctx.astype(BF16), den

    def output(q0, k0, p_loc, p_ctx, den):
        ost = (_dot(p_loc, v_ref[pl.ds(k0, span), :]) + _dot(p_ctx, vx)) / den
        o = jnp.where(half == 0, ost[0:GRID_W], ost[GRID_W:2 * GRID_W])
        o_ref[pl.ds(q0, GRID_W), :] = o.astype(BF16)

    def body(i, carry):
        sc, pr = {}, {}
        for step in range(unroll + 2):
            if step < unroll:
                sc[step] = scores(i * unroll + step)
            if 1 <= step < unroll + 1:
                q0, k0, s_loc, s_ctx = sc.pop(step - 1)
                pr[step - 1] = (q0, k0) + softmax(s_loc, s_ctx)
            if step >= 2:
                output(*pr.pop(step - 2))
        return carry

    lax.fori_loop(0, rows // unroll, body, 0)


def _na(dq, dk, dv, bias_cols, nb, s, cl, bs):
    rows = s // GRID_W
    unroll = 8
    assert rows % unroll == 0
    kern = functools.partial(_na_kernel, rows=rows, unroll=unroll)
    seq = pl.BlockSpec((s, 128), lambda b, p: (b, p))
    ctx = pl.BlockSpec((cl, 128), lambda b, p: (bs // cl + b, p))
    return pl.pallas_call(
        kern,
        grid=(nb, NA_HEADS // 2),
        in_specs=[seq, seq, seq, ctx, ctx,
                  pl.BlockSpec((2, 2 * NA_KH - 1, GRID_W, GRID_W), lambda b, p: (p, 0, 0, 0))],
        out_specs=seq,
        out_shape=jax.ShapeDtypeStruct((bs, 512), BF16),
        scratch_shapes=[pltpu.VMEM((2, NA_KH, GRID_W, NA_KH * GRID_W), F32)],
        compiler_params=_cp(("parallel", "parallel")),
        name="na",
    )(dq, dk, dv, dk, dv, bias_cols)


def _na_bias_cols(rpb):
    c = np.arange(GRID_W)
    qs = np.clip(c - NA_KW // 2, 0, GRID_W - NA_KW)
    kc = np.arange(GRID_W)
    ok = (kc[None, :] >= qs[:, None]) & (kc[None, :] < qs[:, None] + NA_KW)
    dc = np.clip(kc[None, :] - c[:, None] + NA_KW - 1, 0, 2 * NA_KW - 2)
    sel = (np.arange(2 * NA_KW - 1)[:, None, None] == dc[None]).astype(np.float32)
    cols = jnp.einsum("hab,bck->hack", rpb.astype(F32), sel, precision=HI)
    return jnp.where(ok[None, None], cols * LOG2E, NEG)


def _mix1_kernel(x_ref, oc_ref, od_ref, mod_ref, wo_ref, g2_ref, rw_ref, rb_ref, h2buf_ref,
                 xn_ref, h2_ref, ei_ref, ga_ref):
    del h2buf_ref
    y = _dot(oc_ref[...], wo_ref[0:512, :]) + _dot(od_ref[...], wo_ref[512:1024, :])
    xn, h2, eidx, gates = _post_mixer(x_ref[...], y, mod_ref[0], g2_ref[...], rw_ref[...], rb_ref[...])
    xn_ref[...] = xn
    h2_ref[...] = h2.astype(BF16)
    ei_ref[...] = eidx
    ga_ref[...] = gates


def _mix1(x, oc, od, mods, wo, g2, rw, rb, h2buf, tm, bs, s):
    d = x.shape[1]
    row = lambda w: pl.BlockSpec((tm, w), lambda i: (i, 0))
    full = lambda a: pl.BlockSpec(a.shape, lambda i: (0,) * a.ndim)
    return pl.pallas_call(
        _mix1_kernel,
        grid=(bs // tm,),
        in_specs=[row(d), row(512), row(512), pl.BlockSpec((1, N_MOD, d), lambda i: (i * tm // s, 0, 0)),
                  full(wo), full(g2), full(rw), full(rb), pl.BlockSpec(memory_space=pl.ANY)],
        out_specs=[row(d), row(d), pl.BlockSpec((8, tm), lambda i: (0, i)), row(128)],
        out_shape=[jax.ShapeDtypeStruct((bs, d), F32), jax.ShapeDtypeStruct(h2buf.shape, BF16),
                   jax.ShapeDtypeStruct((8, bs), I32), jax.ShapeDtypeStruct((bs, 128), F32)],
        input_output_aliases={8: 1},
        compiler_params=_cp(("parallel",)),
        name="mix1",
    )(x, oc, od, mods, wo, g2, rw, rb, h2buf)


def _final_kernel(x_ref, y1_ref, y2_ref, ga_ref, mod_ref, g_ref, o_ref):
    x = _moe_combine(x_ref[...], y1_ref[...], y2_ref[...], ga_ref[...], mod_ref[0][5:6])
    o_ref[...] = _rms(x, g_ref[...])


def _final(x, y, gates, mods, g, tm, s):
    r, d = x.shape
    second = pl.BlockSpec((tm, d), lambda i: (i + r // tm, 0))
    row = lambda w: pl.BlockSpec((tm, w), lambda i: (i, 0))
    return pl.pallas_call(
        _final_kernel,
        grid=(r // tm,),
        in_specs=[row(d), row(d), second, row(128), pl.BlockSpec((1, N_MOD, d), lambda i: (i * tm // s, 0, 0)),
                  pl.BlockSpec((1, d), lambda i: (0, 0))],
        out_specs=row(d),
        out_shape=jax.ShapeDtypeStruct((r, d), F32),
        compiler_params=_cp(("parallel",)),
        name="final",
    )(x, y, y, gates, mods, g)


def _rope_tables(s):
    nf = HEAD_DIM // 4
    t = np.arange(s)
    inv = ROPE_THETA ** (-np.arange(nf, dtype=np.float64) / nf)
    ar = (t // GRID_W)[:, None] * inv
    ac = (t % GRID_W)[:, None] * inv
    cos = np.concatenate([np.cos(ar), np.cos(ar), np.cos(ac), np.cos(ac)], axis=1)
    sin = np.concatenate([-np.sin(ar), np.sin(ar), -np.sin(ac), np.sin(ac)], axis=1)
    return (jnp.asarray(np.concatenate([cos, cos], axis=1), F32),
            jnp.asarray(np.concatenate([sin, sin], axis=1), F32))


def kernel(x, c, ctx, c_ctx, ada_w, ada_b, norm1_g, norm2_g, ev_w_in, ev_w_out, sc_conv_w, dn_conv_w, dn_a_log, dn_dt_bias, dn_onorm_g, od_w_in, od_w_out, swa_sink, na_rpb, router_w, router_b, moe_w_gate, moe_w_up, moe_w_down, final_g):
    nb, s, d = x.shape
    cl = ctx.shape[1]
    bs = nb * s
    tm = 512
    ts = 256
    tmm = 512
    assert d == 1024 and s % tm == 0 and (nb * cl) % tm == 0 and cl % ts == 0 and s % ts == 0
    assert s // GRID_W >= NA_KH and bs % cl == 0 and nb + 1 <= 8

    xl, xc = x.reshape(bs, d), ctx.reshape(nb * cl, d)
    cc = jnp.zeros((8, d), F32).at[:nb].set(c).at[nb].set(c_ctx)
    mods = _ada(cc, ada_w, ada_b).reshape(ada_w.shape[0], 8, N_MOD, d)
    rw32 = jnp.pad(router_w, ((0, 0), (0, 128 - N_EXPERTS)))
    rw_hi = rw32.astype(BF16)
    rw = jnp.concatenate([rw_hi, (rw32 - rw_hi.astype(F32)).astype(BF16)], axis=1)
    rb = router_b.reshape(N_EXPERTS, 1)
    row = lambda v: v.reshape(1, -1)

    w_in0 = jnp.pad(ev_w_in[0], ((0, 0), (0, 3712 - ev_w_in.shape[-1]))).astype(BF16)
    sc, qkv, z, bg = _inproj0(xl, xc, mods[0], row(norm1_g[0]), w_in0, tm, bs, s, nb)
    pad16 = lambda v: jnp.pad(v.reshape(-1), (8, 128 - 16)).reshape(1, 128)
    uf, ub, wf, wb, qf, qb, kf, kb, af, ab, gc = _dnchunk(qkv, bg, dn_conv_w[0], pad16(dn_a_log[0]),
                                                          pad16(dn_dt_bias[0]), ts, bs, s, cl)
    o_pairs = _dnscan(uf, ub, wf, wb, qf, qb, kf, kb, af, ab, gc, nb, s, cl, bs)
    r_all = bs + nb * cl
    moe_rows = lambda t: (2 * t // tmm + N_EXPERTS) * tmm
    x0, h2, ei, ga = _mix0(xl, xc, sc, o_pairs, z, mods[0], sc_conv_w[0], row(dn_onorm_g[0]),
                           ev_w_out[0].astype(BF16), row(norm2_g[0]), rw, rb,
                           jnp.zeros((moe_rows(r_all), d), BF16), ts, bs, s, cl, nb)
    y = _moe(h2, r_all, ei[0], ei[1], moe_w_gate, moe_w_up, moe_w_down, 0, tmm)

    perm = np.concatenate([np.arange(HEAD_DIM) + HEAD_DIM * (g + 4 * a) for g in range(4) for a in range(2)])
    w1 = od_w_in[0]
    w_in1 = jnp.concatenate([w1[:, 0:512][:, perm], w1[:, 512:]], axis=1).astype(BF16)
    wo1 = od_w_out[0]
    w_out1 = jnp.concatenate([wo1[0:512][perm], wo1[512:]], axis=0).astype(BF16)
    sink_row = jnp.pad(swa_sink[0][np.array([g + 4 * a for g in range(4) for a in range(2)])] * LOG2E,
                       (0, 128 - SWA_HEADS)).reshape(1, 128)
    cos, sin = _rope_tables(s)
    x1, cq, ckt, cv, dq, dk, dv = _inproj1(x0, y, ga, mods[0], mods[1], row(norm1_g[1]), w_in1, cos, sin,
                                           tm, bs, s, nb)
    oc = _swa(cq, ckt, cv, sink_row, nb, s, cl, bs)
    od = _na(dq, dk, dv, _na_bias_cols(na_rpb[0]), nb, s, cl, bs)
    assert moe_rows(bs) <= h2.shape[0]
    x2, h2, ei, ga = _mix1(x1, oc, od, mods[1], w_out1, row(norm2_g[1]), rw, rb, h2, tm, bs, s)
    y = _moe(h2, bs, ei[0], ei[1], moe_w_gate, moe_w_up, moe_w_down, 1, tmm)
    out = _final(x2, y, ga, mods[1], row(final_g), tm, s)
    return out.reshape(nb, s, d)
```

```python
import functools
import math

import numpy as np
import jax
import jax.numpy as jnp
from jax import lax
from jax.experimental import pallas as pl
from jax.experimental.pallas import tpu as pltpu

F32 = jnp.float32
BF16 = jnp.bfloat16
I32 = jnp.int32
HI = lax.Precision.HIGHEST

EPS = 1e-6
N_MOD = 6
GRID_W = 64
HEAD_DIM = 64
DN_HEADS = 4
DN_HD = 128
DN_CHUNK = 64
SWA_HEADS = 8
SWA_KV = 2
SWA_BLOCK = 128
SWA_WINDOW = 128
NA_HEADS = 8
NA_KH = 8
NA_KW = 16
ROPE_THETA = 10000.0
N_EXPERTS = 16
N_GROUPS = 4
NEG = -1e30
LOG2E = 1.4426950408889634
VMEM_LIMIT = 56 * 1024 * 1024


def _cp(sem, vmem=VMEM_LIMIT):
    return pltpu.CompilerParams(dimension_semantics=sem, vmem_limit_bytes=vmem)


def _dot(a, b, precision=None):
    return jnp.dot(a, b, preferred_element_type=F32, precision=precision)


def _dot_nt(a, b, precision=None):
    return lax.dot_general(a, b, (((1,), (1,)), ((), ())), preferred_element_type=F32, precision=precision)


def _dot_tn(a, b, precision=None):
    return lax.dot_general(a, b, (((0,), (0,)), ((), ())), preferred_element_type=F32, precision=precision)


def _silu(x):
    return x * jax.nn.sigmoid(x)


def _rms(x, g):
    return x * lax.rsqrt(jnp.mean(x * x, axis=-1, keepdims=True) + EPS) * g


def _ada_kernel(cc_ref, w_ref, b_ref, o_ref):
    a = _silu(cc_ref[...])
    o_ref[0] = _dot(a, w_ref[0], HI) + b_ref[0]


def _ada(cc, ada_w, ada_b):
    depth, d, n = ada_w.shape
    tn = 1536
    return pl.pallas_call(
        _ada_kernel,
        grid=(depth, n // tn),
        in_specs=[pl.BlockSpec((8, d), lambda l, j: (0, 0)),
                  pl.BlockSpec((1, d, tn), lambda l, j: (l, 0, j)),
                  pl.BlockSpec((1, 1, tn), lambda l, j: (l, 0, j))],
        out_specs=pl.BlockSpec((1, 8, tn), lambda l, j: (l, 0, j)),
        out_shape=jax.ShapeDtypeStruct((depth, 8, n), F32),
        compiler_params=_cp(("parallel", "parallel")),
        name="ada",
    )(cc, ada_w, ada_b.reshape(depth, 1, n))


def _mod_index(i, tm, bs, s, nb):
    row0 = i * tm
    return jnp.where(row0 < bs, row0 // s, nb)


def _token_specs(tm, bs, d):
    nlat = bs // tm
    return [pl.BlockSpec((tm, d), lambda i: (jnp.minimum(i, nlat - 1), 0)),
            pl.BlockSpec((tm, d), lambda i: (jnp.maximum(i - nlat, 0), 0))]


def _token_rows(xl_ref, xc_ref, tm, bs):
    return jnp.where(pl.program_id(0) * tm < bs, xl_ref[...], xc_ref[...])


def _inproj0_kernel(xl_ref, xc_ref, mod_ref, g_ref, w_ref, sc_ref, qkv_ref, z_ref, bg_ref, *, tm, bs):
    m = mod_ref[0]
    in_lat = pl.program_id(0) * tm < bs
    for rows in (slice(0, tm // 2), slice(tm // 2, tm)):
        x = jnp.where(in_lat, xl_ref[rows], xc_ref[rows])
        h = (_rms(x, g_ref[...]) * (1.0 + m[1:2]) + m[0:1]).astype(BF16)
        sc_ref[rows] = _dot(h, w_ref[:, 0:1536]).astype(BF16)
        qkv_ref[rows] = _dot(h, w_ref[:, 1536:3072]).astype(BF16)
        z_ref[rows] = _dot(h, w_ref[:, 3072:3584]).astype(BF16)
        bg_ref[rows] = _dot(h, w_ref[:, 3584:3712])


def _inproj0(xl, xc, mods, g, w, tm, bs, s, nb):
    d = xl.shape[1]
    r = bs + xc.shape[0]
    mi = functools.partial(_mod_index, tm=tm, bs=bs, s=s, nb=nb)
    return pl.pallas_call(
        functools.partial(_inproj0_kernel, tm=tm, bs=bs),
        grid=(r // tm,),
        in_specs=_token_specs(tm, bs, d) + [
            pl.BlockSpec((1, N_MOD, d), lambda i: (mi(i), 0, 0)),
            pl.BlockSpec((1, d), lambda i: (0, 0)),
            pl.BlockSpec(w.shape, lambda i: (0, 0))],
        out_specs=[pl.BlockSpec((tm, 1536), lambda i: (i, 0)),
                   pl.BlockSpec((tm, 1536), lambda i: (i, 0)),
                   pl.BlockSpec((tm, 512), lambda i: (i, 0)),
                   pl.BlockSpec((tm, 128), lambda i: (i, 0))],
        out_shape=[jax.ShapeDtypeStruct((r, 1536), BF16), jax.ShapeDtypeStruct((r, 1536), BF16),
                   jax.ShapeDtypeStruct((r, 512), BF16), jax.ShapeDtypeStruct((r, 128), F32)],
        compiler_params=_cp(("parallel",)),
        name="inproj0",
    )(xl, xc, mods, g, w)


def _seq_edges(i, ts, bs, s, cl):
    row0 = i * ts
    in_lat = row0 < bs
    r_in = jnp.where(in_lat, row0 % s, (row0 - bs) % cl)
    seqlen = jnp.where(in_lat, s, cl)
    return r_in == 0, r_in + ts == seqlen


def _shifted(x, prev_row, next_row):
    n = x.shape[0]
    rows = lax.broadcasted_iota(I32, x.shape, 0)
    xp = jnp.where(rows == 0, prev_row, pltpu.roll(x, 1, 0))
    xn = jnp.where(rows == n - 1, next_row, pltpu.roll(x, n - 1, 0))
    return xp, xn


HALO = 16


def _halo_specs(ts, width, r):
    nblk = r // HALO
    k = ts // HALO
    return [pl.BlockSpec((HALO, width), lambda i: (jnp.maximum(i * k - 1, 0), 0)),
            pl.BlockSpec((HALO, width), lambda i: (jnp.minimum((i + 1) * k, nblk - 1), 0))]


def _dnprep_tile(x_ref, prev_ref, next_ref, bg_ref, cw_ref, alog_ref, dt_ref,
                 q_ref, k_ref, v_ref, bga_ref, *, ts, bs, s, cl):
    first, last = _seq_edges(pl.program_id(0), ts, bs, s, cl)
    for c in range(12):
        sl = slice(128 * c, 128 * c + 128)
        x = x_ref[:, sl].astype(F32)
        pr = jnp.where(first, 0.0, prev_ref[HALO - 1:HALO, sl].astype(F32))
        nx = jnp.where(last, 0.0, next_ref[0:1, sl].astype(F32))
        xp, xn = _shifted(x, pr, nx)
        w = cw_ref[:, sl]
        y = _silu(xp * w[0:1] + x * w[1:2] + xn * w[2:3])
        hs = slice(128 * (c % 4), 128 * (c % 4) + 128)
        if c < 8:
            y = y * lax.rsqrt(jnp.sum(y * y, axis=-1, keepdims=True) + EPS)
        if c < 4:
            q_ref[:, hs] = y * DN_HD ** -0.5
        elif c < 8:
            k_ref[:, hs] = y
        else:
            v_ref[:, hs] = y
    b = bg_ref[...]
    cols = lax.broadcasted_iota(I32, b.shape, 1)
    beta = jax.nn.sigmoid(b)
    t = b + dt_ref[...]
    softplus = jnp.maximum(t, 0.0) + jnp.log1p(jnp.exp(-jnp.abs(t)))
    g = -jnp.exp(alog_ref[...]) * softplus
    bga_ref[...] = jnp.where(cols < 8, beta, jnp.where(cols < 16, g, 0.0))


def _dnchunk_kernel(x_ref, prev_ref, next_ref, bgraw_ref, cw_ref, alog_ref, dt_ref,
                    uf_ref, ub_ref, wf_ref, wb_ref, qf_ref, qb_ref, kf_ref, kb_ref, af_ref, ab_ref, gc_ref,
                    q_ref, k_ref, v_ref, bg_ref, *, nchunks, bs, s, cl):
    _dnprep_tile(x_ref, prev_ref, next_ref, bgraw_ref, cw_ref, alog_ref, dt_ref, q_ref, k_ref, v_ref, bg_ref,
                 ts=nchunks * DN_CHUNK, bs=bs, s=s, cl=cl)
    outs = ((uf_ref, wf_ref, qf_ref, kf_ref, af_ref), (ub_ref, wb_ref, qb_ref, kb_ref, ab_ref))
    c, nh = DN_CHUNK, DN_HEADS
    head_of_col = lax.broadcasted_iota(I32, (1, nh * c), 1) // c
    zero = jnp.zeros((), BF16)

    def block_diag(x):
        return jnp.concatenate([jnp.where(head_of_col == h, x, zero) for h in range(nh)], axis=0)

    chains = []
    for cc in range(nchunks):
        chains += _dnchunk_setup(slice(cc * c, (cc + 1) * c), q_ref, k_ref, v_ref, bg_ref, gc_ref)
    for ch in chains:
        ch["tm"] = ch["nmat"]
        nb16 = ch["nmat"].astype(BF16)
        ch["npow"] = _dot(nb16, block_diag(nb16))
    for _ in range(4):
        for ch in chains:
            nb16 = ch["npow"].astype(BF16)
            ch["both"] = _dot(jnp.concatenate([nb16, ch["tm"].astype(BF16)], axis=0), block_diag(nb16))
        for ch in chains:
            ch["tm"] = ch["tm"] + ch["npow"] + ch["both"][c:2 * c]
            ch["npow"] = ch["both"][0:c]
    for ch in chains:
        ch["both"] = _dot(ch["tm"].astype(BF16), block_diag(ch["npow"].astype(BF16)))
    for ch in chains:
        ch["tm"] = ch["tm"] + ch["npow"] + ch["both"]
    for ch in chains:
        ch["uw"] = ch["rhs"] + _dot(block_diag(ch["tm"].astype(BF16)), ch["rhs"].astype(BF16))
    for ch in chains:
        u_ref, w_ref, qd_ref, kd_ref, at_ref = outs[ch["d"]]
        rows, uw = ch["rows"], ch["uw"]
        for h in range(nh):
            hs = slice(DN_HD * h, DN_HD * h + DN_HD)
            rs = slice(c * h, c * h + c)
            u_ref[rows, hs] = uw[rs, 0:DN_HD]
            w_ref[rows, hs] = uw[rs, DN_HD:2 * DN_HD].astype(BF16)
            qd_ref[rows, hs] = ch["qd"][rs]
        kd_ref[slice(2 * rows.start, 2 * rows.stop)] = ch["kd"].T.astype(BF16)
        at_ref[rows] = ch["att"].astype(BF16)


def _dnchunk_setup(rows, q_ref, k_ref, v_ref, bg_ref, gc_ref):
    c, nh = DN_CHUNK, DN_HEADS
    n = c * nh
    bg = bg_ref[rows]
    i64 = lax.broadcasted_iota(I32, (c, c), 0)
    j64 = lax.broadcasted_iota(I32, (c, c), 1)
    cols = lax.broadcasted_iota(I32, bg.shape, 1)
    gcf = _dot((i64 >= j64).astype(F32), bg, HI)
    gcb = _dot((i64 <= j64).astype(F32), bg, HI)
    gc = jnp.where(cols >= 12, gcb, gcf)
    gc_ref[rows] = gc
    gct = gc.T
    ii = lax.broadcasted_iota(I32, (c, n), 0)
    jj = lax.broadcasted_iota(I32, (c, n), 1)
    head_of_col = jj // c
    jj = jj % c

    def stack(ref):
        return jnp.concatenate([ref[rows, DN_HD * h:DN_HD * h + DN_HD] for h in range(nh)], axis=0)

    def stacked_cols(arr, r0, r1, col0):
        return jnp.concatenate([jnp.broadcast_to(arr[r0:r1, col0 + h:col0 + h + 1], (c, DN_HD)) for h in range(nh)],
                               axis=0)

    def side_by_side_cols(arr, col0):
        out = arr[:, col0 + nh - 1:col0 + nh]
        for h in range(nh - 2, -1, -1):
            out = jnp.where(head_of_col == h, arr[:, col0 + h:col0 + h + 1], out)
        return out

    def diag_blocks(x):
        out = x[(nh - 1) * c:nh * c]
        for h in range(nh - 2, -1, -1):
            out = jnp.where(head_of_col == h, x[h * c:(h + 1) * c], out)
        return out

    kst, qst, vst = stack(k_ref), stack(q_ref), stack(v_ref)
    kb = kst.astype(BF16)
    kq = _dot_nt(jnp.concatenate([kb, qst.astype(BF16)], axis=0), kb)
    kkt, qkt = diag_blocks(kq[0:n]), diag_blocks(kq[n:2 * n])
    chains = []
    for d in range(2):
        incl = (ii >= jj) if d == 0 else (ii <= jj)
        strict = (ii > jj) if d == 0 else (ii < jj)
        last = c - 1 if d == 0 else 0
        grow = jnp.concatenate([gct[8 + 4 * d + h:9 + 4 * d + h, :] for h in range(nh)], axis=1)
        decay = jnp.exp(jnp.where(incl, side_by_side_cols(gc, 8 + 4 * d) - grow, NEG))
        nmat = jnp.where(strict, -(side_by_side_cols(bg, 4 * d) * kkt * decay), 0.0)
        b1 = stacked_cols(bg, 0, c, 4 * d)
        gcol = stacked_cols(gc, 0, c, 8 + 4 * d)
        glast = stacked_cols(gc, last, last + 1, 8 + 4 * d)
        e1 = jnp.exp(gcol)
        chains.append(dict(
            d=d, rows=rows, nmat=nmat,
            rhs=jnp.concatenate([b1 * vst, (b1 * e1) * kst], axis=1),
            qd=(qst * e1).astype(BF16),
            kd=kst * jnp.exp(glast - gcol),
            att=qkt * decay))
    return chains


def _dnchunk(qkv, bg, cw, alog_row, dt_row, ts, bs, s, cl):
    r = qkv.shape[0]
    nchunks = ts // DN_CHUNK
    c = ts
    row = lambda w: pl.BlockSpec((c, w), lambda i: (i, 0))
    shp = lambda w, dt: jax.ShapeDtypeStruct((r, w), dt)
    return pl.pallas_call(
        functools.partial(_dnchunk_kernel, nchunks=nchunks, bs=bs, s=s, cl=cl),
        grid=(r // c,),
        in_specs=[row(1536)] + _halo_specs(ts, 1536, r) + [
            row(128),
            pl.BlockSpec((3, 1536), lambda i: (0, 0)),
            pl.BlockSpec((1, 128), lambda i: (0, 0)),
            pl.BlockSpec((1, 128), lambda i: (0, 0))],
        scratch_shapes=[pltpu.VMEM((c, 512), F32)] * 3 + [pltpu.VMEM((c, 128), F32)],
        out_specs=[row(512)] * 6 + [pl.BlockSpec((2 * c, 256), lambda i: (i, 0))] * 2 + [row(256), row(256), row(128)],
        out_shape=([shp(512, F32)] * 2 + [shp(512, BF16)] * 4 + [jax.ShapeDtypeStruct((2 * r, 256), BF16)] * 2
                   + [shp(256, BF16)] * 2 + [shp(128, F32)]),
        compiler_params=_cp(("parallel",)),
        name="dnchunk",
    )(qkv, qkv, qkv, bg, cw, alog_row, dt_row)


def _dnscan_kernel(*refs, nsub, nb):
    nchain = 2 * nb
    ins, outs, s_ref = refs[:6 * nchain], refs[6 * nchain:7 * nchain], refs[7 * nchain]

    @pl.when(pl.program_id(0) == 0)
    def _():
        s_ref[...] = jnp.zeros_like(s_ref)

    c, nh = DN_CHUNK, DN_HEADS
    head_of_lane = lax.broadcasted_iota(I32, (1, nh * DN_HD), 1) // DN_HD
    head_of_col = lax.broadcasted_iota(I32, (1, nh * c), 1) // c
    zero = jnp.zeros((), BF16)

    def block_diag(tile, head_ids):
        return jnp.concatenate([jnp.where(head_ids == h, tile, zero) for h in range(nh)], axis=0)

    states = [s_ref[ci] for ci in range(nchain)]
    for sub in range(nsub):
        work = []
        for ci in range(nchain):
            d = ci % 2
            u_ref, w_ref, qd_ref, kd_ref, at_ref, g_ref = ins[6 * ci:6 * ci + 6]
            last = c - 1 if d == 0 else 0
            k = sub if d == 0 else nsub - 1 - sub
            rows = slice(c * k, c * k + c)
            g = g_ref[rows]
            decay = jnp.concatenate(
                [jnp.broadcast_to(jnp.exp(g[last:last + 1, 8 + 4 * d + h:9 + 4 * d + h]), (DN_HD, DN_HD))
                 for h in range(nh)], axis=0)
            ust = jnp.concatenate([u_ref[rows, DN_HD * h:DN_HD * h + DN_HD] for h in range(nh)], axis=0)
            stb = states[ci].astype(BF16)
            vnew = ust - _dot(block_diag(w_ref[rows], head_of_lane), stb)
            work.append((rows, decay, stb, vnew.astype(BF16)))
        for ci in range(nchain):
            u_ref, w_ref, qd_ref, kd_ref, at_ref, g_ref = ins[6 * ci:6 * ci + 6]
            rows, decay, stb, vnb = work[ci]
            o = (_dot(block_diag(qd_ref[rows], head_of_lane), stb)
                 + _dot(block_diag(at_ref[rows], head_of_col), vnb))
            kdt = kd_ref[slice(2 * rows.start, 2 * rows.stop)]
            states[ci] = states[ci] * decay + _dot(block_diag(kdt, head_of_col), vnb)
            for h in range(nh):
                outs[ci][rows, DN_HD * h:DN_HD * h + DN_HD] = o[c * h:c * h + c].astype(BF16)
    for ci in range(nchain):
        s_ref[ci] = states[ci]


def _dnscan(uf, ub, wf, wb, qf, qb, kf, kb, af, ab, gc, nb, s, cl, bs):
    nsub = 4
    c = DN_CHUNK * nsub
    assert cl % c == 0 and s % c == 0 and bs % c == 0
    ncc, ncl = cl // c, s // c
    ns = ncc + ncl

    def src_block(b, d):
        if d == 0:
            return lambda t: jnp.where(t < ncc, bs // c + b * ncc + t, b * ncl + t - ncc)
        return lambda t: jnp.where(t < ncc, bs // c + b * ncc + (ncc - 1 - t), b * ncl + (ncl - 1 - (t - ncc)))

    def dst_block(d):
        if d == 0:
            return lambda t: jnp.where(t < ncc, ncl + t, t - ncc)
        return lambda t: jnp.where(t < ncc, ncl + (ncc - 1 - t), ncl - 1 - (t - ncc))

    in_specs, args, out_specs = [], [], []
    for b in range(nb):
        for d, group in enumerate(((uf, wf, qf, kf, af, gc), (ub, wb, qb, kb, ab, gc))):
            idx = src_block(b, d)
            blk = lambda w, idx=idx: pl.BlockSpec((c, w), lambda t: (idx(t), 0))
            in_specs += [blk(512), blk(512), blk(512), pl.BlockSpec((2 * c, 256), lambda t, idx=idx: (idx(t), 0)),
                         blk(256), blk(128)]
            args += list(group)
            out_specs.append(pl.BlockSpec((c, 512), lambda t, f=dst_block(d): (f(t), 0)))
    outs = pl.pallas_call(
        functools.partial(_dnscan_kernel, nsub=nsub, nb=nb),
        grid=(ns,),
        in_specs=in_specs,
        out_specs=out_specs,
        out_shape=[jax.ShapeDtypeStruct((s + cl, 512), BF16)] * (2 * nb),
        scratch_shapes=[pltpu.VMEM((2 * nb, DN_HEADS * DN_HD, DN_HD), F32)],
        compiler_params=_cp(("arbitrary",)),
        name="dnscan",
    )(*args)
    return [(outs[2 * b], outs[2 * b + 1]) for b in range(nb)]


def _route(logits, bias_col):
    epg = N_EXPERTS // N_GROUPS
    tm = logits.shape[0]
    scores = jax.nn.sigmoid(logits.T[0:N_EXPERTS])
    gsel = scores + bias_col
    row = lambda a, k: a[k:k + 1]
    best = gidx = None
    for g in range(N_GROUPS):
        a = [row(gsel, epg * g + k) for k in range(epg)]
        m01, n01 = jnp.maximum(a[0], a[1]), jnp.minimum(a[0], a[1])
        m23, n23 = jnp.maximum(a[2], a[3]), jnp.minimum(a[2], a[3])
        gs = jnp.maximum(m01, m23) + jnp.maximum(jnp.minimum(m01, m23), jnp.maximum(n01, n23))
        if g == 0:
            best, gidx = gs, jnp.zeros_like(gs)
        else:
            better = gs > best
            best = jnp.where(better, gs, best)
            gidx = jnp.where(better, float(g), gidx)
    sel = [None] * epg
    raw = [None] * epg
    for g in range(N_GROUPS):
        for k in range(epg):
            v, u = row(gsel, epg * g + k), row(scores, epg * g + k)
            sel[k] = v if g == 0 else jnp.where(gidx == g, v, sel[k])
            raw[k] = u if g == 0 else jnp.where(gidx == g, u, raw[k])
    v1, e1, w1 = sel[0], jnp.zeros_like(gidx), raw[0]
    for k in range(1, epg):
        better = sel[k] > v1
        v1 = jnp.where(better, sel[k], v1)
        e1 = jnp.where(better, float(k), e1)
        w1 = jnp.where(better, raw[k], w1)
    v2 = e2 = w2 = None
    for k in range(epg):
        cand = jnp.where(e1 == k, -jnp.inf, sel[k])
        if k == 0:
            v2, e2, w2 = cand, jnp.zeros_like(gidx), raw[0]
        else:
            better = cand > v2
            v2 = jnp.where(better, cand, v2)
            e2 = jnp.where(better, float(k), e2)
            w2 = jnp.where(better, raw[k], w2)
    tot = w1 + w2
    eidx = jnp.concatenate([gidx * epg + e1, gidx * epg + e2, jnp.zeros((6, tm), F32)], axis=0).astype(I32)
    gates_t = jnp.concatenate([w1 / tot, w2 / tot, jnp.zeros((126, tm), F32)], axis=0)
    return eidx, gates_t.T


def _post_mixer(x, y, m, g2, rw, rb):
    xn = x + m[2:3] * y
    h2 = _rms(xn, g2) * (1.0 + m[4:5]) + m[3:4]
    hi = h2.astype(BF16)
    lo = (h2 - hi.astype(F32)).astype(BF16)
    hw = _dot(hi, rw)
    logits = hw[:, 0:128] + (hw[:, 128:256] + _dot(lo, rw[:, 0:128]))
    eidx, gates = _route(logits, rb)
    return xn, hi, eidx, gates


def _mix0_kernel(xl_ref, xc_ref, sc_ref, prev_ref, next_ref, z_ref, mod_ref, cw_ref, on_ref, wo_ref,
                 g2_ref, rw_ref, rb_ref, h2buf_ref, *rest, ts, bs, s, cl, nb):
    del h2buf_ref
    o_refs = rest[:2 * nb]
    xn_ref, h2_ref, ei_ref, ga_ref = rest[2 * nb:]
    first, last = _seq_edges(pl.program_id(0), ts, bs, s, cl)
    row0 = pl.program_id(0) * ts
    batch = jnp.where(row0 < bs, row0 // s, (row0 - bs) // cl)
    ya = []
    for c in range(4):
        sl = slice(128 * c, 128 * c + 128)
        sg = slice(512 + 128 * c, 512 + 128 * c + 128)
        sx = slice(1024 + 128 * c, 1024 + 128 * c + 128)
        f32 = lambda ref, rows, cols: ref[rows, cols].astype(F32)
        u = f32(sc_ref, slice(None), sg) * f32(sc_ref, slice(None), sx)
        pr = jnp.where(first, 0.0, f32(prev_ref, slice(HALO - 1, HALO), sg) * f32(prev_ref, slice(HALO - 1, HALO), sx))
        nx = jnp.where(last, 0.0, f32(next_ref, slice(0, 1), sg) * f32(next_ref, slice(0, 1), sx))
        up, un = _shifted(u, pr, nx)
        w = cw_ref[:, sl]
        ya.append((f32(sc_ref, slice(None), sl) * (up * w[0:1] + u * w[1:2] + un * w[2:3])).astype(BF16))
    yb = []
    for h in range(DN_HEADS):
        hs = slice(DN_HD * h, DN_HD * h + DN_HD)
        o = o_refs[2 * nb - 2][:, hs].astype(F32) + o_refs[2 * nb - 1][:, hs].astype(F32)
        for b in range(nb - 2, -1, -1):
            o = jnp.where(batch == b, o_refs[2 * b][:, hs].astype(F32) + o_refs[2 * b + 1][:, hs].astype(F32), o)
        yb.append((_rms(o, on_ref[...]) * _silu(z_ref[:, hs].astype(F32))).astype(BF16))
    ycat = jnp.concatenate(ya + yb, axis=1)
    y = _dot(ycat, wo_ref[...])
    xn, h2, eidx, gates = _post_mixer(_token_rows(xl_ref, xc_ref, ts, bs), y, mod_ref[0], g2_ref[...], rw_ref[...],
                                      rb_ref[...])
    xn_ref[...] = xn
    h2_ref[...] = h2.astype(BF16)
    ei_ref[...] = eidx
    ga_ref[...] = gates


def _mix0(xl, xc, sc, o_pairs, z, mods, cw, on, wo, g2, rw, rb, h2buf, ts, bs, s, cl, nb):
    d = xl.shape[1]
    r = bs + xc.shape[0]
    kern = functools.partial(_mix0_kernel, ts=ts, bs=bs, s=s, cl=cl, nb=nb)

    def o_spec(b):
        def index(i):
            row0 = i * ts
            in_lat = row0 < bs
            owner = jnp.where(in_lat, row0 // s, (row0 - bs) // cl)
            own = jnp.where(in_lat, (row0 % s) // ts, s // ts + ((row0 - bs) % cl) // ts)
            nxt = jnp.where(in_lat, 0, s // ts)
            prv = jnp.where(in_lat, s // ts - 1, (s + cl) // ts - 1)
            return jnp.where(owner == b, own, jnp.where(owner < b, nxt, prv)), 0
        return pl.BlockSpec((ts, 512), index)

    mi = functools.partial(_mod_index, tm=ts, bs=bs, s=s, nb=nb)
    row = lambda w: pl.BlockSpec((ts, w), lambda i: (i, 0))
    full = lambda a: pl.BlockSpec(a.shape, lambda i: (0,) * a.ndim)
    return pl.pallas_call(
        kern,
        grid=(r // ts,),
        in_specs=_token_specs(ts, bs, d) + [row(1536)] + _halo_specs(ts, 1536, r) + [
            row(512),
            pl.BlockSpec((1, N_MOD, d), lambda i: (mi(i), 0, 0)),
            full(cw), full(on), full(wo), full(g2), full(rw), full(rb), pl.BlockSpec(memory_space=pl.ANY)]
        + [o_spec(b) for b in range(nb) for _ in range(2)],
        out_specs=[row(d), row(d), pl.BlockSpec((8, ts), lambda i: (0, i)), row(128)],
        out_shape=[jax.ShapeDtypeStruct((r, d), F32), jax.ShapeDtypeStruct(h2buf.shape, BF16),
                   jax.ShapeDtypeStruct((8, r), I32), jax.ShapeDtypeStruct((r, 128), F32)],
        input_output_aliases={13: 1},
        compiler_params=_cp(("parallel",)),
        name="mix0",
    )(xl, xc, sc, sc, sc, z, mods, cw, on, wo, g2, rw, rb, h2buf, *[a for pair in o_pairs for a in pair])


def _gmm_kernel(te_ref, tf_ref, tv_ref, ne_ref, sl_ref, x_ref, wg_hbm, wu_hbm, wd_hbm, y_ref,
                wbuf, wgb, wub, wdb, sem, *, layer):
    t = pl.program_id(0)

    def weight_copies(e, slot):
        return [pltpu.make_async_copy(w.at[layer, e], wbuf.at[slot, k], sem.at[slot, k])
                for k, w in enumerate((wg_hbm, wu_hbm, wd_hbm))]

    @pl.when(t == 0)
    def _():
        for cp in weight_copies(te_ref[0], sl_ref[0]):
            cp.start()

    @pl.when(tf_ref[t] == 1)
    def _():
        slot = sl_ref[t]
        for cp in weight_copies(te_ref[t], slot):
            cp.wait()
        wgb[...] = wbuf[slot, 0].astype(BF16)
        wub[...] = wbuf[slot, 1].astype(BF16)
        wdb[...] = wbuf[slot, 2].astype(BF16)

        @pl.when(ne_ref[t] >= 0)
        def _():
            for cp in weight_copies(ne_ref[t], 1 - slot):
                cp.start()

    @pl.when(tv_ref[t] == 1)
    def _():
        x = x_ref[...]
        a = (_silu(_dot(x, wgb[...])) * _dot(x, wub[...])).astype(BF16)
        y_ref[...] = _dot(a, wdb[...]).astype(BF16)

    @pl.when(tv_ref[t] == 0)
    def _():
        y_ref[...] = jnp.zeros_like(y_ref)


def _gmm(xs, w_gate, w_up, w_down, layer, tile_expert, tile_first, tile_valid, next_expert, tile_slot, tmm):
    p, d = xs.shape
    de = w_gate.shape[-1]
    assert d == de
    nt = p // tmm
    row = lambda t, *_: (t, 0)
    grid_spec = pltpu.PrefetchScalarGridSpec(
        num_scalar_prefetch=5,
        grid=(nt,),
        in_specs=[pl.BlockSpec((tmm, d), row)] + [pl.BlockSpec(memory_space=pl.ANY)] * 3,
        out_specs=pl.BlockSpec((tmm, d), row),
        scratch_shapes=[pltpu.VMEM((2, 3, d, de), F32),
                        pltpu.VMEM((d, de), BF16), pltpu.VMEM((d, de), BF16), pltpu.VMEM((de, d), BF16),
                        pltpu.SemaphoreType.DMA((2, 3))],
    )
    return pl.pallas_call(
        functools.partial(_gmm_kernel, layer=layer),
        grid_spec=grid_spec,
        out_shape=jax.ShapeDtypeStruct((p, d), BF16),
        compiler_params=_cp(("arbitrary",)),
        name="gmm",
    )(tile_expert, tile_first, tile_valid, next_expert, tile_slot, xs, w_gate, w_up, w_down)


def _moe(h2, t_tok, e_first, e_second, w_gate, w_up, w_down, layer, tmm):
    n = 2 * t_tok
    e_flat = jnp.concatenate([e_first, e_second])
    onehot = (e_flat[:, None] == jnp.arange(N_EXPERTS, dtype=I32)[None, :]).astype(I32)
    csum = jnp.cumsum(onehot, axis=0)
    counts = csum[-1]
    ptiles = (counts + tmm - 1) // tmm
    tile_end = jnp.cumsum(ptiles)
    dest = jnp.sum(onehot * (csum - 1 + ((tile_end - ptiles) * tmm)[None, :]), axis=1)
    nt = n // tmm + N_EXPERTS
    tid = jnp.arange(nt, dtype=I32)
    tile_valid = (tid < tile_end[-1]).astype(I32)
    te = jnp.minimum(jnp.sum((tile_end[None, :] <= tid[:, None]).astype(I32), axis=1), N_EXPERTS - 1)
    last_used = jnp.max(jnp.where(tile_valid == 1, te, 0))
    te = jnp.where(tile_valid == 1, te, last_used)
    tile_first = jnp.concatenate([jnp.ones((1,), I32), (te[1:] != te[:-1]).astype(I32)])
    tile_slot = (jnp.cumsum(tile_first) - 1) % 2
    experts = jnp.arange(N_EXPERTS, dtype=I32)
    later = (experts[None, :] > te[:, None]) & (counts[None, :] > 0)
    next_expert = jnp.min(jnp.where(later, experts[None, :], N_EXPERTS), axis=1)
    next_expert = jnp.where(next_expert < N_EXPERTS, next_expert, -1)
    order = jnp.argsort(e_flat, stable=True).astype(I32)
    seg_start = (tile_end - ptiles) * tmm
    shift = seg_start - (jnp.cumsum(counts) - counts)
    pos = tid[:, None] * tmm + jnp.arange(tmm, dtype=I32)[None, :]
    te_onehot = te[:, None] == jnp.arange(N_EXPERTS, dtype=I32)[None, :]
    per_tile = lambda v: jnp.sum(jnp.where(te_onehot, v[None, :], 0), axis=1, keepdims=True)
    used = (pos - per_tile(seg_start) < per_tile(counts)) & (tile_valid[:, None] == 1)
    src = (jnp.where(used, jnp.take(order, jnp.clip(pos - per_tile(shift), 0, n - 1)), pos) % t_tok).reshape(-1)
    assert h2.shape[0] >= nt * tmm
    xs = jnp.take(h2, src, axis=0, mode="clip")
    ys = _gmm(xs, w_gate, w_up, w_down, layer, te, tile_first, tile_valid, next_expert, tile_slot, tmm)
    return jnp.take(ys, dest, axis=0, mode="clip")


def _moe_combine(x, y1, y2, gates, m5):
    g = gates
    f = g[:, 0:1] * y1.astype(F32) + g[:, 1:2] * y2.astype(F32)
    return x + m5 * f


def _rope(x, cos, sin):
    n = x.shape[1]
    lane = lax.broadcasted_iota(I32, x.shape, 1)
    sw = jnp.where(lane % 32 < 16, pltpu.roll(x, n - 16, 1), pltpu.roll(x, 16, 1))
    reps = n // 128
    if reps > 1:
        cos = jnp.concatenate([cos] * reps, axis=1)
        sin = jnp.concatenate([sin] * reps, axis=1)
    return x * cos + sw * sin


def _inproj1_kernel(x_ref, y1_ref, y2_ref, ga_ref, m0_ref, m1_ref, g_ref, w_ref, cos_ref, sin_ref,
                    x1_ref, cq_ref, ckt_ref, cv_ref, dq_ref, dk_ref, dv_ref, *, tm, bs):
    m = m1_ref[0]
    in_lat = pl.program_id(0) * tm < bs
    scale = HEAD_DIM ** -0.5 * LOG2E
    for rows in (slice(0, tm // 2), slice(tm // 2, tm)):
        x1 = _moe_combine(x_ref[rows], y1_ref[rows], y2_ref[rows], ga_ref[rows], m0_ref[0][5:6])
        x1_ref[rows] = x1
        h = (_rms(x1, g_ref[...]) * (1.0 + m[1:2]) + m[0:1]).astype(BF16)
        cos, sin = cos_ref[rows], sin_ref[rows]
        cq = _dot(h, w_ref[:, 0:512])
        cq_ref[rows] = (jnp.where(in_lat, _rope(cq, cos, sin), cq) * scale).astype(BF16)
        ck = _dot(h, w_ref[:, 512:640])
        ckt_ref[:, rows] = jnp.where(in_lat, _rope(ck, cos, sin), ck).T.astype(BF16)
        cv_ref[rows] = _dot(h, w_ref[:, 640:768]).astype(BF16)
        dq_ref[rows] = (_dot(h, w_ref[:, 768:1280]) * scale).astype(BF16)
        dk_ref[rows] = _dot(h, w_ref[:, 1280:1792]).astype(BF16)
        dv_ref[rows] = _dot(h, w_ref[:, 1792:2304]).astype(BF16)


def _inproj1(x, y, gates, mods0, mods1, g, w, cos, sin, tm, bs, s, nb):
    r, d = x.shape
    second = pl.BlockSpec((tm, d), lambda i: (i + r // tm, 0))
    kern = functools.partial(_inproj1_kernel, tm=tm, bs=bs)
    mi = functools.partial(_mod_index, tm=tm, bs=bs, s=s, nb=nb)
    row = lambda wd: pl.BlockSpec((tm, wd), lambda i: (i, 0))
    modspec = pl.BlockSpec((1, N_MOD, d), lambda i: (mi(i), 0, 0))
    tab = pl.BlockSpec((tm, 128), lambda i: (jnp.where(i * tm < bs, (i * tm % s) // tm, 0), 0))
    shp = lambda wd, dt: jax.ShapeDtypeStruct((r, wd), dt)
    return pl.pallas_call(
        kern,
        grid=(r // tm,),
        in_specs=[row(d), row(d), second, row(128), modspec, modspec,
                  pl.BlockSpec((1, d), lambda i: (0, 0)), pl.BlockSpec(w.shape, lambda i: (0, 0)), tab, tab],
        out_specs=[row(d), row(512), pl.BlockSpec((128, tm), lambda i: (0, i)), row(128), row(512), row(512),
                   row(512)],
        out_shape=[shp(d, F32), shp(512, BF16), jax.ShapeDtypeStruct((128, r), BF16), shp(128, BF16),
                   shp(512, BF16), shp(512, BF16), shp(512, BF16)],
        compiler_params=_cp(("parallel",)),
        name="inproj1",
    )(x, y, y, gates, mods0, mods1, g, w, cos, sin)


def _swa_kernel(q_ref, ktp_ref, kto_ref, ktn_ref, ktx_ref, vp_ref, vo_ref, vn_ref, vx_ref, sink_ref, o_ref,
                *, nblk, cl, nq):
    grp = pl.program_id(1)
    wb = SWA_BLOCK
    nloc = 3 * wb
    ktx, vx = ktx_ref[...], vx_ref[...]
    kt_all = jnp.concatenate([ktp_ref[...], kto_ref[...], ktn_ref[...]], axis=1)
    v_all = jnp.concatenate([vp_ref[...], vo_ref[...], vn_ref[...]], axis=0)
    kts = [jnp.concatenate([kt_all[:, wb * u:wb * u + nloc], ktx], axis=1) for u in range(nq)]
    vvs = [jnp.concatenate([v_all[wb * u:wb * u + nloc], vx], axis=0) for u in range(nq)]
    a_i = lax.broadcasted_iota(I32, (2 * wb, nloc), 0) % wb
    c_i = lax.broadcasted_iota(I32, (2 * wb, nloc), 1)
    band = (c_i >= a_i) & (c_i <= a_i + 2 * SWA_WINDOW)
    oks = []
    for u in range(nq):
        i = nq * grp + u
        lo = jnp.where(i > 0, 0, wb)
        hi = jnp.where(i < nblk - 1, 3 * wb, 2 * wb)
        oks.append(band & (c_i >= lo) & (c_i < hi))
    half = lax.broadcasted_iota(I32, (1, 128), 1) // HEAD_DIM
    zero = jnp.zeros((), BF16)
    sink = sink_ref[...]
    items = [(u, g) for u in range(nq) for g in range(4)]

    def scores(u, g):
        q2 = q_ref[wb * u:wb * u + wb, 128 * g:128 * g + 128]
        qst = jnp.concatenate([jnp.where(half == 0, q2, zero), jnp.where(half == 1, q2, zero)], axis=0)
        s_all = _dot(qst, kts[u])
        return jnp.concatenate([jnp.where(oks[u], s_all[:, 0:nloc], NEG), s_all[:, nloc:]], axis=1)

    def softmax(g, sc):
        sk = jnp.concatenate([jnp.broadcast_to(sink[0:1, 2 * g + a:2 * g + a + 1], (wb, 1)) for a in range(2)],
                             axis=0)
        m = jnp.maximum(jnp.max(sc, axis=-1, keepdims=True), sk)
        p = jnp.exp2(sc - m)
        return p.astype(BF16), jnp.sum(p, axis=-1, keepdims=True) + jnp.exp2(sk - m)

    def output(u, g, p, den):
        ost = _dot(p, vvs[u]) / den
        o_ref[wb * u:wb * u + wb, 128 * g:128 * g + 128] = jnp.where(half == 0, ost[0:wb], ost[wb:2 * wb]).astype(BF16)

    n = len(items)
    sc, pr = {}, {}
    for step in range(n + 2):
        if step < n:
            sc[step] = scores(*items[step])
        if 1 <= step < n + 1:
            pr[step - 1] = softmax(items[step - 1][1], sc.pop(step - 1))
        if step >= 2:
            output(*items[step - 2], *pr.pop(step - 2))


def _swa(cq, ckt, cv, sink_row, nb, s, cl, bs):
    wb = SWA_BLOCK
    nblk = s // wb
    nq = 4
    assert nblk % nq == 0
    npair = nblk // nq
    kern = functools.partial(_swa_kernel, nblk=nblk, cl=cl, nq=nq)
    prev = lambda b, j: b * nblk + jnp.maximum(nq * j - 1, 0)
    nxt = lambda b, j: b * nblk + jnp.minimum(nq * j + nq, nblk - 1)
    own = lambda b, j: b * npair + j
    return pl.pallas_call(
        kern,
        grid=(nb, npair),
        in_specs=[pl.BlockSpec((nq * wb, 512), lambda b, j: (own(b, j), 0)),
                  pl.BlockSpec((128, wb), lambda b, j: (0, prev(b, j))),
                  pl.BlockSpec((128, nq * wb), lambda b, j: (0, own(b, j))),
                  pl.BlockSpec((128, wb), lambda b, j: (0, nxt(b, j))),
                  pl.BlockSpec((128, cl), lambda b, j: (0, bs // cl + b)),
                  pl.BlockSpec((wb, 128), lambda b, j: (prev(b, j), 0)),
                  pl.BlockSpec((nq * wb, 128), lambda b, j: (own(b, j), 0)),
                  pl.BlockSpec((wb, 128), lambda b, j: (nxt(b, j), 0)),
                  pl.BlockSpec((cl, 128), lambda b, j: (bs // cl + b, 0)),
                  pl.BlockSpec((1, 128), lambda b, j: (0, 0))],
        out_specs=pl.BlockSpec((nq * wb, 512), lambda b, j: (own(b, j), 0)),
        out_shape=jax.ShapeDtypeStruct((bs, 512), BF16),
        compiler_params=_cp(("parallel", "parallel")),
        name="swa",
    )(cq, ckt, ckt, ckt, ckt, cv, cv, cv, cv, sink_row)


def _na_kernel(q_ref, k_ref, v_ref, kx_ref, vx_ref, cols_ref, o_ref, bias_ref, *, rows, unroll):
    for h in range(2):
        for off in range(NA_KH):
            for i in range(NA_KH):
                bias_ref[h, off, :, GRID_W * i:GRID_W * (i + 1)] = cols_ref[h, off + i]
    half = lax.broadcasted_iota(I32, (1, 128), 1) // HEAD_DIM
    zero = jnp.zeros((), BF16)
    vx = vx_ref[...]
    kxt = kx_ref[...].astype(F32).T.astype(BF16)
    span = NA_KH * GRID_W

    def scores(r):
        rs = jnp.clip(r - NA_KH // 2, 0, rows - NA_KH)
        off = rs - r + NA_KH - 1
        q0 = pl.multiple_of(r * GRID_W, GRID_W)
        k0 = pl.multiple_of(rs * GRID_W, GRID_W)
        q2 = q_ref[pl.ds(q0, GRID_W), :]
        qst = jnp.concatenate([jnp.where(half == 0, q2, zero), jnp.where(half == 1, q2, zero)], axis=0)
        s_loc = (_dot_nt(qst, k_ref[pl.ds(k0, span), :])
                 + jnp.concatenate([bias_ref[0, off], bias_ref[1, off]], axis=0))
        return q0, k0, s_loc, _dot(qst, kxt)

    def softmax(s_loc, s_ctx):
        m = jnp.maximum(jnp.max(s_loc, axis=-1, keepdims=True), jnp.max(s_ctx, axis=-1, keepdims=True))
        p_loc = jnp.exp2(s_loc - m)
        p_ctx = jnp.exp2(s_ctx - m)
        den = jnp.sum(p_loc, axis=-1, keepdims=True) + jnp.sum(p_ctx, axis=-1, keepdims=True)
        return p_loc.astype(BF16), p_ctx.astype(BF16), den

    def output(q0, k0, p_loc, p_ctx, den):
        ost = (_dot(p_loc, v_ref[pl.ds(k0, span), :]) + _dot(p_ctx, vx)) / den
        o = jnp.where(half == 0, ost[0:GRID_W], ost[GRID_W:2 * GRID_W])
        o_ref[pl.ds(q0, GRID_W), :] = o.astype(BF16)

    def body(i, carry):
        sc, pr = {}, {}
        for step in range(unroll + 2):
            if step < unroll:
                sc[step] = scores(i * unroll + step)
            if 1 <= step < unroll + 1:
                q0, k0, s_loc, s_ctx = sc.pop(step - 1)
                pr[step - 1] = (q0, k0) + softmax(s_loc, s_ctx)
            if step >= 2:
                output(*pr.pop(step - 2))
        return carry

    lax.fori_loop(0, rows // unroll, body, 0)


def _na(dq, dk, dv, bias_cols, nb, s, cl, bs):
    rows = s // GRID_W
    unroll = 8
    assert rows % unroll == 0
    kern = functools.partial(_na_kernel, rows=rows, unroll=unroll)
    seq = pl.BlockSpec((s, 128), lambda b, p: (b, p))
    ctx = pl.BlockSpec((cl, 128), lambda b, p: (bs // cl + b, p))
    return pl.pallas_call(
        kern,
        grid=(nb, NA_HEADS // 2),
        in_specs=[seq, seq, seq, ctx, ctx,
                  pl.BlockSpec((2, 2 * NA_KH - 1, GRID_W, GRID_W), lambda b, p: (p, 0, 0, 0))],
        out_specs=seq,
        out_shape=jax.ShapeDtypeStruct((bs, 512), BF16),
        scratch_shapes=[pltpu.VMEM((2, NA_KH, GRID_W, NA_KH * GRID_W), F32)],
        compiler_params=_cp(("parallel", "parallel")),
        name="na",
    )(dq, dk, dv, dk, dv, bias_cols)


def _na_bias_cols(rpb):
    c = np.arange(GRID_W)
    qs = np.clip(c - NA_KW // 2, 0, GRID_W - NA_KW)
    kc = np.arange(GRID_W)
    ok = (kc[None, :] >= qs[:, None]) & (kc[None, :] < qs[:, None] + NA_KW)
    dc = np.clip(kc[None, :] - c[:, None] + NA_KW - 1, 0, 2 * NA_KW - 2)
    sel = (np.arange(2 * NA_KW - 1)[:, None, None] == dc[None]).astype(np.float32)
    cols = jnp.einsum("hab,bck->hack", rpb.astype(F32), sel, precision=HI)
    return jnp.where(ok[None, None], cols * LOG2E, NEG)


def _mix1_kernel(x_ref, oc_ref, od_ref, mod_ref, wo_ref, g2_ref, rw_ref, rb_ref, h2buf_ref,
                 xn_ref, h2_ref, ei_ref, ga_ref):
    del h2buf_ref
    y = _dot(oc_ref[...], wo_ref[0:512, :]) + _dot(od_ref[...], wo_ref[512:1024, :])
    xn, h2, eidx, gates = _post_mixer(x_ref[...], y, mod_ref[0], g2_ref[...], rw_ref[...], rb_ref[...])
    xn_ref[...] = xn
    h2_ref[...] = h2.astype(BF16)
    ei_ref[...] = eidx
    ga_ref[...] = gates


def _mix1(x, oc, od, mods, wo, g2, rw, rb, h2buf, tm, bs, s):
    d = x.shape[1]
    row = lambda w: pl.BlockSpec((tm, w), lambda i: (i, 0))
    full = lambda a: pl.BlockSpec(a.shape, lambda i: (0,) * a.ndim)
    return pl.pallas_call(
        _mix1_kernel,
        grid=(bs // tm,),
        in_specs=[row(d), row(512), row(512), pl.BlockSpec((1, N_MOD, d), lambda i: (i * tm // s, 0, 0)),
                  full(wo), full(g2), full(rw), full(rb), pl.BlockSpec(memory_space=pl.ANY)],
        out_specs=[row(d), row(d), pl.BlockSpec((8, tm), lambda i: (0, i)), row(128)],
        out_shape=[jax.ShapeDtypeStruct((bs, d), F32), jax.ShapeDtypeStruct(h2buf.shape, BF16),
                   jax.ShapeDtypeStruct((8, bs), I32), jax.ShapeDtypeStruct((bs, 128), F32)],
        input_output_aliases={8: 1},
        compiler_params=_cp(("parallel",)),
        name="mix1",
    )(x, oc, od, mods, wo, g2, rw, rb, h2buf)


def _final_kernel(x_ref, y1_ref, y2_ref, ga_ref, mod_ref, g_ref, o_ref):
    x = _moe_combine(x_ref[...], y1_ref[...], y2_ref[...], ga_ref[...], mod_ref[0][5:6])
    o_ref[...] = _rms(x, g_ref[...])


def _final(x, y, gates, mods, g, tm, s):
    r, d = x.shape
    second = pl.BlockSpec((tm, d), lambda i: (i + r // tm, 0))
    row = lambda w: pl.BlockSpec((tm, w), lambda i: (i, 0))
    return pl.pallas_call(
        _final_kernel,
        grid=(r // tm,),
        in_specs=[row(d), row(d), second, row(128), pl.BlockSpec((1, N_MOD, d), lambda i: (i * tm // s, 0, 0)),
                  pl.BlockSpec((1, d), lambda i: (0, 0))],
        out_specs=row(d),
        out_shape=jax.ShapeDtypeStruct((r, d), F32),
        compiler_params=_cp(("parallel",)),
        name="final",
    )(x, y, y, gates, mods, g)


def _rope_tables(s):
    nf = HEAD_DIM // 4
    t = np.arange(s)
    inv = ROPE_THETA ** (-np.arange(nf, dtype=np.float64) / nf)
    ar = (t // GRID_W)[:, None] * inv
    ac = (t % GRID_W)[:, None] * inv
    cos = np.concatenate([np.cos(ar), np.cos(ar), np.cos(ac), np.cos(ac)], axis=1)
    sin = np.concatenate([-np.sin(ar), np.sin(ar), -np.sin(ac), np.sin(ac)], axis=1)
    return (jnp.asarray(np.concatenate([cos, cos], axis=1), F32),
            jnp.asarray(np.concatenate([sin, sin], axis=1), F32))


def kernel(x, c, ctx, c_ctx, ada_w, ada_b, norm1_g, norm2_g, ev_w_in, ev_w_out, sc_conv_w, dn_conv_w, dn_a_log, dn_dt_bias, dn_onorm_g, od_w_in, od_w_out, swa_sink, na_rpb, router_w, router_b, moe_w_gate, moe_w_up, moe_w_down, final_g):
    nb, s, d = x.shape
    cl = ctx.shape[1]
    bs = nb * s
    tm = 512
    ts = 256
    tmm = 512
    assert d == 1024 and s % tm == 0 and (nb * cl) % tm == 0 and cl % ts == 0 and s % ts == 0
    assert s // GRID_W >= NA_KH and bs % cl == 0 and nb + 1 <= 8

    xl, xc = x.reshape(bs, d), ctx.reshape(nb * cl, d)
    cc = jnp.zeros((8, d), F32).at[:nb].set(c).at[nb].set(c_ctx)
    mods = _ada(cc, ada_w, ada_b).reshape(ada_w.shape[0], 8, N_MOD, d)
    rw32 = jnp.pad(router_w, ((0, 0), (0, 128 - N_EXPERTS)))
    rw_hi = rw32.astype(BF16)
    rw = jnp.concatenate([rw_hi, (rw32 - rw_hi.astype(F32)).astype(BF16)], axis=1)
    rb = router_b.reshape(N_EXPERTS, 1)
    row = lambda v: v.reshape(1, -1)

    w_in0 = jnp.pad(ev_w_in[0], ((0, 0), (0, 3712 - ev_w_in.shape[-1]))).astype(BF16)
    sc, qkv, z, bg = _inproj0(xl, xc, mods[0], row(norm1_g[0]), w_in0, tm, bs, s, nb)
    pad16 = lambda v: jnp.pad(v.reshape(-1), (8, 128 - 16)).reshape(1, 128)
    uf, ub, wf, wb, qf, qb, kf, kb, af, ab, gc = _dnchunk(qkv, bg, dn_conv_w[0], pad16(dn_a_log[0]),
                                                          pad16(dn_dt_bias[0]), ts, bs, s, cl)
    o_pairs = _dnscan(uf, ub, wf, wb, qf, qb, kf, kb, af, ab, gc, nb, s, cl, bs)
    r_all = bs + nb * cl
    moe_rows = lambda t: (2 * t // tmm + N_EXPERTS) * tmm
    x0, h2, ei, ga = _mix0(xl, xc, sc, o_pairs, z, mods[0], sc_conv_w[0], row(dn_onorm_g[0]),
                           ev_w_out[0].astype(BF16), row(norm2_g[0]), rw, rb,
                           jnp.zeros((moe_rows(r_all), d), BF16), ts, bs, s, cl, nb)
    y = _moe(h2, r_all, ei[0], ei[1], moe_w_gate, moe_w_up, moe_w_down, 0, tmm)

    perm = np.concatenate([np.arange(HEAD_DIM) + HEAD_DIM * (g + 4 * a) for g in range(4) for a in range(2)])
    w1 = od_w_in[0]
    w_in1 = jnp.concatenate([w1[:, 0:512][:, perm], w1[:, 512:]], axis=1).astype(BF16)
    wo1 = od_w_out[0]
    w_out1 = jnp.concatenate([wo1[0:512][perm], wo1[512:]], axis=0).astype(BF16)
    sink_row = jnp.pad(swa_sink[0][np.array([g + 4 * a for g in range(4) for a in range(2)])] * LOG2E,
                       (0, 128 - SWA_HEADS)).reshape(1, 128)
    cos, sin = _rope_tables(s)
    x1, cq, ckt, cv, dq, dk, dv = _inproj1(x0, y, ga, mods[0], mods[1], row(norm1_g[1]), w_in1, cos, sin,
                                           tm, bs, s, nb)
    oc = _swa(cq, ckt, cv, sink_row, nb, s, cl, bs)
    od = _na(dq, dk, dv, _na_bias_cols(na_rpb[0]), nb, s, cl, bs)
    assert moe_rows(bs) <= h2.shape[0]
    x2, h2, ei, ga = _mix1(x1, oc, od, mods[1], w_out1, row(norm2_g[1]), rw, rb, h2, tm, bs, s)
    y = _moe(h2, bs, ei[0], ei[1], moe_w_gate, moe_w_up, moe_w_down, 1, tmm)
    out = _final(x2, y, ga, mods[1], row(final_g), tm, s)
    return out.reshape(nb, s, d)
```

```python
import functools
import math

import numpy as np
import jax
import jax.numpy as jnp
from jax import lax
from jax.experimental import pallas as pl
from jax.experimental.pallas import tpu as pltpu

F32 = jnp.float32
BF16 = jnp.bfloat16
I32 = jnp.int32
HI = lax.Precision.HIGHEST

EPS = 1e-6
N_MOD = 6
GRID_W = 64
HEAD_DIM = 64
DN_HEADS = 4
DN_HD = 128
DN_CHUNK = 64
SWA_HEADS = 8
SWA_KV = 2
SWA_BLOCK = 128
SWA_WINDOW = 128
NA_HEADS = 8
NA_KH = 8
NA_KW = 16
ROPE_THETA = 10000.0
N_EXPERTS = 16
N_GROUPS = 4
NEG = -1e30
LOG2E = 1.4426950408889634
VMEM_LIMIT = 56 * 1024 * 1024


def _cp(sem, vmem=VMEM_LIMIT):
    return pltpu.CompilerParams(dimension_semantics=sem, vmem_limit_bytes=vmem)


def _dot(a, b, precision=None):
    return jnp.dot(a, b, preferred_element_type=F32, precision=precision)


def _dot_nt(a, b, precision=None):
    return lax.dot_general(a, b, (((1,), (1,)), ((), ())), preferred_element_type=F32, precision=precision)


def _dot_tn(a, b, precision=None):
    return lax.dot_general(a, b, (((0,), (0,)), ((), ())), preferred_element_type=F32, precision=precision)


def _silu(x):
    return x * jax.nn.sigmoid(x)


def _rms(x, g):
    return x * lax.rsqrt(jnp.mean(x * x, axis=-1, keepdims=True) + EPS) * g


def _ada_kernel(cc_ref, w_ref, b_ref, o_ref):
    a = _silu(cc_ref[...])
    o_ref[0] = _dot(a, w_ref[0], HI) + b_ref[0]


def _ada(cc, ada_w, ada_b):
    depth, d, n = ada_w.shape
    tn = 1536
    return pl.pallas_call(
        _ada_kernel,
        grid=(depth, n // tn),
        in_specs=[pl.BlockSpec((8, d), lambda l, j: (0, 0)),
                  pl.BlockSpec((1, d, tn), lambda l, j: (l, 0, j)),
                  pl.BlockSpec((1, 1, tn), lambda l, j: (l, 0, j))],
        out_specs=pl.BlockSpec((1, 8, tn), lambda l, j: (l, 0, j)),
        out_shape=jax.ShapeDtypeStruct((depth, 8, n), F32),
        compiler_params=_cp(("parallel", "parallel")),
        name="ada",
    )(cc, ada_w, ada_b.reshape(depth, 1, n))


def _mod_index(i, tm, bs, s, nb):
    row0 = i * tm
    return jnp.where(row0 < bs, row0 // s, nb)


def _token_specs(tm, bs, d):
    nlat = bs // tm
    return [pl.BlockSpec((tm, d), lambda i: (jnp.minimum(i, nlat - 1), 0)),
            pl.BlockSpec((tm, d), lambda i: (jnp.maximum(i - nlat, 0), 0))]


def _token_rows(xl_ref, xc_ref, tm, bs):
    return jnp.where(pl.program_id(0) * tm < bs, xl_ref[...], xc_ref[...])


def _inproj0_kernel(xl_ref, xc_ref, mod_ref, g_ref, w_ref, sc_ref, qkv_ref, z_ref, bg_ref, *, tm, bs):
    m = mod_ref[0]
    in_lat = pl.program_id(0) * tm < bs
    for rows in (slice(0, tm // 2), slice(tm // 2, tm)):
        x = jnp.where(in_lat, xl_ref[rows], xc_ref[rows])
        h = (_rms(x, g_ref[...]) * (1.0 + m[1:2]) + m[0:1]).astype(BF16)
        sc_ref[rows] = _dot(h, w_ref[:, 0:1536]).astype(BF16)
        qkv_ref[rows] = _dot(h, w_ref[:, 1536:3072]).astype(BF16)
        z_ref[rows] = _dot(h, w_ref[:, 3072:3584]).astype(BF16)
        bg_ref[rows] = _dot(h, w_ref[:, 3584:3712])


def _inproj0(xl, xc, mods, g, w, tm, bs, s, nb):
    d = xl.shape[1]
    r = bs + xc.shape[0]
    mi = functools.partial(_mod_index, tm=tm, bs=bs, s=s, nb=nb)
    return pl.pallas_call(
        functools.partial(_inproj0_kernel, tm=tm, bs=bs),
        grid=(r // tm,),
        in_specs=_token_specs(tm, bs, d) + [
            pl.BlockSpec((1, N_MOD, d), lambda i: (mi(i), 0, 0)),
            pl.BlockSpec((1, d), lambda i: (0, 0)),
            pl.BlockSpec(w.shape, lambda i: (0, 0))],
        out_specs=[pl.BlockSpec((tm, 1536), lambda i: (i, 0)),
                   pl.BlockSpec((tm, 1536), lambda i: (i, 0)),
                   pl.BlockSpec((tm, 512), lambda i: (i, 0)),
                   pl.BlockSpec((tm, 128), lambda i: (i, 0))],
        out_shape=[jax.ShapeDtypeStruct((r, 1536), BF16), jax.ShapeDtypeStruct((r, 1536), BF16),
                   jax.ShapeDtypeStruct((r, 512), BF16), jax.ShapeDtypeStruct((r, 128), F32)],
        compiler_params=_cp(("parallel",)),
        name="inproj0",
    )(xl, xc, mods, g, w)


def _seq_edges(i, ts, bs, s, cl):
    row0 = i * ts
    in_lat = row0 < bs
    r_in = jnp.where(in_lat, row0 % s, (row0 - bs) % cl)
    seqlen = jnp.where(in_lat, s, cl)
    return r_in == 0, r_in + ts == seqlen


def _shifted(x, prev_row, next_row):
    n = x.shape[0]
    rows = lax.broadcasted_iota(I32, x.shape, 0)
    xp = jnp.where(rows == 0, prev_row, pltpu.roll(x, 1, 0))
    xn = jnp.where(rows == n - 1, next_row, pltpu.roll(x, n - 1, 0))
    return xp, xn


HALO = 16


def _halo_specs(ts, width, r):
    nblk = r // HALO
    k = ts // HALO
    return [pl.BlockSpec((HALO, width), lambda i: (jnp.maximum(i * k - 1, 0), 0)),
            pl.BlockSpec((HALO, width), lambda i: (jnp.minimum((i + 1) * k, nblk - 1), 0))]


def _dnprep_tile(x_ref, prev_ref, next_ref, bg_ref, cw_ref, alog_ref, dt_ref,
                 q_ref, k_ref, v_ref, bga_ref, *, ts, bs, s, cl):
    first, last = _seq_edges(pl.program_id(0), ts, bs, s, cl)
    for c in range(12):
        sl = slice(128 * c, 128 * c + 128)
        x = x_ref[:, sl].astype(F32)
        pr = jnp.where(first, 0.0, prev_ref[HALO - 1:HALO, sl].astype(F32))
        nx = jnp.where(last, 0.0, next_ref[0:1, sl].astype(F32))
        xp, xn = _shifted(x, pr, nx)
        w = cw_ref[:, sl]
        y = _silu(xp * w[0:1] + x * w[1:2] + xn * w[2:3])
        hs = slice(128 * (c % 4), 128 * (c % 4) + 128)
        if c < 8:
            y = y * lax.rsqrt(jnp.sum(y * y, axis=-1, keepdims=True) + EPS)
        if c < 4:
            q_ref[:, hs] = y * DN_HD ** -0.5
        elif c < 8:
            k_ref[:, hs] = y
        else:
            v_ref[:, hs] = y
    b = bg_ref[...]
    cols = lax.broadcasted_iota(I32, b.shape, 1)
    beta = jax.nn.sigmoid(b)
    t = b + dt_ref[...]
    softplus = jnp.maximum(t, 0.0) + jnp.log1p(jnp.exp(-jnp.abs(t)))
    g = -jnp.exp(alog_ref[...]) * softplus
    bga_ref[...] = jnp.where(cols < 8, beta, jnp.where(cols < 16, g, 0.0))


def _dnchunk_kernel(x_ref, prev_ref, next_ref, bgraw_ref, cw_ref, alog_ref, dt_ref,
                    uf_ref, ub_ref, wf_ref, wb_ref, qf_ref, qb_ref, kf_ref, kb_ref, af_ref, ab_ref, gc_ref,
                    q_ref, k_ref, v_ref, bg_ref, *, nchunks, bs, s, cl):
    _dnprep_tile(x_ref, prev_ref, next_ref, bgraw_ref, cw_ref, alog_ref, dt_ref, q_ref, k_ref, v_ref, bg_ref,
                 ts=nchunks * DN_CHUNK, bs=bs, s=s, cl=cl)
    outs = ((uf_ref, wf_ref, qf_ref, kf_ref, af_ref), (ub_ref, wb_ref, qb_ref, kb_ref, ab_ref))
    c, nh = DN_CHUNK, DN_HEADS
    head_of_col = lax.broadcasted_iota(I32, (1, nh * c), 1) // c
    zero = jnp.zeros((), BF16)

    def block_diag(x):
        return jnp.concatenate([jnp.where(head_of_col == h, x, zero) for h in range(nh)], axis=0)

    chains = []
    for cc in range(nchunks):
        chains += _dnchunk_setup(slice(cc * c, (cc + 1) * c), q_ref, k_ref, v_ref, bg_ref, gc_ref)
    for ch in chains:
        ch["tm"] = ch["nmat"]
        nb16 = ch["nmat"].astype(BF16)
        ch["npow"] = _dot(nb16, block_diag(nb16))
    for _ in range(4):
        for ch in chains:
            nb16 = ch["npow"].astype(BF16)
            ch["both"] = _dot(jnp.concatenate([nb16, ch["tm"].astype(BF16)], axis=0), block_diag(nb16))
        for ch in chains:
            ch["tm"] = ch["tm"] + ch["npow"] + ch["both"][c:2 * c]
            ch["npow"] = ch["both"][0:c]
    for ch in chains:
        ch["both"] = _dot(ch["tm"].astype(BF16), block_diag(ch["npow"].astype(BF16)))
    for ch in chains:
        ch["tm"] = ch["tm"] + ch["npow"] + ch["both"]
    for ch in chains:
        ch["uw"] = ch["rhs"] + _dot(block_diag(ch["tm"].astype(BF16)), ch["rhs"].astype(BF16))
    for ch in chains:
        u_ref, w_ref, qd_ref, kd_ref, at_ref = outs[ch["d"]]
        rows, uw = ch["rows"], ch["uw"]
        for h in range(nh):
            hs = slice(DN_HD * h, DN_HD * h + DN_HD)
            rs = slice(c * h, c * h + c)
            u_ref[rows, hs] = uw[rs, 0:DN_HD]
            w_ref[rows, hs] = uw[rs, DN_HD:2 * DN_HD].astype(BF16)
            qd_ref[rows, hs] = ch["qd"][rs]
        kd_ref[slice(2 * rows.start, 2 * rows.stop)] = ch["kd"].T.astype(BF16)
        at_ref[rows] = ch["att"].astype(BF16)


def _dnchunk_setup(rows, q_ref, k_ref, v_ref, bg_ref, gc_ref):
    c, nh = DN_CHUNK, DN_HEADS
    n = c * nh
    bg = bg_ref[rows]
    i64 = lax.broadcasted_iota(I32, (c, c), 0)
    j64 = lax.broadcasted_iota(I32, (c, c), 1)
    cols = lax.broadcasted_iota(I32, bg.shape, 1)
    gcf = _dot((i64 >= j64).astype(F32), bg, HI)
    gcb = _dot((i64 <= j64).astype(F32), bg, HI)
    gc = jnp.where(cols >= 12, gcb, gcf)
    gc_ref[rows] = gc
    gct = gc.T
    ii = lax.broadcasted_iota(I32, (c, n), 0)
    jj = lax.broadcasted_iota(I32, (c, n), 1)
    head_of_col = jj // c
    jj = jj % c

    def stack(ref):
        return jnp.concatenate([ref[rows, DN_HD * h:DN_HD * h + DN_HD] for h in range(nh)], axis=0)

    def stacked_cols(arr, r0, r1, col0):
        return jnp.concatenate([jnp.broadcast_to(arr[r0:r1, col0 + h:col0 + h + 1], (c, DN_HD)) for h in range(nh)],
                               axis=0)

    def side_by_side_cols(arr, col0):
        out = arr[:, col0 + nh - 1:col0 + nh]
        for h in range(nh - 2, -1, -1):
            out = jnp.where(head_of_col == h, arr[:, col0 + h:col0 + h + 1], out)
        return out

    def diag_blocks(x):
        out = x[(nh - 1) * c:nh * c]
        for h in range(nh - 2, -1, -1):
            out = jnp.where(head_of_col == h, x[h * c:(h + 1) * c], out)
        return out

    kst, qst, vst = stack(k_ref), stack(q_ref), stack(v_ref)
    kb = kst.astype(BF16)
    kq = _dot_nt(jnp.concatenate([kb, qst.astype(BF16)], axis=0), kb)
    kkt, qkt = diag_blocks(kq[0:n]), diag_blocks(kq[n:2 * n])
    chains = []
    for d in range(2):
        incl = (ii >= jj) if d == 0 else (ii <= jj)
        strict = (ii > jj) if d == 0 else (ii < jj)
        last = c - 1 if d == 0 else 0
        grow = jnp.concatenate([gct[8 + 4 * d + h:9 + 4 * d + h, :] for h in range(nh)], axis=1)
        decay = jnp.exp(jnp.where(incl, side_by_side_cols(gc, 8 + 4 * d) - grow, NEG))
        nmat = jnp.where(strict, -(side_by_side_cols(bg, 4 * d) * kkt * decay), 0.0)
        b1 = stacked_cols(bg, 0, c, 4 * d)
        gcol = stacked_cols(gc, 0, c, 8 + 4 * d)
        glast = stacked_cols(gc, last, last + 1, 8 + 4 * d)
        e1 = jnp.exp(gcol)
        chains.append(dict(
            d=d, rows=rows, nmat=nmat,
            rhs=jnp.concatenate([b1 * vst, (b1 * e1) * kst], axis=1),
            qd=(qst * e1).astype(BF16),
            kd=kst * jnp.exp(glast - gcol),
            att=qkt * decay))
    return chains


def _dnchunk(qkv, bg, cw, alog_row, dt_row, ts, bs, s, cl):
    r = qkv.shape[0]
    nchunks = ts // DN_CHUNK
    c = ts
    row = lambda w: pl.BlockSpec((c, w), lambda i: (i, 0))
    shp = lambda w, dt: jax.ShapeDtypeStruct((r, w), dt)
    return pl.pallas_call(
        functools.partial(_dnchunk_kernel, nchunks=nchunks, bs=bs, s=s, cl=cl),
        grid=(r // c,),
        in_specs=[row(1536)] + _halo_specs(ts, 1536, r) + [
            row(128),
            pl.BlockSpec((3, 1536), lambda i: (0, 0)),
            pl.BlockSpec((1, 128), lambda i: (0, 0)),
            pl.BlockSpec((1, 128), lambda i: (0, 0))],
        scratch_shapes=[pltpu.VMEM((c, 512), F32)] * 3 + [pltpu.VMEM((c, 128), F32)],
        out_specs=[row(512)] * 6 + [pl.BlockSpec((2 * c, 256), lambda i: (i, 0))] * 2 + [row(256), row(256), row(128)],
        out_shape=([shp(512, F32)] * 2 + [shp(512, BF16)] * 4 + [jax.ShapeDtypeStruct((2 * r, 256), BF16)] * 2
                   + [shp(256, BF16)] * 2 + [shp(128, F32)]),
        compiler_params=_cp(("parallel",)),
        name="dnchunk",
    )(qkv, qkv, qkv, bg, cw, alog_row, dt_row)


def _dnscan_kernel(*refs, nsub, nb):
    nchain = 2 * nb
    ins, outs, s_ref = refs[:6 * nchain], refs[6 * nchain:7 * nchain], refs[7 * nchain]

    @pl.when(pl.program_id(0) == 0)
    def _():
        s_ref[...] = jnp.zeros_like(s_ref)

    c, nh = DN_CHUNK, DN_HEADS
    head_of_lane = lax.broadcasted_iota(I32, (1, nh * DN_HD), 1) // DN_HD
    head_of_col = lax.broadcasted_iota(I32, (1, nh * c), 1) // c
    zero = jnp.zeros((), BF16)

    def block_diag(tile, head_ids):
        return jnp.concatenate([jnp.where(head_ids == h, tile, zero) for h in range(nh)], axis=0)

    states = [s_ref[ci] for ci in range(nchain)]
    for sub in range(nsub):
        work = []
        for ci in range(nchain):
            d = ci % 2
            u_ref, w_ref, qd_ref, kd_ref, at_ref, g_ref = ins[6 * ci:6 * ci + 6]
            last = c - 1 if d == 0 else 0
            k = sub if d == 0 else nsub - 1 - sub
            rows = slice(c * k, c * k + c)
            g = g_ref[rows]
            decay = jnp.concatenate(
                [jnp.broadcast_to(jnp.exp(g[last:last + 1, 8 + 4 * d + h:9 + 4 * d + h]), (DN_HD, DN_HD))
                 for h in range(nh)], axis=0)
            ust = jnp.concatenate([u_ref[rows, DN_HD * h:DN_HD * h + DN_HD] for h in range(nh)], axis=0)
            stb = states[ci].astype(BF16)
            vnew = ust - _dot(block_diag(w_ref[rows], head_of_lane), stb)
            work.append((rows, decay, stb, vnew.astype(BF16)))
        for ci in range(nchain):
            u_ref, w_ref, qd_ref, kd_ref, at_ref, g_ref = ins[6 * ci:6 * ci + 6]
            rows, decay, stb, vnb = work[ci]
            o = (_dot(block_diag(qd_ref[rows], head_of_lane), stb)
                 + _dot(block_diag(at_ref[rows], head_of_col), vnb))
            kdt = kd_ref[slice(2 * rows.start, 2 * rows.stop)]
            states[ci] = states[ci] * decay + _dot(block_diag(kdt, head_of_col), vnb)
            for h in range(nh):
                outs[ci][rows, DN_HD * h:DN_HD * h + DN_HD] = o[c * h:c * h + c].astype(BF16)
    for ci in range(nchain):
        s_ref[ci] = states[ci]


def _dnscan(uf, ub, wf, wb, qf, qb, kf, kb, af, ab, gc, nb, s, cl, bs):
    nsub = 4
    c = DN_CHUNK * nsub
    assert cl % c == 0 and s % c == 0 and bs % c == 0
    ncc, ncl = cl // c, s // c
    ns = ncc + ncl

    def src_block(b, d):
        if d == 0:
            return lambda t: jnp.where(t < ncc, bs // c + b * ncc + t, b * ncl + t - ncc)
        return lambda t: jnp.where(t < ncc, bs // c + b * ncc + (ncc - 1 - t), b * ncl + (ncl - 1 - (t - ncc)))

    def dst_block(d):
        if d == 0:
            return lambda t: jnp.where(t < ncc, ncl + t, t - ncc)
        return lambda t: jnp.where(t < ncc, ncl + (ncc - 1 - t), ncl - 1 - (t - ncc))

    in_specs, args, out_specs = [], [], []
    for b in range(nb):
        for d, group in enumerate(((uf, wf, qf, kf, af, gc), (ub, wb, qb, kb, ab, gc))):
            idx = src_block(b, d)
            blk = lambda w, idx=idx: pl.BlockSpec((c, w), lambda t: (idx(t), 0))
            in_specs += [blk(512), blk(512), blk(512), pl.BlockSpec((2 * c, 256), lambda t, idx=idx: (idx(t), 0)),
                         blk(256), blk(128)]
            args += list(group)
            out_specs.append(pl.BlockSpec((c, 512), lambda t, f=dst_block(d): (f(t), 0)))
    outs = pl.pallas_call(
        functools.partial(_dnscan_kernel, nsub=nsub, nb=nb),
        grid=(ns,),
        in_specs=in_specs,
        out_specs=out_specs,
        out_shape=[jax.ShapeDtypeStruct((s + cl, 512), BF16)] * (2 * nb),
        scratch_shapes=[pltpu.VMEM((2 * nb, DN_HEADS * DN_HD, DN_HD), F32)],
        compiler_params=_cp(("arbitrary",)),
        name="dnscan",
    )(*args)
    return [(outs[2 * b], outs[2 * b + 1]) for b in range(nb)]


def _route(logits, bias_col):
    epg = N_EXPERTS // N_GROUPS
    tm = logits.shape[0]
    scores = jax.nn.sigmoid(logits.T[0:N_EXPERTS])
    gsel = scores + bias_col
    row = lambda a, k: a[k:k + 1]
    best = gidx = None
    for g in range(N_GROUPS):
        a = [row(gsel, epg * g + k) for k in range(epg)]
        m01, n01 = jnp.maximum(a[0], a[1]), jnp.minimum(a[0], a[1])
        m23, n23 = jnp.maximum(a[2], a[3]), jnp.minimum(a[2], a[3])
        gs = jnp.maximum(m01, m23) + jnp.maximum(jnp.minimum(m01, m23), jnp.maximum(n01, n23))
        if g == 0:
            best, gidx = gs, jnp.zeros_like(gs)
        else:
            better = gs > best
            best = jnp.where(better, gs, best)
            gidx = jnp.where(better, float(g), gidx)
    sel = [None] * epg
    raw = [None] * epg
    for g in range(N_GROUPS):
        for k in range(epg):
            v, u = row(gsel, epg * g + k), row(scores, epg * g + k)
            sel[k] = v if g == 0 else jnp.where(gidx == g, v, sel[k])
            raw[k] = u if g == 0 else jnp.where(gidx == g, u, raw[k])
    v1, e1, w1 = sel[0], jnp.zeros_like(gidx), raw[0]
    for k in range(1, epg):
        better = sel[k] > v1
        v1 = jnp.where(better, sel[k], v1)
        e1 = jnp.where(better, float(k), e1)
        w1 = jnp.where(better, raw[k], w1)
    v2 = e2 = w2 = None
    for k in range(epg):
        cand = jnp.where(e1 == k, -jnp.inf, sel[k])
        if k == 0:
            v2, e2, w2 = cand, jnp.zeros_like(gidx), raw[0]
        else:
            better = cand > v2
            v2 = jnp.where(better, cand, v2)
            e2 = jnp.where(better, float(k), e2)
            w2 = jnp.where(better, raw[k], w2)
    tot = w1 + w2
    eidx = jnp.concatenate([gidx * epg + e1, gidx * epg + e2, jnp.zeros((6, tm), F32)], axis=0).astype(I32)
    gates_t = jnp.concatenate([w1 / tot, w2 / tot, jnp.zeros((126, tm), F32)], axis=0)
    return eidx, gates_t.T


def _post_mixer(x, y, m, g2, rw, rb):
    xn = x + m[2:3] * y
    h2 = _rms(xn, g2) * (1.0 + m[4:5]) + m[3:4]
    hi = h2.astype(BF16)
    lo = (h2 - hi.astype(F32)).astype(BF16)
    hw = _dot(hi, rw)
    logits = hw[:, 0:128] + (hw[:, 128:256] + _dot(lo, rw[:, 0:128]))
    eidx, gates = _route(logits, rb)
    return xn, hi, eidx, gates


def _mix0_kernel(xl_ref, xc_ref, sc_ref, prev_ref, next_ref, z_ref, mod_ref, cw_ref, on_ref, wo_ref,
                 g2_ref, rw_ref, rb_ref, h2buf_ref, *rest, ts, bs, s, cl, nb):
    del h2buf_ref
    o_refs = rest[:2 * nb]
    xn_ref, h2_ref, ei_ref, ga_ref = rest[2 * nb:]
    first, last = _seq_edges(pl.program_id(0), ts, bs, s, cl)
    row0 = pl.program_id(0) * ts
    batch = jnp.where(row0 < bs, row0 // s, (row0 - bs) // cl)
    ya = []
    for c in range(4):
        sl = slice(128 * c, 128 * c + 128)
        sg = slice(512 + 128 * c, 512 + 128 * c + 128)
        sx = slice(1024 + 128 * c, 1024 + 128 * c + 128)
        f32 = lambda ref, rows, cols: ref[rows, cols].astype(F32)
        u = f32(sc_ref, slice(None), sg) * f32(sc_ref, slice(None), sx)
        pr = jnp.where(first, 0.0, f32(prev_ref, slice(HALO - 1, HALO), sg) * f32(prev_ref, slice(HALO - 1, HALO), sx))
        nx = jnp.where(last, 0.0, f32(next_ref, slice(0, 1), sg) * f32(next_ref, slice(0, 1), sx))
        up, un = _shifted(u, pr, nx)
        w = cw_ref[:, sl]
        ya.append((f32(sc_ref, slice(None), sl) * (up * w[0:1] + u * w[1:2] + un * w[2:3])).astype(BF16))
    yb = []
    for h in range(DN_HEADS):
        hs = slice(DN_HD * h, DN_HD * h + DN_HD)
        o = o_refs[2 * nb - 2][:, hs].astype(F32) + o_refs[2 * nb - 1][:, hs].astype(F32)
        for b in range(nb - 2, -1, -1):
            o = jnp.where(batch == b, o_refs[2 * b][:, hs].astype(F32) + o_refs[2 * b + 1][:, hs].astype(F32), o)
        yb.append((_rms(o, on_ref[...]) * _silu(z_ref[:, hs].astype(F32))).astype(BF16))
    ycat = jnp.concatenate(ya + yb, axis=1)
    y = _dot(ycat, wo_ref[...])
    xn, h2, eidx, gates = _post_mixer(_token_rows(xl_ref, xc_ref, ts, bs), y, mod_ref[0], g2_ref[...], rw_ref[...],
                                      rb_ref[...])
    xn_ref[...] = xn
    h2_ref[...] = h2.astype(BF16)
    ei_ref[...] = eidx
    ga_ref[...] = gates


def _mix0(xl, xc, sc, o_pairs, z, mods, cw, on, wo, g2, rw, rb, h2buf, ts, bs, s, cl, nb):
    d = xl.shape[1]
    r = bs + xc.shape[0]
    kern = functools.partial(_mix0_kernel, ts=ts, bs=bs, s=s, cl=cl, nb=nb)

    def o_spec(b):
        def index(i):
            row0 = i * ts
            in_lat = row0 < bs
            owner = jnp.where(in_lat, row0 // s, (row0 - bs) // cl)
            own = jnp.where(in_lat, (row0 % s) // ts, s // ts + ((row0 - bs) % cl) // ts)
            nxt = jnp.where(in_lat, 0, s // ts)
            prv = jnp.where(in_lat, s // ts - 1, (s + cl) // ts - 1)
            return jnp.where(owner == b, own, jnp.where(owner < b, nxt, prv)), 0
        return pl.BlockSpec((ts, 512), index)

    mi = functools.partial(_mod_index, tm=ts, bs=bs, s=s, nb=nb)
    row = lambda w: pl.BlockSpec((ts, w), lambda i: (i, 0))
    full = lambda a: pl.BlockSpec(a.shape, lambda i: (0,) * a.ndim)
    return pl.pallas_call(
        kern,
        grid=(r // ts,),
        in_specs=_token_specs(ts, bs, d) + [row(1536)] + _halo_specs(ts, 1536, r) + [
            row(512),
            pl.BlockSpec((1, N_MOD, d), lambda i: (mi(i), 0, 0)),
            full(cw), full(on), full(wo), full(g2), full(rw), full(rb), pl.BlockSpec(memory_space=pl.ANY)]
        + [o_spec(b) for b in range(nb) for _ in range(2)],
        out_specs=[row(d), row(d), pl.BlockSpec((8, ts), lambda i: (0, i)), row(128)],
        out_shape=[jax.ShapeDtypeStruct((r, d), F32), jax.ShapeDtypeStruct(h2buf.shape, BF16),
                   jax.ShapeDtypeStruct((8, r), I32), jax.ShapeDtypeStruct((r, 128), F32)],
        input_output_aliases={13: 1},
        compiler_params=_cp(("parallel",)),
        name="mix0",
    )(xl, xc, sc, sc, sc, z, mods, cw, on, wo, g2, rw, rb, h2buf, *[a for pair in o_pairs for a in pair])


def _gmm_kernel(te_ref, tf_ref, tv_ref, ne_ref, sl_ref, x_ref, wg_hbm, wu_hbm, wd_hbm, y_ref,
                wbuf, wgb, wub, wdb, sem, *, layer):
    t = pl.program_id(0)

    def weight_copies(e, slot):
        return [pltpu.make_async_copy(w.at[layer, e], wbuf.at[slot, k], sem.at[slot, k])
                for k, w in enumerate((wg_hbm, wu_hbm, wd_hbm))]

    @pl.when(t == 0)
    def _():
        for cp in weight_copies(te_ref[0], sl_ref[0]):
            cp.start()

    @pl.when(tf_ref[t] == 1)
    def _():
        slot = sl_ref[t]
        for cp in weight_copies(te_ref[t], slot):
            cp.wait()
        wgb[...] = wbuf[slot, 0].astype(BF16)
        wub[...] = wbuf[slot, 1].astype(BF16)
        wdb[...] = wbuf[slot, 2].astype(BF16)

        @pl.when(ne_ref[t] >= 0)
        def _():
            for cp in weight_copies(ne_ref[t], 1 - slot):
                cp.start()

    @pl.when(tv_ref[t] == 1)
    def _():
        x = x_ref[...]
        a = (_silu(_dot(x, wgb[...])) * _dot(x, wub[...])).astype(BF16)
        y_ref[...] = _dot(a, wdb[...]).astype(BF16)

    @pl.when(tv_ref[t] == 0)
    def _():
        y_ref[...] = jnp.zeros_like(y_ref)


def _gmm(xs, w_gate, w_up, w_down, layer, tile_expert, tile_first, tile_valid, next_expert, tile_slot, tmm):
    p, d = xs.shape
    de = w_gate.shape[-1]
    assert d == de
    nt = p // tmm
    row = lambda t, *_: (t, 0)
    grid_spec = pltpu.PrefetchScalarGridSpec(
        num_scalar_prefetch=5,
        grid=(nt,),
        in_specs=[pl.BlockSpec((tmm, d), row)] + [pl.BlockSpec(memory_space=pl.ANY)] * 3,
        out_specs=pl.BlockSpec((tmm, d), row),
        scratch_shapes=[pltpu.VMEM((2, 3, d, de), F32),
                        pltpu.VMEM((d, de), BF16), pltpu.VMEM((d, de), BF16), pltpu.VMEM((de, d), BF16),
                        pltpu.SemaphoreType.DMA((2, 3))],
    )
    return pl.pallas_call(
        functools.partial(_gmm_kernel, layer=layer),
        grid_spec=grid_spec,
        out_shape=jax.ShapeDtypeStruct((p, d), BF16),
        compiler_params=_cp(("arbitrary",)),
        name="gmm",
    )(tile_expert, tile_first, tile_valid, next_expert, tile_slot, xs, w_gate, w_up, w_down)


def _moe(h2, t_tok, e_first, e_second, w_gate, w_up, w_down, layer, tmm):
    n = 2 * t_tok
    e_flat = jnp.concatenate([e_first, e_second])
    onehot = (e_flat[:, None] == jnp.arange(N_EXPERTS, dtype=I32)[None, :]).astype(I32)
    csum = jnp.cumsum(onehot, axis=0)
    counts = csum[-1]
    ptiles = (counts + tmm - 1) // tmm
    tile_end = jnp.cumsum(ptiles)
    dest = jnp.sum(onehot * (csum - 1 + ((tile_end - ptiles) * tmm)[None, :]), axis=1)
    nt = n // tmm + N_EXPERTS
    tid = jnp.arange(nt, dtype=I32)
    tile_valid = (tid < tile_end[-1]).astype(I32)
    te = jnp.minimum(jnp.sum((tile_end[None, :] <= tid[:, None]).astype(I32), axis=1), N_EXPERTS - 1)
    last_used = jnp.max(jnp.where(tile_valid == 1, te, 0))
    te = jnp.where(tile_valid == 1, te, last_used)
    tile_first = jnp.concatenate([jnp.ones((1,), I32), (te[1:] != te[:-1]).astype(I32)])
    tile_slot = (jnp.cumsum(tile_first) - 1) % 2
    experts = jnp.arange(N_EXPERTS, dtype=I32)
    later = (experts[None, :] > te[:, None]) & (counts[None, :] > 0)
    next_expert = jnp.min(jnp.where(later, experts[None, :], N_EXPERTS), axis=1)
    next_expert = jnp.where(next_expert < N_EXPERTS, next_expert, -1)
    order = jnp.argsort(e_flat, stable=True).astype(I32)
    seg_start = (tile_end - ptiles) * tmm
    shift = seg_start - (jnp.cumsum(counts) - counts)
    pos = tid[:, None] * tmm + jnp.arange(tmm, dtype=I32)[None, :]
    te_onehot = te[:, None] == jnp.arange(N_EXPERTS, dtype=I32)[None, :]
    per_tile = lambda v: jnp.sum(jnp.where(te_onehot, v[None, :], 0), axis=1, keepdims=True)
    used = (pos - per_tile(seg_start) < per_tile(counts)) & (tile_valid[:, None] == 1)
    src = (jnp.where(used, jnp.take(order, jnp.clip(pos - per_tile(shift), 0, n - 1)), pos) % t_tok).reshape(-1)
    assert h2.shape[0] >= nt * tmm
    xs = jnp.take(h2, src, axis=0, mode="clip")
    ys = _gmm(xs, w_gate, w_up, w_down, layer, te, tile_first, tile_valid, next_expert, tile_slot, tmm)
    return jnp.take(ys, dest, axis=0, mode="clip")


def _moe_combine(x, y1, y2, gates, m5):
    g = gates
    f = g[:, 0:1] * y1.astype(F32) + g[:, 1:2] * y2.astype(F32)
    return x + m5 * f


def _rope(x, cos, sin):
    n = x.shape[1]
    lane = lax.broadcasted_iota(I32, x.shape, 1)
    sw = jnp.where(lane % 32 < 16, pltpu.roll(x, n - 16, 1), pltpu.roll(x, 16, 1))
    reps = n // 128
    if reps > 1:
        cos = jnp.concatenate([cos] * reps, axis=1)
        sin = jnp.concatenate([sin] * reps, axis=1)
    return x * cos + sw * sin


def _inproj1_kernel(x_ref, y1_ref, y2_ref, ga_ref, m0_ref, m1_ref, g_ref, w_ref, cos_ref, sin_ref,
                    x1_ref, cq_ref, ckt_ref, cv_ref, dq_ref, dk_ref, dv_ref, *, tm, bs):
    m = m1_ref[0]
    in_lat = pl.program_id(0) * tm < bs
    scale = HEAD_DIM ** -0.5 * LOG2E
    for rows in (slice(0, tm // 2), slice(tm // 2, tm)):
        x1 = _moe_combine(x_ref[rows], y1_ref[rows], y2_ref[rows], ga_ref[rows], m0_ref[0][5:6])
        x1_ref[rows] = x1
        h = (_rms(x1, g_ref[...]) * (1.0 + m[1:2]) + m[0:1]).astype(BF16)
        cos, sin = cos_ref[rows], sin_ref[rows]
        cq = _dot(h, w_ref[:, 0:512])
        cq_ref[rows] = (jnp.where(in_lat, _rope(cq, cos, sin), cq) * scale).astype(BF16)
        ck = _dot(h, w_ref[:, 512:640])
        ckt_ref[:, rows] = jnp.where(in_lat, _rope(ck, cos, sin), ck).T.astype(BF16)
        cv_ref[rows] = _dot(h, w_ref[:, 640:768]).astype(BF16)
        dq_ref[rows] = (_dot(h, w_ref[:, 768:1280]) * scale).astype(BF16)
        dk_ref[rows] = _dot(h, w_ref[:, 1280:1792]).astype(BF16)
        dv_ref[rows] = _dot(h, w_ref[:, 1792:2304]).astype(BF16)


def _inproj1(x, y, gates, mods0, mods1, g, w, cos, sin, tm, bs, s, nb):
    r, d = x.shape
    second = pl.BlockSpec((tm, d), lambda i: (i + r // tm, 0))
    kern = functools.partial(_inproj1_kernel, tm=tm, bs=bs)
    mi = functools.partial(_mod_index, tm=tm, bs=bs, s=s, nb=nb)
    row = lambda wd: pl.BlockSpec((tm, wd), lambda i: (i, 0))
    modspec = pl.BlockSpec((1, N_MOD, d), lambda i: (mi(i), 0, 0))
    tab = pl.BlockSpec((tm, 128), lambda i: (jnp.where(i * tm < bs, (i * tm % s) // tm, 0), 0))
    shp = lambda wd, dt: jax.ShapeDtypeStruct((r, wd), dt)
    return pl.pallas_call(
        kern,
        grid=(r // tm,),
        in_specs=[row(d), row(d), second, row(128), modspec, modspec,
                  pl.BlockSpec((1, d), lambda i: (0, 0)), pl.BlockSpec(w.shape, lambda i: (0, 0)), tab, tab],
        out_specs=[row(d), row(512), pl.BlockSpec((128, tm), lambda i: (0, i)), row(128), row(512), row(512),
                   row(512)],
        out_shape=[shp(d, F32), shp(512, BF16), jax.ShapeDtypeStruct((128, r), BF16), shp(128, BF16),
                   shp(512, BF16), shp(512, BF16), shp(512, BF16)],
        compiler_params=_cp(("parallel",)),
        name="inproj1",
    )(x, y, y, gates, mods0, mods1, g, w, cos, sin)


def _swa_kernel(q_ref, ktp_ref, kto_ref, ktn_ref, ktx_ref, vp_ref, vo_ref, vn_ref, vx_ref, sink_ref, o_ref,
                *, nblk, cl, nq):
    grp = pl.program_id(1)
    wb = SWA_BLOCK
    nloc = 3 * wb
    ktx, vx = ktx_ref[...], vx_ref[...]
    kt_all = jnp.concatenate([ktp_ref[...], kto_ref[...], ktn_ref[...]], axis=1)
    v_all = jnp.concatenate([vp_ref[...], vo_ref[...], vn_ref[...]], axis=0)
    kts = [jnp.concatenate([kt_all[:, wb * u:wb * u + nloc], ktx], axis=1) for u in range(nq)]
    vvs = [jnp.concatenate([v_all[wb * u:wb * u + nloc], vx], axis=0) for u in range(nq)]
    a_i = lax.broadcasted_iota(I32, (2 * wb, nloc), 0) % wb
    c_i = lax.broadcasted_iota(I32, (2 * wb, nloc), 1)
    band = (c_i >= a_i) & (c_i <= a_i + 2 * SWA_WINDOW)
    oks = []
    for u in range(nq):
        i = nq * grp + u
        lo = jnp.where(i > 0, 0, wb)
        hi = jnp.where(i < nblk - 1, 3 * wb, 2 * wb)
        oks.append(band & (c_i >= lo) & (c_i < hi))
    half = lax.broadcasted_iota(I32, (1, 128), 1) // HEAD_DIM
    zero = jnp.zeros((), BF16)
    sink = sink_ref[...]
    items = [(u, g) for u in range(nq) for g in range(4)]

    def scores(u, g):
        q2 = q_ref[wb * u:wb * u + wb, 128 * g:128 * g + 128]
        qst = jnp.concatenate([jnp.where(half == 0, q2, zero), jnp.where(half == 1, q2, zero)], axis=0)
        s_all = _dot(qst, kts[u])
        return jnp.concatenate([jnp.where(oks[u], s_all[:, 0:nloc], NEG), s_all[:, nloc:]], axis=1)

    def softmax(g, sc):
        sk = jnp.concatenate([jnp.broadcast_to(sink[0:1, 2 * g + a:2 * g + a + 1], (wb, 1)) for a in range(2)],
                             axis=0)
        m = jnp.maximum(jnp.max(sc, axis=-1, keepdims=True), sk)
        p = jnp.exp2(sc - m)
        return p.astype(BF16), jnp.sum(p, axis=-1, keepdims=True) + jnp.exp2(sk - m)

    def output(u, g, p, den):
        ost = _dot(p, vvs[u]) / den
        o_ref[wb * u:wb * u + wb, 128 * g:128 * g + 128] = jnp.where(half == 0, ost[0:wb], ost[wb:2 * wb]).astype(BF16)

    n = len(items)
    sc, pr = {}, {}
    for step in range(n + 2):
        if step < n:
            sc[step] = scores(*items[step])
        if 1 <= step < n + 1:
            pr[step - 1] = softmax(items[step - 1][1], sc.pop(step - 1))
        if step >= 2:
            output(*items[step - 2], *pr.pop(step - 2))


def _swa(cq, ckt, cv, sink_row, nb, s, cl, bs):
    wb = SWA_BLOCK
    nblk = s // wb
    nq = 8
    assert nblk % nq == 0
    npair = nblk // nq
    kern = functools.partial(_swa_kernel, nblk=nblk, cl=cl, nq=nq)
    prev = lambda b, j: b * nblk + jnp.maximum(nq * j - 1, 0)
    nxt = lambda b, j: b * nblk + jnp.minimum(nq * j + nq, nblk - 1)
    own = lambda b, j: b * npair + j
    return pl.pallas_call(
        kern,
        grid=(nb, npair),
        in_specs=[pl.BlockSpec((nq * wb, 512), lambda b, j: (own(b, j), 0)),
                  pl.BlockSpec((128, wb), lambda b, j: (0, prev(b, j))),
                  pl.BlockSpec((128, nq * wb), lambda b, j: (0, own(b, j))),
                  pl.BlockSpec((128, wb), lambda b, j: (0, nxt(b, j))),
                  pl.BlockSpec((128, cl), lambda b, j: (0, bs // cl + b)),
                  pl.BlockSpec((wb, 128), lambda b, j: (prev(b, j), 0)),
                  pl.BlockSpec((nq * wb, 128), lambda b, j: (own(b, j), 0)),
                  pl.BlockSpec((wb, 128), lambda b, j: (nxt(b, j), 0)),
                  pl.BlockSpec((cl, 128), lambda b, j: (bs // cl + b, 0)),
                  pl.BlockSpec((1, 128), lambda b, j: (0, 0))],
        out_specs=pl.BlockSpec((nq * wb, 512), lambda b, j: (own(b, j), 0)),
        out_shape=jax.ShapeDtypeStruct((bs, 512), BF16),
        compiler_params=_cp(("parallel", "parallel")),
        name="swa",
    )(cq, ckt, ckt, ckt, ckt, cv, cv, cv, cv, sink_row)


def _na_kernel(q_ref, k_ref, v_ref, kx_ref, vx_ref, cols_ref, o_ref, bias_ref, *, rows, unroll):
    for h in range(2):
        for off in range(NA_KH):
            for i in range(NA_KH):
                bias_ref[h, off, :, GRID_W * i:GRID_W * (i + 1)] = cols_ref[h, off + i]
    half = lax.broadcasted_iota(I32, (1, 128), 1) // HEAD_DIM
    zero = jnp.zeros((), BF16)
    vx = vx_ref[...]
    kxt = kx_ref[...].astype(F32).T.astype(BF16)
    span = NA_KH * GRID_W

    def scores(r):
        rs = jnp.clip(r - NA_KH // 2, 0, rows - NA_KH)
        off = rs - r + NA_KH - 1
        q0 = pl.multiple_of(r * GRID_W, GRID_W)
        k0 = pl.multiple_of(rs * GRID_W, GRID_W)
        q2 = q_ref[pl.ds(q0, GRID_W), :]
        qst = jnp.concatenate([jnp.where(half == 0, q2, zero), jnp.where(half == 1, q2, zero)], axis=0)
        s_loc = (_dot_nt(qst, k_ref[pl.ds(k0, span), :])
                 + jnp.concatenate([bias_ref[0, off], bias_ref[1, off]], axis=0))
        return q0, k0, s_loc, _dot(qst, kxt)

    def softmax(s_loc, s_ctx):
        m = jnp.maximum(jnp.max(s_loc, axis=-1, keepdims=True), jnp.max(s_ctx, axis=-1, keepdims=True))
        p_loc = jnp.exp2(s_loc - m)
        p_ctx = jnp.exp2(s_ctx - m)
        den = jnp.sum(p_loc, axis=-1, keepdims=True) + jnp.sum(p_ctx, axis=-1, keepdims=True)
        return p_loc.astype(BF16), p_ctx.astype(BF16), den

    def output(q0, k0, p_loc, p_ctx, den):
        ost = (_dot(p_loc, v_ref[pl.ds(k0, span), :]) + _dot(p_ctx, vx)) / den
        o = jnp.where(half == 0, ost[0:GRID_W], ost[GRID_W:2 * GRID_W])
        o_ref[pl.ds(q0, GRID_W), :] = o.astype(BF16)

    def body(i, carry):
        sc, pr = {}, {}
        for step in range(unroll + 2):
            if step < unroll:
                sc[step] = scores(i * unroll + step)
            if 1 <= step < unroll + 1:
                q0, k0, s_loc, s_ctx = sc.pop(step - 1)
                pr[step - 1] = (q0, k0) + softmax(s_loc, s_ctx)
            if step >= 2:
                output(*pr.pop(step - 2))
        return carry

    lax.fori_loop(0, rows // unroll, body, 0)


def _na(dq, dk, dv, bias_cols, nb, s, cl, bs):
    rows = s // GRID_W
    unroll = 8
    assert rows % unroll == 0
    kern = functools.partial(_na_kernel, rows=rows, unroll=unroll)
    seq = pl.BlockSpec((s, 128), lambda b, p: (b, p))
    ctx = pl.BlockSpec((cl, 128), lambda b, p: (bs // cl + b, p))
    return pl.pallas_call(
        kern,
        grid=(nb, NA_HEADS // 2),
        in_specs=[seq, seq, seq, ctx, ctx,
                  pl.BlockSpec((2, 2 * NA_KH - 1, GRID_W, GRID_W), lambda b, p: (p, 0, 0, 0))],
        out_specs=seq,
        out_shape=jax.ShapeDtypeStruct((bs, 512), BF16),
        scratch_shapes=[pltpu.VMEM((2, NA_KH, GRID_W, NA_KH * GRID_W), F32)],
        compiler_params=_cp(("parallel", "parallel")),
        name="na",
    )(dq, dk, dv, dk, dv, bias_cols)


def _na_bias_cols(rpb):
    c = np.arange(GRID_W)
    qs = np.clip(c - NA_KW // 2, 0, GRID_W - NA_KW)
    kc = np.arange(GRID_W)
    ok = (kc[None, :] >= qs[:, None]) & (kc[None, :] < qs[:, None] + NA_KW)
    dc = np.clip(kc[None, :] - c[:, None] + NA_KW - 1, 0, 2 * NA_KW - 2)
    sel = (np.arange(2 * NA_KW - 1)[:, None, None] == dc[None]).astype(np.float32)
    cols = jnp.einsum("hab,bck->hack", rpb.astype(F32), sel, precision=HI)
    return jnp.where(ok[None, None], cols * LOG2E, NEG)


def _mix1_kernel(x_ref, oc_ref, od_ref, mod_ref, wo_ref, g2_ref, rw_ref, rb_ref, h2buf_ref,
                 xn_ref, h2_ref, ei_ref, ga_ref):
    del h2buf_ref
    y = _dot(oc_ref[...], wo_ref[0:512, :]) + _dot(od_ref[...], wo_ref[512:1024, :])
    xn, h2, eidx, gates = _post_mixer(x_ref[...], y, mod_ref[0], g2_ref[...], rw_ref[...], rb_ref[...])
    xn_ref[...] = xn
    h2_ref[...] = h2.astype(BF16)
    ei_ref[...] = eidx
    ga_ref[...] = gates


def _mix1(x, oc, od, mods, wo, g2, rw, rb, h2buf, tm, bs, s):
    d = x.shape[1]
    row = lambda w: pl.BlockSpec((tm, w), lambda i: (i, 0))
    full = lambda a: pl.BlockSpec(a.shape, lambda i: (0,) * a.ndim)
    return pl.pallas_call(
        _mix1_kernel,
        grid=(bs // tm,),
        in_specs=[row(d), row(512), row(512), pl.BlockSpec((1, N_MOD, d), lambda i: (i * tm // s, 0, 0)),
                  full(wo), full(g2), full(rw), full(rb), pl.BlockSpec(memory_space=pl.ANY)],
        out_specs=[row(d), row(d), pl.BlockSpec((8, tm), lambda i: (0, i)), row(128)],
        out_shape=[jax.ShapeDtypeStruct((bs, d), F32), jax.ShapeDtypeStruct(h2buf.shape, BF16),
                   jax.ShapeDtypeStruct((8, bs), I32), jax.ShapeDtypeStruct((bs, 128), F32)],
        input_output_aliases={8: 1},
        compiler_params=_cp(("parallel",)),
        name="mix1",
    )(x, oc, od, mods, wo, g2, rw, rb, h2buf)


def _final_kernel(x_ref, y1_ref, y2_ref, ga_ref, mod_ref, g_ref, o_ref):
    x = _moe_combine(x_ref[...], y1_ref[...], y2_ref[...], ga_ref[...], mod_ref[0][5:6])
    o_ref[...] = _rms(x, g_ref[...])


def _final(x, y, gates, mods, g, tm, s):
    r, d = x.shape
    second = pl.BlockSpec((tm, d), lambda i: (i + r // tm, 0))
    row = lambda w: pl.BlockSpec((tm, w), lambda i: (i, 0))
    return pl.pallas_call(
        _final_kernel,
        grid=(r // tm,),
        in_specs=[row(d), row(d), second, row(128), pl.BlockSpec((1, N_MOD, d), lambda i: (i * tm // s, 0, 0)),
                  pl.BlockSpec((1, d), lambda i: (0, 0))],
        out_specs=row(d),
        out_shape=jax.ShapeDtypeStruct((r, d), F32),
        compiler_params=_cp(("parallel",)),
        name="final",
    )(x, y, y, gates, mods, g)


def _rope_tables(s):
    nf = HEAD_DIM // 4
    t = np.arange(s)
    inv = ROPE_THETA ** (-np.arange(nf, dtype=np.float64) / nf)
    ar = (t // GRID_W)[:, None] * inv
    ac = (t % GRID_W)[:, None] * inv
    cos = np.concatenate([np.cos(ar), np.cos(ar), np.cos(ac), np.cos(ac)], axis=1)
    sin = np.concatenate([-np.sin(ar), np.sin(ar), -np.sin(ac), np.sin(ac)], axis=1)
    return (jnp.asarray(np.concatenate([cos, cos], axis=1), F32),
            jnp.asarray(np.concatenate([sin, sin], axis=1), F32))


def kernel(x, c, ctx, c_ctx, ada_w, ada_b, norm1_g, norm2_g, ev_w_in, ev_w_out, sc_conv_w, dn_conv_w, dn_a_log, dn_dt_bias, dn_onorm_g, od_w_in, od_w_out, swa_sink, na_rpb, router_w, router_b, moe_w_gate, moe_w_up, moe_w_down, final_g):
    nb, s, d = x.shape
    cl = ctx.shape[1]
    bs = nb * s
    tm = 512
    ts = 256
    tmm = 512
    assert d == 1024 and s % tm == 0 and (nb * cl) % tm == 0 and cl % ts == 0 and s % ts == 0
    assert s // GRID_W >= NA_KH and bs % cl == 0 and nb + 1 <= 8

    xl, xc = x.reshape(bs, d), ctx.reshape(nb * cl, d)
    cc = jnp.zeros((8, d), F32).at[:nb].set(c).at[nb].set(c_ctx)
    mods = _ada(cc, ada_w, ada_b).reshape(ada_w.shape[0], 8, N_MOD, d)
    rw32 = jnp.pad(router_w, ((0, 0), (0, 128 - N_EXPERTS)))
    rw_hi = rw32.astype(BF16)
    rw = jnp.concatenate([rw_hi, (rw32 - rw_hi.astype(F32)).astype(BF16)], axis=1)
    rb = router_b.reshape(N_EXPERTS, 1)
    row = lambda v: v.reshape(1, -1)

    w_in0 = jnp.pad(ev_w_in[0], ((0, 0), (0, 3712 - ev_w_in.shape[-1]))).astype(BF16)
    sc, qkv, z, bg = _inproj0(xl, xc, mods[0], row(norm1_g[0]), w_in0, tm, bs, s, nb)
    pad16 = lambda v: jnp.pad(v.reshape(-1), (8, 128 - 16)).reshape(1, 128)
    uf, ub, wf, wb, qf, qb, kf, kb, af, ab, gc = _dnchunk(qkv, bg, dn_conv_w[0], pad16(dn_a_log[0]),
                                                          pad16(dn_dt_bias[0]), ts, bs, s, cl)
    o_pairs = _dnscan(uf, ub, wf, wb, qf, qb, kf, kb, af, ab, gc, nb, s, cl, bs)
    r_all = bs + nb * cl
    moe_rows = lambda t: (2 * t // tmm + N_EXPERTS) * tmm
    x0, h2, ei, ga = _mix0(xl, xc, sc, o_pairs, z, mods[0], sc_conv_w[0], row(dn_onorm_g[0]),
                           ev_w_out[0].astype(BF16), row(norm2_g[0]), rw, rb,
                           jnp.zeros((moe_rows(r_all), d), BF16), ts, bs, s, cl, nb)
    y = _moe(h2, r_all, ei[0], ei[1], moe_w_gate, moe_w_up, moe_w_down, 0, tmm)

    perm = np.concatenate([np.arange(HEAD_DIM) + HEAD_DIM * (g + 4 * a) for g in range(4) for a in range(2)])
    w1 = od_w_in[0]
    w_in1 = jnp.concatenate([w1[:, 0:512][:, perm], w1[:, 512:]], axis=1).astype(BF16)
    wo1 = od_w_out[0]
    w_out1 = jnp.concatenate([wo1[0:512][perm], wo1[512:]], axis=0).astype(BF16)
    sink_row = jnp.pad(swa_sink[0][np.array([g + 4 * a for g in range(4) for a in range(2)])] * LOG2E,
                       (0, 128 - SWA_HEADS)).reshape(1, 128)
    cos, sin = _rope_tables(s)
    x1, cq, ckt, cv, dq, dk, dv = _inproj1(x0, y, ga, mods[0], mods[1], row(norm1_g[1]), w_in1, cos, sin,
                                           tm, bs, s, nb)
    oc = _swa(cq, ckt, cv, sink_row, nb, s, cl, bs)
    od = _na(dq, dk, dv, _na_bias_cols(na_rpb[0]), nb, s, cl, bs)
    assert moe_rows(bs) <= h2.shape[0]
    x2, h2, ei, ga = _mix1(x1, oc, od, mods[1], w_out1, row(norm2_g[1]), rw, rb, h2, tm, bs, s)
    y = _moe(h2, bs, ei[0], ei[1], moe_w_gate, moe_w_up, moe_w_down, 1, tmm)
    out = _final(x2, y, ga, mods[1], row(final_g), tm, s)
    return out.reshape(nb, s, d)
```

```python
import functools
import math

import numpy as np
import jax
import jax.numpy as jnp
from jax import lax
from jax.experimental import pallas as pl
from jax.experimental.pallas import tpu as pltpu

F32 = jnp.float32
BF16 = jnp.bfloat16
I32 = jnp.int32
HI = lax.Precision.HIGHEST

EPS = 1e-6
N_MOD = 6
GRID_W = 64
HEAD_DIM = 64
DN_HEADS = 4
DN_HD = 128
DN_CHUNK = 64
SWA_HEADS = 8
SWA_KV = 2
SWA_BLOCK = 128
SWA_WINDOW = 128
NA_HEADS = 8
NA_KH = 8
NA_KW = 16
ROPE_THETA = 10000.0
N_EXPERTS = 16
N_GROUPS = 4
NEG = -1e30
LOG2E = 1.4426950408889634
VMEM_LIMIT = 56 * 1024 * 1024


def _cp(sem, vmem=VMEM_LIMIT):
    return pltpu.CompilerParams(dimension_semantics=sem, vmem_limit_bytes=vmem)


def _dot(a, b, precision=None):
    return jnp.dot(a, b, preferred_element_type=F32, precision=precision)


def _dot_nt(a, b, precision=None):
    return lax.dot_general(a, b, (((1,), (1,)), ((), ())), preferred_element_type=F32, precision=precision)


def _dot_tn(a, b, precision=None):
    return lax.dot_general(a, b, (((0,), (0,)), ((), ())), preferred_element_type=F32, precision=precision)


def _silu(x):
    return x * jax.nn.sigmoid(x)


def _rms(x, g):
    return x * lax.rsqrt(jnp.mean(x * x, axis=-1, keepdims=True) + EPS) * g


def _ada_kernel(cc_ref, w_ref, b_ref, o_ref):
    a = _silu(cc_ref[...])
    o_ref[0] = _dot(a, w_ref[0], HI) + b_ref[0]


def _ada(cc, ada_w, ada_b):
    depth, d, n = ada_w.shape
    tn = 1536
    return pl.pallas_call(
        _ada_kernel,
        grid=(depth, n // tn),
        in_specs=[pl.BlockSpec((8, d), lambda l, j: (0, 0)),
                  pl.BlockSpec((1, d, tn), lambda l, j: (l, 0, j)),
                  pl.BlockSpec((1, 1, tn), lambda l, j: (l, 0, j))],
        out_specs=pl.BlockSpec((1, 8, tn), lambda l, j: (l, 0, j)),
        out_shape=jax.ShapeDtypeStruct((depth, 8, n), F32),
        compiler_params=_cp(("parallel", "parallel")),
        name="ada",
    )(cc, ada_w, ada_b.reshape(depth, 1, n))


def _mod_index(i, tm, bs, s, nb):
    row0 = i * tm
    return jnp.where(row0 < bs, row0 // s, nb)


def _token_specs(tm, bs, d):
    nlat = bs // tm
    return [pl.BlockSpec((tm, d), lambda i: (jnp.minimum(i, nlat - 1), 0)),
            pl.BlockSpec((tm, d), lambda i: (jnp.maximum(i - nlat, 0), 0))]


def _token_rows(xl_ref, xc_ref, tm, bs):
    return jnp.where(pl.program_id(0) * tm < bs, xl_ref[...], xc_ref[...])


def _inproj0_kernel(xl_ref, xc_ref, mod_ref, g_ref, w_ref, sc_ref, qkv_ref, z_ref, bg_ref, *, tm, bs):
    m = mod_ref[0]
    in_lat = pl.program_id(0) * tm < bs
    for rows in (slice(0, tm // 2), slice(tm // 2, tm)):
        x = jnp.where(in_lat, xl_ref[rows], xc_ref[rows])
        h = (_rms(x, g_ref[...]) * (1.0 + m[1:2]) + m[0:1]).astype(BF16)
        sc_ref[rows] = _dot(h, w_ref[:, 0:1536]).astype(BF16)
        qkv_ref[rows] = _dot(h, w_ref[:, 1536:3072]).astype(BF16)
        z_ref[rows] = _dot(h, w_ref[:, 3072:3584]).astype(BF16)
        bg_ref[rows] = _dot(h, w_ref[:, 3584:3712])


def _inproj0(xl, xc, mods, g, w, tm, bs, s, nb):
    d = xl.shape[1]
    r = bs + xc.shape[0]
    mi = functools.partial(_mod_index, tm=tm, bs=bs, s=s, nb=nb)
    return pl.pallas_call(
        functools.partial(_inproj0_kernel, tm=tm, bs=bs),
        grid=(r // tm,),
        in_specs=_token_specs(tm, bs, d) + [
            pl.BlockSpec((1, N_MOD, d), lambda i: (mi(i), 0, 0)),
            pl.BlockSpec((1, d), lambda i: (0, 0)),
            pl.BlockSpec(w.shape, lambda i: (0, 0))],
        out_specs=[pl.BlockSpec((tm, 1536), lambda i: (i, 0)),
                   pl.BlockSpec((tm, 1536), lambda i: (i, 0)),
                   pl.BlockSpec((tm, 512), lambda i: (i, 0)),
                   pl.BlockSpec((tm, 128), lambda i: (i, 0))],
        out_shape=[jax.ShapeDtypeStruct((r, 1536), BF16), jax.ShapeDtypeStruct((r, 1536), BF16),
                   jax.ShapeDtypeStruct((r, 512), BF16), jax.ShapeDtypeStruct((r, 128), F32)],
        compiler_params=_cp(("parallel",)),
        name="inproj0",
    )(xl, xc, mods, g, w)


def _seq_edges(i, ts, bs, s, cl):
    row0 = i * ts
    in_lat = row0 < bs
    r_in = jnp.where(in_lat, row0 % s, (row0 - bs) % cl)
    seqlen = jnp.where(in_lat, s, cl)
    return r_in == 0, r_in + ts == seqlen


def _shifted(x, prev_row, next_row):
    n = x.shape[0]
    rows = lax.broadcasted_iota(I32, x.shape, 0)
    xp = jnp.where(rows == 0, prev_row, pltpu.roll(x, 1, 0))
    xn = jnp.where(rows == n - 1, next_row, pltpu.roll(x, n - 1, 0))
    return xp, xn


HALO = 16


def _halo_specs(ts, width, r):
    nblk = r // HALO
    k = ts // HALO
    return [pl.BlockSpec((HALO, width), lambda i: (jnp.maximum(i * k - 1, 0), 0)),
            pl.BlockSpec((HALO, width), lambda i: (jnp.minimum((i + 1) * k, nblk - 1), 0))]


def _dnprep_tile(x_ref, prev_ref, next_ref, bg_ref, cw_ref, alog_ref, dt_ref,
                 q_ref, k_ref, v_ref, bga_ref, *, ts, bs, s, cl):
    first, last = _seq_edges(pl.program_id(0), ts, bs, s, cl)
    for c in range(12):
        sl = slice(128 * c, 128 * c + 128)
        x = x_ref[:, sl].astype(F32)
        pr = jnp.where(first, 0.0, prev_ref[HALO - 1:HALO, sl].astype(F32))
        nx = jnp.where(last, 0.0, next_ref[0:1, sl].astype(F32))
        xp, xn = _shifted(x, pr, nx)
        w = cw_ref[:, sl]
        y = _silu(xp * w[0:1] + x * w[1:2] + xn * w[2:3])
        hs = slice(128 * (c % 4), 128 * (c % 4) + 128)
        if c < 8:
            y = y * lax.rsqrt(jnp.sum(y * y, axis=-1, keepdims=True) + EPS)
        if c < 4:
            q_ref[:, hs] = y * DN_HD ** -0.5
        elif c < 8:
            k_ref[:, hs] = y
        else:
            v_ref[:, hs] = y
    b = bg_ref[...]
    cols = lax.broadcasted_iota(I32, b.shape, 1)
    beta = jax.nn.sigmoid(b)
    t = b + dt_ref[...]
    softplus = jnp.maximum(t, 0.0) + jnp.log1p(jnp.exp(-jnp.abs(t)))
    g = -jnp.exp(alog_ref[...]) * softplus
    bga_ref[...] = jnp.where(cols < 8, beta, jnp.where(cols < 16, g, 0.0))


def _dnchunk_kernel(x_ref, prev_ref, next_ref, bgraw_ref, cw_ref, alog_ref, dt_ref,
                    uf_ref, ub_ref, wf_ref, wb_ref, qf_ref, qb_ref, kf_ref, kb_ref, af_ref, ab_ref, gc_ref,
                    q_ref, k_ref, v_ref, bg_ref, *, nchunks, bs, s, cl):
    _dnprep_tile(x_ref, prev_ref, next_ref, bgraw_ref, cw_ref, alog_ref, dt_ref, q_ref, k_ref, v_ref, bg_ref,
                 ts=nchunks * DN_CHUNK, bs=bs, s=s, cl=cl)
    outs = ((uf_ref, wf_ref, qf_ref, kf_ref, af_ref), (ub_ref, wb_ref, qb_ref, kb_ref, ab_ref))
    c, nh = DN_CHUNK, DN_HEADS
    head_of_col = lax.broadcasted_iota(I32, (1, nh * c), 1) // c
    zero = jnp.zeros((), BF16)

    def block_diag(x):
        return jnp.concatenate([jnp.where(head_of_col == h, x, zero) for h in range(nh)], axis=0)

    chains = []
    for cc in range(nchunks):
        chains += _dnchunk_setup(slice(cc * c, (cc + 1) * c), q_ref, k_ref, v_ref, bg_ref, gc_ref)
    for ch in chains:
        ch["tm"] = ch["nmat"]
        nb16 = ch["nmat"].astype(BF16)
        ch["npow"] = _dot(nb16, block_diag(nb16))
    for _ in range(4):
        for ch in chains:
            nb16 = ch["npow"].astype(BF16)
            ch["both"] = _dot(jnp.concatenate([nb16, ch["tm"].astype(BF16)], axis=0), block_diag(nb16))
        for ch in chains:
            ch["tm"] = ch["tm"] + ch["npow"] + ch["both"][c:2 * c]
            ch["npow"] = ch["both"][0:c]
    for ch in chains:
        ch["both"] = _dot(ch["tm"].astype(BF16), block_diag(ch["npow"].astype(BF16)))
    for ch in chains:
        ch["tm"] = ch["tm"] + ch["npow"] + ch["both"]
    for ch in chains:
        ch["uw"] = ch["rhs"] + _dot(block_diag(ch["tm"].astype(BF16)), ch["rhs"].astype(BF16))
    for ch in chains:
        u_ref, w_ref, qd_ref, kd_ref, at_ref = outs[ch["d"]]
        rows, uw = ch["rows"], ch["uw"]
        for h in range(nh):
            hs = slice(DN_HD * h, DN_HD * h + DN_HD)
            rs = slice(c * h, c * h + c)
            u_ref[rows, hs] = uw[rs, 0:DN_HD]
            w_ref[rows, hs] = uw[rs, DN_HD:2 * DN_HD].astype(BF16)
            qd_ref[rows, hs] = ch["qd"][rs]
        kd_ref[slice(2 * rows.start, 2 * rows.stop)] = ch["kd"].T.astype(BF16)
        at_ref[rows] = ch["att"].astype(BF16)


def _dnchunk_setup(rows, q_ref, k_ref, v_ref, bg_ref, gc_ref):
    c, nh = DN_CHUNK, DN_HEADS
    n = c * nh
    bg = bg_ref[rows]
    i64 = lax.broadcasted_iota(I32, (c, c), 0)
    j64 = lax.broadcasted_iota(I32, (c, c), 1)
    cols = lax.broadcasted_iota(I32, bg.shape, 1)
    gcf = _dot((i64 >= j64).astype(F32), bg, HI)
    gcb = _dot((i64 <= j64).astype(F32), bg, HI)
    gc = jnp.where(cols >= 12, gcb, gcf)
    gc_ref[rows] = gc
    gct = gc.T
    ii = lax.broadcasted_iota(I32, (c, n), 0)
    jj = lax.broadcasted_iota(I32, (c, n), 1)
    head_of_col = jj // c
    jj = jj % c

    def stack(ref):
        return jnp.concatenate([ref[rows, DN_HD * h:DN_HD * h + DN_HD] for h in range(nh)], axis=0)

    def stacked_cols(arr, r0, r1, col0):
        return jnp.concatenate([jnp.broadcast_to(arr[r0:r1, col0 + h:col0 + h + 1], (c, DN_HD)) for h in range(nh)],
                               axis=0)

    def side_by_side_cols(arr, col0):
        out = arr[:, col0 + nh - 1:col0 + nh]
        for h in range(nh - 2, -1, -1):
            out = jnp.where(head_of_col == h, arr[:, col0 + h:col0 + h + 1], out)
        return out

    def diag_blocks(x):
        out = x[(nh - 1) * c:nh * c]
        for h in range(nh - 2, -1, -1):
            out = jnp.where(head_of_col == h, x[h * c:(h + 1) * c], out)
        return out

    kst, qst, vst = stack(k_ref), stack(q_ref), stack(v_ref)
    kb = kst.astype(BF16)
    kq = _dot_nt(jnp.concatenate([kb, qst.astype(BF16)], axis=0), kb)
    kkt, qkt = diag_blocks(kq[0:n]), diag_blocks(kq[n:2 * n])
    chains = []
    for d in range(2):
        incl = (ii >= jj) if d == 0 else (ii <= jj)
        strict = (ii > jj) if d == 0 else (ii < jj)
        last = c - 1 if d == 0 else 0
        grow = jnp.concatenate([gct[8 + 4 * d + h:9 + 4 * d + h, :] for h in range(nh)], axis=1)
        decay = jnp.exp(jnp.where(incl, side_by_side_cols(gc, 8 + 4 * d) - grow, NEG))
        nmat = jnp.where(strict, -(side_by_side_cols(bg, 4 * d) * kkt * decay), 0.0)
        b1 = stacked_cols(bg, 0, c, 4 * d)
        gcol = stacked_cols(gc, 0, c, 8 + 4 * d)
        glast = stacked_cols(gc, last, last + 1, 8 + 4 * d)
        e1 = jnp.exp(gcol)
        chains.append(dict(
            d=d, rows=rows, nmat=nmat,
            rhs=jnp.concatenate([b1 * vst, (b1 * e1) * kst], axis=1),
            qd=(qst * e1).astype(BF16),
            kd=kst * jnp.exp(glast - gcol),
            att=qkt * decay))
    return chains


def _dnchunk(qkv, bg, cw, alog_row, dt_row, ts, bs, s, cl):
    r = qkv.shape[0]
    nchunks = ts // DN_CHUNK
    c = ts
    row = lambda w: pl.BlockSpec((c, w), lambda i: (i, 0))
    shp = lambda w, dt: jax.ShapeDtypeStruct((r, w), dt)
    return pl.pallas_call(
        functools.partial(_dnchunk_kernel, nchunks=nchunks, bs=bs, s=s, cl=cl),
        grid=(r // c,),
        in_specs=[row(1536)] + _halo_specs(ts, 1536, r) + [
            row(128),
            pl.BlockSpec((3, 1536), lambda i: (0, 0)),
            pl.BlockSpec((1, 128), lambda i: (0, 0)),
            pl.BlockSpec((1, 128), lambda i: (0, 0))],
        scratch_shapes=[pltpu.VMEM((c, 512), F32)] * 3 + [pltpu.VMEM((c, 128), F32)],
        out_specs=[row(512)] * 6 + [pl.BlockSpec((2 * c, 256), lambda i: (i, 0))] * 2 + [row(256), row(256), row(128)],
        out_shape=([shp(512, F32)] * 2 + [shp(512, BF16)] * 4 + [jax.ShapeDtypeStruct((2 * r, 256), BF16)] * 2
                   + [shp(256, BF16)] * 2 + [shp(128, F32)]),
        compiler_params=_cp(("parallel",)),
        name="dnchunk",
    )(qkv, qkv, qkv, bg, cw, alog_row, dt_row)


def _dnscan_kernel(*refs, nsub, nb):
    nchain = 2 * nb
    ins, outs, s_ref = refs[:6 * nchain], refs[6 * nchain:7 * nchain], refs[7 * nchain]

    @pl.when(pl.program_id(0) == 0)
    def _():
        s_ref[...] = jnp.zeros_like(s_ref)

    c, nh = DN_CHUNK, DN_HEADS
    head_of_lane = lax.broadcasted_iota(I32, (1, nh * DN_HD), 1) // DN_HD
    head_of_col = lax.broadcasted_iota(I32, (1, nh * c), 1) // c
    zero = jnp.zeros((), BF16)

    def block_diag(tile, head_ids):
        return jnp.concatenate([jnp.where(head_ids == h, tile, zero) for h in range(nh)], axis=0)

    states = [s_ref[ci] for ci in range(nchain)]
    for sub in range(nsub):
        work = []
        for ci in range(nchain):
            d = ci % 2
            u_ref, w_ref, qd_ref, kd_ref, at_ref, g_ref = ins[6 * ci:6 * ci + 6]
            last = c - 1 if d == 0 else 0
            k = sub if d == 0 else nsub - 1 - sub
            rows = slice(c * k, c * k + c)
            g = g_ref[rows]
            decay = jnp.concatenate(
                [jnp.broadcast_to(jnp.exp(g[last:last + 1, 8 + 4 * d + h:9 + 4 * d + h]), (DN_HD, DN_HD))
                 for h in range(nh)], axis=0)
            ust = jnp.concatenate([u_ref[rows, DN_HD * h:DN_HD * h + DN_HD] for h in range(nh)], axis=0)
            stb = states[ci].astype(BF16)
            vnew = ust - _dot(block_diag(w_ref[rows], head_of_lane), stb)
            work.append((rows, decay, stb, vnew.astype(BF16)))
        for ci in range(nchain):
            u_ref, w_ref, qd_ref, kd_ref, at_ref, g_ref = ins[6 * ci:6 * ci + 6]
            rows, decay, stb, vnb = work[ci]
            o = (_dot(block_diag(qd_ref[rows], head_of_lane), stb)
                 + _dot(block_diag(at_ref[rows], head_of_col), vnb))
            kdt = kd_ref[slice(2 * rows.start, 2 * rows.stop)]
            states[ci] = states[ci] * decay + _dot(block_diag(kdt, head_of_col), vnb)
            for h in range(nh):
                outs[ci][rows, DN_HD * h:DN_HD * h + DN_HD] = o[c * h:c * h + c].astype(BF16)
    for ci in range(nchain):
        s_ref[ci] = states[ci]


def _dnscan(uf, ub, wf, wb, qf, qb, kf, kb, af, ab, gc, nb, s, cl, bs):
    nsub = 4
    c = DN_CHUNK * nsub
    assert cl % c == 0 and s % c == 0 and bs % c == 0
    ncc, ncl = cl // c, s // c
    ns = ncc + ncl

    def src_block(b, d):
        if d == 0:
            return lambda t: jnp.where(t < ncc, bs // c + b * ncc + t, b * ncl + t - ncc)
        return lambda t: jnp.where(t < ncc, bs // c + b * ncc + (ncc - 1 - t), b * ncl + (ncl - 1 - (t - ncc)))

    def dst_block(d):
        if d == 0:
            return lambda t: jnp.where(t < ncc, ncl + t, t - ncc)
        return lambda t: jnp.where(t < ncc, ncl + (ncc - 1 - t), ncl - 1 - (t - ncc))

    in_specs, args, out_specs = [], [], []
    for b in range(nb):
        for d, group in enumerate(((uf, wf, qf, kf, af, gc), (ub, wb, qb, kb, ab, gc))):
            idx = src_block(b, d)
            blk = lambda w, idx=idx: pl.BlockSpec((c, w), lambda t: (idx(t), 0))
            in_specs += [blk(512), blk(512), blk(512), pl.BlockSpec((2 * c, 256), lambda t, idx=idx: (idx(t), 0)),
                         blk(256), blk(128)]
            args += list(group)
            out_specs.append(pl.BlockSpec((c, 512), lambda t, f=dst_block(d): (f(t), 0)))
    outs = pl.pallas_call(
        functools.partial(_dnscan_kernel, nsub=nsub, nb=nb),
        grid=(ns,),
        in_specs=in_specs,
        out_specs=out_specs,
        out_shape=[jax.ShapeDtypeStruct((s + cl, 512), BF16)] * (2 * nb),
        scratch_shapes=[pltpu.VMEM((2 * nb, DN_HEADS * DN_HD, DN_HD), F32)],
        compiler_params=_cp(("arbitrary",)),
        name="dnscan",
    )(*args)
    return [(outs[2 * b], outs[2 * b + 1]) for b in range(nb)]


def _route(logits, bias_col):
    epg = N_EXPERTS // N_GROUPS
    tm = logits.shape[0]
    scores = jax.nn.sigmoid(logits.T[0:N_EXPERTS])
    gsel = scores + bias_col
    row = lambda a, k: a[k:k + 1]
    best = gidx = None
    for g in range(N_GROUPS):
        a = [row(gsel, epg * g + k) for k in range(epg)]
        m01, n01 = jnp.maximum(a[0], a[1]), jnp.minimum(a[0], a[1])
        m23, n23 = jnp.maximum(a[2], a[3]), jnp.minimum(a[2], a[3])
        gs = jnp.maximum(m01, m23) + jnp.maximum(jnp.minimum(m01, m23), jnp.maximum(n01, n23))
        if g == 0:
            best, gidx = gs, jnp.zeros_like(gs)
        else:
            better = gs > best
            best = jnp.where(better, gs, best)
            gidx = jnp.where(better, float(g), gidx)
    sel = [None] * epg
    raw = [None] * epg
    for g in range(N_GROUPS):
        for k in range(epg):
            v, u = row(gsel, epg * g + k), row(scores, epg * g + k)
            sel[k] = v if g == 0 else jnp.where(gidx == g, v, sel[k])
            raw[k] = u if g == 0 else jnp.where(gidx == g, u, raw[k])
    v1, e1, w1 = sel[0], jnp.zeros_like(gidx), raw[0]
    for k in range(1, epg):
        better = sel[k] > v1
        v1 = jnp.where(better, sel[k], v1)
        e1 = jnp.where(better, float(k), e1)
        w1 = jnp.where(better, raw[k], w1)
    v2 = e2 = w2 = None
    for k in range(epg):
        cand = jnp.where(e1 == k, -jnp.inf, sel[k])
        if k == 0:
            v2, e2, w2 = cand, jnp.zeros_like(gidx), raw[0]
        else:
            better = cand > v2
            v2 = jnp.where(better, cand, v2)
            e2 = jnp.where(better, float(k), e2)
            w2 = jnp.where(better, raw[k], w2)
    tot = w1 + w2
    eidx = jnp.concatenate([gidx * epg + e1, gidx * epg + e2, jnp.zeros((6, tm), F32)], axis=0).astype(I32)
    gates_t = jnp.concatenate([w1 / tot, w2 / tot, jnp.zeros((126, tm), F32)], axis=0)
    return eidx, gates_t.T


def _post_mixer(x, y, m, g2, rw, rb):
    xn = x + m[2:3] * y
    h2 = _rms(xn, g2) * (1.0 + m[4:5]) + m[3:4]
    hi = h2.astype(BF16)
    lo = (h2 - hi.astype(F32)).astype(BF16)
    hw = _dot(hi, rw)
    logits = hw[:, 0:128] + (hw[:, 128:256] + _dot(lo, rw[:, 0:128]))
    eidx, gates = _route(logits, rb)
    return xn, hi, eidx, gates


def _mix0_kernel(xl_ref, xc_ref, sc_ref, prev_ref, next_ref, z_ref, mod_ref, cw_ref, on_ref, wo_ref,
                 g2_ref, rw_ref, rb_ref, h2buf_ref, *rest, ts, bs, s, cl, nb):
    del h2buf_ref
    o_refs = rest[:2 * nb]
    xn_ref, h2_ref, ei_ref, ga_ref = rest[2 * nb:]
    first, last = _seq_edges(pl.program_id(0), ts, bs, s, cl)
    row0 = pl.program_id(0) * ts
    batch = jnp.where(row0 < bs, row0 // s, (row0 - bs) // cl)
    ya = []
    for c in range(4):
        sl = slice(128 * c, 128 * c + 128)
        sg = slice(512 + 128 * c, 512 + 128 * c + 128)
        sx = slice(1024 + 128 * c, 1024 + 128 * c + 128)
        f32 = lambda ref, rows, cols: ref[rows, cols].astype(F32)
        u = f32(sc_ref, slice(None), sg) * f32(sc_ref, slice(None), sx)
        pr = jnp.where(first, 0.0, f32(prev_ref, slice(HALO - 1, HALO), sg) * f32(prev_ref, slice(HALO - 1, HALO), sx))
        nx = jnp.where(last, 0.0, f32(next_ref, slice(0, 1), sg) * f32(next_ref, slice(0, 1), sx))
        up, un = _shifted(u, pr, nx)
        w = cw_ref[:, sl]
        ya.append((f32(sc_ref, slice(None), sl) * (up * w[0:1] + u * w[1:2] + un * w[2:3])).astype(BF16))
    yb = []
    for h in range(DN_HEADS):
        hs = slice(DN_HD * h, DN_HD * h + DN_HD)
        o = o_refs[2 * nb - 2][:, hs].astype(F32) + o_refs[2 * nb - 1][:, hs].astype(F32)
        for b in range(nb - 2, -1, -1):
            o = jnp.where(batch == b, o_refs[2 * b][:, hs].astype(F32) + o_refs[2 * b + 1][:, hs].astype(F32), o)
        yb.append((_rms(o, on_ref[...]) * _silu(z_ref[:, hs].astype(F32))).astype(BF16))
    ycat = jnp.concatenate(ya + yb, axis=1)
    y = _dot(ycat, wo_ref[...])
    xn, h2, eidx, gates = _post_mixer(_token_rows(xl_ref, xc_ref, ts, bs), y, mod_ref[0], g2_ref[...], rw_ref[...],
                                      rb_ref[...])
    xn_ref[...] = xn
    h2_ref[...] = h2.astype(BF16)
    ei_ref[...] = eidx
    ga_ref[...] = gates


def _mix0(xl, xc, sc, o_pairs, z, mods, cw, on, wo, g2, rw, rb, h2buf, ts, bs, s, cl, nb):
    d = xl.shape[1]
    r = bs + xc.shape[0]
    kern = functools.partial(_mix0_kernel, ts=ts, bs=bs, s=s, cl=cl, nb=nb)

    def o_spec(b):
        def index(i):
            row0 = i * ts
            in_lat = row0 < bs
            owner = jnp.where(in_lat, row0 // s, (row0 - bs) // cl)
            own = jnp.where(in_lat, (row0 % s) // ts, s // ts + ((row0 - bs) % cl) // ts)
            nxt = jnp.where(in_lat, 0, s // ts)
            prv = jnp.where(in_lat, s // ts - 1, (s + cl) // ts - 1)
            return jnp.where(owner == b, own, jnp.where(owner < b, nxt, prv)), 0
        return pl.BlockSpec((ts, 512), index)

    mi = functools.partial(_mod_index, tm=ts, bs=bs, s=s, nb=nb)
    row = lambda w: pl.BlockSpec((ts, w), lambda i: (i, 0))
    full = lambda a: pl.BlockSpec(a.shape, lambda i: (0,) * a.ndim)
    return pl.pallas_call(
        kern,
        grid=(r // ts,),
        in_specs=_token_specs(ts, bs, d) + [row(1536)] + _halo_specs(ts, 1536, r) + [
            row(512),
            pl.BlockSpec((1, N_MOD, d), lambda i: (mi(i), 0, 0)),
            full(cw), full(on), full(wo), full(g2), full(rw), full(rb), pl.BlockSpec(memory_space=pl.ANY)]
        + [o_spec(b) for b in range(nb) for _ in range(2)],
        out_specs=[row(d), row(d), pl.BlockSpec((8, ts), lambda i: (0, i)), row(128)],
        out_shape=[jax.ShapeDtypeStruct((r, d), F32), jax.ShapeDtypeStruct(h2buf.shape, BF16),
                   jax.ShapeDtypeStruct((8, r), I32), jax.ShapeDtypeStruct((r, 128), F32)],
        input_output_aliases={13: 1},
        compiler_params=_cp(("parallel",)),
        name="mix0",
    )(xl, xc, sc, sc, sc, z, mods, cw, on, wo, g2, rw, rb, h2buf, *[a for pair in o_pairs for a in pair])


def _gmm_kernel(te_ref, tf_ref, tv_ref, ne_ref, sl_ref, x_ref, wg_hbm, wu_hbm, wd_hbm, y_ref,
                wbuf, wgb, wub, wdb, sem, *, layer):
    t = pl.program_id(0)

    def weight_copies(e, slot):
        return [pltpu.make_async_copy(w.at[layer, e], wbuf.at[slot, k], sem.at[slot, k])
                for k, w in enumerate((wg_hbm, wu_hbm, wd_hbm))]

    @pl.when(t == 0)
    def _():
        for cp in weight_copies(te_ref[0], sl_ref[0]):
            cp.start()

    @pl.when(tf_ref[t] == 1)
    def _():
        slot = sl_ref[t]
        for cp in weight_copies(te_ref[t], slot):
            cp.wait()
        wgb[...] = wbuf[slot, 0].astype(BF16)
        wub[...] = wbuf[slot, 1].astype(BF16)
        wdb[...] = wbuf[slot, 2].astype(BF16)

        @pl.when(ne_ref[t] >= 0)
        def _():
            for cp in weight_copies(ne_ref[t], 1 - slot):
                cp.start()

    @pl.when(tv_ref[t] == 1)
    def _():
        x = x_ref[...]
        a = (_silu(_dot(x, wgb[...])) * _dot(x, wub[...])).astype(BF16)
        y_ref[...] = _dot(a, wdb[...]).astype(BF16)

    @pl.when(tv_ref[t] == 0)
    def _():
        y_ref[...] = jnp.zeros_like(y_ref)


def _gmm(xs, w_gate, w_up, w_down, layer, tile_expert, tile_first, tile_valid, next_expert, tile_slot, tmm):
    p, d = xs.shape
    de = w_gate.shape[-1]
    assert d == de
    nt = p // tmm
    row = lambda t, *_: (t, 0)
    grid_spec = pltpu.PrefetchScalarGridSpec(
        num_scalar_prefetch=5,
        grid=(nt,),
        in_specs=[pl.BlockSpec((tmm, d), row)] + [pl.BlockSpec(memory_space=pl.ANY)] * 3,
        out_specs=pl.BlockSpec((tmm, d), row),
        scratch_shapes=[pltpu.VMEM((2, 3, d, de), F32),
                        pltpu.VMEM((d, de), BF16), pltpu.VMEM((d, de), BF16), pltpu.VMEM((de, d), BF16),
                        pltpu.SemaphoreType.DMA((2, 3))],
    )
    return pl.pallas_call(
        functools.partial(_gmm_kernel, layer=layer),
        grid_spec=grid_spec,
        out_shape=jax.ShapeDtypeStruct((p, d), BF16),
        compiler_params=_cp(("arbitrary",)),
        name="gmm",
    )(tile_expert, tile_first, tile_valid, next_expert, tile_slot, xs, w_gate, w_up, w_down)


def _moe(h2, t_tok, e_first, e_second, w_gate, w_up, w_down, layer, tmm):
    n = 2 * t_tok
    e_flat = jnp.concatenate([e_first, e_second])
    onehot = (e_flat[:, None] == jnp.arange(N_EXPERTS, dtype=I32)[None, :]).astype(I32)
    csum = jnp.cumsum(onehot, axis=0)
    counts = csum[-1]
    ptiles = (counts + tmm - 1) // tmm
    tile_end = jnp.cumsum(ptiles)
    dest = jnp.sum(onehot * (csum - 1 + ((tile_end - ptiles) * tmm)[None, :]), axis=1)
    nt = n // tmm + N_EXPERTS
    tid = jnp.arange(nt, dtype=I32)
    tile_valid = (tid < tile_end[-1]).astype(I32)
    te = jnp.minimum(jnp.sum((tile_end[None, :] <= tid[:, None]).astype(I32), axis=1), N_EXPERTS - 1)
    last_used = jnp.max(jnp.where(tile_valid == 1, te, 0))
    te = jnp.where(tile_valid == 1, te, last_used)
    tile_first = jnp.concatenate([jnp.ones((1,), I32), (te[1:] != te[:-1]).astype(I32)])
    tile_slot = (jnp.cumsum(tile_first) - 1) % 2
    experts = jnp.arange(N_EXPERTS, dtype=I32)
    later = (experts[None, :] > te[:, None]) & (counts[None, :] > 0)
    next_expert = jnp.min(jnp.where(later, experts[None, :], N_EXPERTS), axis=1)
    next_expert = jnp.where(next_expert < N_EXPERTS, next_expert, -1)
    order = jnp.argsort(e_flat, stable=True).astype(I32)
    seg_start = (tile_end - ptiles) * tmm
    shift = seg_start - (jnp.cumsum(counts) - counts)
    pos = tid[:, None] * tmm + jnp.arange(tmm, dtype=I32)[None, :]
    te_onehot = te[:, None] == jnp.arange(N_EXPERTS, dtype=I32)[None, :]
    per_tile = lambda v: jnp.sum(jnp.where(te_onehot, v[None, :], 0), axis=1, keepdims=True)
    used = (pos - per_tile(seg_start) < per_tile(counts)) & (tile_valid[:, None] == 1)
    src = (jnp.where(used, jnp.take(order, jnp.clip(pos - per_tile(shift), 0, n - 1)), pos) % t_tok).reshape(-1)
    assert h2.shape[0] >= nt * tmm
    xs = jnp.take(h2, src, axis=0, mode="clip")
    ys = _gmm(xs, w_gate, w_up, w_down, layer, te, tile_first, tile_valid, next_expert, tile_slot, tmm)
    return jnp.take(ys, dest, axis=0, mode="clip")


def _moe_combine(x, y1, y2, gates, m5):
    g = gates
    f = g[:, 0:1] * y1.astype(F32) + g[:, 1:2] * y2.astype(F32)
    return x + m5 * f


def _rope(x, cos, sin):
    n = x.shape[1]
    lane = lax.broadcasted_iota(I32, x.shape, 1)
    sw = jnp.where(lane % 32 < 16, pltpu.roll(x, n - 16, 1), pltpu.roll(x, 16, 1))
    reps = n // 128
    if reps > 1:
        cos = jnp.concatenate([cos] * reps, axis=1)
        sin = jnp.concatenate([sin] * reps, axis=1)
    return x * cos + sw * sin


def _inproj1_kernel(x_ref, y1_ref, y2_ref, ga_ref, m0_ref, m1_ref, g_ref, w_ref, cos_ref, sin_ref,
                    x1_ref, cq_ref, ckt_ref, cv_ref, dq_ref, dk_ref, dv_ref, *, tm, bs):
    m = m1_ref[0]
    in_lat = pl.program_id(0) * tm < bs
    scale = HEAD_DIM ** -0.5 * LOG2E
    for rows in (slice(0, tm // 2), slice(tm // 2, tm)):
        x1 = _moe_combine(x_ref[rows], y1_ref[rows], y2_ref[rows], ga_ref[rows], m0_ref[0][5:6])
        x1_ref[rows] = x1
        h = (_rms(x1, g_ref[...]) * (1.0 + m[1:2]) + m[0:1]).astype(BF16)
        cos, sin = cos_ref[rows], sin_ref[rows]
        cq = _dot(h, w_ref[:, 0:512])
        cq_ref[rows] = (jnp.where(in_lat, _rope(cq, cos, sin), cq) * scale).astype(BF16)
        ck = _dot(h, w_ref[:, 512:640])
        ckt_ref[:, rows] = jnp.where(in_lat, _rope(ck, cos, sin), ck).T.astype(BF16)
        cv_ref[rows] = _dot(h, w_ref[:, 640:768]).astype(BF16)
        dq_ref[rows] = (_dot(h, w_ref[:, 768:1280]) * scale).astype(BF16)
        dk_ref[rows] = _dot(h, w_ref[:, 1280:1792]).astype(BF16)
        dv_ref[rows] = _dot(h, w_ref[:, 1792:2304]).astype(BF16)


def _inproj1(x, y, gates, mods0, mods1, g, w, cos, sin, tm, bs, s, nb):
    r, d = x.shape
    second = pl.BlockSpec((tm, d), lambda i: (i + r // tm, 0))
    kern = functools.partial(_inproj1_kernel, tm=tm, bs=bs)
    mi = functools.partial(_mod_index, tm=tm, bs=bs, s=s, nb=nb)
    row = lambda wd: pl.BlockSpec((tm, wd), lambda i: (i, 0))
    modspec = pl.BlockSpec((1, N_MOD, d), lambda i: (mi(i), 0, 0))
    tab = pl.BlockSpec((tm, 128), lambda i: (jnp.where(i * tm < bs, (i * tm % s) // tm, 0), 0))
    shp = lambda wd, dt: jax.ShapeDtypeStruct((r, wd), dt)
    return pl.pallas_call(
        kern,
        grid=(r // tm,),
        in_specs=[row(d), row(d), second, row(128), modspec, modspec,
                  pl.BlockSpec((1, d), lambda i: (0, 0)), pl.BlockSpec(w.shape, lambda i: (0, 0)), tab, tab],
        out_specs=[row(d), row(512), pl.BlockSpec((128, tm), lambda i: (0, i)), row(128), row(512), row(512),
                   row(512)],
        out_shape=[shp(d, F32), shp(512, BF16), jax.ShapeDtypeStruct((128, r), BF16), shp(128, BF16),
                   shp(512, BF16), shp(512, BF16), shp(512, BF16)],
        compiler_params=_cp(("parallel",)),
        name="inproj1",
    )(x, y, y, gates, mods0, mods1, g, w, cos, sin)


def _swa_kernel(q_ref, ktp_ref, kto_ref, ktn_ref, ktx_ref, vp_ref, vo_ref, vn_ref, vx_ref, sink_ref, o_ref,
                *, nblk, cl, nq):
    grp = pl.program_id(1)
    wb = SWA_BLOCK
    nloc = 3 * wb
    ktx, vx = ktx_ref[...], vx_ref[...]
    kt_all = jnp.concatenate([ktp_ref[...], kto_ref[...], ktn_ref[...]], axis=1)
    v_all = jnp.concatenate([vp_ref[...], vo_ref[...], vn_ref[...]], axis=0)
    kts = [jnp.concatenate([kt_all[:, wb * u:wb * u + nloc], ktx], axis=1) for u in range(nq)]
    vvs = [jnp.concatenate([v_all[wb * u:wb * u + nloc], vx], axis=0) for u in range(nq)]
    a_i = lax.broadcasted_iota(I32, (2 * wb, nloc), 0) % wb
    c_i = lax.broadcasted_iota(I32, (2 * wb, nloc), 1)
    band = (c_i >= a_i) & (c_i <= a_i + 2 * SWA_WINDOW)
    oks = []
    for u in range(nq):
        i = nq * grp + u
        lo = jnp.where(i > 0, 0, wb)
        hi = jnp.where(i < nblk - 1, 3 * wb, 2 * wb)
        oks.append(band & (c_i >= lo) & (c_i < hi))
    half = lax.broadcasted_iota(I32, (1, 128), 1) // HEAD_DIM
    zero = jnp.zeros((), BF16)
    sink = sink_ref[...]
    items = [(u, g) for u in range(nq) for g in range(4)]

    def scores(u, g):
        q2 = q_ref[wb * u:wb * u + wb, 128 * g:128 * g + 128]
        qst = jnp.concatenate([jnp.where(half == 0, q2, zero), jnp.where(half == 1, q2, zero)], axis=0)
        s_all = _dot(qst, kts[u])
        return jnp.concatenate([jnp.where(oks[u], s_all[:, 0:nloc], NEG), s_all[:, nloc:]], axis=1)

    def softmax(g, sc):
        sk = jnp.concatenate([jnp.broadcast_to(sink[0:1, 2 * g + a:2 * g + a + 1], (wb, 1)) for a in range(2)],
                             axis=0)
        m = jnp.maximum(jnp.max(sc, axis=-1, keepdims=True), sk)
        p = jnp.exp2(sc - m)
        return p.astype(BF16), jnp.sum(p, axis=-1, keepdims=True) + jnp.exp2(sk - m)

    def output(u, g, p, den):
        ost = _dot(p, vvs[u]) / den
        o_ref[wb * u:wb * u + wb, 128 * g:128 * g + 128] = jnp.where(half == 0, ost[0:wb], ost[wb:2 * wb]).astype(BF16)

    n = len(items)
    sc, pr = {}, {}
    for step in range(n + 2):
        if step < n:
            sc[step] = scores(*items[step])
        if 1 <= step < n + 1:
            pr[step - 1] = softmax(items[step - 1][1], sc.pop(step - 1))
        if step >= 2:
            output(*items[step - 2], *pr.pop(step - 2))


def _swa(cq, ckt, cv, sink_row, nb, s, cl, bs):
    wb = SWA_BLOCK
    nblk = s // wb
    nq = 8
    assert nblk % nq == 0
    npair = nblk // nq
    kern = functools.partial(_swa_kernel, nblk=nblk, cl=cl, nq=nq)
    prev = lambda b, j: b * nblk + jnp.maximum(nq * j - 1, 0)
    nxt = lambda b, j: b * nblk + jnp.minimum(nq * j + nq, nblk - 1)
    own = lambda b, j: b * npair + j
    return pl.pallas_call(
        kern,
        grid=(nb, npair),
        in_specs=[pl.BlockSpec((nq * wb, 512), lambda b, j: (own(b, j), 0)),
                  pl.BlockSpec((128, wb), lambda b, j: (0, prev(b, j))),
                  pl.BlockSpec((128, nq * wb), lambda b, j: (0, own(b, j))),
                  pl.BlockSpec((128, wb), lambda b, j: (0, nxt(b, j))),
                  pl.BlockSpec((128, cl), lambda b, j: (0, bs // cl + b)),
                  pl.BlockSpec((wb, 128), lambda b, j: (prev(b, j), 0)),
                  pl.BlockSpec((nq * wb, 128), lambda b, j: (own(b, j), 0)),
                  pl.BlockSpec((wb, 128), lambda b, j: (nxt(b, j), 0)),
                  pl.BlockSpec((cl, 128), lambda b, j: (bs // cl + b, 0)),
                  pl.BlockSpec((1, 128), lambda b, j: (0, 0))],
        out_specs=pl.BlockSpec((nq * wb, 512), lambda b, j: (own(b, j), 0)),
        out_shape=jax.ShapeDtypeStruct((bs, 512), BF16),
        compiler_params=_cp(("parallel", "parallel")),
        name="swa",
    )(cq, ckt, ckt, ckt, ckt, cv, cv, cv, cv, sink_row)


def _na_kernel(q_ref, k_ref, v_ref, kx_ref, vx_ref, cols_ref, o_ref, bias_ref, *, rows, unroll):
    for h in range(2):
        for off in range(NA_KH):
            for i in range(NA_KH):
                bias_ref[h, off, :, GRID_W * i:GRID_W * (i + 1)] = cols_ref[h, off + i]
    half = lax.broadcasted_iota(I32, (1, 128), 1) // HEAD_DIM
    zero = jnp.zeros((), BF16)
    vx = vx_ref[...]
    kxt = kx_ref[...].astype(F32).T.astype(BF16)
    span = NA_KH * GRID_W

    def scores(r):
        rs = jnp.clip(r - NA_KH // 2, 0, rows - NA_KH)
        off = rs - r + NA_KH - 1
        q0 = pl.multiple_of(r * GRID_W, GRID_W)
        k0 = pl.multiple_of(rs * GRID_W, GRID_W)
        q2 = q_ref[pl.ds(q0, GRID_W), :]
        qst = jnp.concatenate([jnp.where(half == 0, q2, zero), jnp.where(half == 1, q2, zero)], axis=0)
        s_loc = (_dot_nt(qst, k_ref[pl.ds(k0, span), :])
                 + jnp.concatenate([bias_ref[0, off], bias_ref[1, off]], axis=0))
        return q0, k0, s_loc, _dot(qst, kxt)

    def softmax(s_loc, s_ctx):
        m = jnp.maximum(jnp.max(s_loc, axis=-1, keepdims=True), jnp.max(s_ctx, axis=-1, keepdims=True))
        p_loc = jnp.exp2(s_loc - m)
        p_ctx = jnp.exp2(s_ctx - m)
        den = jnp.sum(p_loc, axis=-1, keepdims=True) + jnp.sum(p_ctx, axis=-1, keepdims=True)
        return p_loc.astype(BF16), p_ctx.astype(BF16), den

    def output(q0, k0, p_loc, p_ctx, den):
        ost = (_dot(p_loc, v_ref[pl.ds(k0, span), :]) + _dot(p_ctx, vx)) / den
        o = jnp.where(half == 0, ost[0:GRID_W], ost[GRID_W:2 * GRID_W])
        o_ref[pl.ds(q0, GRID_W), :] = o.astype(BF16)

    def body(i, carry):
        sc, pr = {}, {}
        for step in range(unroll + 2):
            if step < unroll:
                sc[step] = scores(i * unroll + step)
            if 1 <= step < unroll + 1:
                q0, k0, s_loc, s_ctx = sc.pop(step - 1)
                pr[step - 1] = (q0, k0) + softmax(s_loc, s_ctx)
            if step >= 2:
                output(*pr.pop(step - 2))
        return carry

    lax.fori_loop(0, rows // unroll, body, 0)


def _na(dq, dk, dv, bias_cols, nb, s, cl, bs):
    rows = s // GRID_W
    unroll = 16
    assert rows % unroll == 0
    kern = functools.partial(_na_kernel, rows=rows, unroll=unroll)
    seq = pl.BlockSpec((s, 128), lambda b, p: (b, p))
    ctx = pl.BlockSpec((cl, 128), lambda b, p: (bs // cl + b, p))
    return pl.pallas_call(
        kern,
        grid=(nb, NA_HEADS // 2),
        in_specs=[seq, seq, seq, ctx, ctx,
                  pl.BlockSpec((2, 2 * NA_KH - 1, GRID_W, GRID_W), lambda b, p: (p, 0, 0, 0))],
        out_specs=seq,
        out_shape=jax.ShapeDtypeStruct((bs, 512), BF16),
        scratch_shapes=[pltpu.VMEM((2, NA_KH, GRID_W, NA_KH * GRID_W), F32)],
        compiler_params=_cp(("parallel", "parallel")),
        name="na",
    )(dq, dk, dv, dk, dv, bias_cols)


def _na_bias_cols(rpb):
    c = np.arange(GRID_W)
    qs = np.clip(c - NA_KW // 2, 0, GRID_W - NA_KW)
    kc = np.arange(GRID_W)
    ok = (kc[None, :] >= qs[:, None]) & (kc[None, :] < qs[:, None] + NA_KW)
    dc = np.clip(kc[None, :] - c[:, None] + NA_KW - 1, 0, 2 * NA_KW - 2)
    sel = (np.arange(2 * NA_KW - 1)[:, None, None] == dc[None]).astype(np.float32)
    cols = jnp.einsum("hab,bck->hack", rpb.astype(F32), sel, precision=HI)
    return jnp.where(ok[None, None], cols * LOG2E, NEG)


def _mix1_kernel(x_ref, oc_ref, od_ref, mod_ref, wo_ref, g2_ref, rw_ref, rb_ref, h2buf_ref,
                 xn_ref, h2_ref, ei_ref, ga_ref):
    del h2buf_ref
    y = _dot(oc_ref[...], wo_ref[0:512, :]) + _dot(od_ref[...], wo_ref[512:1024, :])
    xn, h2, eidx, gates = _post_mixer(x_ref[...], y, mod_ref[0], g2_ref[...], rw_ref[...], rb_ref[...])
    xn_ref[...] = xn
    h2_ref[...] = h2.astype(BF16)
    ei_ref[...] = eidx
    ga_ref[...] = gates


def _mix1(x, oc, od, mods, wo, g2, rw, rb, h2buf, tm, bs, s):
    d = x.shape[1]
    row = lambda w: pl.BlockSpec((tm, w), lambda i: (i, 0))
    full = lambda a: pl.BlockSpec(a.shape, lambda i: (0,) * a.ndim)
    return pl.pallas_call(
        _mix1_kernel,
        grid=(bs // tm,),
        in_specs=[row(d), row(512), row(512), pl.BlockSpec((1, N_MOD, d), lambda i: (i * tm // s, 0, 0)),
                  full(wo), full(g2), full(rw), full(rb), pl.BlockSpec(memory_space=pl.ANY)],
        out_specs=[row(d), row(d), pl.BlockSpec((8, tm), lambda i: (0, i)), row(128)],
        out_shape=[jax.ShapeDtypeStruct((bs, d), F32), jax.ShapeDtypeStruct(h2buf.shape, BF16),
                   jax.ShapeDtypeStruct((8, bs), I32), jax.ShapeDtypeStruct((bs, 128), F32)],
        input_output_aliases={8: 1},
        compiler_params=_cp(("parallel",)),
        name="mix1",
    )(x, oc, od, mods, wo, g2, rw, rb, h2buf)


def _final_kernel(x_ref, y1_ref, y2_ref, ga_ref, mod_ref, g_ref, o_ref):
    x = _moe_combine(x_ref[...], y1_ref[...], y2_ref[...], ga_ref[...], mod_ref[0][5:6])
    o_ref[...] = _rms(x, g_ref[...])


def _final(x, y, gates, mods, g, tm, s):
    r, d = x.shape
    second = pl.BlockSpec((tm, d), lambda i: (i + r // tm, 0))
    row = lambda w: pl.BlockSpec((tm, w), lambda i: (i, 0))
    return pl.pallas_call(
        _final_kernel,
        grid=(r // tm,),
        in_specs=[row(d), row(d), second, row(128), pl.BlockSpec((1, N_MOD, d), lambda i: (i * tm // s, 0, 0)),
                  pl.BlockSpec((1, d), lambda i: (0, 0))],
        out_specs=row(d),
        out_shape=jax.ShapeDtypeStruct((r, d), F32),
        compiler_params=_cp(("parallel",)),
        name="final",
    )(x, y, y, gates, mods, g)


def _rope_tables(s):
    nf = HEAD_DIM // 4
    t = np.arange(s)
    inv = ROPE_THETA ** (-np.arange(nf, dtype=np.float64) / nf)
    ar = (t // GRID_W)[:, None] * inv
    ac = (t % GRID_W)[:, None] * inv
    cos = np.concatenate([np.cos(ar), np.cos(ar), np.cos(ac), np.cos(ac)], axis=1)
    sin = np.concatenate([-np.sin(ar), np.sin(ar), -np.sin(ac), np.sin(ac)], axis=1)
    return (jnp.asarray(np.concatenate([cos, cos], axis=1), F32),
            jnp.asarray(np.concatenate([sin, sin], axis=1), F32))


def kernel(x, c, ctx, c_ctx, ada_w, ada_b, norm1_g, norm2_g, ev_w_in, ev_w_out, sc_conv_w, dn_conv_w, dn_a_log, dn_dt_bias, dn_onorm_g, od_w_in, od_w_out, swa_sink, na_rpb, router_w, router_b, moe_w_gate, moe_w_up, moe_w_down, final_g):
    nb, s, d = x.shape
    cl = ctx.shape[1]
    bs = nb * s
    tm = 512
    ts = 256
    tmm = 512
    assert d == 1024 and s % tm == 0 and (nb * cl) % tm == 0 and cl % ts == 0 and s % ts == 0
    assert s // GRID_W >= NA_KH and bs % cl == 0 and nb + 1 <= 8

    xl, xc = x.reshape(bs, d), ctx.reshape(nb * cl, d)
    cc = jnp.zeros((8, d), F32).at[:nb].set(c).at[nb].set(c_ctx)
    mods = _ada(cc, ada_w, ada_b).reshape(ada_w.shape[0], 8, N_MOD, d)
    rw32 = jnp.pad(router_w, ((0, 0), (0, 128 - N_EXPERTS)))
    rw_hi = rw32.astype(BF16)
    rw = jnp.concatenate([rw_hi, (rw32 - rw_hi.astype(F32)).astype(BF16)], axis=1)
    rb = router_b.reshape(N_EXPERTS, 1)
    row = lambda v: v.reshape(1, -1)

    w_in0 = jnp.pad(ev_w_in[0], ((0, 0), (0, 3712 - ev_w_in.shape[-1]))).astype(BF16)
    sc, qkv, z, bg = _inproj0(xl, xc, mods[0], row(norm1_g[0]), w_in0, tm, bs, s, nb)
    pad16 = lambda v: jnp.pad(v.reshape(-1), (8, 128 - 16)).reshape(1, 128)
    uf, ub, wf, wb, qf, qb, kf, kb, af, ab, gc = _dnchunk(qkv, bg, dn_conv_w[0], pad16(dn_a_log[0]),
                                                          pad16(dn_dt_bias[0]), ts, bs, s, cl)
    o_pairs = _dnscan(uf, ub, wf, wb, qf, qb, kf, kb, af, ab, gc, nb, s, cl, bs)
    r_all = bs + nb * cl
    moe_rows = lambda t: (2 * t // tmm + N_EXPERTS) * tmm
    x0, h2, ei, ga = _mix0(xl, xc, sc, o_pairs, z, mods[0], sc_conv_w[0], row(dn_onorm_g[0]),
                           ev_w_out[0].astype(BF16), row(norm2_g[0]), rw, rb,
                           jnp.zeros((moe_rows(r_all), d), BF16), ts, bs, s, cl, nb)
    y = _moe(h2, r_all, ei[0], ei[1], moe_w_gate, moe_w_up, moe_w_down, 0, tmm)

    perm = np.concatenate([np.arange(HEAD_DIM) + HEAD_DIM * (g + 4 * a) for g in range(4) for a in range(2)])
    w1 = od_w_in[0]
    w_in1 = jnp.concatenate([w1[:, 0:512][:, perm], w1[:, 512:]], axis=1).astype(BF16)
    wo1 = od_w_out[0]
    w_out1 = jnp.concatenate([wo1[0:512][perm], wo1[512:]], axis=0).astype(BF16)
    sink_row = jnp.pad(swa_sink[0][np.array([g + 4 * a for g in range(4) for a in range(2)])] * LOG2E,
                       (0, 128 - SWA_HEADS)).reshape(1, 128)
    cos, sin = _rope_tables(s)
    x1, cq, ckt, cv, dq, dk, dv = _inproj1(x0, y, ga, mods[0], mods[1], row(norm1_g[1]), w_in1, cos, sin,
                                           tm, bs, s, nb)
    oc = _swa(cq, ckt, cv, sink_row, nb, s, cl, bs)
    od = _na(dq, dk, dv, _na_bias_cols(na_rpb[0]), nb, s, cl, bs)
    assert moe_rows(bs) <= h2.shape[0]
    x2, h2, ei, ga = _mix1(x1, oc, od, mods[1], w_out1, row(norm2_g[1]), rw, rb, h2, tm, bs, s)
    y = _moe(h2, bs, ei[0], ei[1], moe_w_gate, moe_w_up, moe_w_down, 1, tmm)
    out = _final(x2, y, ga, mods[1], row(final_g), tm, s)
    return out.reshape(nb, s, d)
```
